```python
import jax
import jax.numpy as jnp
from jax import lax
import numpy as np

D_MODEL = 1024
BATCH = 16
SEQ = 2048
DEPTH = 4

N_META = 16
HEAD_DIM = 64
N_Q_HEADS = 8
N_KV_HEADS = 2
GROUP = N_Q_HEADS // N_KV_HEADS
ATTN_WIDTH = N_Q_HEADS * HEAD_DIM
KV_WIDTH = N_KV_HEADS * HEAD_DIM
LRU_WIDTH = D_MODEL - ATTN_WIDTH
LRU_BLOCKS = 8
LRU_BLOCK = LRU_WIDTH // LRU_BLOCKS
CONV_WIDTH = 4
LRU_C = 8.0
WINDOW = 128
BLOCK = 128
ROPE_THETA = 10000.0
D_FF = 4 * D_MODEL
EPS = 1e-6
IN_COLS = ATTN_WIDTH + 2 * KV_WIDTH + 2 * LRU_WIDTH

kernel_name = "hymba_rglru_swa_sink_hybrid"


def rmsnorm(x, g):
    xf = x.astype(jnp.float32)
    y = xf * lax.rsqrt(jnp.mean(xf * xf, axis=-1, keepdims=True) + EPS)
    return (y * g.astype(jnp.float32)).astype(x.dtype)


def rope_tables(t):
    inv_freq = ROPE_THETA ** (-jnp.arange(0, HEAD_DIM, 2, dtype=jnp.float32) / HEAD_DIM)
    ang = jnp.arange(t, dtype=jnp.float32)[:, None] * inv_freq[None, :]
    return jnp.cos(ang)[:, None, :], jnp.sin(ang)[:, None, :]


def apply_rope(x, cos, sin):
    xf = x.astype(jnp.float32)
    x1, x2 = jnp.split(xf, 2, axis=-1)
    return jnp.concatenate([x1 * cos - x2 * sin, x2 * cos + x1 * sin], axis=-1).astype(x.dtype)


def two_block_band(t, nb):
    b = t.shape[0]
    tb = t.reshape(b, nb, BLOCK, t.shape[2], t.shape[3])
    prev = jnp.pad(tb, ((0, 0), (1, 0), (0, 0), (0, 0), (0, 0)))[:, :-1]
    return jnp.concatenate([prev, tb], axis=2)


def sliding_window_attention(q, k, v, sinks):
    b, t = q.shape[0], q.shape[1]
    pad_front = (-N_META) % BLOCK
    pad_back = (-(t + pad_front)) % BLOCK
    tp = t + pad_front + pad_back
    nb = tp // BLOCK
    padw = ((0, 0), (pad_front, pad_back), (0, 0), (0, 0))
    qb = jnp.pad(q, padw).reshape(b, nb, BLOCK, N_KV_HEADS, GROUP, HEAD_DIM)
    k2 = two_block_band(jnp.pad(k, padw), nb)
    v2 = two_block_band(jnp.pad(v, padw), nb)
    scale = HEAD_DIM ** -0.5
    scores = jnp.einsum('bnqhgd,bnkhd->bnhgqk', qb, k2).astype(jnp.float32) * scale
    blk = jnp.arange(nb)[:, None, None]
    q_idx = blk * BLOCK + jnp.arange(BLOCK)[None, :, None]
    k_idx = (blk - 1) * BLOCK + jnp.arange(2 * BLOCK)[None, None, :]
    mask = (k_idx <= q_idx) & (q_idx - k_idx < WINDOW) & (k_idx >= pad_front)
    scores = jnp.where(mask[None, :, None, None], scores, -jnp.inf)
    sink = sinks.astype(jnp.float32).reshape(1, 1, N_KV_HEADS, GROUP, 1, 1)
    m = jnp.maximum(jnp.max(scores, axis=-1, keepdims=True), sink)
    p = jnp.exp(scores - m)
    denom = jnp.sum(p, axis=-1, keepdims=True) + jnp.exp(sink - m)
    probs = (p / denom).astype(v.dtype)
    out = jnp.einsum('bnhgqk,bnkhd->bnqhgd', probs, v2)
    return out.reshape(b, tp, ATTN_WIDTH)[:, pad_front:pad_front + t]


def rg_lru_branch(xb, gb, conv_w, conv_b, gate_a_w, gate_a_b, gate_x_w, gate_x_b, lam):
    b, t = xb.shape[0], xb.shape[1]
    xc = lax.conv_general_dilated(
        xb, conv_w[:, None, :].astype(xb.dtype), window_strides=(1,),
        padding=[(CONV_WIDTH - 1, 0)], dimension_numbers=('NWC', 'WIO', 'NWC'),
        feature_group_count=LRU_WIDTH) + conv_b
    xblk = xc.reshape(b, t, LRU_BLOCKS, LRU_BLOCK)
    r = jax.nn.sigmoid(jnp.einsum('btnc,ncd->btnd', xblk, gate_a_w).reshape(b, t, LRU_WIDTH) + gate_a_b)
    i = jax.nn.sigmoid(jnp.einsum('btnc,ncd->btnd', xblk, gate_x_w).reshape(b, t, LRU_WIDTH) + gate_x_b)
    log_a = -LRU_C * r.astype(jnp.float32) * jax.nn.softplus(-lam.astype(jnp.float32))
    a = jnp.exp(log_a)
    u = jnp.sqrt(-jnp.expm1(2.0 * log_a)) * (i * xc).astype(jnp.float32)

    def step(h, inp):
        a_t, u_t = inp
        h = a_t * h + u_t
        return h, h

    h0 = jnp.zeros((b, LRU_WIDTH), jnp.float32)
    _, hs = lax.scan(step, h0, (jnp.moveaxis(a, 1, 0), jnp.moveaxis(u, 1, 0)))
    h = jnp.moveaxis(hs, 0, 1).astype(xb.dtype)
    return h * jax.nn.gelu(gb)


def _fwd_setup_inputs(seed: int = 0) -> dict:
    key = jax.random.key(seed)
    ks = jax.random.split(key, 20)
    f32 = jnp.float32

    def nrm(k, shape, scale):
        return jax.random.normal(k, shape, f32) * scale

    def gain(k, shape):
        return 1.0 + 0.05 * jax.random.normal(k, shape, f32)

    u = jax.random.uniform(ks[11], (DEPTH, LRU_WIDTH), f32, 0.9, 0.999)
    a = u ** (1.0 / LRU_C)
    lam = jnp.log(a) - jnp.log1p(-a)
    return {
        "x": nrm(ks[0], (BATCH, SEQ, D_MODEL), 1.0),
        "meta_tokens": nrm(ks[1], (N_META, D_MODEL), 1.0),
        "pre_mix_norm": gain(ks[2], (DEPTH, D_MODEL)),
        "w_in": nrm(ks[3], (DEPTH, D_MODEL, IN_COLS), D_MODEL ** -0.5),
        "attn_sinks": nrm(ks[4], (DEPTH, N_Q_HEADS), 0.5),
        "conv_w": nrm(ks[5], (DEPTH, CONV_WIDTH, LRU_WIDTH), CONV_WIDTH ** -0.5),
        "conv_b": nrm(ks[6], (DEPTH, LRU_WIDTH), 0.01),
        "gate_a_w": nrm(ks[7], (DEPTH, LRU_BLOCKS, LRU_BLOCK, LRU_BLOCK), LRU_BLOCK ** -0.5),
        "gate_a_b": nrm(ks[8], (DEPTH, LRU_WIDTH), 0.01),
        "gate_x_w": nrm(ks[9], (DEPTH, LRU_BLOCKS, LRU_BLOCK, LRU_BLOCK), LRU_BLOCK ** -0.5),
        "gate_x_b": nrm(ks[10], (DEPTH, LRU_WIDTH), 0.01),
        "lru_lambda": lam,
        "attn_group_norm": gain(ks[12], (DEPTH, ATTN_WIDTH)),
        "lru_group_norm": gain(ks[13], (DEPTH, LRU_WIDTH)),
        "w_out": nrm(ks[14], (DEPTH, D_MODEL, D_MODEL), D_MODEL ** -0.5),
        "post_mix_norm": gain(ks[15], (DEPTH, D_MODEL)),
        "pre_mlp_norm": gain(ks[16], (DEPTH, D_MODEL)),
        "w_up": nrm(ks[17], (DEPTH, D_MODEL, D_FF), D_MODEL ** -0.5),
        "w_down": nrm(ks[18], (DEPTH, D_FF, D_MODEL), D_FF ** -0.5),
        "post_mlp_norm": gain(ks[19], (DEPTH, D_MODEL)),
    }


def _fwd_reference(x, meta_tokens, pre_mix_norm, w_in, attn_sinks, conv_w, conv_b, gate_a_w, gate_a_b,
              gate_x_w, gate_x_b, lru_lambda, attn_group_norm, lru_group_norm, w_out, post_mix_norm,
              pre_mlp_norm, w_up, w_down, post_mlp_norm):
    b = x.shape[0]
    meta = jnp.broadcast_to(meta_tokens.astype(x.dtype)[None], (b, N_META, D_MODEL))
    h = jnp.concatenate([meta, x], axis=1)
    t = h.shape[1]
    cos, sin = rope_tables(t)
    splits = [ATTN_WIDTH, ATTN_WIDTH + KV_WIDTH, ATTN_WIDTH + 2 * KV_WIDTH,
              ATTN_WIDTH + 2 * KV_WIDTH + LRU_WIDTH]
    for l in range(DEPTH):
        z = rmsnorm(h, pre_mix_norm[l])
        proj = z @ w_in[l]
        q, k, v, xb, gb = jnp.split(proj, splits, axis=-1)
        q = apply_rope(q.reshape(b, t, N_Q_HEADS, HEAD_DIM), cos, sin)
        k = apply_rope(k.reshape(b, t, N_KV_HEADS, HEAD_DIM), cos, sin)
        v = v.reshape(b, t, N_KV_HEADS, HEAD_DIM)
        attn = sliding_window_attention(q, k, v, attn_sinks[l])
        lru = rg_lru_branch(xb, gb, conv_w[l], conv_b[l], gate_a_w[l], gate_a_b[l],
                            gate_x_w[l], gate_x_b[l], lru_lambda[l])
        groups = jnp.concatenate([rmsnorm(attn, attn_group_norm[l]),
                                  rmsnorm(lru, lru_group_norm[l])], axis=-1)
        h = h + rmsnorm(groups @ w_out[l], post_mix_norm[l])
        z = rmsnorm(h, pre_mlp_norm[l])
        y = jnp.square(jax.nn.relu(z @ w_up[l])) @ w_down[l]
        h = h + rmsnorm(y, post_mlp_norm[l])
    return h[:, N_META:]


import jax as _jax
import jax.numpy as _jnp

TWIN_FORMAT = 'train_step'
FWD_PARAMS = ['x', 'meta_tokens', 'pre_mix_norm', 'w_in', 'attn_sinks', 'conv_w', 'conv_b', 'gate_a_w', 'gate_a_b', 'gate_x_w', 'gate_x_b', 'lru_lambda', 'attn_group_norm', 'lru_group_norm', 'w_out', 'post_mix_norm', 'pre_mlp_norm', 'w_up', 'w_down', 'post_mlp_norm']
TWIN_WEIGHTS = ['meta_tokens', 'pre_mix_norm', 'w_in', 'attn_sinks', 'conv_w', 'conv_b', 'gate_a_w', 'gate_a_b', 'gate_x_w', 'gate_x_b', 'lru_lambda', 'attn_group_norm', 'lru_group_norm', 'w_out', 'post_mix_norm', 'pre_mlp_norm', 'w_up', 'w_down', 'post_mlp_norm']
TWIN_DIFF_INPUT = 'x'
TWIN_INPUTS = ['x', 'meta_tokens', 'pre_mix_norm', 'w_in', 'attn_sinks', 'conv_w', 'conv_b', 'gate_a_w', 'gate_a_b', 'gate_x_w', 'gate_x_b', 'lru_lambda', 'attn_group_norm', 'lru_group_norm', 'w_out', 'post_mix_norm', 'pre_mlp_norm', 'w_up', 'w_down', 'post_mlp_norm', 'loss_target', 'm_meta_tokens', 'm_pre_mix_norm', 'm_w_in', 'm_attn_sinks', 'm_conv_w', 'm_conv_b', 'm_gate_a_w', 'm_gate_a_b', 'm_gate_x_w', 'm_gate_x_b', 'm_lru_lambda', 'm_attn_group_norm', 'm_lru_group_norm', 'm_w_out', 'm_post_mix_norm', 'm_pre_mlp_norm', 'm_w_up', 'm_w_down', 'm_post_mlp_norm', 'v_meta_tokens', 'v_pre_mix_norm', 'v_w_in', 'v_attn_sinks', 'v_conv_w', 'v_conv_b', 'v_gate_a_w', 'v_gate_a_b', 'v_gate_x_w', 'v_gate_x_b', 'v_lru_lambda', 'v_attn_group_norm', 'v_lru_group_norm', 'v_w_out', 'v_post_mix_norm', 'v_pre_mlp_norm', 'v_w_up', 'v_w_down', 'v_post_mlp_norm']
TWIN_OUTPUTS = ['loss', 'grad_x', 'grad_meta_tokens', 'grad_pre_mix_norm', 'grad_w_in', 'grad_attn_sinks', 'grad_conv_w', 'grad_conv_b', 'grad_gate_a_w', 'grad_gate_a_b', 'grad_gate_x_w', 'grad_gate_x_b', 'grad_lru_lambda', 'grad_attn_group_norm', 'grad_lru_group_norm', 'grad_w_out', 'grad_post_mix_norm', 'grad_pre_mlp_norm', 'grad_w_up', 'grad_w_down', 'grad_post_mlp_norm', 'delta_meta_tokens', 'delta_pre_mix_norm', 'delta_w_in', 'delta_attn_sinks', 'delta_conv_w', 'delta_conv_b', 'delta_gate_a_w', 'delta_gate_a_b', 'delta_gate_x_w', 'delta_gate_x_b', 'delta_lru_lambda', 'delta_attn_group_norm', 'delta_lru_group_norm', 'delta_w_out', 'delta_post_mix_norm', 'delta_pre_mlp_norm', 'delta_w_up', 'delta_w_down', 'delta_post_mlp_norm', 'new_m_meta_tokens', 'new_m_pre_mix_norm', 'new_m_w_in', 'new_m_attn_sinks', 'new_m_conv_w', 'new_m_conv_b', 'new_m_gate_a_w', 'new_m_gate_a_b', 'new_m_gate_x_w', 'new_m_gate_x_b', 'new_m_lru_lambda', 'new_m_attn_group_norm', 'new_m_lru_group_norm', 'new_m_w_out', 'new_m_post_mix_norm', 'new_m_pre_mlp_norm', 'new_m_w_up', 'new_m_w_down', 'new_m_post_mlp_norm', 'new_v_meta_tokens', 'new_v_pre_mix_norm', 'new_v_w_in', 'new_v_attn_sinks', 'new_v_conv_w', 'new_v_conv_b', 'new_v_gate_a_w', 'new_v_gate_a_b', 'new_v_gate_x_w', 'new_v_gate_x_b', 'new_v_lru_lambda', 'new_v_attn_group_norm', 'new_v_lru_group_norm', 'new_v_w_out', 'new_v_post_mix_norm', 'new_v_pre_mlp_norm', 'new_v_w_up', 'new_v_w_down', 'new_v_post_mlp_norm']
TWIN_LEAF_KINDS = {'loss': 'loss', 'grad_x': 'grad_x', 'grad_meta_tokens': 'grad_w', 'grad_pre_mix_norm': 'grad_w', 'grad_w_in': 'grad_w', 'grad_attn_sinks': 'grad_w', 'grad_conv_w': 'grad_w', 'grad_conv_b': 'grad_w', 'grad_gate_a_w': 'grad_w', 'grad_gate_a_b': 'grad_w', 'grad_gate_x_w': 'grad_w', 'grad_gate_x_b': 'grad_w', 'grad_lru_lambda': 'grad_w', 'grad_attn_group_norm': 'grad_w', 'grad_lru_group_norm': 'grad_w', 'grad_w_out': 'grad_w', 'grad_post_mix_norm': 'grad_w', 'grad_pre_mlp_norm': 'grad_w', 'grad_w_up': 'grad_w', 'grad_w_down': 'grad_w', 'grad_post_mlp_norm': 'grad_w', 'delta_meta_tokens': 'delta_w', 'delta_pre_mix_norm': 'delta_w', 'delta_w_in': 'delta_w', 'delta_attn_sinks': 'delta_w', 'delta_conv_w': 'delta_w', 'delta_conv_b': 'delta_w', 'delta_gate_a_w': 'delta_w', 'delta_gate_a_b': 'delta_w', 'delta_gate_x_w': 'delta_w', 'delta_gate_x_b': 'delta_w', 'delta_lru_lambda': 'delta_w', 'delta_attn_group_norm': 'delta_w', 'delta_lru_group_norm': 'delta_w', 'delta_w_out': 'delta_w', 'delta_post_mix_norm': 'delta_w', 'delta_pre_mlp_norm': 'delta_w', 'delta_w_up': 'delta_w', 'delta_w_down': 'delta_w', 'delta_post_mlp_norm': 'delta_w', 'new_m_meta_tokens': 'new_m', 'new_m_pre_mix_norm': 'new_m', 'new_m_w_in': 'new_m', 'new_m_attn_sinks': 'new_m', 'new_m_conv_w': 'new_m', 'new_m_conv_b': 'new_m', 'new_m_gate_a_w': 'new_m', 'new_m_gate_a_b': 'new_m', 'new_m_gate_x_w': 'new_m', 'new_m_gate_x_b': 'new_m', 'new_m_lru_lambda': 'new_m', 'new_m_attn_group_norm': 'new_m', 'new_m_lru_group_norm': 'new_m', 'new_m_w_out': 'new_m', 'new_m_post_mix_norm': 'new_m', 'new_m_pre_mlp_norm': 'new_m', 'new_m_w_up': 'new_m', 'new_m_w_down': 'new_m', 'new_m_post_mlp_norm': 'new_m', 'new_v_meta_tokens': 'new_v', 'new_v_pre_mix_norm': 'new_v', 'new_v_w_in': 'new_v', 'new_v_attn_sinks': 'new_v', 'new_v_conv_w': 'new_v', 'new_v_conv_b': 'new_v', 'new_v_gate_a_w': 'new_v', 'new_v_gate_a_b': 'new_v', 'new_v_gate_x_w': 'new_v', 'new_v_gate_x_b': 'new_v', 'new_v_lru_lambda': 'new_v', 'new_v_attn_group_norm': 'new_v', 'new_v_lru_group_norm': 'new_v', 'new_v_w_out': 'new_v', 'new_v_post_mix_norm': 'new_v', 'new_v_pre_mlp_norm': 'new_v', 'new_v_w_up': 'new_v', 'new_v_w_down': 'new_v', 'new_v_post_mlp_norm': 'new_v'}


def _forward(args):
    return _fwd_reference(*[args[k] for k in FWD_PARAMS])


def _output_shape():
    out = _jax.eval_shape(lambda: _forward(_fwd_setup_inputs(0)))
    return out.shape, out.dtype

N_MICROBATCH = 1
ADAM_LR = 0.001
ADAM_B1 = 0.9
ADAM_B2 = 0.999
ADAM_EPS = 1e-08
ADAM_WD = 0.01
ADAM_STEP = 10
PER_EXAMPLE_BATCH_AXIS = {'x': 0, 'loss_target': 0}
SHARED_INPUTS = []
_WEIGHT_DTYPES = {'meta_tokens': _jnp.float32, 'pre_mix_norm': _jnp.float32, 'w_in': _jnp.float32, 'attn_sinks': _jnp.float32, 'conv_w': _jnp.float32, 'conv_b': _jnp.float32, 'gate_a_w': _jnp.float32, 'gate_a_b': _jnp.float32, 'gate_x_w': _jnp.float32, 'gate_x_b': _jnp.float32, 'lru_lambda': _jnp.float32, 'attn_group_norm': _jnp.float32, 'lru_group_norm': _jnp.float32, 'w_out': _jnp.float32, 'post_mix_norm': _jnp.float32, 'pre_mlp_norm': _jnp.float32, 'w_up': _jnp.float32, 'w_down': _jnp.float32, 'post_mlp_norm': _jnp.float32}
MOMENT_SCALE = {'meta_tokens': 1.251238e+00, 'pre_mix_norm': 2.332363e+01, 'w_in': 1.654417e+01, 'attn_sinks': 1.650910e+00, 'conv_w': 1.756440e+01, 'conv_b': 1.581345e+02, 'gate_a_w': 5.766768e+00, 'gate_a_b': 3.638379e+00, 'gate_x_w': 1.115766e+01, 'gate_x_b': 6.350857e+00, 'lru_lambda': 6.937431e+00, 'attn_group_norm': 2.545319e+01, 'lru_group_norm': 2.258368e+01, 'w_out': 2.209693e+01, 'post_mix_norm': 3.764134e+01, 'pre_mlp_norm': 1.063385e+01, 'w_up': 5.285315e+00, 'w_down': 2.981835e+01, 'post_mlp_norm': 4.421345e+01}


def _to_microbatches(a, axis):
    t = _jnp.moveaxis(a, axis, 0)
    t = t.reshape((N_MICROBATCH, t.shape[0] // N_MICROBATCH) + t.shape[1:])
    return _jnp.moveaxis(t, 1, axis + 1)


def setup_inputs(seed: int = 0) -> dict:
    inp = _fwd_setup_inputs(seed)
    key = _jax.random.fold_in(_jax.random.key(seed), 7919)
    shape, _ = _output_shape()
    out = dict(inp)
    out["loss_target"] = _jax.random.normal(_jax.random.fold_in(key, 0), shape, _jnp.float32)
    for i, name in enumerate(TWIN_WEIGHTS):
        w = inp[name].astype(_jnp.float32)
        if MOMENT_SCALE is None:
            s = _jnp.sqrt(_jnp.mean(_jnp.square(w)) + 1e-30)
        else:
            s = MOMENT_SCALE[name]
        km, kv = _jax.random.split(_jax.random.fold_in(key, i + 1))
        out[name] = w
        out["m_" + name] = s * _jax.random.normal(km, w.shape, _jnp.float32)
        out["v_" + name] = (s * s) * _jax.random.uniform(kv, w.shape, _jnp.float32, 0.5, 1.5)
    if N_MICROBATCH > 1:
        for name, axis in PER_EXAMPLE_BATCH_AXIS.items():
            out[name] = _to_microbatches(out[name], axis)
    return {'x': out['x'], 'meta_tokens': out['meta_tokens'], 'pre_mix_norm': out['pre_mix_norm'], 'w_in': out['w_in'], 'attn_sinks': out['attn_sinks'], 'conv_w': out['conv_w'], 'conv_b': out['conv_b'], 'gate_a_w': out['gate_a_w'], 'gate_a_b': out['gate_a_b'], 'gate_x_w': out['gate_x_w'], 'gate_x_b': out['gate_x_b'], 'lru_lambda': out['lru_lambda'], 'attn_group_norm': out['attn_group_norm'], 'lru_group_norm': out['lru_group_norm'], 'w_out': out['w_out'], 'post_mix_norm': out['post_mix_norm'], 'pre_mlp_norm': out['pre_mlp_norm'], 'w_up': out['w_up'], 'w_down': out['w_down'], 'post_mlp_norm': out['post_mlp_norm'], 'loss_target': out['loss_target'], 'm_meta_tokens': out['m_meta_tokens'], 'm_pre_mix_norm': out['m_pre_mix_norm'], 'm_w_in': out['m_w_in'], 'm_attn_sinks': out['m_attn_sinks'], 'm_conv_w': out['m_conv_w'], 'm_conv_b': out['m_conv_b'], 'm_gate_a_w': out['m_gate_a_w'], 'm_gate_a_b': out['m_gate_a_b'], 'm_gate_x_w': out['m_gate_x_w'], 'm_gate_x_b': out['m_gate_x_b'], 'm_lru_lambda': out['m_lru_lambda'], 'm_attn_group_norm': out['m_attn_group_norm'], 'm_lru_group_norm': out['m_lru_group_norm'], 'm_w_out': out['m_w_out'], 'm_post_mix_norm': out['m_post_mix_norm'], 'm_pre_mlp_norm': out['m_pre_mlp_norm'], 'm_w_up': out['m_w_up'], 'm_w_down': out['m_w_down'], 'm_post_mlp_norm': out['m_post_mlp_norm'], 'v_meta_tokens': out['v_meta_tokens'], 'v_pre_mix_norm': out['v_pre_mix_norm'], 'v_w_in': out['v_w_in'], 'v_attn_sinks': out['v_attn_sinks'], 'v_conv_w': out['v_conv_w'], 'v_conv_b': out['v_conv_b'], 'v_gate_a_w': out['v_gate_a_w'], 'v_gate_a_b': out['v_gate_a_b'], 'v_gate_x_w': out['v_gate_x_w'], 'v_gate_x_b': out['v_gate_x_b'], 'v_lru_lambda': out['v_lru_lambda'], 'v_attn_group_norm': out['v_attn_group_norm'], 'v_lru_group_norm': out['v_lru_group_norm'], 'v_w_out': out['v_w_out'], 'v_post_mix_norm': out['v_post_mix_norm'], 'v_pre_mlp_norm': out['v_pre_mlp_norm'], 'v_w_up': out['v_w_up'], 'v_w_down': out['v_w_down'], 'v_post_mlp_norm': out['v_post_mlp_norm']}


def _loss(weights, diff, rest, loss_target):
    with _jax.named_scope("forward"):
        args = {**rest, TWIN_DIFF_INPUT: diff, **{k: w.astype(_WEIGHT_DTYPES[k]) for k, w in weights.items()}}
        y = _forward(args)
    with _jax.named_scope("loss_head"):
        err = _jnp.square(y.astype(_jnp.float32) - loss_target)
        return 0.5 * _jnp.sum(_jnp.mean(err, axis=-1)) if err.ndim else 0.5 * err


def _adamw(w, g, m, v):
    m = ADAM_B1 * m + (1.0 - ADAM_B1) * g
    v = ADAM_B2 * v + (1.0 - ADAM_B2) * _jnp.square(g)
    m_hat = m / (1.0 - ADAM_B1 ** ADAM_STEP)
    v_hat = v / (1.0 - ADAM_B2 ** ADAM_STEP)
    delta = -ADAM_LR * (m_hat / (_jnp.sqrt(v_hat) + ADAM_EPS) + ADAM_WD * w)
    return delta, m, v


def reference(x, meta_tokens, pre_mix_norm, w_in, attn_sinks, conv_w, conv_b, gate_a_w, gate_a_b, gate_x_w, gate_x_b, lru_lambda, attn_group_norm, lru_group_norm, w_out, post_mix_norm, pre_mlp_norm, w_up, w_down, post_mlp_norm, loss_target, m_meta_tokens, m_pre_mix_norm, m_w_in, m_attn_sinks, m_conv_w, m_conv_b, m_gate_a_w, m_gate_a_b, m_gate_x_w, m_gate_x_b, m_lru_lambda, m_attn_group_norm, m_lru_group_norm, m_w_out, m_post_mix_norm, m_pre_mlp_norm, m_w_up, m_w_down, m_post_mlp_norm, v_meta_tokens, v_pre_mix_norm, v_w_in, v_attn_sinks, v_conv_w, v_conv_b, v_gate_a_w, v_gate_a_b, v_gate_x_w, v_gate_x_b, v_lru_lambda, v_attn_group_norm, v_lru_group_norm, v_w_out, v_post_mix_norm, v_pre_mlp_norm, v_w_up, v_w_down, v_post_mlp_norm):
    given = dict(x=x, meta_tokens=meta_tokens, pre_mix_norm=pre_mix_norm, w_in=w_in, attn_sinks=attn_sinks, conv_w=conv_w, conv_b=conv_b, gate_a_w=gate_a_w, gate_a_b=gate_a_b, gate_x_w=gate_x_w, gate_x_b=gate_x_b, lru_lambda=lru_lambda, attn_group_norm=attn_group_norm, lru_group_norm=lru_group_norm, w_out=w_out, post_mix_norm=post_mix_norm, pre_mlp_norm=pre_mlp_norm, w_up=w_up, w_down=w_down, post_mlp_norm=post_mlp_norm, loss_target=loss_target, m_meta_tokens=m_meta_tokens, m_pre_mix_norm=m_pre_mix_norm, m_w_in=m_w_in, m_attn_sinks=m_attn_sinks, m_conv_w=m_conv_w, m_conv_b=m_conv_b, m_gate_a_w=m_gate_a_w, m_gate_a_b=m_gate_a_b, m_gate_x_w=m_gate_x_w, m_gate_x_b=m_gate_x_b, m_lru_lambda=m_lru_lambda, m_attn_group_norm=m_attn_group_norm, m_lru_group_norm=m_lru_group_norm, m_w_out=m_w_out, m_post_mix_norm=m_post_mix_norm, m_pre_mlp_norm=m_pre_mlp_norm, m_w_up=m_w_up, m_w_down=m_w_down, m_post_mlp_norm=m_post_mlp_norm, v_meta_tokens=v_meta_tokens, v_pre_mix_norm=v_pre_mix_norm, v_w_in=v_w_in, v_attn_sinks=v_attn_sinks, v_conv_w=v_conv_w, v_conv_b=v_conv_b, v_gate_a_w=v_gate_a_w, v_gate_a_b=v_gate_a_b, v_gate_x_w=v_gate_x_w, v_gate_x_b=v_gate_x_b, v_lru_lambda=v_lru_lambda, v_attn_group_norm=v_attn_group_norm, v_lru_group_norm=v_lru_group_norm, v_w_out=v_w_out, v_post_mix_norm=v_post_mix_norm, v_pre_mlp_norm=v_pre_mlp_norm, v_w_up=v_w_up, v_w_down=v_w_down, v_post_mlp_norm=v_post_mlp_norm)
    weights = {n: given[n] for n in TWIN_WEIGHTS}
    shared = {n: given[n] for n in SHARED_INPUTS}
    per_example = {n: given[n] for n in ['x']}
    grad_fn = _jax.value_and_grad(_loss, argnums=(0, 1))

    def one_microbatch(ex, loss_target):
        ex = dict(ex)
        diff = ex.pop(TWIN_DIFF_INPUT)
        return grad_fn(weights, diff, {**shared, **ex}, loss_target)

    if N_MICROBATCH == 1:
        loss, (grad_w, grad_x) = one_microbatch(per_example, given["loss_target"])
    else:
        def body(carry, xs):
            loss_sum, grad_sum = carry
            l_k, (gw_k, gx_k) = one_microbatch(xs[0], xs[1])
            with _jax.named_scope("update"):
                return (loss_sum + l_k, _jax.tree.map(_jnp.add, grad_sum, gw_k)), gx_k

        init = (_jnp.zeros((), _jnp.float32), _jax.tree.map(_jnp.zeros_like, weights))
        (loss, grad_w), grad_x = _jax.lax.scan(body, init, (per_example, given["loss_target"]))
    with _jax.named_scope("update"):
        delta_w, new_m, new_v = {}, {}, {}
        for n in TWIN_WEIGHTS:
            delta_w[n], new_m[n], new_v[n] = _adamw(weights[n], grad_w[n], given["m_" + n], given["v_" + n])
    return (loss, grad_x, *[grad_w[n] for n in TWIN_WEIGHTS], *[delta_w[n] for n in TWIN_WEIGHTS],
            *[new_m[n] for n in TWIN_WEIGHTS], *[new_v[n] for n in TWIN_WEIGHTS])
```

```python
import functools

import jax
import jax.numpy as jnp
from jax import lax
from jax.experimental import pallas as pl
from jax.experimental.pallas import tpu as pltpu

F32 = jnp.float32
BF16 = jnp.bfloat16

D_MODEL = 1024
N_HEADS = 8
ATTN_W = 512
KV_W = 128
LRU_W = 512
LRU_BLOCKS = 8
LRU_BLOCK = 64
IN_COLS = 1792
D_FF = 4096
N_META = 16
CONV_TAPS = 4
LRU_C = 8.0
ROPE_THETA = 10000.0
EPS = 1e-6
ATTN_SCALE = 0.125

ADAM_LR = 0.001
ADAM_B1 = 0.9
ADAM_B2 = 0.999
ADAM_EPS = 1e-08
ADAM_WD = 0.01
ADAM_STEP = 10

N_CHIPS = 4
N_DEV = 8
TIME_BLOCK = 128
ROW_TILE = 256
LANES = 128
SUBLANES = 8
MASKED = -1e30
VMEM_LIMIT = 56 * 1024 * 1024

MESH = pl.DeviceIdType.MESH
HBM_SPEC = pl.BlockSpec(memory_space=pltpu.HBM)

WEIGHT_NAMES = ['meta_tokens', 'pre_mix_norm', 'w_in', 'attn_sinks', 'conv_w', 'conv_b', 'gate_a_w', 'gate_a_b',
                'gate_x_w', 'gate_x_b', 'lru_lambda', 'attn_group_norm', 'lru_group_norm', 'w_out', 'post_mix_norm',
                'pre_mlp_norm', 'w_up', 'w_down', 'post_mlp_norm']
BIG_NAMES = ('w_in', 'w_out', 'w_up', 'w_down')
COLUMN_SHARDED_SMALL = ('meta_tokens', 'conv_w')
REPLICATED_NAMES = tuple(n for n in WEIGHT_NAMES if n not in BIG_NAMES and n not in COLUMN_SHARDED_SMALL)


def _sds(shape, dtype):
    return jax.ShapeDtypeStruct(tuple(shape), dtype)


def _params(*sem):
    return pltpu.CompilerParams(dimension_semantics=sem, vmem_limit_bytes=VMEM_LIMIT)


def _row_spec(width, tile=ROW_TILE):
    return pl.BlockSpec((tile, width), lambda i: (i, 0))


def _whole_spec(a):
    nd = a.ndim
    return pl.BlockSpec(a.shape, lambda *_: (0,) * nd)


def _rms(x, g):
    r = lax.rsqrt(jnp.mean(x * x, axis=-1, keepdims=True) + EPS)
    return x * r * g


def _rms_bwd(dy, x, g):
    r = lax.rsqrt(jnp.mean(x * x, axis=-1, keepdims=True) + EPS)
    xh = x * r
    dg = jnp.sum(dy * xh, axis=0, keepdims=True)
    dxh = dy * g
    dx = r * (dxh - xh * jnp.mean(dxh * xh, axis=-1, keepdims=True))
    return dx, dg


def _rope(x, cos, sin_signed):
    width = x.shape[1]
    reps = width // LANES
    if reps > 1:
        cos = jnp.tile(cos, (1, reps))
        sin_signed = jnp.tile(sin_signed, (1, reps))
    lane = lax.broadcasted_iota(jnp.int32, x.shape, 1)
    first_half = jnp.bitwise_and(lane, 32) == 0
    other = jnp.where(first_half, pltpu.roll(x, width - 32, 1), pltpu.roll(x, 32, 1))
    return x * cos + other * sin_signed


def _sigmoid(x):
    return 1.0 / (1.0 + jnp.exp(-x))


def _log1p(e):
    return jnp.where(e < 1e-3, e * (1.0 - e * (0.5 - e * (1.0 / 3.0))), jnp.log(1.0 + e))


def _one_minus_exp(x):
    series = -x * (1.0 + x * (0.5 + x * ((1.0 / 6.0) + x * (1.0 / 24.0))))
    return jnp.where(x > -0.05, series, 1.0 - jnp.exp(x))


GELU_K = 0.7978845608028654
GELU_C = 0.044715


def _gelu(x):
    t = jnp.tanh(GELU_K * (x + GELU_C * x * x * x))
    return 0.5 * x * (1.0 + t)


def _gelu_and_grad(x):
    x2 = x * x
    t = jnp.tanh(GELU_K * (x + GELU_C * x * x2))
    val = 0.5 * x * (1.0 + t)
    grad = 0.5 * (1.0 + t) + 0.5 * x * (1.0 - t * t) * GELU_K * (1.0 + 3.0 * GELU_C * x2)
    return val, grad


def _dot(a, b):
    return jnp.dot(a, b, preferred_element_type=F32)


def _dot_nt(a, b):
    return lax.dot_general(a, b, (((1,), (1,)), ((), ())), preferred_element_type=F32)


def _dot_tn(a, b):
    return lax.dot_general(a, b, (((0,), (0,)), ((), ())), preferred_element_type=F32)


def _put_rows(rows_8, values):
    d = values[0].shape[1]
    rowid = lax.broadcasted_iota(jnp.int32, (rows_8, d), 0)
    out = jnp.zeros((rows_8, d), F32)
    for k, v in enumerate(values):
        out = out + jnp.where(rowid == k, v, 0.0)
    return out


def _in_proj_fwd(h, gain, w_in, rope_cos, rope_sin):
    rows = h.shape[0]

    def body(h_ref, g_ref, w_ref, c_ref, s_ref, z_ref, q_ref, kv_ref, xg_ref):
        z = _rms(h_ref[...], g_ref[...]).astype(BF16)
        z_ref[...] = z
        proj = _dot(z, w_ref[...])
        cos = c_ref[...]
        sin = s_ref[...]
        q_ref[...] = _rope(proj[:, :ATTN_W], cos, sin).astype(BF16)
        kv_ref[:, :KV_W] = _rope(proj[:, ATTN_W:ATTN_W + KV_W], cos, sin).astype(BF16)
        kv_ref[:, KV_W:] = proj[:, ATTN_W + KV_W:ATTN_W + 2 * KV_W].astype(BF16)
        xg_ref[...] = proj[:, ATTN_W + 2 * KV_W:]

    return pl.pallas_call(
        body, name="in_proj_fwd", grid=(rows // ROW_TILE,),
        in_specs=[_row_spec(D_MODEL), _whole_spec(gain), _whole_spec(w_in), _row_spec(LANES), _row_spec(LANES)],
        out_specs=[_row_spec(D_MODEL), _row_spec(ATTN_W), _row_spec(2 * KV_W), _row_spec(2 * LRU_W)],
        out_shape=[_sds((rows, D_MODEL), BF16), _sds((rows, ATTN_W), BF16), _sds((rows, 2 * KV_W), BF16),
                   _sds((rows, 2 * LRU_W), F32)],
        compiler_params=_params("parallel"),
    )(h, gain, w_in, rope_cos, rope_sin)


def _in_proj_bwd(dh_mid, h, dq, dkv, dxg, rope_cos, rope_sin, gain, w_in):
    rows = h.shape[0]

    def body(dhm_ref, h_ref, dq_ref, dkv_ref, dxg_ref, c_ref, s_ref, g_ref, w_ref, dh_ref, dp_ref, gacc_ref):
        @pl.when(pl.program_id(0) == 0)
        def _():
            gacc_ref[...] = jnp.zeros_like(gacc_ref)

        cos = c_ref[...]
        sin = -s_ref[...]
        dp_ref[:, :ATTN_W] = _rope(dq_ref[...], cos, sin).astype(BF16)
        dp_ref[:, ATTN_W:ATTN_W + KV_W] = _rope(dkv_ref[:, :KV_W], cos, sin).astype(BF16)
        dp_ref[:, ATTN_W + KV_W:ATTN_W + 2 * KV_W] = dkv_ref[:, KV_W:].astype(BF16)
        dp_ref[:, ATTN_W + 2 * KV_W:] = dxg_ref[...].astype(BF16)
        dz = _dot_nt(dp_ref[...], w_ref[...])
        dx, dg = _rms_bwd(dz, h_ref[...], g_ref[...])
        dh_ref[...] = dhm_ref[...] + dx
        gacc_ref[...] += _put_rows(SUBLANES, [dg])

    return pl.pallas_call(
        body, name="in_proj_bwd", grid=(rows // ROW_TILE,),
        in_specs=[_row_spec(D_MODEL), _row_spec(D_MODEL), _row_spec(ATTN_W), _row_spec(2 * KV_W), _row_spec(2 * LRU_W),
                  _row_spec(LANES), _row_spec(LANES), _whole_spec(gain), _whole_spec(w_in)],
        out_specs=[_row_spec(D_MODEL), _row_spec(IN_COLS), pl.BlockSpec((SUBLANES, D_MODEL), lambda i: (0, 0))],
        out_shape=[_sds((rows, D_MODEL), F32), _sds((rows, IN_COLS), BF16), _sds((SUBLANES, D_MODEL), F32)],
        compiler_params=_params("arbitrary"),
    )(dh_mid, h, dq, dkv, dxg, rope_cos, rope_sin, gain, w_in)


def _kv_lane_variants(t, group):
    lane = lax.broadcasted_iota(jnp.int32, t.shape, 1)
    low = lane < 64
    swapped = pltpu.roll(t, 64, 1)
    if group == 0:
        lo, hi = jnp.where(low, t, 0.0), jnp.where(low, 0.0, swapped)
    else:
        lo, hi = jnp.where(low, swapped, 0.0), jnp.where(low, 0.0, t)
    return lo.astype(BF16), hi.astype(BF16)


def _window_mask(j):
    r = lax.broadcasted_iota(jnp.int32, (TIME_BLOCK, 2 * TIME_BLOCK), 0)
    c = lax.broadcasted_iota(jnp.int32, (TIME_BLOCK, 2 * TIME_BLOCK), 1)
    return (c > r) & (c <= r + TIME_BLOCK) & ((c >= TIME_BLOCK) | (j > 0))


def _attn_fwd(q, kv, sinks, batch, n_blocks):
    rows = q.shape[0]

    def body(sink_ref, q_ref, kvc_ref, kvp_ref, o_ref, lse_ref):
        j = pl.program_id(1)
        kv2 = jnp.concatenate([kvp_ref[...], kvc_ref[...]], axis=0).astype(F32)
        mask = _window_mask(j)
        lane8 = lax.broadcasted_iota(jnp.int32, (TIME_BLOCK, N_HEADS), 1)
        lse_tile = jnp.zeros((TIME_BLOCK, N_HEADS), F32)
        for group in range(2):
            k_lo, k_hi = _kv_lane_variants(kv2[:, :KV_W], group)
            v_lo, v_hi = _kv_lane_variants(kv2[:, KV_W:], group)
            for p in range(2):
                pair = 2 * group + p
                q_pair = q_ref[:, LANES * pair:LANES * (pair + 1)]
                acc = None
                for half, (kk, vv) in enumerate(((k_lo, v_lo), (k_hi, v_hi))):
                    head = 2 * pair + half
                    sink = sink_ref[head]
                    s = jnp.where(mask, _dot_nt(q_pair, kk) * ATTN_SCALE, MASKED)
                    m = jnp.maximum(jnp.max(s, axis=1, keepdims=True), sink)
                    e = jnp.exp(s - m)
                    den = jnp.sum(e, axis=1, keepdims=True) + jnp.exp(sink - m)
                    part = _dot((e / den).astype(BF16), vv)
                    acc = part if acc is None else acc + part
                    lse_tile = lse_tile + jnp.where(lane8 == head, m + jnp.log(den), 0.0)
                o_ref[:, LANES * pair:LANES * (pair + 1)] = acc
        lse_ref[...] = lse_tile

    def blk(width):
        return pl.BlockSpec((TIME_BLOCK, width), lambda b, j: (b * n_blocks + j, 0))

    prev = pl.BlockSpec((TIME_BLOCK, 2 * KV_W), lambda b, j: (b * n_blocks + jnp.maximum(j - 1, 0), 0))
    return pl.pallas_call(
        body, name="attn_fwd", grid=(batch, n_blocks),
        in_specs=[pl.BlockSpec(memory_space=pltpu.SMEM), blk(ATTN_W), blk(2 * KV_W), prev],
        out_specs=[blk(ATTN_W), blk(N_HEADS)],
        out_shape=[_sds((rows, ATTN_W), F32), _sds((rows, N_HEADS), F32)],
        compiler_params=_params("parallel", "parallel"),
    )(sinks, q, kv, kv)


def _attn_bwd(q, kv, sinks, out, lse, d_out, batch, n_blocks):
    rows = q.shape[0]

    def body(sink_ref, q_ref, kvc_ref, kvp_ref, o_ref, do_ref, lse_ref, dq_ref, dkv_ref, dsink_ref, carry):
        b = pl.program_id(0)
        j = pl.program_id(1)

        @pl.when((b == 0) & (j == 0))
        def _():
            dsink_ref[...] = jnp.zeros_like(dsink_ref)

        @pl.when(j < n_blocks)
        def _():
            kv2 = jnp.concatenate([kvp_ref[...], kvc_ref[...]], axis=0).astype(F32)
            mask = _window_mask(j)
            lane = lax.broadcasted_iota(jnp.int32, (TIME_BLOCK, LANES), 1)
            lane2 = lax.broadcasted_iota(jnp.int32, (2 * TIME_BLOCK, LANES), 1)
            low2 = lane2 < 64
            lse_tile = lse_ref[...]
            dk_tile = jnp.zeros((2 * TIME_BLOCK, KV_W), F32)
            dv_tile = jnp.zeros((2 * TIME_BLOCK, KV_W), F32)
            dsink_vals = []
            for group in range(2):
                k_lo, k_hi = _kv_lane_variants(kv2[:, :KV_W], group)
                v_lo, v_hi = _kv_lane_variants(kv2[:, KV_W:], group)
                dk_acc = [jnp.zeros((2 * TIME_BLOCK, LANES), F32) for _ in range(2)]
                dv_acc = [jnp.zeros((2 * TIME_BLOCK, LANES), F32) for _ in range(2)]
                for p in range(2):
                    pair = 2 * group + p
                    cols = slice(LANES * pair, LANES * (pair + 1))
                    q_pair = q_ref[:, cols]
                    do_pair = do_ref[:, cols]
                    do_b = do_pair.astype(BF16)
                    od = do_pair * o_ref[:, cols]
                    dq_acc = jnp.zeros((TIME_BLOCK, LANES), F32)
                    for half, (kk, vv) in enumerate(((k_lo, v_lo), (k_hi, v_hi))):
                        head = 2 * pair + half
                        sink = sink_ref[head]
                        lse_h = lse_tile[:, head:head + 1]
                        in_half = (lane < 64) if half == 0 else (lane >= 64)
                        delta = jnp.sum(jnp.where(in_half, od, 0.0), axis=1, keepdims=True)
                        s = jnp.where(mask, _dot_nt(q_pair, kk) * ATTN_SCALE, MASKED)
                        prob = jnp.exp(s - lse_h)
                        dp = _dot_nt(do_b, vv)
                        ds = (prob * (dp - delta) * ATTN_SCALE).astype(BF16)
                        dq_acc = dq_acc + _dot(ds, kk)
                        dk_acc[half] = dk_acc[half] + _dot_tn(ds, q_pair)
                        dv_acc[half] = dv_acc[half] + _dot_tn(prob.astype(BF16), do_b)
                        dsink_vals.append(jnp.sum(-jnp.exp(sink - lse_h) * delta, axis=0, keepdims=True))
                    dq_ref[:, cols] = dq_acc
                if group == 0:
                    dk_tile = dk_tile + jnp.where(low2, dk_acc[0] + pltpu.roll(dk_acc[1], 64, 1), 0.0)
                    dv_tile = dv_tile + jnp.where(low2, dv_acc[0] + pltpu.roll(dv_acc[1], 64, 1), 0.0)
                else:
                    dk_tile = dk_tile + jnp.where(low2, 0.0, pltpu.roll(dk_acc[0], 64, 1) + dk_acc[1])
                    dv_tile = dv_tile + jnp.where(low2, 0.0, pltpu.roll(dv_acc[0], 64, 1) + dv_acc[1])

            @pl.when(j > 0)
            def _():
                dkv_ref[:, :KV_W] = carry[:, :KV_W] + dk_tile[:TIME_BLOCK]
                dkv_ref[:, KV_W:] = carry[:, KV_W:] + dv_tile[:TIME_BLOCK]

            carry[:, :KV_W] = dk_tile[TIME_BLOCK:]
            carry[:, KV_W:] = dv_tile[TIME_BLOCK:]
            rowid = lax.broadcasted_iota(jnp.int32, (N_HEADS, LANES), 0)
            upd = jnp.zeros((N_HEADS, LANES), F32)
            for head, val in enumerate(dsink_vals):
                upd = upd + jnp.where(rowid == head, val, 0.0)
            dsink_ref[...] += upd

        @pl.when(j == n_blocks)
        def _():
            dkv_ref[...] = carry[...]

    last = n_blocks - 1

    def blk(width):
        return pl.BlockSpec((TIME_BLOCK, width), lambda b, j: (b * n_blocks + jnp.minimum(j, last), 0))

    prev = pl.BlockSpec((TIME_BLOCK, 2 * KV_W), lambda b, j: (b * n_blocks + jnp.maximum(jnp.minimum(j, last) - 1, 0), 0))
    dkv_spec = pl.BlockSpec((TIME_BLOCK, 2 * KV_W), lambda b, j: (b * n_blocks + jnp.maximum(j - 1, 0), 0))
    return pl.pallas_call(
        body, name="attn_bwd", grid=(batch, n_blocks + 1),
        in_specs=[pl.BlockSpec(memory_space=pltpu.SMEM), blk(ATTN_W), blk(2 * KV_W), prev, blk(ATTN_W), blk(ATTN_W),
                  blk(N_HEADS)],
        out_specs=[blk(ATTN_W), dkv_spec, pl.BlockSpec((N_HEADS, LANES), lambda b, j: (0, 0))],
        out_shape=[_sds((rows, ATTN_W), F32), _sds((rows, 2 * KV_W), F32), _sds((N_HEADS, LANES), F32)],
        scratch_shapes=[pltpu.VMEM((TIME_BLOCK, 2 * KV_W), F32)],
        compiler_params=_params("arbitrary", "arbitrary"),
    )(sinks, q, kv, kv, out, d_out, lse)


def _conv_taps(xb, prev8):
    ext = jnp.concatenate([prev8, xb], axis=0)
    n = ext.shape[0]
    return [xb] + [pltpu.roll(ext, k, 0)[SUBLANES:n] for k in range(1, CONV_TAPS)]


def _lru_gates(xc, wa, ba, wx, bx, lam):
    xcb = xc.astype(BF16)
    r = _sigmoid(_dot(xcb, wa) + ba)
    i = _sigmoid(_dot(xcb, wx) + bx)
    sp = jnp.maximum(-lam, 0.0) + _log1p(jnp.exp(-jnp.abs(lam)))
    log_a = -LRU_C * r * sp
    a = jnp.exp(log_a)
    mult = jnp.sqrt(_one_minus_exp(2.0 * log_a))
    return xcb, r, i, sp, a, mult


def _scan_fwd(a, u):
    n = a.shape[0]
    row = lax.broadcasted_iota(jnp.int32, a.shape, 0)
    s = 1
    while s < n:
        valid = row >= s
        u = jnp.where(valid, u + a * pltpu.roll(u, s, 0), u)
        a = jnp.where(valid, a * pltpu.roll(a, s, 0), a)
        s *= 2
    return a, u


def _scan_rev(cf, g):
    n = g.shape[0]
    row = lax.broadcasted_iota(jnp.int32, g.shape, 0)
    s = 1
    while s < n:
        valid = row + s < n
        g = jnp.where(valid, g + cf * pltpu.roll(g, n - s, 0), g)
        cf = jnp.where(valid, cf * pltpu.roll(cf, n - s, 0), cf)
        s *= 2
    return g


def _lru_fwd(xg, conv_w, conv_b, wa, ba, wx, bx, lam, batch, n_blocks):
    rows = xg.shape[0]

    def body(xg_ref, cw_ref, cb_ref, wa_ref, ba_ref, wx_ref, bx_ref, lam_ref, hs_ref, lru_ref, x_prev, h_carry):
        @pl.when(pl.program_id(1) == 0)
        def _():
            x_prev[...] = jnp.zeros_like(x_prev)
            h_carry[...] = jnp.zeros_like(h_carry)

        xb = xg_ref[:, :LRU_W]
        taps = _conv_taps(xb, x_prev[...])
        xc = cb_ref[...] + sum(cw_ref[CONV_TAPS - 1 - k:CONV_TAPS - k, :] * taps[k] for k in range(CONV_TAPS))
        _, _, i, _, a, mult = _lru_gates(xc, wa_ref[...], ba_ref[...], wx_ref[...], bx_ref[...], lam_ref[...])
        a_cum, h = _scan_fwd(a, mult * i * xc)
        h = h + a_cum * h_carry[...]
        hs_ref[...] = h
        lru_ref[...] = h * _gelu(xg_ref[:, LRU_W:])
        h_carry[...] = h[TIME_BLOCK - 1:TIME_BLOCK, :]
        x_prev[...] = xb[TIME_BLOCK - SUBLANES:TIME_BLOCK, :]

    def blk(width):
        return pl.BlockSpec((TIME_BLOCK, width), lambda b, j: (b * n_blocks + j, 0))

    small = [conv_w, conv_b, wa, ba, wx, bx, lam]
    return pl.pallas_call(
        body, name="lru_fwd", grid=(batch, n_blocks),
        in_specs=[blk(2 * LRU_W)] + [_whole_spec(a) for a in small],
        out_specs=[blk(LRU_W), blk(LRU_W)],
        out_shape=[_sds((rows, LRU_W), F32), _sds((rows, LRU_W), F32)],
        scratch_shapes=[pltpu.VMEM((SUBLANES, LRU_W), F32), pltpu.VMEM((1, LRU_W), F32)],
        compiler_params=_params("arbitrary", "arbitrary"),
    )(xg, *small)


def _lru_bwd(xg, hs, d_lru, conv_w, conv_b, wa, ba, wx, bx, lam, batch, n_blocks):
    rows = xg.shape[0]
    last_row = TIME_BLOCK - 1

    def body(xg_ref, xgh_ref, hs_ref, hsh_ref, dl_ref, cw_ref, cb_ref, wa_ref, ba_ref, wx_ref, bx_ref, lam_ref,
             dxg_ref, dwa_ref, dwx_ref, vec_ref, dh_carry, dxc_next):
        b = pl.program_id(0)
        j = pl.program_id(1)

        @pl.when((b == 0) & (j == 0))
        def _():
            dwa_ref[...] = jnp.zeros_like(dwa_ref)
            dwx_ref[...] = jnp.zeros_like(dwx_ref)
            vec_ref[...] = jnp.zeros_like(vec_ref)

        @pl.when(j == 0)
        def _():
            dh_carry[...] = jnp.zeros_like(dh_carry)
            dxc_next[...] = jnp.zeros_like(dxc_next)

        first = j == n_blocks - 1
        xb = xg_ref[:, :LRU_W]
        prev8 = jnp.where(first, 0.0, xgh_ref[:, :LRU_W])
        h_before = jnp.where(first, 0.0, hsh_ref[SUBLANES - 1:SUBLANES, :])
        cw = cw_ref[...]
        lam = lam_ref[...]
        wa = wa_ref[...]
        wx = wx_ref[...]
        taps = _conv_taps(xb, prev8)
        xc = cb_ref[...] + sum(cw[CONV_TAPS - 1 - k:CONV_TAPS - k, :] * taps[k] for k in range(CONV_TAPS))
        xcb, r, i, sp, a, mult = _lru_gates(xc, wa, ba_ref[...], wx, bx_ref[...], lam)
        hs = hs_ref[...]
        row = lax.broadcasted_iota(jnp.int32, hs.shape, 0)
        h_prev = jnp.where(row == 0, h_before, pltpu.roll(hs, 1, 0))
        dl = dl_ref[...]
        gate, dgate = _gelu_and_grad(xg_ref[:, LRU_W:])
        dxg_ref[:, LRU_W:] = dl * hs * dgate
        g = dl * gate + jnp.where(row == last_row, dh_carry[...], 0.0)
        cf = jnp.where(row == last_row, 0.0, pltpu.roll(a, last_row, 0))
        dh = _scan_rev(cf, g)
        dh_carry[...] = a[0:1, :] * dh[0:1, :]
        dmult = dh * i * xc
        di = dh * mult * xc
        dxc = dh * mult * i
        dlog_a = dh * h_prev * a - dmult * (a * a / mult)
        dr = dlog_a * (-LRU_C * sp)
        dlam = jnp.sum(dlog_a * (-LRU_C * r), axis=0, keepdims=True) * (-_sigmoid(-lam))
        dpr = dr * r * (1.0 - r)
        dpi = di * i * (1.0 - i)
        dprb = dpr.astype(BF16)
        dpib = dpi.astype(BF16)
        dxc = dxc + _dot_nt(dprb, wa) + _dot_nt(dpib, wx)
        dwa_ref[...] += _dot_tn(xcb, dprb)
        dwx_ref[...] += _dot_tn(xcb, dpib)
        ext = jnp.concatenate([dxc, dxc_next[...]], axis=0)
        n = ext.shape[0]
        dxb = cw[CONV_TAPS - 1:CONV_TAPS, :] * dxc
        for k in range(1, CONV_TAPS):
            dxb = dxb + cw[CONV_TAPS - 1 - k:CONV_TAPS - k, :] * pltpu.roll(ext, n - k, 0)[:TIME_BLOCK]
        dxg_ref[:, :LRU_W] = dxb
        dxc_next[...] = dxc[:SUBLANES, :]
        vecs = [jnp.sum(dxc * taps[CONV_TAPS - 1 - t], axis=0, keepdims=True) for t in range(CONV_TAPS)]
        vecs += [jnp.sum(dxc, axis=0, keepdims=True), jnp.sum(dpr, axis=0, keepdims=True),
                 jnp.sum(dpi, axis=0, keepdims=True), dlam]
        vec_ref[...] += _put_rows(SUBLANES, vecs)

    def tblk(b, j):
        return b * n_blocks + n_blocks - 1 - j

    def blk(width):
        return pl.BlockSpec((TIME_BLOCK, width), lambda b, j: (tblk(b, j), 0))

    per8 = TIME_BLOCK // SUBLANES

    def halo(width):
        return pl.BlockSpec((SUBLANES, width), lambda b, j: (jnp.maximum(per8 * tblk(b, j) - 1, 0), 0))

    small = [conv_w, conv_b, wa, ba, wx, bx, lam]
    acc = lambda shape: pl.BlockSpec(shape, lambda b, j: (0, 0))
    return pl.pallas_call(
        body, name="lru_bwd", grid=(batch, n_blocks),
        in_specs=[blk(2 * LRU_W), halo(2 * LRU_W), blk(LRU_W), halo(LRU_W), blk(LRU_W)] + [_whole_spec(a) for a in small],
        out_specs=[blk(2 * LRU_W), acc((LRU_W, LRU_W)), acc((LRU_W, LRU_W)), acc((SUBLANES, LRU_W))],
        out_shape=[_sds((rows, 2 * LRU_W), F32), _sds((LRU_W, LRU_W), F32), _sds((LRU_W, LRU_W), F32),
                   _sds((SUBLANES, LRU_W), F32)],
        scratch_shapes=[pltpu.VMEM((1, LRU_W), F32), pltpu.VMEM((SUBLANES, LRU_W), F32)],
        compiler_params=_params("arbitrary", "arbitrary"),
    )(xg, xg, hs, hs, d_lru, *small)


def _out_proj_fwd(attn, lru, g_attn, g_lru, w_out, h, g_post):
    rows = h.shape[0]

    def body(at_ref, lr_ref, ga_ref, gl_ref, w_ref, h_ref, gp_ref, grp_ref, o_ref, hm_ref):
        a = _rms(at_ref[...], ga_ref[...]).astype(BF16)
        l = _rms(lr_ref[...], gl_ref[...]).astype(BF16)
        grp_ref[:, :ATTN_W] = a
        grp_ref[:, ATTN_W:] = l
        o = _dot(a, w_ref[:ATTN_W, :]) + _dot(l, w_ref[ATTN_W:, :])
        o_ref[...] = o
        hm_ref[...] = h_ref[...] + _rms(o, gp_ref[...])

    return pl.pallas_call(
        body, name="out_proj_fwd", grid=(rows // ROW_TILE,),
        in_specs=[_row_spec(ATTN_W), _row_spec(LRU_W), _whole_spec(g_attn), _whole_spec(g_lru), _whole_spec(w_out),
                  _row_spec(D_MODEL), _whole_spec(g_post)],
        out_specs=[_row_spec(D_MODEL), _row_spec(D_MODEL), _row_spec(D_MODEL)],
        out_shape=[_sds((rows, D_MODEL), BF16), _sds((rows, D_MODEL), F32), _sds((rows, D_MODEL), F32)],
        compiler_params=_params("parallel"),
    )(attn, lru, g_attn, g_lru, w_out, h, g_post)


def _out_proj_bwd(dh_mid, o, attn, lru, g_attn, g_lru, g_post, w_out):
    rows = o.shape[0]

    def body(dhm_ref, o_ref, at_ref, lr_ref, ga_ref, gl_ref, gp_ref, w_ref, do_ref, dat_ref, dlr_ref, gacc_ref):
        @pl.when(pl.program_id(0) == 0)
        def _():
            gacc_ref[...] = jnp.zeros_like(gacc_ref)

        do, dgp = _rms_bwd(dhm_ref[...], o_ref[...], gp_ref[...])
        dob = do.astype(BF16)
        do_ref[...] = dob
        dat, dga = _rms_bwd(_dot_nt(dob, w_ref[:ATTN_W, :]), at_ref[...], ga_ref[...])
        dlr, dgl = _rms_bwd(_dot_nt(dob, w_ref[ATTN_W:, :]), lr_ref[...], gl_ref[...])
        dat_ref[...] = dat
        dlr_ref[...] = dlr
        gacc_ref[...] += _put_rows(SUBLANES, [dgp, jnp.concatenate([dga, dgl], axis=1)])

    return pl.pallas_call(
        body, name="out_proj_bwd", grid=(rows // ROW_TILE,),
        in_specs=[_row_spec(D_MODEL), _row_spec(D_MODEL), _row_spec(ATTN_W), _row_spec(LRU_W), _whole_spec(g_attn),
                  _whole_spec(g_lru), _whole_spec(g_post), _whole_spec(w_out)],
        out_specs=[_row_spec(D_MODEL), _row_spec(ATTN_W), _row_spec(LRU_W),
                   pl.BlockSpec((SUBLANES, D_MODEL), lambda i: (0, 0))],
        out_shape=[_sds((rows, D_MODEL), BF16), _sds((rows, ATTN_W), F32), _sds((rows, LRU_W), F32),
                   _sds((SUBLANES, D_MODEL), F32)],
        compiler_params=_params("arbitrary"),
    )(dh_mid, o, attn, lru, g_attn, g_lru, g_post, w_out)


def _mlp_fwd(h_mid, g_pre, w_up4, w_down, g_post):
    rows = h_mid.shape[0]

    def body(h_ref, g1_ref, wu_ref, wd_ref, g2_ref, z_ref, up_ref, act_ref, y_ref, ho_ref):
        x = h_ref[...]
        z = _rms(x, g1_ref[...]).astype(BF16)
        z_ref[...] = z
        y = jnp.zeros((ROW_TILE, D_MODEL), F32)
        for s in range(N_CHIPS):
            cols = slice(s * D_MODEL, (s + 1) * D_MODEL)
            u = _dot(z, wu_ref[s])
            up_ref[:, cols] = u
            a = jnp.square(jnp.maximum(u, 0.0)).astype(BF16)
            act_ref[:, cols] = a
            y = y + _dot(a, wd_ref[cols, :])
        y_ref[...] = y
        ho_ref[...] = x + _rms(y, g2_ref[...])

    return pl.pallas_call(
        body, name="mlp_fwd", grid=(rows // ROW_TILE,),
        in_specs=[_row_spec(D_MODEL), _whole_spec(g_pre), _whole_spec(w_up4), _whole_spec(w_down), _whole_spec(g_post)],
        out_specs=[_row_spec(D_MODEL), _row_spec(D_FF), _row_spec(D_FF), _row_spec(D_MODEL), _row_spec(D_MODEL)],
        out_shape=[_sds((rows, D_MODEL), BF16), _sds((rows, D_FF), F32), _sds((rows, D_FF), BF16),
                   _sds((rows, D_MODEL), F32), _sds((rows, D_MODEL), F32)],
        compiler_params=_params("parallel"),
    )(h_mid, g_pre, w_up4, w_down, g_post)


def _mlp_bwd(dh_out, h_mid, y, up, w_up4, w_down, g_pre, g_post):
    rows = y.shape[0]

    def body(dh_ref, hm_ref, y_ref, up_ref, wu_ref, wd_ref, g1_ref, g2_ref, dhm_ref, dy_ref, dup_ref, gacc_ref):
        @pl.when(pl.program_id(0) == 0)
        def _():
            gacc_ref[...] = jnp.zeros_like(gacc_ref)

        dh = dh_ref[...]
        dy, dg2 = _rms_bwd(dh, y_ref[...], g2_ref[...])
        dyb = dy.astype(BF16)
        dy_ref[...] = dyb
        dz = jnp.zeros((ROW_TILE, D_MODEL), F32)
        for s in range(N_CHIPS):
            cols = slice(s * D_MODEL, (s + 1) * D_MODEL)
            dact = _dot_nt(dyb, wd_ref[cols, :])
            dup = (dact * (2.0 * jnp.maximum(up_ref[:, cols], 0.0))).astype(BF16)
            dup_ref[:, cols] = dup
            dz = dz + _dot_nt(dup, wu_ref[s])
        dx, dg1 = _rms_bwd(dz, hm_ref[...], g1_ref[...])
        dhm_ref[...] = dh + dx
        gacc_ref[...] += _put_rows(SUBLANES, [dg1, dg2])

    return pl.pallas_call(
        body, name="mlp_bwd", grid=(rows // ROW_TILE,),
        in_specs=[_row_spec(D_MODEL), _row_spec(D_MODEL), _row_spec(D_MODEL), _row_spec(D_FF), _whole_spec(w_up4),
                  _whole_spec(w_down), _whole_spec(g_pre), _whole_spec(g_post)],
        out_specs=[_row_spec(D_MODEL), _row_spec(D_MODEL), _row_spec(D_FF),
                   pl.BlockSpec((SUBLANES, D_MODEL), lambda i: (0, 0))],
        out_shape=[_sds((rows, D_MODEL), F32), _sds((rows, D_MODEL), BF16), _sds((rows, D_FF), BF16),
                   _sds((SUBLANES, D_MODEL), F32)],
        compiler_params=_params("arbitrary"),
    )(dh_out, h_mid, y, up, w_up4, w_down, g_pre, g_post)


def _matmul_tn(a, b, tm, tn, name, column_blocks=False):
    rows, m = a.shape
    n = b.shape[1]

    def body(a_ref, b_ref, o_ref):
        o_ref[...] = _dot_tn(a_ref[...], b_ref[...])

    if column_blocks:
        out_spec = pl.BlockSpec((None, tm, tn), lambda i, j: (j, i, 0))
        out_shape = _sds((n // tn, m, tn), F32)
    else:
        out_spec = pl.BlockSpec((tm, tn), lambda i, j: (i, j))
        out_shape = _sds((m, n), F32)
    return pl.pallas_call(
        body, name=name, grid=(m // tm, n // tn),
        in_specs=[pl.BlockSpec((rows, tm), lambda i, j: (0, i)), pl.BlockSpec((rows, tn), lambda i, j: (0, j))],
        out_specs=out_spec, out_shape=out_shape,
        compiler_params=_params("parallel", "parallel"),
    )(a, b)


def _loss_head(h, target_padded, batch, n_blocks, n_real):
    rows = h.shape[0]

    def body(h_ref, t_ref, dh_ref, l_ref):
        j = pl.program_id(1)

        @pl.when((pl.program_id(0) == 0) & (j == 0))
        def _():
            l_ref[...] = jnp.zeros_like(l_ref)

        pos = j * TIME_BLOCK + lax.broadcasted_iota(jnp.int32, (TIME_BLOCK, D_MODEL), 0)
        d = jnp.where((pos >= N_META) & (pos < n_real), h_ref[...] - t_ref[...], 0.0)
        dh_ref[...] = d * (1.0 / D_MODEL)
        l_ref[...] += jnp.sum(jnp.sum(d * d, axis=0, keepdims=True), axis=1, keepdims=True)

    blk = pl.BlockSpec((TIME_BLOCK, D_MODEL), lambda b, j: (b * n_blocks + j, 0))
    return pl.pallas_call(
        body, name="loss_head", grid=(batch, n_blocks),
        in_specs=[blk, blk], out_specs=[blk, pl.BlockSpec((SUBLANES, LANES), lambda b, j: (0, 0))],
        out_shape=[_sds((rows, D_MODEL), F32), _sds((SUBLANES, LANES), F32)],
        compiler_params=_params("arbitrary", "arbitrary"),
    )(h, target_padded)


def _meta_grad(dh0, batch, rows_per_example):
    per = rows_per_example // N_META

    def body(d_ref, o_ref):
        @pl.when(pl.program_id(0) == 0)
        def _():
            o_ref[...] = jnp.zeros_like(o_ref)

        o_ref[...] += d_ref[...]

    return pl.pallas_call(
        body, name="meta_grad", grid=(batch,),
        in_specs=[pl.BlockSpec((N_META, D_MODEL), lambda b: (b * per, 0))],
        out_specs=pl.BlockSpec((N_META, D_MODEL), lambda b: (0, 0)),
        out_shape=_sds((N_META, D_MODEL), F32),
        compiler_params=_params("arbitrary"),
    )(dh0)


def _elementwise_tile(rows, cols):
    tile = rows
    while tile * cols * 4 > (1 << 20) and tile % 16 == 0:
        tile //= 2
    return tile


def _sum_slots(buf, name):
    k, rows, cols = buf.shape
    tile = _elementwise_tile(rows, cols)

    def body(*refs):
        total = refs[0][...].astype(F32)
        for r in refs[1:k]:
            total = total + r[...].astype(F32)
        refs[k][...] = total

    def slot(s):
        return pl.BlockSpec((None, tile, cols), lambda i: (s, i, 0))

    return pl.pallas_call(
        body, name=name, grid=(rows // tile,),
        in_specs=[slot(s) for s in range(k)], out_specs=pl.BlockSpec((tile, cols), lambda i: (i, 0)),
        out_shape=_sds((rows, cols), F32), compiler_params=_params("parallel"),
    )(*([buf] * k))


def _adamw(w, m, v, grads, name):
    rows, cols = w.shape
    tile = _elementwise_tile(rows, cols)
    ng = len(grads)
    m_scale = 1.0 - ADAM_B1 ** ADAM_STEP
    v_scale = 1.0 - ADAM_B2 ** ADAM_STEP

    def body(*refs):
        w_ref, m_ref, v_ref = refs[:3]
        g_refs = refs[3:3 + ng]
        g_out, d_out, m_out, v_out = refs[3 + ng:]
        g = g_refs[0][...]
        for r in g_refs[1:]:
            g = g + r[...]
        m_new = ADAM_B1 * m_ref[...] + (1.0 - ADAM_B1) * g
        v_new = ADAM_B2 * v_ref[...] + (1.0 - ADAM_B2) * (g * g)
        m_hat = m_new / m_scale
        v_hat = v_new / v_scale
        g_out[...] = g
        d_out[...] = -ADAM_LR * (m_hat / (jnp.sqrt(v_hat) + ADAM_EPS) + ADAM_WD * w_ref[...])
        m_out[...] = m_new
        v_out[...] = v_new

    spec = pl.BlockSpec((tile, cols), lambda i: (i, 0))
    return pl.pallas_call(
        body, name=name, grid=(rows // tile,),
        in_specs=[spec] * (3 + ng), out_specs=[spec] * 4, out_shape=[_sds((rows, cols), F32)] * 4,
        compiler_params=_params("parallel"),
    )(w, m, v, *grads)


def _position():
    return lax.axis_index("x"), lax.axis_index("y"), lax.axis_index("c")


def _other_chips(x, y):
    return [(1 - x, y), (x, 1 - y), (1 - x, 1 - y)]


def _exchange_chips(arrays, name, scatter):
    n = len(arrays)

    def body(*refs):
        src, dst = refs[:n], refs[n:2 * n]
        send_sems, recv_sems, local_sems = refs[2 * n:]
        x, y, c = _position()
        mine = 2 * x + y
        copies = []
        for i in range(n):
            own = src[i].at[mine] if scatter else src[i]
            copies.append(pltpu.make_async_copy(own, dst[i].at[mine], local_sems.at[i]))
        for k, (px, py) in enumerate(_other_chips(x, y)):
            for i in range(n):
                piece = src[i].at[2 * px + py] if scatter else src[i]
                copies.append(pltpu.make_async_remote_copy(
                    src_ref=piece, dst_ref=dst[i].at[mine], send_sem=send_sems.at[k, i], recv_sem=recv_sems.at[k, i],
                    device_id=(px, py, c), device_id_type=MESH))
        for cp in copies:
            cp.start()
        for cp in copies:
            cp.wait()

    def out_shape(a):
        return _sds(a.shape if scatter else (N_CHIPS,) + a.shape, a.dtype)

    return pl.pallas_call(
        body, name=name, in_specs=[HBM_SPEC] * n, out_specs=[HBM_SPEC] * n, out_shape=[out_shape(a) for a in arrays],
        scratch_shapes=[pltpu.SemaphoreType.DMA((N_CHIPS - 1, n)), pltpu.SemaphoreType.DMA((N_CHIPS - 1, n)),
                        pltpu.SemaphoreType.DMA((n,))],
    )(*arrays)


def _swap_sibling(arrays, name):
    n = len(arrays)

    def body(*refs):
        src, dst = refs[:n], refs[n:2 * n]
        send_sems, recv_sems = refs[2 * n:]
        x, y, c = _position()
        copies = [pltpu.make_async_remote_copy(
            src_ref=src[i], dst_ref=dst[i], send_sem=send_sems.at[i], recv_sem=recv_sems.at[i],
            device_id=(x, y, 1 - c), device_id_type=MESH) for i in range(n)]
        for cp in copies:
            cp.start()
        for cp in copies:
            cp.wait()

    return pl.pallas_call(
        body, name=name, in_specs=[HBM_SPEC] * n, out_specs=[HBM_SPEC] * n,
        out_shape=[_sds(a.shape, a.dtype) for a in arrays],
        scratch_shapes=[pltpu.SemaphoreType.DMA((n,)), pltpu.SemaphoreType.DMA((n,))],
    )(*arrays)


def _gather_all_devices(a, name):
    def body(src, dst, send_sems, recv_sems, local_sem):
        x, y, c = _position()
        mine = 4 * x + 2 * y + c
        copies = [pltpu.make_async_copy(src, dst.at[mine], local_sem)]
        k = 0
        for fx in range(2):
            for fy in range(2):
                for fc in range(2):
                    if fx or fy or fc:
                        peer = (1 - x if fx else x, 1 - y if fy else y, 1 - c if fc else c)
                        copies.append(pltpu.make_async_remote_copy(
                            src_ref=src, dst_ref=dst.at[mine], send_sem=send_sems.at[k], recv_sem=recv_sems.at[k],
                            device_id=peer, device_id_type=MESH))
                        k += 1
        for cp in copies:
            cp.start()
        for cp in copies:
            cp.wait()

    return pl.pallas_call(
        body, name=name, in_specs=[HBM_SPEC], out_specs=HBM_SPEC, out_shape=_sds((N_DEV,) + a.shape, a.dtype),
        scratch_shapes=[pltpu.SemaphoreType.DMA((N_DEV - 1,)), pltpu.SemaphoreType.DMA((N_DEV - 1,)),
                        pltpu.SemaphoreType.DMA(())],
    )(a)


def _rope_tables(batch, rows_per_example):
    inv_freq = ROPE_THETA ** (-jnp.arange(0, 64, 2, dtype=F32) / 64)
    ang = jnp.arange(rows_per_example, dtype=F32)[:, None] * inv_freq[None, :]
    cos, sin = jnp.cos(ang), jnp.sin(ang)
    cos128 = jnp.concatenate([cos, cos, cos, cos], axis=1)
    sin128 = jnp.concatenate([-sin, sin, -sin, sin], axis=1)
    return jnp.tile(cos128, (batch, 1)), jnp.tile(sin128, (batch, 1))


def _block_diagonal(w):
    eye = jnp.eye(LRU_BLOCKS, dtype=w.dtype)
    return (w[:, :, None, :] * eye[:, None, :, None]).reshape(LRU_W, LRU_W)


def _diagonal_blocks(dense):
    d4 = dense.reshape(LRU_BLOCKS, LRU_BLOCK, LRU_BLOCKS, LRU_BLOCK)
    return jnp.stack([d4[n, :, n, :] for n in range(LRU_BLOCKS)])


def _row(v):
    return v.reshape(1, -1)


def _local_step(x, loss_target, meta_tokens, small, w_in, w_out, w_up4, w_down):
    batch, seq, _ = x.shape
    depth = len(w_in)
    n_real = N_META + seq
    n_blocks = -(-n_real // TIME_BLOCK)
    per_example = n_blocks * TIME_BLOCK
    pad = per_example - n_real
    meta = jnp.broadcast_to(meta_tokens[None], (batch, N_META, D_MODEL))
    h = jnp.concatenate([meta, x, jnp.zeros((batch, pad, D_MODEL), F32)], axis=1).reshape(batch * per_example, D_MODEL)
    target = jnp.pad(loss_target, ((0, 0), (N_META, pad), (0, 0))).reshape(batch * per_example, D_MODEL)
    rope_cos, rope_sin = _rope_tables(batch, per_example)

    saved = []
    for l in range(depth):
        wa = _block_diagonal(small['gate_a_w'][l]).astype(BF16)
        wx = _block_diagonal(small['gate_x_w'][l]).astype(BF16)
        lru_small = (small['conv_w'][l], _row(small['conv_b'][l]), wa, _row(small['gate_a_b'][l]), wx,
                     _row(small['gate_x_b'][l]), _row(small['lru_lambda'][l]))
        z1, q, kv, xg = _in_proj_fwd(h, _row(small['pre_mix_norm'][l]), w_in[l], rope_cos, rope_sin)
        attn, lse = _attn_fwd(q, kv, small['attn_sinks'][l], batch, n_blocks)
        hs, lru = _lru_fwd(xg, *lru_small, batch, n_blocks)
        groups, o, h_mid = _out_proj_fwd(attn, lru, _row(small['attn_group_norm'][l]), _row(small['lru_group_norm'][l]),
                                         w_out[l], h, _row(small['post_mix_norm'][l]))
        z2, up, act, y, h_out = _mlp_fwd(h_mid, _row(small['pre_mlp_norm'][l]), w_up4[l], w_down[l],
                                         _row(small['post_mlp_norm'][l]))
        saved.append(dict(h=h, z1=z1, q=q, kv=kv, xg=xg, attn=attn, lse=lse, hs=hs, lru=lru, groups=groups, o=o,
                          h_mid=h_mid, z2=z2, up=up, act=act, y=y, lru_small=lru_small))
        h = h_out

    dh, sq_err = _loss_head(h, target, batch, n_blocks, n_real)

    gs = {n: [None] * depth for n in REPLICATED_NAMES + ('conv_w',)}
    big = {n: [None] * depth for n in BIG_NAMES}
    for l in reversed(range(depth)):
        s = saved[l]
        dh_mid, dy, dup, g_mlp = _mlp_bwd(dh, s['h_mid'], s['y'], s['up'], w_up4[l], w_down[l],
                                          _row(small['pre_mlp_norm'][l]), _row(small['post_mlp_norm'][l]))
        big['w_down'][l] = _matmul_tn(s['act'], dy, 512, D_MODEL, "grad_w_down")
        big['w_up'][l] = _matmul_tn(s['z2'], dup, 512, D_MODEL, "grad_w_up", column_blocks=True)
        do, d_attn, d_lru, g_out = _out_proj_bwd(dh_mid, s['o'], s['attn'], s['lru'], _row(small['attn_group_norm'][l]),
                                                 _row(small['lru_group_norm'][l]), _row(small['post_mix_norm'][l]), w_out[l])
        big['w_out'][l] = _matmul_tn(s['groups'], do, 512, D_MODEL, "grad_w_out")
        dxg, dwa, dwx, g_lru = _lru_bwd(s['xg'], s['hs'], d_lru, *s['lru_small'], batch, n_blocks)
        dq, dkv, dsink = _attn_bwd(s['q'], s['kv'], small['attn_sinks'][l], s['attn'], s['lse'], d_attn, batch, n_blocks)
        dh, dproj, g_in = _in_proj_bwd(dh_mid, s['h'], dq, dkv, dxg, rope_cos, rope_sin, _row(small['pre_mix_norm'][l]),
                                       w_in[l])
        big['w_in'][l] = _matmul_tn(s['z1'], dproj, 512, IN_COLS // 2, "grad_w_in")
        gs['pre_mlp_norm'][l], gs['post_mlp_norm'][l] = g_mlp[0], g_mlp[1]
        gs['post_mix_norm'][l] = g_out[0]
        gs['attn_group_norm'][l], gs['lru_group_norm'][l] = g_out[1, :ATTN_W], g_out[1, ATTN_W:]
        gs['pre_mix_norm'][l] = g_in[0]
        gs['attn_sinks'][l] = dsink[:, 0]
        gs['conv_w'][l] = g_lru[:CONV_TAPS]
        gs['conv_b'][l], gs['gate_a_b'][l], gs['gate_x_b'][l], gs['lru_lambda'][l] = g_lru[4], g_lru[5], g_lru[6], g_lru[7]
        gs['gate_a_w'][l] = _diagonal_blocks(dwa)
        gs['gate_x_w'][l] = _diagonal_blocks(dwx)

    grad_x = dh.reshape(batch, per_example, D_MODEL)[:, N_META:n_real]
    grad_meta = _meta_grad(dh, batch, per_example)
    small_grads = {n: jnp.stack(v) for n, v in gs.items()}
    return sq_err, grad_x, grad_meta, small_grads, big


def _pack(arrays):
    flat = jnp.concatenate([a.reshape(-1) for a in arrays])
    tile = SUBLANES * LANES
    padded = -(-flat.shape[0] // tile) * tile
    return jnp.pad(flat, (0, padded - flat.shape[0])).reshape(-1, LANES)


def _unpack(buf, shapes):
    flat = buf.reshape(-1)
    out, at = [], 0
    for shp in shapes:
        size = 1
        for d in shp:
            size *= d
        out.append(flat[at:at + size].reshape(shp))
        at += size
    return out


def kernel(x, meta_tokens, pre_mix_norm, w_in, attn_sinks, conv_w, conv_b, gate_a_w, gate_a_b, gate_x_w, gate_x_b, lru_lambda, attn_group_norm, lru_group_norm, w_out, post_mix_norm, pre_mlp_norm, w_up, w_down, post_mlp_norm, loss_target, m_meta_tokens, m_pre_mix_norm, m_w_in, m_attn_sinks, m_conv_w, m_conv_b, m_gate_a_w, m_gate_a_b, m_gate_x_w, m_gate_x_b, m_lru_lambda, m_attn_group_norm, m_lru_group_norm, m_w_out, m_post_mix_norm, m_pre_mlp_norm, m_w_up, m_w_down, m_post_mlp_norm, v_meta_tokens, v_pre_mix_norm, v_w_in, v_attn_sinks, v_conv_w, v_conv_b, v_gate_a_w, v_gate_a_b, v_gate_x_w, v_gate_x_b, v_lru_lambda, v_attn_group_norm, v_lru_group_norm, v_w_out, v_post_mix_norm, v_pre_mlp_norm, v_w_up, v_w_down, v_post_mlp_norm):
    given = dict(locals())
    w = {n: given[n] for n in WEIGHT_NAMES}
    m = {n: given['m_' + n] for n in WEIGHT_NAMES}
    v = {n: given['v_' + n] for n in WEIGHT_NAMES}
    depth = w_in.shape[0]
    chip = 2 * lax.axis_index("x") + lax.axis_index("y")

    shards = []
    for l in range(depth):
        shards += [w_in[l].astype(BF16), w_out[l].astype(BF16), w_up[l].astype(BF16), w_down[l].astype(BF16)]
    shards += [meta_tokens, conv_w]
    gathered = _exchange_chips(shards, "gather_weights", scatter=False)
    full_w_in = [jnp.concatenate([gathered[4 * l][s] for s in range(N_CHIPS)], axis=1) for l in range(depth)]
    full_w_out = [gathered[4 * l + 1].reshape(D_MODEL, D_MODEL) for l in range(depth)]
    full_w_up4 = [gathered[4 * l + 2] for l in range(depth)]
    full_w_down = [gathered[4 * l + 3].reshape(D_FF, D_MODEL) for l in range(depth)]
    full_meta = jnp.concatenate([gathered[-2][s] for s in range(N_CHIPS)], axis=1)
    small = {n: w[n] for n in REPLICATED_NAMES}
    small['conv_w'] = jnp.concatenate([gathered[-1][s] for s in range(N_CHIPS)], axis=2)

    sq_err, grad_x, grad_meta, small_grads, big = _local_step(x, loss_target, full_meta, small, full_w_in, full_w_out,
                                                            full_w_up4, full_w_down)
    loss = lax.psum(sq_err[0, 0] * (0.5 / D_MODEL), ("x", "y", "c"))

    pieces = []
    for l in range(depth):
        g_in = big['w_in'][l].reshape(D_MODEL, N_CHIPS, IN_COLS // N_CHIPS).transpose(1, 0, 2)
        pieces += [g_in.astype(BF16), big['w_out'][l].reshape(N_CHIPS, D_MODEL // N_CHIPS, D_MODEL).astype(BF16),
                   big['w_up'][l].astype(BF16), big['w_down'][l].reshape(N_CHIPS, D_MODEL, D_MODEL).astype(BF16)]
    received = _exchange_chips(pieces, "scatter_grads", scatter=True)
    partial = [_sum_slots(r.reshape(N_CHIPS, -1, r.shape[-1]), "sum_grad_pieces") for r in received]
    sibling = _swap_sibling(partial, "swap_partial_grads")

    small_names = list(REPLICATED_NAMES) + list(COLUMN_SHARDED_SMALL)
    small_full = dict(small_grads)
    small_full['meta_tokens'] = grad_meta
    packed = _pack([small_full[n] for n in small_names])
    slots = _gather_all_devices(packed, "gather_small_grads")
    summed = _unpack(_sum_slots(slots, "sum_small_grads"), [small_full[n].shape for n in small_names])
    grads = dict(zip(small_names, summed))
    grads['meta_tokens'] = lax.dynamic_slice_in_dim(grads['meta_tokens'], chip * meta_tokens.shape[1], meta_tokens.shape[1], 1)
    grads['conv_w'] = lax.dynamic_slice_in_dim(grads['conv_w'], chip * conv_w.shape[2], conv_w.shape[2], 2)

    out_g, out_d, out_m, out_v = {}, {}, {}, {}
    for bi, n in enumerate(BIG_NAMES):
        cols = w[n].shape[-1]
        res = [_adamw(w[n][l].reshape(-1, cols), m[n][l].reshape(-1, cols), v[n][l].reshape(-1, cols),
                      [partial[4 * l + bi], sibling[4 * l + bi]], "adamw_" + n) for l in range(depth)]
        for k, store in enumerate((out_g, out_d, out_m, out_v)):
            store[n] = jnp.stack([r[k] for r in res]).reshape(w[n].shape)
    shapes = [w[n].shape for n in small_names]
    res = _adamw(_pack([w[n] for n in small_names]), _pack([m[n] for n in small_names]), _pack([v[n] for n in small_names]),
                 [_pack([grads[n] for n in small_names])], "adamw_small")
    for k, store in enumerate((out_g, out_d, out_m, out_v)):
        for n, a in zip(small_names, _unpack(res[k], shapes)):
            store[n] = a

    return (loss, grad_x, *[out_g[n] for n in WEIGHT_NAMES], *[out_d[n] for n in WEIGHT_NAMES],
            *[out_m[n] for n in WEIGHT_NAMES], *[out_v[n] for n in WEIGHT_NAMES])
```

```python
import functools

import jax
import jax.numpy as jnp
from jax import lax
from jax.experimental import pallas as pl
from jax.experimental.pallas import tpu as pltpu

F32 = jnp.float32
BF16 = jnp.bfloat16

D_MODEL = 1024
N_HEADS = 8
ATTN_W = 512
KV_W = 128
LRU_W = 512
LRU_BLOCKS = 8
LRU_BLOCK = 64
IN_COLS = 1792
D_FF = 4096
N_META = 16
CONV_TAPS = 4
LRU_C = 8.0
ROPE_THETA = 10000.0
EPS = 1e-6
ATTN_SCALE = 0.125

ADAM_LR = 0.001
ADAM_B1 = 0.9
ADAM_B2 = 0.999
ADAM_EPS = 1e-08
ADAM_WD = 0.01
ADAM_STEP = 10

N_CHIPS = 4
N_DEV = 8
TIME_BLOCK = 128
ROW_TILE = 256
LANES = 128
SUBLANES = 8
MASKED = -1e30
VMEM_LIMIT = 56 * 1024 * 1024

MESH = pl.DeviceIdType.MESH
HBM_SPEC = pl.BlockSpec(memory_space=pltpu.HBM)

WEIGHT_NAMES = ['meta_tokens', 'pre_mix_norm', 'w_in', 'attn_sinks', 'conv_w', 'conv_b', 'gate_a_w', 'gate_a_b',
                'gate_x_w', 'gate_x_b', 'lru_lambda', 'attn_group_norm', 'lru_group_norm', 'w_out', 'post_mix_norm',
                'pre_mlp_norm', 'w_up', 'w_down', 'post_mlp_norm']
BIG_NAMES = ('w_in', 'w_out', 'w_up', 'w_down')
COLUMN_SHARDED_SMALL = ('meta_tokens', 'conv_w')
REPLICATED_NAMES = tuple(n for n in WEIGHT_NAMES if n not in BIG_NAMES and n not in COLUMN_SHARDED_SMALL)


def _sds(shape, dtype):
    return jax.ShapeDtypeStruct(tuple(shape), dtype)


def _params(*sem):
    return pltpu.CompilerParams(dimension_semantics=sem, vmem_limit_bytes=VMEM_LIMIT)


def _row_spec(width, tile=ROW_TILE):
    return pl.BlockSpec((tile, width), lambda i: (i, 0))


def _whole_spec(a):
    nd = a.ndim
    return pl.BlockSpec(a.shape, lambda *_: (0,) * nd)


def _rms(x, g):
    r = lax.rsqrt(jnp.mean(x * x, axis=-1, keepdims=True) + EPS)
    return x * r * g


def _rms_bwd(dy, x, g):
    r = lax.rsqrt(jnp.mean(x * x, axis=-1, keepdims=True) + EPS)
    xh = x * r
    dg = jnp.sum(dy * xh, axis=0, keepdims=True)
    dxh = dy * g
    dx = r * (dxh - xh * jnp.mean(dxh * xh, axis=-1, keepdims=True))
    return dx, dg


def _rope(x, cos, sin_signed):
    width = x.shape[1]
    reps = width // LANES
    if reps > 1:
        cos = jnp.tile(cos, (1, reps))
        sin_signed = jnp.tile(sin_signed, (1, reps))
    lane = lax.broadcasted_iota(jnp.int32, x.shape, 1)
    first_half = jnp.bitwise_and(lane, 32) == 0
    other = jnp.where(first_half, pltpu.roll(x, width - 32, 1), pltpu.roll(x, 32, 1))
    return x * cos + other * sin_signed


def _sigmoid(x):
    return 1.0 / (1.0 + jnp.exp(-x))


def _log1p(e):
    return jnp.where(e < 1e-3, e * (1.0 - e * (0.5 - e * (1.0 / 3.0))), jnp.log(1.0 + e))


def _one_minus_exp(x):
    series = -x * (1.0 + x * (0.5 + x * ((1.0 / 6.0) + x * (1.0 / 24.0))))
    return jnp.where(x > -0.05, series, 1.0 - jnp.exp(x))


GELU_K = 0.7978845608028654
GELU_C = 0.044715


def _gelu(x):
    t = jnp.tanh(GELU_K * (x + GELU_C * x * x * x))
    return 0.5 * x * (1.0 + t)


def _gelu_and_grad(x):
    x2 = x * x
    t = jnp.tanh(GELU_K * (x + GELU_C * x * x2))
    val = 0.5 * x * (1.0 + t)
    grad = 0.5 * (1.0 + t) + 0.5 * x * (1.0 - t * t) * GELU_K * (1.0 + 3.0 * GELU_C * x2)
    return val, grad


def _dot(a, b):
    return jnp.dot(a, b, preferred_element_type=F32)


def _dot_nt(a, b):
    return lax.dot_general(a, b, (((1,), (1,)), ((), ())), preferred_element_type=F32)


def _dot_tn(a, b):
    return lax.dot_general(a, b, (((0,), (0,)), ((), ())), preferred_element_type=F32)


def _put_rows(rows_8, values):
    d = values[0].shape[1]
    rowid = lax.broadcasted_iota(jnp.int32, (rows_8, d), 0)
    out = jnp.zeros((rows_8, d), F32)
    for k, v in enumerate(values):
        out = out + jnp.where(rowid == k, v, 0.0)
    return out


def _in_proj_fwd(h, gain, w_in, rope_cos, rope_sin):
    rows = h.shape[0]

    def body(h_ref, g_ref, w_ref, c_ref, s_ref, z_ref, q_ref, kv_ref, xg_ref):
        z = _rms(h_ref[...], g_ref[...]).astype(BF16)
        z_ref[...] = z
        proj = _dot(z, w_ref[...])
        cos = c_ref[...]
        sin = s_ref[...]
        q_ref[...] = _rope(proj[:, :ATTN_W], cos, sin).astype(BF16)
        kv_ref[:, :KV_W] = _rope(proj[:, ATTN_W:ATTN_W + KV_W], cos, sin).astype(BF16)
        kv_ref[:, KV_W:] = proj[:, ATTN_W + KV_W:ATTN_W + 2 * KV_W].astype(BF16)
        xg_ref[...] = proj[:, ATTN_W + 2 * KV_W:]

    return pl.pallas_call(
        body, name="in_proj_fwd", grid=(rows // ROW_TILE,),
        in_specs=[_row_spec(D_MODEL), _whole_spec(gain), _whole_spec(w_in), _row_spec(LANES), _row_spec(LANES)],
        out_specs=[_row_spec(D_MODEL), _row_spec(ATTN_W), _row_spec(2 * KV_W), _row_spec(2 * LRU_W)],
        out_shape=[_sds((rows, D_MODEL), BF16), _sds((rows, ATTN_W), BF16), _sds((rows, 2 * KV_W), BF16),
                   _sds((rows, 2 * LRU_W), F32)],
        compiler_params=_params("parallel"),
    )(h, gain, w_in, rope_cos, rope_sin)


def _in_proj_bwd(dh_mid, h, dq, dkv, dxg, rope_cos, rope_sin, gain, w_in):
    rows = h.shape[0]

    def body(dhm_ref, h_ref, dq_ref, dkv_ref, dxg_ref, c_ref, s_ref, g_ref, w_ref, dh_ref, dp_ref, gacc_ref):
        @pl.when(pl.program_id(0) == 0)
        def _():
            gacc_ref[...] = jnp.zeros_like(gacc_ref)

        cos = c_ref[...]
        sin = -s_ref[...]
        dp_ref[:, :ATTN_W] = _rope(dq_ref[...], cos, sin).astype(BF16)
        dp_ref[:, ATTN_W:ATTN_W + KV_W] = _rope(dkv_ref[:, :KV_W], cos, sin).astype(BF16)
        dp_ref[:, ATTN_W + KV_W:ATTN_W + 2 * KV_W] = dkv_ref[:, KV_W:].astype(BF16)
        dp_ref[:, ATTN_W + 2 * KV_W:] = dxg_ref[...].astype(BF16)
        dz = _dot_nt(dp_ref[...], w_ref[...])
        dx, dg = _rms_bwd(dz, h_ref[...], g_ref[...])
        dh_ref[...] = dhm_ref[...] + dx
        gacc_ref[...] += _put_rows(SUBLANES, [dg])

    return pl.pallas_call(
        body, name="in_proj_bwd", grid=(rows // ROW_TILE,),
        in_specs=[_row_spec(D_MODEL), _row_spec(D_MODEL), _row_spec(ATTN_W), _row_spec(2 * KV_W), _row_spec(2 * LRU_W),
                  _row_spec(LANES), _row_spec(LANES), _whole_spec(gain), _whole_spec(w_in)],
        out_specs=[_row_spec(D_MODEL), _row_spec(IN_COLS), pl.BlockSpec((SUBLANES, D_MODEL), lambda i: (0, 0))],
        out_shape=[_sds((rows, D_MODEL), F32), _sds((rows, IN_COLS), BF16), _sds((SUBLANES, D_MODEL), F32)],
        compiler_params=_params("arbitrary"),
    )(dh_mid, h, dq, dkv, dxg, rope_cos, rope_sin, gain, w_in)


def _kv_lane_variants(t, group):
    lane = lax.broadcasted_iota(jnp.int32, t.shape, 1)
    low = lane < 64
    swapped = pltpu.roll(t, 64, 1)
    if group == 0:
        lo, hi = jnp.where(low, t, 0.0), jnp.where(low, 0.0, swapped)
    else:
        lo, hi = jnp.where(low, swapped, 0.0), jnp.where(low, 0.0, t)
    return lo.astype(BF16), hi.astype(BF16)


def _window_mask(j):
    r = lax.broadcasted_iota(jnp.int32, (TIME_BLOCK, 2 * TIME_BLOCK), 0)
    c = lax.broadcasted_iota(jnp.int32, (TIME_BLOCK, 2 * TIME_BLOCK), 1)
    return (c > r) & (c <= r + TIME_BLOCK) & ((c >= TIME_BLOCK) | (j > 0))


def _attn_fwd(q, kv, sinks, batch, n_blocks):
    rows = q.shape[0]

    def body(sink_ref, q_ref, kvc_ref, kvp_ref, o_ref, lse_ref):
        j = pl.program_id(1)
        kv2 = jnp.concatenate([kvp_ref[...], kvc_ref[...]], axis=0).astype(F32)
        mask = _window_mask(j)
        lane8 = lax.broadcasted_iota(jnp.int32, (TIME_BLOCK, N_HEADS), 1)
        lse_tile = jnp.zeros((TIME_BLOCK, N_HEADS), F32)
        for group in range(2):
            k_lo, k_hi = _kv_lane_variants(kv2[:, :KV_W], group)
            v_lo, v_hi = _kv_lane_variants(kv2[:, KV_W:], group)
            for p in range(2):
                pair = 2 * group + p
                q_pair = q_ref[:, LANES * pair:LANES * (pair + 1)]
                acc = None
                for half, (kk, vv) in enumerate(((k_lo, v_lo), (k_hi, v_hi))):
                    head = 2 * pair + half
                    sink = sink_ref[head]
                    s = jnp.where(mask, _dot_nt(q_pair, kk) * ATTN_SCALE, MASKED)
                    m = jnp.maximum(jnp.max(s, axis=1, keepdims=True), sink)
                    e = jnp.exp(s - m)
                    den = jnp.sum(e, axis=1, keepdims=True) + jnp.exp(sink - m)
                    part = _dot((e / den).astype(BF16), vv)
                    acc = part if acc is None else acc + part
                    lse_tile = lse_tile + jnp.where(lane8 == head, m + jnp.log(den), 0.0)
                o_ref[:, LANES * pair:LANES * (pair + 1)] = acc
        lse_ref[...] = lse_tile

    def blk(width):
        return pl.BlockSpec((TIME_BLOCK, width), lambda b, j: (b * n_blocks + j, 0))

    prev = pl.BlockSpec((TIME_BLOCK, 2 * KV_W), lambda b, j: (b * n_blocks + jnp.maximum(j - 1, 0), 0))
    return pl.pallas_call(
        body, name="attn_fwd", grid=(batch, n_blocks),
        in_specs=[pl.BlockSpec(memory_space=pltpu.SMEM), blk(ATTN_W), blk(2 * KV_W), prev],
        out_specs=[blk(ATTN_W), blk(N_HEADS)],
        out_shape=[_sds((rows, ATTN_W), F32), _sds((rows, N_HEADS), F32)],
        compiler_params=_params("parallel", "parallel"),
    )(sinks, q, kv, kv)


def _attn_bwd(q, kv, sinks, out, lse, d_out, batch, n_blocks):
    rows = q.shape[0]

    def body(sink_ref, q_ref, kvc_ref, kvp_ref, o_ref, do_ref, lse_ref, dq_ref, dkv_ref, dsink_ref, carry):
        b = pl.program_id(0)
        j = pl.program_id(1)

        @pl.when((b == 0) & (j == 0))
        def _():
            dsink_ref[...] = jnp.zeros_like(dsink_ref)

        @pl.when(j < n_blocks)
        def _():
            kv2 = jnp.concatenate([kvp_ref[...], kvc_ref[...]], axis=0).astype(F32)
            mask = _window_mask(j)
            lane = lax.broadcasted_iota(jnp.int32, (TIME_BLOCK, LANES), 1)
            lane2 = lax.broadcasted_iota(jnp.int32, (2 * TIME_BLOCK, LANES), 1)
            low2 = lane2 < 64
            lse_tile = lse_ref[...]
            dk_tile = jnp.zeros((2 * TIME_BLOCK, KV_W), F32)
            dv_tile = jnp.zeros((2 * TIME_BLOCK, KV_W), F32)
            dsink_vals = []
            for group in range(2):
                k_lo, k_hi = _kv_lane_variants(kv2[:, :KV_W], group)
                v_lo, v_hi = _kv_lane_variants(kv2[:, KV_W:], group)
                dk_acc = [jnp.zeros((2 * TIME_BLOCK, LANES), F32) for _ in range(2)]
                dv_acc = [jnp.zeros((2 * TIME_BLOCK, LANES), F32) for _ in range(2)]
                for p in range(2):
                    pair = 2 * group + p
                    cols = slice(LANES * pair, LANES * (pair + 1))
                    q_pair = q_ref[:, cols]
                    do_pair = do_ref[:, cols]
                    do_b = do_pair.astype(BF16)
                    od = do_pair * o_ref[:, cols]
                    dq_acc = jnp.zeros((TIME_BLOCK, LANES), F32)
                    for half, (kk, vv) in enumerate(((k_lo, v_lo), (k_hi, v_hi))):
                        head = 2 * pair + half
                        sink = sink_ref[head]
                        lse_h = lse_tile[:, head:head + 1]
                        in_half = (lane < 64) if half == 0 else (lane >= 64)
                        delta = jnp.sum(jnp.where(in_half, od, 0.0), axis=1, keepdims=True)
                        s = jnp.where(mask, _dot_nt(q_pair, kk) * ATTN_SCALE, MASKED)
                        prob = jnp.exp(s - lse_h)
                        dp = _dot_nt(do_b, vv)
                        ds = (prob * (dp - delta) * ATTN_SCALE).astype(BF16)
                        dq_acc = dq_acc + _dot(ds, kk)
                        dk_acc[half] = dk_acc[half] + _dot_tn(ds, q_pair)
                        dv_acc[half] = dv_acc[half] + _dot_tn(prob.astype(BF16), do_b)
                        dsink_vals.append(jnp.sum(-jnp.exp(sink - lse_h) * delta, axis=0, keepdims=True))
                    dq_ref[:, cols] = dq_acc
                if group == 0:
                    dk_tile = dk_tile + jnp.where(low2, dk_acc[0] + pltpu.roll(dk_acc[1], 64, 1), 0.0)
                    dv_tile = dv_tile + jnp.where(low2, dv_acc[0] + pltpu.roll(dv_acc[1], 64, 1), 0.0)
                else:
                    dk_tile = dk_tile + jnp.where(low2, 0.0, pltpu.roll(dk_acc[0], 64, 1) + dk_acc[1])
                    dv_tile = dv_tile + jnp.where(low2, 0.0, pltpu.roll(dv_acc[0], 64, 1) + dv_acc[1])

            @pl.when(j > 0)
            def _():
                dkv_ref[:, :KV_W] = carry[:, :KV_W] + dk_tile[:TIME_BLOCK]
                dkv_ref[:, KV_W:] = carry[:, KV_W:] + dv_tile[:TIME_BLOCK]

            carry[:, :KV_W] = dk_tile[TIME_BLOCK:]
            carry[:, KV_W:] = dv_tile[TIME_BLOCK:]
            rowid = lax.broadcasted_iota(jnp.int32, (N_HEADS, LANES), 0)
            upd = jnp.zeros((N_HEADS, LANES), F32)
            for head, val in enumerate(dsink_vals):
                upd = upd + jnp.where(rowid == head, val, 0.0)
            dsink_ref[...] += upd

        @pl.when(j == n_blocks)
        def _():
            dkv_ref[...] = carry[...]

    last = n_blocks - 1

    def blk(width):
        return pl.BlockSpec((TIME_BLOCK, width), lambda b, j: (b * n_blocks + jnp.minimum(j, last), 0))

    prev = pl.BlockSpec((TIME_BLOCK, 2 * KV_W), lambda b, j: (b * n_blocks + jnp.maximum(jnp.minimum(j, last) - 1, 0), 0))
    dkv_spec = pl.BlockSpec((TIME_BLOCK, 2 * KV_W), lambda b, j: (b * n_blocks + jnp.maximum(j - 1, 0), 0))
    return pl.pallas_call(
        body, name="attn_bwd", grid=(batch, n_blocks + 1),
        in_specs=[pl.BlockSpec(memory_space=pltpu.SMEM), blk(ATTN_W), blk(2 * KV_W), prev, blk(ATTN_W), blk(ATTN_W),
                  blk(N_HEADS)],
        out_specs=[blk(ATTN_W), dkv_spec, pl.BlockSpec((N_HEADS, LANES), lambda b, j: (0, 0))],
        out_shape=[_sds((rows, ATTN_W), F32), _sds((rows, 2 * KV_W), F32), _sds((N_HEADS, LANES), F32)],
        scratch_shapes=[pltpu.VMEM((TIME_BLOCK, 2 * KV_W), F32)],
        compiler_params=_params("arbitrary", "arbitrary"),
    )(sinks, q, kv, kv, out, d_out, lse)


def _conv_taps(xb, prev8):
    ext = jnp.concatenate([prev8, xb], axis=0)
    n = ext.shape[0]
    return [xb] + [pltpu.roll(ext, k, 0)[SUBLANES:n] for k in range(1, CONV_TAPS)]


def _lru_gates(xc, wa, ba, wx, bx, lam):
    xcb = xc.astype(BF16)
    r = _sigmoid(_dot(xcb, wa) + ba)
    i = _sigmoid(_dot(xcb, wx) + bx)
    sp = jnp.maximum(-lam, 0.0) + _log1p(jnp.exp(-jnp.abs(lam)))
    log_a = -LRU_C * r * sp
    a = jnp.exp(log_a)
    mult = jnp.sqrt(_one_minus_exp(2.0 * log_a))
    return xcb, r, i, sp, a, mult


def _scan_fwd(a, u):
    n = a.shape[0]
    row = lax.broadcasted_iota(jnp.int32, a.shape, 0)
    s = 1
    while s < n:
        valid = row >= s
        u = jnp.where(valid, u + a * pltpu.roll(u, s, 0), u)
        a = jnp.where(valid, a * pltpu.roll(a, s, 0), a)
        s *= 2
    return a, u


def _scan_rev(cf, g):
    n = g.shape[0]
    row = lax.broadcasted_iota(jnp.int32, g.shape, 0)
    s = 1
    while s < n:
        valid = row + s < n
        g = jnp.where(valid, g + cf * pltpu.roll(g, n - s, 0), g)
        cf = jnp.where(valid, cf * pltpu.roll(cf, n - s, 0), cf)
        s *= 2
    return g


def _lru_fwd(xg, conv_w, conv_b, wa, ba, wx, bx, lam, batch, n_blocks):
    rows = xg.shape[0]

    def body(xg_ref, cw_ref, cb_ref, wa_ref, ba_ref, wx_ref, bx_ref, lam_ref, hs_ref, lru_ref, x_prev, h_carry):
        @pl.when(pl.program_id(1) == 0)
        def _():
            x_prev[...] = jnp.zeros_like(x_prev)
            h_carry[...] = jnp.zeros_like(h_carry)

        xb = xg_ref[:, :LRU_W]
        taps = _conv_taps(xb, x_prev[...])
        xc = cb_ref[...] + sum(cw_ref[CONV_TAPS - 1 - k:CONV_TAPS - k, :] * taps[k] for k in range(CONV_TAPS))
        _, _, i, _, a, mult = _lru_gates(xc, wa_ref[...], ba_ref[...], wx_ref[...], bx_ref[...], lam_ref[...])
        a_cum, h = _scan_fwd(a, mult * i * xc)
        h = h + a_cum * h_carry[...]
        hs_ref[...] = h
        lru_ref[...] = h * _gelu(xg_ref[:, LRU_W:])
        h_carry[...] = h[TIME_BLOCK - 1:TIME_BLOCK, :]
        x_prev[...] = xb[TIME_BLOCK - SUBLANES:TIME_BLOCK, :]

    def blk(width):
        return pl.BlockSpec((TIME_BLOCK, width), lambda b, j: (b * n_blocks + j, 0))

    small = [conv_w, conv_b, wa, ba, wx, bx, lam]
    return pl.pallas_call(
        body, name="lru_fwd", grid=(batch, n_blocks),
        in_specs=[blk(2 * LRU_W)] + [_whole_spec(a) for a in small],
        out_specs=[blk(LRU_W), blk(LRU_W)],
        out_shape=[_sds((rows, LRU_W), F32), _sds((rows, LRU_W), F32)],
        scratch_shapes=[pltpu.VMEM((SUBLANES, LRU_W), F32), pltpu.VMEM((1, LRU_W), F32)],
        compiler_params=_params("arbitrary", "arbitrary"),
    )(xg, *small)


def _lru_bwd(xg, hs, d_lru, conv_w, conv_b, wa, ba, wx, bx, lam, batch, n_blocks):
    rows = xg.shape[0]
    last_row = TIME_BLOCK - 1

    def body(xg_ref, xgh_ref, hs_ref, hsh_ref, dl_ref, cw_ref, cb_ref, wa_ref, ba_ref, wx_ref, bx_ref, lam_ref,
             dxg_ref, dwa_ref, dwx_ref, vec_ref, dh_carry, dxc_next):
        b = pl.program_id(0)
        j = pl.program_id(1)

        @pl.when((b == 0) & (j == 0))
        def _():
            dwa_ref[...] = jnp.zeros_like(dwa_ref)
            dwx_ref[...] = jnp.zeros_like(dwx_ref)
            vec_ref[...] = jnp.zeros_like(vec_ref)

        @pl.when(j == 0)
        def _():
            dh_carry[...] = jnp.zeros_like(dh_carry)
            dxc_next[...] = jnp.zeros_like(dxc_next)

        first = j == n_blocks - 1
        xb = xg_ref[:, :LRU_W]
        prev8 = jnp.where(first, 0.0, xgh_ref[:, :LRU_W])
        h_before = jnp.where(first, 0.0, hsh_ref[SUBLANES - 1:SUBLANES, :])
        cw = cw_ref[...]
        lam = lam_ref[...]
        wa = wa_ref[...]
        wx = wx_ref[...]
        taps = _conv_taps(xb, prev8)
        xc = cb_ref[...] + sum(cw[CONV_TAPS - 1 - k:CONV_TAPS - k, :] * taps[k] for k in range(CONV_TAPS))
        xcb, r, i, sp, a, mult = _lru_gates(xc, wa, ba_ref[...], wx, bx_ref[...], lam)
        hs = hs_ref[...]
        row = lax.broadcasted_iota(jnp.int32, hs.shape, 0)
        h_prev = jnp.where(row == 0, h_before, pltpu.roll(hs, 1, 0))
        dl = dl_ref[...]
        gate, dgate = _gelu_and_grad(xg_ref[:, LRU_W:])
        dxg_ref[:, LRU_W:] = dl * hs * dgate
        g = dl * gate + jnp.where(row == last_row, dh_carry[...], 0.0)
        cf = jnp.where(row == last_row, 0.0, pltpu.roll(a, last_row, 0))
        dh = _scan_rev(cf, g)
        dh_carry[...] = a[0:1, :] * dh[0:1, :]
        dmult = dh * i * xc
        di = dh * mult * xc
        dxc = dh * mult * i
        dlog_a = dh * h_prev * a - dmult * (a * a / mult)
        dr = dlog_a * (-LRU_C * sp)
        dlam = jnp.sum(dlog_a * (-LRU_C * r), axis=0, keepdims=True) * (-_sigmoid(-lam))
        dpr = dr * r * (1.0 - r)
        dpi = di * i * (1.0 - i)
        dprb = dpr.astype(BF16)
        dpib = dpi.astype(BF16)
        dxc = dxc + _dot_nt(dprb, wa) + _dot_nt(dpib, wx)
        dwa_ref[...] += _dot_tn(xcb, dprb)
        dwx_ref[...] += _dot_tn(xcb, dpib)
        ext = jnp.concatenate([dxc, dxc_next[...]], axis=0)
        n = ext.shape[0]
        dxb = cw[CONV_TAPS - 1:CONV_TAPS, :] * dxc
        for k in range(1, CONV_TAPS):
            dxb = dxb + cw[CONV_TAPS - 1 - k:CONV_TAPS - k, :] * pltpu.roll(ext, n - k, 0)[:TIME_BLOCK]
        dxg_ref[:, :LRU_W] = dxb
        dxc_next[...] = dxc[:SUBLANES, :]
        vecs = [jnp.sum(dxc * taps[CONV_TAPS - 1 - t], axis=0, keepdims=True) for t in range(CONV_TAPS)]
        vecs += [jnp.sum(dxc, axis=0, keepdims=True), jnp.sum(dpr, axis=0, keepdims=True),
                 jnp.sum(dpi, axis=0, keepdims=True), dlam]
        vec_ref[...] += _put_rows(SUBLANES, vecs)

    def tblk(b, j):
        return b * n_blocks + n_blocks - 1 - j

    def blk(width):
        return pl.BlockSpec((TIME_BLOCK, width), lambda b, j: (tblk(b, j), 0))

    per8 = TIME_BLOCK // SUBLANES

    def halo(width):
        return pl.BlockSpec((SUBLANES, width), lambda b, j: (jnp.maximum(per8 * tblk(b, j) - 1, 0), 0))

    small = [conv_w, conv_b, wa, ba, wx, bx, lam]
    acc = lambda shape: pl.BlockSpec(shape, lambda b, j: (0, 0))
    return pl.pallas_call(
        body, name="lru_bwd", grid=(batch, n_blocks),
        in_specs=[blk(2 * LRU_W), halo(2 * LRU_W), blk(LRU_W), halo(LRU_W), blk(LRU_W)] + [_whole_spec(a) for a in small],
        out_specs=[blk(2 * LRU_W), acc((LRU_W, LRU_W)), acc((LRU_W, LRU_W)), acc((SUBLANES, LRU_W))],
        out_shape=[_sds((rows, 2 * LRU_W), F32), _sds((LRU_W, LRU_W), F32), _sds((LRU_W, LRU_W), F32),
                   _sds((SUBLANES, LRU_W), F32)],
        scratch_shapes=[pltpu.VMEM((1, LRU_W), F32), pltpu.VMEM((SUBLANES, LRU_W), F32)],
        compiler_params=_params("arbitrary", "arbitrary"),
    )(xg, xg, hs, hs, d_lru, *small)


def _out_proj_fwd(attn, lru, g_attn, g_lru, w_out, h, g_post):
    rows = h.shape[0]

    def body(at_ref, lr_ref, ga_ref, gl_ref, w_ref, h_ref, gp_ref, grp_ref, o_ref, hm_ref):
        a = _rms(at_ref[...], ga_ref[...]).astype(BF16)
        l = _rms(lr_ref[...], gl_ref[...]).astype(BF16)
        grp_ref[:, :ATTN_W] = a
        grp_ref[:, ATTN_W:] = l
        o = _dot(a, w_ref[:ATTN_W, :]) + _dot(l, w_ref[ATTN_W:, :])
        o_ref[...] = o
        hm_ref[...] = h_ref[...] + _rms(o, gp_ref[...])

    return pl.pallas_call(
        body, name="out_proj_fwd", grid=(rows // ROW_TILE,),
        in_specs=[_row_spec(ATTN_W), _row_spec(LRU_W), _whole_spec(g_attn), _whole_spec(g_lru), _whole_spec(w_out),
                  _row_spec(D_MODEL), _whole_spec(g_post)],
        out_specs=[_row_spec(D_MODEL), _row_spec(D_MODEL), _row_spec(D_MODEL)],
        out_shape=[_sds((rows, D_MODEL), BF16), _sds((rows, D_MODEL), F32), _sds((rows, D_MODEL), F32)],
        compiler_params=_params("parallel"),
    )(attn, lru, g_attn, g_lru, w_out, h, g_post)


def _out_proj_bwd(dh_mid, o, attn, lru, g_attn, g_lru, g_post, w_out):
    rows = o.shape[0]

    def body(dhm_ref, o_ref, at_ref, lr_ref, ga_ref, gl_ref, gp_ref, w_ref, do_ref, dat_ref, dlr_ref, gacc_ref):
        @pl.when(pl.program_id(0) == 0)
        def _():
            gacc_ref[...] = jnp.zeros_like(gacc_ref)

        do, dgp = _rms_bwd(dhm_ref[...], o_ref[...], gp_ref[...])
        dob = do.astype(BF16)
        do_ref[...] = dob
        dat, dga = _rms_bwd(_dot_nt(dob, w_ref[:ATTN_W, :]), at_ref[...], ga_ref[...])
        dlr, dgl = _rms_bwd(_dot_nt(dob, w_ref[ATTN_W:, :]), lr_ref[...], gl_ref[...])
        dat_ref[...] = dat
        dlr_ref[...] = dlr
        gacc_ref[...] += _put_rows(SUBLANES, [dgp, jnp.concatenate([dga, dgl], axis=1)])

    return pl.pallas_call(
        body, name="out_proj_bwd", grid=(rows // ROW_TILE,),
        in_specs=[_row_spec(D_MODEL), _row_spec(D_MODEL), _row_spec(ATTN_W), _row_spec(LRU_W), _whole_spec(g_attn),
                  _whole_spec(g_lru), _whole_spec(g_post), _whole_spec(w_out)],
        out_specs=[_row_spec(D_MODEL), _row_spec(ATTN_W), _row_spec(LRU_W),
                   pl.BlockSpec((SUBLANES, D_MODEL), lambda i: (0, 0))],
        out_shape=[_sds((rows, D_MODEL), BF16), _sds((rows, ATTN_W), F32), _sds((rows, LRU_W), F32),
                   _sds((SUBLANES, D_MODEL), F32)],
        compiler_params=_params("arbitrary"),
    )(dh_mid, o, attn, lru, g_attn, g_lru, g_post, w_out)


def _mlp_fwd(h_mid, g_pre, w_up4, w_down, g_post):
    rows = h_mid.shape[0]

    def body(h_ref, g1_ref, wu_ref, wd_ref, g2_ref, z_ref, up_ref, act_ref, y_ref, ho_ref):
        x = h_ref[...]
        z = _rms(x, g1_ref[...]).astype(BF16)
        z_ref[...] = z
        y = jnp.zeros((ROW_TILE, D_MODEL), F32)
        for s in range(N_CHIPS):
            cols = slice(s * D_MODEL, (s + 1) * D_MODEL)
            u = _dot(z, wu_ref[s])
            up_ref[:, cols] = u
            a = jnp.square(jnp.maximum(u, 0.0)).astype(BF16)
            act_ref[:, cols] = a
            y = y + _dot(a, wd_ref[cols, :])
        y_ref[...] = y
        ho_ref[...] = x + _rms(y, g2_ref[...])

    return pl.pallas_call(
        body, name="mlp_fwd", grid=(rows // ROW_TILE,),
        in_specs=[_row_spec(D_MODEL), _whole_spec(g_pre), _whole_spec(w_up4), _whole_spec(w_down), _whole_spec(g_post)],
        out_specs=[_row_spec(D_MODEL), _row_spec(D_FF), _row_spec(D_FF), _row_spec(D_MODEL), _row_spec(D_MODEL)],
        out_shape=[_sds((rows, D_MODEL), BF16), _sds((rows, D_FF), F32), _sds((rows, D_FF), BF16),
                   _sds((rows, D_MODEL), F32), _sds((rows, D_MODEL), F32)],
        compiler_params=_params("parallel"),
    )(h_mid, g_pre, w_up4, w_down, g_post)


def _mlp_bwd(dh_out, h_mid, y, up, w_up4, w_down, g_pre, g_post):
    rows = y.shape[0]

    def body(dh_ref, hm_ref, y_ref, up_ref, wu_ref, wd_ref, g1_ref, g2_ref, dhm_ref, dy_ref, dup_ref, gacc_ref):
        @pl.when(pl.program_id(0) == 0)
        def _():
            gacc_ref[...] = jnp.zeros_like(gacc_ref)

        dh = dh_ref[...]
        dy, dg2 = _rms_bwd(dh, y_ref[...], g2_ref[...])
        dyb = dy.astype(BF16)
        dy_ref[...] = dyb
        dz = jnp.zeros((ROW_TILE, D_MODEL), F32)
        for s in range(N_CHIPS):
            cols = slice(s * D_MODEL, (s + 1) * D_MODEL)
            dact = _dot_nt(dyb, wd_ref[cols, :])
            dup = (dact * (2.0 * jnp.maximum(up_ref[:, cols], 0.0))).astype(BF16)
            dup_ref[:, cols] = dup
            dz = dz + _dot_nt(dup, wu_ref[s])
        dx, dg1 = _rms_bwd(dz, hm_ref[...], g1_ref[...])
        dhm_ref[...] = dh + dx
        gacc_ref[...] += _put_rows(SUBLANES, [dg1, dg2])

    return pl.pallas_call(
        body, name="mlp_bwd", grid=(rows // ROW_TILE,),
        in_specs=[_row_spec(D_MODEL), _row_spec(D_MODEL), _row_spec(D_MODEL), _row_spec(D_FF), _whole_spec(w_up4),
                  _whole_spec(w_down), _whole_spec(g_pre), _whole_spec(g_post)],
        out_specs=[_row_spec(D_MODEL), _row_spec(D_MODEL), _row_spec(D_FF),
                   pl.BlockSpec((SUBLANES, D_MODEL), lambda i: (0, 0))],
        out_shape=[_sds((rows, D_MODEL), F32), _sds((rows, D_MODEL), BF16), _sds((rows, D_FF), BF16),
                   _sds((SUBLANES, D_MODEL), F32)],
        compiler_params=_params("arbitrary"),
    )(dh_out, h_mid, y, up, w_up4, w_down, g_pre, g_post)


def _matmul_tn(a, b, tm, tn, name, out_dtype, column_blocks=False):
    rows, m = a.shape
    n = b.shape[1]

    def body(a_ref, b_ref, o_ref):
        o_ref[...] = _dot_tn(a_ref[...], b_ref[...]).astype(out_dtype)

    if column_blocks:
        out_spec = pl.BlockSpec((None, tm, tn), lambda i, j: (j, i, 0))
        out_shape = _sds((n // tn, m, tn), out_dtype)
    else:
        out_spec = pl.BlockSpec((tm, tn), lambda i, j: (i, j))
        out_shape = _sds((m, n), out_dtype)
    return pl.pallas_call(
        body, name=name, grid=(m // tm, n // tn),
        in_specs=[pl.BlockSpec((rows, tm), lambda i, j: (0, i)), pl.BlockSpec((rows, tn), lambda i, j: (0, j))],
        out_specs=out_spec, out_shape=out_shape,
        compiler_params=_params("parallel", "parallel"),
    )(a, b)


def _loss_head(h, target_padded, batch, n_blocks, n_real):
    rows = h.shape[0]

    def body(h_ref, t_ref, dh_ref, l_ref):
        j = pl.program_id(1)

        @pl.when((pl.program_id(0) == 0) & (j == 0))
        def _():
            l_ref[...] = jnp.zeros_like(l_ref)

        pos = j * TIME_BLOCK + lax.broadcasted_iota(jnp.int32, (TIME_BLOCK, D_MODEL), 0)
        d = jnp.where((pos >= N_META) & (pos < n_real), h_ref[...] - t_ref[...], 0.0)
        dh_ref[...] = d * (1.0 / D_MODEL)
        l_ref[...] += jnp.sum(jnp.sum(d * d, axis=0, keepdims=True), axis=1, keepdims=True)

    blk = pl.BlockSpec((TIME_BLOCK, D_MODEL), lambda b, j: (b * n_blocks + j, 0))
    return pl.pallas_call(
        body, name="loss_head", grid=(batch, n_blocks),
        in_specs=[blk, blk], out_specs=[blk, pl.BlockSpec((SUBLANES, LANES), lambda b, j: (0, 0))],
        out_shape=[_sds((rows, D_MODEL), F32), _sds((SUBLANES, LANES), F32)],
        compiler_params=_params("arbitrary", "arbitrary"),
    )(h, target_padded)


def _meta_grad(dh0, batch, rows_per_example):
    per = rows_per_example // N_META

    def body(d_ref, o_ref):
        @pl.when(pl.program_id(0) == 0)
        def _():
            o_ref[...] = jnp.zeros_like(o_ref)

        o_ref[...] += d_ref[...]

    return pl.pallas_call(
        body, name="meta_grad", grid=(batch,),
        in_specs=[pl.BlockSpec((N_META, D_MODEL), lambda b: (b * per, 0))],
        out_specs=pl.BlockSpec((N_META, D_MODEL), lambda b: (0, 0)),
        out_shape=_sds((N_META, D_MODEL), F32),
        compiler_params=_params("arbitrary"),
    )(dh0)


def _elementwise_tile(rows, cols):
    tile = rows
    while tile * cols * 4 > (1 << 20) and tile % 16 == 0:
        tile //= 2
    return tile


def _sum_slots(buf, name):
    k, rows, cols = buf.shape
    tile = _elementwise_tile(rows, cols)

    def body(*refs):
        total = refs[0][...].astype(F32)
        for r in refs[1:k]:
            total = total + r[...].astype(F32)
        refs[k][...] = total

    def slot(s):
        return pl.BlockSpec((None, tile, cols), lambda i: (s, i, 0))

    return pl.pallas_call(
        body, name=name, grid=(rows // tile,),
        in_specs=[slot(s) for s in range(k)], out_specs=pl.BlockSpec((tile, cols), lambda i: (i, 0)),
        out_shape=_sds((rows, cols), F32), compiler_params=_params("parallel"),
    )(*([buf] * k))


def _sum_pieces(pieces, landed, chip):
    _, rows, cols = pieces.shape
    tile = _elementwise_tile(rows, cols)

    def body(chip_ref, own_ref, a_ref, b_ref, c_ref, o_ref):
        o_ref[...] = ((own_ref[...].astype(F32) + a_ref[...].astype(F32)) + b_ref[...].astype(F32)) + c_ref[...].astype(F32)

    def slot(offset):
        return pl.BlockSpec((None, tile, cols), lambda i, chip_ref: ((chip_ref[0] + offset) % N_CHIPS, i, 0))

    grid_spec = pltpu.PrefetchScalarGridSpec(
        num_scalar_prefetch=1, grid=(rows // tile,), in_specs=[slot(0), slot(1), slot(2), slot(3)],
        out_specs=pl.BlockSpec((tile, cols), lambda i, chip_ref: (i, 0)))
    return pl.pallas_call(
        body, name="sum_grad_pieces", grid_spec=grid_spec, out_shape=_sds((rows, cols), F32),
        compiler_params=_params("parallel"),
    )(chip, pieces, landed, landed, landed)


def _adamw(w, m, v, grads, name, layer=None, prev=None):
    rows, cols = grads[0].shape
    tile = _elementwise_tile(rows, cols)
    ng = len(grads)
    m_scale = 1.0 - ADAM_B1 ** ADAM_STEP
    v_scale = 1.0 - ADAM_B2 ** ADAM_STEP

    def body(*refs):
        w_ref, m_ref, v_ref = refs[:3]
        g_refs = refs[3:3 + ng]
        g_out, d_out, m_out, v_out = refs[-4:]
        g = g_refs[0][...]
        for r in g_refs[1:]:
            g = g + r[...]
        m_new = ADAM_B1 * m_ref[...] + (1.0 - ADAM_B1) * g
        v_new = ADAM_B2 * v_ref[...] + (1.0 - ADAM_B2) * (g * g)
        m_hat = m_new / m_scale
        v_hat = v_new / v_scale
        g_out[...] = g
        d_out[...] = -ADAM_LR * (m_hat / (jnp.sqrt(v_hat) + ADAM_EPS) + ADAM_WD * w_ref[...])
        m_out[...] = m_new
        v_out[...] = v_new

    spec = pl.BlockSpec((tile, cols), lambda i: (i, 0))
    if layer is None:
        return pl.pallas_call(
            body, name=name, grid=(rows // tile,),
            in_specs=[spec] * (3 + ng), out_specs=[spec] * 4, out_shape=[_sds((rows, cols), F32)] * 4,
            compiler_params=_params("parallel"),
        )(w, m, v, *grads)
    of_layer = pl.BlockSpec((None, tile, cols), lambda i: (layer, i, 0))
    carried = [] if prev is None else list(prev)
    return pl.pallas_call(
        body, name=name, grid=(rows // tile,),
        in_specs=[of_layer] * 3 + [spec] * ng + [pl.BlockSpec(memory_space=pl.ANY)] * len(carried),
        out_specs=[of_layer] * 4, out_shape=[_sds(w.shape, F32)] * 4,
        input_output_aliases={3 + ng + k: k for k in range(len(carried))},
        compiler_params=_params("parallel"),
    )(w, m, v, *grads, *carried)


def _position():
    return lax.axis_index("x"), lax.axis_index("y"), lax.axis_index("c")


def _other_chips(x, y):
    return [(1 - x, y), (x, 1 - y), (1 - x, 1 - y)]


def _exchange_chips(arrays, name, scatter):
    n = len(arrays)

    def body(*refs):
        src, dst = refs[:n], refs[n:2 * n]
        send_sems, recv_sems, local_sems = refs[2 * n:]
        x, y, c = _position()
        mine = 2 * x + y
        copies = []
        for i in range(n):
            own = src[i].at[mine] if scatter else src[i]
            copies.append(pltpu.make_async_copy(own, dst[i].at[mine], local_sems.at[i]))
        for k, (px, py) in enumerate(_other_chips(x, y)):
            for i in range(n):
                piece = src[i].at[2 * px + py] if scatter else src[i]
                copies.append(pltpu.make_async_remote_copy(
                    src_ref=piece, dst_ref=dst[i].at[mine], send_sem=send_sems.at[k, i], recv_sem=recv_sems.at[k, i],
                    device_id=(px, py, c), device_id_type=MESH))
        for cp in copies:
            cp.start()
        for cp in copies:
            cp.wait()

    def out_shape(a):
        return _sds(a.shape if scatter else (N_CHIPS,) + a.shape, a.dtype)

    return pl.pallas_call(
        body, name=name, in_specs=[HBM_SPEC] * n, out_specs=[HBM_SPEC] * n, out_shape=[out_shape(a) for a in arrays],
        scratch_shapes=[pltpu.SemaphoreType.DMA((N_CHIPS - 1, n)), pltpu.SemaphoreType.DMA((N_CHIPS - 1, n)),
                        pltpu.SemaphoreType.DMA((n,))],
    )(*arrays)


def _swap_sibling(arrays, name):
    n = len(arrays)

    def body(*refs):
        src, dst = refs[:n], refs[n:2 * n]
        send_sems, recv_sems = refs[2 * n:]
        x, y, c = _position()
        copies = [pltpu.make_async_remote_copy(
            src_ref=src[i], dst_ref=dst[i], send_sem=send_sems.at[i], recv_sem=recv_sems.at[i],
            device_id=(x, y, 1 - c), device_id_type=MESH) for i in range(n)]
        for cp in copies:
            cp.start()
        for cp in copies:
            cp.wait()

    return pl.pallas_call(
        body, name=name, in_specs=[HBM_SPEC] * n, out_specs=[HBM_SPEC] * n,
        out_shape=[_sds(a.shape, a.dtype) for a in arrays],
        scratch_shapes=[pltpu.SemaphoreType.DMA((n,)), pltpu.SemaphoreType.DMA((n,))],
    )(*arrays)


def _gather_all_devices(a, name):
    def body(src, dst, send_sems, recv_sems, local_sem):
        x, y, c = _position()
        mine = 4 * x + 2 * y + c
        copies = [pltpu.make_async_copy(src, dst.at[mine], local_sem)]
        k = 0
        for fx in range(2):
            for fy in range(2):
                for fc in range(2):
                    if fx or fy or fc:
                        peer = (1 - x if fx else x, 1 - y if fy else y, 1 - c if fc else c)
                        copies.append(pltpu.make_async_remote_copy(
                            src_ref=src, dst_ref=dst.at[mine], send_sem=send_sems.at[k], recv_sem=recv_sems.at[k],
                            device_id=peer, device_id_type=MESH))
                        k += 1
        for cp in copies:
            cp.start()
        for cp in copies:
            cp.wait()

    return pl.pallas_call(
        body, name=name, in_specs=[HBM_SPEC], out_specs=HBM_SPEC, out_shape=_sds((N_DEV,) + a.shape, a.dtype),
        scratch_shapes=[pltpu.SemaphoreType.DMA((N_DEV - 1,)), pltpu.SemaphoreType.DMA((N_DEV - 1,)),
                        pltpu.SemaphoreType.DMA(())],
    )(a)


SEM_SPEC = pl.BlockSpec(memory_space=pltpu.SEMAPHORE)
ANY_SPEC = pl.BlockSpec(memory_space=pl.ANY)
IN_FLIGHT = pltpu.SideEffectType.DATAFLOW_SIDE_EFFECTING
PEER_SEMS = pltpu.SemaphoreType.DMA((N_CHIPS - 1,))


def _in_hbm(a):
    return pltpu.with_memory_space_constraint(a, pltpu.HBM)


def _place_shard(w, layer, chip, name):
    _, rows, cols = w.shape
    tile = _elementwise_tile(rows, cols)

    def body(chip_ref, w_ref, o_ref):
        o_ref[...] = w_ref[...].astype(BF16)

    grid_spec = pltpu.PrefetchScalarGridSpec(
        num_scalar_prefetch=1, grid=(rows // tile,),
        in_specs=[pl.BlockSpec((None, tile, cols), lambda i, chip_ref: (layer, i, 0))],
        out_specs=pl.BlockSpec((None, tile, cols), lambda i, chip_ref: (chip_ref[0], i, 0)))
    return pl.pallas_call(
        body, name=name, grid_spec=grid_spec, out_shape=_sds((N_CHIPS, rows, cols), BF16),
        compiler_params=_params("parallel"),
    )(chip, w)


def _gather_start(bufs, name):
    n = len(bufs)

    def body(*refs):
        buf = refs[:n]
        send, recv = refs[n:2 * n], refs[2 * n:3 * n]
        token = refs[4 * n]
        x, y, c = _position()
        mine = 2 * x + y
        for i in range(n):
            for k, (px, py) in enumerate(_other_chips(x, y)):
                pltpu.make_async_remote_copy(
                    src_ref=buf[i].at[mine], dst_ref=buf[i].at[mine], send_sem=send[i].at[k], recv_sem=recv[i].at[k],
                    device_id=(px, py, c), device_id_type=MESH).start()
        token[...] = jnp.zeros_like(token)

    out = pl.pallas_call(
        body, name=name, in_specs=[HBM_SPEC] * n,
        out_specs=[SEM_SPEC] * (2 * n) + [HBM_SPEC] * n + [pl.BlockSpec(memory_space=pltpu.VMEM)],
        out_shape=[PEER_SEMS] * (2 * n) + [pltpu.HBM(b.shape, b.dtype) for b in bufs] + [_sds((SUBLANES, LANES), F32)],
        input_output_aliases={i: 2 * n + i for i in range(n)},
        compiler_params=pltpu.CompilerParams(has_side_effects=IN_FLIGHT),
    )(*[_in_hbm(b) for b in bufs])
    return [(out[i], out[n + i], out[2 * n + i]) for i in range(n)], out[3 * n]


def _gather_wait(send, recv, buf, after, name):
    def body(buf_ref, send_ref, recv_ref, after_ref, out_ref):
        x, y, c = _position()
        mine = 2 * x + y
        for k, (px, py) in enumerate(_other_chips(x, y)):
            cp = pltpu.make_async_remote_copy(
                src_ref=buf_ref.at[mine], dst_ref=buf_ref.at[2 * px + py], send_sem=send_ref.at[k], recv_sem=recv_ref.at[k],
                device_id=(px, py, c), device_id_type=MESH)
            cp.wait_send()
            cp.wait_recv()

    return pl.pallas_call(
        body, name=name, in_specs=[HBM_SPEC, SEM_SPEC, SEM_SPEC, ANY_SPEC], out_specs=HBM_SPEC,
        out_shape=pltpu.HBM(buf.shape, buf.dtype), input_output_aliases={0: 0},
        compiler_params=pltpu.CompilerParams(has_side_effects=IN_FLIGHT),
    )(buf, send, recv, after)


def _scatter_start(pieces, name):
    n = len(pieces)

    def body(*refs):
        src = refs[:n]
        send, recv = refs[n:2 * n], refs[2 * n:3 * n]
        land = refs[4 * n:5 * n]
        x, y, c = _position()
        mine = 2 * x + y
        for i in range(n):
            for k, (px, py) in enumerate(_other_chips(x, y)):
                pltpu.make_async_remote_copy(
                    src_ref=src[i].at[2 * px + py], dst_ref=land[i].at[mine], send_sem=send[i].at[k], recv_sem=recv[i].at[k],
                    device_id=(px, py, c), device_id_type=MESH).start()

    hbm = [pltpu.HBM(p.shape, p.dtype) for p in pieces]
    out = pl.pallas_call(
        body, name=name, in_specs=[HBM_SPEC] * n,
        out_specs=[SEM_SPEC] * (2 * n) + [HBM_SPEC] * (2 * n),
        out_shape=[PEER_SEMS] * (2 * n) + hbm + hbm,
        input_output_aliases={i: 2 * n + i for i in range(n)},
        compiler_params=pltpu.CompilerParams(has_side_effects=IN_FLIGHT),
    )(*[_in_hbm(p) for p in pieces])
    return [(out[i], out[n + i], out[2 * n + i], out[3 * n + i]) for i in range(n)]


def _scatter_wait(send, recv, pieces, land, after, name):
    def body(src_ref, land_ref, send_ref, recv_ref, after_ref, src_out, land_out):
        x, y, c = _position()
        for k, (px, py) in enumerate(_other_chips(x, y)):
            cp = pltpu.make_async_remote_copy(
                src_ref=src_ref.at[2 * px + py], dst_ref=land_ref.at[2 * px + py], send_sem=send_ref.at[k],
                recv_sem=recv_ref.at[k], device_id=(px, py, c), device_id_type=MESH)
            cp.wait_send()
            cp.wait_recv()

    return pl.pallas_call(
        body, name=name, in_specs=[HBM_SPEC, HBM_SPEC, SEM_SPEC, SEM_SPEC, ANY_SPEC], out_specs=[HBM_SPEC, HBM_SPEC],
        out_shape=[pltpu.HBM(pieces.shape, pieces.dtype), pltpu.HBM(land.shape, land.dtype)],
        input_output_aliases={0: 0, 1: 1},
        compiler_params=pltpu.CompilerParams(has_side_effects=IN_FLIGHT),
    )(pieces, land, send, recv, after)


def _rope_tables(batch, rows_per_example):
    inv_freq = ROPE_THETA ** (-jnp.arange(0, 64, 2, dtype=F32) / 64)
    ang = jnp.arange(rows_per_example, dtype=F32)[:, None] * inv_freq[None, :]
    cos, sin = jnp.cos(ang), jnp.sin(ang)
    cos128 = jnp.concatenate([cos, cos, cos, cos], axis=1)
    sin128 = jnp.concatenate([-sin, sin, -sin, sin], axis=1)
    return jnp.tile(cos128, (batch, 1)), jnp.tile(sin128, (batch, 1))


def _block_diagonal(w):
    eye = jnp.eye(LRU_BLOCKS, dtype=w.dtype)
    return (w[:, :, None, :] * eye[:, None, :, None]).reshape(LRU_W, LRU_W)


def _diagonal_blocks(dense):
    d4 = dense.reshape(LRU_BLOCKS, LRU_BLOCK, LRU_BLOCKS, LRU_BLOCK)
    return jnp.stack([d4[n, :, n, :] for n in range(LRU_BLOCKS)])


def _row(v):
    return v.reshape(1, -1)


def _local_step(x, loss_target, meta_tokens, small, depth, big_weight, on_layer_grads):
    batch, seq, _ = x.shape
    n_real = N_META + seq
    n_blocks = -(-n_real // TIME_BLOCK)
    per_example = n_blocks * TIME_BLOCK
    pad = per_example - n_real
    meta = jnp.broadcast_to(meta_tokens[None], (batch, N_META, D_MODEL))
    h = jnp.concatenate([meta, x, jnp.zeros((batch, pad, D_MODEL), F32)], axis=1).reshape(batch * per_example, D_MODEL)
    target = jnp.pad(loss_target, ((0, 0), (N_META, pad), (0, 0))).reshape(batch * per_example, D_MODEL)
    rope_cos, rope_sin = _rope_tables(batch, per_example)

    saved = []
    for l in range(depth):
        wa = _block_diagonal(small['gate_a_w'][l]).astype(BF16)
        wx = _block_diagonal(small['gate_x_w'][l]).astype(BF16)
        lru_small = (small['conv_w'][l], _row(small['conv_b'][l]), wa, _row(small['gate_a_b'][l]), wx,
                     _row(small['gate_x_b'][l]), _row(small['lru_lambda'][l]))
        w_in = big_weight('w_in', l, h)
        z1, q, kv, xg = _in_proj_fwd(h, _row(small['pre_mix_norm'][l]), w_in, rope_cos, rope_sin)
        attn, lse = _attn_fwd(q, kv, small['attn_sinks'][l], batch, n_blocks)
        hs, lru = _lru_fwd(xg, *lru_small, batch, n_blocks)
        w_out = big_weight('w_out', l, xg)
        groups, o, h_mid = _out_proj_fwd(attn, lru, _row(small['attn_group_norm'][l]), _row(small['lru_group_norm'][l]),
                                         w_out, h, _row(small['post_mix_norm'][l]))
        w_up4 = big_weight('w_up', l, hs)
        w_down = big_weight('w_down', l, hs)
        z2, up, act, y, h_out = _mlp_fwd(h_mid, _row(small['pre_mlp_norm'][l]), w_up4, w_down,
                                         _row(small['post_mlp_norm'][l]))
        saved.append(dict(h=h, z1=z1, q=q, kv=kv, xg=xg, attn=attn, lse=lse, hs=hs, lru=lru, groups=groups, o=o,
                          h_mid=h_mid, z2=z2, up=up, act=act, y=y, lru_small=lru_small, w_in=w_in, w_out=w_out,
                          w_up4=w_up4, w_down=w_down))
        h = h_out

    dh, sq_err = _loss_head(h, target, batch, n_blocks, n_real)

    gs = {n: [None] * depth for n in REPLICATED_NAMES + ('conv_w',)}
    for l in reversed(range(depth)):
        s = saved[l]
        big = {}
        dh_mid, dy, dup, g_mlp = _mlp_bwd(dh, s['h_mid'], s['y'], s['up'], s['w_up4'], s['w_down'],
                                          _row(small['pre_mlp_norm'][l]), _row(small['post_mlp_norm'][l]))
        big['w_down'] = _matmul_tn(s['act'], dy, 512, D_MODEL, "grad_w_down", BF16)
        big['w_up'] = _matmul_tn(s['z2'], dup, 512, D_MODEL, "grad_w_up", BF16, column_blocks=True)
        do, d_attn, d_lru, g_out = _out_proj_bwd(dh_mid, s['o'], s['attn'], s['lru'], _row(small['attn_group_norm'][l]),
                                                 _row(small['lru_group_norm'][l]), _row(small['post_mix_norm'][l]),
                                                 s['w_out'])
        big['w_out'] = _matmul_tn(s['groups'], do, 512, D_MODEL, "grad_w_out", BF16)
        dxg, dwa, dwx, g_lru = _lru_bwd(s['xg'], s['hs'], d_lru, *s['lru_small'], batch, n_blocks)
        dq, dkv, dsink = _attn_bwd(s['q'], s['kv'], small['attn_sinks'][l], s['attn'], s['lse'], d_attn, batch, n_blocks)
        dh, dproj, g_in = _in_proj_bwd(dh_mid, s['h'], dq, dkv, dxg, rope_cos, rope_sin, _row(small['pre_mix_norm'][l]),
                                       s['w_in'])
        big['w_in'] = _matmul_tn(s['z1'], dproj, 512, IN_COLS // 2, "grad_w_in", F32)
        on_layer_grads(l, big)
        gs['pre_mlp_norm'][l], gs['post_mlp_norm'][l] = g_mlp[0], g_mlp[1]
        gs['post_mix_norm'][l] = g_out[0]
        gs['attn_group_norm'][l], gs['lru_group_norm'][l] = g_out[1, :ATTN_W], g_out[1, ATTN_W:]
        gs['pre_mix_norm'][l] = g_in[0]
        gs['attn_sinks'][l] = dsink[:, 0]
        gs['conv_w'][l] = g_lru[:CONV_TAPS]
        gs['conv_b'][l], gs['gate_a_b'][l], gs['gate_x_b'][l], gs['lru_lambda'][l] = g_lru[4], g_lru[5], g_lru[6], g_lru[7]
        gs['gate_a_w'][l] = _diagonal_blocks(dwa)
        gs['gate_x_w'][l] = _diagonal_blocks(dwx)

    grad_x = dh.reshape(batch, per_example, D_MODEL)[:, N_META:n_real]
    grad_meta = _meta_grad(dh, batch, per_example)
    small_grads = {n: jnp.stack(v) for n, v in gs.items()}
    return sq_err, grad_x, grad_meta, small_grads


PACK_UNIT = SUBLANES * LANES


def _size(shape):
    size = 1
    for d in shape:
        size *= d
    return size


def _pack(arrays):
    parts = []
    for a in arrays:
        flat = a.reshape(-1)
        padded = -(-flat.shape[0] // PACK_UNIT) * PACK_UNIT
        if padded != flat.shape[0]:
            flat = jnp.pad(flat, (0, padded - flat.shape[0]))
        parts.append(flat.reshape(-1, LANES))
    return jnp.concatenate(parts, axis=0)


def _unpack(buf, shapes):
    out, at = [], 0
    for shp in shapes:
        size = _size(shp)
        rows = -(-size // PACK_UNIT) * SUBLANES
        part = buf[at:at + rows]
        if rows * LANES != size:
            part = part.reshape(-1)[:size]
        out.append(part.reshape(shp))
        at += rows
    return out


def kernel(x, meta_tokens, pre_mix_norm, w_in, attn_sinks, conv_w, conv_b, gate_a_w, gate_a_b, gate_x_w, gate_x_b, lru_lambda, attn_group_norm, lru_group_norm, w_out, post_mix_norm, pre_mlp_norm, w_up, w_down, post_mlp_norm, loss_target, m_meta_tokens, m_pre_mix_norm, m_w_in, m_attn_sinks, m_conv_w, m_conv_b, m_gate_a_w, m_gate_a_b, m_gate_x_w, m_gate_x_b, m_lru_lambda, m_attn_group_norm, m_lru_group_norm, m_w_out, m_post_mix_norm, m_pre_mlp_norm, m_w_up, m_w_down, m_post_mlp_norm, v_meta_tokens, v_pre_mix_norm, v_w_in, v_attn_sinks, v_conv_w, v_conv_b, v_gate_a_w, v_gate_a_b, v_gate_x_w, v_gate_x_b, v_lru_lambda, v_attn_group_norm, v_lru_group_norm, v_w_out, v_post_mix_norm, v_pre_mlp_norm, v_w_up, v_w_down, v_post_mlp_norm):
    given = dict(locals())
    w = {n: given[n] for n in WEIGHT_NAMES}
    m = {n: given['m_' + n] for n in WEIGHT_NAMES}
    v = {n: given['v_' + n] for n in WEIGHT_NAMES}
    depth = w_in.shape[0]
    chip = 2 * lax.axis_index("x") + lax.axis_index("y")
    chip1 = chip.reshape(1).astype(jnp.int32)

    in_flight, tokens = [], []
    for l in range(depth):
        bufs = [_place_shard(w[n], l, chip1, "place_" + n) for n in BIG_NAMES]
        handles, token = _gather_start(bufs, "gather_start_%d" % l)
        in_flight.append(dict(zip(BIG_NAMES, handles)))
        tokens.append(token)
    all_started = sum(tokens[1:], tokens[0])

    def big_weight(name, l, after):
        send, recv, buf = in_flight[l][name]
        if l == 0 and name == 'w_in':
            after = all_started
        full = _gather_wait(send, recv, buf, after, "gather_wait_%d_%s" % (l, name))
        if name == 'w_in':
            return jnp.concatenate([full[s] for s in range(N_CHIPS)], axis=1)
        if name == 'w_out':
            return full.reshape(D_MODEL, D_MODEL)
        if name == 'w_down':
            return full.reshape(D_FF, D_MODEL)
        return full

    gathered = _exchange_chips([meta_tokens, conv_w], "gather_small_weights", scatter=False)
    full_meta = jnp.concatenate([gathered[0][s] for s in range(N_CHIPS)], axis=1)
    small = {n: w[n] for n in REPLICATED_NAMES}
    small['conv_w'] = jnp.concatenate([gathered[1][s] for s in range(N_CHIPS)], axis=2)

    scattering = [None] * depth

    def on_layer_grads(l, big):
        g_in = big['w_in'].reshape(D_MODEL, N_CHIPS, IN_COLS // N_CHIPS).transpose(1, 0, 2).astype(BF16)
        pieces = [g_in, big['w_out'].reshape(N_CHIPS, D_MODEL // N_CHIPS, D_MODEL), big['w_up'],
                  big['w_down'].reshape(N_CHIPS, D_MODEL, D_MODEL)]
        scattering[l] = _scatter_start(pieces, "scatter_start_%d" % l)

    sq_err, grad_x, grad_meta, small_grads = _local_step(x, loss_target, full_meta, small, depth, big_weight, on_layer_grads)
    loss = lax.psum(sq_err[0, 0] * (0.5 / D_MODEL), ("x", "y", "c"))

    results = {n: None for n in BIG_NAMES}
    for l in reversed(range(depth)):
        partial = []
        for n, (send, recv, pieces, land) in zip(BIG_NAMES, scattering[l]):
            pieces, land = _scatter_wait(send, recv, pieces, land, grad_meta, "scatter_wait_%d_%s" % (l, n))
            partial.append(_sum_pieces(pieces, land, chip1))
        sibling = _swap_sibling(partial, "swap_partial_grads")
        for n, mine, other in zip(BIG_NAMES, partial, sibling):
            results[n] = _adamw(w[n], m[n], v[n], [mine, other], "adamw_" + n, layer=l, prev=results[n])

    small_names = list(REPLICATED_NAMES) + list(COLUMN_SHARDED_SMALL)
    small_full = dict(small_grads)
    small_full['meta_tokens'] = grad_meta
    packed = _pack([small_full[n] for n in small_names])
    slots = _gather_all_devices(packed, "gather_small_grads")
    summed = _unpack(_sum_slots(slots, "sum_small_grads"), [small_full[n].shape for n in small_names])
    grads = dict(zip(small_names, summed))
    grads['meta_tokens'] = lax.dynamic_slice_in_dim(grads['meta_tokens'], chip * meta_tokens.shape[1], meta_tokens.shape[1], 1)
    grads['conv_w'] = lax.dynamic_slice_in_dim(grads['conv_w'], chip * conv_w.shape[2], conv_w.shape[2], 2)

    out_g, out_d, out_m, out_v = {}, {}, {}, {}
    for n in BIG_NAMES:
        out_g[n], out_d[n], out_m[n], out_v[n] = results[n]
    shapes = [w[n].shape for n in small_names]
    res = _adamw(_pack([w[n] for n in small_names]), _pack([m[n] for n in small_names]), _pack([v[n] for n in small_names]),
                 [_pack([grads[n] for n in small_names])], "adamw_small")
    for k, store in enumerate((out_g, out_d, out_m, out_v)):
        for n, a in zip(small_names, _unpack(res[k], shapes)):
            store[n] = a

    return (loss, grad_x, *[out_g[n] for n in WEIGHT_NAMES], *[out_d[n] for n in WEIGHT_NAMES],
            *[out_m[n] for n in WEIGHT_NAMES], *[out_v[n] for n in WEIGHT_NAMES])
```

```python
import functools

import jax
import jax.numpy as jnp
from jax import lax
from jax.experimental import pallas as pl
from jax.experimental.pallas import tpu as pltpu

F32 = jnp.float32
BF16 = jnp.bfloat16

D_MODEL = 1024
N_HEADS = 8
ATTN_W = 512
KV_W = 128
LRU_W = 512
LRU_BLOCKS = 8
LRU_BLOCK = 64
IN_COLS = 1792
D_FF = 4096
N_META = 16
CONV_TAPS = 4
LRU_C = 8.0
ROPE_THETA = 10000.0
EPS = 1e-6
ATTN_SCALE = 0.125

ADAM_LR = 0.001
ADAM_B1 = 0.9
ADAM_B2 = 0.999
ADAM_EPS = 1e-08
ADAM_WD = 0.01
ADAM_STEP = 10

N_CHIPS = 4
N_DEV = 8
TIME_BLOCK = 128
ROW_TILE = 256
LANES = 128
SUBLANES = 8
MASKED = -1e30
VMEM_LIMIT = 56 * 1024 * 1024

MESH = pl.DeviceIdType.MESH
HBM_SPEC = pl.BlockSpec(memory_space=pltpu.HBM)

WEIGHT_NAMES = ['meta_tokens', 'pre_mix_norm', 'w_in', 'attn_sinks', 'conv_w', 'conv_b', 'gate_a_w', 'gate_a_b',
                'gate_x_w', 'gate_x_b', 'lru_lambda', 'attn_group_norm', 'lru_group_norm', 'w_out', 'post_mix_norm',
                'pre_mlp_norm', 'w_up', 'w_down', 'post_mlp_norm']
BIG_NAMES = ('w_in', 'w_out', 'w_up', 'w_down')
COLUMN_SHARDED_SMALL = ('meta_tokens', 'conv_w')
REPLICATED_NAMES = tuple(n for n in WEIGHT_NAMES if n not in BIG_NAMES and n not in COLUMN_SHARDED_SMALL)


def _sds(shape, dtype):
    return jax.ShapeDtypeStruct(tuple(shape), dtype)


def _params(*sem):
    return pltpu.CompilerParams(dimension_semantics=sem, vmem_limit_bytes=VMEM_LIMIT)


def _row_spec(width, tile=ROW_TILE):
    return pl.BlockSpec((tile, width), lambda i: (i, 0))


def _whole_spec(a):
    nd = a.ndim
    return pl.BlockSpec(a.shape, lambda *_: (0,) * nd)


def _rms(x, g):
    r = lax.rsqrt(jnp.mean(x * x, axis=-1, keepdims=True) + EPS)
    return x * r * g


def _rms_bwd(dy, x, g):
    r = lax.rsqrt(jnp.mean(x * x, axis=-1, keepdims=True) + EPS)
    xh = x * r
    dg = jnp.sum(dy * xh, axis=0, keepdims=True)
    dxh = dy * g
    dx = r * (dxh - xh * jnp.mean(dxh * xh, axis=-1, keepdims=True))
    return dx, dg


def _rope(x, cos, sin_signed):
    width = x.shape[1]
    reps = width // LANES
    if reps > 1:
        cos = jnp.tile(cos, (1, reps))
        sin_signed = jnp.tile(sin_signed, (1, reps))
    lane = lax.broadcasted_iota(jnp.int32, x.shape, 1)
    first_half = jnp.bitwise_and(lane, 32) == 0
    other = jnp.where(first_half, pltpu.roll(x, width - 32, 1), pltpu.roll(x, 32, 1))
    return x * cos + other * sin_signed


def _sigmoid(x):
    return 1.0 / (1.0 + jnp.exp(-x))


def _log1p(e):
    return jnp.where(e < 1e-3, e * (1.0 - e * (0.5 - e * (1.0 / 3.0))), jnp.log(1.0 + e))


def _one_minus_exp(x):
    series = -x * (1.0 + x * (0.5 + x * ((1.0 / 6.0) + x * (1.0 / 24.0))))
    return jnp.where(x > -0.05, series, 1.0 - jnp.exp(x))


GELU_K = 0.7978845608028654
GELU_C = 0.044715


def _gelu(x):
    t = jnp.tanh(GELU_K * (x + GELU_C * x * x * x))
    return 0.5 * x * (1.0 + t)


def _gelu_and_grad(x):
    x2 = x * x
    t = jnp.tanh(GELU_K * (x + GELU_C * x * x2))
    val = 0.5 * x * (1.0 + t)
    grad = 0.5 * (1.0 + t) + 0.5 * x * (1.0 - t * t) * GELU_K * (1.0 + 3.0 * GELU_C * x2)
    return val, grad


def _dot(a, b):
    return jnp.dot(a, b, preferred_element_type=F32)


def _dot_nt(a, b):
    return lax.dot_general(a, b, (((1,), (1,)), ((), ())), preferred_element_type=F32)


def _dot_tn(a, b):
    return lax.dot_general(a, b, (((0,), (0,)), ((), ())), preferred_element_type=F32)


def _put_rows(rows_8, values):
    d = values[0].shape[1]
    rowid = lax.broadcasted_iota(jnp.int32, (rows_8, d), 0)
    out = jnp.zeros((rows_8, d), F32)
    for k, v in enumerate(values):
        out = out + jnp.where(rowid == k, v, 0.0)
    return out


def _in_proj_fwd(h, gain, w_in, rope_cos, rope_sin):
    rows = h.shape[0]

    def body(h_ref, g_ref, w_ref, c_ref, s_ref, z_ref, q_ref, kv_ref, xg_ref):
        z = _rms(h_ref[...], g_ref[...]).astype(BF16)
        z_ref[...] = z
        proj = _dot(z, w_ref[...])
        cos = c_ref[...]
        sin = s_ref[...]
        q_ref[...] = _rope(proj[:, :ATTN_W], cos, sin).astype(BF16)
        kv_ref[:, :KV_W] = _rope(proj[:, ATTN_W:ATTN_W + KV_W], cos, sin).astype(BF16)
        kv_ref[:, KV_W:] = proj[:, ATTN_W + KV_W:ATTN_W + 2 * KV_W].astype(BF16)
        xg_ref[...] = proj[:, ATTN_W + 2 * KV_W:]

    return pl.pallas_call(
        body, name="in_proj_fwd", grid=(rows // ROW_TILE,),
        in_specs=[_row_spec(D_MODEL), _whole_spec(gain), _whole_spec(w_in), _row_spec(LANES), _row_spec(LANES)],
        out_specs=[_row_spec(D_MODEL), _row_spec(ATTN_W), _row_spec(2 * KV_W), _row_spec(2 * LRU_W)],
        out_shape=[_sds((rows, D_MODEL), BF16), _sds((rows, ATTN_W), BF16), _sds((rows, 2 * KV_W), BF16),
                   _sds((rows, 2 * LRU_W), F32)],
        compiler_params=_params("parallel"),
    )(h, gain, w_in, rope_cos, rope_sin)


def _in_proj_bwd(dh_mid, h, dq, dkv, dxg, rope_cos, rope_sin, gain, w_in):
    rows = h.shape[0]

    def body(dhm_ref, h_ref, dq_ref, dkv_ref, dxg_ref, c_ref, s_ref, g_ref, w_ref, dh_ref, dp_ref, gacc_ref):
        @pl.when(pl.program_id(0) == 0)
        def _():
            gacc_ref[...] = jnp.zeros_like(gacc_ref)

        cos = c_ref[...]
        sin = -s_ref[...]
        dp_ref[:, :ATTN_W] = _rope(dq_ref[...], cos, sin).astype(BF16)
        dp_ref[:, ATTN_W:ATTN_W + KV_W] = _rope(dkv_ref[:, :KV_W], cos, sin).astype(BF16)
        dp_ref[:, ATTN_W + KV_W:ATTN_W + 2 * KV_W] = dkv_ref[:, KV_W:].astype(BF16)
        dp_ref[:, ATTN_W + 2 * KV_W:] = dxg_ref[...].astype(BF16)
        dz = _dot_nt(dp_ref[...], w_ref[...])
        dx, dg = _rms_bwd(dz, h_ref[...], g_ref[...])
        dh_ref[...] = dhm_ref[...] + dx
        gacc_ref[...] += _put_rows(SUBLANES, [dg])

    return pl.pallas_call(
        body, name="in_proj_bwd", grid=(rows // ROW_TILE,),
        in_specs=[_row_spec(D_MODEL), _row_spec(D_MODEL), _row_spec(ATTN_W), _row_spec(2 * KV_W), _row_spec(2 * LRU_W),
                  _row_spec(LANES), _row_spec(LANES), _whole_spec(gain), _whole_spec(w_in)],
        out_specs=[_row_spec(D_MODEL), _row_spec(IN_COLS), pl.BlockSpec((SUBLANES, D_MODEL), lambda i: (0, 0))],
        out_shape=[_sds((rows, D_MODEL), F32), _sds((rows, IN_COLS), BF16), _sds((SUBLANES, D_MODEL), F32)],
        compiler_params=_params("arbitrary"),
    )(dh_mid, h, dq, dkv, dxg, rope_cos, rope_sin, gain, w_in)


def _kv_lane_variants(t, group):
    lane = lax.broadcasted_iota(jnp.int32, t.shape, 1)
    low = lane < 64
    swapped = pltpu.roll(t, 64, 1)
    if group == 0:
        lo, hi = jnp.where(low, t, 0.0), jnp.where(low, 0.0, swapped)
    else:
        lo, hi = jnp.where(low, swapped, 0.0), jnp.where(low, 0.0, t)
    return lo.astype(BF16), hi.astype(BF16)


def _window_mask(j):
    r = lax.broadcasted_iota(jnp.int32, (TIME_BLOCK, 2 * TIME_BLOCK), 0)
    c = lax.broadcasted_iota(jnp.int32, (TIME_BLOCK, 2 * TIME_BLOCK), 1)
    return (c > r) & (c <= r + TIME_BLOCK) & ((c >= TIME_BLOCK) | (j > 0))


def _attn_fwd(q, kv, sinks, batch, n_blocks):
    rows = q.shape[0]

    def body(sink_ref, q_ref, kvc_ref, kvp_ref, o_ref, lse_ref):
        j = pl.program_id(1)
        kv2 = jnp.concatenate([kvp_ref[...], kvc_ref[...]], axis=0).astype(F32)
        mask = _window_mask(j)
        lane8 = lax.broadcasted_iota(jnp.int32, (TIME_BLOCK, N_HEADS), 1)
        lse_tile = jnp.zeros((TIME_BLOCK, N_HEADS), F32)
        for group in range(2):
            k_lo, k_hi = _kv_lane_variants(kv2[:, :KV_W], group)
            v_lo, v_hi = _kv_lane_variants(kv2[:, KV_W:], group)
            for p in range(2):
                pair = 2 * group + p
                q_pair = q_ref[:, LANES * pair:LANES * (pair + 1)]
                acc = None
                for half, (kk, vv) in enumerate(((k_lo, v_lo), (k_hi, v_hi))):
                    head = 2 * pair + half
                    sink = sink_ref[head]
                    s = jnp.where(mask, _dot_nt(q_pair, kk) * ATTN_SCALE, MASKED)
                    m = jnp.maximum(jnp.max(s, axis=1, keepdims=True), sink)
                    e = jnp.exp(s - m)
                    den = jnp.sum(e, axis=1, keepdims=True) + jnp.exp(sink - m)
                    part = _dot((e / den).astype(BF16), vv)
                    acc = part if acc is None else acc + part
                    lse_tile = lse_tile + jnp.where(lane8 == head, m + jnp.log(den), 0.0)
                o_ref[:, LANES * pair:LANES * (pair + 1)] = acc
        lse_ref[...] = lse_tile

    def blk(width):
        return pl.BlockSpec((TIME_BLOCK, width), lambda b, j: (b * n_blocks + j, 0))

    prev = pl.BlockSpec((TIME_BLOCK, 2 * KV_W), lambda b, j: (b * n_blocks + jnp.maximum(j - 1, 0), 0))
    return pl.pallas_call(
        body, name="attn_fwd", grid=(batch, n_blocks),
        in_specs=[pl.BlockSpec(memory_space=pltpu.SMEM), blk(ATTN_W), blk(2 * KV_W), prev],
        out_specs=[blk(ATTN_W), blk(N_HEADS)],
        out_shape=[_sds((rows, ATTN_W), F32), _sds((rows, N_HEADS), F32)],
        compiler_params=_params("parallel", "parallel"),
    )(sinks, q, kv, kv)


def _attn_bwd(q, kv, sinks, out, lse, d_out, batch, n_blocks):
    rows = q.shape[0]

    def body(sink_ref, q_ref, kvc_ref, kvp_ref, o_ref, do_ref, lse_ref, dq_ref, dkv_ref, dsink_ref, carry):
        b = pl.program_id(0)
        j = pl.program_id(1)

        @pl.when((b == 0) & (j == 0))
        def _():
            dsink_ref[...] = jnp.zeros_like(dsink_ref)

        @pl.when(j < n_blocks)
        def _():
            kv2 = jnp.concatenate([kvp_ref[...], kvc_ref[...]], axis=0).astype(F32)
            mask = _window_mask(j)
            lane = lax.broadcasted_iota(jnp.int32, (TIME_BLOCK, LANES), 1)
            lane2 = lax.broadcasted_iota(jnp.int32, (2 * TIME_BLOCK, LANES), 1)
            low2 = lane2 < 64
            lse_tile = lse_ref[...]
            dk_tile = jnp.zeros((2 * TIME_BLOCK, KV_W), F32)
            dv_tile = jnp.zeros((2 * TIME_BLOCK, KV_W), F32)
            dsink_vals = []
            for group in range(2):
                k_lo, k_hi = _kv_lane_variants(kv2[:, :KV_W], group)
                v_lo, v_hi = _kv_lane_variants(kv2[:, KV_W:], group)
                dk_acc = [jnp.zeros((2 * TIME_BLOCK, LANES), F32) for _ in range(2)]
                dv_acc = [jnp.zeros((2 * TIME_BLOCK, LANES), F32) for _ in range(2)]
                for p in range(2):
                    pair = 2 * group + p
                    cols = slice(LANES * pair, LANES * (pair + 1))
                    q_pair = q_ref[:, cols]
                    do_pair = do_ref[:, cols]
                    do_b = do_pair.astype(BF16)
                    od = do_pair * o_ref[:, cols]
                    dq_acc = jnp.zeros((TIME_BLOCK, LANES), F32)
                    for half, (kk, vv) in enumerate(((k_lo, v_lo), (k_hi, v_hi))):
                        head = 2 * pair + half
                        sink = sink_ref[head]
                        lse_h = lse_tile[:, head:head + 1]
                        in_half = (lane < 64) if half == 0 else (lane >= 64)
                        delta = jnp.sum(jnp.where(in_half, od, 0.0), axis=1, keepdims=True)
                        s = jnp.where(mask, _dot_nt(q_pair, kk) * ATTN_SCALE, MASKED)
                        prob = jnp.exp(s - lse_h)
                        dp = _dot_nt(do_b, vv)
                        ds = (prob * (dp - delta) * ATTN_SCALE).astype(BF16)
                        dq_acc = dq_acc + _dot(ds, kk)
                        dk_acc[half] = dk_acc[half] + _dot_tn(ds, q_pair)
                        dv_acc[half] = dv_acc[half] + _dot_tn(prob.astype(BF16), do_b)
                        dsink_vals.append(jnp.sum(-jnp.exp(sink - lse_h) * delta, axis=0, keepdims=True))
                    dq_ref[:, cols] = dq_acc
                if group == 0:
                    dk_tile = dk_tile + jnp.where(low2, dk_acc[0] + pltpu.roll(dk_acc[1], 64, 1), 0.0)
                    dv_tile = dv_tile + jnp.where(low2, dv_acc[0] + pltpu.roll(dv_acc[1], 64, 1), 0.0)
                else:
                    dk_tile = dk_tile + jnp.where(low2, 0.0, pltpu.roll(dk_acc[0], 64, 1) + dk_acc[1])
                    dv_tile = dv_tile + jnp.where(low2, 0.0, pltpu.roll(dv_acc[0], 64, 1) + dv_acc[1])

            @pl.when(j > 0)
            def _():
                dkv_ref[:, :KV_W] = carry[:, :KV_W] + dk_tile[:TIME_BLOCK]
                dkv_ref[:, KV_W:] = carry[:, KV_W:] + dv_tile[:TIME_BLOCK]

            carry[:, :KV_W] = dk_tile[TIME_BLOCK:]
            carry[:, KV_W:] = dv_tile[TIME_BLOCK:]
            rowid = lax.broadcasted_iota(jnp.int32, (N_HEADS, LANES), 0)
            upd = jnp.zeros((N_HEADS, LANES), F32)
            for head, val in enumerate(dsink_vals):
                upd = upd + jnp.where(rowid == head, val, 0.0)
            dsink_ref[...] += upd

        @pl.when(j == n_blocks)
        def _():
            dkv_ref[...] = carry[...]

    last = n_blocks - 1

    def blk(width):
        return pl.BlockSpec((TIME_BLOCK, width), lambda b, j: (b * n_blocks + jnp.minimum(j, last), 0))

    prev = pl.BlockSpec((TIME_BLOCK, 2 * KV_W), lambda b, j: (b * n_blocks + jnp.maximum(jnp.minimum(j, last) - 1, 0), 0))
    dkv_spec = pl.BlockSpec((TIME_BLOCK, 2 * KV_W), lambda b, j: (b * n_blocks + jnp.maximum(j - 1, 0), 0))
    return pl.pallas_call(
        body, name="attn_bwd", grid=(batch, n_blocks + 1),
        in_specs=[pl.BlockSpec(memory_space=pltpu.SMEM), blk(ATTN_W), blk(2 * KV_W), prev, blk(ATTN_W), blk(ATTN_W),
                  blk(N_HEADS)],
        out_specs=[blk(ATTN_W), dkv_spec, pl.BlockSpec((N_HEADS, LANES), lambda b, j: (0, 0))],
        out_shape=[_sds((rows, ATTN_W), F32), _sds((rows, 2 * KV_W), F32), _sds((N_HEADS, LANES), F32)],
        scratch_shapes=[pltpu.VMEM((TIME_BLOCK, 2 * KV_W), F32)],
        compiler_params=_params("arbitrary", "arbitrary"),
    )(sinks, q, kv, kv, out, d_out, lse)


def _conv_taps(xb, prev8):
    ext = jnp.concatenate([prev8, xb], axis=0)
    n = ext.shape[0]
    return [xb] + [pltpu.roll(ext, k, 0)[SUBLANES:n] for k in range(1, CONV_TAPS)]


def _lru_gates(xc, wa, ba, wx, bx, lam):
    xcb = xc.astype(BF16)
    r = _sigmoid(_dot(xcb, wa) + ba)
    i = _sigmoid(_dot(xcb, wx) + bx)
    sp = jnp.maximum(-lam, 0.0) + _log1p(jnp.exp(-jnp.abs(lam)))
    log_a = -LRU_C * r * sp
    a = jnp.exp(log_a)
    mult = jnp.sqrt(_one_minus_exp(2.0 * log_a))
    return xcb, r, i, sp, a, mult


def _scan_fwd(a, u):
    n = a.shape[0]
    row = lax.broadcasted_iota(jnp.int32, a.shape, 0)
    s = 1
    while s < n:
        valid = row >= s
        u = jnp.where(valid, u + a * pltpu.roll(u, s, 0), u)
        a = jnp.where(valid, a * pltpu.roll(a, s, 0), a)
        s *= 2
    return a, u


def _scan_rev(cf, g):
    n = g.shape[0]
    row = lax.broadcasted_iota(jnp.int32, g.shape, 0)
    s = 1
    while s < n:
        valid = row + s < n
        g = jnp.where(valid, g + cf * pltpu.roll(g, n - s, 0), g)
        cf = jnp.where(valid, cf * pltpu.roll(cf, n - s, 0), cf)
        s *= 2
    return g


def _lru_fwd(xg, conv_w, conv_b, wa, ba, wx, bx, lam, batch, n_blocks):
    rows = xg.shape[0]

    def body(xg_ref, cw_ref, cb_ref, wa_ref, ba_ref, wx_ref, bx_ref, lam_ref, hs_ref, lru_ref, x_prev, h_carry):
        @pl.when(pl.program_id(1) == 0)
        def _():
            x_prev[...] = jnp.zeros_like(x_prev)
            h_carry[...] = jnp.zeros_like(h_carry)

        xb = xg_ref[:, :LRU_W]
        taps = _conv_taps(xb, x_prev[...])
        xc = cb_ref[...] + sum(cw_ref[CONV_TAPS - 1 - k:CONV_TAPS - k, :] * taps[k] for k in range(CONV_TAPS))
        _, _, i, _, a, mult = _lru_gates(xc, wa_ref[...], ba_ref[...], wx_ref[...], bx_ref[...], lam_ref[...])
        a_cum, h = _scan_fwd(a, mult * i * xc)
        h = h + a_cum * h_carry[...]
        hs_ref[...] = h
        lru_ref[...] = h * _gelu(xg_ref[:, LRU_W:])
        h_carry[...] = h[TIME_BLOCK - 1:TIME_BLOCK, :]
        x_prev[...] = xb[TIME_BLOCK - SUBLANES:TIME_BLOCK, :]

    def blk(width):
        return pl.BlockSpec((TIME_BLOCK, width), lambda b, j: (b * n_blocks + j, 0))

    small = [conv_w, conv_b, wa, ba, wx, bx, lam]
    return pl.pallas_call(
        body, name="lru_fwd", grid=(batch, n_blocks),
        in_specs=[blk(2 * LRU_W)] + [_whole_spec(a) for a in small],
        out_specs=[blk(LRU_W), blk(LRU_W)],
        out_shape=[_sds((rows, LRU_W), F32), _sds((rows, LRU_W), F32)],
        scratch_shapes=[pltpu.VMEM((SUBLANES, LRU_W), F32), pltpu.VMEM((1, LRU_W), F32)],
        compiler_params=_params("arbitrary", "arbitrary"),
    )(xg, *small)


def _lru_bwd(xg, hs, d_lru, conv_w, conv_b, wa, ba, wx, bx, lam, batch, n_blocks):
    rows = xg.shape[0]
    last_row = TIME_BLOCK - 1

    def body(xg_ref, xgh_ref, hs_ref, hsh_ref, dl_ref, cw_ref, cb_ref, wa_ref, ba_ref, wx_ref, bx_ref, lam_ref,
             dxg_ref, dwa_ref, dwx_ref, vec_ref, dh_carry, dxc_next):
        b = pl.program_id(0)
        j = pl.program_id(1)

        @pl.when((b == 0) & (j == 0))
        def _():
            dwa_ref[...] = jnp.zeros_like(dwa_ref)
            dwx_ref[...] = jnp.zeros_like(dwx_ref)
            vec_ref[...] = jnp.zeros_like(vec_ref)

        @pl.when(j == 0)
        def _():
            dh_carry[...] = jnp.zeros_like(dh_carry)
            dxc_next[...] = jnp.zeros_like(dxc_next)

        first = j == n_blocks - 1
        xb = xg_ref[:, :LRU_W]
        prev8 = jnp.where(first, 0.0, xgh_ref[:, :LRU_W])
        h_before = jnp.where(first, 0.0, hsh_ref[SUBLANES - 1:SUBLANES, :])
        cw = cw_ref[...]
        lam = lam_ref[...]
        wa = wa_ref[...]
        wx = wx_ref[...]
        taps = _conv_taps(xb, prev8)
        xc = cb_ref[...] + sum(cw[CONV_TAPS - 1 - k:CONV_TAPS - k, :] * taps[k] for k in range(CONV_TAPS))
        xcb, r, i, sp, a, mult = _lru_gates(xc, wa, ba_ref[...], wx, bx_ref[...], lam)
        hs = hs_ref[...]
        row = lax.broadcasted_iota(jnp.int32, hs.shape, 0)
        h_prev = jnp.where(row == 0, h_before, pltpu.roll(hs, 1, 0))
        dl = dl_ref[...]
        gate, dgate = _gelu_and_grad(xg_ref[:, LRU_W:])
        dxg_ref[:, LRU_W:] = dl * hs * dgate
        g = dl * gate + jnp.where(row == last_row, dh_carry[...], 0.0)
        cf = jnp.where(row == last_row, 0.0, pltpu.roll(a, last_row, 0))
        dh = _scan_rev(cf, g)
        dh_carry[...] = a[0:1, :] * dh[0:1, :]
        dmult = dh * i * xc
        di = dh * mult * xc
        dxc = dh * mult * i
        dlog_a = dh * h_prev * a - dmult * (a * a / mult)
        dr = dlog_a * (-LRU_C * sp)
        dlam = jnp.sum(dlog_a * (-LRU_C * r), axis=0, keepdims=True) * (-_sigmoid(-lam))
        dpr = dr * r * (1.0 - r)
        dpi = di * i * (1.0 - i)
        dprb = dpr.astype(BF16)
        dpib = dpi.astype(BF16)
        dxc = dxc + _dot_nt(dprb, wa) + _dot_nt(dpib, wx)
        dwa_ref[...] += _dot_tn(xcb, dprb)
        dwx_ref[...] += _dot_tn(xcb, dpib)
        ext = jnp.concatenate([dxc, dxc_next[...]], axis=0)
        n = ext.shape[0]
        dxb = cw[CONV_TAPS - 1:CONV_TAPS, :] * dxc
        for k in range(1, CONV_TAPS):
            dxb = dxb + cw[CONV_TAPS - 1 - k:CONV_TAPS - k, :] * pltpu.roll(ext, n - k, 0)[:TIME_BLOCK]
        dxg_ref[:, :LRU_W] = dxb
        dxc_next[...] = dxc[:SUBLANES, :]
        vecs = [jnp.sum(dxc * taps[CONV_TAPS - 1 - t], axis=0, keepdims=True) for t in range(CONV_TAPS)]
        vecs += [jnp.sum(dxc, axis=0, keepdims=True), jnp.sum(dpr, axis=0, keepdims=True),
                 jnp.sum(dpi, axis=0, keepdims=True), dlam]
        vec_ref[...] += _put_rows(SUBLANES, vecs)

    def tblk(b, j):
        return b * n_blocks + n_blocks - 1 - j

    def blk(width):
        return pl.BlockSpec((TIME_BLOCK, width), lambda b, j: (tblk(b, j), 0))

    per8 = TIME_BLOCK // SUBLANES

    def halo(width):
        return pl.BlockSpec((SUBLANES, width), lambda b, j: (jnp.maximum(per8 * tblk(b, j) - 1, 0), 0))

    small = [conv_w, conv_b, wa, ba, wx, bx, lam]
    acc = lambda shape: pl.BlockSpec(shape, lambda b, j: (0, 0))
    return pl.pallas_call(
        body, name="lru_bwd", grid=(batch, n_blocks),
        in_specs=[blk(2 * LRU_W), halo(2 * LRU_W), blk(LRU_W), halo(LRU_W), blk(LRU_W)] + [_whole_spec(a) for a in small],
        out_specs=[blk(2 * LRU_W), acc((LRU_W, LRU_W)), acc((LRU_W, LRU_W)), acc((SUBLANES, LRU_W))],
        out_shape=[_sds((rows, 2 * LRU_W), F32), _sds((LRU_W, LRU_W), F32), _sds((LRU_W, LRU_W), F32),
                   _sds((SUBLANES, LRU_W), F32)],
        scratch_shapes=[pltpu.VMEM((1, LRU_W), F32), pltpu.VMEM((SUBLANES, LRU_W), F32)],
        compiler_params=_params("arbitrary", "arbitrary"),
    )(xg, xg, hs, hs, d_lru, *small)


def _out_proj_fwd(attn, lru, g_attn, g_lru, w_out, h, g_post):
    rows = h.shape[0]

    def body(at_ref, lr_ref, ga_ref, gl_ref, w_ref, h_ref, gp_ref, grp_ref, o_ref, hm_ref):
        a = _rms(at_ref[...], ga_ref[...]).astype(BF16)
        l = _rms(lr_ref[...], gl_ref[...]).astype(BF16)
        grp_ref[:, :ATTN_W] = a
        grp_ref[:, ATTN_W:] = l
        o = _dot(a, w_ref[:ATTN_W, :]) + _dot(l, w_ref[ATTN_W:, :])
        o_ref[...] = o
        hm_ref[...] = h_ref[...] + _rms(o, gp_ref[...])

    return pl.pallas_call(
        body, name="out_proj_fwd", grid=(rows // ROW_TILE,),
        in_specs=[_row_spec(ATTN_W), _row_spec(LRU_W), _whole_spec(g_attn), _whole_spec(g_lru), _whole_spec(w_out),
                  _row_spec(D_MODEL), _whole_spec(g_post)],
        out_specs=[_row_spec(D_MODEL), _row_spec(D_MODEL), _row_spec(D_MODEL)],
        out_shape=[_sds((rows, D_MODEL), BF16), _sds((rows, D_MODEL), F32), _sds((rows, D_MODEL), F32)],
        compiler_params=_params("parallel"),
    )(attn, lru, g_attn, g_lru, w_out, h, g_post)


def _out_proj_bwd(dh_mid, o, attn, lru, g_attn, g_lru, g_post, w_out):
    rows = o.shape[0]

    def body(dhm_ref, o_ref, at_ref, lr_ref, ga_ref, gl_ref, gp_ref, w_ref, do_ref, dat_ref, dlr_ref, gacc_ref):
        @pl.when(pl.program_id(0) == 0)
        def _():
            gacc_ref[...] = jnp.zeros_like(gacc_ref)

        do, dgp = _rms_bwd(dhm_ref[...], o_ref[...], gp_ref[...])
        dob = do.astype(BF16)
        do_ref[...] = dob
        dat, dga = _rms_bwd(_dot_nt(dob, w_ref[:ATTN_W, :]), at_ref[...], ga_ref[...])
        dlr, dgl = _rms_bwd(_dot_nt(dob, w_ref[ATTN_W:, :]), lr_ref[...], gl_ref[...])
        dat_ref[...] = dat
        dlr_ref[...] = dlr
        gacc_ref[...] += _put_rows(SUBLANES, [dgp, jnp.concatenate([dga, dgl], axis=1)])

    return pl.pallas_call(
        body, name="out_proj_bwd", grid=(rows // ROW_TILE,),
        in_specs=[_row_spec(D_MODEL), _row_spec(D_MODEL), _row_spec(ATTN_W), _row_spec(LRU_W), _whole_spec(g_attn),
                  _whole_spec(g_lru), _whole_spec(g_post), _whole_spec(w_out)],
        out_specs=[_row_spec(D_MODEL), _row_spec(ATTN_W), _row_spec(LRU_W),
                   pl.BlockSpec((SUBLANES, D_MODEL), lambda i: (0, 0))],
        out_shape=[_sds((rows, D_MODEL), BF16), _sds((rows, ATTN_W), F32), _sds((rows, LRU_W), F32),
                   _sds((SUBLANES, D_MODEL), F32)],
        compiler_params=_params("arbitrary"),
    )(dh_mid, o, attn, lru, g_attn, g_lru, g_post, w_out)


def _mlp_fwd(h_mid, g_pre, w_up4, w_down, g_post):
    rows = h_mid.shape[0]

    def body(h_ref, g1_ref, wu_ref, wd_ref, g2_ref, z_ref, up_ref, act_ref, y_ref, ho_ref):
        x = h_ref[...]
        z = _rms(x, g1_ref[...]).astype(BF16)
        z_ref[...] = z
        y = jnp.zeros((ROW_TILE, D_MODEL), F32)
        for s in range(N_CHIPS):
            cols = slice(s * D_MODEL, (s + 1) * D_MODEL)
            u = _dot(z, wu_ref[s])
            up_ref[:, cols] = u
            a = jnp.square(jnp.maximum(u, 0.0)).astype(BF16)
            act_ref[:, cols] = a
            y = y + _dot(a, wd_ref[cols, :])
        y_ref[...] = y
        ho_ref[...] = x + _rms(y, g2_ref[...])

    return pl.pallas_call(
        body, name="mlp_fwd", grid=(rows // ROW_TILE,),
        in_specs=[_row_spec(D_MODEL), _whole_spec(g_pre), _whole_spec(w_up4), _whole_spec(w_down), _whole_spec(g_post)],
        out_specs=[_row_spec(D_MODEL), _row_spec(D_FF), _row_spec(D_FF), _row_spec(D_MODEL), _row_spec(D_MODEL)],
        out_shape=[_sds((rows, D_MODEL), BF16), _sds((rows, D_FF), F32), _sds((rows, D_FF), BF16),
                   _sds((rows, D_MODEL), F32), _sds((rows, D_MODEL), F32)],
        compiler_params=_params("parallel"),
    )(h_mid, g_pre, w_up4, w_down, g_post)


def _mlp_bwd(dh_out, h_mid, y, up, w_up4, w_down, g_pre, g_post):
    rows = y.shape[0]

    def body(dh_ref, hm_ref, y_ref, up_ref, wu_ref, wd_ref, g1_ref, g2_ref, dhm_ref, dy_ref, dup_ref, gacc_ref):
        @pl.when(pl.program_id(0) == 0)
        def _():
            gacc_ref[...] = jnp.zeros_like(gacc_ref)

        dh = dh_ref[...]
        dy, dg2 = _rms_bwd(dh, y_ref[...], g2_ref[...])
        dyb = dy.astype(BF16)
        dy_ref[...] = dyb
        dz = jnp.zeros((ROW_TILE, D_MODEL), F32)
        for s in range(N_CHIPS):
            cols = slice(s * D_MODEL, (s + 1) * D_MODEL)
            dact = _dot_nt(dyb, wd_ref[cols, :])
            dup = (dact * (2.0 * jnp.maximum(up_ref[:, cols], 0.0))).astype(BF16)
            dup_ref[:, cols] = dup
            dz = dz + _dot_nt(dup, wu_ref[s])
        dx, dg1 = _rms_bwd(dz, hm_ref[...], g1_ref[...])
        dhm_ref[...] = dh + dx
        gacc_ref[...] += _put_rows(SUBLANES, [dg1, dg2])

    return pl.pallas_call(
        body, name="mlp_bwd", grid=(rows // ROW_TILE,),
        in_specs=[_row_spec(D_MODEL), _row_spec(D_MODEL), _row_spec(D_MODEL), _row_spec(D_FF), _whole_spec(w_up4),
                  _whole_spec(w_down), _whole_spec(g_pre), _whole_spec(g_post)],
        out_specs=[_row_spec(D_MODEL), _row_spec(D_MODEL), _row_spec(D_FF),
                   pl.BlockSpec((SUBLANES, D_MODEL), lambda i: (0, 0))],
        out_shape=[_sds((rows, D_MODEL), F32), _sds((rows, D_MODEL), BF16), _sds((rows, D_FF), BF16),
                   _sds((SUBLANES, D_MODEL), F32)],
        compiler_params=_params("arbitrary"),
    )(dh_out, h_mid, y, up, w_up4, w_down, g_pre, g_post)


def _matmul_tn(a, b, tm, tn, name, out_dtype, column_blocks=False):
    rows, m = a.shape
    n = b.shape[1]

    def body(a_ref, b_ref, o_ref):
        o_ref[...] = _dot_tn(a_ref[...], b_ref[...]).astype(out_dtype)

    if column_blocks:
        out_spec = pl.BlockSpec((None, tm, tn), lambda i, j: (j, i, 0))
        out_shape = _sds((n // tn, m, tn), out_dtype)
    else:
        out_spec = pl.BlockSpec((tm, tn), lambda i, j: (i, j))
        out_shape = _sds((m, n), out_dtype)
    return pl.pallas_call(
        body, name=name, grid=(m // tm, n // tn),
        in_specs=[pl.BlockSpec((rows, tm), lambda i, j: (0, i)), pl.BlockSpec((rows, tn), lambda i, j: (0, j))],
        out_specs=out_spec, out_shape=out_shape,
        compiler_params=_params("parallel", "parallel"),
    )(a, b)


def _loss_head(h, target_padded, batch, n_blocks, n_real):
    rows = h.shape[0]

    def body(h_ref, t_ref, dh_ref, l_ref):
        j = pl.program_id(1)

        @pl.when((pl.program_id(0) == 0) & (j == 0))
        def _():
            l_ref[...] = jnp.zeros_like(l_ref)

        pos = j * TIME_BLOCK + lax.broadcasted_iota(jnp.int32, (TIME_BLOCK, D_MODEL), 0)
        d = jnp.where((pos >= N_META) & (pos < n_real), h_ref[...] - t_ref[...], 0.0)
        dh_ref[...] = d * (1.0 / D_MODEL)
        l_ref[...] += jnp.sum(jnp.sum(d * d, axis=0, keepdims=True), axis=1, keepdims=True)

    blk = pl.BlockSpec((TIME_BLOCK, D_MODEL), lambda b, j: (b * n_blocks + j, 0))
    return pl.pallas_call(
        body, name="loss_head", grid=(batch, n_blocks),
        in_specs=[blk, blk], out_specs=[blk, pl.BlockSpec((SUBLANES, LANES), lambda b, j: (0, 0))],
        out_shape=[_sds((rows, D_MODEL), F32), _sds((SUBLANES, LANES), F32)],
        compiler_params=_params("arbitrary", "arbitrary"),
    )(h, target_padded)


def _meta_grad(dh0, batch, rows_per_example):
    per = rows_per_example // N_META

    def body(d_ref, o_ref):
        @pl.when(pl.program_id(0) == 0)
        def _():
            o_ref[...] = jnp.zeros_like(o_ref)

        o_ref[...] += d_ref[...]

    return pl.pallas_call(
        body, name="meta_grad", grid=(batch,),
        in_specs=[pl.BlockSpec((N_META, D_MODEL), lambda b: (b * per, 0))],
        out_specs=pl.BlockSpec((N_META, D_MODEL), lambda b: (0, 0)),
        out_shape=_sds((N_META, D_MODEL), F32),
        compiler_params=_params("arbitrary"),
    )(dh0)


def _elementwise_tile(rows, cols):
    tile = rows
    while tile * cols * 4 > (1 << 20) and tile % 16 == 0:
        tile //= 2
    return tile


def _sum_slots(buf, name):
    k, rows, cols = buf.shape
    tile = _elementwise_tile(rows, cols)

    def body(*refs):
        total = refs[0][...].astype(F32)
        for r in refs[1:k]:
            total = total + r[...].astype(F32)
        refs[k][...] = total

    def slot(s):
        return pl.BlockSpec((None, tile, cols), lambda i: (s, i, 0))

    return pl.pallas_call(
        body, name=name, grid=(rows // tile,),
        in_specs=[slot(s) for s in range(k)], out_specs=pl.BlockSpec((tile, cols), lambda i: (i, 0)),
        out_shape=_sds((rows, cols), F32), compiler_params=_params("parallel"),
    )(*([buf] * k))


def _sum_pieces(pieces, landed, chip):
    _, rows, cols = pieces.shape
    tile = _elementwise_tile(rows, cols)

    def body(chip_ref, own_ref, a_ref, b_ref, c_ref, o_ref):
        o_ref[...] = ((own_ref[...].astype(F32) + a_ref[...].astype(F32)) + b_ref[...].astype(F32)) + c_ref[...].astype(F32)

    def slot(offset):
        return pl.BlockSpec((None, tile, cols), lambda i, chip_ref: ((chip_ref[0] + offset) % N_CHIPS, i, 0))

    grid_spec = pltpu.PrefetchScalarGridSpec(
        num_scalar_prefetch=1, grid=(rows // tile,), in_specs=[slot(0), slot(1), slot(2), slot(3)],
        out_specs=pl.BlockSpec((tile, cols), lambda i, chip_ref: (i, 0)))
    return pl.pallas_call(
        body, name="sum_grad_pieces", grid_spec=grid_spec, out_shape=_sds((rows, cols), F32),
        compiler_params=_params("parallel"),
    )(chip, pieces, landed, landed, landed)


def _adamw(w, m, v, grads, name, layer=None, prev=None):
    rows, cols = grads[0].shape
    tile = _elementwise_tile(rows, cols)
    ng = len(grads)
    m_scale = 1.0 - ADAM_B1 ** ADAM_STEP
    v_scale = 1.0 - ADAM_B2 ** ADAM_STEP

    def body(*refs):
        w_ref, m_ref, v_ref = refs[:3]
        g_refs = refs[3:3 + ng]
        g_out, d_out, m_out, v_out = refs[-4:]
        g = g_refs[0][...]
        for r in g_refs[1:]:
            g = g + r[...]
        m_new = ADAM_B1 * m_ref[...] + (1.0 - ADAM_B1) * g
        v_new = ADAM_B2 * v_ref[...] + (1.0 - ADAM_B2) * (g * g)
        m_hat = m_new / m_scale
        v_hat = v_new / v_scale
        g_out[...] = g
        d_out[...] = -ADAM_LR * (m_hat / (jnp.sqrt(v_hat) + ADAM_EPS) + ADAM_WD * w_ref[...])
        m_out[...] = m_new
        v_out[...] = v_new

    spec = pl.BlockSpec((tile, cols), lambda i: (i, 0))
    if layer is None:
        return pl.pallas_call(
            body, name=name, grid=(rows // tile,),
            in_specs=[spec] * (3 + ng), out_specs=[spec] * 4, out_shape=[_sds((rows, cols), F32)] * 4,
            compiler_params=_params("parallel"),
        )(w, m, v, *grads)
    of_layer = pl.BlockSpec((None, tile, cols), lambda i: (layer, i, 0))
    carried = [] if prev is None else list(prev)
    return pl.pallas_call(
        body, name=name, grid=(rows // tile,),
        in_specs=[of_layer] * 3 + [spec] * ng + [pl.BlockSpec(memory_space=pl.ANY)] * len(carried),
        out_specs=[of_layer] * 4, out_shape=[_sds(w.shape, F32)] * 4,
        input_output_aliases={3 + ng + k: k for k in range(len(carried))},
        compiler_params=_params("parallel"),
    )(w, m, v, *grads, *carried)


def _position():
    return lax.axis_index("x"), lax.axis_index("y"), lax.axis_index("c")


def _other_chips(x, y):
    return [(1 - x, y), (x, 1 - y), (1 - x, 1 - y)]


def _exchange_chips(arrays, name, scatter):
    n = len(arrays)

    def body(*refs):
        src, dst = refs[:n], refs[n:2 * n]
        send_sems, recv_sems, local_sems = refs[2 * n:]
        x, y, c = _position()
        mine = 2 * x + y
        copies = []
        for i in range(n):
            own = src[i].at[mine] if scatter else src[i]
            copies.append(pltpu.make_async_copy(own, dst[i].at[mine], local_sems.at[i]))
        for k, (px, py) in enumerate(_other_chips(x, y)):
            for i in range(n):
                piece = src[i].at[2 * px + py] if scatter else src[i]
                copies.append(pltpu.make_async_remote_copy(
                    src_ref=piece, dst_ref=dst[i].at[mine], send_sem=send_sems.at[k, i], recv_sem=recv_sems.at[k, i],
                    device_id=(px, py, c), device_id_type=MESH))
        for cp in copies:
            cp.start()
        for cp in copies:
            cp.wait()

    def out_shape(a):
        return _sds(a.shape if scatter else (N_CHIPS,) + a.shape, a.dtype)

    return pl.pallas_call(
        body, name=name, in_specs=[HBM_SPEC] * n, out_specs=[HBM_SPEC] * n, out_shape=[out_shape(a) for a in arrays],
        scratch_shapes=[pltpu.SemaphoreType.DMA((N_CHIPS - 1, n)), pltpu.SemaphoreType.DMA((N_CHIPS - 1, n)),
                        pltpu.SemaphoreType.DMA((n,))],
    )(*arrays)


def _swap_sibling(arrays, name):
    n = len(arrays)

    def body(*refs):
        src, dst = refs[:n], refs[n:2 * n]
        send_sems, recv_sems = refs[2 * n:]
        x, y, c = _position()
        copies = [pltpu.make_async_remote_copy(
            src_ref=src[i], dst_ref=dst[i], send_sem=send_sems.at[i], recv_sem=recv_sems.at[i],
            device_id=(x, y, 1 - c), device_id_type=MESH) for i in range(n)]
        for cp in copies:
            cp.start()
        for cp in copies:
            cp.wait()

    return pl.pallas_call(
        body, name=name, in_specs=[HBM_SPEC] * n, out_specs=[HBM_SPEC] * n,
        out_shape=[_sds(a.shape, a.dtype) for a in arrays],
        scratch_shapes=[pltpu.SemaphoreType.DMA((n,)), pltpu.SemaphoreType.DMA((n,))],
    )(*arrays)


SEM_SPEC = pl.BlockSpec(memory_space=pltpu.SEMAPHORE)
ANY_SPEC = pl.BlockSpec(memory_space=pl.ANY)
IN_FLIGHT = pltpu.SideEffectType.DATAFLOW_SIDE_EFFECTING


def _peer_sems(all_devices):
    return pltpu.SemaphoreType.DMA(((N_DEV if all_devices else N_CHIPS) - 1,))


def _own_slot(all_devices, x, y, c):
    return 4 * x + 2 * y + c if all_devices else 2 * x + y


def _peers(all_devices, x, y, c):
    if not all_devices:
        return [((px, py, c), 2 * px + py) for px, py in _other_chips(x, y)]
    out = []
    for fx in range(2):
        for fy in range(2):
            for fc in range(2):
                if fx or fy or fc:
                    px, py, pc = (1 - x if fx else x), (1 - y if fy else y), (1 - c if fc else c)
                    out.append(((px, py, pc), 4 * px + 2 * py + pc))
    return out


def _in_hbm(a):
    return pltpu.with_memory_space_constraint(a, pltpu.HBM)


def _place_slot(a, slot, n_slots, name):
    rows, cols = a.shape
    tile = _elementwise_tile(rows, cols)

    def body(slot_ref, a_ref, o_ref):
        o_ref[...] = a_ref[...]

    grid_spec = pltpu.PrefetchScalarGridSpec(
        num_scalar_prefetch=1, grid=(rows // tile,),
        in_specs=[pl.BlockSpec((tile, cols), lambda i, slot_ref: (i, 0))],
        out_specs=pl.BlockSpec((None, tile, cols), lambda i, slot_ref: (slot_ref[0], i, 0)))
    return pl.pallas_call(
        body, name=name, grid_spec=grid_spec, out_shape=_sds((n_slots, rows, cols), a.dtype),
        compiler_params=_params("parallel"),
    )(slot, a)


def _place_shard(w, layer, chip, name):
    _, rows, cols = w.shape
    tile = _elementwise_tile(rows, cols)

    def body(chip_ref, w_ref, o_ref):
        o_ref[...] = w_ref[...].astype(BF16)

    grid_spec = pltpu.PrefetchScalarGridSpec(
        num_scalar_prefetch=1, grid=(rows // tile,),
        in_specs=[pl.BlockSpec((None, tile, cols), lambda i, chip_ref: (layer, i, 0))],
        out_specs=pl.BlockSpec((None, tile, cols), lambda i, chip_ref: (chip_ref[0], i, 0)))
    return pl.pallas_call(
        body, name=name, grid_spec=grid_spec, out_shape=_sds((N_CHIPS, rows, cols), BF16),
        compiler_params=_params("parallel"),
    )(chip, w)


def _gather_start(bufs, after, name, all_devices=False):
    n = len(bufs)

    def body(*refs):
        buf = refs[:n]
        send, recv = refs[n + 1:2 * n + 1], refs[2 * n + 1:3 * n + 1]
        token = refs[4 * n + 1]
        x, y, c = _position()
        mine = _own_slot(all_devices, x, y, c)
        for i in range(n):
            for k, (peer, _) in enumerate(_peers(all_devices, x, y, c)):
                pltpu.make_async_remote_copy(
                    src_ref=buf[i].at[mine], dst_ref=buf[i].at[mine], send_sem=send[i].at[k], recv_sem=recv[i].at[k],
                    device_id=peer, device_id_type=MESH).start()
        token[...] = jnp.zeros_like(token)

    sems = _peer_sems(all_devices)
    out = pl.pallas_call(
        body, name=name, in_specs=[HBM_SPEC] * n + [ANY_SPEC],
        out_specs=[SEM_SPEC] * (2 * n) + [HBM_SPEC] * n + [pl.BlockSpec(memory_space=pltpu.VMEM)],
        out_shape=[sems] * (2 * n) + [pltpu.HBM(b.shape, b.dtype) for b in bufs] + [_sds((SUBLANES, LANES), F32)],
        input_output_aliases={i: 2 * n + i for i in range(n)},
        compiler_params=pltpu.CompilerParams(has_side_effects=IN_FLIGHT),
    )(*[_in_hbm(b) for b in bufs], after)
    return [(out[i], out[n + i], out[2 * n + i]) for i in range(n)], out[3 * n]


def _gather_wait(send, recv, buf, after, name, all_devices=False):
    def body(buf_ref, send_ref, recv_ref, after_ref, out_ref):
        x, y, c = _position()
        mine = _own_slot(all_devices, x, y, c)
        for k, (peer, slot) in enumerate(_peers(all_devices, x, y, c)):
            cp = pltpu.make_async_remote_copy(
                src_ref=buf_ref.at[mine], dst_ref=buf_ref.at[slot], send_sem=send_ref.at[k], recv_sem=recv_ref.at[k],
                device_id=peer, device_id_type=MESH)
            cp.wait_send()
            cp.wait_recv()

    return pl.pallas_call(
        body, name=name, in_specs=[HBM_SPEC, SEM_SPEC, SEM_SPEC, ANY_SPEC], out_specs=HBM_SPEC,
        out_shape=pltpu.HBM(buf.shape, buf.dtype), input_output_aliases={0: 0},
        compiler_params=pltpu.CompilerParams(has_side_effects=IN_FLIGHT),
    )(buf, send, recv, after)


def _scatter_start(pieces, name):
    n = len(pieces)

    def body(*refs):
        src = refs[:n]
        send, recv = refs[n:2 * n], refs[2 * n:3 * n]
        land = refs[4 * n:5 * n]
        token = refs[5 * n]
        x, y, c = _position()
        mine = 2 * x + y
        for i in range(n):
            for k, (px, py) in enumerate(_other_chips(x, y)):
                pltpu.make_async_remote_copy(
                    src_ref=src[i].at[2 * px + py], dst_ref=land[i].at[mine], send_sem=send[i].at[k], recv_sem=recv[i].at[k],
                    device_id=(px, py, c), device_id_type=MESH).start()
        token[...] = jnp.zeros_like(token)

    hbm = [pltpu.HBM(p.shape, p.dtype) for p in pieces]
    out = pl.pallas_call(
        body, name=name, in_specs=[HBM_SPEC] * n,
        out_specs=[SEM_SPEC] * (2 * n) + [HBM_SPEC] * (2 * n) + [pl.BlockSpec(memory_space=pltpu.VMEM)],
        out_shape=[_peer_sems(False)] * (2 * n) + hbm + hbm + [_sds((SUBLANES, LANES), F32)],
        input_output_aliases={i: 2 * n + i for i in range(n)},
        compiler_params=pltpu.CompilerParams(has_side_effects=IN_FLIGHT),
    )(*[_in_hbm(p) for p in pieces])
    return [(out[i], out[n + i], out[2 * n + i], out[3 * n + i]) for i in range(n)], out[4 * n]


def _scatter_wait(send, recv, pieces, land, after, name):
    def body(src_ref, land_ref, send_ref, recv_ref, after_ref, src_out, land_out):
        x, y, c = _position()
        for k, (px, py) in enumerate(_other_chips(x, y)):
            cp = pltpu.make_async_remote_copy(
                src_ref=src_ref.at[2 * px + py], dst_ref=land_ref.at[2 * px + py], send_sem=send_ref.at[k],
                recv_sem=recv_ref.at[k], device_id=(px, py, c), device_id_type=MESH)
            cp.wait_send()
            cp.wait_recv()

    return pl.pallas_call(
        body, name=name, in_specs=[HBM_SPEC, HBM_SPEC, SEM_SPEC, SEM_SPEC, ANY_SPEC], out_specs=[HBM_SPEC, HBM_SPEC],
        out_shape=[pltpu.HBM(pieces.shape, pieces.dtype), pltpu.HBM(land.shape, land.dtype)],
        input_output_aliases={0: 0, 1: 1},
        compiler_params=pltpu.CompilerParams(has_side_effects=IN_FLIGHT),
    )(pieces, land, send, recv, after)


def _rope_tables(batch, rows_per_example):
    inv_freq = ROPE_THETA ** (-jnp.arange(0, 64, 2, dtype=F32) / 64)
    ang = jnp.arange(rows_per_example, dtype=F32)[:, None] * inv_freq[None, :]
    cos, sin = jnp.cos(ang), jnp.sin(ang)
    cos128 = jnp.concatenate([cos, cos, cos, cos], axis=1)
    sin128 = jnp.concatenate([-sin, sin, -sin, sin], axis=1)
    return jnp.tile(cos128, (batch, 1)), jnp.tile(sin128, (batch, 1))


def _block_diagonal(w):
    eye = jnp.eye(LRU_BLOCKS, dtype=w.dtype)
    return (w[:, :, None, :] * eye[:, None, :, None]).reshape(LRU_W, LRU_W)


def _diagonal_blocks(dense):
    d4 = dense.reshape(LRU_BLOCKS, LRU_BLOCK, LRU_BLOCKS, LRU_BLOCK)
    return jnp.stack([d4[n, :, n, :] for n in range(LRU_BLOCKS)])


def _row(v):
    return v.reshape(1, -1)


def _local_step(x, loss_target, meta_tokens, small, depth, big_weight, on_layer_grads):
    batch, seq, _ = x.shape
    n_real = N_META + seq
    n_blocks = -(-n_real // TIME_BLOCK)
    per_example = n_blocks * TIME_BLOCK
    pad = per_example - n_real
    meta = jnp.broadcast_to(meta_tokens[None], (batch, N_META, D_MODEL))
    h = jnp.concatenate([meta, x, jnp.zeros((batch, pad, D_MODEL), F32)], axis=1).reshape(batch * per_example, D_MODEL)
    target = jnp.pad(loss_target, ((0, 0), (N_META, pad), (0, 0))).reshape(batch * per_example, D_MODEL)
    rope_cos, rope_sin = _rope_tables(batch, per_example)

    saved = []
    for l in range(depth):
        wa = _block_diagonal(small['gate_a_w'][l]).astype(BF16)
        wx = _block_diagonal(small['gate_x_w'][l]).astype(BF16)
        lru_small = (small['conv_w'][l], _row(small['conv_b'][l]), wa, _row(small['gate_a_b'][l]), wx,
                     _row(small['gate_x_b'][l]), _row(small['lru_lambda'][l]))
        w_in = big_weight('w_in', l, h)
        z1, q, kv, xg = _in_proj_fwd(h, _row(small['pre_mix_norm'][l]), w_in, rope_cos, rope_sin)
        attn, lse = _attn_fwd(q, kv, small['attn_sinks'][l], batch, n_blocks)
        hs, lru = _lru_fwd(xg, *lru_small, batch, n_blocks)
        w_out = big_weight('w_out', l, xg)
        groups, o, h_mid = _out_proj_fwd(attn, lru, _row(small['attn_group_norm'][l]), _row(small['lru_group_norm'][l]),
                                         w_out, h, _row(small['post_mix_norm'][l]))
        w_up4 = big_weight('w_up', l, hs)
        w_down = big_weight('w_down', l, hs)
        z2, up, act, y, h_out = _mlp_fwd(h_mid, _row(small['pre_mlp_norm'][l]), w_up4, w_down,
                                         _row(small['post_mlp_norm'][l]))
        saved.append(dict(h=h, z1=z1, q=q, kv=kv, xg=xg, attn=attn, lse=lse, hs=hs, lru=lru, groups=groups, o=o,
                          h_mid=h_mid, z2=z2, up=up, act=act, y=y, lru_small=lru_small, w_in=w_in, w_out=w_out,
                          w_up4=w_up4, w_down=w_down))
        h = h_out

    dh, sq_err = _loss_head(h, target, batch, n_blocks, n_real)

    gs = {n: [None] * depth for n in REPLICATED_NAMES + ('conv_w',)}
    handed_over = None
    for l in reversed(range(depth)):
        s = saved[l]
        big = {}
        g_post_mlp = _row(small['post_mlp_norm'][l])
        if handed_over is not None:
            g_post_mlp = g_post_mlp + handed_over[0, 0]
        dh_mid, dy, dup, g_mlp = _mlp_bwd(dh, s['h_mid'], s['y'], s['up'], s['w_up4'], s['w_down'],
                                          _row(small['pre_mlp_norm'][l]), g_post_mlp)
        big['w_down'] = _matmul_tn(s['act'], dy, 512, D_MODEL, "grad_w_down", BF16)
        big['w_up'] = _matmul_tn(s['z2'], dup, 512, D_MODEL, "grad_w_up", BF16, column_blocks=True)
        do, d_attn, d_lru, g_out = _out_proj_bwd(dh_mid, s['o'], s['attn'], s['lru'], _row(small['attn_group_norm'][l]),
                                                 _row(small['lru_group_norm'][l]), _row(small['post_mix_norm'][l]),
                                                 s['w_out'])
        big['w_out'] = _matmul_tn(s['groups'], do, 512, D_MODEL, "grad_w_out", BF16)
        dxg, dwa, dwx, g_lru = _lru_bwd(s['xg'], s['hs'], d_lru, *s['lru_small'], batch, n_blocks)
        dq, dkv, dsink = _attn_bwd(s['q'], s['kv'], small['attn_sinks'][l], s['attn'], s['lse'], d_attn, batch, n_blocks)
        dh, dproj, g_in = _in_proj_bwd(dh_mid, s['h'], dq, dkv, dxg, rope_cos, rope_sin, _row(small['pre_mix_norm'][l]),
                                       s['w_in'])
        big['w_in'] = _matmul_tn(s['z1'], dproj, 512, IN_COLS // 2, "grad_w_in", F32)
        handed_over = on_layer_grads(l, big)
        gs['pre_mlp_norm'][l], gs['post_mlp_norm'][l] = g_mlp[0], g_mlp[1]
        gs['post_mix_norm'][l] = g_out[0]
        gs['attn_group_norm'][l], gs['lru_group_norm'][l] = g_out[1, :ATTN_W], g_out[1, ATTN_W:]
        gs['pre_mix_norm'][l] = g_in[0]
        gs['attn_sinks'][l] = dsink[:, 0]
        gs['conv_w'][l] = g_lru[:CONV_TAPS]
        gs['conv_b'][l], gs['gate_a_b'][l], gs['gate_x_b'][l], gs['lru_lambda'][l] = g_lru[4], g_lru[5], g_lru[6], g_lru[7]
        gs['gate_a_w'][l] = _diagonal_blocks(dwa)
        gs['gate_x_w'][l] = _diagonal_blocks(dwx)

    grad_x = dh.reshape(batch, per_example, D_MODEL)[:, N_META:n_real]
    grad_meta = _meta_grad(dh, batch, per_example)
    small_grads = {n: jnp.stack(v) for n, v in gs.items()}
    return sq_err, grad_x, grad_meta, small_grads


PACK_UNIT = SUBLANES * LANES


def _size(shape):
    size = 1
    for d in shape:
        size *= d
    return size


def _pack(arrays):
    parts = []
    for a in arrays:
        flat = a.reshape(-1)
        padded = -(-flat.shape[0] // PACK_UNIT) * PACK_UNIT
        if padded != flat.shape[0]:
            flat = jnp.pad(flat, (0, padded - flat.shape[0]))
        parts.append(flat.reshape(-1, LANES))
    return jnp.concatenate(parts, axis=0)


def _unpack(buf, shapes):
    out, at = [], 0
    for shp in shapes:
        size = _size(shp)
        rows = -(-size // PACK_UNIT) * SUBLANES
        part = buf[at:at + rows]
        if rows * LANES != size:
            part = part.reshape(-1)[:size]
        out.append(part.reshape(shp))
        at += rows
    return out


def kernel(x, meta_tokens, pre_mix_norm, w_in, attn_sinks, conv_w, conv_b, gate_a_w, gate_a_b, gate_x_w, gate_x_b, lru_lambda, attn_group_norm, lru_group_norm, w_out, post_mix_norm, pre_mlp_norm, w_up, w_down, post_mlp_norm, loss_target, m_meta_tokens, m_pre_mix_norm, m_w_in, m_attn_sinks, m_conv_w, m_conv_b, m_gate_a_w, m_gate_a_b, m_gate_x_w, m_gate_x_b, m_lru_lambda, m_attn_group_norm, m_lru_group_norm, m_w_out, m_post_mix_norm, m_pre_mlp_norm, m_w_up, m_w_down, m_post_mlp_norm, v_meta_tokens, v_pre_mix_norm, v_w_in, v_attn_sinks, v_conv_w, v_conv_b, v_gate_a_w, v_gate_a_b, v_gate_x_w, v_gate_x_b, v_lru_lambda, v_attn_group_norm, v_lru_group_norm, v_w_out, v_post_mix_norm, v_pre_mlp_norm, v_w_up, v_w_down, v_post_mlp_norm):
    given = dict(locals())
    w = {n: given[n] for n in WEIGHT_NAMES}
    m = {n: given['m_' + n] for n in WEIGHT_NAMES}
    v = {n: given['v_' + n] for n in WEIGHT_NAMES}
    depth = w_in.shape[0]
    chip = 2 * lax.axis_index("x") + lax.axis_index("y")
    chip1 = chip.reshape(1).astype(jnp.int32)

    in_flight = []
    all_started = chip1
    for l in range(depth):
        bufs = [_place_shard(w[n], l, chip1, "place_" + n) for n in BIG_NAMES]
        handles, all_started = _gather_start(bufs, all_started, "gather_start_%d" % l)
        in_flight.append(dict(zip(BIG_NAMES, handles)))

    def big_weight(name, l, after):
        send, recv, buf = in_flight[l][name]
        if l == 0 and name == 'w_in':
            after = all_started
        full = _gather_wait(send, recv, buf, after, "gather_wait_%d_%s" % (l, name))
        if name == 'w_in':
            return jnp.concatenate([full[s] for s in range(N_CHIPS)], axis=1)
        if name == 'w_out':
            return full.reshape(D_MODEL, D_MODEL)
        if name == 'w_down':
            return full.reshape(D_FF, D_MODEL)
        return full

    gathered = _exchange_chips([meta_tokens, conv_w], "gather_small_weights", scatter=False)
    full_meta = jnp.concatenate([gathered[0][s] for s in range(N_CHIPS)], axis=1)
    small = {n: w[n] for n in REPLICATED_NAMES}
    small['conv_w'] = jnp.concatenate([gathered[1][s] for s in range(N_CHIPS)], axis=2)

    scattering = [None] * depth

    def on_layer_grads(l, big):
        g_in = big['w_in'].reshape(D_MODEL, N_CHIPS, IN_COLS // N_CHIPS).transpose(1, 0, 2).astype(BF16)
        pieces = [g_in, big['w_out'].reshape(N_CHIPS, D_MODEL // N_CHIPS, D_MODEL), big['w_up'],
                  big['w_down'].reshape(N_CHIPS, D_MODEL, D_MODEL)]
        scattering[l], started = _scatter_start(pieces, "scatter_start_%d" % l)
        return started

    sq_err, grad_x, grad_meta, small_grads = _local_step(x, loss_target, full_meta, small, depth, big_weight, on_layer_grads)
    loss = lax.psum(sq_err[0, 0] * (0.5 / D_MODEL), ("x", "y", "c"))

    small_names = list(REPLICATED_NAMES) + list(COLUMN_SHARDED_SMALL)
    small_full = dict(small_grads)
    small_full['meta_tokens'] = grad_meta
    device1 = (2 * chip + lax.axis_index("c")).reshape(1).astype(jnp.int32)
    packed = _place_slot(_pack([small_full[n] for n in small_names]), device1, N_DEV, "place_small_grads")
    [small_flight], done = _gather_start([packed], grad_meta, "gather_small_grads_start", all_devices=True)

    results = {n: None for n in BIG_NAMES}
    for l in reversed(range(depth)):
        partial = []
        for n, (send, recv, pieces, land) in zip(BIG_NAMES, scattering[l]):
            pieces, land = _scatter_wait(send, recv, pieces, land, done, "scatter_wait_%d_%s" % (l, n))
            partial.append(_sum_pieces(pieces, land, chip1))
        sibling = _swap_sibling(partial, "swap_partial_grads")
        done = sibling[0]
        for n, mine, other in zip(BIG_NAMES, partial, sibling):
            results[n] = _adamw(w[n], m[n], v[n], [mine, other], "adamw_" + n, layer=l, prev=results[n])

    slots = _gather_wait(*small_flight, results['w_down'][0], "gather_small_grads_wait", all_devices=True)
    summed = _unpack(_sum_slots(slots, "sum_small_grads"), [small_full[n].shape for n in small_names])
    grads = dict(zip(small_names, summed))
    grads['meta_tokens'] = lax.dynamic_slice_in_dim(grads['meta_tokens'], chip * meta_tokens.shape[1], meta_tokens.shape[1], 1)
    grads['conv_w'] = lax.dynamic_slice_in_dim(grads['conv_w'], chip * conv_w.shape[2], conv_w.shape[2], 2)

    out_g, out_d, out_m, out_v = {}, {}, {}, {}
    for n in BIG_NAMES:
        out_g[n], out_d[n], out_m[n], out_v[n] = results[n]
    shapes = [w[n].shape for n in small_names]
    res = _adamw(_pack([w[n] for n in small_names]), _pack([m[n] for n in small_names]), _pack([v[n] for n in small_names]),
                 [_pack([grads[n] for n in small_names])], "adamw_small")
    for k, store in enumerate((out_g, out_d, out_m, out_v)):
        for n, a in zip(small_names, _unpack(res[k], shapes)):
            store[n] = a

    return (loss, grad_x, *[out_g[n] for n in WEIGHT_NAMES], *[out_d[n] for n in WEIGHT_NAMES],
            *[out_m[n] for n in WEIGHT_NAMES], *[out_v[n] for n in WEIGHT_NAMES])
```

```python
import functools

import jax
import jax.numpy as jnp
from jax import lax
from jax.experimental import pallas as pl
from jax.experimental.pallas import tpu as pltpu

F32 = jnp.float32
BF16 = jnp.bfloat16

D_MODEL = 1024
N_HEADS = 8
ATTN_W = 512
KV_W = 128
LRU_W = 512
LRU_BLOCKS = 8
LRU_BLOCK = 64
IN_COLS = 1792
D_FF = 4096
N_META = 16
CONV_TAPS = 4
LRU_C = 8.0
ROPE_THETA = 10000.0
EPS = 1e-6
ATTN_SCALE = 0.125

ADAM_LR = 0.001
ADAM_B1 = 0.9
ADAM_B2 = 0.999
ADAM_EPS = 1e-08
ADAM_WD = 0.01
ADAM_STEP = 10

N_CHIPS = 4
N_DEV = 8
TIME_BLOCK = 128
ROW_TILE = 256
LANES = 128
SUBLANES = 8
MASKED = -1e30
VMEM_LIMIT = 56 * 1024 * 1024

MESH = pl.DeviceIdType.MESH
HBM_SPEC = pl.BlockSpec(memory_space=pltpu.HBM)

WEIGHT_NAMES = ['meta_tokens', 'pre_mix_norm', 'w_in', 'attn_sinks', 'conv_w', 'conv_b', 'gate_a_w', 'gate_a_b',
                'gate_x_w', 'gate_x_b', 'lru_lambda', 'attn_group_norm', 'lru_group_norm', 'w_out', 'post_mix_norm',
                'pre_mlp_norm', 'w_up', 'w_down', 'post_mlp_norm']
BIG_NAMES = ('w_in', 'w_out', 'w_up', 'w_down')
COLUMN_SHARDED_SMALL = ('meta_tokens', 'conv_w')
REPLICATED_NAMES = tuple(n for n in WEIGHT_NAMES if n not in BIG_NAMES and n not in COLUMN_SHARDED_SMALL)


def _sds(shape, dtype):
    return jax.ShapeDtypeStruct(tuple(shape), dtype)


def _params(*sem):
    return pltpu.CompilerParams(dimension_semantics=sem, vmem_limit_bytes=VMEM_LIMIT)


def _row_spec(width, tile=ROW_TILE):
    return pl.BlockSpec((tile, width), lambda i: (i, 0))


def _whole_spec(a):
    nd = a.ndim
    return pl.BlockSpec(a.shape, lambda *_: (0,) * nd)


def _rms(x, g):
    r = lax.rsqrt(jnp.mean(x * x, axis=-1, keepdims=True) + EPS)
    return x * r * g


def _rms_bwd(dy, x, g):
    r = lax.rsqrt(jnp.mean(x * x, axis=-1, keepdims=True) + EPS)
    xh = x * r
    dg = jnp.sum(dy * xh, axis=0, keepdims=True)
    dxh = dy * g
    dx = r * (dxh - xh * jnp.mean(dxh * xh, axis=-1, keepdims=True))
    return dx, dg


def _rope(x, cos, sin_signed):
    width = x.shape[1]
    reps = width // LANES
    if reps > 1:
        cos = jnp.tile(cos, (1, reps))
        sin_signed = jnp.tile(sin_signed, (1, reps))
    lane = lax.broadcasted_iota(jnp.int32, x.shape, 1)
    first_half = jnp.bitwise_and(lane, 32) == 0
    other = jnp.where(first_half, pltpu.roll(x, width - 32, 1), pltpu.roll(x, 32, 1))
    return x * cos + other * sin_signed


def _sigmoid(x):
    return 1.0 / (1.0 + jnp.exp(-x))


def _log1p(e):
    return jnp.where(e < 1e-3, e * (1.0 - e * (0.5 - e * (1.0 / 3.0))), jnp.log(1.0 + e))


def _one_minus_exp(x):
    series = -x * (1.0 + x * (0.5 + x * ((1.0 / 6.0) + x * (1.0 / 24.0))))
    return jnp.where(x > -0.05, series, 1.0 - jnp.exp(x))


GELU_K = 0.7978845608028654
GELU_C = 0.044715


def _gelu(x):
    t = jnp.tanh(GELU_K * (x + GELU_C * x * x * x))
    return 0.5 * x * (1.0 + t)


def _gelu_and_grad(x):
    x2 = x * x
    t = jnp.tanh(GELU_K * (x + GELU_C * x * x2))
    val = 0.5 * x * (1.0 + t)
    grad = 0.5 * (1.0 + t) + 0.5 * x * (1.0 - t * t) * GELU_K * (1.0 + 3.0 * GELU_C * x2)
    return val, grad


def _dot(a, b):
    return jnp.dot(a, b, preferred_element_type=F32)


def _dot_nt(a, b):
    return lax.dot_general(a, b, (((1,), (1,)), ((), ())), preferred_element_type=F32)


def _dot_tn(a, b):
    return lax.dot_general(a, b, (((0,), (0,)), ((), ())), preferred_element_type=F32)


def _put_rows(rows_8, values):
    d = values[0].shape[1]
    rowid = lax.broadcasted_iota(jnp.int32, (rows_8, d), 0)
    out = jnp.zeros((rows_8, d), F32)
    for k, v in enumerate(values):
        out = out + jnp.where(rowid == k, v, 0.0)
    return out


def _in_proj_fwd(h, gain, w_in, rope_cos, rope_sin):
    rows = h.shape[0]

    def body(h_ref, g_ref, w_ref, c_ref, s_ref, z_ref, q_ref, kv_ref, xg_ref):
        z = _rms(h_ref[...], g_ref[...]).astype(BF16)
        z_ref[...] = z
        proj = _dot_nt(z, w_ref[...])
        cos = c_ref[...]
        sin = s_ref[...]
        q_ref[...] = _rope(proj[:, :ATTN_W], cos, sin).astype(BF16)
        kv_ref[:, :KV_W] = _rope(proj[:, ATTN_W:ATTN_W + KV_W], cos, sin).astype(BF16)
        kv_ref[:, KV_W:] = proj[:, ATTN_W + KV_W:ATTN_W + 2 * KV_W].astype(BF16)
        xg_ref[...] = proj[:, ATTN_W + 2 * KV_W:]

    return pl.pallas_call(
        body, name="in_proj_fwd", grid=(rows // ROW_TILE,),
        in_specs=[_row_spec(D_MODEL), _whole_spec(gain), _whole_spec(w_in), _row_spec(LANES), _row_spec(LANES)],
        out_specs=[_row_spec(D_MODEL), _row_spec(ATTN_W), _row_spec(2 * KV_W), _row_spec(2 * LRU_W)],
        out_shape=[_sds((rows, D_MODEL), BF16), _sds((rows, ATTN_W), BF16), _sds((rows, 2 * KV_W), BF16),
                   _sds((rows, 2 * LRU_W), F32)],
        compiler_params=_params("parallel"),
    )(h, gain, w_in, rope_cos, rope_sin)


def _in_proj_bwd(dh_mid, h, dq, dkv, dxg, rope_cos, rope_sin, gain, w_in):
    rows = h.shape[0]

    def body(dhm_ref, h_ref, dq_ref, dkv_ref, dxg_ref, c_ref, s_ref, g_ref, w_ref, dh_ref, dp_ref, gacc_ref):
        @pl.when(pl.program_id(0) == 0)
        def _():
            gacc_ref[...] = jnp.zeros_like(gacc_ref)

        cos = c_ref[...]
        sin = -s_ref[...]
        dp_ref[:, :ATTN_W] = _rope(dq_ref[...], cos, sin).astype(BF16)
        dp_ref[:, ATTN_W:ATTN_W + KV_W] = _rope(dkv_ref[:, :KV_W], cos, sin).astype(BF16)
        dp_ref[:, ATTN_W + KV_W:ATTN_W + 2 * KV_W] = dkv_ref[:, KV_W:].astype(BF16)
        dp_ref[:, ATTN_W + 2 * KV_W:] = dxg_ref[...].astype(BF16)
        dz = _dot(dp_ref[...], w_ref[...])
        dx, dg = _rms_bwd(dz, h_ref[...], g_ref[...])
        dh_ref[...] = dhm_ref[...] + dx
        gacc_ref[...] += _put_rows(SUBLANES, [dg])

    return pl.pallas_call(
        body, name="in_proj_bwd", grid=(rows // ROW_TILE,),
        in_specs=[_row_spec(D_MODEL), _row_spec(D_MODEL), _row_spec(ATTN_W), _row_spec(2 * KV_W), _row_spec(2 * LRU_W),
                  _row_spec(LANES), _row_spec(LANES), _whole_spec(gain), _whole_spec(w_in)],
        out_specs=[_row_spec(D_MODEL), _row_spec(IN_COLS), pl.BlockSpec((SUBLANES, D_MODEL), lambda i: (0, 0))],
        out_shape=[_sds((rows, D_MODEL), F32), _sds((rows, IN_COLS), BF16), _sds((SUBLANES, D_MODEL), F32)],
        compiler_params=_params("arbitrary"),
    )(dh_mid, h, dq, dkv, dxg, rope_cos, rope_sin, gain, w_in)


def _kv_lane_variants(t, group):
    lane = lax.broadcasted_iota(jnp.int32, t.shape, 1)
    low = lane < 64
    swapped = pltpu.roll(t, 64, 1)
    if group == 0:
        lo, hi = jnp.where(low, t, 0.0), jnp.where(low, 0.0, swapped)
    else:
        lo, hi = jnp.where(low, swapped, 0.0), jnp.where(low, 0.0, t)
    return jnp.concatenate([lo, hi], axis=0).astype(BF16)


GROUP_ROWS = 2 * TIME_BLOCK
KEYS = 2 * TIME_BLOCK


def _window_mask(j):
    r = jnp.bitwise_and(lax.broadcasted_iota(jnp.int32, (GROUP_ROWS, KEYS), 0), TIME_BLOCK - 1)
    c = lax.broadcasted_iota(jnp.int32, (GROUP_ROWS, KEYS), 1)
    return (c > r) & (c <= r + TIME_BLOCK) & ((c >= TIME_BLOCK) | (j > 0))


def _group_rows(ref, group):
    lo = 2 * group * LANES
    return jnp.concatenate([ref[:, lo:lo + LANES], ref[:, lo + LANES:lo + 2 * LANES]], axis=0)


def _per_head(sink_ref, group, half):
    upper = lax.broadcasted_iota(jnp.int32, (GROUP_ROWS, 1), 0) < TIME_BLOCK
    return jnp.where(upper, sink_ref[4 * group + half], sink_ref[4 * group + 2 + half])


def _attn_fwd(q, kv, sinks, batch, n_blocks):
    rows = q.shape[0]

    def body(sink_ref, q_ref, kvc_ref, kvp_ref, o_ref, lse_ref):
        j = pl.program_id(1)
        kv2 = jnp.concatenate([kvp_ref[...], kvc_ref[...]], axis=0).astype(F32)
        mask = _window_mask(j)
        lane8 = lax.broadcasted_iota(jnp.int32, (TIME_BLOCK, N_HEADS), 1)
        lse_tile = jnp.zeros((TIME_BLOCK, N_HEADS), F32)
        for group in range(2):
            k_cat = _kv_lane_variants(kv2[:, :KV_W], group)
            v_cat = _kv_lane_variants(kv2[:, KV_W:], group)
            s_all = _dot_nt(_group_rows(q_ref, group), k_cat)
            probs = []
            for half in range(2):
                sink = _per_head(sink_ref, group, half)
                s = jnp.where(mask, s_all[:, half * KEYS:(half + 1) * KEYS] * ATTN_SCALE, MASKED)
                m = jnp.maximum(jnp.max(s, axis=1, keepdims=True), sink)
                e = jnp.exp(s - m)
                den = jnp.sum(e, axis=1, keepdims=True) + jnp.exp(sink - m)
                probs.append((e / den).astype(BF16))
                lse = m + jnp.log(den)
                lse_tile = lse_tile + jnp.where(lane8 == 4 * group + half, lse[:TIME_BLOCK], 0.0)
                lse_tile = lse_tile + jnp.where(lane8 == 4 * group + 2 + half, lse[TIME_BLOCK:], 0.0)
            out = _dot(jnp.concatenate(probs, axis=1), v_cat)
            o_ref[:, 2 * group * LANES:(2 * group + 1) * LANES] = out[:TIME_BLOCK]
            o_ref[:, (2 * group + 1) * LANES:(2 * group + 2) * LANES] = out[TIME_BLOCK:]
        lse_ref[...] = lse_tile

    def blk(width):
        return pl.BlockSpec((TIME_BLOCK, width), lambda b, j: (b * n_blocks + j, 0))

    prev = pl.BlockSpec((TIME_BLOCK, 2 * KV_W), lambda b, j: (b * n_blocks + jnp.maximum(j - 1, 0), 0))
    return pl.pallas_call(
        body, name="attn_fwd", grid=(batch, n_blocks),
        in_specs=[pl.BlockSpec(memory_space=pltpu.SMEM), blk(ATTN_W), blk(2 * KV_W), prev],
        out_specs=[blk(ATTN_W), blk(N_HEADS)],
        out_shape=[_sds((rows, ATTN_W), F32), _sds((rows, N_HEADS), F32)],
        compiler_params=_params("parallel", "parallel"),
    )(sinks, q, kv, kv)


def _attn_bwd(q, kv, sinks, out, lse, d_out, batch, n_blocks):
    rows = q.shape[0]

    def body(sink_ref, q_ref, kvc_ref, kvp_ref, o_ref, do_ref, lse_ref, dq_ref, dkv_ref, dsink_ref, carry):
        b = pl.program_id(0)
        j = pl.program_id(1)

        @pl.when((b == 0) & (j == 0))
        def _():
            dsink_ref[...] = jnp.zeros_like(dsink_ref)

        @pl.when(j < n_blocks)
        def _():
            kv2 = jnp.concatenate([kvp_ref[...], kvc_ref[...]], axis=0).astype(F32)
            mask = _window_mask(j)
            lane = lax.broadcasted_iota(jnp.int32, (GROUP_ROWS, LANES), 1)
            low = lax.broadcasted_iota(jnp.int32, (KEYS, LANES), 1) < 64
            upper = lax.broadcasted_iota(jnp.int32, (GROUP_ROWS, 1), 0) < TIME_BLOCK
            lse_tile = lse_ref[...]
            dk_tile = jnp.zeros((KEYS, KV_W), F32)
            dv_tile = jnp.zeros((KEYS, KV_W), F32)
            dsink_vals = {}
            for group in range(2):
                k_cat = _kv_lane_variants(kv2[:, :KV_W], group)
                v_cat = _kv_lane_variants(kv2[:, KV_W:], group)
                q_rows = _group_rows(q_ref, group)
                do_rows = _group_rows(do_ref, group)
                do_b = do_rows.astype(BF16)
                od = do_rows * _group_rows(o_ref, group)
                s_all = _dot_nt(q_rows, k_cat)
                dp_all = _dot_nt(do_b, v_cat)
                probs, dss = [], []
                for half in range(2):
                    heads = (4 * group + half, 4 * group + 2 + half)
                    sink = _per_head(sink_ref, group, half)
                    lse_h = jnp.concatenate([lse_tile[:, h:h + 1] for h in heads], axis=0)
                    in_half = (lane < 64) if half == 0 else (lane >= 64)
                    delta = jnp.sum(jnp.where(in_half, od, 0.0), axis=1, keepdims=True)
                    cols = slice(half * KEYS, (half + 1) * KEYS)
                    prob = jnp.exp(jnp.where(mask, s_all[:, cols] * ATTN_SCALE, MASKED) - lse_h)
                    probs.append(prob.astype(BF16))
                    dss.append((prob * (dp_all[:, cols] - delta) * ATTN_SCALE).astype(BF16))
                    dsink = -jnp.exp(sink - lse_h) * delta
                    dsink_vals[heads[0]] = jnp.sum(jnp.where(upper, dsink, 0.0), axis=0, keepdims=True)
                    dsink_vals[heads[1]] = jnp.sum(jnp.where(upper, 0.0, dsink), axis=0, keepdims=True)
                ds = jnp.concatenate(dss, axis=1)
                dq_rows = _dot(ds, k_cat)
                dq_ref[:, 2 * group * LANES:(2 * group + 1) * LANES] = dq_rows[:TIME_BLOCK]
                dq_ref[:, (2 * group + 1) * LANES:(2 * group + 2) * LANES] = dq_rows[TIME_BLOCK:]
                dk_cat = _dot_tn(ds, q_rows)
                dv_cat = _dot_tn(jnp.concatenate(probs, axis=1), do_b)
                if group == 0:
                    dk_tile = dk_tile + jnp.where(low, dk_cat[:KEYS] + pltpu.roll(dk_cat[KEYS:], 64, 1), 0.0)
                    dv_tile = dv_tile + jnp.where(low, dv_cat[:KEYS] + pltpu.roll(dv_cat[KEYS:], 64, 1), 0.0)
                else:
                    dk_tile = dk_tile + jnp.where(low, 0.0, pltpu.roll(dk_cat[:KEYS], 64, 1) + dk_cat[KEYS:])
                    dv_tile = dv_tile + jnp.where(low, 0.0, pltpu.roll(dv_cat[:KEYS], 64, 1) + dv_cat[KEYS:])

            @pl.when(j > 0)
            def _():
                dkv_ref[:, :KV_W] = carry[:, :KV_W] + dk_tile[:TIME_BLOCK]
                dkv_ref[:, KV_W:] = carry[:, KV_W:] + dv_tile[:TIME_BLOCK]

            carry[:, :KV_W] = dk_tile[TIME_BLOCK:]
            carry[:, KV_W:] = dv_tile[TIME_BLOCK:]
            rowid = lax.broadcasted_iota(jnp.int32, (N_HEADS, LANES), 0)
            upd = jnp.zeros((N_HEADS, LANES), F32)
            for head, val in dsink_vals.items():
                upd = upd + jnp.where(rowid == head, val, 0.0)
            dsink_ref[...] += upd

        @pl.when(j == n_blocks)
        def _():
            dkv_ref[...] = carry[...]

    last = n_blocks - 1

    def blk(width):
        return pl.BlockSpec((TIME_BLOCK, width), lambda b, j: (b * n_blocks + jnp.minimum(j, last), 0))

    prev = pl.BlockSpec((TIME_BLOCK, 2 * KV_W), lambda b, j: (b * n_blocks + jnp.maximum(jnp.minimum(j, last) - 1, 0), 0))
    dkv_spec = pl.BlockSpec((TIME_BLOCK, 2 * KV_W), lambda b, j: (b * n_blocks + jnp.maximum(j - 1, 0), 0))
    return pl.pallas_call(
        body, name="attn_bwd", grid=(batch, n_blocks + 1),
        in_specs=[pl.BlockSpec(memory_space=pltpu.SMEM), blk(ATTN_W), blk(2 * KV_W), prev, blk(ATTN_W), blk(ATTN_W),
                  blk(N_HEADS)],
        out_specs=[blk(ATTN_W), dkv_spec, pl.BlockSpec((N_HEADS, LANES), lambda b, j: (0, 0))],
        out_shape=[_sds((rows, ATTN_W), F32), _sds((rows, 2 * KV_W), F32), _sds((N_HEADS, LANES), F32)],
        scratch_shapes=[pltpu.VMEM((TIME_BLOCK, 2 * KV_W), F32)],
        compiler_params=_params("arbitrary", "arbitrary"),
    )(sinks, q, kv, kv, out, d_out, lse)


def _conv_taps(xb, prev8):
    ext = jnp.concatenate([prev8, xb], axis=0)
    n = ext.shape[0]
    return [xb] + [pltpu.roll(ext, k, 0)[SUBLANES:n] for k in range(1, CONV_TAPS)]


def _lru_gates(xc, wa, ba, wx, bx, lam):
    xcb = xc.astype(BF16)
    r = _sigmoid(_dot(xcb, wa) + ba)
    i = _sigmoid(_dot(xcb, wx) + bx)
    sp = jnp.maximum(-lam, 0.0) + _log1p(jnp.exp(-jnp.abs(lam)))
    log_a = -LRU_C * r * sp
    a = jnp.exp(log_a)
    mult = jnp.sqrt(_one_minus_exp(2.0 * log_a))
    return xcb, r, i, sp, a, mult


def _scan_fwd(a, u, h_before):
    n, d = a.shape
    groups = n // SUBLANES
    a = a.reshape(groups, SUBLANES, d)
    u = u.reshape(groups, SUBLANES, d)
    sub = lax.broadcasted_iota(jnp.int32, a.shape, 1)
    s = 1
    while s < SUBLANES:
        valid = sub >= s
        u = jnp.where(valid, u + a * pltpu.roll(u, s, 1), u)
        a = jnp.where(valid, a * pltpu.roll(a, s, 1), a)
        s *= 2
    out, prev = [], h_before
    for g in range(groups):
        out.append(u[g] + a[g] * prev)
        prev = out[-1][SUBLANES - 1:SUBLANES, :]
    return jnp.concatenate(out, axis=0)


def _scan_rev(cf, g, d_after):
    n, d = g.shape
    groups = n // SUBLANES
    cf = cf.reshape(groups, SUBLANES, d)
    g = g.reshape(groups, SUBLANES, d)
    sub = lax.broadcasted_iota(jnp.int32, g.shape, 1)
    s = 1
    while s < SUBLANES:
        valid = sub + s < SUBLANES
        g = jnp.where(valid, g + cf * pltpu.roll(g, SUBLANES - s, 1), g)
        cf = jnp.where(valid, cf * pltpu.roll(cf, SUBLANES - s, 1), cf)
        s *= 2
    out, nxt = [None] * groups, d_after
    for k in reversed(range(groups)):
        out[k] = g[k] + cf[k] * nxt
        nxt = out[k][0:1, :]
    return jnp.concatenate(out, axis=0)


def _lru_fwd(xg, conv_w, conv_b, wa, ba, wx, bx, lam, batch, n_blocks):
    rows = xg.shape[0]

    def body(xg_ref, cw_ref, cb_ref, wa_ref, ba_ref, wx_ref, bx_ref, lam_ref, hs_ref, lru_ref, x_prev, h_carry):
        @pl.when(pl.program_id(1) == 0)
        def _():
            x_prev[...] = jnp.zeros_like(x_prev)
            h_carry[...] = jnp.zeros_like(h_carry)

        xb = xg_ref[:, :LRU_W]
        taps = _conv_taps(xb, x_prev[...])
        xc = cb_ref[...] + sum(cw_ref[CONV_TAPS - 1 - k:CONV_TAPS - k, :] * taps[k] for k in range(CONV_TAPS))
        _, _, i, _, a, mult = _lru_gates(xc, wa_ref[...], ba_ref[...], wx_ref[...], bx_ref[...], lam_ref[...])
        h = _scan_fwd(a, mult * i * xc, h_carry[...])
        hs_ref[...] = h
        lru_ref[...] = h * _gelu(xg_ref[:, LRU_W:])
        h_carry[...] = h[TIME_BLOCK - 1:TIME_BLOCK, :]
        x_prev[...] = xb[TIME_BLOCK - SUBLANES:TIME_BLOCK, :]

    def blk(width):
        return pl.BlockSpec((TIME_BLOCK, width), lambda b, j: (b * n_blocks + j, 0))

    small = [conv_w, conv_b, wa, ba, wx, bx, lam]
    return pl.pallas_call(
        body, name="lru_fwd", grid=(batch, n_blocks),
        in_specs=[blk(2 * LRU_W)] + [_whole_spec(a) for a in small],
        out_specs=[blk(LRU_W), blk(LRU_W)],
        out_shape=[_sds((rows, LRU_W), F32), _sds((rows, LRU_W), F32)],
        scratch_shapes=[pltpu.VMEM((SUBLANES, LRU_W), F32), pltpu.VMEM((1, LRU_W), F32)],
        compiler_params=_params("arbitrary", "arbitrary"),
    )(xg, *small)


def _lru_bwd(xg, hs, d_lru, conv_w, conv_b, wa, ba, wx, bx, lam, batch, n_blocks):
    rows = xg.shape[0]
    last_row = TIME_BLOCK - 1

    def body(xg_ref, xgh_ref, hs_ref, hsh_ref, dl_ref, cw_ref, cb_ref, wa_ref, ba_ref, wx_ref, bx_ref, lam_ref,
             dxg_ref, dwa_ref, dwx_ref, vec_ref, dh_carry, dxc_next):
        b = pl.program_id(0)
        j = pl.program_id(1)

        @pl.when((b == 0) & (j == 0))
        def _():
            dwa_ref[...] = jnp.zeros_like(dwa_ref)
            dwx_ref[...] = jnp.zeros_like(dwx_ref)
            vec_ref[...] = jnp.zeros_like(vec_ref)

        @pl.when(j == 0)
        def _():
            dh_carry[...] = jnp.zeros_like(dh_carry)
            dxc_next[...] = jnp.zeros_like(dxc_next)

        first = j == n_blocks - 1
        xb = xg_ref[:, :LRU_W]
        prev8 = jnp.where(first, 0.0, xgh_ref[:, :LRU_W])
        h_before = jnp.where(first, 0.0, hsh_ref[SUBLANES - 1:SUBLANES, :])
        cw = cw_ref[...]
        lam = lam_ref[...]
        wa = wa_ref[...]
        wx = wx_ref[...]
        taps = _conv_taps(xb, prev8)
        xc = cb_ref[...] + sum(cw[CONV_TAPS - 1 - k:CONV_TAPS - k, :] * taps[k] for k in range(CONV_TAPS))
        xcb, r, i, sp, a, mult = _lru_gates(xc, wa, ba_ref[...], wx, bx_ref[...], lam)
        hs = hs_ref[...]
        row = lax.broadcasted_iota(jnp.int32, hs.shape, 0)
        h_prev = jnp.where(row == 0, h_before, pltpu.roll(hs, 1, 0))
        dl = dl_ref[...]
        gate, dgate = _gelu_and_grad(xg_ref[:, LRU_W:])
        dxg_ref[:, LRU_W:] = dl * hs * dgate
        cf = jnp.where(row == last_row, 1.0, pltpu.roll(a, last_row, 0))
        dh = _scan_rev(cf, dl * gate, dh_carry[...])
        dh_carry[...] = a[0:1, :] * dh[0:1, :]
        dmult = dh * i * xc
        di = dh * mult * xc
        dxc = dh * mult * i
        dlog_a = dh * h_prev * a - dmult * (a * a / mult)
        dr = dlog_a * (-LRU_C * sp)
        dlam = jnp.sum(dlog_a * (-LRU_C * r), axis=0, keepdims=True) * (-_sigmoid(-lam))
        dpr = dr * r * (1.0 - r)
        dpi = di * i * (1.0 - i)
        dprb = dpr.astype(BF16)
        dpib = dpi.astype(BF16)
        dxc = dxc + _dot_nt(dprb, wa) + _dot_nt(dpib, wx)
        dwa_ref[...] += _dot_tn(xcb, dprb)
        dwx_ref[...] += _dot_tn(xcb, dpib)
        ext = jnp.concatenate([dxc, dxc_next[...]], axis=0)
        n = ext.shape[0]
        dxb = cw[CONV_TAPS - 1:CONV_TAPS, :] * dxc
        for k in range(1, CONV_TAPS):
            dxb = dxb + cw[CONV_TAPS - 1 - k:CONV_TAPS - k, :] * pltpu.roll(ext, n - k, 0)[:TIME_BLOCK]
        dxg_ref[:, :LRU_W] = dxb
        dxc_next[...] = dxc[:SUBLANES, :]
        vecs = [jnp.sum(dxc * taps[CONV_TAPS - 1 - t], axis=0, keepdims=True) for t in range(CONV_TAPS)]
        vecs += [jnp.sum(dxc, axis=0, keepdims=True), jnp.sum(dpr, axis=0, keepdims=True),
                 jnp.sum(dpi, axis=0, keepdims=True), dlam]
        vec_ref[...] += _put_rows(SUBLANES, vecs)

    def tblk(b, j):
        return b * n_blocks + n_blocks - 1 - j

    def blk(width):
        return pl.BlockSpec((TIME_BLOCK, width), lambda b, j: (tblk(b, j), 0))

    per8 = TIME_BLOCK // SUBLANES

    def halo(width):
        return pl.BlockSpec((SUBLANES, width), lambda b, j: (jnp.maximum(per8 * tblk(b, j) - 1, 0), 0))

    small = [conv_w, conv_b, wa, ba, wx, bx, lam]
    acc = lambda shape: pl.BlockSpec(shape, lambda b, j: (0, 0))
    return pl.pallas_call(
        body, name="lru_bwd", grid=(batch, n_blocks),
        in_specs=[blk(2 * LRU_W), halo(2 * LRU_W), blk(LRU_W), halo(LRU_W), blk(LRU_W)] + [_whole_spec(a) for a in small],
        out_specs=[blk(2 * LRU_W), acc((LRU_W, LRU_W)), acc((LRU_W, LRU_W)), acc((SUBLANES, LRU_W))],
        out_shape=[_sds((rows, 2 * LRU_W), F32), _sds((LRU_W, LRU_W), F32), _sds((LRU_W, LRU_W), F32),
                   _sds((SUBLANES, LRU_W), F32)],
        scratch_shapes=[pltpu.VMEM((1, LRU_W), F32), pltpu.VMEM((SUBLANES, LRU_W), F32)],
        compiler_params=_params("arbitrary", "arbitrary"),
    )(xg, xg, hs, hs, d_lru, *small)


def _out_proj_fwd(attn, lru, g_attn, g_lru, w_out, h, g_post):
    rows = h.shape[0]

    def body(at_ref, lr_ref, ga_ref, gl_ref, w_ref, h_ref, gp_ref, grp_ref, o_ref, hm_ref):
        a = _rms(at_ref[...], ga_ref[...]).astype(BF16)
        l = _rms(lr_ref[...], gl_ref[...]).astype(BF16)
        grp_ref[:, :ATTN_W] = a
        grp_ref[:, ATTN_W:] = l
        o = _dot(a, w_ref[:ATTN_W, :]) + _dot(l, w_ref[ATTN_W:, :])
        o_ref[...] = o
        hm_ref[...] = h_ref[...] + _rms(o, gp_ref[...])

    return pl.pallas_call(
        body, name="out_proj_fwd", grid=(rows // ROW_TILE,),
        in_specs=[_row_spec(ATTN_W), _row_spec(LRU_W), _whole_spec(g_attn), _whole_spec(g_lru), _whole_spec(w_out),
                  _row_spec(D_MODEL), _whole_spec(g_post)],
        out_specs=[_row_spec(D_MODEL), _row_spec(D_MODEL), _row_spec(D_MODEL)],
        out_shape=[_sds((rows, D_MODEL), BF16), _sds((rows, D_MODEL), F32), _sds((rows, D_MODEL), F32)],
        compiler_params=_params("parallel"),
    )(attn, lru, g_attn, g_lru, w_out, h, g_post)


def _out_proj_bwd(dh_mid, o, attn, lru, g_attn, g_lru, g_post, w_out):
    rows = o.shape[0]

    def body(dhm_ref, o_ref, at_ref, lr_ref, ga_ref, gl_ref, gp_ref, w_ref, do_ref, dat_ref, dlr_ref, gacc_ref):
        @pl.when(pl.program_id(0) == 0)
        def _():
            gacc_ref[...] = jnp.zeros_like(gacc_ref)

        do, dgp = _rms_bwd(dhm_ref[...], o_ref[...], gp_ref[...])
        dob = do.astype(BF16)
        do_ref[...] = dob
        dat, dga = _rms_bwd(_dot_nt(dob, w_ref[:ATTN_W, :]), at_ref[...], ga_ref[...])
        dlr, dgl = _rms_bwd(_dot_nt(dob, w_ref[ATTN_W:, :]), lr_ref[...], gl_ref[...])
        dat_ref[...] = dat
        dlr_ref[...] = dlr
        gacc_ref[...] += _put_rows(SUBLANES, [dgp, jnp.concatenate([dga, dgl], axis=1)])

    return pl.pallas_call(
        body, name="out_proj_bwd", grid=(rows // ROW_TILE,),
        in_specs=[_row_spec(D_MODEL), _row_spec(D_MODEL), _row_spec(ATTN_W), _row_spec(LRU_W), _whole_spec(g_attn),
                  _whole_spec(g_lru), _whole_spec(g_post), _whole_spec(w_out)],
        out_specs=[_row_spec(D_MODEL), _row_spec(ATTN_W), _row_spec(LRU_W),
                   pl.BlockSpec((SUBLANES, D_MODEL), lambda i: (0, 0))],
        out_shape=[_sds((rows, D_MODEL), BF16), _sds((rows, ATTN_W), F32), _sds((rows, LRU_W), F32),
                   _sds((SUBLANES, D_MODEL), F32)],
        compiler_params=_params("arbitrary"),
    )(dh_mid, o, attn, lru, g_attn, g_lru, g_post, w_out)


def _mlp_fwd(h_mid, g_pre, w_up4, w_down, g_post):
    rows = h_mid.shape[0]

    def body(h_ref, g1_ref, wu_ref, wd_ref, g2_ref, z_ref, up_ref, act_ref, y_ref, ho_ref):
        x = h_ref[...]
        z = _rms(x, g1_ref[...]).astype(BF16)
        z_ref[...] = z
        y = jnp.zeros((ROW_TILE, D_MODEL), F32)
        for s in range(N_CHIPS):
            cols = slice(s * D_MODEL, (s + 1) * D_MODEL)
            u = _dot(z, wu_ref[s])
            up_ref[:, cols] = u
            a = jnp.square(jnp.maximum(u, 0.0)).astype(BF16)
            act_ref[:, cols] = a
            y = y + _dot(a, wd_ref[cols, :])
        y_ref[...] = y
        ho_ref[...] = x + _rms(y, g2_ref[...])

    return pl.pallas_call(
        body, name="mlp_fwd", grid=(rows // ROW_TILE,),
        in_specs=[_row_spec(D_MODEL), _whole_spec(g_pre), _whole_spec(w_up4), _whole_spec(w_down), _whole_spec(g_post)],
        out_specs=[_row_spec(D_MODEL), _row_spec(D_FF), _row_spec(D_FF), _row_spec(D_MODEL), _row_spec(D_MODEL)],
        out_shape=[_sds((rows, D_MODEL), BF16), _sds((rows, D_FF), F32), _sds((rows, D_FF), BF16),
                   _sds((rows, D_MODEL), F32), _sds((rows, D_MODEL), F32)],
        compiler_params=_params("parallel"),
    )(h_mid, g_pre, w_up4, w_down, g_post)


def _mlp_bwd(dh_out, h_mid, y, up, w_up4, w_down, g_pre, g_post):
    rows = y.shape[0]

    def body(dh_ref, hm_ref, y_ref, up_ref, wu_ref, wd_ref, g1_ref, g2_ref, dhm_ref, dy_ref, dup_ref, gacc_ref):
        @pl.when(pl.program_id(0) == 0)
        def _():
            gacc_ref[...] = jnp.zeros_like(gacc_ref)

        dh = dh_ref[...]
        dy, dg2 = _rms_bwd(dh, y_ref[...], g2_ref[...])
        dyb = dy.astype(BF16)
        dy_ref[...] = dyb
        dz = jnp.zeros((ROW_TILE, D_MODEL), F32)
        for s in range(N_CHIPS):
            cols = slice(s * D_MODEL, (s + 1) * D_MODEL)
            dact = _dot_nt(dyb, wd_ref[cols, :])
            dup = (dact * (2.0 * jnp.maximum(up_ref[:, cols], 0.0))).astype(BF16)
            dup_ref[:, cols] = dup
            dz = dz + _dot_nt(dup, wu_ref[s])
        dx, dg1 = _rms_bwd(dz, hm_ref[...], g1_ref[...])
        dhm_ref[...] = dh + dx
        gacc_ref[...] += _put_rows(SUBLANES, [dg1, dg2])

    return pl.pallas_call(
        body, name="mlp_bwd", grid=(rows // ROW_TILE,),
        in_specs=[_row_spec(D_MODEL), _row_spec(D_MODEL), _row_spec(D_MODEL), _row_spec(D_FF), _whole_spec(w_up4),
                  _whole_spec(w_down), _whole_spec(g_pre), _whole_spec(g_post)],
        out_specs=[_row_spec(D_MODEL), _row_spec(D_MODEL), _row_spec(D_FF),
                   pl.BlockSpec((SUBLANES, D_MODEL), lambda i: (0, 0))],
        out_shape=[_sds((rows, D_MODEL), F32), _sds((rows, D_MODEL), BF16), _sds((rows, D_FF), BF16),
                   _sds((SUBLANES, D_MODEL), F32)],
        compiler_params=_params("arbitrary"),
    )(dh_out, h_mid, y, up, w_up4, w_down, g_pre, g_post)


def _matmul_tn(a, b, tm, tn, name, out_dtype, column_blocks=False):
    rows, m = a.shape
    n = b.shape[1]

    def body(a_ref, b_ref, o_ref):
        o_ref[...] = _dot_tn(a_ref[...], b_ref[...]).astype(out_dtype)

    if column_blocks:
        out_spec = pl.BlockSpec((None, tm, tn), lambda i, j: (j, i, 0))
        out_shape = _sds((n // tn, m, tn), out_dtype)
    else:
        out_spec = pl.BlockSpec((tm, tn), lambda i, j: (i, j))
        out_shape = _sds((m, n), out_dtype)
    return pl.pallas_call(
        body, name=name, grid=(m // tm, n // tn),
        in_specs=[pl.BlockSpec((rows, tm), lambda i, j: (0, i)), pl.BlockSpec((rows, tn), lambda i, j: (0, j))],
        out_specs=out_spec, out_shape=out_shape,
        compiler_params=_params("parallel", "parallel"),
    )(a, b)


def _loss_head(h, target_padded, batch, n_blocks, n_real):
    rows = h.shape[0]

    def body(h_ref, t_ref, dh_ref, l_ref):
        j = pl.program_id(1)

        @pl.when((pl.program_id(0) == 0) & (j == 0))
        def _():
            l_ref[...] = jnp.zeros_like(l_ref)

        pos = j * TIME_BLOCK + lax.broadcasted_iota(jnp.int32, (TIME_BLOCK, D_MODEL), 0)
        d = jnp.where((pos >= N_META) & (pos < n_real), h_ref[...] - t_ref[...], 0.0)
        dh_ref[...] = d * (1.0 / D_MODEL)
        l_ref[...] += jnp.sum(jnp.sum(d * d, axis=0, keepdims=True), axis=1, keepdims=True)

    blk = pl.BlockSpec((TIME_BLOCK, D_MODEL), lambda b, j: (b * n_blocks + j, 0))
    return pl.pallas_call(
        body, name="loss_head", grid=(batch, n_blocks),
        in_specs=[blk, blk], out_specs=[blk, pl.BlockSpec((SUBLANES, LANES), lambda b, j: (0, 0))],
        out_shape=[_sds((rows, D_MODEL), F32), _sds((SUBLANES, LANES), F32)],
        compiler_params=_params("arbitrary", "arbitrary"),
    )(h, target_padded)


def _meta_grad(dh0, batch, rows_per_example):
    per = rows_per_example // N_META

    def body(d_ref, o_ref):
        @pl.when(pl.program_id(0) == 0)
        def _():
            o_ref[...] = jnp.zeros_like(o_ref)

        o_ref[...] += d_ref[...]

    return pl.pallas_call(
        body, name="meta_grad", grid=(batch,),
        in_specs=[pl.BlockSpec((N_META, D_MODEL), lambda b: (b * per, 0))],
        out_specs=pl.BlockSpec((N_META, D_MODEL), lambda b: (0, 0)),
        out_shape=_sds((N_META, D_MODEL), F32),
        compiler_params=_params("arbitrary"),
    )(dh0)


def _elementwise_tile(rows, cols):
    tile = rows
    while tile * cols * 4 > (1 << 20) and tile % 16 == 0:
        tile //= 2
    return tile


def _sum_slots(buf, name):
    k, rows, cols = buf.shape
    tile = _elementwise_tile(rows, cols)

    def body(*refs):
        total = refs[0][...].astype(F32)
        for r in refs[1:k]:
            total = total + r[...].astype(F32)
        refs[k][...] = total

    def slot(s):
        return pl.BlockSpec((None, tile, cols), lambda i: (s, i, 0))

    return pl.pallas_call(
        body, name=name, grid=(rows // tile,),
        in_specs=[slot(s) for s in range(k)], out_specs=pl.BlockSpec((tile, cols), lambda i: (i, 0)),
        out_shape=_sds((rows, cols), F32), compiler_params=_params("parallel"),
    )(*([buf] * k))


def _sum_pieces(pieces, landed, chip):
    _, rows, cols = pieces.shape
    tile = _elementwise_tile(rows, cols)

    def body(chip_ref, own_ref, a_ref, b_ref, c_ref, o_ref):
        o_ref[...] = ((own_ref[...].astype(F32) + a_ref[...].astype(F32)) + b_ref[...].astype(F32)) + c_ref[...].astype(F32)

    def slot(offset):
        return pl.BlockSpec((None, tile, cols), lambda i, chip_ref: ((chip_ref[0] + offset) % N_CHIPS, i, 0))

    grid_spec = pltpu.PrefetchScalarGridSpec(
        num_scalar_prefetch=1, grid=(rows // tile,), in_specs=[slot(0), slot(1), slot(2), slot(3)],
        out_specs=pl.BlockSpec((tile, cols), lambda i, chip_ref: (i, 0)))
    return pl.pallas_call(
        body, name="sum_grad_pieces", grid_spec=grid_spec, out_shape=_sds((rows, cols), F32),
        compiler_params=_params("parallel"),
    )(chip, pieces, landed, landed, landed)


def _adamw(w, m, v, grads, name, layer=None, prev=None):
    rows, cols = grads[0].shape
    tile = _elementwise_tile(rows, cols)
    ng = len(grads)
    m_scale = 1.0 - ADAM_B1 ** ADAM_STEP
    v_scale = 1.0 - ADAM_B2 ** ADAM_STEP

    def body(*refs):
        w_ref, m_ref, v_ref = refs[:3]
        g_refs = refs[3:3 + ng]
        if layer is None:
            g_out, d_out, m_out, v_out = refs[-4:]
        else:
            g_out, d_out, m_out, v_out, token = refs[-5:]
            token[...] = jnp.zeros_like(token)
        g = g_refs[0][...]
        for r in g_refs[1:]:
            g = g + r[...]
        m_new = ADAM_B1 * m_ref[...] + (1.0 - ADAM_B1) * g
        v_new = ADAM_B2 * v_ref[...] + (1.0 - ADAM_B2) * (g * g)
        m_hat = m_new / m_scale
        v_hat = v_new / v_scale
        g_out[...] = g
        d_out[...] = -ADAM_LR * (m_hat / (jnp.sqrt(v_hat) + ADAM_EPS) + ADAM_WD * w_ref[...])
        m_out[...] = m_new
        v_out[...] = v_new

    spec = pl.BlockSpec((tile, cols), lambda i: (i, 0))
    if layer is None:
        return pl.pallas_call(
            body, name=name, grid=(rows // tile,),
            in_specs=[spec] * (3 + ng), out_specs=[spec] * 4, out_shape=[_sds((rows, cols), F32)] * 4,
            compiler_params=_params("parallel"),
        )(w, m, v, *grads)
    of_layer = pl.BlockSpec((None, tile, cols), lambda i: (layer, i, 0))
    carried = [] if prev is None else list(prev[:4])
    return pl.pallas_call(
        body, name=name, grid=(rows // tile,),
        in_specs=[of_layer] * 3 + [spec] * ng + [pl.BlockSpec(memory_space=pl.ANY)] * len(carried),
        out_specs=[of_layer] * 4 + [pl.BlockSpec((SUBLANES, LANES), lambda i: (0, 0))],
        out_shape=[_sds(w.shape, F32)] * 4 + [_sds((SUBLANES, LANES), F32)],
        input_output_aliases={3 + ng + k: k for k in range(len(carried))},
        compiler_params=_params("arbitrary"),
    )(w, m, v, *grads, *carried)


def _position():
    return lax.axis_index("x"), lax.axis_index("y"), lax.axis_index("c")


def _other_chips(x, y):
    return [(1 - x, y), (x, 1 - y), (1 - x, 1 - y)]


def _exchange_chips(arrays, name, scatter):
    n = len(arrays)

    def body(*refs):
        src, dst = refs[:n], refs[n:2 * n]
        send_sems, recv_sems, local_sems = refs[2 * n:]
        x, y, c = _position()
        mine = 2 * x + y
        copies = []
        for i in range(n):
            own = src[i].at[mine] if scatter else src[i]
            copies.append(pltpu.make_async_copy(own, dst[i].at[mine], local_sems.at[i]))
        for k, (px, py) in enumerate(_other_chips(x, y)):
            for i in range(n):
                piece = src[i].at[2 * px + py] if scatter else src[i]
                copies.append(pltpu.make_async_remote_copy(
                    src_ref=piece, dst_ref=dst[i].at[mine], send_sem=send_sems.at[k, i], recv_sem=recv_sems.at[k, i],
                    device_id=(px, py, c), device_id_type=MESH))
        for cp in copies:
            cp.start()
        for cp in copies:
            cp.wait()

    def out_shape(a):
        return _sds(a.shape if scatter else (N_CHIPS,) + a.shape, a.dtype)

    return pl.pallas_call(
        body, name=name, in_specs=[HBM_SPEC] * n, out_specs=[HBM_SPEC] * n, out_shape=[out_shape(a) for a in arrays],
        scratch_shapes=[pltpu.SemaphoreType.DMA((N_CHIPS - 1, n)), pltpu.SemaphoreType.DMA((N_CHIPS - 1, n)),
                        pltpu.SemaphoreType.DMA((n,))],
    )(*arrays)


def _swap_sibling(arrays, name):
    n = len(arrays)

    def body(*refs):
        src, dst = refs[:n], refs[n:2 * n]
        send_sems, recv_sems = refs[2 * n:]
        x, y, c = _position()
        copies = [pltpu.make_async_remote_copy(
            src_ref=src[i], dst_ref=dst[i], send_sem=send_sems.at[i], recv_sem=recv_sems.at[i],
            device_id=(x, y, 1 - c), device_id_type=MESH) for i in range(n)]
        for cp in copies:
            cp.start()
        for cp in copies:
            cp.wait()

    return pl.pallas_call(
        body, name=name, in_specs=[HBM_SPEC] * n, out_specs=[HBM_SPEC] * n,
        out_shape=[_sds(a.shape, a.dtype) for a in arrays],
        scratch_shapes=[pltpu.SemaphoreType.DMA((n,)), pltpu.SemaphoreType.DMA((n,))],
    )(*arrays)


SEM_SPEC = pl.BlockSpec(memory_space=pltpu.SEMAPHORE)
ANY_SPEC = pl.BlockSpec(memory_space=pl.ANY)
IN_FLIGHT = pltpu.SideEffectType.DATAFLOW_SIDE_EFFECTING


def _peer_sems(all_devices):
    return pltpu.SemaphoreType.DMA(((N_DEV if all_devices else N_CHIPS) - 1,))


def _own_slot(all_devices, x, y, c):
    return 4 * x + 2 * y + c if all_devices else 2 * x + y


def _peers(all_devices, x, y, c):
    if not all_devices:
        return [((px, py, c), 2 * px + py) for px, py in _other_chips(x, y)]
    out = []
    for fx in range(2):
        for fy in range(2):
            for fc in range(2):
                if fx or fy or fc:
                    px, py, pc = (1 - x if fx else x), (1 - y if fy else y), (1 - c if fc else c)
                    out.append(((px, py, pc), 4 * px + 2 * py + pc))
    return out


def _in_hbm(a):
    return pltpu.with_memory_space_constraint(a, pltpu.HBM)


def _place_slot(a, slot, n_slots, name):
    rows, cols = a.shape
    tile = _elementwise_tile(rows, cols)

    def body(slot_ref, a_ref, o_ref):
        o_ref[...] = a_ref[...]

    grid_spec = pltpu.PrefetchScalarGridSpec(
        num_scalar_prefetch=1, grid=(rows // tile,),
        in_specs=[pl.BlockSpec((tile, cols), lambda i, slot_ref: (i, 0))],
        out_specs=pl.BlockSpec((None, tile, cols), lambda i, slot_ref: (slot_ref[0], i, 0)))
    return pl.pallas_call(
        body, name=name, grid_spec=grid_spec, out_shape=_sds((n_slots, rows, cols), a.dtype),
        compiler_params=_params("parallel"),
    )(slot, a)


def _place_shard(w, layer, chip, name):
    _, rows, cols = w.shape
    tile = _elementwise_tile(rows, cols)

    def body(chip_ref, w_ref, o_ref):
        o_ref[...] = w_ref[...].astype(BF16)

    grid_spec = pltpu.PrefetchScalarGridSpec(
        num_scalar_prefetch=1, grid=(rows // tile,),
        in_specs=[pl.BlockSpec((None, tile, cols), lambda i, chip_ref: (layer, i, 0))],
        out_specs=pl.BlockSpec((None, tile, cols), lambda i, chip_ref: (chip_ref[0], i, 0)))
    return pl.pallas_call(
        body, name=name, grid_spec=grid_spec, out_shape=_sds((N_CHIPS, rows, cols), BF16),
        compiler_params=_params("parallel"),
    )(chip, w)


def _gather_start(bufs, after, name, all_devices=False):
    n = len(bufs)

    def body(*refs):
        buf = refs[:n]
        send, recv = refs[n + 1:2 * n + 1], refs[2 * n + 1:3 * n + 1]
        token = refs[4 * n + 1]
        x, y, c = _position()
        mine = _own_slot(all_devices, x, y, c)
        for i in range(n):
            for k, (peer, _) in enumerate(_peers(all_devices, x, y, c)):
                pltpu.make_async_remote_copy(
                    src_ref=buf[i].at[mine], dst_ref=buf[i].at[mine], send_sem=send[i].at[k], recv_sem=recv[i].at[k],
                    device_id=peer, device_id_type=MESH).start()
        token[...] = jnp.zeros_like(token)

    sems = _peer_sems(all_devices)
    out = pl.pallas_call(
        body, name=name, in_specs=[HBM_SPEC] * n + [ANY_SPEC],
        out_specs=[SEM_SPEC] * (2 * n) + [HBM_SPEC] * n + [pl.BlockSpec(memory_space=pltpu.VMEM)],
        out_shape=[sems] * (2 * n) + [pltpu.HBM(b.shape, b.dtype) for b in bufs] + [_sds((SUBLANES, LANES), F32)],
        input_output_aliases={i: 2 * n + i for i in range(n)},
        compiler_params=pltpu.CompilerParams(has_side_effects=IN_FLIGHT),
    )(*[_in_hbm(b) for b in bufs], after)
    return [(out[i], out[n + i], out[2 * n + i]) for i in range(n)], out[3 * n]


def _gather_wait(send, recv, buf, after, name, all_devices=False):
    def body(buf_ref, send_ref, recv_ref, after_ref, out_ref):
        x, y, c = _position()
        mine = _own_slot(all_devices, x, y, c)
        for k, (peer, slot) in enumerate(_peers(all_devices, x, y, c)):
            cp = pltpu.make_async_remote_copy(
                src_ref=buf_ref.at[mine], dst_ref=buf_ref.at[slot], send_sem=send_ref.at[k], recv_sem=recv_ref.at[k],
                device_id=peer, device_id_type=MESH)
            cp.wait_send()
            cp.wait_recv()

    return pl.pallas_call(
        body, name=name, in_specs=[HBM_SPEC, SEM_SPEC, SEM_SPEC, ANY_SPEC], out_specs=HBM_SPEC,
        out_shape=pltpu.HBM(buf.shape, buf.dtype), input_output_aliases={0: 0},
        compiler_params=pltpu.CompilerParams(has_side_effects=IN_FLIGHT),
    )(buf, send, recv, after)


def _scatter_start(pieces, name):
    n = len(pieces)

    def body(*refs):
        src = refs[:n]
        send, recv = refs[n:2 * n], refs[2 * n:3 * n]
        land = refs[4 * n:5 * n]
        token = refs[5 * n]
        x, y, c = _position()
        mine = 2 * x + y
        for i in range(n):
            for k, (px, py) in enumerate(_other_chips(x, y)):
                pltpu.make_async_remote_copy(
                    src_ref=src[i].at[2 * px + py], dst_ref=land[i].at[mine], send_sem=send[i].at[k], recv_sem=recv[i].at[k],
                    device_id=(px, py, c), device_id_type=MESH).start()
        token[...] = jnp.zeros_like(token)

    hbm = [pltpu.HBM(p.shape, p.dtype) for p in pieces]
    out = pl.pallas_call(
        body, name=name, in_specs=[HBM_SPEC] * n,
        out_specs=[SEM_SPEC] * (2 * n) + [HBM_SPEC] * (2 * n) + [pl.BlockSpec(memory_space=pltpu.VMEM)],
        out_shape=[_peer_sems(False)] * (2 * n) + hbm + hbm + [_sds((SUBLANES, LANES), F32)],
        input_output_aliases={i: 2 * n + i for i in range(n)},
        compiler_params=pltpu.CompilerParams(has_side_effects=IN_FLIGHT),
    )(*[_in_hbm(p) for p in pieces])
    return [(out[i], out[n + i], out[2 * n + i], out[3 * n + i]) for i in range(n)], out[4 * n]


def _scatter_wait(send, recv, pieces, land, after, name):
    def body(src_ref, land_ref, send_ref, recv_ref, after_ref, src_out, land_out):
        x, y, c = _position()
        for k, (px, py) in enumerate(_other_chips(x, y)):
            cp = pltpu.make_async_remote_copy(
                src_ref=src_ref.at[2 * px + py], dst_ref=land_ref.at[2 * px + py], send_sem=send_ref.at[k],
                recv_sem=recv_ref.at[k], device_id=(px, py, c), device_id_type=MESH)
            cp.wait_send()
            cp.wait_recv()

    return pl.pallas_call(
        body, name=name, in_specs=[HBM_SPEC, HBM_SPEC, SEM_SPEC, SEM_SPEC, ANY_SPEC], out_specs=[HBM_SPEC, HBM_SPEC],
        out_shape=[pltpu.HBM(pieces.shape, pieces.dtype), pltpu.HBM(land.shape, land.dtype)],
        input_output_aliases={0: 0, 1: 1},
        compiler_params=pltpu.CompilerParams(has_side_effects=IN_FLIGHT),
    )(pieces, land, send, recv, after)


def _rope_tables(batch, rows_per_example):
    inv_freq = ROPE_THETA ** (-jnp.arange(0, 64, 2, dtype=F32) / 64)
    ang = jnp.arange(rows_per_example, dtype=F32)[:, None] * inv_freq[None, :]
    cos, sin = jnp.cos(ang), jnp.sin(ang)
    cos128 = jnp.concatenate([cos, cos, cos, cos], axis=1)
    sin128 = jnp.concatenate([-sin, sin, -sin, sin], axis=1)
    return jnp.tile(cos128, (batch, 1)), jnp.tile(sin128, (batch, 1))


def _block_diagonal(w):
    eye = jnp.eye(LRU_BLOCKS, dtype=w.dtype)
    return (w[:, :, None, :] * eye[:, None, :, None]).reshape(LRU_W, LRU_W)


def _diagonal_blocks(dense):
    d4 = dense.reshape(LRU_BLOCKS, LRU_BLOCK, LRU_BLOCKS, LRU_BLOCK)
    return jnp.stack([d4[n, :, n, :] for n in range(LRU_BLOCKS)])


def _row(v):
    return v.reshape(1, -1)


def _local_step(x, loss_target, meta_tokens, small, depth, big_weight, on_layer_grads):
    batch, seq, _ = x.shape
    n_real = N_META + seq
    n_blocks = -(-n_real // TIME_BLOCK)
    per_example = n_blocks * TIME_BLOCK
    pad = per_example - n_real
    meta = jnp.broadcast_to(meta_tokens[None], (batch, N_META, D_MODEL))
    h = jnp.concatenate([meta, x, jnp.zeros((batch, pad, D_MODEL), F32)], axis=1).reshape(batch * per_example, D_MODEL)
    target = jnp.pad(loss_target, ((0, 0), (N_META, pad), (0, 0))).reshape(batch * per_example, D_MODEL)
    rope_cos, rope_sin = _rope_tables(batch, per_example)

    saved = []
    for l in range(depth):
        wa = _block_diagonal(small['gate_a_w'][l]).astype(BF16)
        wx = _block_diagonal(small['gate_x_w'][l]).astype(BF16)
        lru_small = (small['conv_w'][l], _row(small['conv_b'][l]), wa, _row(small['gate_a_b'][l]), wx,
                     _row(small['gate_x_b'][l]), _row(small['lru_lambda'][l]))
        w_in = big_weight('w_in', l, h)
        z1, q, kv, xg = _in_proj_fwd(h, _row(small['pre_mix_norm'][l]), w_in, rope_cos, rope_sin)
        attn, lse = _attn_fwd(q, kv, small['attn_sinks'][l], batch, n_blocks)
        hs, lru = _lru_fwd(xg, *lru_small, batch, n_blocks)
        w_out = big_weight('w_out', l, xg)
        groups, o, h_mid = _out_proj_fwd(attn, lru, _row(small['attn_group_norm'][l]), _row(small['lru_group_norm'][l]),
                                         w_out, h, _row(small['post_mix_norm'][l]))
        w_up4 = big_weight('w_up', l, hs)
        w_down = big_weight('w_down', l, hs)
        z2, up, act, y, h_out = _mlp_fwd(h_mid, _row(small['pre_mlp_norm'][l]), w_up4, w_down,
                                         _row(small['post_mlp_norm'][l]))
        saved.append(dict(h=h, z1=z1, q=q, kv=kv, xg=xg, attn=attn, lse=lse, hs=hs, lru=lru, groups=groups, o=o,
                          h_mid=h_mid, z2=z2, up=up, act=act, y=y, lru_small=lru_small, w_in=w_in, w_out=w_out,
                          w_up4=w_up4, w_down=w_down))
        h = h_out

    dh, sq_err = _loss_head(h, target, batch, n_blocks, n_real)

    gs = {n: [None] * depth for n in REPLICATED_NAMES + ('conv_w',)}
    handed_over = None
    for l in reversed(range(depth)):
        s = saved[l]
        big = {}
        g_post_mlp = _row(small['post_mlp_norm'][l])
        if handed_over is not None:
            g_post_mlp = g_post_mlp + handed_over[0, 0]
        dh_mid, dy, dup, g_mlp = _mlp_bwd(dh, s['h_mid'], s['y'], s['up'], s['w_up4'], s['w_down'],
                                          _row(small['pre_mlp_norm'][l]), g_post_mlp)
        big['w_down'] = _matmul_tn(s['act'], dy, 512, D_MODEL, "grad_w_down", BF16)
        big['w_up'] = _matmul_tn(s['z2'], dup, 512, D_MODEL, "grad_w_up", BF16, column_blocks=True)
        do, d_attn, d_lru, g_out = _out_proj_bwd(dh_mid, s['o'], s['attn'], s['lru'], _row(small['attn_group_norm'][l]),
                                                 _row(small['lru_group_norm'][l]), _row(small['post_mix_norm'][l]),
                                                 s['w_out'])
        big['w_out'] = _matmul_tn(s['groups'], do, 512, D_MODEL, "grad_w_out", BF16)
        dxg, dwa, dwx, g_lru = _lru_bwd(s['xg'], s['hs'], d_lru, *s['lru_small'], batch, n_blocks)
        dq, dkv, dsink = _attn_bwd(s['q'], s['kv'], small['attn_sinks'][l], s['attn'], s['lse'], d_attn, batch, n_blocks)
        dh, dproj, g_in = _in_proj_bwd(dh_mid, s['h'], dq, dkv, dxg, rope_cos, rope_sin, _row(small['pre_mix_norm'][l]),
                                       s['w_in'])
        big['w_in'] = _matmul_tn(dproj, s['z1'], IN_COLS // 2, D_MODEL, "grad_w_in", BF16)
        handed_over = on_layer_grads(l, big)
        gs['pre_mlp_norm'][l], gs['post_mlp_norm'][l] = g_mlp[0], g_mlp[1]
        gs['post_mix_norm'][l] = g_out[0]
        gs['attn_group_norm'][l], gs['lru_group_norm'][l] = g_out[1, :ATTN_W], g_out[1, ATTN_W:]
        gs['pre_mix_norm'][l] = g_in[0]
        gs['attn_sinks'][l] = dsink[:, 0]
        gs['conv_w'][l] = g_lru[:CONV_TAPS]
        gs['conv_b'][l], gs['gate_a_b'][l], gs['gate_x_b'][l], gs['lru_lambda'][l] = g_lru[4], g_lru[5], g_lru[6], g_lru[7]
        gs['gate_a_w'][l] = _diagonal_blocks(dwa)
        gs['gate_x_w'][l] = _diagonal_blocks(dwx)

    grad_x = dh.reshape(batch, per_example, D_MODEL)[:, N_META:n_real]
    grad_meta = _meta_grad(dh, batch, per_example)
    small_grads = {n: jnp.stack(v) for n, v in gs.items()}
    return sq_err, grad_x, grad_meta, small_grads


PACK_UNIT = SUBLANES * LANES


def _size(shape):
    size = 1
    for d in shape:
        size *= d
    return size


def _pack(arrays):
    parts = []
    for a in arrays:
        flat = a.reshape(-1)
        padded = -(-flat.shape[0] // PACK_UNIT) * PACK_UNIT
        if padded != flat.shape[0]:
            flat = jnp.pad(flat, (0, padded - flat.shape[0]))
        parts.append(flat.reshape(-1, LANES))
    return jnp.concatenate(parts, axis=0)


def _unpack(buf, shapes):
    out, at = [], 0
    for shp in shapes:
        size = _size(shp)
        rows = -(-size // PACK_UNIT) * SUBLANES
        part = buf[at:at + rows]
        if rows * LANES != size:
            part = part.reshape(-1)[:size]
        out.append(part.reshape(shp))
        at += rows
    return out


def kernel(x, meta_tokens, pre_mix_norm, w_in, attn_sinks, conv_w, conv_b, gate_a_w, gate_a_b, gate_x_w, gate_x_b, lru_lambda, attn_group_norm, lru_group_norm, w_out, post_mix_norm, pre_mlp_norm, w_up, w_down, post_mlp_norm, loss_target, m_meta_tokens, m_pre_mix_norm, m_w_in, m_attn_sinks, m_conv_w, m_conv_b, m_gate_a_w, m_gate_a_b, m_gate_x_w, m_gate_x_b, m_lru_lambda, m_attn_group_norm, m_lru_group_norm, m_w_out, m_post_mix_norm, m_pre_mlp_norm, m_w_up, m_w_down, m_post_mlp_norm, v_meta_tokens, v_pre_mix_norm, v_w_in, v_attn_sinks, v_conv_w, v_conv_b, v_gate_a_w, v_gate_a_b, v_gate_x_w, v_gate_x_b, v_lru_lambda, v_attn_group_norm, v_lru_group_norm, v_w_out, v_post_mix_norm, v_pre_mlp_norm, v_w_up, v_w_down, v_post_mlp_norm):
    given = dict(locals())
    w = {n: given[n] for n in WEIGHT_NAMES}
    m = {n: given['m_' + n] for n in WEIGHT_NAMES}
    v = {n: given['v_' + n] for n in WEIGHT_NAMES}
    depth = w_in.shape[0]
    for d in (w, m, v):
        d['w_in'] = jnp.swapaxes(d['w_in'], 1, 2)
    chip = 2 * lax.axis_index("x") + lax.axis_index("y")
    chip1 = chip.reshape(1).astype(jnp.int32)

    in_flight = []
    all_started = chip1
    for l in range(depth):
        bufs = [_place_shard(w[n], l, chip1, "place_" + n) for n in BIG_NAMES]
        handles, all_started = _gather_start(bufs, all_started, "gather_start_%d" % l)
        in_flight.append(dict(zip(BIG_NAMES, handles)))

    def big_weight(name, l, after):
        send, recv, buf = in_flight[l][name]
        if l == 0 and name == 'w_in':
            after = all_started
        full = _gather_wait(send, recv, buf, after, "gather_wait_%d_%s" % (l, name))
        if name == 'w_in':
            return full.reshape(IN_COLS, D_MODEL)
        if name == 'w_out':
            return full.reshape(D_MODEL, D_MODEL)
        if name == 'w_down':
            return full.reshape(D_FF, D_MODEL)
        return full

    gathered = _exchange_chips([meta_tokens, conv_w], "gather_small_weights", scatter=False)
    full_meta = jnp.concatenate([gathered[0][s] for s in range(N_CHIPS)], axis=1)
    small = {n: w[n] for n in REPLICATED_NAMES}
    small['conv_w'] = jnp.concatenate([gathered[1][s] for s in range(N_CHIPS)], axis=2)

    scattering = [None] * depth

    def on_layer_grads(l, big):
        pieces = [big['w_in'].reshape(N_CHIPS, IN_COLS // N_CHIPS, D_MODEL),
                  big['w_out'].reshape(N_CHIPS, D_MODEL // N_CHIPS, D_MODEL), big['w_up'],
                  big['w_down'].reshape(N_CHIPS, D_MODEL, D_MODEL)]
        scattering[l], started = _scatter_start(pieces, "scatter_start_%d" % l)
        return started

    sq_err, grad_x, grad_meta, small_grads = _local_step(x, loss_target, full_meta, small, depth, big_weight, on_layer_grads)
    loss = lax.psum(sq_err[0, 0] * (0.5 / D_MODEL), ("x", "y", "c"))

    small_names = list(REPLICATED_NAMES) + list(COLUMN_SHARDED_SMALL)
    small_full = dict(small_grads)
    small_full['meta_tokens'] = grad_meta
    device1 = (2 * chip + lax.axis_index("c")).reshape(1).astype(jnp.int32)
    packed = _place_slot(_pack([small_full[n] for n in small_names]), device1, N_DEV, "place_small_grads")
    [small_flight], done = _gather_start([packed], grad_meta, "gather_small_grads_start", all_devices=True)

    results = {n: None for n in BIG_NAMES}
    for l in reversed(range(depth)):
        partial = []
        for n, (send, recv, pieces, land) in zip(BIG_NAMES, scattering[l]):
            pieces, land = _scatter_wait(send, recv, pieces, land, done, "scatter_wait_%d_%s" % (l, n))
            partial.append(_sum_pieces(pieces, land, chip1))
        sibling = _swap_sibling(partial, "swap_partial_grads")
        for n, mine, other in zip(BIG_NAMES, partial, sibling):
            results[n] = _adamw(w[n], m[n], v[n], [mine, other], "adamw_" + n, layer=l, prev=results[n])
        done = results['w_down'][4]

    slots = _gather_wait(*small_flight, done, "gather_small_grads_wait", all_devices=True)
    summed = _unpack(_sum_slots(slots, "sum_small_grads"), [small_full[n].shape for n in small_names])
    grads = dict(zip(small_names, summed))
    grads['meta_tokens'] = lax.dynamic_slice_in_dim(grads['meta_tokens'], chip * meta_tokens.shape[1], meta_tokens.shape[1], 1)
    grads['conv_w'] = lax.dynamic_slice_in_dim(grads['conv_w'], chip * conv_w.shape[2], conv_w.shape[2], 2)

    out_g, out_d, out_m, out_v = {}, {}, {}, {}
    for n in BIG_NAMES:
        out_g[n], out_d[n], out_m[n], out_v[n] = [jnp.swapaxes(r, 1, 2) if n == 'w_in' else r for r in results[n][:4]]
    shapes = [w[n].shape for n in small_names]
    res = _adamw(_pack([w[n] for n in small_names]), _pack([m[n] for n in small_names]), _pack([v[n] for n in small_names]),
                 [_pack([grads[n] for n in small_names])], "adamw_small")
    for k, store in enumerate((out_g, out_d, out_m, out_v)):
        for n, a in zip(small_names, _unpack(res[k], shapes)):
            store[n] = a

    return (loss, grad_x, *[out_g[n] for n in WEIGHT_NAMES], *[out_d[n] for n in WEIGHT_NAMES],
            *[out_m[n] for n in WEIGHT_NAMES], *[out_v[n] for n in WEIGHT_NAMES])
```

```python
import functools

import jax
import jax.numpy as jnp
from jax import lax
from jax.experimental import pallas as pl
from jax.experimental.pallas import tpu as pltpu

F32 = jnp.float32
BF16 = jnp.bfloat16

D_MODEL = 1024
N_HEADS = 8
ATTN_W = 512
KV_W = 128
LRU_W = 512
LRU_BLOCKS = 8
LRU_BLOCK = 64
IN_COLS = 1792
D_FF = 4096
N_META = 16
CONV_TAPS = 4
LRU_C = 8.0
ROPE_THETA = 10000.0
EPS = 1e-6
ATTN_SCALE = 0.125

ADAM_LR = 0.001
ADAM_B1 = 0.9
ADAM_B2 = 0.999
ADAM_EPS = 1e-08
ADAM_WD = 0.01
ADAM_STEP = 10

N_CHIPS = 4
N_DEV = 8
TIME_BLOCK = 128
ROW_TILE = 256
PROJ_ROW_TILE = 544
LANES = 128
SUBLANES = 8
MASKED = -1e30
VMEM_LIMIT = 56 * 1024 * 1024

MESH = pl.DeviceIdType.MESH
HBM_SPEC = pl.BlockSpec(memory_space=pltpu.HBM)

WEIGHT_NAMES = ['meta_tokens', 'pre_mix_norm', 'w_in', 'attn_sinks', 'conv_w', 'conv_b', 'gate_a_w', 'gate_a_b',
                'gate_x_w', 'gate_x_b', 'lru_lambda', 'attn_group_norm', 'lru_group_norm', 'w_out', 'post_mix_norm',
                'pre_mlp_norm', 'w_up', 'w_down', 'post_mlp_norm']
BIG_NAMES = ('w_in', 'w_out', 'w_up', 'w_down')
COLUMN_SHARDED_SMALL = ('meta_tokens', 'conv_w')
REPLICATED_NAMES = tuple(n for n in WEIGHT_NAMES if n not in BIG_NAMES and n not in COLUMN_SHARDED_SMALL)


def _sds(shape, dtype):
    return jax.ShapeDtypeStruct(tuple(shape), dtype)


def _params(*sem):
    return pltpu.CompilerParams(dimension_semantics=sem, vmem_limit_bytes=VMEM_LIMIT)


def _row_spec(width, tile=ROW_TILE):
    return pl.BlockSpec((tile, width), lambda i: (i, 0))


def _proj_tile(rows):
    tile = PROJ_ROW_TILE
    while rows % tile:
        tile -= 16
    return tile


def _whole_spec(a):
    nd = a.ndim
    return pl.BlockSpec(a.shape, lambda *_: (0,) * nd)


def _rms(x, g):
    r = lax.rsqrt(jnp.mean(x * x, axis=-1, keepdims=True) + EPS)
    return x * r * g


def _rms_bwd(dy, x, g):
    r = lax.rsqrt(jnp.mean(x * x, axis=-1, keepdims=True) + EPS)
    xh = x * r
    dg = jnp.sum(dy * xh, axis=0, keepdims=True)
    dxh = dy * g
    dx = r * (dxh - xh * jnp.mean(dxh * xh, axis=-1, keepdims=True))
    return dx, dg


def _rope(x, cos, sin_signed):
    width = x.shape[1]
    reps = width // LANES
    if reps > 1:
        cos = jnp.tile(cos, (1, reps))
        sin_signed = jnp.tile(sin_signed, (1, reps))
    lane = lax.broadcasted_iota(jnp.int32, x.shape, 1)
    first_half = jnp.bitwise_and(lane, 32) == 0
    other = jnp.where(first_half, pltpu.roll(x, width - 32, 1), pltpu.roll(x, 32, 1))
    return x * cos + other * sin_signed


def _sigmoid(x):
    return 1.0 / (1.0 + jnp.exp(-x))


def _log1p(e):
    return jnp.where(e < 1e-3, e * (1.0 - e * (0.5 - e * (1.0 / 3.0))), jnp.log(1.0 + e))


def _one_minus_exp(x):
    series = -x * (1.0 + x * (0.5 + x * ((1.0 / 6.0) + x * (1.0 / 24.0))))
    return jnp.where(x > -0.05, series, 1.0 - jnp.exp(x))


GELU_K = 0.7978845608028654
GELU_C = 0.044715


def _gelu(x):
    t = jnp.tanh(GELU_K * (x + GELU_C * x * x * x))
    return 0.5 * x * (1.0 + t)


def _gelu_and_grad(x):
    x2 = x * x
    t = jnp.tanh(GELU_K * (x + GELU_C * x * x2))
    val = 0.5 * x * (1.0 + t)
    grad = 0.5 * (1.0 + t) + 0.5 * x * (1.0 - t * t) * GELU_K * (1.0 + 3.0 * GELU_C * x2)
    return val, grad


def _dot(a, b):
    return jnp.dot(a, b, preferred_element_type=F32)


def _dot_nt(a, b):
    return lax.dot_general(a, b, (((1,), (1,)), ((), ())), preferred_element_type=F32)


def _dot_tn(a, b):
    return lax.dot_general(a, b, (((0,), (0,)), ((), ())), preferred_element_type=F32)


def _put_rows(rows_8, values):
    d = values[0].shape[1]
    rowid = lax.broadcasted_iota(jnp.int32, (rows_8, d), 0)
    out = jnp.zeros((rows_8, d), F32)
    for k, v in enumerate(values):
        out = out + jnp.where(rowid == k, v, 0.0)
    return out


def _in_proj_fwd(h, gain, w_in, rope_cos, rope_sin):
    rows = h.shape[0]

    def body(h_ref, g_ref, w_ref, c_ref, s_ref, z_ref, q_ref, kv_ref, xg_ref):
        z = _rms(h_ref[...], g_ref[...]).astype(BF16)
        z_ref[...] = z
        proj = _dot_nt(z, w_ref[...])
        cos = c_ref[...]
        sin = s_ref[...]
        q_ref[...] = _rope(proj[:, :ATTN_W], cos, sin).astype(BF16)
        kv_ref[:, :KV_W] = _rope(proj[:, ATTN_W:ATTN_W + KV_W], cos, sin).astype(BF16)
        kv_ref[:, KV_W:] = proj[:, ATTN_W + KV_W:ATTN_W + 2 * KV_W].astype(BF16)
        xg_ref[...] = proj[:, ATTN_W + 2 * KV_W:]

    tile = _proj_tile(rows)
    rs = functools.partial(_row_spec, tile=tile)
    return pl.pallas_call(
        body, name="in_proj_fwd", grid=(rows // tile,),
        in_specs=[rs(D_MODEL), _whole_spec(gain), _whole_spec(w_in), rs(LANES), rs(LANES)],
        out_specs=[rs(D_MODEL), rs(ATTN_W), rs(2 * KV_W), rs(2 * LRU_W)],
        out_shape=[_sds((rows, D_MODEL), BF16), _sds((rows, ATTN_W), BF16), _sds((rows, 2 * KV_W), BF16),
                   _sds((rows, 2 * LRU_W), F32)],
        compiler_params=_params("parallel"),
    )(h, gain, w_in, rope_cos, rope_sin)


def _in_proj_bwd(dh_mid, h, dq, dkv, dxg, rope_cos, rope_sin, gain, w_in):
    rows = h.shape[0]

    def body(dhm_ref, h_ref, dq_ref, dkv_ref, dxg_ref, c_ref, s_ref, g_ref, w_ref, dh_ref, dp_ref, gacc_ref):
        @pl.when(pl.program_id(0) == 0)
        def _():
            gacc_ref[...] = jnp.zeros_like(gacc_ref)

        cos = c_ref[...]
        sin = -s_ref[...]
        dp_ref[:, :ATTN_W] = _rope(dq_ref[...], cos, sin).astype(BF16)
        dp_ref[:, ATTN_W:ATTN_W + KV_W] = _rope(dkv_ref[:, :KV_W], cos, sin).astype(BF16)
        dp_ref[:, ATTN_W + KV_W:ATTN_W + 2 * KV_W] = dkv_ref[:, KV_W:].astype(BF16)
        dp_ref[:, ATTN_W + 2 * KV_W:] = dxg_ref[...]
        dz = _dot(dp_ref[...], w_ref[...])
        dx, dg = _rms_bwd(dz, h_ref[...], g_ref[...])
        dh_ref[...] = dhm_ref[...] + dx
        gacc_ref[...] += _put_rows(SUBLANES, [dg])

    tile = _proj_tile(rows)
    rs = functools.partial(_row_spec, tile=tile)
    return pl.pallas_call(
        body, name="in_proj_bwd", grid=(rows // tile,),
        in_specs=[rs(D_MODEL), rs(D_MODEL), rs(ATTN_W), rs(2 * KV_W), rs(2 * LRU_W),
                  rs(LANES), rs(LANES), _whole_spec(gain), _whole_spec(w_in)],
        out_specs=[rs(D_MODEL), rs(IN_COLS), pl.BlockSpec((SUBLANES, D_MODEL), lambda i: (0, 0))],
        out_shape=[_sds((rows, D_MODEL), F32), _sds((rows, IN_COLS), BF16), _sds((SUBLANES, D_MODEL), F32)],
        compiler_params=_params("arbitrary"),
    )(dh_mid, h, dq, dkv, dxg, rope_cos, rope_sin, gain, w_in)


def _kv_lane_variants(t, group):
    lane = lax.broadcasted_iota(jnp.int32, t.shape, 1)
    low = lane < 64
    swapped = pltpu.roll(t, 64, 1)
    if group == 0:
        lo, hi = jnp.where(low, t, 0.0), jnp.where(low, 0.0, swapped)
    else:
        lo, hi = jnp.where(low, swapped, 0.0), jnp.where(low, 0.0, t)
    return jnp.concatenate([lo, hi], axis=0).astype(BF16)


GROUP_ROWS = 2 * TIME_BLOCK
KEYS = 2 * TIME_BLOCK


def _window_mask(j):
    r = jnp.bitwise_and(lax.broadcasted_iota(jnp.int32, (GROUP_ROWS, KEYS), 0), TIME_BLOCK - 1)
    c = lax.broadcasted_iota(jnp.int32, (GROUP_ROWS, KEYS), 1)
    return (c > r) & (c <= r + TIME_BLOCK) & ((c >= TIME_BLOCK) | (j > 0))


def _group_rows(ref, group):
    lo = 2 * group * LANES
    return jnp.concatenate([ref[:, lo:lo + LANES], ref[:, lo + LANES:lo + 2 * LANES]], axis=0)


def _per_head(sink_ref, group, half):
    upper = lax.broadcasted_iota(jnp.int32, (GROUP_ROWS, 1), 0) < TIME_BLOCK
    return jnp.where(upper, sink_ref[4 * group + half], sink_ref[4 * group + 2 + half])


def _attn_fwd(q, kv, sinks, batch, n_blocks):
    rows = q.shape[0]

    def body(sink_ref, q_ref, kvc_ref, kvp_ref, o_ref, lse_ref):
        j = pl.program_id(1)
        kv2 = jnp.concatenate([kvp_ref[...], kvc_ref[...]], axis=0).astype(F32)
        mask = _window_mask(j)
        lane8 = lax.broadcasted_iota(jnp.int32, (TIME_BLOCK, N_HEADS), 1)
        lse_tile = jnp.zeros((TIME_BLOCK, N_HEADS), F32)
        for group in range(2):
            k_cat = _kv_lane_variants(kv2[:, :KV_W], group)
            v_cat = _kv_lane_variants(kv2[:, KV_W:], group)
            s_all = _dot_nt(_group_rows(q_ref, group), k_cat)
            probs = []
            for half in range(2):
                sink = _per_head(sink_ref, group, half)
                s = jnp.where(mask, s_all[:, half * KEYS:(half + 1) * KEYS] * ATTN_SCALE, MASKED)
                m = jnp.maximum(jnp.max(s, axis=1, keepdims=True), sink)
                e = jnp.exp(s - m)
                den = jnp.sum(e, axis=1, keepdims=True) + jnp.exp(sink - m)
                probs.append((e / den).astype(BF16))
                lse = m + jnp.log(den)
                lse_tile = lse_tile + jnp.where(lane8 == 4 * group + half, lse[:TIME_BLOCK], 0.0)
                lse_tile = lse_tile + jnp.where(lane8 == 4 * group + 2 + half, lse[TIME_BLOCK:], 0.0)
            out = _dot(jnp.concatenate(probs, axis=1), v_cat)
            o_ref[:, 2 * group * LANES:(2 * group + 1) * LANES] = out[:TIME_BLOCK]
            o_ref[:, (2 * group + 1) * LANES:(2 * group + 2) * LANES] = out[TIME_BLOCK:]
        lse_ref[...] = lse_tile

    def blk(width):
        return pl.BlockSpec((TIME_BLOCK, width), lambda b, j: (b * n_blocks + j, 0))

    prev = pl.BlockSpec((TIME_BLOCK, 2 * KV_W), lambda b, j: (b * n_blocks + jnp.maximum(j - 1, 0), 0))
    return pl.pallas_call(
        body, name="attn_fwd", grid=(batch, n_blocks),
        in_specs=[pl.BlockSpec(memory_space=pltpu.SMEM), blk(ATTN_W), blk(2 * KV_W), prev],
        out_specs=[blk(ATTN_W), blk(N_HEADS)],
        out_shape=[_sds((rows, ATTN_W), F32), _sds((rows, N_HEADS), F32)],
        compiler_params=_params("parallel", "parallel"),
    )(sinks, q, kv, kv)


def _attn_bwd(q, kv, sinks, out, lse, d_out, batch, n_blocks):
    rows = q.shape[0]

    def body(sink_ref, q_ref, kvc_ref, kvp_ref, o_ref, do_ref, lse_ref, dq_ref, dkv_ref, dsink_ref, carry):
        b = pl.program_id(0)
        j = pl.program_id(1)

        @pl.when((b == 0) & (j == 0))
        def _():
            dsink_ref[...] = jnp.zeros_like(dsink_ref)

        @pl.when(j < n_blocks)
        def _():
            kv2 = jnp.concatenate([kvp_ref[...], kvc_ref[...]], axis=0).astype(F32)
            mask = _window_mask(j)
            lane = lax.broadcasted_iota(jnp.int32, (GROUP_ROWS, LANES), 1)
            low = lax.broadcasted_iota(jnp.int32, (KEYS, LANES), 1) < 64
            upper = lax.broadcasted_iota(jnp.int32, (GROUP_ROWS, 1), 0) < TIME_BLOCK
            lse_tile = lse_ref[...]
            dk_tile = jnp.zeros((KEYS, KV_W), F32)
            dv_tile = jnp.zeros((KEYS, KV_W), F32)
            dsink_vals = {}
            for group in range(2):
                k_cat = _kv_lane_variants(kv2[:, :KV_W], group)
                v_cat = _kv_lane_variants(kv2[:, KV_W:], group)
                q_rows = _group_rows(q_ref, group)
                do_rows = _group_rows(do_ref, group)
                do_b = do_rows.astype(BF16)
                od = do_rows * _group_rows(o_ref, group)
                s_all = _dot_nt(q_rows, k_cat)
                dp_all = _dot_nt(do_b, v_cat)
                probs, dss = [], []
                for half in range(2):
                    heads = (4 * group + half, 4 * group + 2 + half)
                    sink = _per_head(sink_ref, group, half)
                    lse_h = jnp.concatenate([lse_tile[:, h:h + 1] for h in heads], axis=0)
                    in_half = (lane < 64) if half == 0 else (lane >= 64)
                    delta = jnp.sum(jnp.where(in_half, od, 0.0), axis=1, keepdims=True)
                    cols = slice(half * KEYS, (half + 1) * KEYS)
                    prob = jnp.exp(jnp.where(mask, s_all[:, cols] * ATTN_SCALE, MASKED) - lse_h)
                    probs.append(prob.astype(BF16))
                    dss.append((prob * (dp_all[:, cols] - delta) * ATTN_SCALE).astype(BF16))
                    dsink = -jnp.exp(sink - lse_h) * delta
                    dsink_vals[heads[0]] = jnp.sum(jnp.where(upper, dsink, 0.0), axis=0, keepdims=True)
                    dsink_vals[heads[1]] = jnp.sum(jnp.where(upper, 0.0, dsink), axis=0, keepdims=True)
                ds = jnp.concatenate(dss, axis=1)
                dq_rows = _dot(ds, k_cat)
                dq_ref[:, 2 * group * LANES:(2 * group + 1) * LANES] = dq_rows[:TIME_BLOCK]
                dq_ref[:, (2 * group + 1) * LANES:(2 * group + 2) * LANES] = dq_rows[TIME_BLOCK:]
                dk_cat = _dot_tn(ds, q_rows)
                dv_cat = _dot_tn(jnp.concatenate(probs, axis=1), do_b)
                if group == 0:
                    dk_tile = dk_tile + jnp.where(low, dk_cat[:KEYS] + pltpu.roll(dk_cat[KEYS:], 64, 1), 0.0)
                    dv_tile = dv_tile + jnp.where(low, dv_cat[:KEYS] + pltpu.roll(dv_cat[KEYS:], 64, 1), 0.0)
                else:
                    dk_tile = dk_tile + jnp.where(low, 0.0, pltpu.roll(dk_cat[:KEYS], 64, 1) + dk_cat[KEYS:])
                    dv_tile = dv_tile + jnp.where(low, 0.0, pltpu.roll(dv_cat[:KEYS], 64, 1) + dv_cat[KEYS:])

            @pl.when(j > 0)
            def _():
                dkv_ref[:, :KV_W] = carry[:, :KV_W] + dk_tile[:TIME_BLOCK]
                dkv_ref[:, KV_W:] = carry[:, KV_W:] + dv_tile[:TIME_BLOCK]

            carry[:, :KV_W] = dk_tile[TIME_BLOCK:]
            carry[:, KV_W:] = dv_tile[TIME_BLOCK:]
            rowid = lax.broadcasted_iota(jnp.int32, (N_HEADS, LANES), 0)
            upd = jnp.zeros((N_HEADS, LANES), F32)
            for head, val in dsink_vals.items():
                upd = upd + jnp.where(rowid == head, val, 0.0)
            dsink_ref[...] += upd

        @pl.when(j == n_blocks)
        def _():
            dkv_ref[...] = carry[...]

    last = n_blocks - 1

    def blk(width):
        return pl.BlockSpec((TIME_BLOCK, width), lambda b, j: (b * n_blocks + jnp.minimum(j, last), 0))

    prev = pl.BlockSpec((TIME_BLOCK, 2 * KV_W), lambda b, j: (b * n_blocks + jnp.maximum(jnp.minimum(j, last) - 1, 0), 0))
    dkv_spec = pl.BlockSpec((TIME_BLOCK, 2 * KV_W), lambda b, j: (b * n_blocks + jnp.maximum(j - 1, 0), 0))
    return pl.pallas_call(
        body, name="attn_bwd", grid=(batch, n_blocks + 1),
        in_specs=[pl.BlockSpec(memory_space=pltpu.SMEM), blk(ATTN_W), blk(2 * KV_W), prev, blk(ATTN_W), blk(ATTN_W),
                  blk(N_HEADS)],
        out_specs=[blk(ATTN_W), dkv_spec, pl.BlockSpec((N_HEADS, LANES), lambda b, j: (0, 0))],
        out_shape=[_sds((rows, ATTN_W), F32), _sds((rows, 2 * KV_W), F32), _sds((N_HEADS, LANES), F32)],
        scratch_shapes=[pltpu.VMEM((TIME_BLOCK, 2 * KV_W), F32)],
        compiler_params=_params("arbitrary", "arbitrary"),
    )(sinks, q, kv, kv, out, d_out, lse)


def _conv_taps(xb, prev8):
    ext = jnp.concatenate([prev8, xb], axis=0)
    n = ext.shape[0]
    return [xb] + [pltpu.roll(ext, k, 0)[SUBLANES:n] for k in range(1, CONV_TAPS)]


def _lru_gates(xc, wa, ba, wx, bx, lam):
    xcb = xc.astype(BF16)
    r = _sigmoid(_dot(xcb, wa) + ba)
    i = _sigmoid(_dot(xcb, wx) + bx)
    sp = jnp.maximum(-lam, 0.0) + _log1p(jnp.exp(-jnp.abs(lam)))
    log_a = -LRU_C * r * sp
    a = jnp.exp(log_a)
    mult = jnp.sqrt(_one_minus_exp(2.0 * log_a))
    return xcb, r, i, sp, a, mult


def _scan_fwd(a, u, h_before):
    n, d = a.shape
    groups = n // SUBLANES
    a = a.reshape(groups, SUBLANES, d)
    u = u.reshape(groups, SUBLANES, d)
    sub = lax.broadcasted_iota(jnp.int32, a.shape, 1)
    s = 1
    while s < SUBLANES:
        valid = sub >= s
        u = jnp.where(valid, u + a * pltpu.roll(u, s, 1), u)
        a = jnp.where(valid, a * pltpu.roll(a, s, 1), a)
        s *= 2
    out, prev = [], h_before
    for g in range(groups):
        out.append(u[g] + a[g] * prev)
        prev = out[-1][SUBLANES - 1:SUBLANES, :]
    return jnp.concatenate(out, axis=0)


def _scan_rev(cf, g, d_after):
    n, d = g.shape
    groups = n // SUBLANES
    cf = cf.reshape(groups, SUBLANES, d)
    g = g.reshape(groups, SUBLANES, d)
    sub = lax.broadcasted_iota(jnp.int32, g.shape, 1)
    s = 1
    while s < SUBLANES:
        valid = sub + s < SUBLANES
        g = jnp.where(valid, g + cf * pltpu.roll(g, SUBLANES - s, 1), g)
        cf = jnp.where(valid, cf * pltpu.roll(cf, SUBLANES - s, 1), cf)
        s *= 2
    out, nxt = [None] * groups, d_after
    for k in reversed(range(groups)):
        out[k] = g[k] + cf[k] * nxt
        nxt = out[k][0:1, :]
    return jnp.concatenate(out, axis=0)


def _lru_fwd(xg, conv_w, conv_b, wa, ba, wx, bx, lam, batch, n_blocks):
    rows = xg.shape[0]

    def body(xg_ref, cw_ref, cb_ref, wa_ref, ba_ref, wx_ref, bx_ref, lam_ref, hs_ref, lru_ref, x_prev, h_carry):
        @pl.when(pl.program_id(1) == 0)
        def _():
            x_prev[...] = jnp.zeros_like(x_prev)
            h_carry[...] = jnp.zeros_like(h_carry)

        xb = xg_ref[:, :LRU_W]
        taps = _conv_taps(xb, x_prev[...])
        xc = cb_ref[...] + sum(cw_ref[CONV_TAPS - 1 - k:CONV_TAPS - k, :] * taps[k] for k in range(CONV_TAPS))
        _, _, i, _, a, mult = _lru_gates(xc, wa_ref[...], ba_ref[...], wx_ref[...], bx_ref[...], lam_ref[...])
        h = _scan_fwd(a, mult * i * xc, h_carry[...])
        hs_ref[...] = h
        lru_ref[...] = h * _gelu(xg_ref[:, LRU_W:])
        h_carry[...] = h[TIME_BLOCK - 1:TIME_BLOCK, :]
        x_prev[...] = xb[TIME_BLOCK - SUBLANES:TIME_BLOCK, :]

    def blk(width):
        return pl.BlockSpec((TIME_BLOCK, width), lambda b, j: (b * n_blocks + j, 0))

    small = [conv_w, conv_b, wa, ba, wx, bx, lam]
    return pl.pallas_call(
        body, name="lru_fwd", grid=(batch, n_blocks),
        in_specs=[blk(2 * LRU_W)] + [_whole_spec(a) for a in small],
        out_specs=[blk(LRU_W), blk(LRU_W)],
        out_shape=[_sds((rows, LRU_W), F32), _sds((rows, LRU_W), F32)],
        scratch_shapes=[pltpu.VMEM((SUBLANES, LRU_W), F32), pltpu.VMEM((1, LRU_W), F32)],
        compiler_params=_params("arbitrary", "arbitrary"),
    )(xg, *small)


def _lru_bwd(xg, hs, d_lru, conv_w, conv_b, wa, ba, wx, bx, lam, batch, n_blocks):
    rows = xg.shape[0]
    last_row = TIME_BLOCK - 1

    def body(xg_ref, xgh_ref, hs_ref, hsh_ref, dl_ref, cw_ref, cb_ref, wa_ref, ba_ref, wx_ref, bx_ref, lam_ref,
             dxg_ref, dwa_ref, dwx_ref, vec_ref, dh_carry, dxc_next):
        b = pl.program_id(0)
        j = pl.program_id(1)

        @pl.when((b == 0) & (j == 0))
        def _():
            dwa_ref[...] = jnp.zeros_like(dwa_ref)
            dwx_ref[...] = jnp.zeros_like(dwx_ref)
            vec_ref[...] = jnp.zeros_like(vec_ref)

        @pl.when(j == 0)
        def _():
            dh_carry[...] = jnp.zeros_like(dh_carry)
            dxc_next[...] = jnp.zeros_like(dxc_next)

        first = j == n_blocks - 1
        xb = xg_ref[:, :LRU_W]
        prev8 = jnp.where(first, 0.0, xgh_ref[:, :LRU_W])
        h_before = jnp.where(first, 0.0, hsh_ref[SUBLANES - 1:SUBLANES, :])
        cw = cw_ref[...]
        lam = lam_ref[...]
        wa = wa_ref[...]
        wx = wx_ref[...]
        taps = _conv_taps(xb, prev8)
        xc = cb_ref[...] + sum(cw[CONV_TAPS - 1 - k:CONV_TAPS - k, :] * taps[k] for k in range(CONV_TAPS))
        xcb, r, i, sp, a, mult = _lru_gates(xc, wa, ba_ref[...], wx, bx_ref[...], lam)
        hs = hs_ref[...]
        row = lax.broadcasted_iota(jnp.int32, hs.shape, 0)
        h_prev = jnp.where(row == 0, h_before, pltpu.roll(hs, 1, 0))
        dl = dl_ref[...]
        gate, dgate = _gelu_and_grad(xg_ref[:, LRU_W:])
        dxg_ref[:, LRU_W:] = (dl * hs * dgate).astype(BF16)
        cf = jnp.where(row == last_row, 1.0, pltpu.roll(a, last_row, 0))
        dh = _scan_rev(cf, dl * gate, dh_carry[...])
        dh_carry[...] = a[0:1, :] * dh[0:1, :]
        dmult = dh * i * xc
        di = dh * mult * xc
        dxc = dh * mult * i
        dlog_a = dh * h_prev * a - dmult * (a * a / mult)
        dr = dlog_a * (-LRU_C * sp)
        dlam = jnp.sum(dlog_a * (-LRU_C * r), axis=0, keepdims=True) * (-_sigmoid(-lam))
        dpr = dr * r * (1.0 - r)
        dpi = di * i * (1.0 - i)
        dprb = dpr.astype(BF16)
        dpib = dpi.astype(BF16)
        dxc = dxc + _dot_nt(dprb, wa) + _dot_nt(dpib, wx)
        dwa_ref[...] += _dot_tn(xcb, dprb)
        dwx_ref[...] += _dot_tn(xcb, dpib)
        ext = jnp.concatenate([dxc, dxc_next[...]], axis=0)
        n = ext.shape[0]
        dxb = cw[CONV_TAPS - 1:CONV_TAPS, :] * dxc
        for k in range(1, CONV_TAPS):
            dxb = dxb + cw[CONV_TAPS - 1 - k:CONV_TAPS - k, :] * pltpu.roll(ext, n - k, 0)[:TIME_BLOCK]
        dxg_ref[:, :LRU_W] = dxb.astype(BF16)
        dxc_next[...] = dxc[:SUBLANES, :]
        vecs = [jnp.sum(dxc * taps[CONV_TAPS - 1 - t], axis=0, keepdims=True) for t in range(CONV_TAPS)]
        vecs += [jnp.sum(dxc, axis=0, keepdims=True), jnp.sum(dpr, axis=0, keepdims=True),
                 jnp.sum(dpi, axis=0, keepdims=True), dlam]
        vec_ref[...] += _put_rows(SUBLANES, vecs)

    def tblk(b, j):
        return b * n_blocks + n_blocks - 1 - j

    def blk(width):
        return pl.BlockSpec((TIME_BLOCK, width), lambda b, j: (tblk(b, j), 0))

    per8 = TIME_BLOCK // SUBLANES

    def halo(width):
        return pl.BlockSpec((SUBLANES, width), lambda b, j: (jnp.maximum(per8 * tblk(b, j) - 1, 0), 0))

    small = [conv_w, conv_b, wa, ba, wx, bx, lam]
    acc = lambda shape: pl.BlockSpec(shape, lambda b, j: (0, 0))
    return pl.pallas_call(
        body, name="lru_bwd", grid=(batch, n_blocks),
        in_specs=[blk(2 * LRU_W), halo(2 * LRU_W), blk(LRU_W), halo(LRU_W), blk(LRU_W)] + [_whole_spec(a) for a in small],
        out_specs=[blk(2 * LRU_W), acc((LRU_W, LRU_W)), acc((LRU_W, LRU_W)), acc((SUBLANES, LRU_W))],
        out_shape=[_sds((rows, 2 * LRU_W), BF16), _sds((LRU_W, LRU_W), F32), _sds((LRU_W, LRU_W), F32),
                   _sds((SUBLANES, LRU_W), F32)],
        scratch_shapes=[pltpu.VMEM((1, LRU_W), F32), pltpu.VMEM((SUBLANES, LRU_W), F32)],
        compiler_params=_params("arbitrary", "arbitrary"),
    )(xg, xg, hs, hs, d_lru, *small)


def _out_proj_fwd(attn, lru, g_attn, g_lru, w_out, h, g_post):
    rows = h.shape[0]

    def body(at_ref, lr_ref, ga_ref, gl_ref, w_ref, h_ref, gp_ref, grp_ref, o_ref, hm_ref):
        a = _rms(at_ref[...], ga_ref[...]).astype(BF16)
        l = _rms(lr_ref[...], gl_ref[...]).astype(BF16)
        grp_ref[:, :ATTN_W] = a
        grp_ref[:, ATTN_W:] = l
        o = _dot(a, w_ref[:ATTN_W, :]) + _dot(l, w_ref[ATTN_W:, :])
        o_ref[...] = o
        hm_ref[...] = h_ref[...] + _rms(o, gp_ref[...])

    tile = _proj_tile(rows)
    rs = functools.partial(_row_spec, tile=tile)
    return pl.pallas_call(
        body, name="out_proj_fwd", grid=(rows // tile,),
        in_specs=[rs(ATTN_W), rs(LRU_W), _whole_spec(g_attn), _whole_spec(g_lru), _whole_spec(w_out),
                  rs(D_MODEL), _whole_spec(g_post)],
        out_specs=[rs(D_MODEL), rs(D_MODEL), rs(D_MODEL)],
        out_shape=[_sds((rows, D_MODEL), BF16), _sds((rows, D_MODEL), F32), _sds((rows, D_MODEL), F32)],
        compiler_params=_params("parallel"),
    )(attn, lru, g_attn, g_lru, w_out, h, g_post)


def _out_proj_bwd(dh_mid, o, attn, lru, g_attn, g_lru, g_post, w_out):
    rows = o.shape[0]

    def body(dhm_ref, o_ref, at_ref, lr_ref, ga_ref, gl_ref, gp_ref, w_ref, do_ref, dat_ref, dlr_ref, gacc_ref):
        @pl.when(pl.program_id(0) == 0)
        def _():
            gacc_ref[...] = jnp.zeros_like(gacc_ref)

        do, dgp = _rms_bwd(dhm_ref[...], o_ref[...], gp_ref[...])
        dob = do.astype(BF16)
        do_ref[...] = dob
        dat, dga = _rms_bwd(_dot_nt(dob, w_ref[:ATTN_W, :]), at_ref[...], ga_ref[...])
        dlr, dgl = _rms_bwd(_dot_nt(dob, w_ref[ATTN_W:, :]), lr_ref[...], gl_ref[...])
        dat_ref[...] = dat
        dlr_ref[...] = dlr
        gacc_ref[...] += _put_rows(SUBLANES, [dgp, jnp.concatenate([dga, dgl], axis=1)])

    tile = _proj_tile(rows)
    rs = functools.partial(_row_spec, tile=tile)
    return pl.pallas_call(
        body, name="out_proj_bwd", grid=(rows // tile,),
        in_specs=[rs(D_MODEL), rs(D_MODEL), rs(ATTN_W), rs(LRU_W), _whole_spec(g_attn),
                  _whole_spec(g_lru), _whole_spec(g_post), _whole_spec(w_out)],
        out_specs=[rs(D_MODEL), rs(ATTN_W), rs(LRU_W),
                   pl.BlockSpec((SUBLANES, D_MODEL), lambda i: (0, 0))],
        out_shape=[_sds((rows, D_MODEL), BF16), _sds((rows, ATTN_W), F32), _sds((rows, LRU_W), F32),
                   _sds((SUBLANES, D_MODEL), F32)],
        compiler_params=_params("arbitrary"),
    )(dh_mid, o, attn, lru, g_attn, g_lru, g_post, w_out)


def _mlp_fwd(h_mid, g_pre, w_up4, w_down, g_post):
    rows = h_mid.shape[0]

    def body(h_ref, g1_ref, wu_ref, wd_ref, g2_ref, z_ref, up_ref, act_ref, y_ref, ho_ref):
        x = h_ref[...]
        z = _rms(x, g1_ref[...]).astype(BF16)
        z_ref[...] = z
        y = jnp.zeros((ROW_TILE, D_MODEL), F32)
        for s in range(N_CHIPS):
            cols = slice(s * D_MODEL, (s + 1) * D_MODEL)
            u = _dot(z, wu_ref[s])
            up_ref[:, cols] = u
            a = jnp.square(jnp.maximum(u, 0.0)).astype(BF16)
            act_ref[:, cols] = a
            y = y + _dot(a, wd_ref[cols, :])
        y_ref[...] = y
        ho_ref[...] = x + _rms(y, g2_ref[...])

    return pl.pallas_call(
        body, name="mlp_fwd", grid=(rows // ROW_TILE,),
        in_specs=[_row_spec(D_MODEL), _whole_spec(g_pre), _whole_spec(w_up4), _whole_spec(w_down), _whole_spec(g_post)],
        out_specs=[_row_spec(D_MODEL), _row_spec(D_FF), _row_spec(D_FF), _row_spec(D_MODEL), _row_spec(D_MODEL)],
        out_shape=[_sds((rows, D_MODEL), BF16), _sds((rows, D_FF), F32), _sds((rows, D_FF), BF16),
                   _sds((rows, D_MODEL), F32), _sds((rows, D_MODEL), F32)],
        compiler_params=_params("parallel"),
    )(h_mid, g_pre, w_up4, w_down, g_post)


def _mlp_bwd(dh_out, h_mid, y, up, w_up4, w_down, g_pre, g_post):
    rows = y.shape[0]

    def body(dh_ref, hm_ref, y_ref, up_ref, wu_ref, wd_ref, g1_ref, g2_ref, dhm_ref, dy_ref, dup_ref, gacc_ref):
        @pl.when(pl.program_id(0) == 0)
        def _():
            gacc_ref[...] = jnp.zeros_like(gacc_ref)

        dh = dh_ref[...]
        dy, dg2 = _rms_bwd(dh, y_ref[...], g2_ref[...])
        dyb = dy.astype(BF16)
        dy_ref[...] = dyb
        dz = jnp.zeros((ROW_TILE, D_MODEL), F32)
        for s in range(N_CHIPS):
            cols = slice(s * D_MODEL, (s + 1) * D_MODEL)
            dact = _dot_nt(dyb, wd_ref[cols, :])
            dup = (dact * (2.0 * jnp.maximum(up_ref[:, cols], 0.0))).astype(BF16)
            dup_ref[:, cols] = dup
            dz = dz + _dot_nt(dup, wu_ref[s])
        dx, dg1 = _rms_bwd(dz, hm_ref[...], g1_ref[...])
        dhm_ref[...] = dh + dx
        gacc_ref[...] += _put_rows(SUBLANES, [dg1, dg2])

    return pl.pallas_call(
        body, name="mlp_bwd", grid=(rows // ROW_TILE,),
        in_specs=[_row_spec(D_MODEL), _row_spec(D_MODEL), _row_spec(D_MODEL), _row_spec(D_FF), _whole_spec(w_up4),
                  _whole_spec(w_down), _whole_spec(g_pre), _whole_spec(g_post)],
        out_specs=[_row_spec(D_MODEL), _row_spec(D_MODEL), _row_spec(D_FF),
                   pl.BlockSpec((SUBLANES, D_MODEL), lambda i: (0, 0))],
        out_shape=[_sds((rows, D_MODEL), F32), _sds((rows, D_MODEL), BF16), _sds((rows, D_FF), BF16),
                   _sds((SUBLANES, D_MODEL), F32)],
        compiler_params=_params("arbitrary"),
    )(dh_out, h_mid, y, up, w_up4, w_down, g_pre, g_post)


def _matmul_tn(a, b, tm, tn, name, out_dtype, column_blocks=False):
    rows, m = a.shape
    n = b.shape[1]

    def body(a_ref, b_ref, o_ref):
        o_ref[...] = _dot_tn(a_ref[...], b_ref[...]).astype(out_dtype)

    if column_blocks:
        out_spec = pl.BlockSpec((None, tm, tn), lambda i, j: (j, i, 0))
        out_shape = _sds((n // tn, m, tn), out_dtype)
    else:
        out_spec = pl.BlockSpec((tm, tn), lambda i, j: (i, j))
        out_shape = _sds((m, n), out_dtype)
    return pl.pallas_call(
        body, name=name, grid=(m // tm, n // tn),
        in_specs=[pl.BlockSpec((rows, tm), lambda i, j: (0, i)), pl.BlockSpec((rows, tn), lambda i, j: (0, j))],
        out_specs=out_spec, out_shape=out_shape,
        compiler_params=_params("parallel", "parallel"),
    )(a, b)


LOSS_COLS = 256


def _loss_head(h, loss_target):
    batch, per_example, _ = h.shape
    seq = loss_target.shape[1]
    n_real = N_META + seq
    assert seq % SUBLANES == 0

    def body(h_ref, t_ref, dh_ref, l_ref):
        @pl.when((pl.program_id(0) == 0) & (pl.program_id(1) == 0))
        def _():
            l_ref[...] = jnp.zeros_like(l_ref)

        d = h_ref[N_META:n_real, :] - t_ref[...]
        dh_ref[:N_META, :] = jnp.zeros((N_META, LOSS_COLS), F32)
        dh_ref[N_META:n_real, :] = d * (1.0 / D_MODEL)
        dh_ref[n_real:, :] = jnp.zeros((per_example - n_real, LOSS_COLS), F32)
        l_ref[...] += jnp.sum(jnp.sum(d * d, axis=0, keepdims=True), axis=1, keepdims=True)

    blk = pl.BlockSpec((None, per_example, LOSS_COLS), lambda b, j: (b, 0, j))
    return pl.pallas_call(
        body, name="loss_head", grid=(batch, D_MODEL // LOSS_COLS),
        in_specs=[blk, pl.BlockSpec((None, seq, LOSS_COLS), lambda b, j: (b, 0, j))],
        out_specs=[blk, pl.BlockSpec((SUBLANES, LANES), lambda b, j: (0, 0))],
        out_shape=[_sds(h.shape, F32), _sds((SUBLANES, LANES), F32)],
        compiler_params=_params("arbitrary", "arbitrary"),
    )(h, loss_target)


def _meta_grad(dh0, batch, rows_per_example):
    per = rows_per_example // N_META

    def body(d_ref, o_ref):
        @pl.when(pl.program_id(0) == 0)
        def _():
            o_ref[...] = jnp.zeros_like(o_ref)

        o_ref[...] += d_ref[...]

    return pl.pallas_call(
        body, name="meta_grad", grid=(batch,),
        in_specs=[pl.BlockSpec((N_META, D_MODEL), lambda b: (b * per, 0))],
        out_specs=pl.BlockSpec((N_META, D_MODEL), lambda b: (0, 0)),
        out_shape=_sds((N_META, D_MODEL), F32),
        compiler_params=_params("arbitrary"),
    )(dh0)


def _elementwise_tile(rows, cols):
    tile = rows
    while tile * cols * 4 > (1 << 20) and tile % 16 == 0:
        tile //= 2
    return tile


def _sum_slots(buf, name):
    k, rows, cols = buf.shape
    tile = _elementwise_tile(rows, cols)

    def body(*refs):
        total = refs[0][...].astype(F32)
        for r in refs[1:k]:
            total = total + r[...].astype(F32)
        refs[k][...] = total

    def slot(s):
        return pl.BlockSpec((None, tile, cols), lambda i: (s, i, 0))

    return pl.pallas_call(
        body, name=name, grid=(rows // tile,),
        in_specs=[slot(s) for s in range(k)], out_specs=pl.BlockSpec((tile, cols), lambda i: (i, 0)),
        out_shape=_sds((rows, cols), F32), compiler_params=_params("parallel"),
    )(*([buf] * k))


def _sum_pieces(pieces, landed, chip):
    _, rows, cols = pieces.shape
    tile = _elementwise_tile(rows, cols)

    def body(chip_ref, own_ref, a_ref, b_ref, c_ref, o_ref):
        o_ref[...] = ((own_ref[...].astype(F32) + a_ref[...].astype(F32)) + b_ref[...].astype(F32)) + c_ref[...].astype(F32)

    def slot(offset):
        return pl.BlockSpec((None, tile, cols), lambda i, chip_ref: ((chip_ref[0] + offset) % N_CHIPS, i, 0))

    grid_spec = pltpu.PrefetchScalarGridSpec(
        num_scalar_prefetch=1, grid=(rows // tile,), in_specs=[slot(0), slot(1), slot(2), slot(3)],
        out_specs=pl.BlockSpec((tile, cols), lambda i, chip_ref: (i, 0)))
    return pl.pallas_call(
        body, name="sum_grad_pieces", grid_spec=grid_spec, out_shape=_sds((rows, cols), F32),
        compiler_params=_params("parallel"),
    )(chip, pieces, landed, landed, landed)


def _adamw(w, m, v, grads, name, layer=None, prev=None):
    rows, cols = grads[0].shape
    tile = _elementwise_tile(rows, cols)
    ng = len(grads)
    m_scale = 1.0 - ADAM_B1 ** ADAM_STEP
    v_scale = 1.0 - ADAM_B2 ** ADAM_STEP

    def body(*refs):
        w_ref, m_ref, v_ref = refs[:3]
        g_refs = refs[3:3 + ng]
        if layer is None:
            g_out, d_out, m_out, v_out = refs[-4:]
        else:
            g_out, d_out, m_out, v_out, token = refs[-5:]
            token[...] = jnp.zeros_like(token)
        g = g_refs[0][...]
        for r in g_refs[1:]:
            g = g + r[...]
        m_new = ADAM_B1 * m_ref[...] + (1.0 - ADAM_B1) * g
        v_new = ADAM_B2 * v_ref[...] + (1.0 - ADAM_B2) * (g * g)
        m_hat = m_new / m_scale
        v_hat = v_new / v_scale
        g_out[...] = g
        d_out[...] = -ADAM_LR * (m_hat / (jnp.sqrt(v_hat) + ADAM_EPS) + ADAM_WD * w_ref[...])
        m_out[...] = m_new
        v_out[...] = v_new

    spec = pl.BlockSpec((tile, cols), lambda i: (i, 0))
    if layer is None:
        return pl.pallas_call(
            body, name=name, grid=(rows // tile,),
            in_specs=[spec] * (3 + ng), out_specs=[spec] * 4, out_shape=[_sds((rows, cols), F32)] * 4,
            compiler_params=_params("parallel"),
        )(w, m, v, *grads)
    of_layer = pl.BlockSpec((None, tile, cols), lambda i: (layer, i, 0))
    carried = [] if prev is None else list(prev[:4])
    return pl.pallas_call(
        body, name=name, grid=(rows // tile,),
        in_specs=[of_layer] * 3 + [spec] * ng + [pl.BlockSpec(memory_space=pl.ANY)] * len(carried),
        out_specs=[of_layer] * 4 + [pl.BlockSpec((SUBLANES, LANES), lambda i: (0, 0))],
        out_shape=[_sds(w.shape, F32)] * 4 + [_sds((SUBLANES, LANES), F32)],
        input_output_aliases={3 + ng + k: k for k in range(len(carried))},
        compiler_params=_params("arbitrary"),
    )(w, m, v, *grads, *carried)


def _position():
    return lax.axis_index("x"), lax.axis_index("y"), lax.axis_index("c")


def _other_chips(x, y):
    return [(1 - x, y), (x, 1 - y), (1 - x, 1 - y)]


def _exchange_chips(arrays, name, scatter):
    n = len(arrays)

    def body(*refs):
        src, dst = refs[:n], refs[n:2 * n]
        send_sems, recv_sems, local_sems = refs[2 * n:]
        x, y, c = _position()
        mine = 2 * x + y
        copies = []
        for i in range(n):
            own = src[i].at[mine] if scatter else src[i]
            copies.append(pltpu.make_async_copy(own, dst[i].at[mine], local_sems.at[i]))
        for k, (px, py) in enumerate(_other_chips(x, y)):
            for i in range(n):
                piece = src[i].at[2 * px + py] if scatter else src[i]
                copies.append(pltpu.make_async_remote_copy(
                    src_ref=piece, dst_ref=dst[i].at[mine], send_sem=send_sems.at[k, i], recv_sem=recv_sems.at[k, i],
                    device_id=(px, py, c), device_id_type=MESH))
        for cp in copies:
            cp.start()
        for cp in copies:
            cp.wait()

    def out_shape(a):
        return _sds(a.shape if scatter else (N_CHIPS,) + a.shape, a.dtype)

    return pl.pallas_call(
        body, name=name, in_specs=[HBM_SPEC] * n, out_specs=[HBM_SPEC] * n, out_shape=[out_shape(a) for a in arrays],
        scratch_shapes=[pltpu.SemaphoreType.DMA((N_CHIPS - 1, n)), pltpu.SemaphoreType.DMA((N_CHIPS - 1, n)),
                        pltpu.SemaphoreType.DMA((n,))],
    )(*arrays)


def _swap_sibling(arrays, name):
    n = len(arrays)

    def body(*refs):
        src, dst = refs[:n], refs[n:2 * n]
        send_sems, recv_sems = refs[2 * n:]
        x, y, c = _position()
        copies = [pltpu.make_async_remote_copy(
            src_ref=src[i], dst_ref=dst[i], send_sem=send_sems.at[i], recv_sem=recv_sems.at[i],
            device_id=(x, y, 1 - c), device_id_type=MESH) for i in range(n)]
        for cp in copies:
            cp.start()
        for cp in copies:
            cp.wait()

    return pl.pallas_call(
        body, name=name, in_specs=[HBM_SPEC] * n, out_specs=[HBM_SPEC] * n,
        out_shape=[_sds(a.shape, a.dtype) for a in arrays],
        scratch_shapes=[pltpu.SemaphoreType.DMA((n,)), pltpu.SemaphoreType.DMA((n,))],
    )(*arrays)


SEM_SPEC = pl.BlockSpec(memory_space=pltpu.SEMAPHORE)
ANY_SPEC = pl.BlockSpec(memory_space=pl.ANY)
IN_FLIGHT = pltpu.SideEffectType.DATAFLOW_SIDE_EFFECTING


def _peer_sems(all_devices):
    return pltpu.SemaphoreType.DMA(((N_DEV if all_devices else N_CHIPS) - 1,))


def _own_slot(all_devices, x, y, c):
    return 4 * x + 2 * y + c if all_devices else 2 * x + y


def _peers(all_devices, x, y, c):
    if not all_devices:
        return [((px, py, c), 2 * px + py) for px, py in _other_chips(x, y)]
    out = []
    for fx in range(2):
        for fy in range(2):
            for fc in range(2):
                if fx or fy or fc:
                    px, py, pc = (1 - x if fx else x), (1 - y if fy else y), (1 - c if fc else c)
                    out.append(((px, py, pc), 4 * px + 2 * py + pc))
    return out


def _in_hbm(a):
    return pltpu.with_memory_space_constraint(a, pltpu.HBM)


def _place_slot(a, slot, n_slots, name):
    rows, cols = a.shape
    tile = _elementwise_tile(rows, cols)

    def body(slot_ref, a_ref, o_ref):
        o_ref[...] = a_ref[...]

    grid_spec = pltpu.PrefetchScalarGridSpec(
        num_scalar_prefetch=1, grid=(rows // tile,),
        in_specs=[pl.BlockSpec((tile, cols), lambda i, slot_ref: (i, 0))],
        out_specs=pl.BlockSpec((None, tile, cols), lambda i, slot_ref: (slot_ref[0], i, 0)))
    return pl.pallas_call(
        body, name=name, grid_spec=grid_spec, out_shape=_sds((n_slots, rows, cols), a.dtype),
        compiler_params=_params("parallel"),
    )(slot, a)


def _place_shard(w, layer, chip, name):
    _, rows, cols = w.shape
    tile = _elementwise_tile(rows, cols)

    def body(chip_ref, w_ref, o_ref):
        o_ref[...] = w_ref[...].astype(BF16)

    grid_spec = pltpu.PrefetchScalarGridSpec(
        num_scalar_prefetch=1, grid=(rows // tile,),
        in_specs=[pl.BlockSpec((None, tile, cols), lambda i, chip_ref: (layer, i, 0))],
        out_specs=pl.BlockSpec((None, tile, cols), lambda i, chip_ref: (chip_ref[0], i, 0)))
    return pl.pallas_call(
        body, name=name, grid_spec=grid_spec, out_shape=_sds((N_CHIPS, rows, cols), BF16),
        compiler_params=_params("parallel"),
    )(chip, w)


def _gather_start(bufs, after, name, all_devices=False):
    n = len(bufs)

    def body(*refs):
        buf = refs[:n]
        send, recv = refs[n + 1:2 * n + 1], refs[2 * n + 1:3 * n + 1]
        token = refs[4 * n + 1]
        x, y, c = _position()
        mine = _own_slot(all_devices, x, y, c)
        for i in range(n):
            for k, (peer, _) in enumerate(_peers(all_devices, x, y, c)):
                pltpu.make_async_remote_copy(
                    src_ref=buf[i].at[mine], dst_ref=buf[i].at[mine], send_sem=send[i].at[k], recv_sem=recv[i].at[k],
                    device_id=peer, device_id_type=MESH).start()
        token[...] = jnp.zeros_like(token)

    sems = _peer_sems(all_devices)
    out = pl.pallas_call(
        body, name=name, in_specs=[HBM_SPEC] * n + [ANY_SPEC],
        out_specs=[SEM_SPEC] * (2 * n) + [HBM_SPEC] * n + [pl.BlockSpec(memory_space=pltpu.VMEM)],
        out_shape=[sems] * (2 * n) + [pltpu.HBM(b.shape, b.dtype) for b in bufs] + [_sds((SUBLANES, LANES), F32)],
        input_output_aliases={i: 2 * n + i for i in range(n)},
        compiler_params=pltpu.CompilerParams(has_side_effects=IN_FLIGHT),
    )(*[_in_hbm(b) for b in bufs], after)
    return [(out[i], out[n + i], out[2 * n + i]) for i in range(n)], out[3 * n]


def _gather_wait(send, recv, buf, after, name, all_devices=False):
    def body(buf_ref, send_ref, recv_ref, after_ref, out_ref):
        x, y, c = _position()
        mine = _own_slot(all_devices, x, y, c)
        for k, (peer, slot) in enumerate(_peers(all_devices, x, y, c)):
            cp = pltpu.make_async_remote_copy(
                src_ref=buf_ref.at[mine], dst_ref=buf_ref.at[slot], send_sem=send_ref.at[k], recv_sem=recv_ref.at[k],
                device_id=peer, device_id_type=MESH)
            cp.wait_send()
            cp.wait_recv()

    return pl.pallas_call(
        body, name=name, in_specs=[HBM_SPEC, SEM_SPEC, SEM_SPEC, ANY_SPEC], out_specs=HBM_SPEC,
        out_shape=pltpu.HBM(buf.shape, buf.dtype), input_output_aliases={0: 0},
        compiler_params=pltpu.CompilerParams(has_side_effects=IN_FLIGHT),
    )(buf, send, recv, after)


def _scatter_start(pieces, name):
    n = len(pieces)

    def body(*refs):
        src = refs[:n]
        send, recv = refs[n:2 * n], refs[2 * n:3 * n]
        land = refs[4 * n:5 * n]
        token = refs[5 * n]
        x, y, c = _position()
        mine = 2 * x + y
        for i in range(n):
            for k, (px, py) in enumerate(_other_chips(x, y)):
                pltpu.make_async_remote_copy(
                    src_ref=src[i].at[2 * px + py], dst_ref=land[i].at[mine], send_sem=send[i].at[k], recv_sem=recv[i].at[k],
                    device_id=(px, py, c), device_id_type=MESH).start()
        token[...] = jnp.zeros_like(token)

    hbm = [pltpu.HBM(p.shape, p.dtype) for p in pieces]
    out = pl.pallas_call(
        body, name=name, in_specs=[HBM_SPEC] * n,
        out_specs=[SEM_SPEC] * (2 * n) + [HBM_SPEC] * (2 * n) + [pl.BlockSpec(memory_space=pltpu.VMEM)],
        out_shape=[_peer_sems(False)] * (2 * n) + hbm + hbm + [_sds((SUBLANES, LANES), F32)],
        input_output_aliases={i: 2 * n + i for i in range(n)},
        compiler_params=pltpu.CompilerParams(has_side_effects=IN_FLIGHT),
    )(*[_in_hbm(p) for p in pieces])
    return [(out[i], out[n + i], out[2 * n + i], out[3 * n + i]) for i in range(n)], out[4 * n]


def _scatter_wait(send, recv, pieces, land, after, name):
    def body(src_ref, land_ref, send_ref, recv_ref, after_ref, src_out, land_out):
        x, y, c = _position()
        for k, (px, py) in enumerate(_other_chips(x, y)):
            cp = pltpu.make_async_remote_copy(
                src_ref=src_ref.at[2 * px + py], dst_ref=land_ref.at[2 * px + py], send_sem=send_ref.at[k],
                recv_sem=recv_ref.at[k], device_id=(px, py, c), device_id_type=MESH)
            cp.wait_send()
            cp.wait_recv()

    return pl.pallas_call(
        body, name=name, in_specs=[HBM_SPEC, HBM_SPEC, SEM_SPEC, SEM_SPEC, ANY_SPEC], out_specs=[HBM_SPEC, HBM_SPEC],
        out_shape=[pltpu.HBM(pieces.shape, pieces.dtype), pltpu.HBM(land.shape, land.dtype)],
        input_output_aliases={0: 0, 1: 1},
        compiler_params=pltpu.CompilerParams(has_side_effects=IN_FLIGHT),
    )(pieces, land, send, recv, after)


def _rope_tables(batch, rows_per_example):
    inv_freq = ROPE_THETA ** (-jnp.arange(0, 64, 2, dtype=F32) / 64)
    ang = jnp.arange(rows_per_example, dtype=F32)[:, None] * inv_freq[None, :]
    cos, sin = jnp.cos(ang), jnp.sin(ang)
    cos128 = jnp.concatenate([cos, cos, cos, cos], axis=1)
    sin128 = jnp.concatenate([-sin, sin, -sin, sin], axis=1)
    return jnp.tile(cos128, (batch, 1)), jnp.tile(sin128, (batch, 1))


def _block_diagonal(w):
    eye = jnp.eye(LRU_BLOCKS, dtype=w.dtype)
    return (w[:, :, None, :] * eye[:, None, :, None]).reshape(LRU_W, LRU_W)


def _diagonal_blocks(dense):
    d4 = dense.reshape(LRU_BLOCKS, LRU_BLOCK, LRU_BLOCKS, LRU_BLOCK)
    return jnp.stack([d4[n, :, n, :] for n in range(LRU_BLOCKS)])


def _row(v):
    return v.reshape(1, -1)


def _local_step(x, loss_target, meta_tokens, small, depth, big_weight, on_layer_grads):
    batch, seq, _ = x.shape
    n_real = N_META + seq
    n_blocks = -(-n_real // TIME_BLOCK)
    per_example = n_blocks * TIME_BLOCK
    pad = per_example - n_real
    meta = jnp.broadcast_to(meta_tokens[None], (batch, N_META, D_MODEL))
    h = jnp.concatenate([meta, x, jnp.zeros((batch, pad, D_MODEL), F32)], axis=1).reshape(batch * per_example, D_MODEL)
    rope_cos, rope_sin = _rope_tables(batch, per_example)

    saved = []
    for l in range(depth):
        wa = _block_diagonal(small['gate_a_w'][l]).astype(BF16)
        wx = _block_diagonal(small['gate_x_w'][l]).astype(BF16)
        lru_small = (small['conv_w'][l], _row(small['conv_b'][l]), wa, _row(small['gate_a_b'][l]), wx,
                     _row(small['gate_x_b'][l]), _row(small['lru_lambda'][l]))
        w_in = big_weight('w_in', l, h)
        z1, q, kv, xg = _in_proj_fwd(h, _row(small['pre_mix_norm'][l]), w_in, rope_cos, rope_sin)
        attn, lse = _attn_fwd(q, kv, small['attn_sinks'][l], batch, n_blocks)
        hs, lru = _lru_fwd(xg, *lru_small, batch, n_blocks)
        w_out = big_weight('w_out', l, xg)
        groups, o, h_mid = _out_proj_fwd(attn, lru, _row(small['attn_group_norm'][l]), _row(small['lru_group_norm'][l]),
                                         w_out, h, _row(small['post_mix_norm'][l]))
        w_up4 = big_weight('w_up', l, hs)
        w_down = big_weight('w_down', l, hs)
        z2, up, act, y, h_out = _mlp_fwd(h_mid, _row(small['pre_mlp_norm'][l]), w_up4, w_down,
                                         _row(small['post_mlp_norm'][l]))
        saved.append(dict(h=h, z1=z1, q=q, kv=kv, xg=xg, attn=attn, lse=lse, hs=hs, lru=lru, groups=groups, o=o,
                          h_mid=h_mid, z2=z2, up=up, act=act, y=y, lru_small=lru_small, w_in=w_in, w_out=w_out,
                          w_up4=w_up4, w_down=w_down))
        h = h_out

    dh, sq_err = _loss_head(h.reshape(batch, per_example, D_MODEL), loss_target)
    dh = dh.reshape(batch * per_example, D_MODEL)

    gs = {n: [None] * depth for n in REPLICATED_NAMES + ('conv_w',)}
    handed_over = None
    for l in reversed(range(depth)):
        s = saved[l]

        def ordered_after(gain):
            return gain if handed_over is None else gain + handed_over[0, 0]

        dh_mid, dy, dup, g_mlp = _mlp_bwd(dh, s['h_mid'], s['y'], s['up'], s['w_up4'], s['w_down'],
                                          _row(small['pre_mlp_norm'][l]), ordered_after(_row(small['post_mlp_norm'][l])))
        handed_over = on_layer_grads(l, {
            'w_down': _matmul_tn(s['act'], dy, 512, D_MODEL, "grad_w_down", BF16),
            'w_up': _matmul_tn(s['z2'], dup, 512, D_MODEL, "grad_w_up", BF16, column_blocks=True)})
        do, d_attn, d_lru, g_out = _out_proj_bwd(dh_mid, s['o'], s['attn'], s['lru'], _row(small['attn_group_norm'][l]),
                                                 _row(small['lru_group_norm'][l]),
                                                 ordered_after(_row(small['post_mix_norm'][l])), s['w_out'])
        dxg, dwa, dwx, g_lru = _lru_bwd(s['xg'], s['hs'], d_lru, *s['lru_small'], batch, n_blocks)
        dq, dkv, dsink = _attn_bwd(s['q'], s['kv'], small['attn_sinks'][l], s['attn'], s['lse'], d_attn, batch, n_blocks)
        dh, dproj, g_in = _in_proj_bwd(dh_mid, s['h'], dq, dkv, dxg, rope_cos, rope_sin, _row(small['pre_mix_norm'][l]),
                                       s['w_in'])
        handed_over = on_layer_grads(l, {
            'w_out': _matmul_tn(s['groups'], do, 512, D_MODEL, "grad_w_out", BF16),
            'w_in': _matmul_tn(dproj, s['z1'], IN_COLS // 2, D_MODEL, "grad_w_in", BF16)})
        gs['pre_mlp_norm'][l], gs['post_mlp_norm'][l] = g_mlp[0], g_mlp[1]
        gs['post_mix_norm'][l] = g_out[0]
        gs['attn_group_norm'][l], gs['lru_group_norm'][l] = g_out[1, :ATTN_W], g_out[1, ATTN_W:]
        gs['pre_mix_norm'][l] = g_in[0]
        gs['attn_sinks'][l] = dsink[:, 0]
        gs['conv_w'][l] = g_lru[:CONV_TAPS]
        gs['conv_b'][l], gs['gate_a_b'][l], gs['gate_x_b'][l], gs['lru_lambda'][l] = g_lru[4], g_lru[5], g_lru[6], g_lru[7]
        gs['gate_a_w'][l] = _diagonal_blocks(dwa)
        gs['gate_x_w'][l] = _diagonal_blocks(dwx)

    grad_x = dh.reshape(batch, per_example, D_MODEL)[:, N_META:n_real]
    grad_meta = _meta_grad(dh, batch, per_example)
    small_grads = {n: jnp.stack(v) for n, v in gs.items()}
    return sq_err, grad_x, grad_meta, small_grads


PACK_UNIT = SUBLANES * LANES


def _size(shape):
    size = 1
    for d in shape:
        size *= d
    return size


def _pack(arrays):
    parts = []
    for a in arrays:
        flat = a.reshape(-1)
        padded = -(-flat.shape[0] // PACK_UNIT) * PACK_UNIT
        if padded != flat.shape[0]:
            flat = jnp.pad(flat, (0, padded - flat.shape[0]))
        parts.append(flat.reshape(-1, LANES))
    return jnp.concatenate(parts, axis=0)


def _unpack(buf, shapes):
    out, at = [], 0
    for shp in shapes:
        size = _size(shp)
        rows = -(-size // PACK_UNIT) * SUBLANES
        part = buf[at:at + rows]
        if rows * LANES != size:
            part = part.reshape(-1)[:size]
        out.append(part.reshape(shp))
        at += rows
    return out


def kernel(x, meta_tokens, pre_mix_norm, w_in, attn_sinks, conv_w, conv_b, gate_a_w, gate_a_b, gate_x_w, gate_x_b, lru_lambda, attn_group_norm, lru_group_norm, w_out, post_mix_norm, pre_mlp_norm, w_up, w_down, post_mlp_norm, loss_target, m_meta_tokens, m_pre_mix_norm, m_w_in, m_attn_sinks, m_conv_w, m_conv_b, m_gate_a_w, m_gate_a_b, m_gate_x_w, m_gate_x_b, m_lru_lambda, m_attn_group_norm, m_lru_group_norm, m_w_out, m_post_mix_norm, m_pre_mlp_norm, m_w_up, m_w_down, m_post_mlp_norm, v_meta_tokens, v_pre_mix_norm, v_w_in, v_attn_sinks, v_conv_w, v_conv_b, v_gate_a_w, v_gate_a_b, v_gate_x_w, v_gate_x_b, v_lru_lambda, v_attn_group_norm, v_lru_group_norm, v_w_out, v_post_mix_norm, v_pre_mlp_norm, v_w_up, v_w_down, v_post_mlp_norm):
    given = dict(locals())
    w = {n: given[n] for n in WEIGHT_NAMES}
    m = {n: given['m_' + n] for n in WEIGHT_NAMES}
    v = {n: given['v_' + n] for n in WEIGHT_NAMES}
    depth = w_in.shape[0]
    for d in (w, m, v):
        d['w_in'] = jnp.swapaxes(d['w_in'], 1, 2)
    chip = 2 * lax.axis_index("x") + lax.axis_index("y")
    chip1 = chip.reshape(1).astype(jnp.int32)

    in_flight = []
    all_started = chip1
    for l in range(depth):
        bufs = [_place_shard(w[n], l, chip1, "place_" + n) for n in BIG_NAMES]
        handles, all_started = _gather_start(bufs, all_started, "gather_start_%d" % l)
        in_flight.append(dict(zip(BIG_NAMES, handles)))

    def big_weight(name, l, after):
        send, recv, buf = in_flight[l][name]
        if l == 0 and name == 'w_in':
            after = all_started
        full = _gather_wait(send, recv, buf, after, "gather_wait_%d_%s" % (l, name))
        if name == 'w_in':
            return full.reshape(IN_COLS, D_MODEL)
        if name == 'w_out':
            return full.reshape(D_MODEL, D_MODEL)
        if name == 'w_down':
            return full.reshape(D_FF, D_MODEL)
        return full

    gathered = _exchange_chips([meta_tokens, conv_w], "gather_small_weights", scatter=False)
    full_meta = jnp.concatenate([gathered[0][s] for s in range(N_CHIPS)], axis=1)
    small = {n: w[n] for n in REPLICATED_NAMES}
    small['conv_w'] = jnp.concatenate([gathered[1][s] for s in range(N_CHIPS)], axis=2)

    scattering = [{} for _ in range(depth)]

    def on_layer_grads(l, big):
        names = list(big)
        pieces = [big[n].reshape(N_CHIPS, -1, D_MODEL) for n in names]
        handles, started = _scatter_start(pieces, "scatter_start_%d_%s" % (l, names[0]))
        scattering[l].update(zip(names, handles))
        return started

    sq_err, grad_x, grad_meta, small_grads = _local_step(x, loss_target, full_meta, small, depth, big_weight, on_layer_grads)
    loss = lax.psum(sq_err[0, 0] * (0.5 / D_MODEL), ("x", "y", "c"))

    small_names = list(REPLICATED_NAMES) + list(COLUMN_SHARDED_SMALL)
    small_full = dict(small_grads)
    small_full['meta_tokens'] = grad_meta
    device1 = (2 * chip + lax.axis_index("c")).reshape(1).astype(jnp.int32)
    packed = _place_slot(_pack([small_full[n] for n in small_names]), device1, N_DEV, "place_small_grads")
    [small_flight], done = _gather_start([packed], grad_meta, "gather_small_grads_start", all_devices=True)

    results = {n: None for n in BIG_NAMES}
    for l in reversed(range(depth)):
        partial = []
        for n in BIG_NAMES:
            send, recv, pieces, land = scattering[l][n]
            pieces, land = _scatter_wait(send, recv, pieces, land, done, "scatter_wait_%d_%s" % (l, n))
            partial.append(_sum_pieces(pieces, land, chip1))
        sibling = _swap_sibling(partial, "swap_partial_grads")
        for n, mine, other in zip(BIG_NAMES, partial, sibling):
            results[n] = _adamw(w[n], m[n], v[n], [mine, other], "adamw_" + n, layer=l, prev=results[n])
        done = results['w_down'][4]

    slots = _gather_wait(*small_flight, done, "gather_small_grads_wait", all_devices=True)
    summed = _unpack(_sum_slots(slots, "sum_small_grads"), [small_full[n].shape for n in small_names])
    grads = dict(zip(small_names, summed))
    grads['meta_tokens'] = lax.dynamic_slice_in_dim(grads['meta_tokens'], chip * meta_tokens.shape[1], meta_tokens.shape[1], 1)
    grads['conv_w'] = lax.dynamic_slice_in_dim(grads['conv_w'], chip * conv_w.shape[2], conv_w.shape[2], 2)

    out_g, out_d, out_m, out_v = {}, {}, {}, {}
    for n in BIG_NAMES:
        out_g[n], out_d[n], out_m[n], out_v[n] = [jnp.swapaxes(r, 1, 2) if n == 'w_in' else r for r in results[n][:4]]
    shapes = [w[n].shape for n in small_names]
    res = _adamw(_pack([w[n] for n in small_names]), _pack([m[n] for n in small_names]), _pack([v[n] for n in small_names]),
                 [_pack([grads[n] for n in small_names])], "adamw_small")
    for k, store in enumerate((out_g, out_d, out_m, out_v)):
        for n, a in zip(small_names, _unpack(res[k], shapes)):
            store[n] = a

    return (loss, grad_x, *[out_g[n] for n in WEIGHT_NAMES], *[out_d[n] for n in WEIGHT_NAMES],
            *[out_m[n] for n in WEIGHT_NAMES], *[out_v[n] for n in WEIGHT_NAMES])
```

```python
import functools

import jax
import jax.numpy as jnp
from jax import lax
from jax.experimental import pallas as pl
from jax.experimental.pallas import tpu as pltpu

F32 = jnp.float32
BF16 = jnp.bfloat16

D_MODEL = 1024
N_HEADS = 8
ATTN_W = 512
KV_W = 128
LRU_W = 512
LRU_BLOCKS = 8
LRU_BLOCK = 64
IN_COLS = 1792
D_FF = 4096
N_META = 16
CONV_TAPS = 4
LRU_C = 8.0
ROPE_THETA = 10000.0
EPS = 1e-6
ATTN_SCALE = 0.125

ADAM_LR = 0.001
ADAM_B1 = 0.9
ADAM_B2 = 0.999
ADAM_EPS = 1e-08
ADAM_WD = 0.01
ADAM_STEP = 10

N_CHIPS = 4
N_DEV = 8
TIME_BLOCK = 128
ROW_TILE = 256
PROJ_ROW_TILE = 544
LANES = 128
SUBLANES = 8
MASKED = -1e30
VMEM_LIMIT = 56 * 1024 * 1024

MESH = pl.DeviceIdType.MESH
HBM_SPEC = pl.BlockSpec(memory_space=pltpu.HBM)

WEIGHT_NAMES = ['meta_tokens', 'pre_mix_norm', 'w_in', 'attn_sinks', 'conv_w', 'conv_b', 'gate_a_w', 'gate_a_b',
                'gate_x_w', 'gate_x_b', 'lru_lambda', 'attn_group_norm', 'lru_group_norm', 'w_out', 'post_mix_norm',
                'pre_mlp_norm', 'w_up', 'w_down', 'post_mlp_norm']
BIG_NAMES = ('w_in', 'w_out', 'w_up', 'w_down')
COLUMN_SHARDED_SMALL = ('meta_tokens', 'conv_w')
REPLICATED_NAMES = tuple(n for n in WEIGHT_NAMES if n not in BIG_NAMES and n not in COLUMN_SHARDED_SMALL)


def _sds(shape, dtype):
    return jax.ShapeDtypeStruct(tuple(shape), dtype)


def _params(*sem):
    return pltpu.CompilerParams(dimension_semantics=sem, vmem_limit_bytes=VMEM_LIMIT)


def _row_spec(width, tile=ROW_TILE):
    return pl.BlockSpec((tile, width), lambda i: (i, 0))


def _proj_tile(rows):
    tile = PROJ_ROW_TILE
    while rows % tile:
        tile -= 16
    return tile


def _whole_spec(a):
    nd = a.ndim
    return pl.BlockSpec(a.shape, lambda *_: (0,) * nd)


def _rms(x, g):
    r = lax.rsqrt(jnp.mean(x * x, axis=-1, keepdims=True) + EPS)
    return x * r * g


def _rms_bwd(dy, x, g):
    r = lax.rsqrt(jnp.mean(x * x, axis=-1, keepdims=True) + EPS)
    xh = x * r
    dg = jnp.sum(dy * xh, axis=0, keepdims=True)
    dxh = dy * g
    dx = r * (dxh - xh * jnp.mean(dxh * xh, axis=-1, keepdims=True))
    return dx, dg


def _rope(x, cos, sin_signed):
    width = x.shape[1]
    reps = width // LANES
    if reps > 1:
        cos = jnp.tile(cos, (1, reps))
        sin_signed = jnp.tile(sin_signed, (1, reps))
    lane = lax.broadcasted_iota(jnp.int32, x.shape, 1)
    first_half = jnp.bitwise_and(lane, 32) == 0
    other = jnp.where(first_half, pltpu.roll(x, width - 32, 1), pltpu.roll(x, 32, 1))
    return x * cos + other * sin_signed


def _sigmoid(x):
    return 1.0 / (1.0 + jnp.exp(-x))


def _log1p(e):
    return jnp.where(e < 1e-3, e * (1.0 - e * (0.5 - e * (1.0 / 3.0))), jnp.log(1.0 + e))


def _one_minus_exp(x):
    series = -x * (1.0 + x * (0.5 + x * ((1.0 / 6.0) + x * (1.0 / 24.0))))
    return jnp.where(x > -0.05, series, 1.0 - jnp.exp(x))


GELU_K = 0.7978845608028654
GELU_C = 0.044715


def _gelu(x):
    t = jnp.tanh(GELU_K * (x + GELU_C * x * x * x))
    return 0.5 * x * (1.0 + t)


def _gelu_and_grad(x):
    x2 = x * x
    t = jnp.tanh(GELU_K * (x + GELU_C * x * x2))
    val = 0.5 * x * (1.0 + t)
    grad = 0.5 * (1.0 + t) + 0.5 * x * (1.0 - t * t) * GELU_K * (1.0 + 3.0 * GELU_C * x2)
    return val, grad


def _dot(a, b):
    return jnp.dot(a, b, preferred_element_type=F32)


def _dot_nt(a, b):
    return lax.dot_general(a, b, (((1,), (1,)), ((), ())), preferred_element_type=F32)


def _dot_tn(a, b):
    return lax.dot_general(a, b, (((0,), (0,)), ((), ())), preferred_element_type=F32)


def _put_rows(rows_8, values):
    d = values[0].shape[1]
    rowid = lax.broadcasted_iota(jnp.int32, (rows_8, d), 0)
    out = jnp.zeros((rows_8, d), F32)
    for k, v in enumerate(values):
        out = out + jnp.where(rowid == k, v, 0.0)
    return out


def _in_proj_fwd(h, gain, w_in, rope_cos, rope_sin):
    rows = h.shape[0]

    def body(h_ref, g_ref, w_ref, c_ref, s_ref, z_ref, q_ref, kv_ref, xg_ref):
        z = _rms(h_ref[...], g_ref[...]).astype(BF16)
        z_ref[...] = z
        proj = _dot_nt(z, w_ref[...])
        cos = c_ref[...]
        sin = s_ref[...]
        q_ref[...] = (_rope(proj[:, :ATTN_W], cos, sin) * ATTN_SCALE).astype(BF16)
        kv_ref[:, :KV_W] = _rope(proj[:, ATTN_W:ATTN_W + KV_W], cos, sin).astype(BF16)
        kv_ref[:, KV_W:] = proj[:, ATTN_W + KV_W:ATTN_W + 2 * KV_W].astype(BF16)
        xg_ref[...] = proj[:, ATTN_W + 2 * KV_W:]

    tile = _proj_tile(rows)
    rs = functools.partial(_row_spec, tile=tile)
    return pl.pallas_call(
        body, name="in_proj_fwd", grid=(rows // tile,),
        in_specs=[rs(D_MODEL), _whole_spec(gain), _whole_spec(w_in), rs(LANES), rs(LANES)],
        out_specs=[rs(D_MODEL), rs(ATTN_W), rs(2 * KV_W), rs(2 * LRU_W)],
        out_shape=[_sds((rows, D_MODEL), BF16), _sds((rows, ATTN_W), BF16), _sds((rows, 2 * KV_W), BF16),
                   _sds((rows, 2 * LRU_W), F32)],
        compiler_params=_params("parallel"),
    )(h, gain, w_in, rope_cos, rope_sin)


def _in_proj_bwd(dh_mid, h, dq, dkv, dxg, rope_cos, rope_sin, gain, w_in):
    rows = h.shape[0]

    def body(dhm_ref, h_ref, dq_ref, dkv_ref, dxg_ref, c_ref, s_ref, g_ref, w_ref, dh_ref, dp_ref, gacc_ref):
        @pl.when(pl.program_id(0) == 0)
        def _():
            gacc_ref[...] = jnp.zeros_like(gacc_ref)

        cos = c_ref[...]
        sin = -s_ref[...]
        dp_ref[:, :ATTN_W] = (_rope(dq_ref[...], cos, sin) * ATTN_SCALE).astype(BF16)
        dp_ref[:, ATTN_W:ATTN_W + KV_W] = _rope(dkv_ref[:, :KV_W], cos, sin).astype(BF16)
        dp_ref[:, ATTN_W + KV_W:ATTN_W + 2 * KV_W] = dkv_ref[:, KV_W:].astype(BF16)
        dp_ref[:, ATTN_W + 2 * KV_W:] = dxg_ref[...]
        dz = _dot(dp_ref[...], w_ref[...])
        dx, dg = _rms_bwd(dz, h_ref[...], g_ref[...])
        dh_ref[...] = dhm_ref[...] + dx
        gacc_ref[...] += _put_rows(SUBLANES, [dg])

    tile = _proj_tile(rows)
    rs = functools.partial(_row_spec, tile=tile)
    return pl.pallas_call(
        body, name="in_proj_bwd", grid=(rows // tile,),
        in_specs=[rs(D_MODEL), rs(D_MODEL), rs(ATTN_W), rs(2 * KV_W), rs(2 * LRU_W),
                  rs(LANES), rs(LANES), _whole_spec(gain), _whole_spec(w_in)],
        out_specs=[rs(D_MODEL), rs(IN_COLS), pl.BlockSpec((SUBLANES, D_MODEL), lambda i: (0, 0))],
        out_shape=[_sds((rows, D_MODEL), F32), _sds((rows, IN_COLS), BF16), _sds((SUBLANES, D_MODEL), F32)],
        compiler_params=_params("arbitrary"),
    )(dh_mid, h, dq, dkv, dxg, rope_cos, rope_sin, gain, w_in)


def _kv_lane_variants(t, group):
    lane = lax.broadcasted_iota(jnp.int32, t.shape, 1)
    low = lane < 64
    swapped = pltpu.roll(t, 64, 1)
    if group == 0:
        lo, hi = jnp.where(low, t, 0.0), jnp.where(low, 0.0, swapped)
    else:
        lo, hi = jnp.where(low, swapped, 0.0), jnp.where(low, 0.0, t)
    return jnp.concatenate([lo, hi], axis=0).astype(BF16)


GROUP_ROWS = 2 * TIME_BLOCK
KEYS = 2 * TIME_BLOCK


def _window_mask(j):
    r = jnp.bitwise_and(lax.broadcasted_iota(jnp.int32, (GROUP_ROWS, KEYS), 0), TIME_BLOCK - 1)
    c = lax.broadcasted_iota(jnp.int32, (GROUP_ROWS, KEYS), 1)
    return (c > r) & (c <= r + TIME_BLOCK) & ((c >= TIME_BLOCK) | (j > 0))


def _group_rows(ref, group):
    lo = 2 * group * LANES
    return jnp.concatenate([ref[:, lo:lo + LANES], ref[:, lo + LANES:lo + 2 * LANES]], axis=0)


def _per_head(sink_ref, group, half):
    upper = lax.broadcasted_iota(jnp.int32, (GROUP_ROWS, 1), 0) < TIME_BLOCK
    return jnp.where(upper, sink_ref[4 * group + half], sink_ref[4 * group + 2 + half])


def _attn_fwd(q, kv, sinks, batch, n_blocks):
    rows = q.shape[0]

    def body(sink_ref, q_ref, kvc_ref, kvp_ref, o_ref, lse_ref):
        j = pl.program_id(1)
        kv2 = jnp.concatenate([kvp_ref[...], kvc_ref[...]], axis=0).astype(F32)
        mask = _window_mask(j)
        lane8 = lax.broadcasted_iota(jnp.int32, (TIME_BLOCK, N_HEADS), 1)
        lse_tile = jnp.zeros((TIME_BLOCK, N_HEADS), F32)
        for group in range(2):
            k_cat = _kv_lane_variants(kv2[:, :KV_W], group)
            v_cat = _kv_lane_variants(kv2[:, KV_W:], group)
            s_all = _dot_nt(_group_rows(q_ref, group), k_cat)
            probs = []
            for half in range(2):
                sink = _per_head(sink_ref, group, half)
                s = jnp.where(mask, s_all[:, half * KEYS:(half + 1) * KEYS], MASKED)
                m = jnp.maximum(jnp.max(s, axis=1, keepdims=True), sink)
                e = jnp.exp(s - m)
                den = jnp.sum(e, axis=1, keepdims=True) + jnp.exp(sink - m)
                probs.append((e / den).astype(BF16))
                lse = m + jnp.log(den)
                lse_tile = lse_tile + jnp.where(lane8 == 4 * group + half, lse[:TIME_BLOCK], 0.0)
                lse_tile = lse_tile + jnp.where(lane8 == 4 * group + 2 + half, lse[TIME_BLOCK:], 0.0)
            out = _dot(jnp.concatenate(probs, axis=1), v_cat)
            o_ref[:, 2 * group * LANES:(2 * group + 1) * LANES] = out[:TIME_BLOCK]
            o_ref[:, (2 * group + 1) * LANES:(2 * group + 2) * LANES] = out[TIME_BLOCK:]
        lse_ref[...] = lse_tile

    def blk(width):
        return pl.BlockSpec((TIME_BLOCK, width), lambda b, j: (b * n_blocks + j, 0))

    prev = pl.BlockSpec((TIME_BLOCK, 2 * KV_W), lambda b, j: (b * n_blocks + jnp.maximum(j - 1, 0), 0))
    return pl.pallas_call(
        body, name="attn_fwd", grid=(batch, n_blocks),
        in_specs=[pl.BlockSpec(memory_space=pltpu.SMEM), blk(ATTN_W), blk(2 * KV_W), prev],
        out_specs=[blk(ATTN_W), blk(N_HEADS)],
        out_shape=[_sds((rows, ATTN_W), F32), _sds((rows, N_HEADS), F32)],
        compiler_params=_params("parallel", "parallel"),
    )(sinks, q, kv, kv)


def _attn_bwd(q, kv, sinks, out, lse, d_out, batch, n_blocks):
    rows = q.shape[0]

    def body(sink_ref, q_ref, kvc_ref, kvp_ref, o_ref, do_ref, lse_ref, dq_ref, dkv_ref, dsink_ref, carry):
        b = pl.program_id(0)
        j = pl.program_id(1)

        @pl.when((b == 0) & (j == 0))
        def _():
            dsink_ref[...] = jnp.zeros_like(dsink_ref)

        @pl.when(j < n_blocks)
        def _():
            kv2 = jnp.concatenate([kvp_ref[...], kvc_ref[...]], axis=0).astype(F32)
            mask = _window_mask(j)
            lane = lax.broadcasted_iota(jnp.int32, (GROUP_ROWS, LANES), 1)
            low = lax.broadcasted_iota(jnp.int32, (KEYS, LANES), 1) < 64
            upper = lax.broadcasted_iota(jnp.int32, (GROUP_ROWS, 1), 0) < TIME_BLOCK
            lse_tile = lse_ref[...]
            dk_tile = jnp.zeros((KEYS, KV_W), F32)
            dv_tile = jnp.zeros((KEYS, KV_W), F32)
            dsink_vals = {}
            for group in range(2):
                k_cat = _kv_lane_variants(kv2[:, :KV_W], group)
                v_cat = _kv_lane_variants(kv2[:, KV_W:], group)
                q_rows = _group_rows(q_ref, group)
                do_rows = _group_rows(do_ref, group)
                do_b = do_rows.astype(BF16)
                od = do_rows * _group_rows(o_ref, group)
                s_all = _dot_nt(q_rows, k_cat)
                dp_all = _dot_nt(do_b, v_cat)
                probs, dss = [], []
                for half in range(2):
                    heads = (4 * group + half, 4 * group + 2 + half)
                    sink = _per_head(sink_ref, group, half)
                    lse_h = jnp.concatenate([lse_tile[:, h:h + 1] for h in heads], axis=0)
                    in_half = (lane < 64) if half == 0 else (lane >= 64)
                    delta = jnp.sum(jnp.where(in_half, od, 0.0), axis=1, keepdims=True)
                    cols = slice(half * KEYS, (half + 1) * KEYS)
                    prob = jnp.exp(jnp.where(mask, s_all[:, cols], MASKED) - lse_h)
                    probs.append(prob.astype(BF16))
                    dss.append((prob * (dp_all[:, cols] - delta)).astype(BF16))
                    dsink = -jnp.exp(sink - lse_h) * delta
                    dsink_vals[heads[0]] = jnp.sum(jnp.where(upper, dsink, 0.0), axis=0, keepdims=True)
                    dsink_vals[heads[1]] = jnp.sum(jnp.where(upper, 0.0, dsink), axis=0, keepdims=True)
                ds = jnp.concatenate(dss, axis=1)
                dq_rows = _dot(ds, k_cat)
                dq_ref[:, 2 * group * LANES:(2 * group + 1) * LANES] = dq_rows[:TIME_BLOCK]
                dq_ref[:, (2 * group + 1) * LANES:(2 * group + 2) * LANES] = dq_rows[TIME_BLOCK:]
                dk_cat = _dot_tn(ds, q_rows)
                dv_cat = _dot_tn(jnp.concatenate(probs, axis=1), do_b)
                if group == 0:
                    dk_tile = dk_tile + jnp.where(low, dk_cat[:KEYS] + pltpu.roll(dk_cat[KEYS:], 64, 1), 0.0)
                    dv_tile = dv_tile + jnp.where(low, dv_cat[:KEYS] + pltpu.roll(dv_cat[KEYS:], 64, 1), 0.0)
                else:
                    dk_tile = dk_tile + jnp.where(low, 0.0, pltpu.roll(dk_cat[:KEYS], 64, 1) + dk_cat[KEYS:])
                    dv_tile = dv_tile + jnp.where(low, 0.0, pltpu.roll(dv_cat[:KEYS], 64, 1) + dv_cat[KEYS:])

            @pl.when(j > 0)
            def _():
                dkv_ref[:, :KV_W] = carry[:, :KV_W] + dk_tile[:TIME_BLOCK]
                dkv_ref[:, KV_W:] = carry[:, KV_W:] + dv_tile[:TIME_BLOCK]

            carry[:, :KV_W] = dk_tile[TIME_BLOCK:]
            carry[:, KV_W:] = dv_tile[TIME_BLOCK:]
            rowid = lax.broadcasted_iota(jnp.int32, (N_HEADS, LANES), 0)
            upd = jnp.zeros((N_HEADS, LANES), F32)
            for head, val in dsink_vals.items():
                upd = upd + jnp.where(rowid == head, val, 0.0)
            dsink_ref[...] += upd

        @pl.when(j == n_blocks)
        def _():
            dkv_ref[...] = carry[...]

    last = n_blocks - 1

    def blk(width):
        return pl.BlockSpec((TIME_BLOCK, width), lambda b, j: (b * n_blocks + jnp.minimum(j, last), 0))

    prev = pl.BlockSpec((TIME_BLOCK, 2 * KV_W), lambda b, j: (b * n_blocks + jnp.maximum(jnp.minimum(j, last) - 1, 0), 0))
    dkv_spec = pl.BlockSpec((TIME_BLOCK, 2 * KV_W), lambda b, j: (b * n_blocks + jnp.maximum(j - 1, 0), 0))
    return pl.pallas_call(
        body, name="attn_bwd", grid=(batch, n_blocks + 1),
        in_specs=[pl.BlockSpec(memory_space=pltpu.SMEM), blk(ATTN_W), blk(2 * KV_W), prev, blk(ATTN_W), blk(ATTN_W),
                  blk(N_HEADS)],
        out_specs=[blk(ATTN_W), dkv_spec, pl.BlockSpec((N_HEADS, LANES), lambda b, j: (0, 0))],
        out_shape=[_sds((rows, ATTN_W), F32), _sds((rows, 2 * KV_W), F32), _sds((N_HEADS, LANES), F32)],
        scratch_shapes=[pltpu.VMEM((TIME_BLOCK, 2 * KV_W), F32)],
        compiler_params=_params("arbitrary", "arbitrary"),
    )(sinks, q, kv, kv, out, d_out, lse)


def _conv_taps(xb, prev8):
    ext = jnp.concatenate([prev8, xb], axis=0)
    n = ext.shape[0]
    return [xb] + [pltpu.roll(ext, k, 0)[SUBLANES:n] for k in range(1, CONV_TAPS)]


def _lru_gates(xc, wa, ba, wx, bx, lam):
    xcb = xc.astype(BF16)
    r = _sigmoid(_dot(xcb, wa) + ba)
    i = _sigmoid(_dot(xcb, wx) + bx)
    sp = jnp.maximum(-lam, 0.0) + _log1p(jnp.exp(-jnp.abs(lam)))
    log_a = -LRU_C * r * sp
    a = jnp.exp(log_a)
    mult = jnp.sqrt(_one_minus_exp(2.0 * log_a))
    return xcb, r, i, sp, a, mult


def _scan_fwd(a, u, h_before):
    n, d = a.shape
    groups = n // SUBLANES
    a = a.reshape(groups, SUBLANES, d)
    u = u.reshape(groups, SUBLANES, d)
    sub = lax.broadcasted_iota(jnp.int32, a.shape, 1)
    s = 1
    while s < SUBLANES:
        valid = sub >= s
        u = jnp.where(valid, u + a * pltpu.roll(u, s, 1), u)
        a = jnp.where(valid, a * pltpu.roll(a, s, 1), a)
        s *= 2
    out, prev = [], h_before
    for g in range(groups):
        out.append(u[g] + a[g] * prev)
        prev = out[-1][SUBLANES - 1:SUBLANES, :]
    return jnp.concatenate(out, axis=0)


def _scan_rev(cf, g, d_after):
    n, d = g.shape
    groups = n // SUBLANES
    cf = cf.reshape(groups, SUBLANES, d)
    g = g.reshape(groups, SUBLANES, d)
    sub = lax.broadcasted_iota(jnp.int32, g.shape, 1)
    s = 1
    while s < SUBLANES:
        valid = sub + s < SUBLANES
        g = jnp.where(valid, g + cf * pltpu.roll(g, SUBLANES - s, 1), g)
        cf = jnp.where(valid, cf * pltpu.roll(cf, SUBLANES - s, 1), cf)
        s *= 2
    out, nxt = [None] * groups, d_after
    for k in reversed(range(groups)):
        out[k] = g[k] + cf[k] * nxt
        nxt = out[k][0:1, :]
    return jnp.concatenate(out, axis=0)


def _lru_fwd(xg, conv_w, conv_b, wa, ba, wx, bx, lam, batch, n_blocks):
    rows = xg.shape[0]

    def body(xg_ref, cw_ref, cb_ref, wa_ref, ba_ref, wx_ref, bx_ref, lam_ref, hs_ref, lru_ref, x_prev, h_carry):
        @pl.when(pl.program_id(1) == 0)
        def _():
            x_prev[...] = jnp.zeros_like(x_prev)
            h_carry[...] = jnp.zeros_like(h_carry)

        xb = xg_ref[:, :LRU_W]
        taps = _conv_taps(xb, x_prev[...])
        xc = cb_ref[...] + sum(cw_ref[CONV_TAPS - 1 - k:CONV_TAPS - k, :] * taps[k] for k in range(CONV_TAPS))
        _, _, i, _, a, mult = _lru_gates(xc, wa_ref[...], ba_ref[...], wx_ref[...], bx_ref[...], lam_ref[...])
        h = _scan_fwd(a, mult * i * xc, h_carry[...])
        hs_ref[...] = h
        lru_ref[...] = h * _gelu(xg_ref[:, LRU_W:])
        h_carry[...] = h[TIME_BLOCK - 1:TIME_BLOCK, :]
        x_prev[...] = xb[TIME_BLOCK - SUBLANES:TIME_BLOCK, :]

    def blk(width):
        return pl.BlockSpec((TIME_BLOCK, width), lambda b, j: (b * n_blocks + j, 0))

    small = [conv_w, conv_b, wa, ba, wx, bx, lam]
    return pl.pallas_call(
        body, name="lru_fwd", grid=(batch, n_blocks),
        in_specs=[blk(2 * LRU_W)] + [_whole_spec(a) for a in small],
        out_specs=[blk(LRU_W), blk(LRU_W)],
        out_shape=[_sds((rows, LRU_W), F32), _sds((rows, LRU_W), F32)],
        scratch_shapes=[pltpu.VMEM((SUBLANES, LRU_W), F32), pltpu.VMEM((1, LRU_W), F32)],
        compiler_params=_params("arbitrary", "arbitrary"),
    )(xg, *small)


def _lru_bwd(xg, hs, d_lru, conv_w, conv_b, wa, ba, wx, bx, lam, batch, n_blocks):
    rows = xg.shape[0]
    last_row = TIME_BLOCK - 1

    def body(xg_ref, xgh_ref, hs_ref, hsh_ref, dl_ref, cw_ref, cb_ref, wa_ref, ba_ref, wx_ref, bx_ref, lam_ref,
             dxg_ref, dwa_ref, dwx_ref, vec_ref, dh_carry, dxc_next):
        b = pl.program_id(0)
        j = pl.program_id(1)

        @pl.when((b == 0) & (j == 0))
        def _():
            dwa_ref[...] = jnp.zeros_like(dwa_ref)
            dwx_ref[...] = jnp.zeros_like(dwx_ref)
            vec_ref[...] = jnp.zeros_like(vec_ref)

        @pl.when(j == 0)
        def _():
            dh_carry[...] = jnp.zeros_like(dh_carry)
            dxc_next[...] = jnp.zeros_like(dxc_next)

        first = j == n_blocks - 1
        xb = xg_ref[:, :LRU_W]
        prev8 = jnp.where(first, 0.0, xgh_ref[:, :LRU_W])
        h_before = jnp.where(first, 0.0, hsh_ref[SUBLANES - 1:SUBLANES, :])
        cw = cw_ref[...]
        lam = lam_ref[...]
        wa = wa_ref[...]
        wx = wx_ref[...]
        taps = _conv_taps(xb, prev8)
        xc = cb_ref[...] + sum(cw[CONV_TAPS - 1 - k:CONV_TAPS - k, :] * taps[k] for k in range(CONV_TAPS))
        xcb, r, i, sp, a, mult = _lru_gates(xc, wa, ba_ref[...], wx, bx_ref[...], lam)
        hs = hs_ref[...]
        row = lax.broadcasted_iota(jnp.int32, hs.shape, 0)
        h_prev = jnp.where(row == 0, h_before, pltpu.roll(hs, 1, 0))
        dl = dl_ref[...]
        gate, dgate = _gelu_and_grad(xg_ref[:, LRU_W:])
        dxg_ref[:, LRU_W:] = (dl * hs * dgate).astype(BF16)
        cf = jnp.where(row == last_row, 1.0, pltpu.roll(a, last_row, 0))
        dh = _scan_rev(cf, dl * gate, dh_carry[...])
        dh_carry[...] = a[0:1, :] * dh[0:1, :]
        dmult = dh * i * xc
        di = dh * mult * xc
        dxc = dh * mult * i
        dlog_a = dh * h_prev * a - dmult * (a * a / mult)
        dr = dlog_a * (-LRU_C * sp)
        dlam = jnp.sum(dlog_a * (-LRU_C * r), axis=0, keepdims=True) * (-_sigmoid(-lam))
        dpr = dr * r * (1.0 - r)
        dpi = di * i * (1.0 - i)
        dprb = dpr.astype(BF16)
        dpib = dpi.astype(BF16)
        dxc = dxc + _dot_nt(dprb, wa) + _dot_nt(dpib, wx)
        dwa_ref[...] += _dot_tn(xcb, dprb)
        dwx_ref[...] += _dot_tn(xcb, dpib)
        ext = jnp.concatenate([dxc, dxc_next[...]], axis=0)
        n = ext.shape[0]
        dxb = cw[CONV_TAPS - 1:CONV_TAPS, :] * dxc
        for k in range(1, CONV_TAPS):
            dxb = dxb + cw[CONV_TAPS - 1 - k:CONV_TAPS - k, :] * pltpu.roll(ext, n - k, 0)[:TIME_BLOCK]
        dxg_ref[:, :LRU_W] = dxb.astype(BF16)
        dxc_next[...] = dxc[:SUBLANES, :]
        vecs = [jnp.sum(dxc * taps[CONV_TAPS - 1 - t], axis=0, keepdims=True) for t in range(CONV_TAPS)]
        vecs += [jnp.sum(dxc, axis=0, keepdims=True), jnp.sum(dpr, axis=0, keepdims=True),
                 jnp.sum(dpi, axis=0, keepdims=True), dlam]
        vec_ref[...] += _put_rows(SUBLANES, vecs)

    def tblk(b, j):
        return b * n_blocks + n_blocks - 1 - j

    def blk(width):
        return pl.BlockSpec((TIME_BLOCK, width), lambda b, j: (tblk(b, j), 0))

    per8 = TIME_BLOCK // SUBLANES

    def halo(width):
        return pl.BlockSpec((SUBLANES, width), lambda b, j: (jnp.maximum(per8 * tblk(b, j) - 1, 0), 0))

    small = [conv_w, conv_b, wa, ba, wx, bx, lam]
    acc = lambda shape: pl.BlockSpec(shape, lambda b, j: (0, 0))
    return pl.pallas_call(
        body, name="lru_bwd", grid=(batch, n_blocks),
        in_specs=[blk(2 * LRU_W), halo(2 * LRU_W), blk(LRU_W), halo(LRU_W), blk(LRU_W)] + [_whole_spec(a) for a in small],
        out_specs=[blk(2 * LRU_W), acc((LRU_W, LRU_W)), acc((LRU_W, LRU_W)), acc((SUBLANES, LRU_W))],
        out_shape=[_sds((rows, 2 * LRU_W), BF16), _sds((LRU_W, LRU_W), F32), _sds((LRU_W, LRU_W), F32),
                   _sds((SUBLANES, LRU_W), F32)],
        scratch_shapes=[pltpu.VMEM((1, LRU_W), F32), pltpu.VMEM((SUBLANES, LRU_W), F32)],
        compiler_params=_params("arbitrary", "arbitrary"),
    )(xg, xg, hs, hs, d_lru, *small)


def _out_proj_fwd(attn, lru, g_attn, g_lru, w_out, h, g_post):
    rows = h.shape[0]

    def body(at_ref, lr_ref, ga_ref, gl_ref, w_ref, h_ref, gp_ref, grp_ref, o_ref, hm_ref):
        a = _rms(at_ref[...], ga_ref[...]).astype(BF16)
        l = _rms(lr_ref[...], gl_ref[...]).astype(BF16)
        grp_ref[:, :ATTN_W] = a
        grp_ref[:, ATTN_W:] = l
        o = _dot(a, w_ref[:ATTN_W, :]) + _dot(l, w_ref[ATTN_W:, :])
        o_ref[...] = o
        hm_ref[...] = h_ref[...] + _rms(o, gp_ref[...])

    tile = _proj_tile(rows)
    rs = functools.partial(_row_spec, tile=tile)
    return pl.pallas_call(
        body, name="out_proj_fwd", grid=(rows // tile,),
        in_specs=[rs(ATTN_W), rs(LRU_W), _whole_spec(g_attn), _whole_spec(g_lru), _whole_spec(w_out),
                  rs(D_MODEL), _whole_spec(g_post)],
        out_specs=[rs(D_MODEL), rs(D_MODEL), rs(D_MODEL)],
        out_shape=[_sds((rows, D_MODEL), BF16), _sds((rows, D_MODEL), F32), _sds((rows, D_MODEL), F32)],
        compiler_params=_params("parallel"),
    )(attn, lru, g_attn, g_lru, w_out, h, g_post)


def _out_proj_bwd(dh_mid, o, attn, lru, g_attn, g_lru, g_post, w_out):
    rows = o.shape[0]

    def body(dhm_ref, o_ref, at_ref, lr_ref, ga_ref, gl_ref, gp_ref, w_ref, do_ref, dat_ref, dlr_ref, gacc_ref):
        @pl.when(pl.program_id(0) == 0)
        def _():
            gacc_ref[...] = jnp.zeros_like(gacc_ref)

        do, dgp = _rms_bwd(dhm_ref[...], o_ref[...], gp_ref[...])
        dob = do.astype(BF16)
        do_ref[...] = dob
        dat, dga = _rms_bwd(_dot_nt(dob, w_ref[:ATTN_W, :]), at_ref[...], ga_ref[...])
        dlr, dgl = _rms_bwd(_dot_nt(dob, w_ref[ATTN_W:, :]), lr_ref[...], gl_ref[...])
        dat_ref[...] = dat
        dlr_ref[...] = dlr
        gacc_ref[...] += _put_rows(SUBLANES, [dgp, jnp.concatenate([dga, dgl], axis=1)])

    tile = _proj_tile(rows)
    rs = functools.partial(_row_spec, tile=tile)
    return pl.pallas_call(
        body, name="out_proj_bwd", grid=(rows // tile,),
        in_specs=[rs(D_MODEL), rs(D_MODEL), rs(ATTN_W), rs(LRU_W), _whole_spec(g_attn),
                  _whole_spec(g_lru), _whole_spec(g_post), _whole_spec(w_out)],
        out_specs=[rs(D_MODEL), rs(ATTN_W), rs(LRU_W),
                   pl.BlockSpec((SUBLANES, D_MODEL), lambda i: (0, 0))],
        out_shape=[_sds((rows, D_MODEL), BF16), _sds((rows, ATTN_W), F32), _sds((rows, LRU_W), F32),
                   _sds((SUBLANES, D_MODEL), F32)],
        compiler_params=_params("arbitrary"),
    )(dh_mid, o, attn, lru, g_attn, g_lru, g_post, w_out)


def _mlp_fwd(h_mid, g_pre, w_up4, w_down, g_post):
    rows = h_mid.shape[0]

    def body(h_ref, g1_ref, wu_ref, wd_ref, g2_ref, z_ref, slope_ref, act_ref, y_ref, ho_ref):
        x = h_ref[...]
        z = _rms(x, g1_ref[...]).astype(BF16)
        z_ref[...] = z
        y = jnp.zeros((ROW_TILE, D_MODEL), F32)
        for s in range(N_CHIPS):
            cols = slice(s * D_MODEL, (s + 1) * D_MODEL)
            r = jnp.maximum(_dot(z, wu_ref[s]), 0.0)
            slope_ref[:, cols] = (2.0 * r).astype(BF16)
            a = jnp.square(r).astype(BF16)
            act_ref[:, cols] = a
            y = y + _dot(a, wd_ref[cols, :])
        y_ref[...] = y
        ho_ref[...] = x + _rms(y, g2_ref[...])

    return pl.pallas_call(
        body, name="mlp_fwd", grid=(rows // ROW_TILE,),
        in_specs=[_row_spec(D_MODEL), _whole_spec(g_pre), _whole_spec(w_up4), _whole_spec(w_down), _whole_spec(g_post)],
        out_specs=[_row_spec(D_MODEL), _row_spec(D_FF), _row_spec(D_FF), _row_spec(D_MODEL), _row_spec(D_MODEL)],
        out_shape=[_sds((rows, D_MODEL), BF16), _sds((rows, D_FF), BF16), _sds((rows, D_FF), BF16),
                   _sds((rows, D_MODEL), F32), _sds((rows, D_MODEL), F32)],
        compiler_params=_params("parallel"),
    )(h_mid, g_pre, w_up4, w_down, g_post)


def _mlp_bwd(dh_out, h_mid, y, slope, w_up4, w_down, g_pre, g_post):
    rows = y.shape[0]

    def body(dh_ref, hm_ref, y_ref, slope_ref, wu_ref, wd_ref, g1_ref, g2_ref, dhm_ref, dy_ref, dup_ref, gacc_ref):
        @pl.when(pl.program_id(0) == 0)
        def _():
            gacc_ref[...] = jnp.zeros_like(gacc_ref)

        dh = dh_ref[...]
        dy, dg2 = _rms_bwd(dh, y_ref[...], g2_ref[...])
        dyb = dy.astype(BF16)
        dy_ref[...] = dyb
        dz = jnp.zeros((ROW_TILE, D_MODEL), F32)
        for s in range(N_CHIPS):
            cols = slice(s * D_MODEL, (s + 1) * D_MODEL)
            dact = _dot_nt(dyb, wd_ref[cols, :])
            dup = (dact * slope_ref[:, cols].astype(F32)).astype(BF16)
            dup_ref[:, cols] = dup
            dz = dz + _dot_nt(dup, wu_ref[s])
        dx, dg1 = _rms_bwd(dz, hm_ref[...], g1_ref[...])
        dhm_ref[...] = dh + dx
        gacc_ref[...] += _put_rows(SUBLANES, [dg1, dg2])

    return pl.pallas_call(
        body, name="mlp_bwd", grid=(rows // ROW_TILE,),
        in_specs=[_row_spec(D_MODEL), _row_spec(D_MODEL), _row_spec(D_MODEL), _row_spec(D_FF), _whole_spec(w_up4),
                  _whole_spec(w_down), _whole_spec(g_pre), _whole_spec(g_post)],
        out_specs=[_row_spec(D_MODEL), _row_spec(D_MODEL), _row_spec(D_FF),
                   pl.BlockSpec((SUBLANES, D_MODEL), lambda i: (0, 0))],
        out_shape=[_sds((rows, D_MODEL), F32), _sds((rows, D_MODEL), BF16), _sds((rows, D_FF), BF16),
                   _sds((SUBLANES, D_MODEL), F32)],
        compiler_params=_params("arbitrary"),
    )(dh_out, h_mid, y, slope, w_up4, w_down, g_pre, g_post)


def _matmul_tn(a, b, tm, tn, name, out_dtype, column_blocks=False):
    rows, m = a.shape
    n = b.shape[1]

    def body(a_ref, b_ref, o_ref):
        o_ref[...] = _dot_tn(a_ref[...], b_ref[...]).astype(out_dtype)

    if column_blocks:
        out_spec = pl.BlockSpec((None, tm, tn), lambda i, j: (j, i, 0))
        out_shape = _sds((n // tn, m, tn), out_dtype)
    else:
        out_spec = pl.BlockSpec((tm, tn), lambda i, j: (i, j))
        out_shape = _sds((m, n), out_dtype)
    return pl.pallas_call(
        body, name=name, grid=(m // tm, n // tn),
        in_specs=[pl.BlockSpec((rows, tm), lambda i, j: (0, i)), pl.BlockSpec((rows, tn), lambda i, j: (0, j))],
        out_specs=out_spec, out_shape=out_shape,
        compiler_params=_params("parallel", "parallel"),
    )(a, b)


LOSS_COLS = 256


def _loss_head(h, loss_target):
    batch, per_example, _ = h.shape
    seq = loss_target.shape[1]
    n_real = N_META + seq
    assert seq % SUBLANES == 0

    def body(h_ref, t_ref, dh_ref, l_ref):
        @pl.when((pl.program_id(0) == 0) & (pl.program_id(1) == 0))
        def _():
            l_ref[...] = jnp.zeros_like(l_ref)

        d = h_ref[N_META:n_real, :] - t_ref[...]
        dh_ref[:N_META, :] = jnp.zeros((N_META, LOSS_COLS), F32)
        dh_ref[N_META:n_real, :] = d * (1.0 / D_MODEL)
        dh_ref[n_real:, :] = jnp.zeros((per_example - n_real, LOSS_COLS), F32)
        l_ref[...] += jnp.sum(jnp.sum(d * d, axis=0, keepdims=True), axis=1, keepdims=True)

    blk = pl.BlockSpec((None, per_example, LOSS_COLS), lambda b, j: (b, 0, j))
    return pl.pallas_call(
        body, name="loss_head", grid=(batch, D_MODEL // LOSS_COLS),
        in_specs=[blk, pl.BlockSpec((None, seq, LOSS_COLS), lambda b, j: (b, 0, j))],
        out_specs=[blk, pl.BlockSpec((SUBLANES, LANES), lambda b, j: (0, 0))],
        out_shape=[_sds(h.shape, F32), _sds((SUBLANES, LANES), F32)],
        compiler_params=_params("arbitrary", "arbitrary"),
    )(h, loss_target)


def _meta_grad(dh0, batch, rows_per_example):
    per = rows_per_example // N_META

    def body(d_ref, o_ref):
        @pl.when(pl.program_id(0) == 0)
        def _():
            o_ref[...] = jnp.zeros_like(o_ref)

        o_ref[...] += d_ref[...]

    return pl.pallas_call(
        body, name="meta_grad", grid=(batch,),
        in_specs=[pl.BlockSpec((N_META, D_MODEL), lambda b: (b * per, 0))],
        out_specs=pl.BlockSpec((N_META, D_MODEL), lambda b: (0, 0)),
        out_shape=_sds((N_META, D_MODEL), F32),
        compiler_params=_params("arbitrary"),
    )(dh0)


def _elementwise_tile(rows, cols):
    tile = rows
    while tile * cols * 4 > (1 << 20) and tile % 16 == 0:
        tile //= 2
    return tile


def _sum_slots(buf, name):
    k, rows, cols = buf.shape
    tile = _elementwise_tile(rows, cols)

    def body(*refs):
        total = refs[0][...].astype(F32)
        for r in refs[1:k]:
            total = total + r[...].astype(F32)
        refs[k][...] = total

    def slot(s):
        return pl.BlockSpec((None, tile, cols), lambda i: (s, i, 0))

    return pl.pallas_call(
        body, name=name, grid=(rows // tile,),
        in_specs=[slot(s) for s in range(k)], out_specs=pl.BlockSpec((tile, cols), lambda i: (i, 0)),
        out_shape=_sds((rows, cols), F32), compiler_params=_params("parallel"),
    )(*([buf] * k))


def _sum_pieces(pieces, landed, chip):
    _, rows, cols = pieces.shape
    tile = _elementwise_tile(rows, cols)

    def body(chip_ref, own_ref, a_ref, b_ref, c_ref, o_ref):
        o_ref[...] = ((own_ref[...].astype(F32) + a_ref[...].astype(F32)) + b_ref[...].astype(F32)) + c_ref[...].astype(F32)

    def slot(offset):
        return pl.BlockSpec((None, tile, cols), lambda i, chip_ref: ((chip_ref[0] + offset) % N_CHIPS, i, 0))

    grid_spec = pltpu.PrefetchScalarGridSpec(
        num_scalar_prefetch=1, grid=(rows // tile,), in_specs=[slot(0), slot(1), slot(2), slot(3)],
        out_specs=pl.BlockSpec((tile, cols), lambda i, chip_ref: (i, 0)))
    return pl.pallas_call(
        body, name="sum_grad_pieces", grid_spec=grid_spec, out_shape=_sds((rows, cols), F32),
        compiler_params=_params("parallel"),
    )(chip, pieces, landed, landed, landed)


def _adamw(w, m, v, grads, name, layer=None, prev=None):
    rows, cols = grads[0].shape
    tile = _elementwise_tile(rows, cols)
    ng = len(grads)
    m_scale = 1.0 - ADAM_B1 ** ADAM_STEP
    v_scale = 1.0 - ADAM_B2 ** ADAM_STEP

    def body(*refs):
        w_ref, m_ref, v_ref = refs[:3]
        g_refs = refs[3:3 + ng]
        if layer is None:
            g_out, d_out, m_out, v_out = refs[-4:]
        else:
            g_out, d_out, m_out, v_out, token = refs[-5:]
            token[...] = jnp.zeros_like(token)
        g = g_refs[0][...]
        for r in g_refs[1:]:
            g = g + r[...]
        m_new = ADAM_B1 * m_ref[...] + (1.0 - ADAM_B1) * g
        v_new = ADAM_B2 * v_ref[...] + (1.0 - ADAM_B2) * (g * g)
        m_hat = m_new / m_scale
        v_hat = v_new / v_scale
        g_out[...] = g
        d_out[...] = -ADAM_LR * (m_hat / (jnp.sqrt(v_hat) + ADAM_EPS) + ADAM_WD * w_ref[...])
        m_out[...] = m_new
        v_out[...] = v_new

    spec = pl.BlockSpec((tile, cols), lambda i: (i, 0))
    if layer is None:
        return pl.pallas_call(
            body, name=name, grid=(rows // tile,),
            in_specs=[spec] * (3 + ng), out_specs=[spec] * 4, out_shape=[_sds((rows, cols), F32)] * 4,
            compiler_params=_params("parallel"),
        )(w, m, v, *grads)
    of_layer = pl.BlockSpec((None, tile, cols), lambda i: (layer, i, 0))
    carried = [] if prev is None else list(prev[:4])
    return pl.pallas_call(
        body, name=name, grid=(rows // tile,),
        in_specs=[of_layer] * 3 + [spec] * ng + [pl.BlockSpec(memory_space=pl.ANY)] * len(carried),
        out_specs=[of_layer] * 4 + [pl.BlockSpec((SUBLANES, LANES), lambda i: (0, 0))],
        out_shape=[_sds(w.shape, F32)] * 4 + [_sds((SUBLANES, LANES), F32)],
        input_output_aliases={3 + ng + k: k for k in range(len(carried))},
        compiler_params=_params("arbitrary"),
    )(w, m, v, *grads, *carried)


def _position():
    return lax.axis_index("x"), lax.axis_index("y"), lax.axis_index("c")


def _other_chips(x, y):
    return [(1 - x, y), (x, 1 - y), (1 - x, 1 - y)]


def _exchange_chips(arrays, name, scatter):
    n = len(arrays)

    def body(*refs):
        src, dst = refs[:n], refs[n:2 * n]
        send_sems, recv_sems, local_sems = refs[2 * n:]
        x, y, c = _position()
        mine = 2 * x + y
        copies = []
        for i in range(n):
            own = src[i].at[mine] if scatter else src[i]
            copies.append(pltpu.make_async_copy(own, dst[i].at[mine], local_sems.at[i]))
        for k, (px, py) in enumerate(_other_chips(x, y)):
            for i in range(n):
                piece = src[i].at[2 * px + py] if scatter else src[i]
                copies.append(pltpu.make_async_remote_copy(
                    src_ref=piece, dst_ref=dst[i].at[mine], send_sem=send_sems.at[k, i], recv_sem=recv_sems.at[k, i],
                    device_id=(px, py, c), device_id_type=MESH))
        for cp in copies:
            cp.start()
        for cp in copies:
            cp.wait()

    def out_shape(a):
        return _sds(a.shape if scatter else (N_CHIPS,) + a.shape, a.dtype)

    return pl.pallas_call(
        body, name=name, in_specs=[HBM_SPEC] * n, out_specs=[HBM_SPEC] * n, out_shape=[out_shape(a) for a in arrays],
        scratch_shapes=[pltpu.SemaphoreType.DMA((N_CHIPS - 1, n)), pltpu.SemaphoreType.DMA((N_CHIPS - 1, n)),
                        pltpu.SemaphoreType.DMA((n,))],
    )(*arrays)


def _swap_sibling(arrays, name):
    n = len(arrays)

    def body(*refs):
        src, dst = refs[:n], refs[n:2 * n]
        send_sems, recv_sems = refs[2 * n:]
        x, y, c = _position()
        copies = [pltpu.make_async_remote_copy(
            src_ref=src[i], dst_ref=dst[i], send_sem=send_sems.at[i], recv_sem=recv_sems.at[i],
            device_id=(x, y, 1 - c), device_id_type=MESH) for i in range(n)]
        for cp in copies:
            cp.start()
        for cp in copies:
            cp.wait()

    return pl.pallas_call(
        body, name=name, in_specs=[HBM_SPEC] * n, out_specs=[HBM_SPEC] * n,
        out_shape=[_sds(a.shape, a.dtype) for a in arrays],
        scratch_shapes=[pltpu.SemaphoreType.DMA((n,)), pltpu.SemaphoreType.DMA((n,))],
    )(*arrays)


SEM_SPEC = pl.BlockSpec(memory_space=pltpu.SEMAPHORE)
ANY_SPEC = pl.BlockSpec(memory_space=pl.ANY)
IN_FLIGHT = pltpu.SideEffectType.DATAFLOW_SIDE_EFFECTING


def _peer_sems(all_devices):
    return pltpu.SemaphoreType.DMA(((N_DEV if all_devices else N_CHIPS) - 1,))


def _own_slot(all_devices, x, y, c):
    return 4 * x + 2 * y + c if all_devices else 2 * x + y


def _peers(all_devices, x, y, c):
    if not all_devices:
        return [((px, py, c), 2 * px + py) for px, py in _other_chips(x, y)]
    out = []
    for fx in range(2):
        for fy in range(2):
            for fc in range(2):
                if fx or fy or fc:
                    px, py, pc = (1 - x if fx else x), (1 - y if fy else y), (1 - c if fc else c)
                    out.append(((px, py, pc), 4 * px + 2 * py + pc))
    return out


def _in_hbm(a):
    return pltpu.with_memory_space_constraint(a, pltpu.HBM)


def _place_slot(a, slot, n_slots, name):
    rows, cols = a.shape
    tile = _elementwise_tile(rows, cols)

    def body(slot_ref, a_ref, o_ref):
        o_ref[...] = a_ref[...]

    grid_spec = pltpu.PrefetchScalarGridSpec(
        num_scalar_prefetch=1, grid=(rows // tile,),
        in_specs=[pl.BlockSpec((tile, cols), lambda i, slot_ref: (i, 0))],
        out_specs=pl.BlockSpec((None, tile, cols), lambda i, slot_ref: (slot_ref[0], i, 0)))
    return pl.pallas_call(
        body, name=name, grid_spec=grid_spec, out_shape=_sds((n_slots, rows, cols), a.dtype),
        compiler_params=_params("parallel"),
    )(slot, a)


def _place_shard(w, layer, chip, name):
    _, rows, cols = w.shape
    tile = _elementwise_tile(rows, cols)

    def body(chip_ref, w_ref, o_ref):
        o_ref[...] = w_ref[...].astype(BF16)

    grid_spec = pltpu.PrefetchScalarGridSpec(
        num_scalar_prefetch=1, grid=(rows // tile,),
        in_specs=[pl.BlockSpec((None, tile, cols), lambda i, chip_ref: (layer, i, 0))],
        out_specs=pl.BlockSpec((None, tile, cols), lambda i, chip_ref: (chip_ref[0], i, 0)))
    return pl.pallas_call(
        body, name=name, grid_spec=grid_spec, out_shape=_sds((N_CHIPS, rows, cols), BF16),
        compiler_params=_params("parallel"),
    )(chip, w)


def _gather_start(bufs, after, name, all_devices=False):
    n = len(bufs)

    def body(*refs):
        buf = refs[:n]
        send, recv = refs[n + 1:2 * n + 1], refs[2 * n + 1:3 * n + 1]
        token = refs[4 * n + 1]
        x, y, c = _position()
        mine = _own_slot(all_devices, x, y, c)
        for i in range(n):
            for k, (peer, _) in enumerate(_peers(all_devices, x, y, c)):
                pltpu.make_async_remote_copy(
                    src_ref=buf[i].at[mine], dst_ref=buf[i].at[mine], send_sem=send[i].at[k], recv_sem=recv[i].at[k],
                    device_id=peer, device_id_type=MESH).start()
        token[...] = jnp.zeros_like(token)

    sems = _peer_sems(all_devices)
    out = pl.pallas_call(
        body, name=name, in_specs=[HBM_SPEC] * n + [ANY_SPEC],
        out_specs=[SEM_SPEC] * (2 * n) + [HBM_SPEC] * n + [pl.BlockSpec(memory_space=pltpu.VMEM)],
        out_shape=[sems] * (2 * n) + [pltpu.HBM(b.shape, b.dtype) for b in bufs] + [_sds((SUBLANES, LANES), F32)],
        input_output_aliases={i: 2 * n + i for i in range(n)},
        compiler_params=pltpu.CompilerParams(has_side_effects=IN_FLIGHT),
    )(*[_in_hbm(b) for b in bufs], after)
    return [(out[i], out[n + i], out[2 * n + i]) for i in range(n)], out[3 * n]


def _gather_wait(send, recv, buf, after, name, all_devices=False):
    def body(buf_ref, send_ref, recv_ref, after_ref, out_ref):
        x, y, c = _position()
        mine = _own_slot(all_devices, x, y, c)
        for k, (peer, slot) in enumerate(_peers(all_devices, x, y, c)):
            cp = pltpu.make_async_remote_copy(
                src_ref=buf_ref.at[mine], dst_ref=buf_ref.at[slot], send_sem=send_ref.at[k], recv_sem=recv_ref.at[k],
                device_id=peer, device_id_type=MESH)
            cp.wait_send()
            cp.wait_recv()

    return pl.pallas_call(
        body, name=name, in_specs=[HBM_SPEC, SEM_SPEC, SEM_SPEC, ANY_SPEC], out_specs=HBM_SPEC,
        out_shape=pltpu.HBM(buf.shape, buf.dtype), input_output_aliases={0: 0},
        compiler_params=pltpu.CompilerParams(has_side_effects=IN_FLIGHT),
    )(buf, send, recv, after)


def _scatter_start(pieces, name):
    n = len(pieces)

    def body(*refs):
        src = refs[:n]
        send, recv = refs[n:2 * n], refs[2 * n:3 * n]
        land = refs[4 * n:5 * n]
        token = refs[5 * n]
        x, y, c = _position()
        mine = 2 * x + y
        for i in range(n):
            for k, (px, py) in enumerate(_other_chips(x, y)):
                pltpu.make_async_remote_copy(
                    src_ref=src[i].at[2 * px + py], dst_ref=land[i].at[mine], send_sem=send[i].at[k], recv_sem=recv[i].at[k],
                    device_id=(px, py, c), device_id_type=MESH).start()
        token[...] = jnp.zeros_like(token)

    hbm = [pltpu.HBM(p.shape, p.dtype) for p in pieces]
    out = pl.pallas_call(
        body, name=name, in_specs=[HBM_SPEC] * n,
        out_specs=[SEM_SPEC] * (2 * n) + [HBM_SPEC] * (2 * n) + [pl.BlockSpec(memory_space=pltpu.VMEM)],
        out_shape=[_peer_sems(False)] * (2 * n) + hbm + hbm + [_sds((SUBLANES, LANES), F32)],
        input_output_aliases={i: 2 * n + i for i in range(n)},
        compiler_params=pltpu.CompilerParams(has_side_effects=IN_FLIGHT),
    )(*[_in_hbm(p) for p in pieces])
    return [(out[i], out[n + i], out[2 * n + i], out[3 * n + i]) for i in range(n)], out[4 * n]


def _scatter_wait(send, recv, pieces, land, after, name):
    def body(src_ref, land_ref, send_ref, recv_ref, after_ref, src_out, land_out):
        x, y, c = _position()
        for k, (px, py) in enumerate(_other_chips(x, y)):
            cp = pltpu.make_async_remote_copy(
                src_ref=src_ref.at[2 * px + py], dst_ref=land_ref.at[2 * px + py], send_sem=send_ref.at[k],
                recv_sem=recv_ref.at[k], device_id=(px, py, c), device_id_type=MESH)
            cp.wait_send()
            cp.wait_recv()

    return pl.pallas_call(
        body, name=name, in_specs=[HBM_SPEC, HBM_SPEC, SEM_SPEC, SEM_SPEC, ANY_SPEC], out_specs=[HBM_SPEC, HBM_SPEC],
        out_shape=[pltpu.HBM(pieces.shape, pieces.dtype), pltpu.HBM(land.shape, land.dtype)],
        input_output_aliases={0: 0, 1: 1},
        compiler_params=pltpu.CompilerParams(has_side_effects=IN_FLIGHT),
    )(pieces, land, send, recv, after)


def _rope_tables(batch, rows_per_example):
    inv_freq = ROPE_THETA ** (-jnp.arange(0, 64, 2, dtype=F32) / 64)
    ang = jnp.arange(rows_per_example, dtype=F32)[:, None] * inv_freq[None, :]
    cos, sin = jnp.cos(ang), jnp.sin(ang)
    cos128 = jnp.concatenate([cos, cos, cos, cos], axis=1)
    sin128 = jnp.concatenate([-sin, sin, -sin, sin], axis=1)
    return jnp.tile(cos128, (batch, 1)), jnp.tile(sin128, (batch, 1))


def _block_diagonal(w):
    eye = jnp.eye(LRU_BLOCKS, dtype=w.dtype)
    return (w[:, :, None, :] * eye[:, None, :, None]).reshape(LRU_W, LRU_W)


def _diagonal_blocks(dense):
    d4 = dense.reshape(LRU_BLOCKS, LRU_BLOCK, LRU_BLOCKS, LRU_BLOCK)
    return jnp.stack([d4[n, :, n, :] for n in range(LRU_BLOCKS)])


def _row(v):
    return v.reshape(1, -1)


def _local_step(x, loss_target, meta_tokens, small, depth, big_weight, on_layer_grads):
    batch, seq, _ = x.shape
    n_real = N_META + seq
    n_blocks = -(-n_real // TIME_BLOCK)
    per_example = n_blocks * TIME_BLOCK
    pad = per_example - n_real
    meta = jnp.broadcast_to(meta_tokens[None], (batch, N_META, D_MODEL))
    h = jnp.concatenate([meta, x, jnp.zeros((batch, pad, D_MODEL), F32)], axis=1).reshape(batch * per_example, D_MODEL)
    rope_cos, rope_sin = _rope_tables(batch, per_example)

    saved = []
    for l in range(depth):
        wa = _block_diagonal(small['gate_a_w'][l]).astype(BF16)
        wx = _block_diagonal(small['gate_x_w'][l]).astype(BF16)
        lru_small = (small['conv_w'][l], _row(small['conv_b'][l]), wa, _row(small['gate_a_b'][l]), wx,
                     _row(small['gate_x_b'][l]), _row(small['lru_lambda'][l]))
        w_in = big_weight('w_in', l, h)
        z1, q, kv, xg = _in_proj_fwd(h, _row(small['pre_mix_norm'][l]), w_in, rope_cos, rope_sin)
        attn, lse = _attn_fwd(q, kv, small['attn_sinks'][l], batch, n_blocks)
        hs, lru = _lru_fwd(xg, *lru_small, batch, n_blocks)
        w_out = big_weight('w_out', l, xg)
        groups, o, h_mid = _out_proj_fwd(attn, lru, _row(small['attn_group_norm'][l]), _row(small['lru_group_norm'][l]),
                                         w_out, h, _row(small['post_mix_norm'][l]))
        w_up4 = big_weight('w_up', l, hs)
        w_down = big_weight('w_down', l, hs)
        z2, slope, act, y, h_out = _mlp_fwd(h_mid, _row(small['pre_mlp_norm'][l]), w_up4, w_down,
                                         _row(small['post_mlp_norm'][l]))
        saved.append(dict(h=h, z1=z1, q=q, kv=kv, xg=xg, attn=attn, lse=lse, hs=hs, lru=lru, groups=groups, o=o,
                          h_mid=h_mid, z2=z2, slope=slope, act=act, y=y, lru_small=lru_small, w_in=w_in, w_out=w_out,
                          w_up4=w_up4, w_down=w_down))
        h = h_out

    dh, sq_err = _loss_head(h.reshape(batch, per_example, D_MODEL), loss_target)
    dh = dh.reshape(batch * per_example, D_MODEL)

    gs = {n: [None] * depth for n in REPLICATED_NAMES + ('conv_w',)}
    handed_over = None
    for l in reversed(range(depth)):
        s = saved[l]

        def ordered_after(gain):
            return gain if handed_over is None else gain + handed_over[0, 0]

        dh_mid, dy, dup, g_mlp = _mlp_bwd(dh, s['h_mid'], s['y'], s['slope'], s['w_up4'], s['w_down'],
                                          _row(small['pre_mlp_norm'][l]), ordered_after(_row(small['post_mlp_norm'][l])))
        handed_over = on_layer_grads(l, {
            'w_down': _matmul_tn(s['act'], dy, 512, D_MODEL, "grad_w_down", BF16),
            'w_up': _matmul_tn(s['z2'], dup, 512, D_MODEL, "grad_w_up", BF16, column_blocks=True)})
        do, d_attn, d_lru, g_out = _out_proj_bwd(dh_mid, s['o'], s['attn'], s['lru'], _row(small['attn_group_norm'][l]),
                                                 _row(small['lru_group_norm'][l]),
                                                 ordered_after(_row(small['post_mix_norm'][l])), s['w_out'])
        dxg, dwa, dwx, g_lru = _lru_bwd(s['xg'], s['hs'], d_lru, *s['lru_small'], batch, n_blocks)
        dq, dkv, dsink = _attn_bwd(s['q'], s['kv'], small['attn_sinks'][l], s['attn'], s['lse'], d_attn, batch, n_blocks)
        dh, dproj, g_in = _in_proj_bwd(dh_mid, s['h'], dq, dkv, dxg, rope_cos, rope_sin, _row(small['pre_mix_norm'][l]),
                                       s['w_in'])
        handed_over = on_layer_grads(l, {
            'w_out': _matmul_tn(s['groups'], do, 512, D_MODEL, "grad_w_out", BF16),
            'w_in': _matmul_tn(dproj, s['z1'], IN_COLS // 2, D_MODEL, "grad_w_in", BF16)})
        gs['pre_mlp_norm'][l], gs['post_mlp_norm'][l] = g_mlp[0], g_mlp[1]
        gs['post_mix_norm'][l] = g_out[0]
        gs['attn_group_norm'][l], gs['lru_group_norm'][l] = g_out[1, :ATTN_W], g_out[1, ATTN_W:]
        gs['pre_mix_norm'][l] = g_in[0]
        gs['attn_sinks'][l] = dsink[:, 0]
        gs['conv_w'][l] = g_lru[:CONV_TAPS]
        gs['conv_b'][l], gs['gate_a_b'][l], gs['gate_x_b'][l], gs['lru_lambda'][l] = g_lru[4], g_lru[5], g_lru[6], g_lru[7]
        gs['gate_a_w'][l] = _diagonal_blocks(dwa)
        gs['gate_x_w'][l] = _diagonal_blocks(dwx)

    grad_x = dh.reshape(batch, per_example, D_MODEL)[:, N_META:n_real]
    grad_meta = _meta_grad(dh, batch, per_example)
    small_grads = {n: jnp.stack(v) for n, v in gs.items()}
    return sq_err, grad_x, grad_meta, small_grads, handed_over


PACK_UNIT = SUBLANES * LANES


def _size(shape):
    size = 1
    for d in shape:
        size *= d
    return size


def _pack(arrays):
    parts = []
    for a in arrays:
        flat = a.reshape(-1)
        padded = -(-flat.shape[0] // PACK_UNIT) * PACK_UNIT
        if padded != flat.shape[0]:
            flat = jnp.pad(flat, (0, padded - flat.shape[0]))
        parts.append(flat.reshape(-1, LANES))
    return jnp.concatenate(parts, axis=0)


def _unpack(buf, shapes):
    out, at = [], 0
    for shp in shapes:
        size = _size(shp)
        rows = -(-size // PACK_UNIT) * SUBLANES
        part = buf[at:at + rows]
        if rows * LANES != size:
            part = part.reshape(-1)[:size]
        out.append(part.reshape(shp))
        at += rows
    return out


def kernel(x, meta_tokens, pre_mix_norm, w_in, attn_sinks, conv_w, conv_b, gate_a_w, gate_a_b, gate_x_w, gate_x_b, lru_lambda, attn_group_norm, lru_group_norm, w_out, post_mix_norm, pre_mlp_norm, w_up, w_down, post_mlp_norm, loss_target, m_meta_tokens, m_pre_mix_norm, m_w_in, m_attn_sinks, m_conv_w, m_conv_b, m_gate_a_w, m_gate_a_b, m_gate_x_w, m_gate_x_b, m_lru_lambda, m_attn_group_norm, m_lru_group_norm, m_w_out, m_post_mix_norm, m_pre_mlp_norm, m_w_up, m_w_down, m_post_mlp_norm, v_meta_tokens, v_pre_mix_norm, v_w_in, v_attn_sinks, v_conv_w, v_conv_b, v_gate_a_w, v_gate_a_b, v_gate_x_w, v_gate_x_b, v_lru_lambda, v_attn_group_norm, v_lru_group_norm, v_w_out, v_post_mix_norm, v_pre_mlp_norm, v_w_up, v_w_down, v_post_mlp_norm):
    given = dict(locals())
    w = {n: given[n] for n in WEIGHT_NAMES}
    m = {n: given['m_' + n] for n in WEIGHT_NAMES}
    v = {n: given['v_' + n] for n in WEIGHT_NAMES}
    depth = w_in.shape[0]
    for d in (w, m, v):
        d['w_in'] = jnp.swapaxes(d['w_in'], 1, 2)
    chip = 2 * lax.axis_index("x") + lax.axis_index("y")
    chip1 = chip.reshape(1).astype(jnp.int32)

    in_flight = []
    all_started = chip1
    for l in range(depth):
        bufs = [_place_shard(w[n], l, chip1, "place_" + n) for n in BIG_NAMES]
        handles, all_started = _gather_start(bufs, all_started, "gather_start_%d" % l)
        in_flight.append(dict(zip(BIG_NAMES, handles)))

    def big_weight(name, l, after):
        send, recv, buf = in_flight[l][name]
        if l == 0 and name == 'w_in':
            after = all_started
        full = _gather_wait(send, recv, buf, after, "gather_wait_%d_%s" % (l, name))
        if name == 'w_in':
            return full.reshape(IN_COLS, D_MODEL)
        if name == 'w_out':
            return full.reshape(D_MODEL, D_MODEL)
        if name == 'w_down':
            return full.reshape(D_FF, D_MODEL)
        return full

    gathered = _exchange_chips([meta_tokens, conv_w], "gather_small_weights", scatter=False)
    full_meta = jnp.concatenate([gathered[0][s] for s in range(N_CHIPS)], axis=1)
    small = {n: w[n] for n in REPLICATED_NAMES}
    small['conv_w'] = jnp.concatenate([gathered[1][s] for s in range(N_CHIPS)], axis=2)

    scattering = [{} for _ in range(depth)]

    def on_layer_grads(l, big):
        names = list(big)
        pieces = [big[n].reshape(N_CHIPS, -1, D_MODEL) for n in names]
        handles, started = _scatter_start(pieces, "scatter_start_%d_%s" % (l, names[0]))
        scattering[l].update(zip(names, handles))
        return started

    sq_err, grad_x, grad_meta, small_grads, all_handed_over = _local_step(x, loss_target, full_meta, small, depth, big_weight,
                                                                           on_layer_grads)
    loss = lax.psum(sq_err[0, 0] * (0.5 / D_MODEL), ("x", "y", "c"))

    small_names = list(REPLICATED_NAMES) + list(COLUMN_SHARDED_SMALL)
    small_full = dict(small_grads)
    small_full['meta_tokens'] = grad_meta
    device1 = (2 * chip + lax.axis_index("c")).reshape(1).astype(jnp.int32)
    packed = _place_slot(_pack([small_full[n] for n in small_names]), device1, N_DEV, "place_small_grads")
    [small_flight], done = _gather_start([packed], all_handed_over, "gather_small_grads_start", all_devices=True)

    results = {n: None for n in BIG_NAMES}
    for l in reversed(range(depth)):
        partial = []
        for n in BIG_NAMES:
            send, recv, pieces, land = scattering[l][n]
            pieces, land = _scatter_wait(send, recv, pieces, land, done, "scatter_wait_%d_%s" % (l, n))
            partial.append(_sum_pieces(pieces, land, chip1))
        sibling = _swap_sibling(partial, "swap_partial_grads")
        for n, mine, other in zip(BIG_NAMES, partial, sibling):
            results[n] = _adamw(w[n], m[n], v[n], [mine, other], "adamw_" + n, layer=l, prev=results[n])
        done = results['w_down'][4]

    slots = _gather_wait(*small_flight, done, "gather_small_grads_wait", all_devices=True)
    summed = _unpack(_sum_slots(slots, "sum_small_grads"), [small_full[n].shape for n in small_names])
    grads = dict(zip(small_names, summed))
    grads['meta_tokens'] = lax.dynamic_slice_in_dim(grads['meta_tokens'], chip * meta_tokens.shape[1], meta_tokens.shape[1], 1)
    grads['conv_w'] = lax.dynamic_slice_in_dim(grads['conv_w'], chip * conv_w.shape[2], conv_w.shape[2], 2)

    out_g, out_d, out_m, out_v = {}, {}, {}, {}
    for n in BIG_NAMES:
        out_g[n], out_d[n], out_m[n], out_v[n] = [jnp.swapaxes(r, 1, 2) if n == 'w_in' else r for r in results[n][:4]]
    shapes = [w[n].shape for n in small_names]
    res = _adamw(_pack([w[n] for n in small_names]), _pack([m[n] for n in small_names]), _pack([v[n] for n in small_names]),
                 [_pack([grads[n] for n in small_names])], "adamw_small")
    for k, store in enumerate((out_g, out_d, out_m, out_v)):
        for n, a in zip(small_names, _unpack(res[k], shapes)):
            store[n] = a

    return (loss, grad_x, *[out_g[n] for n in WEIGHT_NAMES], *[out_d[n] for n in WEIGHT_NAMES],
            *[out_m[n] for n in WEIGHT_NAMES], *[out_v[n] for n in WEIGHT_NAMES])
```

```python
import functools

import jax
import jax.numpy as jnp
from jax import lax
from jax.experimental import pallas as pl
from jax.experimental.pallas import tpu as pltpu

F32 = jnp.float32
BF16 = jnp.bfloat16

D_MODEL = 1024
N_HEADS = 8
ATTN_W = 512
KV_W = 128
LRU_W = 512
LRU_BLOCKS = 8
LRU_BLOCK = 64
IN_COLS = 1792
D_FF = 4096
N_META = 16
CONV_TAPS = 4
LRU_C = 8.0
ROPE_THETA = 10000.0
EPS = 1e-6
ATTN_SCALE = 0.125

ADAM_LR = 0.001
ADAM_B1 = 0.9
ADAM_B2 = 0.999
ADAM_EPS = 1e-08
ADAM_WD = 0.01
ADAM_STEP = 10

N_CHIPS = 4
N_DEV = 8
TIME_BLOCK = 128
ROW_TILE = 256
PROJ_ROW_TILE = 544
LANES = 128
SUBLANES = 8
MASKED = -1e30
VMEM_LIMIT = 56 * 1024 * 1024

MESH = pl.DeviceIdType.MESH
HBM_SPEC = pl.BlockSpec(memory_space=pltpu.HBM)

WEIGHT_NAMES = ['meta_tokens', 'pre_mix_norm', 'w_in', 'attn_sinks', 'conv_w', 'conv_b', 'gate_a_w', 'gate_a_b',
                'gate_x_w', 'gate_x_b', 'lru_lambda', 'attn_group_norm', 'lru_group_norm', 'w_out', 'post_mix_norm',
                'pre_mlp_norm', 'w_up', 'w_down', 'post_mlp_norm']
BIG_NAMES = ('w_in', 'w_out', 'w_up', 'w_down')
COLUMN_SHARDED_SMALL = ('meta_tokens', 'conv_w')
REPLICATED_NAMES = tuple(n for n in WEIGHT_NAMES if n not in BIG_NAMES and n not in COLUMN_SHARDED_SMALL)


def _sds(shape, dtype):
    return jax.ShapeDtypeStruct(tuple(shape), dtype)


def _params(*sem):
    return pltpu.CompilerParams(dimension_semantics=sem, vmem_limit_bytes=VMEM_LIMIT)


def _row_spec(width, tile=ROW_TILE):
    return pl.BlockSpec((tile, width), lambda i: (i, 0))


def _proj_tile(rows):
    tile = PROJ_ROW_TILE
    while rows % tile:
        tile -= 16
    return tile


def _whole_spec(a):
    nd = a.ndim
    return pl.BlockSpec(a.shape, lambda *_: (0,) * nd)


def _rms(x, g):
    r = lax.rsqrt(jnp.mean(x * x, axis=-1, keepdims=True) + EPS)
    return x * r * g


def _rms_bwd(dy, x, g):
    r = lax.rsqrt(jnp.mean(x * x, axis=-1, keepdims=True) + EPS)
    xh = x * r
    dg = jnp.sum(dy * xh, axis=0, keepdims=True)
    dxh = dy * g
    dx = r * (dxh - xh * jnp.mean(dxh * xh, axis=-1, keepdims=True))
    return dx, dg


def _rope(x, cos, sin_signed):
    width = x.shape[1]
    reps = width // LANES
    if reps > 1:
        cos = jnp.tile(cos, (1, reps))
        sin_signed = jnp.tile(sin_signed, (1, reps))
    lane = lax.broadcasted_iota(jnp.int32, x.shape, 1)
    first_half = jnp.bitwise_and(lane, 32) == 0
    other = jnp.where(first_half, pltpu.roll(x, width - 32, 1), pltpu.roll(x, 32, 1))
    return x * cos + other * sin_signed


def _sigmoid(x):
    return 1.0 / (1.0 + jnp.exp(-x))


def _log1p(e):
    return jnp.where(e < 1e-3, e * (1.0 - e * (0.5 - e * (1.0 / 3.0))), jnp.log(1.0 + e))


def _one_minus_square(a, log_a):
    x = 2.0 * log_a
    return jnp.where(x > -0.002, x * (-1.0 - 0.5 * x), 1.0 - a * a)


GELU_K = 0.7978845608028654
GELU_C = 0.044715


def _gelu(x):
    t = jnp.tanh(GELU_K * (x + GELU_C * x * x * x))
    return 0.5 * x * (1.0 + t)


def _gelu_and_grad(x):
    x2 = x * x
    t = jnp.tanh(GELU_K * (x + GELU_C * x * x2))
    val = 0.5 * x * (1.0 + t)
    grad = 0.5 * (1.0 + t) + 0.5 * x * (1.0 - t * t) * GELU_K * (1.0 + 3.0 * GELU_C * x2)
    return val, grad


def _dot(a, b):
    return jnp.dot(a, b, preferred_element_type=F32)


def _dot_nt(a, b):
    return lax.dot_general(a, b, (((1,), (1,)), ((), ())), preferred_element_type=F32)


def _dot_tn(a, b):
    return lax.dot_general(a, b, (((0,), (0,)), ((), ())), preferred_element_type=F32)


def _put_rows(rows_8, values):
    d = values[0].shape[1]
    rowid = lax.broadcasted_iota(jnp.int32, (rows_8, d), 0)
    out = jnp.zeros((rows_8, d), F32)
    for k, v in enumerate(values):
        out = out + jnp.where(rowid == k, v, 0.0)
    return out


def _in_proj_fwd(h, gain, w_in, rope_cos, rope_sin):
    rows = h.shape[0]

    def body(h_ref, g_ref, w_ref, c_ref, s_ref, z_ref, q_ref, kv_ref, xg_ref):
        z = _rms(h_ref[...], g_ref[...]).astype(BF16)
        z_ref[...] = z
        proj = _dot_nt(z, w_ref[...])
        cos = c_ref[...]
        sin = s_ref[...]
        q_ref[...] = (_rope(proj[:, :ATTN_W], cos, sin) * ATTN_SCALE).astype(BF16)
        kv_ref[:, :KV_W] = _rope(proj[:, ATTN_W:ATTN_W + KV_W], cos, sin).astype(BF16)
        kv_ref[:, KV_W:] = proj[:, ATTN_W + KV_W:ATTN_W + 2 * KV_W].astype(BF16)
        xg_ref[...] = proj[:, ATTN_W + 2 * KV_W:]

    tile = _proj_tile(rows)
    rs = functools.partial(_row_spec, tile=tile)
    return pl.pallas_call(
        body, name="in_proj_fwd", grid=(rows // tile,),
        in_specs=[rs(D_MODEL), _whole_spec(gain), _whole_spec(w_in), rs(LANES), rs(LANES)],
        out_specs=[rs(D_MODEL), rs(ATTN_W), rs(2 * KV_W), rs(2 * LRU_W)],
        out_shape=[_sds((rows, D_MODEL), BF16), _sds((rows, ATTN_W), BF16), _sds((rows, 2 * KV_W), BF16),
                   _sds((rows, 2 * LRU_W), F32)],
        compiler_params=_params("parallel"),
    )(h, gain, w_in, rope_cos, rope_sin)


def _in_proj_bwd(dh_mid, h, dq, dkv, dxg, rope_cos, rope_sin, gain, w_in):
    rows = h.shape[0]

    def body(dhm_ref, h_ref, dq_ref, dkv_ref, dxg_ref, c_ref, s_ref, g_ref, w_ref, dh_ref, dp_ref, gacc_ref):
        @pl.when(pl.program_id(0) == 0)
        def _():
            gacc_ref[...] = jnp.zeros_like(gacc_ref)

        cos = c_ref[...]
        sin = -s_ref[...]
        dp_ref[:, :ATTN_W] = (_rope(dq_ref[...], cos, sin) * ATTN_SCALE).astype(BF16)
        dp_ref[:, ATTN_W:ATTN_W + KV_W] = _rope(dkv_ref[:, :KV_W], cos, sin).astype(BF16)
        dp_ref[:, ATTN_W + KV_W:ATTN_W + 2 * KV_W] = dkv_ref[:, KV_W:].astype(BF16)
        dp_ref[:, ATTN_W + 2 * KV_W:] = dxg_ref[...]
        dz = _dot(dp_ref[...], w_ref[...])
        dx, dg = _rms_bwd(dz, h_ref[...], g_ref[...])
        dh_ref[...] = dhm_ref[...] + dx
        gacc_ref[...] += _put_rows(SUBLANES, [dg])

    tile = _proj_tile(rows)
    rs = functools.partial(_row_spec, tile=tile)
    return pl.pallas_call(
        body, name="in_proj_bwd", grid=(rows // tile,),
        in_specs=[rs(D_MODEL), rs(D_MODEL), rs(ATTN_W), rs(2 * KV_W), rs(2 * LRU_W),
                  rs(LANES), rs(LANES), _whole_spec(gain), _whole_spec(w_in)],
        out_specs=[rs(D_MODEL), rs(IN_COLS), pl.BlockSpec((SUBLANES, D_MODEL), lambda i: (0, 0))],
        out_shape=[_sds((rows, D_MODEL), F32), _sds((rows, IN_COLS), BF16), _sds((SUBLANES, D_MODEL), F32)],
        compiler_params=_params("arbitrary"),
    )(dh_mid, h, dq, dkv, dxg, rope_cos, rope_sin, gain, w_in)


def _kv_lane_variants(t, group):
    lane = lax.broadcasted_iota(jnp.int32, t.shape, 1)
    low = lane < 64
    swapped = pltpu.roll(t, 64, 1)
    if group == 0:
        lo, hi = jnp.where(low, t, 0.0), jnp.where(low, 0.0, swapped)
    else:
        lo, hi = jnp.where(low, swapped, 0.0), jnp.where(low, 0.0, t)
    return jnp.concatenate([lo, hi], axis=0).astype(BF16)


GROUP_ROWS = 2 * TIME_BLOCK
KEYS = 2 * TIME_BLOCK


def _window_mask(j):
    r = jnp.bitwise_and(lax.broadcasted_iota(jnp.int32, (GROUP_ROWS, KEYS), 0), TIME_BLOCK - 1)
    c = lax.broadcasted_iota(jnp.int32, (GROUP_ROWS, KEYS), 1)
    return (c > r) & (c <= r + TIME_BLOCK) & ((c >= TIME_BLOCK) | (j > 0))


def _group_rows(ref, ex, group):
    lo = 2 * group * LANES
    return jnp.concatenate([ref[ex, :, lo:lo + LANES], ref[ex, :, lo + LANES:lo + 2 * LANES]], axis=0)


def _per_head(sink_ref, group, half):
    upper = lax.broadcasted_iota(jnp.int32, (GROUP_ROWS, 1), 0) < TIME_BLOCK
    return jnp.where(upper, sink_ref[4 * group + half], sink_ref[4 * group + 2 + half])


def _by_example(a, batch):
    return a.reshape(batch, a.shape[0] // batch, a.shape[1])


def _attn_fwd(q, kv, sinks, batch, n_blocks):
    rows = q.shape[0]

    def body(sink_ref, q_ref, kvc_ref, kvp_ref, o_ref, lse_ref):
        j = pl.program_id(0)
        mask = _window_mask(j)
        lane8 = lax.broadcasted_iota(jnp.int32, (TIME_BLOCK, N_HEADS), 1)
        for ex in range(batch):
            kv2 = jnp.concatenate([kvp_ref[ex], kvc_ref[ex]], axis=0).astype(F32)
            lse_tile = jnp.zeros((TIME_BLOCK, N_HEADS), F32)
            for group in range(2):
                k_cat = _kv_lane_variants(kv2[:, :KV_W], group)
                v_cat = _kv_lane_variants(kv2[:, KV_W:], group)
                s_all = _dot_nt(_group_rows(q_ref, ex, group), k_cat)
                probs = []
                for half in range(2):
                    sink = _per_head(sink_ref, group, half)
                    s = jnp.where(mask, s_all[:, half * KEYS:(half + 1) * KEYS], MASKED)
                    m = jnp.maximum(jnp.max(s, axis=1, keepdims=True), sink)
                    e = jnp.exp(s - m)
                    den = jnp.sum(e, axis=1, keepdims=True) + jnp.exp(sink - m)
                    probs.append((e / den).astype(BF16))
                    lse = m + jnp.log(den)
                    lse_tile = lse_tile + jnp.where(lane8 == 4 * group + half, lse[:TIME_BLOCK], 0.0)
                    lse_tile = lse_tile + jnp.where(lane8 == 4 * group + 2 + half, lse[TIME_BLOCK:], 0.0)
                out = _dot(jnp.concatenate(probs, axis=1), v_cat)
                o_ref[ex, :, 2 * group * LANES:(2 * group + 1) * LANES] = out[:TIME_BLOCK]
                o_ref[ex, :, (2 * group + 1) * LANES:(2 * group + 2) * LANES] = out[TIME_BLOCK:]
            lse_ref[ex] = lse_tile

    def blk(width):
        return pl.BlockSpec((batch, TIME_BLOCK, width), lambda j: (0, j, 0))

    prev = pl.BlockSpec((batch, TIME_BLOCK, 2 * KV_W), lambda j: (0, jnp.maximum(j - 1, 0), 0))
    kv3 = _by_example(kv, batch)
    out, lse = pl.pallas_call(
        body, name="attn_fwd", grid=(n_blocks,),
        in_specs=[pl.BlockSpec(memory_space=pltpu.SMEM), blk(ATTN_W), blk(2 * KV_W), prev],
        out_specs=[blk(ATTN_W), blk(N_HEADS)],
        out_shape=[_sds((batch, rows // batch, ATTN_W), F32), _sds((batch, rows // batch, N_HEADS), F32)],
        compiler_params=_params("parallel"),
    )(sinks, _by_example(q, batch), kv3, kv3)
    return out.reshape(rows, ATTN_W), lse.reshape(rows, N_HEADS)


def _attn_bwd(q, kv, sinks, out, lse, d_out, batch, n_blocks):
    rows = q.shape[0]

    def body(sink_ref, q_ref, kvc_ref, kvp_ref, o_ref, do_ref, lse_ref, dq_ref, dkv_ref, dsink_ref, carry):
        j = pl.program_id(0)

        @pl.when(j == 0)
        def _():
            dsink_ref[...] = jnp.zeros_like(dsink_ref)

        def one_example(ex, dsink_vals):
            kv2 = jnp.concatenate([kvp_ref[ex], kvc_ref[ex]], axis=0).astype(F32)
            mask = _window_mask(j)
            lane = lax.broadcasted_iota(jnp.int32, (GROUP_ROWS, LANES), 1)
            low = lax.broadcasted_iota(jnp.int32, (KEYS, LANES), 1) < 64
            upper = lax.broadcasted_iota(jnp.int32, (GROUP_ROWS, 1), 0) < TIME_BLOCK
            lse_tile = lse_ref[ex]
            dk_tile = jnp.zeros((KEYS, KV_W), F32)
            dv_tile = jnp.zeros((KEYS, KV_W), F32)
            for group in range(2):
                k_cat = _kv_lane_variants(kv2[:, :KV_W], group)
                v_cat = _kv_lane_variants(kv2[:, KV_W:], group)
                q_rows = _group_rows(q_ref, ex, group)
                do_rows = _group_rows(do_ref, ex, group)
                do_b = do_rows.astype(BF16)
                od = do_rows * _group_rows(o_ref, ex, group)
                s_all = _dot_nt(q_rows, k_cat)
                dp_all = _dot_nt(do_b, v_cat)
                probs, dss = [], []
                for half in range(2):
                    heads = (4 * group + half, 4 * group + 2 + half)
                    sink = _per_head(sink_ref, group, half)
                    lse_h = jnp.concatenate([lse_tile[:, h:h + 1] for h in heads], axis=0)
                    in_half = (lane < 64) if half == 0 else (lane >= 64)
                    delta = jnp.sum(jnp.where(in_half, od, 0.0), axis=1, keepdims=True)
                    cols = slice(half * KEYS, (half + 1) * KEYS)
                    prob = jnp.exp(jnp.where(mask, s_all[:, cols], MASKED) - lse_h)
                    probs.append(prob.astype(BF16))
                    dss.append((prob * (dp_all[:, cols] - delta)).astype(BF16))
                    dsink = -jnp.exp(sink - lse_h) * delta
                    dsink_vals[heads[0]] = dsink_vals[heads[0]] + jnp.sum(jnp.where(upper, dsink, 0.0), axis=0, keepdims=True)
                    dsink_vals[heads[1]] = dsink_vals[heads[1]] + jnp.sum(jnp.where(upper, 0.0, dsink), axis=0, keepdims=True)
                ds = jnp.concatenate(dss, axis=1)
                dq_rows = _dot(ds, k_cat)
                dq_ref[ex, :, 2 * group * LANES:(2 * group + 1) * LANES] = dq_rows[:TIME_BLOCK]
                dq_ref[ex, :, (2 * group + 1) * LANES:(2 * group + 2) * LANES] = dq_rows[TIME_BLOCK:]
                dk_cat = _dot_tn(ds, q_rows)
                dv_cat = _dot_tn(jnp.concatenate(probs, axis=1), do_b)
                if group == 0:
                    dk_tile = dk_tile + jnp.where(low, dk_cat[:KEYS] + pltpu.roll(dk_cat[KEYS:], 64, 1), 0.0)
                    dv_tile = dv_tile + jnp.where(low, dv_cat[:KEYS] + pltpu.roll(dv_cat[KEYS:], 64, 1), 0.0)
                else:
                    dk_tile = dk_tile + jnp.where(low, 0.0, pltpu.roll(dk_cat[:KEYS], 64, 1) + dk_cat[KEYS:])
                    dv_tile = dv_tile + jnp.where(low, 0.0, pltpu.roll(dv_cat[:KEYS], 64, 1) + dv_cat[KEYS:])

            @pl.when(j > 0)
            def _():
                dkv_ref[ex, :, :KV_W] = carry[ex, :, :KV_W] + dk_tile[:TIME_BLOCK]
                dkv_ref[ex, :, KV_W:] = carry[ex, :, KV_W:] + dv_tile[:TIME_BLOCK]

            carry[ex, :, :KV_W] = dk_tile[TIME_BLOCK:]
            carry[ex, :, KV_W:] = dv_tile[TIME_BLOCK:]

        @pl.when(j < n_blocks)
        def _():
            dsink_vals = {head: jnp.zeros((1, 1), F32) for head in range(N_HEADS)}
            for ex in range(batch):
                one_example(ex, dsink_vals)
            rowid = lax.broadcasted_iota(jnp.int32, (N_HEADS, LANES), 0)
            upd = jnp.zeros((N_HEADS, LANES), F32)
            for head, val in dsink_vals.items():
                upd = upd + jnp.where(rowid == head, val, 0.0)
            dsink_ref[...] += upd

        @pl.when(j == n_blocks)
        def _():
            dkv_ref[...] = carry[...]

    last = n_blocks - 1

    def blk(width):
        return pl.BlockSpec((batch, TIME_BLOCK, width), lambda j: (0, jnp.minimum(j, last), 0))

    prev = pl.BlockSpec((batch, TIME_BLOCK, 2 * KV_W), lambda j: (0, jnp.maximum(jnp.minimum(j, last) - 1, 0), 0))
    dkv_spec = pl.BlockSpec((batch, TIME_BLOCK, 2 * KV_W), lambda j: (0, jnp.maximum(j - 1, 0), 0))
    kv3 = _by_example(kv, batch)
    dq, dkv, dsink = pl.pallas_call(
        body, name="attn_bwd", grid=(n_blocks + 1,),
        in_specs=[pl.BlockSpec(memory_space=pltpu.SMEM), blk(ATTN_W), blk(2 * KV_W), prev, blk(ATTN_W), blk(ATTN_W),
                  blk(N_HEADS)],
        out_specs=[blk(ATTN_W), dkv_spec, pl.BlockSpec((N_HEADS, LANES), lambda j: (0, 0))],
        out_shape=[_sds((batch, rows // batch, ATTN_W), F32), _sds((batch, rows // batch, 2 * KV_W), F32),
                   _sds((N_HEADS, LANES), F32)],
        scratch_shapes=[pltpu.VMEM((batch, TIME_BLOCK, 2 * KV_W), F32)],
        compiler_params=_params("arbitrary"),
    )(sinks, _by_example(q, batch), kv3, kv3, _by_example(out, batch), _by_example(d_out, batch), _by_example(lse, batch))
    return dq.reshape(rows, ATTN_W), dkv.reshape(rows, 2 * KV_W), dsink


def _conv_taps(xb, prev8):
    ext = jnp.concatenate([prev8, xb], axis=0)
    n = ext.shape[0]
    return [xb] + [pltpu.roll(ext, k, 0)[SUBLANES:n] for k in range(1, CONV_TAPS)]


def _lru_gates(xc, wa, ba, wx, bx, lam):
    xcb = xc.astype(BF16)
    r = _sigmoid(_dot(xcb, wa) + ba)
    i = _sigmoid(_dot(xcb, wx) + bx)
    sp = jnp.maximum(-lam, 0.0) + _log1p(jnp.exp(-jnp.abs(lam)))
    log_a = -LRU_C * r * sp
    a = jnp.exp(log_a)
    mult = jnp.sqrt(_one_minus_square(a, log_a))
    return xcb, r, i, sp, a, mult


def _scan_fwd(a, u, h_before):
    n, d = a.shape
    groups = n // SUBLANES
    a = a.reshape(groups, SUBLANES, d)
    u = u.reshape(groups, SUBLANES, d)
    sub = lax.broadcasted_iota(jnp.int32, a.shape, 1)
    s = 1
    while s < SUBLANES:
        valid = sub >= s
        u = jnp.where(valid, u + a * pltpu.roll(u, s, 1), u)
        a = jnp.where(valid, a * pltpu.roll(a, s, 1), a)
        s *= 2
    out, prev = [], h_before
    for g in range(groups):
        out.append(u[g] + a[g] * prev)
        prev = out[-1][SUBLANES - 1:SUBLANES, :]
    return jnp.concatenate(out, axis=0)


def _scan_rev(cf, g, d_after):
    n, d = g.shape
    groups = n // SUBLANES
    cf = cf.reshape(groups, SUBLANES, d)
    g = g.reshape(groups, SUBLANES, d)
    sub = lax.broadcasted_iota(jnp.int32, g.shape, 1)
    s = 1
    while s < SUBLANES:
        valid = sub + s < SUBLANES
        g = jnp.where(valid, g + cf * pltpu.roll(g, SUBLANES - s, 1), g)
        cf = jnp.where(valid, cf * pltpu.roll(cf, SUBLANES - s, 1), cf)
        s *= 2
    out, nxt = [None] * groups, d_after
    for k in reversed(range(groups)):
        out[k] = g[k] + cf[k] * nxt
        nxt = out[k][0:1, :]
    return jnp.concatenate(out, axis=0)


def _lru_fwd(xg, conv_w, conv_b, wa, ba, wx, bx, lam, batch, n_blocks):
    rows = xg.shape[0]

    def body(xg_ref, cw_ref, cb_ref, wa_ref, ba_ref, wx_ref, bx_ref, lam_ref, hs_ref, lru_ref, x_prev, h_carry):
        @pl.when(pl.program_id(1) == 0)
        def _():
            x_prev[...] = jnp.zeros_like(x_prev)
            h_carry[...] = jnp.zeros_like(h_carry)

        xb = xg_ref[:, :LRU_W]
        taps = _conv_taps(xb, x_prev[...])
        xc = cb_ref[...] + sum(cw_ref[CONV_TAPS - 1 - k:CONV_TAPS - k, :] * taps[k] for k in range(CONV_TAPS))
        _, _, i, _, a, mult = _lru_gates(xc, wa_ref[...], ba_ref[...], wx_ref[...], bx_ref[...], lam_ref[...])
        h = _scan_fwd(a, mult * i * xc, h_carry[...])
        hs_ref[...] = h
        lru_ref[...] = h * _gelu(xg_ref[:, LRU_W:])
        h_carry[...] = h[TIME_BLOCK - 1:TIME_BLOCK, :]
        x_prev[...] = xb[TIME_BLOCK - SUBLANES:TIME_BLOCK, :]

    def blk(width):
        return pl.BlockSpec((TIME_BLOCK, width), lambda b, j: (b * n_blocks + j, 0))

    small = [conv_w, conv_b, wa, ba, wx, bx, lam]
    return pl.pallas_call(
        body, name="lru_fwd", grid=(batch, n_blocks),
        in_specs=[blk(2 * LRU_W)] + [_whole_spec(a) for a in small],
        out_specs=[blk(LRU_W), blk(LRU_W)],
        out_shape=[_sds((rows, LRU_W), F32), _sds((rows, LRU_W), F32)],
        scratch_shapes=[pltpu.VMEM((SUBLANES, LRU_W), F32), pltpu.VMEM((1, LRU_W), F32)],
        compiler_params=_params("arbitrary", "arbitrary"),
    )(xg, *small)


def _lru_bwd(xg, hs, d_lru, conv_w, conv_b, wa, ba, wx, bx, lam, batch, n_blocks):
    rows = xg.shape[0]
    last_row = TIME_BLOCK - 1

    def body(xg_ref, xgh_ref, hs_ref, hsh_ref, dl_ref, cw_ref, cb_ref, wa_ref, ba_ref, wx_ref, bx_ref, lam_ref,
             dxg_ref, dwa_ref, dwx_ref, vec_ref, dh_carry, dxc_next):
        b = pl.program_id(0)
        j = pl.program_id(1)

        @pl.when((b == 0) & (j == 0))
        def _():
            dwa_ref[...] = jnp.zeros_like(dwa_ref)
            dwx_ref[...] = jnp.zeros_like(dwx_ref)
            vec_ref[...] = jnp.zeros_like(vec_ref)

        @pl.when(j == 0)
        def _():
            dh_carry[...] = jnp.zeros_like(dh_carry)
            dxc_next[...] = jnp.zeros_like(dxc_next)

        first = j == n_blocks - 1
        xb = xg_ref[:, :LRU_W]
        prev8 = jnp.where(first, 0.0, xgh_ref[:, :LRU_W])
        h_before = jnp.where(first, 0.0, hsh_ref[SUBLANES - 1:SUBLANES, :])
        cw = cw_ref[...]
        lam = lam_ref[...]
        wa = wa_ref[...]
        wx = wx_ref[...]
        taps = _conv_taps(xb, prev8)
        xc = cb_ref[...] + sum(cw[CONV_TAPS - 1 - k:CONV_TAPS - k, :] * taps[k] for k in range(CONV_TAPS))
        xcb, r, i, sp, a, mult = _lru_gates(xc, wa, ba_ref[...], wx, bx_ref[...], lam)
        hs = hs_ref[...]
        row = lax.broadcasted_iota(jnp.int32, hs.shape, 0)
        h_prev = jnp.where(row == 0, h_before, pltpu.roll(hs, 1, 0))
        dl = dl_ref[...]
        gate, dgate = _gelu_and_grad(xg_ref[:, LRU_W:])
        dxg_ref[:, LRU_W:] = (dl * hs * dgate).astype(BF16)
        cf = jnp.where(row == last_row, 1.0, pltpu.roll(a, last_row, 0))
        dh = _scan_rev(cf, dl * gate, dh_carry[...])
        dh_carry[...] = a[0:1, :] * dh[0:1, :]
        dmult = dh * i * xc
        di = dh * mult * xc
        dxc = dh * mult * i
        dlog_a = dh * h_prev * a - dmult * (a * a / mult)
        dr = dlog_a * (-LRU_C * sp)
        dlam = jnp.sum(dlog_a * (-LRU_C * r), axis=0, keepdims=True) * (-_sigmoid(-lam))
        dpr = dr * r * (1.0 - r)
        dpi = di * i * (1.0 - i)
        dprb = dpr.astype(BF16)
        dpib = dpi.astype(BF16)
        dxc = dxc + _dot_nt(dprb, wa) + _dot_nt(dpib, wx)
        dwa_ref[...] += _dot_tn(xcb, dprb)
        dwx_ref[...] += _dot_tn(xcb, dpib)
        ext = jnp.concatenate([dxc, dxc_next[...]], axis=0)
        n = ext.shape[0]
        dxb = cw[CONV_TAPS - 1:CONV_TAPS, :] * dxc
        for k in range(1, CONV_TAPS):
            dxb = dxb + cw[CONV_TAPS - 1 - k:CONV_TAPS - k, :] * pltpu.roll(ext, n - k, 0)[:TIME_BLOCK]
        dxg_ref[:, :LRU_W] = dxb.astype(BF16)
        dxc_next[...] = dxc[:SUBLANES, :]
        vecs = [jnp.sum(dxc * taps[CONV_TAPS - 1 - t], axis=0, keepdims=True) for t in range(CONV_TAPS)]
        vecs += [jnp.sum(dxc, axis=0, keepdims=True), jnp.sum(dpr, axis=0, keepdims=True),
                 jnp.sum(dpi, axis=0, keepdims=True), dlam]
        vec_ref[...] += _put_rows(SUBLANES, vecs)

    def tblk(b, j):
        return b * n_blocks + n_blocks - 1 - j

    def blk(width):
        return pl.BlockSpec((TIME_BLOCK, width), lambda b, j: (tblk(b, j), 0))

    per8 = TIME_BLOCK // SUBLANES

    def halo(width):
        return pl.BlockSpec((SUBLANES, width), lambda b, j: (jnp.maximum(per8 * tblk(b, j) - 1, 0), 0))

    small = [conv_w, conv_b, wa, ba, wx, bx, lam]
    acc = lambda shape: pl.BlockSpec(shape, lambda b, j: (0, 0))
    return pl.pallas_call(
        body, name="lru_bwd", grid=(batch, n_blocks),
        in_specs=[blk(2 * LRU_W), halo(2 * LRU_W), blk(LRU_W), halo(LRU_W), blk(LRU_W)] + [_whole_spec(a) for a in small],
        out_specs=[blk(2 * LRU_W), acc((LRU_W, LRU_W)), acc((LRU_W, LRU_W)), acc((SUBLANES, LRU_W))],
        out_shape=[_sds((rows, 2 * LRU_W), BF16), _sds((LRU_W, LRU_W), F32), _sds((LRU_W, LRU_W), F32),
                   _sds((SUBLANES, LRU_W), F32)],
        scratch_shapes=[pltpu.VMEM((1, LRU_W), F32), pltpu.VMEM((SUBLANES, LRU_W), F32)],
        compiler_params=_params("arbitrary", "arbitrary"),
    )(xg, xg, hs, hs, d_lru, *small)


def _out_proj_fwd(attn, lru, g_attn, g_lru, w_out, h, g_post):
    rows = h.shape[0]

    def body(at_ref, lr_ref, ga_ref, gl_ref, w_ref, h_ref, gp_ref, grp_ref, o_ref, hm_ref):
        a = _rms(at_ref[...], ga_ref[...]).astype(BF16)
        l = _rms(lr_ref[...], gl_ref[...]).astype(BF16)
        grp_ref[:, :ATTN_W] = a
        grp_ref[:, ATTN_W:] = l
        o = _dot(a, w_ref[:ATTN_W, :]) + _dot(l, w_ref[ATTN_W:, :])
        o_ref[...] = o
        hm_ref[...] = h_ref[...] + _rms(o, gp_ref[...])

    tile = _proj_tile(rows)
    rs = functools.partial(_row_spec, tile=tile)
    return pl.pallas_call(
        body, name="out_proj_fwd", grid=(rows // tile,),
        in_specs=[rs(ATTN_W), rs(LRU_W), _whole_spec(g_attn), _whole_spec(g_lru), _whole_spec(w_out),
                  rs(D_MODEL), _whole_spec(g_post)],
        out_specs=[rs(D_MODEL), rs(D_MODEL), rs(D_MODEL)],
        out_shape=[_sds((rows, D_MODEL), BF16), _sds((rows, D_MODEL), F32), _sds((rows, D_MODEL), F32)],
        compiler_params=_params("parallel"),
    )(attn, lru, g_attn, g_lru, w_out, h, g_post)


def _out_proj_bwd(dh_mid, o, attn, lru, g_attn, g_lru, g_post, w_out):
    rows = o.shape[0]

    def body(dhm_ref, o_ref, at_ref, lr_ref, ga_ref, gl_ref, gp_ref, w_ref, do_ref, dat_ref, dlr_ref, gacc_ref):
        @pl.when(pl.program_id(0) == 0)
        def _():
            gacc_ref[...] = jnp.zeros_like(gacc_ref)

        do, dgp = _rms_bwd(dhm_ref[...], o_ref[...], gp_ref[...])
        dob = do.astype(BF16)
        do_ref[...] = dob
        dat, dga = _rms_bwd(_dot_nt(dob, w_ref[:ATTN_W, :]), at_ref[...], ga_ref[...])
        dlr, dgl = _rms_bwd(_dot_nt(dob, w_ref[ATTN_W:, :]), lr_ref[...], gl_ref[...])
        dat_ref[...] = dat
        dlr_ref[...] = dlr
        gacc_ref[...] += _put_rows(SUBLANES, [dgp, jnp.concatenate([dga, dgl], axis=1)])

    tile = _proj_tile(rows)
    rs = functools.partial(_row_spec, tile=tile)
    return pl.pallas_call(
        body, name="out_proj_bwd", grid=(rows // tile,),
        in_specs=[rs(D_MODEL), rs(D_MODEL), rs(ATTN_W), rs(LRU_W), _whole_spec(g_attn),
                  _whole_spec(g_lru), _whole_spec(g_post), _whole_spec(w_out)],
        out_specs=[rs(D_MODEL), rs(ATTN_W), rs(LRU_W),
                   pl.BlockSpec((SUBLANES, D_MODEL), lambda i: (0, 0))],
        out_shape=[_sds((rows, D_MODEL), BF16), _sds((rows, ATTN_W), F32), _sds((rows, LRU_W), F32),
                   _sds((SUBLANES, D_MODEL), F32)],
        compiler_params=_params("arbitrary"),
    )(dh_mid, o, attn, lru, g_attn, g_lru, g_post, w_out)


def _mlp_fwd(h_mid, g_pre, w_up4, w_down, g_post):
    rows = h_mid.shape[0]

    def body(h_ref, g1_ref, wu_ref, wd_ref, g2_ref, z_ref, slope_ref, act_ref, y_ref, ho_ref):
        x = h_ref[...]
        z = _rms(x, g1_ref[...]).astype(BF16)
        z_ref[...] = z
        y = jnp.zeros((ROW_TILE, D_MODEL), F32)
        for s in range(N_CHIPS):
            cols = slice(s * D_MODEL, (s + 1) * D_MODEL)
            r = jnp.maximum(_dot(z, wu_ref[s]), 0.0)
            slope_ref[:, cols] = (2.0 * r).astype(BF16)
            a = jnp.square(r).astype(BF16)
            act_ref[:, cols] = a
            y = y + _dot(a, wd_ref[cols, :])
        y_ref[...] = y
        ho_ref[...] = x + _rms(y, g2_ref[...])

    return pl.pallas_call(
        body, name="mlp_fwd", grid=(rows // ROW_TILE,),
        in_specs=[_row_spec(D_MODEL), _whole_spec(g_pre), _whole_spec(w_up4), _whole_spec(w_down), _whole_spec(g_post)],
        out_specs=[_row_spec(D_MODEL), _row_spec(D_FF), _row_spec(D_FF), _row_spec(D_MODEL), _row_spec(D_MODEL)],
        out_shape=[_sds((rows, D_MODEL), BF16), _sds((rows, D_FF), BF16), _sds((rows, D_FF), BF16),
                   _sds((rows, D_MODEL), F32), _sds((rows, D_MODEL), F32)],
        compiler_params=_params("parallel"),
    )(h_mid, g_pre, w_up4, w_down, g_post)


def _mlp_bwd(dh_out, h_mid, y, slope, w_up4, w_down, g_pre, g_post):
    rows = y.shape[0]

    def body(dh_ref, hm_ref, y_ref, slope_ref, wu_ref, wd_ref, g1_ref, g2_ref, dhm_ref, dy_ref, dup_ref, gacc_ref):
        @pl.when(pl.program_id(0) == 0)
        def _():
            gacc_ref[...] = jnp.zeros_like(gacc_ref)

        dh = dh_ref[...]
        dy, dg2 = _rms_bwd(dh, y_ref[...], g2_ref[...])
        dyb = dy.astype(BF16)
        dy_ref[...] = dyb
        dz = jnp.zeros((ROW_TILE, D_MODEL), F32)
        for s in range(N_CHIPS):
            cols = slice(s * D_MODEL, (s + 1) * D_MODEL)
            dact = _dot_nt(dyb, wd_ref[cols, :])
            dup = (dact * slope_ref[:, cols].astype(F32)).astype(BF16)
            dup_ref[:, cols] = dup
            dz = dz + _dot_nt(dup, wu_ref[s])
        dx, dg1 = _rms_bwd(dz, hm_ref[...], g1_ref[...])
        dhm_ref[...] = dh + dx
        gacc_ref[...] += _put_rows(SUBLANES, [dg1, dg2])

    return pl.pallas_call(
        body, name="mlp_bwd", grid=(rows // ROW_TILE,),
        in_specs=[_row_spec(D_MODEL), _row_spec(D_MODEL), _row_spec(D_MODEL), _row_spec(D_FF), _whole_spec(w_up4),
                  _whole_spec(w_down), _whole_spec(g_pre), _whole_spec(g_post)],
        out_specs=[_row_spec(D_MODEL), _row_spec(D_MODEL), _row_spec(D_FF),
                   pl.BlockSpec((SUBLANES, D_MODEL), lambda i: (0, 0))],
        out_shape=[_sds((rows, D_MODEL), F32), _sds((rows, D_MODEL), BF16), _sds((rows, D_FF), BF16),
                   _sds((SUBLANES, D_MODEL), F32)],
        compiler_params=_params("arbitrary"),
    )(dh_out, h_mid, y, slope, w_up4, w_down, g_pre, g_post)


def _matmul_tn(a, b, tm, tn, name, out_dtype, column_blocks=False):
    rows, m = a.shape
    n = b.shape[1]

    def body(a_ref, b_ref, o_ref):
        o_ref[...] = _dot_tn(a_ref[...], b_ref[...]).astype(out_dtype)

    if column_blocks:
        out_spec = pl.BlockSpec((None, tm, tn), lambda i, j: (j, i, 0))
        out_shape = _sds((n // tn, m, tn), out_dtype)
    else:
        out_spec = pl.BlockSpec((tm, tn), lambda i, j: (i, j))
        out_shape = _sds((m, n), out_dtype)
    return pl.pallas_call(
        body, name=name, grid=(m // tm, n // tn),
        in_specs=[pl.BlockSpec((rows, tm), lambda i, j: (0, i)), pl.BlockSpec((rows, tn), lambda i, j: (0, j))],
        out_specs=out_spec, out_shape=out_shape,
        compiler_params=_params("parallel", "parallel"),
    )(a, b)


LOSS_COLS = 256


def _loss_head(h, loss_target):
    batch, per_example, _ = h.shape
    seq = loss_target.shape[1]
    n_real = N_META + seq
    assert seq % SUBLANES == 0

    def body(h_ref, t_ref, dh_ref, l_ref):
        @pl.when((pl.program_id(0) == 0) & (pl.program_id(1) == 0))
        def _():
            l_ref[...] = jnp.zeros_like(l_ref)

        d = h_ref[N_META:n_real, :] - t_ref[...]
        dh_ref[:N_META, :] = jnp.zeros((N_META, LOSS_COLS), F32)
        dh_ref[N_META:n_real, :] = d * (1.0 / D_MODEL)
        dh_ref[n_real:, :] = jnp.zeros((per_example - n_real, LOSS_COLS), F32)
        l_ref[...] += jnp.sum(jnp.sum(d * d, axis=0, keepdims=True), axis=1, keepdims=True)

    blk = pl.BlockSpec((None, per_example, LOSS_COLS), lambda b, j: (b, 0, j))
    return pl.pallas_call(
        body, name="loss_head", grid=(batch, D_MODEL // LOSS_COLS),
        in_specs=[blk, pl.BlockSpec((None, seq, LOSS_COLS), lambda b, j: (b, 0, j))],
        out_specs=[blk, pl.BlockSpec((SUBLANES, LANES), lambda b, j: (0, 0))],
        out_shape=[_sds(h.shape, F32), _sds((SUBLANES, LANES), F32)],
        compiler_params=_params("arbitrary", "arbitrary"),
    )(h, loss_target)


def _meta_grad(dh0, batch, rows_per_example):
    per = rows_per_example // N_META

    def body(d_ref, o_ref):
        @pl.when(pl.program_id(0) == 0)
        def _():
            o_ref[...] = jnp.zeros_like(o_ref)

        o_ref[...] += d_ref[...]

    return pl.pallas_call(
        body, name="meta_grad", grid=(batch,),
        in_specs=[pl.BlockSpec((N_META, D_MODEL), lambda b: (b * per, 0))],
        out_specs=pl.BlockSpec((N_META, D_MODEL), lambda b: (0, 0)),
        out_shape=_sds((N_META, D_MODEL), F32),
        compiler_params=_params("arbitrary"),
    )(dh0)


def _elementwise_tile(rows, cols):
    tile = rows
    while tile * cols * 4 > (1 << 20) and tile % 16 == 0:
        tile //= 2
    return tile


def _sum_slots(buf, name):
    k, rows, cols = buf.shape
    tile = _elementwise_tile(rows, cols)

    def body(*refs):
        total = refs[0][...].astype(F32)
        for r in refs[1:k]:
            total = total + r[...].astype(F32)
        refs[k][...] = total

    def slot(s):
        return pl.BlockSpec((None, tile, cols), lambda i: (s, i, 0))

    return pl.pallas_call(
        body, name=name, grid=(rows // tile,),
        in_specs=[slot(s) for s in range(k)], out_specs=pl.BlockSpec((tile, cols), lambda i: (i, 0)),
        out_shape=_sds((rows, cols), F32), compiler_params=_params("parallel"),
    )(*([buf] * k))


def _sum_pieces(pieces, landed, chip):
    _, rows, cols = pieces.shape
    tile = _elementwise_tile(rows, cols)

    def body(chip_ref, own_ref, a_ref, b_ref, c_ref, o_ref):
        o_ref[...] = ((own_ref[...].astype(F32) + a_ref[...].astype(F32)) + b_ref[...].astype(F32)) + c_ref[...].astype(F32)

    def slot(offset):
        return pl.BlockSpec((None, tile, cols), lambda i, chip_ref: ((chip_ref[0] + offset) % N_CHIPS, i, 0))

    grid_spec = pltpu.PrefetchScalarGridSpec(
        num_scalar_prefetch=1, grid=(rows // tile,), in_specs=[slot(0), slot(1), slot(2), slot(3)],
        out_specs=pl.BlockSpec((tile, cols), lambda i, chip_ref: (i, 0)))
    return pl.pallas_call(
        body, name="sum_grad_pieces", grid_spec=grid_spec, out_shape=_sds((rows, cols), F32),
        compiler_params=_params("parallel"),
    )(chip, pieces, landed, landed, landed)


def _adamw(w, m, v, grads, name, layer=None, prev=None):
    rows, cols = grads[0].shape
    tile = _elementwise_tile(rows, cols)
    ng = len(grads)
    m_scale = 1.0 - ADAM_B1 ** ADAM_STEP
    v_scale = 1.0 - ADAM_B2 ** ADAM_STEP

    def body(*refs):
        w_ref, m_ref, v_ref = refs[:3]
        g_refs = refs[3:3 + ng]
        if layer is None:
            g_out, d_out, m_out, v_out = refs[-4:]
        else:
            g_out, d_out, m_out, v_out, token = refs[-5:]
            token[...] = jnp.zeros_like(token)
        g = g_refs[0][...]
        for r in g_refs[1:]:
            g = g + r[...]
        m_new = ADAM_B1 * m_ref[...] + (1.0 - ADAM_B1) * g
        v_new = ADAM_B2 * v_ref[...] + (1.0 - ADAM_B2) * (g * g)
        m_hat = m_new / m_scale
        v_hat = v_new / v_scale
        g_out[...] = g
        d_out[...] = -ADAM_LR * (m_hat / (jnp.sqrt(v_hat) + ADAM_EPS) + ADAM_WD * w_ref[...])
        m_out[...] = m_new
        v_out[...] = v_new

    spec = pl.BlockSpec((tile, cols), lambda i: (i, 0))
    if layer is None:
        return pl.pallas_call(
            body, name=name, grid=(rows // tile,),
            in_specs=[spec] * (3 + ng), out_specs=[spec] * 4, out_shape=[_sds((rows, cols), F32)] * 4,
            compiler_params=_params("parallel"),
        )(w, m, v, *grads)
    of_layer = pl.BlockSpec((None, tile, cols), lambda i: (layer, i, 0))
    carried = [] if prev is None else list(prev[:4])
    return pl.pallas_call(
        body, name=name, grid=(rows // tile,),
        in_specs=[of_layer] * 3 + [spec] * ng + [pl.BlockSpec(memory_space=pl.ANY)] * len(carried),
        out_specs=[of_layer] * 4 + [pl.BlockSpec((SUBLANES, LANES), lambda i: (0, 0))],
        out_shape=[_sds(w.shape, F32)] * 4 + [_sds((SUBLANES, LANES), F32)],
        input_output_aliases={3 + ng + k: k for k in range(len(carried))},
        compiler_params=_params("arbitrary"),
    )(w, m, v, *grads, *carried)


def _position():
    return lax.axis_index("x"), lax.axis_index("y"), lax.axis_index("c")


def _other_chips(x, y):
    return [(1 - x, y), (x, 1 - y), (1 - x, 1 - y)]


def _exchange_chips(arrays, name, scatter):
    n = len(arrays)

    def body(*refs):
        src, dst = refs[:n], refs[n:2 * n]
        send_sems, recv_sems, local_sems = refs[2 * n:]
        x, y, c = _position()
        mine = 2 * x + y
        copies = []
        for i in range(n):
            own = src[i].at[mine] if scatter else src[i]
            copies.append(pltpu.make_async_copy(own, dst[i].at[mine], local_sems.at[i]))
        for k, (px, py) in enumerate(_other_chips(x, y)):
            for i in range(n):
                piece = src[i].at[2 * px + py] if scatter else src[i]
                copies.append(pltpu.make_async_remote_copy(
                    src_ref=piece, dst_ref=dst[i].at[mine], send_sem=send_sems.at[k, i], recv_sem=recv_sems.at[k, i],
                    device_id=(px, py, c), device_id_type=MESH))
        for cp in copies:
            cp.start()
        for cp in copies:
            cp.wait()

    def out_shape(a):
        return _sds(a.shape if scatter else (N_CHIPS,) + a.shape, a.dtype)

    return pl.pallas_call(
        body, name=name, in_specs=[HBM_SPEC] * n, out_specs=[HBM_SPEC] * n, out_shape=[out_shape(a) for a in arrays],
        scratch_shapes=[pltpu.SemaphoreType.DMA((N_CHIPS - 1, n)), pltpu.SemaphoreType.DMA((N_CHIPS - 1, n)),
                        pltpu.SemaphoreType.DMA((n,))],
    )(*arrays)


def _swap_sibling(arrays, name):
    n = len(arrays)

    def body(*refs):
        src, dst = refs[:n], refs[n:2 * n]
        send_sems, recv_sems = refs[2 * n:]
        x, y, c = _position()
        copies = [pltpu.make_async_remote_copy(
            src_ref=src[i], dst_ref=dst[i], send_sem=send_sems.at[i], recv_sem=recv_sems.at[i],
            device_id=(x, y, 1 - c), device_id_type=MESH) for i in range(n)]
        for cp in copies:
            cp.start()
        for cp in copies:
            cp.wait()

    return pl.pallas_call(
        body, name=name, in_specs=[HBM_SPEC] * n, out_specs=[HBM_SPEC] * n,
        out_shape=[_sds(a.shape, a.dtype) for a in arrays],
        scratch_shapes=[pltpu.SemaphoreType.DMA((n,)), pltpu.SemaphoreType.DMA((n,))],
    )(*arrays)


SEM_SPEC = pl.BlockSpec(memory_space=pltpu.SEMAPHORE)
ANY_SPEC = pl.BlockSpec(memory_space=pl.ANY)
IN_FLIGHT = pltpu.SideEffectType.DATAFLOW_SIDE_EFFECTING


def _peer_sems(all_devices):
    return pltpu.SemaphoreType.DMA(((N_DEV if all_devices else N_CHIPS) - 1,))


def _own_slot(all_devices, x, y, c):
    return 4 * x + 2 * y + c if all_devices else 2 * x + y


def _peers(all_devices, x, y, c):
    if not all_devices:
        return [((px, py, c), 2 * px + py) for px, py in _other_chips(x, y)]
    out = []
    for fx in range(2):
        for fy in range(2):
            for fc in range(2):
                if fx or fy or fc:
                    px, py, pc = (1 - x if fx else x), (1 - y if fy else y), (1 - c if fc else c)
                    out.append(((px, py, pc), 4 * px + 2 * py + pc))
    return out


def _in_hbm(a):
    return pltpu.with_memory_space_constraint(a, pltpu.HBM)


def _place_slot(a, slot, n_slots, name):
    rows, cols = a.shape
    tile = _elementwise_tile(rows, cols)

    def body(slot_ref, a_ref, o_ref):
        o_ref[...] = a_ref[...]

    grid_spec = pltpu.PrefetchScalarGridSpec(
        num_scalar_prefetch=1, grid=(rows // tile,),
        in_specs=[pl.BlockSpec((tile, cols), lambda i, slot_ref: (i, 0))],
        out_specs=pl.BlockSpec((None, tile, cols), lambda i, slot_ref: (slot_ref[0], i, 0)))
    return pl.pallas_call(
        body, name=name, grid_spec=grid_spec, out_shape=_sds((n_slots, rows, cols), a.dtype),
        compiler_params=_params("parallel"),
    )(slot, a)


def _place_shard(w, layer, chip, name):
    _, rows, cols = w.shape
    tile = _elementwise_tile(rows, cols)

    def body(chip_ref, w_ref, o_ref):
        o_ref[...] = w_ref[...].astype(BF16)

    grid_spec = pltpu.PrefetchScalarGridSpec(
        num_scalar_prefetch=1, grid=(rows // tile,),
        in_specs=[pl.BlockSpec((None, tile, cols), lambda i, chip_ref: (layer, i, 0))],
        out_specs=pl.BlockSpec((None, tile, cols), lambda i, chip_ref: (chip_ref[0], i, 0)))
    return pl.pallas_call(
        body, name=name, grid_spec=grid_spec, out_shape=_sds((N_CHIPS, rows, cols), BF16),
        compiler_params=_params("parallel"),
    )(chip, w)


def _gather_start(bufs, after, name, all_devices=False):
    n = len(bufs)

    def body(*refs):
        buf = refs[:n]
        send, recv = refs[n + 1:2 * n + 1], refs[2 * n + 1:3 * n + 1]
        token = refs[4 * n + 1]
        x, y, c = _position()
        mine = _own_slot(all_devices, x, y, c)
        for i in range(n):
            for k, (peer, _) in enumerate(_peers(all_devices, x, y, c)):
                pltpu.make_async_remote_copy(
                    src_ref=buf[i].at[mine], dst_ref=buf[i].at[mine], send_sem=send[i].at[k], recv_sem=recv[i].at[k],
                    device_id=peer, device_id_type=MESH).start()
        token[...] = jnp.zeros_like(token)

    sems = _peer_sems(all_devices)
    out = pl.pallas_call(
        body, name=name, in_specs=[HBM_SPEC] * n + [ANY_SPEC],
        out_specs=[SEM_SPEC] * (2 * n) + [HBM_SPEC] * n + [pl.BlockSpec(memory_space=pltpu.VMEM)],
        out_shape=[sems] * (2 * n) + [pltpu.HBM(b.shape, b.dtype) for b in bufs] + [_sds((SUBLANES, LANES), F32)],
        input_output_aliases={i: 2 * n + i for i in range(n)},
        compiler_params=pltpu.CompilerParams(has_side_effects=IN_FLIGHT),
    )(*[_in_hbm(b) for b in bufs], after)
    return [(out[i], out[n + i], out[2 * n + i]) for i in range(n)], out[3 * n]


def _gather_wait(send, recv, buf, after, name, all_devices=False):
    def body(buf_ref, send_ref, recv_ref, after_ref, out_ref):
        x, y, c = _position()
        mine = _own_slot(all_devices, x, y, c)
        for k, (peer, slot) in enumerate(_peers(all_devices, x, y, c)):
            cp = pltpu.make_async_remote_copy(
                src_ref=buf_ref.at[mine], dst_ref=buf_ref.at[slot], send_sem=send_ref.at[k], recv_sem=recv_ref.at[k],
                device_id=peer, device_id_type=MESH)
            cp.wait_send()
            cp.wait_recv()

    return pl.pallas_call(
        body, name=name, in_specs=[HBM_SPEC, SEM_SPEC, SEM_SPEC, ANY_SPEC], out_specs=HBM_SPEC,
        out_shape=pltpu.HBM(buf.shape, buf.dtype), input_output_aliases={0: 0},
        compiler_params=pltpu.CompilerParams(has_side_effects=IN_FLIGHT),
    )(buf, send, recv, after)


def _scatter_start(pieces, name):
    n = len(pieces)

    def body(*refs):
        src = refs[:n]
        send, recv = refs[n:2 * n], refs[2 * n:3 * n]
        land = refs[4 * n:5 * n]
        token = refs[5 * n]
        x, y, c = _position()
        mine = 2 * x + y
        for i in range(n):
            for k, (px, py) in enumerate(_other_chips(x, y)):
                pltpu.make_async_remote_copy(
                    src_ref=src[i].at[2 * px + py], dst_ref=land[i].at[mine], send_sem=send[i].at[k], recv_sem=recv[i].at[k],
                    device_id=(px, py, c), device_id_type=MESH).start()
        token[...] = jnp.zeros_like(token)

    hbm = [pltpu.HBM(p.shape, p.dtype) for p in pieces]
    out = pl.pallas_call(
        body, name=name, in_specs=[HBM_SPEC] * n,
        out_specs=[SEM_SPEC] * (2 * n) + [HBM_SPEC] * (2 * n) + [pl.BlockSpec(memory_space=pltpu.VMEM)],
        out_shape=[_peer_sems(False)] * (2 * n) + hbm + hbm + [_sds((SUBLANES, LANES), F32)],
        input_output_aliases={i: 2 * n + i for i in range(n)},
        compiler_params=pltpu.CompilerParams(has_side_effects=IN_FLIGHT),
    )(*[_in_hbm(p) for p in pieces])
    return [(out[i], out[n + i], out[2 * n + i], out[3 * n + i]) for i in range(n)], out[4 * n]


def _scatter_wait(send, recv, pieces, land, after, name):
    def body(src_ref, land_ref, send_ref, recv_ref, after_ref, src_out, land_out):
        x, y, c = _position()
        for k, (px, py) in enumerate(_other_chips(x, y)):
            cp = pltpu.make_async_remote_copy(
                src_ref=src_ref.at[2 * px + py], dst_ref=land_ref.at[2 * px + py], send_sem=send_ref.at[k],
                recv_sem=recv_ref.at[k], device_id=(px, py, c), device_id_type=MESH)
            cp.wait_send()
            cp.wait_recv()

    return pl.pallas_call(
        body, name=name, in_specs=[HBM_SPEC, HBM_SPEC, SEM_SPEC, SEM_SPEC, ANY_SPEC], out_specs=[HBM_SPEC, HBM_SPEC],
        out_shape=[pltpu.HBM(pieces.shape, pieces.dtype), pltpu.HBM(land.shape, land.dtype)],
        input_output_aliases={0: 0, 1: 1},
        compiler_params=pltpu.CompilerParams(has_side_effects=IN_FLIGHT),
    )(pieces, land, send, recv, after)


def _rope_tables(batch, rows_per_example):
    inv_freq = ROPE_THETA ** (-jnp.arange(0, 64, 2, dtype=F32) / 64)
    ang = jnp.arange(rows_per_example, dtype=F32)[:, None] * inv_freq[None, :]
    cos, sin = jnp.cos(ang), jnp.sin(ang)
    cos128 = jnp.concatenate([cos, cos, cos, cos], axis=1)
    sin128 = jnp.concatenate([-sin, sin, -sin, sin], axis=1)
    return jnp.tile(cos128, (batch, 1)), jnp.tile(sin128, (batch, 1))


def _block_diagonal(w):
    eye = jnp.eye(LRU_BLOCKS, dtype=w.dtype)
    return (w[:, :, None, :] * eye[:, None, :, None]).reshape(LRU_W, LRU_W)


def _diagonal_blocks(dense):
    d4 = dense.reshape(LRU_BLOCKS, LRU_BLOCK, LRU_BLOCKS, LRU_BLOCK)
    return jnp.stack([d4[n, :, n, :] for n in range(LRU_BLOCKS)])


def _row(v):
    return v.reshape(1, -1)


def _local_step(x, loss_target, meta_tokens, small, depth, big_weight, on_layer_grads):
    batch, seq, _ = x.shape
    n_real = N_META + seq
    n_blocks = -(-n_real // TIME_BLOCK)
    per_example = n_blocks * TIME_BLOCK
    pad = per_example - n_real
    meta = jnp.broadcast_to(meta_tokens[None], (batch, N_META, D_MODEL))
    h = jnp.concatenate([meta, x, jnp.zeros((batch, pad, D_MODEL), F32)], axis=1).reshape(batch * per_example, D_MODEL)
    rope_cos, rope_sin = _rope_tables(batch, per_example)

    saved = []
    for l in range(depth):
        wa = _block_diagonal(small['gate_a_w'][l]).astype(BF16)
        wx = _block_diagonal(small['gate_x_w'][l]).astype(BF16)
        lru_small = (small['conv_w'][l], _row(small['conv_b'][l]), wa, _row(small['gate_a_b'][l]), wx,
                     _row(small['gate_x_b'][l]), _row(small['lru_lambda'][l]))
        w_in = big_weight('w_in', l, h)
        z1, q, kv, xg = _in_proj_fwd(h, _row(small['pre_mix_norm'][l]), w_in, rope_cos, rope_sin)
        attn, lse = _attn_fwd(q, kv, small['attn_sinks'][l], batch, n_blocks)
        hs, lru = _lru_fwd(xg, *lru_small, batch, n_blocks)
        w_out = big_weight('w_out', l, xg)
        groups, o, h_mid = _out_proj_fwd(attn, lru, _row(small['attn_group_norm'][l]), _row(small['lru_group_norm'][l]),
                                         w_out, h, _row(small['post_mix_norm'][l]))
        w_up4 = big_weight('w_up', l, hs)
        w_down = big_weight('w_down', l, hs)
        z2, slope, act, y, h_out = _mlp_fwd(h_mid, _row(small['pre_mlp_norm'][l]), w_up4, w_down,
                                         _row(small['post_mlp_norm'][l]))
        saved.append(dict(h=h, z1=z1, q=q, kv=kv, xg=xg, attn=attn, lse=lse, hs=hs, lru=lru, groups=groups, o=o,
                          h_mid=h_mid, z2=z2, slope=slope, act=act, y=y, lru_small=lru_small, w_in=w_in, w_out=w_out,
                          w_up4=w_up4, w_down=w_down))
        h = h_out

    dh, sq_err = _loss_head(h.reshape(batch, per_example, D_MODEL), loss_target)
    dh = dh.reshape(batch * per_example, D_MODEL)

    gs = {n: [None] * depth for n in REPLICATED_NAMES + ('conv_w',)}
    handed_over = None
    for l in reversed(range(depth)):
        s = saved[l]

        def ordered_after(gain):
            return gain if handed_over is None else gain + handed_over[0, 0]

        dh_mid, dy, dup, g_mlp = _mlp_bwd(dh, s['h_mid'], s['y'], s['slope'], s['w_up4'], s['w_down'],
                                          _row(small['pre_mlp_norm'][l]), ordered_after(_row(small['post_mlp_norm'][l])))
        handed_over = on_layer_grads(l, {
            'w_down': _matmul_tn(s['act'], dy, 512, D_MODEL, "grad_w_down", BF16),
            'w_up': _matmul_tn(s['z2'], dup, 512, D_MODEL, "grad_w_up", BF16, column_blocks=True)})
        do, d_attn, d_lru, g_out = _out_proj_bwd(dh_mid, s['o'], s['attn'], s['lru'], _row(small['attn_group_norm'][l]),
                                                 _row(small['lru_group_norm'][l]),
                                                 ordered_after(_row(small['post_mix_norm'][l])), s['w_out'])
        dxg, dwa, dwx, g_lru = _lru_bwd(s['xg'], s['hs'], d_lru, *s['lru_small'], batch, n_blocks)
        dq, dkv, dsink = _attn_bwd(s['q'], s['kv'], small['attn_sinks'][l], s['attn'], s['lse'], d_attn, batch, n_blocks)
        dh, dproj, g_in = _in_proj_bwd(dh_mid, s['h'], dq, dkv, dxg, rope_cos, rope_sin, _row(small['pre_mix_norm'][l]),
                                       s['w_in'])
        handed_over = on_layer_grads(l, {
            'w_out': _matmul_tn(s['groups'], do, 512, D_MODEL, "grad_w_out", BF16),
            'w_in': _matmul_tn(dproj, s['z1'], IN_COLS // 2, D_MODEL, "grad_w_in", BF16)})
        gs['pre_mlp_norm'][l], gs['post_mlp_norm'][l] = g_mlp[0], g_mlp[1]
        gs['post_mix_norm'][l] = g_out[0]
        gs['attn_group_norm'][l], gs['lru_group_norm'][l] = g_out[1, :ATTN_W], g_out[1, ATTN_W:]
        gs['pre_mix_norm'][l] = g_in[0]
        gs['attn_sinks'][l] = dsink[:, 0]
        gs['conv_w'][l] = g_lru[:CONV_TAPS]
        gs['conv_b'][l], gs['gate_a_b'][l], gs['gate_x_b'][l], gs['lru_lambda'][l] = g_lru[4], g_lru[5], g_lru[6], g_lru[7]
        gs['gate_a_w'][l] = _diagonal_blocks(dwa)
        gs['gate_x_w'][l] = _diagonal_blocks(dwx)

    grad_x = dh.reshape(batch, per_example, D_MODEL)[:, N_META:n_real]
    grad_meta = _meta_grad(dh, batch, per_example)
    small_grads = {n: jnp.stack(v) for n, v in gs.items()}
    return sq_err, grad_x, grad_meta, small_grads, handed_over


PACK_UNIT = SUBLANES * LANES


def _size(shape):
    size = 1
    for d in shape:
        size *= d
    return size


def _pack(arrays):
    parts = []
    for a in arrays:
        flat = a.reshape(-1)
        padded = -(-flat.shape[0] // PACK_UNIT) * PACK_UNIT
        if padded != flat.shape[0]:
            flat = jnp.pad(flat, (0, padded - flat.shape[0]))
        parts.append(flat.reshape(-1, LANES))
    return jnp.concatenate(parts, axis=0)


def _unpack(buf, shapes):
    out, at = [], 0
    for shp in shapes:
        size = _size(shp)
        rows = -(-size // PACK_UNIT) * SUBLANES
        part = buf[at:at + rows]
        if rows * LANES != size:
            part = part.reshape(-1)[:size]
        out.append(part.reshape(shp))
        at += rows
    return out


def kernel(x, meta_tokens, pre_mix_norm, w_in, attn_sinks, conv_w, conv_b, gate_a_w, gate_a_b, gate_x_w, gate_x_b, lru_lambda, attn_group_norm, lru_group_norm, w_out, post_mix_norm, pre_mlp_norm, w_up, w_down, post_mlp_norm, loss_target, m_meta_tokens, m_pre_mix_norm, m_w_in, m_attn_sinks, m_conv_w, m_conv_b, m_gate_a_w, m_gate_a_b, m_gate_x_w, m_gate_x_b, m_lru_lambda, m_attn_group_norm, m_lru_group_norm, m_w_out, m_post_mix_norm, m_pre_mlp_norm, m_w_up, m_w_down, m_post_mlp_norm, v_meta_tokens, v_pre_mix_norm, v_w_in, v_attn_sinks, v_conv_w, v_conv_b, v_gate_a_w, v_gate_a_b, v_gate_x_w, v_gate_x_b, v_lru_lambda, v_attn_group_norm, v_lru_group_norm, v_w_out, v_post_mix_norm, v_pre_mlp_norm, v_w_up, v_w_down, v_post_mlp_norm):
    given = dict(locals())
    w = {n: given[n] for n in WEIGHT_NAMES}
    m = {n: given['m_' + n] for n in WEIGHT_NAMES}
    v = {n: given['v_' + n] for n in WEIGHT_NAMES}
    depth = w_in.shape[0]
    for d in (w, m, v):
        d['w_in'] = jnp.swapaxes(d['w_in'], 1, 2)
    chip = 2 * lax.axis_index("x") + lax.axis_index("y")
    chip1 = chip.reshape(1).astype(jnp.int32)

    in_flight = []
    all_started = chip1
    for l in range(depth):
        bufs = [_place_shard(w[n], l, chip1, "place_" + n) for n in BIG_NAMES]
        handles, all_started = _gather_start(bufs, all_started, "gather_start_%d" % l)
        in_flight.append(dict(zip(BIG_NAMES, handles)))

    def big_weight(name, l, after):
        send, recv, buf = in_flight[l][name]
        if l == 0 and name == 'w_in':
            after = all_started
        full = _gather_wait(send, recv, buf, after, "gather_wait_%d_%s" % (l, name))
        if name == 'w_in':
            return full.reshape(IN_COLS, D_MODEL)
        if name == 'w_out':
            return full.reshape(D_MODEL, D_MODEL)
        if name == 'w_down':
            return full.reshape(D_FF, D_MODEL)
        return full

    gathered = _exchange_chips([meta_tokens, conv_w], "gather_small_weights", scatter=False)
    full_meta = jnp.concatenate([gathered[0][s] for s in range(N_CHIPS)], axis=1)
    small = {n: w[n] for n in REPLICATED_NAMES}
    small['conv_w'] = jnp.concatenate([gathered[1][s] for s in range(N_CHIPS)], axis=2)

    scattering = [{} for _ in range(depth)]

    def on_layer_grads(l, big):
        names = list(big)
        pieces = [big[n].reshape(N_CHIPS, -1, D_MODEL) for n in names]
        handles, started = _scatter_start(pieces, "scatter_start_%d_%s" % (l, names[0]))
        scattering[l].update(zip(names, handles))
        return started

    sq_err, grad_x, grad_meta, small_grads, all_handed_over = _local_step(x, loss_target, full_meta, small, depth, big_weight,
                                                                           on_layer_grads)
    loss = lax.psum(sq_err[0, 0] * (0.5 / D_MODEL), ("x", "y", "c"))

    small_names = list(REPLICATED_NAMES) + list(COLUMN_SHARDED_SMALL)
    small_full = dict(small_grads)
    small_full['meta_tokens'] = grad_meta
    device1 = (2 * chip + lax.axis_index("c")).reshape(1).astype(jnp.int32)
    packed = _place_slot(_pack([small_full[n] for n in small_names]), device1, N_DEV, "place_small_grads")
    [small_flight], done = _gather_start([packed], all_handed_over, "gather_small_grads_start", all_devices=True)

    results = {n: None for n in BIG_NAMES}
    for l in reversed(range(depth)):
        partial = []
        for n in BIG_NAMES:
            send, recv, pieces, land = scattering[l][n]
            pieces, land = _scatter_wait(send, recv, pieces, land, done, "scatter_wait_%d_%s" % (l, n))
            partial.append(_sum_pieces(pieces, land, chip1))
        sibling = _swap_sibling(partial, "swap_partial_grads")
        for n, mine, other in zip(BIG_NAMES, partial, sibling):
            results[n] = _adamw(w[n], m[n], v[n], [mine, other], "adamw_" + n, layer=l, prev=results[n])
        done = results['w_down'][4]

    slots = _gather_wait(*small_flight, done, "gather_small_grads_wait", all_devices=True)
    summed = _unpack(_sum_slots(slots, "sum_small_grads"), [small_full[n].shape for n in small_names])
    grads = dict(zip(small_names, summed))
    grads['meta_tokens'] = lax.dynamic_slice_in_dim(grads['meta_tokens'], chip * meta_tokens.shape[1], meta_tokens.shape[1], 1)
    grads['conv_w'] = lax.dynamic_slice_in_dim(grads['conv_w'], chip * conv_w.shape[2], conv_w.shape[2], 2)

    out_g, out_d, out_m, out_v = {}, {}, {}, {}
    for n in BIG_NAMES:
        out_g[n], out_d[n], out_m[n], out_v[n] = [jnp.swapaxes(r, 1, 2) if n == 'w_in' else r for r in results[n][:4]]
    shapes = [w[n].shape for n in small_names]
    res = _adamw(_pack([w[n] for n in small_names]), _pack([m[n] for n in small_names]), _pack([v[n] for n in small_names]),
                 [_pack([grads[n] for n in small_names])], "adamw_small")
    for k, store in enumerate((out_g, out_d, out_m, out_v)):
        for n, a in zip(small_names, _unpack(res[k], shapes)):
            store[n] = a

    return (loss, grad_x, *[out_g[n] for n in WEIGHT_NAMES], *[out_d[n] for n in WEIGHT_NAMES],
            *[out_m[n] for n in WEIGHT_NAMES], *[out_v[n] for n in WEIGHT_NAMES])
```

```python
import functools

import jax
import jax.numpy as jnp
from jax import lax
from jax.experimental import pallas as pl
from jax.experimental.pallas import tpu as pltpu

F32 = jnp.float32
BF16 = jnp.bfloat16

D_MODEL = 1024
N_HEADS = 8
ATTN_W = 512
KV_W = 128
LRU_W = 512
LRU_BLOCKS = 8
LRU_BLOCK = 64
IN_COLS = 1792
D_FF = 4096
N_META = 16
CONV_TAPS = 4
LRU_C = 8.0
ROPE_THETA = 10000.0
EPS = 1e-6
ATTN_SCALE = 0.125

ADAM_LR = 0.001
ADAM_B1 = 0.9
ADAM_B2 = 0.999
ADAM_EPS = 1e-08
ADAM_WD = 0.01
ADAM_STEP = 10

N_CHIPS = 4
N_DEV = 8
TIME_BLOCK = 128
ROW_TILE = 256
PROJ_ROW_TILE = 544
LANES = 128
SUBLANES = 8
MASKED = -1e30
VMEM_LIMIT = 56 * 1024 * 1024

MESH = pl.DeviceIdType.MESH
HBM_SPEC = pl.BlockSpec(memory_space=pltpu.HBM)

WEIGHT_NAMES = ['meta_tokens', 'pre_mix_norm', 'w_in', 'attn_sinks', 'conv_w', 'conv_b', 'gate_a_w', 'gate_a_b',
                'gate_x_w', 'gate_x_b', 'lru_lambda', 'attn_group_norm', 'lru_group_norm', 'w_out', 'post_mix_norm',
                'pre_mlp_norm', 'w_up', 'w_down', 'post_mlp_norm']
BIG_NAMES = ('w_in', 'w_out', 'w_up', 'w_down')
COLUMN_SHARDED_SMALL = ('meta_tokens', 'conv_w')
REPLICATED_NAMES = tuple(n for n in WEIGHT_NAMES if n not in BIG_NAMES and n not in COLUMN_SHARDED_SMALL)


def _sds(shape, dtype):
    return jax.ShapeDtypeStruct(tuple(shape), dtype)


def _params(*sem):
    return pltpu.CompilerParams(dimension_semantics=sem, vmem_limit_bytes=VMEM_LIMIT)


def _row_spec(width, tile=ROW_TILE):
    return pl.BlockSpec((tile, width), lambda i: (i, 0))


def _proj_tile(rows):
    tile = PROJ_ROW_TILE
    while rows % tile:
        tile -= 16
    return tile


def _whole_spec(a):
    nd = a.ndim
    return pl.BlockSpec(a.shape, lambda *_: (0,) * nd)


def _rms(x, g):
    r = lax.rsqrt(jnp.mean(x * x, axis=-1, keepdims=True) + EPS)
    return x * r * g


def _rms_bwd(dy, x, g):
    r = lax.rsqrt(jnp.mean(x * x, axis=-1, keepdims=True) + EPS)
    xh = x * r
    dg = jnp.sum(dy * xh, axis=0, keepdims=True)
    dxh = dy * g
    dx = r * (dxh - xh * jnp.mean(dxh * xh, axis=-1, keepdims=True))
    return dx, dg


def _rope(x, cos, sin_signed):
    width = x.shape[1]
    reps = width // LANES
    if reps > 1:
        cos = jnp.tile(cos, (1, reps))
        sin_signed = jnp.tile(sin_signed, (1, reps))
    lane = lax.broadcasted_iota(jnp.int32, x.shape, 1)
    first_half = jnp.bitwise_and(lane, 32) == 0
    other = jnp.where(first_half, pltpu.roll(x, width - 32, 1), pltpu.roll(x, 32, 1))
    return x * cos + other * sin_signed


def _sigmoid(x):
    return 1.0 / (1.0 + jnp.exp(-x))


def _log1p(e):
    return jnp.where(e < 1e-3, e * (1.0 - e * (0.5 - e * (1.0 / 3.0))), jnp.log(1.0 + e))


def _one_minus_square(a, log_a):
    x = 2.0 * log_a
    return jnp.where(x > -0.002, x * (-1.0 - 0.5 * x), 1.0 - a * a)


GELU_K = 0.7978845608028654
GELU_C = 0.044715


def _gelu(x):
    t = jnp.tanh(GELU_K * (x + GELU_C * x * x * x))
    return 0.5 * x * (1.0 + t)


def _gelu_and_grad(x):
    x2 = x * x
    t = jnp.tanh(GELU_K * (x + GELU_C * x * x2))
    val = 0.5 * x * (1.0 + t)
    grad = 0.5 * (1.0 + t) + 0.5 * x * (1.0 - t * t) * GELU_K * (1.0 + 3.0 * GELU_C * x2)
    return val, grad


def _dot(a, b):
    return jnp.dot(a, b, preferred_element_type=F32)


def _dot_nt(a, b):
    return lax.dot_general(a, b, (((1,), (1,)), ((), ())), preferred_element_type=F32)


def _dot_tn(a, b):
    return lax.dot_general(a, b, (((0,), (0,)), ((), ())), preferred_element_type=F32)


def _put_rows(rows_8, values):
    d = values[0].shape[1]
    rowid = lax.broadcasted_iota(jnp.int32, (rows_8, d), 0)
    out = jnp.zeros((rows_8, d), F32)
    for k, v in enumerate(values):
        out = out + jnp.where(rowid == k, v, 0.0)
    return out


def _in_proj_fwd(h, gain, w_in, rope_cos, rope_sin):
    rows = h.shape[0]

    def body(h_ref, g_ref, w_ref, c_ref, s_ref, z_ref, q_ref, kv_ref, xg_ref):
        z = _rms(h_ref[...], g_ref[...]).astype(BF16)
        z_ref[...] = z
        proj = _dot_nt(z, w_ref[...])
        cos = c_ref[...]
        sin = s_ref[...]
        q_ref[...] = (_rope(proj[:, :ATTN_W], cos, sin) * ATTN_SCALE).astype(BF16)
        kv_ref[:, :KV_W] = _rope(proj[:, ATTN_W:ATTN_W + KV_W], cos, sin).astype(BF16)
        kv_ref[:, KV_W:] = proj[:, ATTN_W + KV_W:ATTN_W + 2 * KV_W].astype(BF16)
        xg_ref[...] = proj[:, ATTN_W + 2 * KV_W:]

    tile = _proj_tile(rows)
    rs = functools.partial(_row_spec, tile=tile)
    return pl.pallas_call(
        body, name="in_proj_fwd", grid=(rows // tile,),
        in_specs=[rs(D_MODEL), _whole_spec(gain), _whole_spec(w_in), rs(LANES), rs(LANES)],
        out_specs=[rs(D_MODEL), rs(ATTN_W), rs(2 * KV_W), rs(2 * LRU_W)],
        out_shape=[_sds((rows, D_MODEL), BF16), _sds((rows, ATTN_W), BF16), _sds((rows, 2 * KV_W), BF16),
                   _sds((rows, 2 * LRU_W), F32)],
        compiler_params=_params("parallel"),
    )(h, gain, w_in, rope_cos, rope_sin)


def _in_proj_bwd(dh_mid, h, dq, dkv, dxg, rope_cos, rope_sin, gain, w_in):
    rows = h.shape[0]

    def body(dhm_ref, h_ref, dq_ref, dkv_ref, dxg_ref, c_ref, s_ref, g_ref, w_ref, dh_ref, dp_ref, gacc_ref):
        @pl.when(pl.program_id(0) == 0)
        def _():
            gacc_ref[...] = jnp.zeros_like(gacc_ref)

        cos = c_ref[...]
        sin = -s_ref[...]
        dp_ref[:, :ATTN_W] = (_rope(dq_ref[...], cos, sin) * ATTN_SCALE).astype(BF16)
        dp_ref[:, ATTN_W:ATTN_W + KV_W] = _rope(dkv_ref[:, :KV_W], cos, sin).astype(BF16)
        dp_ref[:, ATTN_W + KV_W:ATTN_W + 2 * KV_W] = dkv_ref[:, KV_W:].astype(BF16)
        dp_ref[:, ATTN_W + 2 * KV_W:] = dxg_ref[...]
        dz = _dot(dp_ref[...], w_ref[...])
        dx, dg = _rms_bwd(dz, h_ref[...], g_ref[...])
        dh_ref[...] = dhm_ref[...] + dx
        gacc_ref[...] += _put_rows(SUBLANES, [dg])

    tile = _proj_tile(rows)
    rs = functools.partial(_row_spec, tile=tile)
    return pl.pallas_call(
        body, name="in_proj_bwd", grid=(rows // tile,),
        in_specs=[rs(D_MODEL), rs(D_MODEL), rs(ATTN_W), rs(2 * KV_W), rs(2 * LRU_W),
                  rs(LANES), rs(LANES), _whole_spec(gain), _whole_spec(w_in)],
        out_specs=[rs(D_MODEL), rs(IN_COLS), pl.BlockSpec((SUBLANES, D_MODEL), lambda i: (0, 0))],
        out_shape=[_sds((rows, D_MODEL), F32), _sds((rows, IN_COLS), BF16), _sds((SUBLANES, D_MODEL), F32)],
        compiler_params=_params("arbitrary"),
    )(dh_mid, h, dq, dkv, dxg, rope_cos, rope_sin, gain, w_in)


def _kv_lane_variants(t, group):
    lane = lax.broadcasted_iota(jnp.int32, t.shape, 1)
    low = lane < 64
    swapped = pltpu.roll(t, 64, 1)
    if group == 0:
        lo, hi = jnp.where(low, t, 0.0), jnp.where(low, 0.0, swapped)
    else:
        lo, hi = jnp.where(low, swapped, 0.0), jnp.where(low, 0.0, t)
    return jnp.concatenate([lo, hi], axis=0).astype(BF16)


GROUP_ROWS = 2 * TIME_BLOCK
KEYS = 2 * TIME_BLOCK


def _window_mask(j):
    r = jnp.bitwise_and(lax.broadcasted_iota(jnp.int32, (GROUP_ROWS, KEYS), 0), TIME_BLOCK - 1)
    c = lax.broadcasted_iota(jnp.int32, (GROUP_ROWS, KEYS), 1)
    return (c > r) & (c <= r + TIME_BLOCK) & ((c >= TIME_BLOCK) | (j > 0))


def _group_rows(ref, ex, group):
    lo = 2 * group * LANES
    return jnp.concatenate([ref[ex, :, lo:lo + LANES], ref[ex, :, lo + LANES:lo + 2 * LANES]], axis=0)


def _per_head(sink_ref, group, half):
    upper = lax.broadcasted_iota(jnp.int32, (GROUP_ROWS, 1), 0) < TIME_BLOCK
    return jnp.where(upper, sink_ref[4 * group + half], sink_ref[4 * group + 2 + half])


def _by_example(a, batch):
    return a.reshape(batch, a.shape[0] // batch, a.shape[1])


def _attn_fwd(q, kv, sinks, batch, n_blocks):
    rows = q.shape[0]

    def body(sink_ref, q_ref, kvc_ref, kvp_ref, o_ref, lse_ref):
        j = pl.program_id(0)
        mask = _window_mask(j)
        lane8 = lax.broadcasted_iota(jnp.int32, (TIME_BLOCK, N_HEADS), 1)
        for ex in range(batch):
            kv2 = jnp.concatenate([kvp_ref[ex], kvc_ref[ex]], axis=0).astype(F32)
            lse_tile = jnp.zeros((TIME_BLOCK, N_HEADS), F32)
            for group in range(2):
                k_cat = _kv_lane_variants(kv2[:, :KV_W], group)
                v_cat = _kv_lane_variants(kv2[:, KV_W:], group)
                s_all = _dot_nt(_group_rows(q_ref, ex, group), k_cat)
                probs = []
                for half in range(2):
                    sink = _per_head(sink_ref, group, half)
                    s = jnp.where(mask, s_all[:, half * KEYS:(half + 1) * KEYS], MASKED)
                    m = jnp.maximum(jnp.max(s, axis=1, keepdims=True), sink)
                    e = jnp.exp(s - m)
                    den = jnp.sum(e, axis=1, keepdims=True) + jnp.exp(sink - m)
                    probs.append((e / den).astype(BF16))
                    lse = m + jnp.log(den)
                    lse_tile = lse_tile + jnp.where(lane8 == 4 * group + half, lse[:TIME_BLOCK], 0.0)
                    lse_tile = lse_tile + jnp.where(lane8 == 4 * group + 2 + half, lse[TIME_BLOCK:], 0.0)
                out = _dot(jnp.concatenate(probs, axis=1), v_cat)
                o_ref[ex, :, 2 * group * LANES:(2 * group + 1) * LANES] = out[:TIME_BLOCK]
                o_ref[ex, :, (2 * group + 1) * LANES:(2 * group + 2) * LANES] = out[TIME_BLOCK:]
            lse_ref[ex] = lse_tile

    def blk(width):
        return pl.BlockSpec((batch, TIME_BLOCK, width), lambda j: (0, j, 0))

    prev = pl.BlockSpec((batch, TIME_BLOCK, 2 * KV_W), lambda j: (0, jnp.maximum(j - 1, 0), 0))
    kv3 = _by_example(kv, batch)
    out, lse = pl.pallas_call(
        body, name="attn_fwd", grid=(n_blocks,),
        in_specs=[pl.BlockSpec(memory_space=pltpu.SMEM), blk(ATTN_W), blk(2 * KV_W), prev],
        out_specs=[blk(ATTN_W), blk(N_HEADS)],
        out_shape=[_sds((batch, rows // batch, ATTN_W), F32), _sds((batch, rows // batch, N_HEADS), F32)],
        compiler_params=_params("parallel"),
    )(sinks, _by_example(q, batch), kv3, kv3)
    return out.reshape(rows, ATTN_W), lse.reshape(rows, N_HEADS)


def _attn_bwd(q, kv, sinks, out, lse, d_out, batch, n_blocks):
    rows = q.shape[0]

    def body(sink_ref, q_ref, kvc_ref, kvp_ref, o_ref, do_ref, lse_ref, dq_ref, dkv_ref, dsink_ref, carry):
        j = pl.program_id(0)

        @pl.when(j == 0)
        def _():
            dsink_ref[...] = jnp.zeros_like(dsink_ref)

        def one_example(ex, dsink_vals):
            kv2 = jnp.concatenate([kvp_ref[ex], kvc_ref[ex]], axis=0).astype(F32)
            mask = _window_mask(j)
            lane = lax.broadcasted_iota(jnp.int32, (GROUP_ROWS, LANES), 1)
            low = lax.broadcasted_iota(jnp.int32, (KEYS, LANES), 1) < 64
            upper = lax.broadcasted_iota(jnp.int32, (GROUP_ROWS, 1), 0) < TIME_BLOCK
            lse_tile = lse_ref[ex]
            dk_tile = jnp.zeros((KEYS, KV_W), F32)
            dv_tile = jnp.zeros((KEYS, KV_W), F32)
            for group in range(2):
                k_cat = _kv_lane_variants(kv2[:, :KV_W], group)
                v_cat = _kv_lane_variants(kv2[:, KV_W:], group)
                q_rows = _group_rows(q_ref, ex, group)
                do_rows = _group_rows(do_ref, ex, group)
                do_b = do_rows.astype(BF16)
                od = do_rows * _group_rows(o_ref, ex, group)
                s_all = _dot_nt(q_rows, k_cat)
                dp_all = _dot_nt(do_b, v_cat)
                probs, dss = [], []
                for half in range(2):
                    heads = (4 * group + half, 4 * group + 2 + half)
                    sink = _per_head(sink_ref, group, half)
                    lse_h = jnp.concatenate([lse_tile[:, h:h + 1] for h in heads], axis=0)
                    in_half = (lane < 64) if half == 0 else (lane >= 64)
                    delta = jnp.sum(jnp.where(in_half, od, 0.0), axis=1, keepdims=True)
                    cols = slice(half * KEYS, (half + 1) * KEYS)
                    prob = jnp.exp(jnp.where(mask, s_all[:, cols], MASKED) - lse_h)
                    probs.append(prob.astype(BF16))
                    dss.append((prob * (dp_all[:, cols] - delta)).astype(BF16))
                    dsink = -jnp.exp(sink - lse_h) * delta
                    dsink_vals[heads[0]] = dsink_vals[heads[0]] + jnp.sum(jnp.where(upper, dsink, 0.0), axis=0, keepdims=True)
                    dsink_vals[heads[1]] = dsink_vals[heads[1]] + jnp.sum(jnp.where(upper, 0.0, dsink), axis=0, keepdims=True)
                ds = jnp.concatenate(dss, axis=1)
                dq_rows = _dot(ds, k_cat)
                dq_ref[ex, :, 2 * group * LANES:(2 * group + 1) * LANES] = dq_rows[:TIME_BLOCK]
                dq_ref[ex, :, (2 * group + 1) * LANES:(2 * group + 2) * LANES] = dq_rows[TIME_BLOCK:]
                dk_cat = _dot_tn(ds, q_rows)
                dv_cat = _dot_tn(jnp.concatenate(probs, axis=1), do_b)
                if group == 0:
                    dk_tile = dk_tile + jnp.where(low, dk_cat[:KEYS] + pltpu.roll(dk_cat[KEYS:], 64, 1), 0.0)
                    dv_tile = dv_tile + jnp.where(low, dv_cat[:KEYS] + pltpu.roll(dv_cat[KEYS:], 64, 1), 0.0)
                else:
                    dk_tile = dk_tile + jnp.where(low, 0.0, pltpu.roll(dk_cat[:KEYS], 64, 1) + dk_cat[KEYS:])
                    dv_tile = dv_tile + jnp.where(low, 0.0, pltpu.roll(dv_cat[:KEYS], 64, 1) + dv_cat[KEYS:])

            @pl.when(j > 0)
            def _():
                dkv_ref[ex, :, :KV_W] = carry[ex, :, :KV_W] + dk_tile[:TIME_BLOCK]
                dkv_ref[ex, :, KV_W:] = carry[ex, :, KV_W:] + dv_tile[:TIME_BLOCK]

            carry[ex, :, :KV_W] = dk_tile[TIME_BLOCK:]
            carry[ex, :, KV_W:] = dv_tile[TIME_BLOCK:]

        @pl.when(j < n_blocks)
        def _():
            dsink_vals = {head: jnp.zeros((1, 1), F32) for head in range(N_HEADS)}
            for ex in range(batch):
                one_example(ex, dsink_vals)
            rowid = lax.broadcasted_iota(jnp.int32, (N_HEADS, LANES), 0)
            upd = jnp.zeros((N_HEADS, LANES), F32)
            for head, val in dsink_vals.items():
                upd = upd + jnp.where(rowid == head, val, 0.0)
            dsink_ref[...] += upd

        @pl.when(j == n_blocks)
        def _():
            dkv_ref[...] = carry[...]

    last = n_blocks - 1

    def blk(width):
        return pl.BlockSpec((batch, TIME_BLOCK, width), lambda j: (0, jnp.minimum(j, last), 0))

    prev = pl.BlockSpec((batch, TIME_BLOCK, 2 * KV_W), lambda j: (0, jnp.maximum(jnp.minimum(j, last) - 1, 0), 0))
    dkv_spec = pl.BlockSpec((batch, TIME_BLOCK, 2 * KV_W), lambda j: (0, jnp.maximum(j - 1, 0), 0))
    kv3 = _by_example(kv, batch)
    dq, dkv, dsink = pl.pallas_call(
        body, name="attn_bwd", grid=(n_blocks + 1,),
        in_specs=[pl.BlockSpec(memory_space=pltpu.SMEM), blk(ATTN_W), blk(2 * KV_W), prev, blk(ATTN_W), blk(ATTN_W),
                  blk(N_HEADS)],
        out_specs=[blk(ATTN_W), dkv_spec, pl.BlockSpec((N_HEADS, LANES), lambda j: (0, 0))],
        out_shape=[_sds((batch, rows // batch, ATTN_W), F32), _sds((batch, rows // batch, 2 * KV_W), F32),
                   _sds((N_HEADS, LANES), F32)],
        scratch_shapes=[pltpu.VMEM((batch, TIME_BLOCK, 2 * KV_W), F32)],
        compiler_params=_params("arbitrary"),
    )(sinks, _by_example(q, batch), kv3, kv3, _by_example(out, batch), _by_example(d_out, batch), _by_example(lse, batch))
    return dq.reshape(rows, ATTN_W), dkv.reshape(rows, 2 * KV_W), dsink


def _conv_taps(xb, prev8):
    ext = jnp.concatenate([prev8, xb], axis=0)
    n = ext.shape[0]
    return [xb] + [pltpu.roll(ext, k, 0)[SUBLANES:n] for k in range(1, CONV_TAPS)]


def _lru_gates(xc, wa, ba, wx, bx, lam):
    xcb = xc.astype(BF16)
    r = _sigmoid(_dot(xcb, wa) + ba)
    i = _sigmoid(_dot(xcb, wx) + bx)
    sp = jnp.maximum(-lam, 0.0) + _log1p(jnp.exp(-jnp.abs(lam)))
    log_a = -LRU_C * r * sp
    a = jnp.exp(log_a)
    mult = jnp.sqrt(_one_minus_square(a, log_a))
    return xcb, r, i, sp, a, mult


def _scan_fwd(a, u, h_before):
    n, d = a.shape
    groups = n // SUBLANES
    a = a.reshape(groups, SUBLANES, d)
    u = u.reshape(groups, SUBLANES, d)
    sub = lax.broadcasted_iota(jnp.int32, a.shape, 1)
    s = 1
    while s < SUBLANES:
        valid = sub >= s
        u = jnp.where(valid, u + a * pltpu.roll(u, s, 1), u)
        a = jnp.where(valid, a * pltpu.roll(a, s, 1), a)
        s *= 2
    out, prev = [], h_before
    for g in range(groups):
        out.append(u[g] + a[g] * prev)
        prev = out[-1][SUBLANES - 1:SUBLANES, :]
    return jnp.concatenate(out, axis=0)


def _scan_rev(cf, g, d_after):
    n, d = g.shape
    groups = n // SUBLANES
    cf = cf.reshape(groups, SUBLANES, d)
    g = g.reshape(groups, SUBLANES, d)
    sub = lax.broadcasted_iota(jnp.int32, g.shape, 1)
    s = 1
    while s < SUBLANES:
        valid = sub + s < SUBLANES
        g = jnp.where(valid, g + cf * pltpu.roll(g, SUBLANES - s, 1), g)
        cf = jnp.where(valid, cf * pltpu.roll(cf, SUBLANES - s, 1), cf)
        s *= 2
    out, nxt = [None] * groups, d_after
    for k in reversed(range(groups)):
        out[k] = g[k] + cf[k] * nxt
        nxt = out[k][0:1, :]
    return jnp.concatenate(out, axis=0)


def _lru_fwd(xg, conv_w, conv_b, wa, ba, wx, bx, lam, batch, n_blocks):
    rows = xg.shape[0]

    def body(xg_ref, cw_ref, cb_ref, wa_ref, ba_ref, wx_ref, bx_ref, lam_ref, hs_ref, lru_ref, x_prev, h_carry):
        @pl.when(pl.program_id(0) == 0)
        def _():
            x_prev[...] = jnp.zeros_like(x_prev)
            h_carry[...] = jnp.zeros_like(h_carry)

        for ex in range(batch):
            xb = xg_ref[ex, :, :LRU_W]
            taps = _conv_taps(xb, x_prev[ex])
            xc = cb_ref[...] + sum(cw_ref[CONV_TAPS - 1 - k:CONV_TAPS - k, :] * taps[k] for k in range(CONV_TAPS))
            _, _, i, _, a, mult = _lru_gates(xc, wa_ref[...], ba_ref[...], wx_ref[...], bx_ref[...], lam_ref[...])
            h = _scan_fwd(a, mult * i * xc, h_carry[ex])
            hs_ref[ex] = h
            lru_ref[ex] = h * _gelu(xg_ref[ex, :, LRU_W:])
            h_carry[ex] = h[TIME_BLOCK - 1:TIME_BLOCK, :]
            x_prev[ex] = xb[TIME_BLOCK - SUBLANES:TIME_BLOCK, :]

    def blk(width):
        return pl.BlockSpec((batch, TIME_BLOCK, width), lambda j: (0, j, 0))

    small = [conv_w, conv_b, wa, ba, wx, bx, lam]
    hs, lru = pl.pallas_call(
        body, name="lru_fwd", grid=(n_blocks,),
        in_specs=[blk(2 * LRU_W)] + [_whole_spec(a) for a in small],
        out_specs=[blk(LRU_W), blk(LRU_W)],
        out_shape=[_sds((batch, rows // batch, LRU_W), F32)] * 2,
        scratch_shapes=[pltpu.VMEM((batch, SUBLANES, LRU_W), F32), pltpu.VMEM((batch, 1, LRU_W), F32)],
        compiler_params=_params("arbitrary"),
    )(_by_example(xg, batch), *small)
    return hs.reshape(rows, LRU_W), lru.reshape(rows, LRU_W)


def _lru_bwd(xg, hs, d_lru, conv_w, conv_b, wa, ba, wx, bx, lam, batch, n_blocks):
    rows = xg.shape[0]
    last_row = TIME_BLOCK - 1

    def body(xg_ref, xgh_ref, hs_ref, hsh_ref, dl_ref, cw_ref, cb_ref, wa_ref, ba_ref, wx_ref, bx_ref, lam_ref,
             dxg_ref, dwa_ref, dwx_ref, vec_ref, dh_carry, dxc_next):
        j = pl.program_id(0)

        @pl.when(j == 0)
        def _():
            dwa_ref[...] = jnp.zeros_like(dwa_ref)
            dwx_ref[...] = jnp.zeros_like(dwx_ref)
            vec_ref[...] = jnp.zeros_like(vec_ref)
            dh_carry[...] = jnp.zeros_like(dh_carry)
            dxc_next[...] = jnp.zeros_like(dxc_next)

        for ex in range(batch):
            one_example(ex, j, xg_ref, xgh_ref, hs_ref, hsh_ref, dl_ref, cw_ref, cb_ref, wa_ref, ba_ref, wx_ref, bx_ref,
                        lam_ref, dxg_ref, dwa_ref, dwx_ref, vec_ref, dh_carry, dxc_next)

    def one_example(ex, j, xg_ref, xgh_ref, hs_ref, hsh_ref, dl_ref, cw_ref, cb_ref, wa_ref, ba_ref, wx_ref, bx_ref, lam_ref,
                    dxg_ref, dwa_ref, dwx_ref, vec_ref, dh_carry, dxc_next):
        first = j == n_blocks - 1
        xb = xg_ref[ex, :, :LRU_W]
        prev8 = jnp.where(first, 0.0, xgh_ref[ex, :, :LRU_W])
        h_before = jnp.where(first, 0.0, hsh_ref[ex, SUBLANES - 1:SUBLANES, :])
        cw = cw_ref[...]
        lam = lam_ref[...]
        wa = wa_ref[...]
        wx = wx_ref[...]
        taps = _conv_taps(xb, prev8)
        xc = cb_ref[...] + sum(cw[CONV_TAPS - 1 - k:CONV_TAPS - k, :] * taps[k] for k in range(CONV_TAPS))
        xcb, r, i, sp, a, mult = _lru_gates(xc, wa, ba_ref[...], wx, bx_ref[...], lam)
        hs = hs_ref[ex]
        row = lax.broadcasted_iota(jnp.int32, hs.shape, 0)
        h_prev = jnp.where(row == 0, h_before, pltpu.roll(hs, 1, 0))
        dl = dl_ref[ex]
        gate, dgate = _gelu_and_grad(xg_ref[ex, :, LRU_W:])
        dxg_ref[ex, :, LRU_W:] = (dl * hs * dgate).astype(BF16)
        cf = jnp.where(row == last_row, 1.0, pltpu.roll(a, last_row, 0))
        dh = _scan_rev(cf, dl * gate, dh_carry[ex])
        dh_carry[ex] = a[0:1, :] * dh[0:1, :]
        dmult = dh * i * xc
        di = dh * mult * xc
        dxc = dh * mult * i
        dlog_a = dh * h_prev * a - dmult * (a * a / mult)
        dr = dlog_a * (-LRU_C * sp)
        dlam = jnp.sum(dlog_a * (-LRU_C * r), axis=0, keepdims=True) * (-_sigmoid(-lam))
        dpr = dr * r * (1.0 - r)
        dpi = di * i * (1.0 - i)
        dprb = dpr.astype(BF16)
        dpib = dpi.astype(BF16)
        dxc = dxc + _dot_nt(dprb, wa) + _dot_nt(dpib, wx)
        dwa_ref[...] += _dot_tn(xcb, dprb)
        dwx_ref[...] += _dot_tn(xcb, dpib)
        ext = jnp.concatenate([dxc, dxc_next[ex]], axis=0)
        n = ext.shape[0]
        dxb = cw[CONV_TAPS - 1:CONV_TAPS, :] * dxc
        for k in range(1, CONV_TAPS):
            dxb = dxb + cw[CONV_TAPS - 1 - k:CONV_TAPS - k, :] * pltpu.roll(ext, n - k, 0)[:TIME_BLOCK]
        dxg_ref[ex, :, :LRU_W] = dxb.astype(BF16)
        dxc_next[ex] = dxc[:SUBLANES, :]
        vecs = [jnp.sum(dxc * taps[CONV_TAPS - 1 - t], axis=0, keepdims=True) for t in range(CONV_TAPS)]
        vecs += [jnp.sum(dxc, axis=0, keepdims=True), jnp.sum(dpr, axis=0, keepdims=True),
                 jnp.sum(dpi, axis=0, keepdims=True), dlam]
        vec_ref[...] += _put_rows(SUBLANES, vecs)

    def tblk(j):
        return n_blocks - 1 - j

    def blk(width):
        return pl.BlockSpec((batch, TIME_BLOCK, width), lambda j: (0, tblk(j), 0))

    per8 = TIME_BLOCK // SUBLANES

    def halo(width):
        return pl.BlockSpec((batch, SUBLANES, width), lambda j: (0, jnp.maximum(per8 * tblk(j) - 1, 0), 0))

    small = [conv_w, conv_b, wa, ba, wx, bx, lam]
    acc = lambda shape: pl.BlockSpec(shape, lambda j: (0, 0))
    xg3, hs3 = _by_example(xg, batch), _by_example(hs, batch)
    dxg, dwa, dwx, vec = pl.pallas_call(
        body, name="lru_bwd", grid=(n_blocks,),
        in_specs=[blk(2 * LRU_W), halo(2 * LRU_W), blk(LRU_W), halo(LRU_W), blk(LRU_W)] + [_whole_spec(a) for a in small],
        out_specs=[blk(2 * LRU_W), acc((LRU_W, LRU_W)), acc((LRU_W, LRU_W)), acc((SUBLANES, LRU_W))],
        out_shape=[_sds((batch, rows // batch, 2 * LRU_W), BF16), _sds((LRU_W, LRU_W), F32), _sds((LRU_W, LRU_W), F32),
                   _sds((SUBLANES, LRU_W), F32)],
        scratch_shapes=[pltpu.VMEM((batch, 1, LRU_W), F32), pltpu.VMEM((batch, SUBLANES, LRU_W), F32)],
        compiler_params=_params("arbitrary"),
    )(xg3, xg3, hs3, hs3, _by_example(d_lru, batch), *small)
    return dxg.reshape(rows, 2 * LRU_W), dwa, dwx, vec


def _out_proj_fwd(attn, lru, g_attn, g_lru, w_out, h, g_post):
    rows = h.shape[0]

    def body(at_ref, lr_ref, ga_ref, gl_ref, w_ref, h_ref, gp_ref, grp_ref, o_ref, hm_ref):
        a = _rms(at_ref[...], ga_ref[...]).astype(BF16)
        l = _rms(lr_ref[...], gl_ref[...]).astype(BF16)
        grp_ref[:, :ATTN_W] = a
        grp_ref[:, ATTN_W:] = l
        o = _dot(a, w_ref[:ATTN_W, :]) + _dot(l, w_ref[ATTN_W:, :])
        o_ref[...] = o
        hm_ref[...] = h_ref[...] + _rms(o, gp_ref[...])

    tile = _proj_tile(rows)
    rs = functools.partial(_row_spec, tile=tile)
    return pl.pallas_call(
        body, name="out_proj_fwd", grid=(rows // tile,),
        in_specs=[rs(ATTN_W), rs(LRU_W), _whole_spec(g_attn), _whole_spec(g_lru), _whole_spec(w_out),
                  rs(D_MODEL), _whole_spec(g_post)],
        out_specs=[rs(D_MODEL), rs(D_MODEL), rs(D_MODEL)],
        out_shape=[_sds((rows, D_MODEL), BF16), _sds((rows, D_MODEL), F32), _sds((rows, D_MODEL), F32)],
        compiler_params=_params("parallel"),
    )(attn, lru, g_attn, g_lru, w_out, h, g_post)


def _out_proj_bwd(dh_mid, o, attn, lru, g_attn, g_lru, g_post, w_out):
    rows = o.shape[0]

    def body(dhm_ref, o_ref, at_ref, lr_ref, ga_ref, gl_ref, gp_ref, w_ref, do_ref, dat_ref, dlr_ref, gacc_ref):
        @pl.when(pl.program_id(0) == 0)
        def _():
            gacc_ref[...] = jnp.zeros_like(gacc_ref)

        do, dgp = _rms_bwd(dhm_ref[...], o_ref[...], gp_ref[...])
        dob = do.astype(BF16)
        do_ref[...] = dob
        dat, dga = _rms_bwd(_dot_nt(dob, w_ref[:ATTN_W, :]), at_ref[...], ga_ref[...])
        dlr, dgl = _rms_bwd(_dot_nt(dob, w_ref[ATTN_W:, :]), lr_ref[...], gl_ref[...])
        dat_ref[...] = dat
        dlr_ref[...] = dlr
        gacc_ref[...] += _put_rows(SUBLANES, [dgp, jnp.concatenate([dga, dgl], axis=1)])

    tile = _proj_tile(rows)
    rs = functools.partial(_row_spec, tile=tile)
    return pl.pallas_call(
        body, name="out_proj_bwd", grid=(rows // tile,),
        in_specs=[rs(D_MODEL), rs(D_MODEL), rs(ATTN_W), rs(LRU_W), _whole_spec(g_attn),
                  _whole_spec(g_lru), _whole_spec(g_post), _whole_spec(w_out)],
        out_specs=[rs(D_MODEL), rs(ATTN_W), rs(LRU_W),
                   pl.BlockSpec((SUBLANES, D_MODEL), lambda i: (0, 0))],
        out_shape=[_sds((rows, D_MODEL), BF16), _sds((rows, ATTN_W), F32), _sds((rows, LRU_W), F32),
                   _sds((SUBLANES, D_MODEL), F32)],
        compiler_params=_params("arbitrary"),
    )(dh_mid, o, attn, lru, g_attn, g_lru, g_post, w_out)


def _mlp_fwd(h_mid, g_pre, w_up4, w_down, g_post):
    rows = h_mid.shape[0]

    def body(h_ref, g1_ref, wu_ref, wd_ref, g2_ref, z_ref, slope_ref, act_ref, y_ref, ho_ref):
        x = h_ref[...]
        z = _rms(x, g1_ref[...]).astype(BF16)
        z_ref[...] = z
        y = jnp.zeros((ROW_TILE, D_MODEL), F32)
        for s in range(N_CHIPS):
            cols = slice(s * D_MODEL, (s + 1) * D_MODEL)
            r = jnp.maximum(_dot(z, wu_ref[s]), 0.0)
            slope_ref[:, cols] = (2.0 * r).astype(BF16)
            a = jnp.square(r).astype(BF16)
            act_ref[:, cols] = a
            y = y + _dot(a, wd_ref[cols, :])
        y_ref[...] = y
        ho_ref[...] = x + _rms(y, g2_ref[...])

    return pl.pallas_call(
        body, name="mlp_fwd", grid=(rows // ROW_TILE,),
        in_specs=[_row_spec(D_MODEL), _whole_spec(g_pre), _whole_spec(w_up4), _whole_spec(w_down), _whole_spec(g_post)],
        out_specs=[_row_spec(D_MODEL), _row_spec(D_FF), _row_spec(D_FF), _row_spec(D_MODEL), _row_spec(D_MODEL)],
        out_shape=[_sds((rows, D_MODEL), BF16), _sds((rows, D_FF), BF16), _sds((rows, D_FF), BF16),
                   _sds((rows, D_MODEL), F32), _sds((rows, D_MODEL), F32)],
        compiler_params=_params("parallel"),
    )(h_mid, g_pre, w_up4, w_down, g_post)


def _mlp_bwd(dh_out, h_mid, y, slope, w_up4, w_down, g_pre, g_post):
    rows = y.shape[0]

    def body(dh_ref, hm_ref, y_ref, slope_ref, wu_ref, wd_ref, g1_ref, g2_ref, dhm_ref, dy_ref, dup_ref, gacc_ref):
        @pl.when(pl.program_id(0) == 0)
        def _():
            gacc_ref[...] = jnp.zeros_like(gacc_ref)

        dh = dh_ref[...]
        dy, dg2 = _rms_bwd(dh, y_ref[...], g2_ref[...])
        dyb = dy.astype(BF16)
        dy_ref[...] = dyb
        dz = jnp.zeros((ROW_TILE, D_MODEL), F32)
        for s in range(N_CHIPS):
            cols = slice(s * D_MODEL, (s + 1) * D_MODEL)
            dact = _dot_nt(dyb, wd_ref[cols, :])
            dup = (dact * slope_ref[:, cols].astype(F32)).astype(BF16)
            dup_ref[:, cols] = dup
            dz = dz + _dot_nt(dup, wu_ref[s])
        dx, dg1 = _rms_bwd(dz, hm_ref[...], g1_ref[...])
        dhm_ref[...] = dh + dx
        gacc_ref[...] += _put_rows(SUBLANES, [dg1, dg2])

    return pl.pallas_call(
        body, name="mlp_bwd", grid=(rows // ROW_TILE,),
        in_specs=[_row_spec(D_MODEL), _row_spec(D_MODEL), _row_spec(D_MODEL), _row_spec(D_FF), _whole_spec(w_up4),
                  _whole_spec(w_down), _whole_spec(g_pre), _whole_spec(g_post)],
        out_specs=[_row_spec(D_MODEL), _row_spec(D_MODEL), _row_spec(D_FF),
                   pl.BlockSpec((SUBLANES, D_MODEL), lambda i: (0, 0))],
        out_shape=[_sds((rows, D_MODEL), F32), _sds((rows, D_MODEL), BF16), _sds((rows, D_FF), BF16),
                   _sds((SUBLANES, D_MODEL), F32)],
        compiler_params=_params("arbitrary"),
    )(dh_out, h_mid, y, slope, w_up4, w_down, g_pre, g_post)


def _matmul_tn(a, b, tm, tn, name, out_dtype, column_blocks=False):
    rows, m = a.shape
    n = b.shape[1]

    def body(a_ref, b_ref, o_ref):
        o_ref[...] = _dot_tn(a_ref[...], b_ref[...]).astype(out_dtype)

    if column_blocks:
        out_spec = pl.BlockSpec((None, tm, tn), lambda i, j: (j, i, 0))
        out_shape = _sds((n // tn, m, tn), out_dtype)
    else:
        out_spec = pl.BlockSpec((tm, tn), lambda i, j: (i, j))
        out_shape = _sds((m, n), out_dtype)
    return pl.pallas_call(
        body, name=name, grid=(m // tm, n // tn),
        in_specs=[pl.BlockSpec((rows, tm), lambda i, j: (0, i)), pl.BlockSpec((rows, tn), lambda i, j: (0, j))],
        out_specs=out_spec, out_shape=out_shape,
        compiler_params=_params("parallel", "parallel"),
    )(a, b)


LOSS_COLS = 256


def _loss_head(h, loss_target):
    batch, per_example, _ = h.shape
    seq = loss_target.shape[1]
    n_real = N_META + seq
    assert seq % SUBLANES == 0

    def body(h_ref, t_ref, dh_ref, l_ref):
        @pl.when((pl.program_id(0) == 0) & (pl.program_id(1) == 0))
        def _():
            l_ref[...] = jnp.zeros_like(l_ref)

        d = h_ref[N_META:n_real, :] - t_ref[...]
        dh_ref[:N_META, :] = jnp.zeros((N_META, LOSS_COLS), F32)
        dh_ref[N_META:n_real, :] = d * (1.0 / D_MODEL)
        dh_ref[n_real:, :] = jnp.zeros((per_example - n_real, LOSS_COLS), F32)
        l_ref[...] += jnp.sum(jnp.sum(d * d, axis=0, keepdims=True), axis=1, keepdims=True)

    blk = pl.BlockSpec((None, per_example, LOSS_COLS), lambda b, j: (b, 0, j))
    return pl.pallas_call(
        body, name="loss_head", grid=(batch, D_MODEL // LOSS_COLS),
        in_specs=[blk, pl.BlockSpec((None, seq, LOSS_COLS), lambda b, j: (b, 0, j))],
        out_specs=[blk, pl.BlockSpec((SUBLANES, LANES), lambda b, j: (0, 0))],
        out_shape=[_sds(h.shape, F32), _sds((SUBLANES, LANES), F32)],
        compiler_params=_params("arbitrary", "arbitrary"),
    )(h, loss_target)


def _meta_grad(dh0, batch, rows_per_example):
    per = rows_per_example // N_META

    def body(d_ref, o_ref):
        @pl.when(pl.program_id(0) == 0)
        def _():
            o_ref[...] = jnp.zeros_like(o_ref)

        o_ref[...] += d_ref[...]

    return pl.pallas_call(
        body, name="meta_grad", grid=(batch,),
        in_specs=[pl.BlockSpec((N_META, D_MODEL), lambda b: (b * per, 0))],
        out_specs=pl.BlockSpec((N_META, D_MODEL), lambda b: (0, 0)),
        out_shape=_sds((N_META, D_MODEL), F32),
        compiler_params=_params("arbitrary"),
    )(dh0)


def _elementwise_tile(rows, cols):
    tile = rows
    while tile * cols * 4 > (1 << 20) and tile % 16 == 0:
        tile //= 2
    return tile


def _sum_slots(buf, name):
    k, rows, cols = buf.shape
    tile = _elementwise_tile(rows, cols)

    def body(*refs):
        total = refs[0][...].astype(F32)
        for r in refs[1:k]:
            total = total + r[...].astype(F32)
        refs[k][...] = total

    def slot(s):
        return pl.BlockSpec((None, tile, cols), lambda i: (s, i, 0))

    return pl.pallas_call(
        body, name=name, grid=(rows // tile,),
        in_specs=[slot(s) for s in range(k)], out_specs=pl.BlockSpec((tile, cols), lambda i: (i, 0)),
        out_shape=_sds((rows, cols), F32), compiler_params=_params("parallel"),
    )(*([buf] * k))


def _sum_pieces(pieces, landed, chip):
    _, rows, cols = pieces.shape
    tile = _elementwise_tile(rows, cols)

    def body(chip_ref, own_ref, a_ref, b_ref, c_ref, o_ref):
        total = ((own_ref[...].astype(F32) + a_ref[...].astype(F32)) + b_ref[...].astype(F32)) + c_ref[...].astype(F32)
        o_ref[...] = total.astype(BF16)

    def slot(offset):
        return pl.BlockSpec((None, tile, cols), lambda i, chip_ref: ((chip_ref[0] + offset) % N_CHIPS, i, 0))

    grid_spec = pltpu.PrefetchScalarGridSpec(
        num_scalar_prefetch=1, grid=(rows // tile,), in_specs=[slot(0), slot(1), slot(2), slot(3)],
        out_specs=pl.BlockSpec((tile, cols), lambda i, chip_ref: (i, 0)))
    return pl.pallas_call(
        body, name="sum_grad_pieces", grid_spec=grid_spec, out_shape=_sds((rows, cols), BF16),
        compiler_params=_params("parallel"),
    )(chip, pieces, landed, landed, landed)


def _adamw(w, m, v, grads, name, layer=None, prev=None):
    rows, cols = grads[0].shape
    tile = _elementwise_tile(rows, cols)
    ng = len(grads)
    m_scale = 1.0 - ADAM_B1 ** ADAM_STEP
    v_scale = 1.0 - ADAM_B2 ** ADAM_STEP

    def body(*refs):
        w_ref, m_ref, v_ref = refs[:3]
        g_refs = refs[3:3 + ng]
        if layer is None:
            g_out, d_out, m_out, v_out = refs[-4:]
        else:
            g_out, d_out, m_out, v_out, token = refs[-5:]
            token[...] = jnp.zeros_like(token)
        g = g_refs[0][...].astype(F32)
        for r in g_refs[1:]:
            g = g + r[...].astype(F32)
        m_new = ADAM_B1 * m_ref[...] + (1.0 - ADAM_B1) * g
        v_new = ADAM_B2 * v_ref[...] + (1.0 - ADAM_B2) * (g * g)
        m_hat = m_new / m_scale
        v_hat = v_new / v_scale
        g_out[...] = g
        d_out[...] = -ADAM_LR * (m_hat / (jnp.sqrt(v_hat) + ADAM_EPS) + ADAM_WD * w_ref[...])
        m_out[...] = m_new
        v_out[...] = v_new

    spec = pl.BlockSpec((tile, cols), lambda i: (i, 0))
    if layer is None:
        return pl.pallas_call(
            body, name=name, grid=(rows // tile,),
            in_specs=[spec] * (3 + ng), out_specs=[spec] * 4, out_shape=[_sds((rows, cols), F32)] * 4,
            compiler_params=_params("parallel"),
        )(w, m, v, *grads)
    of_layer = pl.BlockSpec((None, tile, cols), lambda i: (layer, i, 0))
    carried = [] if prev is None else list(prev[:4])
    return pl.pallas_call(
        body, name=name, grid=(rows // tile,),
        in_specs=[of_layer] * 3 + [spec] * ng + [pl.BlockSpec(memory_space=pl.ANY)] * len(carried),
        out_specs=[of_layer] * 4 + [pl.BlockSpec((SUBLANES, LANES), lambda i: (0, 0))],
        out_shape=[_sds(w.shape, F32)] * 4 + [_sds((SUBLANES, LANES), F32)],
        input_output_aliases={3 + ng + k: k for k in range(len(carried))},
        compiler_params=_params("arbitrary"),
    )(w, m, v, *grads, *carried)


def _position():
    return lax.axis_index("x"), lax.axis_index("y"), lax.axis_index("c")


def _other_chips(x, y):
    return [(1 - x, y), (x, 1 - y), (1 - x, 1 - y)]


def _exchange_chips(arrays, name, scatter):
    n = len(arrays)

    def body(*refs):
        src, dst = refs[:n], refs[n:2 * n]
        send_sems, recv_sems, local_sems = refs[2 * n:]
        x, y, c = _position()
        mine = 2 * x + y
        copies = []
        for i in range(n):
            own = src[i].at[mine] if scatter else src[i]
            copies.append(pltpu.make_async_copy(own, dst[i].at[mine], local_sems.at[i]))
        for k, (px, py) in enumerate(_other_chips(x, y)):
            for i in range(n):
                piece = src[i].at[2 * px + py] if scatter else src[i]
                copies.append(pltpu.make_async_remote_copy(
                    src_ref=piece, dst_ref=dst[i].at[mine], send_sem=send_sems.at[k, i], recv_sem=recv_sems.at[k, i],
                    device_id=(px, py, c), device_id_type=MESH))
        for cp in copies:
            cp.start()
        for cp in copies:
            cp.wait()

    def out_shape(a):
        return _sds(a.shape if scatter else (N_CHIPS,) + a.shape, a.dtype)

    return pl.pallas_call(
        body, name=name, in_specs=[HBM_SPEC] * n, out_specs=[HBM_SPEC] * n, out_shape=[out_shape(a) for a in arrays],
        scratch_shapes=[pltpu.SemaphoreType.DMA((N_CHIPS - 1, n)), pltpu.SemaphoreType.DMA((N_CHIPS - 1, n)),
                        pltpu.SemaphoreType.DMA((n,))],
    )(*arrays)


def _swap_sibling(arrays, name):
    n = len(arrays)

    def body(*refs):
        src, dst = refs[:n], refs[n:2 * n]
        send_sems, recv_sems = refs[2 * n:]
        x, y, c = _position()
        copies = [pltpu.make_async_remote_copy(
            src_ref=src[i], dst_ref=dst[i], send_sem=send_sems.at[i], recv_sem=recv_sems.at[i],
            device_id=(x, y, 1 - c), device_id_type=MESH) for i in range(n)]
        for cp in copies:
            cp.start()
        for cp in copies:
            cp.wait()

    return pl.pallas_call(
        body, name=name, in_specs=[HBM_SPEC] * n, out_specs=[HBM_SPEC] * n,
        out_shape=[_sds(a.shape, a.dtype) for a in arrays],
        scratch_shapes=[pltpu.SemaphoreType.DMA((n,)), pltpu.SemaphoreType.DMA((n,))],
    )(*arrays)


SEM_SPEC = pl.BlockSpec(memory_space=pltpu.SEMAPHORE)
ANY_SPEC = pl.BlockSpec(memory_space=pl.ANY)
IN_FLIGHT = pltpu.SideEffectType.DATAFLOW_SIDE_EFFECTING


def _peer_sems(all_devices):
    return pltpu.SemaphoreType.DMA(((N_DEV if all_devices else N_CHIPS) - 1,))


def _own_slot(all_devices, x, y, c):
    return 4 * x + 2 * y + c if all_devices else 2 * x + y


def _peers(all_devices, x, y, c):
    if not all_devices:
        return [((px, py, c), 2 * px + py) for px, py in _other_chips(x, y)]
    out = []
    for fx in range(2):
        for fy in range(2):
            for fc in range(2):
                if fx or fy or fc:
                    px, py, pc = (1 - x if fx else x), (1 - y if fy else y), (1 - c if fc else c)
                    out.append(((px, py, pc), 4 * px + 2 * py + pc))
    return out


def _in_hbm(a):
    return pltpu.with_memory_space_constraint(a, pltpu.HBM)


def _place_slot(a, slot, n_slots, name):
    rows, cols = a.shape
    tile = _elementwise_tile(rows, cols)

    def body(slot_ref, a_ref, o_ref):
        o_ref[...] = a_ref[...]

    grid_spec = pltpu.PrefetchScalarGridSpec(
        num_scalar_prefetch=1, grid=(rows // tile,),
        in_specs=[pl.BlockSpec((tile, cols), lambda i, slot_ref: (i, 0))],
        out_specs=pl.BlockSpec((None, tile, cols), lambda i, slot_ref: (slot_ref[0], i, 0)))
    return pl.pallas_call(
        body, name=name, grid_spec=grid_spec, out_shape=_sds((n_slots, rows, cols), a.dtype),
        compiler_params=_params("parallel"),
    )(slot, a)


def _place_shard(w, layer, chip, name):
    _, rows, cols = w.shape
    tile = _elementwise_tile(rows, cols)

    def body(chip_ref, w_ref, o_ref):
        o_ref[...] = w_ref[...].astype(BF16)

    grid_spec = pltpu.PrefetchScalarGridSpec(
        num_scalar_prefetch=1, grid=(rows // tile,),
        in_specs=[pl.BlockSpec((None, tile, cols), lambda i, chip_ref: (layer, i, 0))],
        out_specs=pl.BlockSpec((None, tile, cols), lambda i, chip_ref: (chip_ref[0], i, 0)))
    return pl.pallas_call(
        body, name=name, grid_spec=grid_spec, out_shape=_sds((N_CHIPS, rows, cols), BF16),
        compiler_params=_params("parallel"),
    )(chip, w)


def _gather_start(bufs, after, name, all_devices=False):
    n = len(bufs)

    def body(*refs):
        buf = refs[:n]
        send, recv = refs[n + 1:2 * n + 1], refs[2 * n + 1:3 * n + 1]
        token = refs[4 * n + 1]
        x, y, c = _position()
        mine = _own_slot(all_devices, x, y, c)
        for i in range(n):
            for k, (peer, _) in enumerate(_peers(all_devices, x, y, c)):
                pltpu.make_async_remote_copy(
                    src_ref=buf[i].at[mine], dst_ref=buf[i].at[mine], send_sem=send[i].at[k], recv_sem=recv[i].at[k],
                    device_id=peer, device_id_type=MESH).start()
        token[...] = jnp.zeros_like(token)

    sems = _peer_sems(all_devices)
    out = pl.pallas_call(
        body, name=name, in_specs=[HBM_SPEC] * n + [ANY_SPEC],
        out_specs=[SEM_SPEC] * (2 * n) + [HBM_SPEC] * n + [pl.BlockSpec(memory_space=pltpu.VMEM)],
        out_shape=[sems] * (2 * n) + [pltpu.HBM(b.shape, b.dtype) for b in bufs] + [_sds((SUBLANES, LANES), F32)],
        input_output_aliases={i: 2 * n + i for i in range(n)},
        compiler_params=pltpu.CompilerParams(has_side_effects=IN_FLIGHT),
    )(*[_in_hbm(b) for b in bufs], after)
    return [(out[i], out[n + i], out[2 * n + i]) for i in range(n)], out[3 * n]


def _gather_wait(send, recv, buf, after, name, all_devices=False):
    def body(buf_ref, send_ref, recv_ref, after_ref, out_ref):
        x, y, c = _position()
        mine = _own_slot(all_devices, x, y, c)
        for k, (peer, slot) in enumerate(_peers(all_devices, x, y, c)):
            cp = pltpu.make_async_remote_copy(
                src_ref=buf_ref.at[mine], dst_ref=buf_ref.at[slot], send_sem=send_ref.at[k], recv_sem=recv_ref.at[k],
                device_id=peer, device_id_type=MESH)
            cp.wait_send()
            cp.wait_recv()

    return pl.pallas_call(
        body, name=name, in_specs=[HBM_SPEC, SEM_SPEC, SEM_SPEC, ANY_SPEC], out_specs=HBM_SPEC,
        out_shape=pltpu.HBM(buf.shape, buf.dtype), input_output_aliases={0: 0},
        compiler_params=pltpu.CompilerParams(has_side_effects=IN_FLIGHT),
    )(buf, send, recv, after)


def _scatter_start(pieces, name):
    n = len(pieces)

    def body(*refs):
        src = refs[:n]
        send, recv = refs[n:2 * n], refs[2 * n:3 * n]
        land = refs[4 * n:5 * n]
        token = refs[5 * n]
        x, y, c = _position()
        mine = 2 * x + y
        for i in range(n):
            for k, (px, py) in enumerate(_other_chips(x, y)):
                pltpu.make_async_remote_copy(
                    src_ref=src[i].at[2 * px + py], dst_ref=land[i].at[mine], send_sem=send[i].at[k], recv_sem=recv[i].at[k],
                    device_id=(px, py, c), device_id_type=MESH).start()
        token[...] = jnp.zeros_like(token)

    hbm = [pltpu.HBM(p.shape, p.dtype) for p in pieces]
    out = pl.pallas_call(
        body, name=name, in_specs=[HBM_SPEC] * n,
        out_specs=[SEM_SPEC] * (2 * n) + [HBM_SPEC] * (2 * n) + [pl.BlockSpec(memory_space=pltpu.VMEM)],
        out_shape=[_peer_sems(False)] * (2 * n) + hbm + hbm + [_sds((SUBLANES, LANES), F32)],
        input_output_aliases={i: 2 * n + i for i in range(n)},
        compiler_params=pltpu.CompilerParams(has_side_effects=IN_FLIGHT),
    )(*[_in_hbm(p) for p in pieces])
    return [(out[i], out[n + i], out[2 * n + i], out[3 * n + i]) for i in range(n)], out[4 * n]


def _scatter_wait(send, recv, pieces, land, after, name):
    def body(src_ref, land_ref, send_ref, recv_ref, after_ref, src_out, land_out):
        x, y, c = _position()
        for k, (px, py) in enumerate(_other_chips(x, y)):
            cp = pltpu.make_async_remote_copy(
                src_ref=src_ref.at[2 * px + py], dst_ref=land_ref.at[2 * px + py], send_sem=send_ref.at[k],
                recv_sem=recv_ref.at[k], device_id=(px, py, c), device_id_type=MESH)
            cp.wait_send()
            cp.wait_recv()

    return pl.pallas_call(
        body, name=name, in_specs=[HBM_SPEC, HBM_SPEC, SEM_SPEC, SEM_SPEC, ANY_SPEC], out_specs=[HBM_SPEC, HBM_SPEC],
        out_shape=[pltpu.HBM(pieces.shape, pieces.dtype), pltpu.HBM(land.shape, land.dtype)],
        input_output_aliases={0: 0, 1: 1},
        compiler_params=pltpu.CompilerParams(has_side_effects=IN_FLIGHT),
    )(pieces, land, send, recv, after)


def _rope_tables(batch, rows_per_example):
    inv_freq = ROPE_THETA ** (-jnp.arange(0, 64, 2, dtype=F32) / 64)
    ang = jnp.arange(rows_per_example, dtype=F32)[:, None] * inv_freq[None, :]
    cos, sin = jnp.cos(ang), jnp.sin(ang)
    cos128 = jnp.concatenate([cos, cos, cos, cos], axis=1)
    sin128 = jnp.concatenate([-sin, sin, -sin, sin], axis=1)
    return jnp.tile(cos128, (batch, 1)), jnp.tile(sin128, (batch, 1))


def _block_diagonal(w):
    eye = jnp.eye(LRU_BLOCKS, dtype=w.dtype)
    return (w[:, :, None, :] * eye[:, None, :, None]).reshape(LRU_W, LRU_W)


def _diagonal_blocks(dense):
    d4 = dense.reshape(LRU_BLOCKS, LRU_BLOCK, LRU_BLOCKS, LRU_BLOCK)
    return jnp.stack([d4[n, :, n, :] for n in range(LRU_BLOCKS)])


def _row(v):
    return v.reshape(1, -1)


def _local_step(x, loss_target, meta_tokens, small, depth, big_weight, on_layer_grads):
    batch, seq, _ = x.shape
    n_real = N_META + seq
    n_blocks = -(-n_real // TIME_BLOCK)
    per_example = n_blocks * TIME_BLOCK
    pad = per_example - n_real
    meta = jnp.broadcast_to(meta_tokens[None], (batch, N_META, D_MODEL))
    h = jnp.concatenate([meta, x, jnp.zeros((batch, pad, D_MODEL), F32)], axis=1).reshape(batch * per_example, D_MODEL)
    rope_cos, rope_sin = _rope_tables(batch, per_example)

    saved = []
    for l in range(depth):
        wa = _block_diagonal(small['gate_a_w'][l]).astype(BF16)
        wx = _block_diagonal(small['gate_x_w'][l]).astype(BF16)
        lru_small = (small['conv_w'][l], _row(small['conv_b'][l]), wa, _row(small['gate_a_b'][l]), wx,
                     _row(small['gate_x_b'][l]), _row(small['lru_lambda'][l]))
        w_in = big_weight('w_in', l, h)
        z1, q, kv, xg = _in_proj_fwd(h, _row(small['pre_mix_norm'][l]), w_in, rope_cos, rope_sin)
        attn, lse = _attn_fwd(q, kv, small['attn_sinks'][l], batch, n_blocks)
        hs, lru = _lru_fwd(xg, *lru_small, batch, n_blocks)
        w_out = big_weight('w_out', l, xg)
        groups, o, h_mid = _out_proj_fwd(attn, lru, _row(small['attn_group_norm'][l]), _row(small['lru_group_norm'][l]),
                                         w_out, h, _row(small['post_mix_norm'][l]))
        w_up4 = big_weight('w_up', l, hs)
        w_down = big_weight('w_down', l, hs)
        z2, slope, act, y, h_out = _mlp_fwd(h_mid, _row(small['pre_mlp_norm'][l]), w_up4, w_down,
                                         _row(small['post_mlp_norm'][l]))
        saved.append(dict(h=h, z1=z1, q=q, kv=kv, xg=xg, attn=attn, lse=lse, hs=hs, lru=lru, groups=groups, o=o,
                          h_mid=h_mid, z2=z2, slope=slope, act=act, y=y, lru_small=lru_small, w_in=w_in, w_out=w_out,
                          w_up4=w_up4, w_down=w_down))
        h = h_out

    dh, sq_err = _loss_head(h.reshape(batch, per_example, D_MODEL), loss_target)
    dh = dh.reshape(batch * per_example, D_MODEL)

    gs = {n: [None] * depth for n in REPLICATED_NAMES + ('conv_w',)}
    handed_over = None
    for l in reversed(range(depth)):
        s = saved[l]

        def ordered_after(gain):
            return gain if handed_over is None else gain + handed_over[0, 0]

        dh_mid, dy, dup, g_mlp = _mlp_bwd(dh, s['h_mid'], s['y'], s['slope'], s['w_up4'], s['w_down'],
                                          _row(small['pre_mlp_norm'][l]), ordered_after(_row(small['post_mlp_norm'][l])))
        handed_over = on_layer_grads(l, {
            'w_down': _matmul_tn(s['act'], dy, 512, D_MODEL, "grad_w_down", BF16),
            'w_up': _matmul_tn(s['z2'], dup, 512, D_MODEL, "grad_w_up", BF16, column_blocks=True)})
        do, d_attn, d_lru, g_out = _out_proj_bwd(dh_mid, s['o'], s['attn'], s['lru'], _row(small['attn_group_norm'][l]),
                                                 _row(small['lru_group_norm'][l]),
                                                 ordered_after(_row(small['post_mix_norm'][l])), s['w_out'])
        dxg, dwa, dwx, g_lru = _lru_bwd(s['xg'], s['hs'], d_lru, *s['lru_small'], batch, n_blocks)
        dq, dkv, dsink = _attn_bwd(s['q'], s['kv'], small['attn_sinks'][l], s['attn'], s['lse'], d_attn, batch, n_blocks)
        dh, dproj, g_in = _in_proj_bwd(dh_mid, s['h'], dq, dkv, dxg, rope_cos, rope_sin, _row(small['pre_mix_norm'][l]),
                                       s['w_in'])
        handed_over = on_layer_grads(l, {
            'w_out': _matmul_tn(s['groups'], do, 512, D_MODEL, "grad_w_out", BF16),
            'w_in': _matmul_tn(dproj, s['z1'], IN_COLS // 2, D_MODEL, "grad_w_in", BF16)})
        gs['pre_mlp_norm'][l], gs['post_mlp_norm'][l] = g_mlp[0], g_mlp[1]
        gs['post_mix_norm'][l] = g_out[0]
        gs['attn_group_norm'][l], gs['lru_group_norm'][l] = g_out[1, :ATTN_W], g_out[1, ATTN_W:]
        gs['pre_mix_norm'][l] = g_in[0]
        gs['attn_sinks'][l] = dsink[:, 0]
        gs['conv_w'][l] = g_lru[:CONV_TAPS]
        gs['conv_b'][l], gs['gate_a_b'][l], gs['gate_x_b'][l], gs['lru_lambda'][l] = g_lru[4], g_lru[5], g_lru[6], g_lru[7]
        gs['gate_a_w'][l] = _diagonal_blocks(dwa)
        gs['gate_x_w'][l] = _diagonal_blocks(dwx)

    grad_x = dh.reshape(batch, per_example, D_MODEL)[:, N_META:n_real]
    grad_meta = _meta_grad(dh, batch, per_example)
    small_grads = {n: jnp.stack(v) for n, v in gs.items()}
    return sq_err, grad_x, grad_meta, small_grads, handed_over


PACK_UNIT = SUBLANES * LANES


def _size(shape):
    size = 1
    for d in shape:
        size *= d
    return size


def _pack(arrays):
    parts = []
    for a in arrays:
        flat = a.reshape(-1)
        padded = -(-flat.shape[0] // PACK_UNIT) * PACK_UNIT
        if padded != flat.shape[0]:
            flat = jnp.pad(flat, (0, padded - flat.shape[0]))
        parts.append(flat.reshape(-1, LANES))
    return jnp.concatenate(parts, axis=0)


def _unpack(buf, shapes):
    out, at = [], 0
    for shp in shapes:
        size = _size(shp)
        rows = -(-size // PACK_UNIT) * SUBLANES
        part = buf[at:at + rows]
        if rows * LANES != size:
            part = part.reshape(-1)[:size]
        out.append(part.reshape(shp))
        at += rows
    return out


def kernel(x, meta_tokens, pre_mix_norm, w_in, attn_sinks, conv_w, conv_b, gate_a_w, gate_a_b, gate_x_w, gate_x_b, lru_lambda, attn_group_norm, lru_group_norm, w_out, post_mix_norm, pre_mlp_norm, w_up, w_down, post_mlp_norm, loss_target, m_meta_tokens, m_pre_mix_norm, m_w_in, m_attn_sinks, m_conv_w, m_conv_b, m_gate_a_w, m_gate_a_b, m_gate_x_w, m_gate_x_b, m_lru_lambda, m_attn_group_norm, m_lru_group_norm, m_w_out, m_post_mix_norm, m_pre_mlp_norm, m_w_up, m_w_down, m_post_mlp_norm, v_meta_tokens, v_pre_mix_norm, v_w_in, v_attn_sinks, v_conv_w, v_conv_b, v_gate_a_w, v_gate_a_b, v_gate_x_w, v_gate_x_b, v_lru_lambda, v_attn_group_norm, v_lru_group_norm, v_w_out, v_post_mix_norm, v_pre_mlp_norm, v_w_up, v_w_down, v_post_mlp_norm):
    given = dict(locals())
    w = {n: given[n] for n in WEIGHT_NAMES}
    m = {n: given['m_' + n] for n in WEIGHT_NAMES}
    v = {n: given['v_' + n] for n in WEIGHT_NAMES}
    depth = w_in.shape[0]
    for d in (w, m, v):
        d['w_in'] = jnp.swapaxes(d['w_in'], 1, 2)
    chip = 2 * lax.axis_index("x") + lax.axis_index("y")
    chip1 = chip.reshape(1).astype(jnp.int32)

    in_flight = []
    all_started = chip1
    for l in range(depth):
        bufs = [_place_shard(w[n], l, chip1, "place_" + n) for n in BIG_NAMES]
        handles, all_started = _gather_start(bufs, all_started, "gather_start_%d" % l)
        in_flight.append(dict(zip(BIG_NAMES, handles)))

    def big_weight(name, l, after):
        send, recv, buf = in_flight[l][name]
        if l == 0 and name == 'w_in':
            after = all_started
        full = _gather_wait(send, recv, buf, after, "gather_wait_%d_%s" % (l, name))
        if name == 'w_in':
            return full.reshape(IN_COLS, D_MODEL)
        if name == 'w_out':
            return full.reshape(D_MODEL, D_MODEL)
        if name == 'w_down':
            return full.reshape(D_FF, D_MODEL)
        return full

    gathered = _exchange_chips([meta_tokens, conv_w], "gather_small_weights", scatter=False)
    full_meta = jnp.concatenate([gathered[0][s] for s in range(N_CHIPS)], axis=1)
    small = {n: w[n] for n in REPLICATED_NAMES}
    small['conv_w'] = jnp.concatenate([gathered[1][s] for s in range(N_CHIPS)], axis=2)

    scattering = [{} for _ in range(depth)]

    def on_layer_grads(l, big):
        names = list(big)
        pieces = [big[n].reshape(N_CHIPS, -1, D_MODEL) for n in names]
        handles, started = _scatter_start(pieces, "scatter_start_%d_%s" % (l, names[0]))
        scattering[l].update(zip(names, handles))
        return started

    sq_err, grad_x, grad_meta, small_grads, all_handed_over = _local_step(x, loss_target, full_meta, small, depth, big_weight,
                                                                           on_layer_grads)
    loss = lax.psum(sq_err[0, 0] * (0.5 / D_MODEL), ("x", "y", "c"))

    small_names = list(REPLICATED_NAMES) + list(COLUMN_SHARDED_SMALL)
    small_full = dict(small_grads)
    small_full['meta_tokens'] = grad_meta
    device1 = (2 * chip + lax.axis_index("c")).reshape(1).astype(jnp.int32)
    packed = _place_slot(_pack([small_full[n] for n in small_names]), device1, N_DEV, "place_small_grads")
    [small_flight], done = _gather_start([packed], all_handed_over, "gather_small_grads_start", all_devices=True)

    results = {n: None for n in BIG_NAMES}
    for l in reversed(range(depth)):
        partial = []
        for n in BIG_NAMES:
            send, recv, pieces, land = scattering[l][n]
            pieces, land = _scatter_wait(send, recv, pieces, land, done, "scatter_wait_%d_%s" % (l, n))
            partial.append(_sum_pieces(pieces, land, chip1))
        sibling = _swap_sibling(partial, "swap_partial_grads")
        for n, mine, other in zip(BIG_NAMES, partial, sibling):
            results[n] = _adamw(w[n], m[n], v[n], [mine, other], "adamw_" + n, layer=l, prev=results[n])
        done = results['w_down'][4]

    slots = _gather_wait(*small_flight, done, "gather_small_grads_wait", all_devices=True)
    summed = _unpack(_sum_slots(slots, "sum_small_grads"), [small_full[n].shape for n in small_names])
    grads = dict(zip(small_names, summed))
    grads['meta_tokens'] = lax.dynamic_slice_in_dim(grads['meta_tokens'], chip * meta_tokens.shape[1], meta_tokens.shape[1], 1)
    grads['conv_w'] = lax.dynamic_slice_in_dim(grads['conv_w'], chip * conv_w.shape[2], conv_w.shape[2], 2)

    out_g, out_d, out_m, out_v = {}, {}, {}, {}
    for n in BIG_NAMES:
        out_g[n], out_d[n], out_m[n], out_v[n] = [jnp.swapaxes(r, 1, 2) if n == 'w_in' else r for r in results[n][:4]]
    shapes = [w[n].shape for n in small_names]
    res = _adamw(_pack([w[n] for n in small_names]), _pack([m[n] for n in small_names]), _pack([v[n] for n in small_names]),
                 [_pack([grads[n] for n in small_names])], "adamw_small")
    for k, store in enumerate((out_g, out_d, out_m, out_v)):
        for n, a in zip(small_names, _unpack(res[k], shapes)):
            store[n] = a

    return (loss, grad_x, *[out_g[n] for n in WEIGHT_NAMES], *[out_d[n] for n in WEIGHT_NAMES],
            *[out_m[n] for n in WEIGHT_NAMES], *[out_v[n] for n in WEIGHT_NAMES])
```

```python
import functools

import jax
import jax.numpy as jnp
from jax import lax
from jax.experimental import pallas as pl
from jax.experimental.pallas import tpu as pltpu

F32 = jnp.float32
BF16 = jnp.bfloat16

D_MODEL = 1024
N_HEADS = 8
ATTN_W = 512
KV_W = 128
LRU_W = 512
LRU_BLOCKS = 8
LRU_BLOCK = 64
IN_COLS = 1792
D_FF = 4096
N_META = 16
CONV_TAPS = 4
LRU_C = 8.0
ROPE_THETA = 10000.0
EPS = 1e-6
ATTN_SCALE = 0.125

ADAM_LR = 0.001
ADAM_B1 = 0.9
ADAM_B2 = 0.999
ADAM_EPS = 1e-08
ADAM_WD = 0.01
ADAM_STEP = 10

N_CHIPS = 4
N_DEV = 8
TIME_BLOCK = 128
ROW_TILE = 256
PROJ_ROW_TILE = 544
LANES = 128
SUBLANES = 8
MASKED = -1e30
VMEM_LIMIT = 56 * 1024 * 1024

MESH = pl.DeviceIdType.MESH
HBM_SPEC = pl.BlockSpec(memory_space=pltpu.HBM)

WEIGHT_NAMES = ['meta_tokens', 'pre_mix_norm', 'w_in', 'attn_sinks', 'conv_w', 'conv_b', 'gate_a_w', 'gate_a_b',
                'gate_x_w', 'gate_x_b', 'lru_lambda', 'attn_group_norm', 'lru_group_norm', 'w_out', 'post_mix_norm',
                'pre_mlp_norm', 'w_up', 'w_down', 'post_mlp_norm']
BIG_NAMES = ('w_in', 'w_out', 'w_up', 'w_down')
COLUMN_SHARDED_SMALL = ('meta_tokens', 'conv_w')
REPLICATED_NAMES = tuple(n for n in WEIGHT_NAMES if n not in BIG_NAMES and n not in COLUMN_SHARDED_SMALL)


def _sds(shape, dtype):
    return jax.ShapeDtypeStruct(tuple(shape), dtype)


def _params(*sem):
    return pltpu.CompilerParams(dimension_semantics=sem, vmem_limit_bytes=VMEM_LIMIT)


def _row_spec(width, tile=ROW_TILE):
    return pl.BlockSpec((tile, width), lambda i: (i, 0))


def _proj_tile(rows):
    tile = PROJ_ROW_TILE
    while rows % tile:
        tile -= 16
    return tile


def _whole_spec(a):
    nd = a.ndim
    return pl.BlockSpec(a.shape, lambda *_: (0,) * nd)


def _rms(x, g):
    r = lax.rsqrt(jnp.mean(x * x, axis=-1, keepdims=True) + EPS)
    return x * r * g


def _rms_bwd(dy, x, g):
    r = lax.rsqrt(jnp.mean(x * x, axis=-1, keepdims=True) + EPS)
    xh = x * r
    dg = jnp.sum(dy * xh, axis=0, keepdims=True)
    dxh = dy * g
    dx = r * (dxh - xh * jnp.mean(dxh * xh, axis=-1, keepdims=True))
    return dx, dg


def _rope(x, cos, sin_signed):
    width = x.shape[1]
    reps = width // LANES
    if reps > 1:
        cos = jnp.tile(cos, (1, reps))
        sin_signed = jnp.tile(sin_signed, (1, reps))
    lane = lax.broadcasted_iota(jnp.int32, x.shape, 1)
    first_half = jnp.bitwise_and(lane, 32) == 0
    other = jnp.where(first_half, pltpu.roll(x, width - 32, 1), pltpu.roll(x, 32, 1))
    return x * cos + other * sin_signed


def _sigmoid(x):
    return 1.0 / (1.0 + jnp.exp(-x))


def _log1p(e):
    return jnp.where(e < 1e-3, e * (1.0 - e * (0.5 - e * (1.0 / 3.0))), jnp.log(1.0 + e))


def _one_minus_square(a, log_a):
    x = 2.0 * log_a
    return jnp.where(x > -0.002, x * (-1.0 - 0.5 * x), 1.0 - a * a)


GELU_K = 0.7978845608028654
GELU_C = 0.044715


def _gelu(x):
    t = jnp.tanh(GELU_K * (x + GELU_C * x * x * x))
    return 0.5 * x * (1.0 + t)


def _gelu_and_grad(x):
    x2 = x * x
    t = jnp.tanh(GELU_K * (x + GELU_C * x * x2))
    val = 0.5 * x * (1.0 + t)
    grad = 0.5 * (1.0 + t) + 0.5 * x * (1.0 - t * t) * GELU_K * (1.0 + 3.0 * GELU_C * x2)
    return val, grad


def _dot(a, b):
    return jnp.dot(a, b, preferred_element_type=F32)


def _dot_nt(a, b):
    return lax.dot_general(a, b, (((1,), (1,)), ((), ())), preferred_element_type=F32)


def _dot_tn(a, b):
    return lax.dot_general(a, b, (((0,), (0,)), ((), ())), preferred_element_type=F32)


def _put_rows(rows_8, values):
    d = values[0].shape[1]
    rowid = lax.broadcasted_iota(jnp.int32, (rows_8, d), 0)
    out = jnp.zeros((rows_8, d), F32)
    for k, v in enumerate(values):
        out = out + jnp.where(rowid == k, v, 0.0)
    return out


def _in_proj_fwd(h, gain, w_in, rope_cos, rope_sin):
    rows = h.shape[0]

    def body(h_ref, g_ref, w_ref, c_ref, s_ref, z_ref, q_ref, kv_ref, xg_ref):
        z = _rms(h_ref[...], g_ref[...]).astype(BF16)
        z_ref[...] = z
        proj = _dot_nt(z, w_ref[...])
        cos = c_ref[...]
        sin = s_ref[...]
        q_ref[...] = (_rope(proj[:, :ATTN_W], cos, sin) * ATTN_SCALE).astype(BF16)
        kv_ref[:, :KV_W] = _rope(proj[:, ATTN_W:ATTN_W + KV_W], cos, sin).astype(BF16)
        kv_ref[:, KV_W:] = proj[:, ATTN_W + KV_W:ATTN_W + 2 * KV_W].astype(BF16)
        xg_ref[...] = proj[:, ATTN_W + 2 * KV_W:]

    tile = _proj_tile(rows)
    rs = functools.partial(_row_spec, tile=tile)
    return pl.pallas_call(
        body, name="in_proj_fwd", grid=(rows // tile,),
        in_specs=[rs(D_MODEL), _whole_spec(gain), _whole_spec(w_in), rs(LANES), rs(LANES)],
        out_specs=[rs(D_MODEL), rs(ATTN_W), rs(2 * KV_W), rs(2 * LRU_W)],
        out_shape=[_sds((rows, D_MODEL), BF16), _sds((rows, ATTN_W), BF16), _sds((rows, 2 * KV_W), BF16),
                   _sds((rows, 2 * LRU_W), F32)],
        compiler_params=_params("parallel"),
    )(h, gain, w_in, rope_cos, rope_sin)


def _in_proj_bwd(dh_mid, h, dq, dkv, dxg, rope_cos, rope_sin, gain, w_in):
    rows = h.shape[0]

    def body(dhm_ref, h_ref, dq_ref, dkv_ref, dxg_ref, c_ref, s_ref, g_ref, w_ref, dh_ref, dp_ref, gacc_ref):
        @pl.when(pl.program_id(0) == 0)
        def _():
            gacc_ref[...] = jnp.zeros_like(gacc_ref)

        cos = c_ref[...]
        sin = -s_ref[...]
        dp_ref[:, :ATTN_W] = (_rope(dq_ref[...], cos, sin) * ATTN_SCALE).astype(BF16)
        dp_ref[:, ATTN_W:ATTN_W + KV_W] = _rope(dkv_ref[:, :KV_W], cos, sin).astype(BF16)
        dp_ref[:, ATTN_W + KV_W:ATTN_W + 2 * KV_W] = dkv_ref[:, KV_W:].astype(BF16)
        dp_ref[:, ATTN_W + 2 * KV_W:] = dxg_ref[...]
        dz = _dot(dp_ref[...], w_ref[...])
        dx, dg = _rms_bwd(dz, h_ref[...], g_ref[...])
        dh_ref[...] = dhm_ref[...] + dx
        gacc_ref[...] += _put_rows(SUBLANES, [dg])

    tile = _proj_tile(rows)
    rs = functools.partial(_row_spec, tile=tile)
    return pl.pallas_call(
        body, name="in_proj_bwd", grid=(rows // tile,),
        in_specs=[rs(D_MODEL), rs(D_MODEL), rs(ATTN_W), rs(2 * KV_W), rs(2 * LRU_W),
                  rs(LANES), rs(LANES), _whole_spec(gain), _whole_spec(w_in)],
        out_specs=[rs(D_MODEL), rs(IN_COLS), pl.BlockSpec((SUBLANES, D_MODEL), lambda i: (0, 0))],
        out_shape=[_sds((rows, D_MODEL), F32), _sds((rows, IN_COLS), BF16), _sds((SUBLANES, D_MODEL), F32)],
        compiler_params=_params("arbitrary"),
    )(dh_mid, h, dq, dkv, dxg, rope_cos, rope_sin, gain, w_in)


def _kv_lane_variants(t, group):
    lane = lax.broadcasted_iota(jnp.int32, t.shape, 1)
    low = lane < 64
    swapped = pltpu.roll(t, 64, 1)
    if group == 0:
        lo, hi = jnp.where(low, t, 0.0), jnp.where(low, 0.0, swapped)
    else:
        lo, hi = jnp.where(low, swapped, 0.0), jnp.where(low, 0.0, t)
    return jnp.concatenate([lo, hi], axis=0).astype(BF16)


GROUP_ROWS = 2 * TIME_BLOCK
KEYS = 2 * TIME_BLOCK


def _window_mask(j):
    r = jnp.bitwise_and(lax.broadcasted_iota(jnp.int32, (GROUP_ROWS, KEYS), 0), TIME_BLOCK - 1)
    c = lax.broadcasted_iota(jnp.int32, (GROUP_ROWS, KEYS), 1)
    return (c > r) & (c <= r + TIME_BLOCK) & ((c >= TIME_BLOCK) | (j > 0))


def _group_rows(ref, ex, group):
    lo = 2 * group * LANES
    return jnp.concatenate([ref[ex, :, lo:lo + LANES], ref[ex, :, lo + LANES:lo + 2 * LANES]], axis=0)


def _per_head(sink_ref, group, half):
    upper = lax.broadcasted_iota(jnp.int32, (GROUP_ROWS, 1), 0) < TIME_BLOCK
    return jnp.where(upper, sink_ref[4 * group + half], sink_ref[4 * group + 2 + half])


def _by_example(a, batch):
    return a.reshape(batch, a.shape[0] // batch, a.shape[1])


def _attn_fwd(q, kv, sinks, batch, n_blocks):
    rows = q.shape[0]

    def body(sink_ref, q_ref, kvc_ref, kvp_ref, o_ref, lse_ref):
        j = pl.program_id(0)
        mask = _window_mask(j)
        lane8 = lax.broadcasted_iota(jnp.int32, (TIME_BLOCK, N_HEADS), 1)
        for ex in range(batch):
            kv2 = jnp.concatenate([kvp_ref[ex], kvc_ref[ex]], axis=0).astype(F32)
            lse_tile = jnp.zeros((TIME_BLOCK, N_HEADS), F32)
            for group in range(2):
                k_cat = _kv_lane_variants(kv2[:, :KV_W], group)
                v_cat = _kv_lane_variants(kv2[:, KV_W:], group)
                s_all = _dot_nt(_group_rows(q_ref, ex, group), k_cat)
                probs = []
                for half in range(2):
                    sink = _per_head(sink_ref, group, half)
                    s = jnp.where(mask, s_all[:, half * KEYS:(half + 1) * KEYS], MASKED)
                    m = jnp.maximum(jnp.max(s, axis=1, keepdims=True), sink)
                    e = jnp.exp(s - m)
                    den = jnp.sum(e, axis=1, keepdims=True) + jnp.exp(sink - m)
                    probs.append((e / den).astype(BF16))
                    lse = m + jnp.log(den)
                    lse_tile = lse_tile + jnp.where(lane8 == 4 * group + half, lse[:TIME_BLOCK], 0.0)
                    lse_tile = lse_tile + jnp.where(lane8 == 4 * group + 2 + half, lse[TIME_BLOCK:], 0.0)
                out = _dot(jnp.concatenate(probs, axis=1), v_cat)
                o_ref[ex, :, 2 * group * LANES:(2 * group + 1) * LANES] = out[:TIME_BLOCK]
                o_ref[ex, :, (2 * group + 1) * LANES:(2 * group + 2) * LANES] = out[TIME_BLOCK:]
            lse_ref[ex] = lse_tile

    def blk(width):
        return pl.BlockSpec((batch, TIME_BLOCK, width), lambda j: (0, j, 0))

    prev = pl.BlockSpec((batch, TIME_BLOCK, 2 * KV_W), lambda j: (0, jnp.maximum(j - 1, 0), 0))
    kv3 = _by_example(kv, batch)
    out, lse = pl.pallas_call(
        body, name="attn_fwd", grid=(n_blocks,),
        in_specs=[pl.BlockSpec(memory_space=pltpu.SMEM), blk(ATTN_W), blk(2 * KV_W), prev],
        out_specs=[blk(ATTN_W), blk(N_HEADS)],
        out_shape=[_sds((batch, rows // batch, ATTN_W), F32), _sds((batch, rows // batch, N_HEADS), F32)],
        compiler_params=_params("parallel"),
    )(sinks, _by_example(q, batch), kv3, kv3)
    return out.reshape(rows, ATTN_W), lse.reshape(rows, N_HEADS)


def _attn_bwd(q, kv, sinks, out, lse, d_out, batch, n_blocks):
    rows = q.shape[0]

    def body(sink_ref, q_ref, kvc_ref, kvp_ref, o_ref, do_ref, lse_ref, dq_ref, dkv_ref, dsink_ref, carry):
        j = pl.program_id(0)

        @pl.when(j == 0)
        def _():
            dsink_ref[...] = jnp.zeros_like(dsink_ref)

        def one_example(ex, dsink_vals):
            kv2 = jnp.concatenate([kvp_ref[ex], kvc_ref[ex]], axis=0).astype(F32)
            mask = _window_mask(j)
            lane = lax.broadcasted_iota(jnp.int32, (GROUP_ROWS, LANES), 1)
            low = lax.broadcasted_iota(jnp.int32, (KEYS, LANES), 1) < 64
            upper = lax.broadcasted_iota(jnp.int32, (GROUP_ROWS, 1), 0) < TIME_BLOCK
            lse_tile = lse_ref[ex]
            dk_tile = jnp.zeros((KEYS, KV_W), F32)
            dv_tile = jnp.zeros((KEYS, KV_W), F32)
            for group in range(2):
                k_cat = _kv_lane_variants(kv2[:, :KV_W], group)
                v_cat = _kv_lane_variants(kv2[:, KV_W:], group)
                q_rows = _group_rows(q_ref, ex, group)
                do_rows = _group_rows(do_ref, ex, group)
                do_b = do_rows.astype(BF16)
                od = do_rows * _group_rows(o_ref, ex, group)
                s_all = _dot_nt(q_rows, k_cat)
                dp_all = _dot_nt(do_b, v_cat)
                probs, dss = [], []
                for half in range(2):
                    heads = (4 * group + half, 4 * group + 2 + half)
                    sink = _per_head(sink_ref, group, half)
                    lse_h = jnp.concatenate([lse_tile[:, h:h + 1] for h in heads], axis=0)
                    in_half = (lane < 64) if half == 0 else (lane >= 64)
                    delta = jnp.sum(jnp.where(in_half, od, 0.0), axis=1, keepdims=True)
                    cols = slice(half * KEYS, (half + 1) * KEYS)
                    prob = jnp.exp(jnp.where(mask, s_all[:, cols], MASKED) - lse_h)
                    probs.append(prob.astype(BF16))
                    dss.append((prob * (dp_all[:, cols] - delta)).astype(BF16))
                    dsink = -jnp.exp(sink - lse_h) * delta
                    dsink_vals[heads[0]] = dsink_vals[heads[0]] + jnp.sum(jnp.where(upper, dsink, 0.0), axis=0, keepdims=True)
                    dsink_vals[heads[1]] = dsink_vals[heads[1]] + jnp.sum(jnp.where(upper, 0.0, dsink), axis=0, keepdims=True)
                ds = jnp.concatenate(dss, axis=1)
                dq_rows = _dot(ds, k_cat)
                dq_ref[ex, :, 2 * group * LANES:(2 * group + 1) * LANES] = dq_rows[:TIME_BLOCK]
                dq_ref[ex, :, (2 * group + 1) * LANES:(2 * group + 2) * LANES] = dq_rows[TIME_BLOCK:]
                dk_cat = _dot_tn(ds, q_rows)
                dv_cat = _dot_tn(jnp.concatenate(probs, axis=1), do_b)
                if group == 0:
                    dk_tile = dk_tile + jnp.where(low, dk_cat[:KEYS] + pltpu.roll(dk_cat[KEYS:], 64, 1), 0.0)
                    dv_tile = dv_tile + jnp.where(low, dv_cat[:KEYS] + pltpu.roll(dv_cat[KEYS:], 64, 1), 0.0)
                else:
                    dk_tile = dk_tile + jnp.where(low, 0.0, pltpu.roll(dk_cat[:KEYS], 64, 1) + dk_cat[KEYS:])
                    dv_tile = dv_tile + jnp.where(low, 0.0, pltpu.roll(dv_cat[:KEYS], 64, 1) + dv_cat[KEYS:])

            @pl.when(j > 0)
            def _():
                dkv_ref[ex, :, :KV_W] = carry[ex, :, :KV_W] + dk_tile[:TIME_BLOCK]
                dkv_ref[ex, :, KV_W:] = carry[ex, :, KV_W:] + dv_tile[:TIME_BLOCK]

            carry[ex, :, :KV_W] = dk_tile[TIME_BLOCK:]
            carry[ex, :, KV_W:] = dv_tile[TIME_BLOCK:]

        @pl.when(j < n_blocks)
        def _():
            dsink_vals = {head: jnp.zeros((1, 1), F32) for head in range(N_HEADS)}
            for ex in range(batch):
                one_example(ex, dsink_vals)
            rowid = lax.broadcasted_iota(jnp.int32, (N_HEADS, LANES), 0)
            upd = jnp.zeros((N_HEADS, LANES), F32)
            for head, val in dsink_vals.items():
                upd = upd + jnp.where(rowid == head, val, 0.0)
            dsink_ref[...] += upd

        @pl.when(j == n_blocks)
        def _():
            dkv_ref[...] = carry[...]

    last = n_blocks - 1

    def blk(width):
        return pl.BlockSpec((batch, TIME_BLOCK, width), lambda j: (0, jnp.minimum(j, last), 0))

    prev = pl.BlockSpec((batch, TIME_BLOCK, 2 * KV_W), lambda j: (0, jnp.maximum(jnp.minimum(j, last) - 1, 0), 0))
    dkv_spec = pl.BlockSpec((batch, TIME_BLOCK, 2 * KV_W), lambda j: (0, jnp.maximum(j - 1, 0), 0))
    kv3 = _by_example(kv, batch)
    dq, dkv, dsink = pl.pallas_call(
        body, name="attn_bwd", grid=(n_blocks + 1,),
        in_specs=[pl.BlockSpec(memory_space=pltpu.SMEM), blk(ATTN_W), blk(2 * KV_W), prev, blk(ATTN_W), blk(ATTN_W),
                  blk(N_HEADS)],
        out_specs=[blk(ATTN_W), dkv_spec, pl.BlockSpec((N_HEADS, LANES), lambda j: (0, 0))],
        out_shape=[_sds((batch, rows // batch, ATTN_W), F32), _sds((batch, rows // batch, 2 * KV_W), F32),
                   _sds((N_HEADS, LANES), F32)],
        scratch_shapes=[pltpu.VMEM((batch, TIME_BLOCK, 2 * KV_W), F32)],
        compiler_params=_params("arbitrary"),
    )(sinks, _by_example(q, batch), kv3, kv3, _by_example(out, batch), _by_example(d_out, batch), _by_example(lse, batch))
    return dq.reshape(rows, ATTN_W), dkv.reshape(rows, 2 * KV_W), dsink


def _conv_taps(xb, prev8):
    ext = jnp.concatenate([prev8, xb], axis=0)
    n = ext.shape[0]
    return [xb] + [pltpu.roll(ext, k, 0)[SUBLANES:n] for k in range(1, CONV_TAPS)]


def _lru_gates(xc, wa, ba, wx, bx, lam):
    xcb = xc.astype(BF16)
    r = _sigmoid(_dot(xcb, wa) + ba)
    i = _sigmoid(_dot(xcb, wx) + bx)
    sp = jnp.maximum(-lam, 0.0) + _log1p(jnp.exp(-jnp.abs(lam)))
    log_a = -LRU_C * r * sp
    a = jnp.exp(log_a)
    mult = jnp.sqrt(_one_minus_square(a, log_a))
    return xcb, r, i, sp, a, mult


def _scan_fwd(a, u, h_before):
    n, d = a.shape
    groups = n // SUBLANES
    a = a.reshape(groups, SUBLANES, d)
    u = u.reshape(groups, SUBLANES, d)
    sub = lax.broadcasted_iota(jnp.int32, a.shape, 1)
    s = 1
    while s < SUBLANES:
        valid = sub >= s
        u = jnp.where(valid, u + a * pltpu.roll(u, s, 1), u)
        a = jnp.where(valid, a * pltpu.roll(a, s, 1), a)
        s *= 2
    out, prev = [], h_before
    for g in range(groups):
        out.append(u[g] + a[g] * prev)
        prev = out[-1][SUBLANES - 1:SUBLANES, :]
    return jnp.concatenate(out, axis=0)


def _scan_rev(cf, g, d_after):
    n, d = g.shape
    groups = n // SUBLANES
    cf = cf.reshape(groups, SUBLANES, d)
    g = g.reshape(groups, SUBLANES, d)
    sub = lax.broadcasted_iota(jnp.int32, g.shape, 1)
    s = 1
    while s < SUBLANES:
        valid = sub + s < SUBLANES
        g = jnp.where(valid, g + cf * pltpu.roll(g, SUBLANES - s, 1), g)
        cf = jnp.where(valid, cf * pltpu.roll(cf, SUBLANES - s, 1), cf)
        s *= 2
    out, nxt = [None] * groups, d_after
    for k in reversed(range(groups)):
        out[k] = g[k] + cf[k] * nxt
        nxt = out[k][0:1, :]
    return jnp.concatenate(out, axis=0)


def _lru_fwd(xg, conv_w, conv_b, wa, ba, wx, bx, lam, batch, n_blocks):
    rows = xg.shape[0]

    def body(xg_ref, cw_ref, cb_ref, wa_ref, ba_ref, wx_ref, bx_ref, lam_ref, hs_ref, lru_ref, x_prev, h_carry):
        @pl.when(pl.program_id(0) == 0)
        def _():
            x_prev[...] = jnp.zeros_like(x_prev)
            h_carry[...] = jnp.zeros_like(h_carry)

        for ex in range(batch):
            xb = xg_ref[ex, :, :LRU_W]
            taps = _conv_taps(xb, x_prev[ex])
            xc = cb_ref[...] + sum(cw_ref[CONV_TAPS - 1 - k:CONV_TAPS - k, :] * taps[k] for k in range(CONV_TAPS))
            _, _, i, _, a, mult = _lru_gates(xc, wa_ref[...], ba_ref[...], wx_ref[...], bx_ref[...], lam_ref[...])
            h = _scan_fwd(a, mult * i * xc, h_carry[ex])
            hs_ref[ex] = h
            lru_ref[ex] = h * _gelu(xg_ref[ex, :, LRU_W:])
            h_carry[ex] = h[TIME_BLOCK - 1:TIME_BLOCK, :]
            x_prev[ex] = xb[TIME_BLOCK - SUBLANES:TIME_BLOCK, :]

    def blk(width):
        return pl.BlockSpec((batch, TIME_BLOCK, width), lambda j: (0, j, 0))

    small = [conv_w, conv_b, wa, ba, wx, bx, lam]
    hs, lru = pl.pallas_call(
        body, name="lru_fwd", grid=(n_blocks,),
        in_specs=[blk(2 * LRU_W)] + [_whole_spec(a) for a in small],
        out_specs=[blk(LRU_W), blk(LRU_W)],
        out_shape=[_sds((batch, rows // batch, LRU_W), F32)] * 2,
        scratch_shapes=[pltpu.VMEM((batch, SUBLANES, LRU_W), F32), pltpu.VMEM((batch, 1, LRU_W), F32)],
        compiler_params=_params("arbitrary"),
    )(_by_example(xg, batch), *small)
    return hs.reshape(rows, LRU_W), lru.reshape(rows, LRU_W)


def _lru_bwd(xg, hs, d_lru, conv_w, conv_b, wa, ba, wx, bx, lam, batch, n_blocks):
    rows = xg.shape[0]
    last_row = TIME_BLOCK - 1

    def body(xg_ref, xgh_ref, hs_ref, hsh_ref, dl_ref, cw_ref, cb_ref, wa_ref, ba_ref, wx_ref, bx_ref, lam_ref,
             dxg_ref, dwa_ref, dwx_ref, vec_ref, dh_carry, dxc_next):
        j = pl.program_id(0)

        @pl.when(j == 0)
        def _():
            dwa_ref[...] = jnp.zeros_like(dwa_ref)
            dwx_ref[...] = jnp.zeros_like(dwx_ref)
            vec_ref[...] = jnp.zeros_like(vec_ref)
            dh_carry[...] = jnp.zeros_like(dh_carry)
            dxc_next[...] = jnp.zeros_like(dxc_next)

        for ex in range(batch):
            one_example(ex, j, xg_ref, xgh_ref, hs_ref, hsh_ref, dl_ref, cw_ref, cb_ref, wa_ref, ba_ref, wx_ref, bx_ref,
                        lam_ref, dxg_ref, dwa_ref, dwx_ref, vec_ref, dh_carry, dxc_next)

    def one_example(ex, j, xg_ref, xgh_ref, hs_ref, hsh_ref, dl_ref, cw_ref, cb_ref, wa_ref, ba_ref, wx_ref, bx_ref, lam_ref,
                    dxg_ref, dwa_ref, dwx_ref, vec_ref, dh_carry, dxc_next):
        first = j == n_blocks - 1
        xb = xg_ref[ex, :, :LRU_W]
        prev8 = jnp.where(first, 0.0, xgh_ref[ex, :, :LRU_W])
        h_before = jnp.where(first, 0.0, hsh_ref[ex, SUBLANES - 1:SUBLANES, :])
        cw = cw_ref[...]
        lam = lam_ref[...]
        wa = wa_ref[...]
        wx = wx_ref[...]
        taps = _conv_taps(xb, prev8)
        xc = cb_ref[...] + sum(cw[CONV_TAPS - 1 - k:CONV_TAPS - k, :] * taps[k] for k in range(CONV_TAPS))
        xcb, r, i, sp, a, mult = _lru_gates(xc, wa, ba_ref[...], wx, bx_ref[...], lam)
        hs = hs_ref[ex]
        row = lax.broadcasted_iota(jnp.int32, hs.shape, 0)
        h_prev = jnp.where(row == 0, h_before, pltpu.roll(hs, 1, 0))
        dl = dl_ref[ex]
        gate, dgate = _gelu_and_grad(xg_ref[ex, :, LRU_W:])
        dxg_ref[ex, :, LRU_W:] = (dl * hs * dgate).astype(BF16)
        cf = jnp.where(row == last_row, 1.0, pltpu.roll(a, last_row, 0))
        dh = _scan_rev(cf, dl * gate, dh_carry[ex])
        dh_carry[ex] = a[0:1, :] * dh[0:1, :]
        dmult = dh * i * xc
        di = dh * mult * xc
        dxc = dh * mult * i
        dlog_a = dh * h_prev * a - dmult * (a * a / mult)
        dr = dlog_a * (-LRU_C * sp)
        dlam = jnp.sum(dlog_a * (-LRU_C * r), axis=0, keepdims=True) * (-_sigmoid(-lam))
        dpr = dr * r * (1.0 - r)
        dpi = di * i * (1.0 - i)
        dprb = dpr.astype(BF16)
        dpib = dpi.astype(BF16)
        dxc = dxc + _dot_nt(dprb, wa) + _dot_nt(dpib, wx)
        dwa_ref[...] += _dot_tn(xcb, dprb)
        dwx_ref[...] += _dot_tn(xcb, dpib)
        ext = jnp.concatenate([dxc, dxc_next[ex]], axis=0)
        n = ext.shape[0]
        dxb = cw[CONV_TAPS - 1:CONV_TAPS, :] * dxc
        for k in range(1, CONV_TAPS):
            dxb = dxb + cw[CONV_TAPS - 1 - k:CONV_TAPS - k, :] * pltpu.roll(ext, n - k, 0)[:TIME_BLOCK]
        dxg_ref[ex, :, :LRU_W] = dxb.astype(BF16)
        dxc_next[ex] = dxc[:SUBLANES, :]
        vecs = [jnp.sum(dxc * taps[CONV_TAPS - 1 - t], axis=0, keepdims=True) for t in range(CONV_TAPS)]
        vecs += [jnp.sum(dxc, axis=0, keepdims=True), jnp.sum(dpr, axis=0, keepdims=True),
                 jnp.sum(dpi, axis=0, keepdims=True), dlam]
        vec_ref[...] += _put_rows(SUBLANES, vecs)

    def tblk(j):
        return n_blocks - 1 - j

    def blk(width):
        return pl.BlockSpec((batch, TIME_BLOCK, width), lambda j: (0, tblk(j), 0))

    per8 = TIME_BLOCK // SUBLANES

    def halo(width):
        return pl.BlockSpec((batch, SUBLANES, width), lambda j: (0, jnp.maximum(per8 * tblk(j) - 1, 0), 0))

    small = [conv_w, conv_b, wa, ba, wx, bx, lam]
    acc = lambda shape: pl.BlockSpec(shape, lambda j: (0, 0))
    xg3, hs3 = _by_example(xg, batch), _by_example(hs, batch)
    dxg, dwa, dwx, vec = pl.pallas_call(
        body, name="lru_bwd", grid=(n_blocks,),
        in_specs=[blk(2 * LRU_W), halo(2 * LRU_W), blk(LRU_W), halo(LRU_W), blk(LRU_W)] + [_whole_spec(a) for a in small],
        out_specs=[blk(2 * LRU_W), acc((LRU_W, LRU_W)), acc((LRU_W, LRU_W)), acc((SUBLANES, LRU_W))],
        out_shape=[_sds((batch, rows // batch, 2 * LRU_W), BF16), _sds((LRU_W, LRU_W), F32), _sds((LRU_W, LRU_W), F32),
                   _sds((SUBLANES, LRU_W), F32)],
        scratch_shapes=[pltpu.VMEM((batch, 1, LRU_W), F32), pltpu.VMEM((batch, SUBLANES, LRU_W), F32)],
        compiler_params=_params("arbitrary"),
    )(xg3, xg3, hs3, hs3, _by_example(d_lru, batch), *small)
    return dxg.reshape(rows, 2 * LRU_W), dwa, dwx, vec


def _out_proj_fwd(attn, lru, g_attn, g_lru, w_out, h, g_post):
    rows = h.shape[0]

    def body(at_ref, lr_ref, ga_ref, gl_ref, w_ref, h_ref, gp_ref, grp_ref, o_ref, hm_ref):
        a = _rms(at_ref[...], ga_ref[...]).astype(BF16)
        l = _rms(lr_ref[...], gl_ref[...]).astype(BF16)
        grp_ref[:, :ATTN_W] = a
        grp_ref[:, ATTN_W:] = l
        o = _dot(a, w_ref[:ATTN_W, :]) + _dot(l, w_ref[ATTN_W:, :])
        o_ref[...] = o
        hm_ref[...] = h_ref[...] + _rms(o, gp_ref[...])

    tile = _proj_tile(rows)
    rs = functools.partial(_row_spec, tile=tile)
    return pl.pallas_call(
        body, name="out_proj_fwd", grid=(rows // tile,),
        in_specs=[rs(ATTN_W), rs(LRU_W), _whole_spec(g_attn), _whole_spec(g_lru), _whole_spec(w_out),
                  rs(D_MODEL), _whole_spec(g_post)],
        out_specs=[rs(D_MODEL), rs(D_MODEL), rs(D_MODEL)],
        out_shape=[_sds((rows, D_MODEL), BF16), _sds((rows, D_MODEL), F32), _sds((rows, D_MODEL), F32)],
        compiler_params=_params("parallel"),
    )(attn, lru, g_attn, g_lru, w_out, h, g_post)


def _out_proj_bwd(dh_mid, o, attn, lru, g_attn, g_lru, g_post, w_out):
    rows = o.shape[0]

    def body(dhm_ref, o_ref, at_ref, lr_ref, ga_ref, gl_ref, gp_ref, w_ref, do_ref, dat_ref, dlr_ref, gacc_ref):
        @pl.when(pl.program_id(0) == 0)
        def _():
            gacc_ref[...] = jnp.zeros_like(gacc_ref)

        do, dgp = _rms_bwd(dhm_ref[...], o_ref[...], gp_ref[...])
        dob = do.astype(BF16)
        do_ref[...] = dob
        dat, dga = _rms_bwd(_dot_nt(dob, w_ref[:ATTN_W, :]), at_ref[...], ga_ref[...])
        dlr, dgl = _rms_bwd(_dot_nt(dob, w_ref[ATTN_W:, :]), lr_ref[...], gl_ref[...])
        dat_ref[...] = dat
        dlr_ref[...] = dlr
        gacc_ref[...] += _put_rows(SUBLANES, [dgp, jnp.concatenate([dga, dgl], axis=1)])

    tile = _proj_tile(rows)
    rs = functools.partial(_row_spec, tile=tile)
    return pl.pallas_call(
        body, name="out_proj_bwd", grid=(rows // tile,),
        in_specs=[rs(D_MODEL), rs(D_MODEL), rs(ATTN_W), rs(LRU_W), _whole_spec(g_attn),
                  _whole_spec(g_lru), _whole_spec(g_post), _whole_spec(w_out)],
        out_specs=[rs(D_MODEL), rs(ATTN_W), rs(LRU_W),
                   pl.BlockSpec((SUBLANES, D_MODEL), lambda i: (0, 0))],
        out_shape=[_sds((rows, D_MODEL), BF16), _sds((rows, ATTN_W), F32), _sds((rows, LRU_W), F32),
                   _sds((SUBLANES, D_MODEL), F32)],
        compiler_params=_params("arbitrary"),
    )(dh_mid, o, attn, lru, g_attn, g_lru, g_post, w_out)


def _mlp_fwd(h_mid, g_pre, w_up4, w_down, g_post):
    rows = h_mid.shape[0]

    def body(h_ref, g1_ref, wu_ref, wd_ref, g2_ref, z_ref, slope_ref, act_ref, y_ref, ho_ref):
        x = h_ref[...]
        z = _rms(x, g1_ref[...]).astype(BF16)
        z_ref[...] = z
        y = jnp.zeros((ROW_TILE, D_MODEL), F32)
        for s in range(N_CHIPS):
            cols = slice(s * D_MODEL, (s + 1) * D_MODEL)
            r = jnp.maximum(_dot(z, wu_ref[s]), 0.0)
            slope_ref[:, cols] = (2.0 * r).astype(BF16)
            a = jnp.square(r).astype(BF16)
            act_ref[:, cols] = a
            y = y + _dot(a, wd_ref[cols, :])
        y_ref[...] = y
        ho_ref[...] = x + _rms(y, g2_ref[...])

    return pl.pallas_call(
        body, name="mlp_fwd", grid=(rows // ROW_TILE,),
        in_specs=[_row_spec(D_MODEL), _whole_spec(g_pre), _whole_spec(w_up4), _whole_spec(w_down), _whole_spec(g_post)],
        out_specs=[_row_spec(D_MODEL), _row_spec(D_FF), _row_spec(D_FF), _row_spec(D_MODEL), _row_spec(D_MODEL)],
        out_shape=[_sds((rows, D_MODEL), BF16), _sds((rows, D_FF), BF16), _sds((rows, D_FF), BF16),
                   _sds((rows, D_MODEL), F32), _sds((rows, D_MODEL), F32)],
        compiler_params=_params("parallel"),
    )(h_mid, g_pre, w_up4, w_down, g_post)


def _mlp_bwd(dh_out, h_mid, y, slope, w_up4, w_down, g_pre, g_post):
    rows = y.shape[0]

    def body(dh_ref, hm_ref, y_ref, slope_ref, wu_ref, wd_ref, g1_ref, g2_ref, dhm_ref, dy_ref, dup_ref, gacc_ref):
        @pl.when(pl.program_id(0) == 0)
        def _():
            gacc_ref[...] = jnp.zeros_like(gacc_ref)

        dh = dh_ref[...]
        dy, dg2 = _rms_bwd(dh, y_ref[...], g2_ref[...])
        dyb = dy.astype(BF16)
        dy_ref[...] = dyb
        dz = jnp.zeros((ROW_TILE, D_MODEL), F32)
        for s in range(N_CHIPS):
            cols = slice(s * D_MODEL, (s + 1) * D_MODEL)
            dact = _dot_nt(dyb, wd_ref[cols, :])
            dup = (dact * slope_ref[:, cols].astype(F32)).astype(BF16)
            dup_ref[:, cols] = dup
            dz = dz + _dot_nt(dup, wu_ref[s])
        dx, dg1 = _rms_bwd(dz, hm_ref[...], g1_ref[...])
        dhm_ref[...] = dh + dx
        gacc_ref[...] += _put_rows(SUBLANES, [dg1, dg2])

    return pl.pallas_call(
        body, name="mlp_bwd", grid=(rows // ROW_TILE,),
        in_specs=[_row_spec(D_MODEL), _row_spec(D_MODEL), _row_spec(D_MODEL), _row_spec(D_FF), _whole_spec(w_up4),
                  _whole_spec(w_down), _whole_spec(g_pre), _whole_spec(g_post)],
        out_specs=[_row_spec(D_MODEL), _row_spec(D_MODEL), _row_spec(D_FF),
                   pl.BlockSpec((SUBLANES, D_MODEL), lambda i: (0, 0))],
        out_shape=[_sds((rows, D_MODEL), F32), _sds((rows, D_MODEL), BF16), _sds((rows, D_FF), BF16),
                   _sds((SUBLANES, D_MODEL), F32)],
        compiler_params=_params("arbitrary"),
    )(dh_out, h_mid, y, slope, w_up4, w_down, g_pre, g_post)


def _matmul_tn(a, b, tm, tn, name, out_dtype, column_blocks=False):
    rows, m = a.shape
    n = b.shape[1]

    def body(a_ref, b_ref, o_ref):
        o_ref[...] = _dot_tn(a_ref[...], b_ref[...]).astype(out_dtype)

    if column_blocks:
        out_spec = pl.BlockSpec((None, tm, tn), lambda i, j: (j, i, 0))
        out_shape = _sds((n // tn, m, tn), out_dtype)
    else:
        out_spec = pl.BlockSpec((tm, tn), lambda i, j: (i, j))
        out_shape = _sds((m, n), out_dtype)
    return pl.pallas_call(
        body, name=name, grid=(m // tm, n // tn),
        in_specs=[pl.BlockSpec((rows, tm), lambda i, j: (0, i)), pl.BlockSpec((rows, tn), lambda i, j: (0, j))],
        out_specs=out_spec, out_shape=out_shape,
        compiler_params=_params("parallel", "parallel"),
    )(a, b)


LOSS_COLS = 256


def _loss_head(h, loss_target):
    batch, per_example, _ = h.shape
    seq = loss_target.shape[1]
    n_real = N_META + seq
    assert seq % SUBLANES == 0

    def body(h_ref, t_ref, dh_ref, l_ref):
        @pl.when((pl.program_id(0) == 0) & (pl.program_id(1) == 0))
        def _():
            l_ref[...] = jnp.zeros_like(l_ref)

        d = h_ref[N_META:n_real, :] - t_ref[...]
        dh_ref[:N_META, :] = jnp.zeros((N_META, LOSS_COLS), F32)
        dh_ref[N_META:n_real, :] = d * (1.0 / D_MODEL)
        dh_ref[n_real:, :] = jnp.zeros((per_example - n_real, LOSS_COLS), F32)
        l_ref[...] += jnp.sum(jnp.sum(d * d, axis=0, keepdims=True), axis=1, keepdims=True)

    blk = pl.BlockSpec((None, per_example, LOSS_COLS), lambda b, j: (b, 0, j))
    return pl.pallas_call(
        body, name="loss_head", grid=(batch, D_MODEL // LOSS_COLS),
        in_specs=[blk, pl.BlockSpec((None, seq, LOSS_COLS), lambda b, j: (b, 0, j))],
        out_specs=[blk, pl.BlockSpec((SUBLANES, LANES), lambda b, j: (0, 0))],
        out_shape=[_sds(h.shape, F32), _sds((SUBLANES, LANES), F32)],
        compiler_params=_params("arbitrary", "arbitrary"),
    )(h, loss_target)


def _meta_grad(dh0, batch, rows_per_example):
    per = rows_per_example // N_META

    def body(d_ref, o_ref):
        @pl.when(pl.program_id(0) == 0)
        def _():
            o_ref[...] = jnp.zeros_like(o_ref)

        o_ref[...] += d_ref[...]

    return pl.pallas_call(
        body, name="meta_grad", grid=(batch,),
        in_specs=[pl.BlockSpec((N_META, D_MODEL), lambda b: (b * per, 0))],
        out_specs=pl.BlockSpec((N_META, D_MODEL), lambda b: (0, 0)),
        out_shape=_sds((N_META, D_MODEL), F32),
        compiler_params=_params("arbitrary"),
    )(dh0)


def _elementwise_tile(rows, cols):
    tile = rows
    while tile * cols * 4 > (1 << 20) and tile % 16 == 0:
        tile //= 2
    return tile


def _sum_slots(buf, name):
    k, rows, cols = buf.shape
    tile = _elementwise_tile(rows, cols)

    def body(*refs):
        total = refs[0][...].astype(F32)
        for r in refs[1:k]:
            total = total + r[...].astype(F32)
        refs[k][...] = total

    def slot(s):
        return pl.BlockSpec((None, tile, cols), lambda i: (s, i, 0))

    return pl.pallas_call(
        body, name=name, grid=(rows // tile,),
        in_specs=[slot(s) for s in range(k)], out_specs=pl.BlockSpec((tile, cols), lambda i: (i, 0)),
        out_shape=_sds((rows, cols), F32), compiler_params=_params("parallel"),
    )(*([buf] * k))


def _sum_pieces(pieces, landed, chip, layer, n_layers, stacked):
    _, rows, cols = pieces.shape
    tile = _elementwise_tile(rows, cols)

    def body(chip_ref, own_ref, a_ref, b_ref, c_ref, *rest):
        o_ref, token = rest[-2:]
        total = ((own_ref[...].astype(F32) + a_ref[...].astype(F32)) + b_ref[...].astype(F32)) + c_ref[...].astype(F32)
        o_ref[...] = total.astype(BF16)
        token[...] = jnp.zeros_like(token)

    def slot(offset):
        return pl.BlockSpec((None, tile, cols), lambda i, chip_ref: ((chip_ref[0] + offset) % N_CHIPS, i, 0))

    carried = [] if stacked is None else [stacked]
    grid_spec = pltpu.PrefetchScalarGridSpec(
        num_scalar_prefetch=1, grid=(rows // tile,),
        in_specs=[slot(0), slot(1), slot(2), slot(3)] + [pl.BlockSpec(memory_space=pl.ANY)] * len(carried),
        out_specs=[pl.BlockSpec((None, tile, cols), lambda i, chip_ref: (layer, i, 0)),
                   pl.BlockSpec((SUBLANES, LANES), lambda i, chip_ref: (0, 0))])
    return pl.pallas_call(
        body, name="sum_grad_pieces", grid_spec=grid_spec,
        out_shape=[_sds((n_layers, rows, cols), BF16), _sds((SUBLANES, LANES), F32)],
        input_output_aliases={5: 0} if carried else {},
        compiler_params=_params("arbitrary"),
    )(chip, pieces, landed, landed, landed, *carried)


def _adamw(w, m, v, grads, name):
    flat = w.ndim == 2
    if flat:
        w, m, v = w[None], m[None], v[None]
        grads = [g[None] for g in grads]
    layers, rows, cols = w.shape
    tile = _elementwise_tile(rows, cols)
    ng = len(grads)
    m_scale = 1.0 - ADAM_B1 ** ADAM_STEP
    v_scale = 1.0 - ADAM_B2 ** ADAM_STEP

    def body(*refs):
        w_ref, m_ref, v_ref = refs[:3]
        g_refs = refs[3:3 + ng]
        g_out, d_out, m_out, v_out = refs[-4:]
        g = g_refs[0][...].astype(F32)
        for r in g_refs[1:]:
            g = g + r[...].astype(F32)
        m_new = ADAM_B1 * m_ref[...] + (1.0 - ADAM_B1) * g
        v_new = ADAM_B2 * v_ref[...] + (1.0 - ADAM_B2) * (g * g)
        m_hat = m_new / m_scale
        v_hat = v_new / v_scale
        g_out[...] = g
        d_out[...] = -ADAM_LR * (m_hat / (jnp.sqrt(v_hat) + ADAM_EPS) + ADAM_WD * w_ref[...])
        m_out[...] = m_new
        v_out[...] = v_new

    spec = pl.BlockSpec((None, tile, cols), lambda l, i: (l, i, 0))
    out = pl.pallas_call(
        body, name=name, grid=(layers, rows // tile),
        in_specs=[spec] * (3 + ng), out_specs=[spec] * 4, out_shape=[_sds(w.shape, F32)] * 4,
        compiler_params=_params("parallel", "parallel"),
    )(w, m, v, *grads)
    return [o[0] for o in out] if flat else out


def _position():
    return lax.axis_index("x"), lax.axis_index("y"), lax.axis_index("c")


def _other_chips(x, y):
    return [(1 - x, y), (x, 1 - y), (1 - x, 1 - y)]


def _exchange_chips(arrays, name, scatter):
    n = len(arrays)

    def body(*refs):
        src, dst = refs[:n], refs[n:2 * n]
        send_sems, recv_sems, local_sems = refs[2 * n:]
        x, y, c = _position()
        mine = 2 * x + y
        copies = []
        for i in range(n):
            own = src[i].at[mine] if scatter else src[i]
            copies.append(pltpu.make_async_copy(own, dst[i].at[mine], local_sems.at[i]))
        for k, (px, py) in enumerate(_other_chips(x, y)):
            for i in range(n):
                piece = src[i].at[2 * px + py] if scatter else src[i]
                copies.append(pltpu.make_async_remote_copy(
                    src_ref=piece, dst_ref=dst[i].at[mine], send_sem=send_sems.at[k, i], recv_sem=recv_sems.at[k, i],
                    device_id=(px, py, c), device_id_type=MESH))
        for cp in copies:
            cp.start()
        for cp in copies:
            cp.wait()

    def out_shape(a):
        return _sds(a.shape if scatter else (N_CHIPS,) + a.shape, a.dtype)

    return pl.pallas_call(
        body, name=name, in_specs=[HBM_SPEC] * n, out_specs=[HBM_SPEC] * n, out_shape=[out_shape(a) for a in arrays],
        scratch_shapes=[pltpu.SemaphoreType.DMA((N_CHIPS - 1, n)), pltpu.SemaphoreType.DMA((N_CHIPS - 1, n)),
                        pltpu.SemaphoreType.DMA((n,))],
    )(*arrays)


def _swap_sibling(arrays, name):
    n = len(arrays)

    def body(*refs):
        src, dst = refs[:n], refs[n:2 * n]
        send_sems, recv_sems = refs[2 * n:]
        x, y, c = _position()
        copies = [pltpu.make_async_remote_copy(
            src_ref=src[i], dst_ref=dst[i], send_sem=send_sems.at[i], recv_sem=recv_sems.at[i],
            device_id=(x, y, 1 - c), device_id_type=MESH) for i in range(n)]
        for cp in copies:
            cp.start()
        for cp in copies:
            cp.wait()

    return pl.pallas_call(
        body, name=name, in_specs=[HBM_SPEC] * n, out_specs=[HBM_SPEC] * n,
        out_shape=[_sds(a.shape, a.dtype) for a in arrays],
        scratch_shapes=[pltpu.SemaphoreType.DMA((n,)), pltpu.SemaphoreType.DMA((n,))],
    )(*arrays)


SEM_SPEC = pl.BlockSpec(memory_space=pltpu.SEMAPHORE)
ANY_SPEC = pl.BlockSpec(memory_space=pl.ANY)
IN_FLIGHT = pltpu.SideEffectType.DATAFLOW_SIDE_EFFECTING


def _peer_sems(all_devices):
    return pltpu.SemaphoreType.DMA(((N_DEV if all_devices else N_CHIPS) - 1,))


def _own_slot(all_devices, x, y, c):
    return 4 * x + 2 * y + c if all_devices else 2 * x + y


def _peers(all_devices, x, y, c):
    if not all_devices:
        return [((px, py, c), 2 * px + py) for px, py in _other_chips(x, y)]
    out = []
    for fx in range(2):
        for fy in range(2):
            for fc in range(2):
                if fx or fy or fc:
                    px, py, pc = (1 - x if fx else x), (1 - y if fy else y), (1 - c if fc else c)
                    out.append(((px, py, pc), 4 * px + 2 * py + pc))
    return out


def _in_hbm(a):
    return pltpu.with_memory_space_constraint(a, pltpu.HBM)


def _place_slot(a, slot, n_slots, name):
    rows, cols = a.shape
    tile = _elementwise_tile(rows, cols)

    def body(slot_ref, a_ref, o_ref):
        o_ref[...] = a_ref[...]

    grid_spec = pltpu.PrefetchScalarGridSpec(
        num_scalar_prefetch=1, grid=(rows // tile,),
        in_specs=[pl.BlockSpec((tile, cols), lambda i, slot_ref: (i, 0))],
        out_specs=pl.BlockSpec((None, tile, cols), lambda i, slot_ref: (slot_ref[0], i, 0)))
    return pl.pallas_call(
        body, name=name, grid_spec=grid_spec, out_shape=_sds((n_slots, rows, cols), a.dtype),
        compiler_params=_params("parallel"),
    )(slot, a)


def _place_shard(w, layer, chip, name):
    _, rows, cols = w.shape
    tile = _elementwise_tile(rows, cols)

    def body(chip_ref, w_ref, o_ref):
        o_ref[...] = w_ref[...].astype(BF16)

    grid_spec = pltpu.PrefetchScalarGridSpec(
        num_scalar_prefetch=1, grid=(rows // tile,),
        in_specs=[pl.BlockSpec((None, tile, cols), lambda i, chip_ref: (layer, i, 0))],
        out_specs=pl.BlockSpec((None, tile, cols), lambda i, chip_ref: (chip_ref[0], i, 0)))
    return pl.pallas_call(
        body, name=name, grid_spec=grid_spec, out_shape=_sds((N_CHIPS, rows, cols), BF16),
        compiler_params=_params("parallel"),
    )(chip, w)


def _gather_start(bufs, after, name, all_devices=False):
    n = len(bufs)

    def body(*refs):
        buf = refs[:n]
        send, recv = refs[n + 1:2 * n + 1], refs[2 * n + 1:3 * n + 1]
        token = refs[4 * n + 1]
        x, y, c = _position()
        mine = _own_slot(all_devices, x, y, c)
        for i in range(n):
            for k, (peer, _) in enumerate(_peers(all_devices, x, y, c)):
                pltpu.make_async_remote_copy(
                    src_ref=buf[i].at[mine], dst_ref=buf[i].at[mine], send_sem=send[i].at[k], recv_sem=recv[i].at[k],
                    device_id=peer, device_id_type=MESH).start()
        token[...] = jnp.zeros_like(token)

    sems = _peer_sems(all_devices)
    out = pl.pallas_call(
        body, name=name, in_specs=[HBM_SPEC] * n + [ANY_SPEC],
        out_specs=[SEM_SPEC] * (2 * n) + [HBM_SPEC] * n + [pl.BlockSpec(memory_space=pltpu.VMEM)],
        out_shape=[sems] * (2 * n) + [pltpu.HBM(b.shape, b.dtype) for b in bufs] + [_sds((SUBLANES, LANES), F32)],
        input_output_aliases={i: 2 * n + i for i in range(n)},
        compiler_params=pltpu.CompilerParams(has_side_effects=IN_FLIGHT),
    )(*[_in_hbm(b) for b in bufs], after)
    return [(out[i], out[n + i], out[2 * n + i]) for i in range(n)], out[3 * n]


def _gather_wait(send, recv, buf, after, name, all_devices=False):
    def body(buf_ref, send_ref, recv_ref, after_ref, out_ref):
        x, y, c = _position()
        mine = _own_slot(all_devices, x, y, c)
        for k, (peer, slot) in enumerate(_peers(all_devices, x, y, c)):
            cp = pltpu.make_async_remote_copy(
                src_ref=buf_ref.at[mine], dst_ref=buf_ref.at[slot], send_sem=send_ref.at[k], recv_sem=recv_ref.at[k],
                device_id=peer, device_id_type=MESH)
            cp.wait_send()
            cp.wait_recv()

    return pl.pallas_call(
        body, name=name, in_specs=[HBM_SPEC, SEM_SPEC, SEM_SPEC, ANY_SPEC], out_specs=HBM_SPEC,
        out_shape=pltpu.HBM(buf.shape, buf.dtype), input_output_aliases={0: 0},
        compiler_params=pltpu.CompilerParams(has_side_effects=IN_FLIGHT),
    )(buf, send, recv, after)


def _scatter_start(pieces, name):
    n = len(pieces)

    def body(*refs):
        src = refs[:n]
        send, recv = refs[n:2 * n], refs[2 * n:3 * n]
        land = refs[4 * n:5 * n]
        token = refs[5 * n]
        x, y, c = _position()
        mine = 2 * x + y
        for i in range(n):
            for k, (px, py) in enumerate(_other_chips(x, y)):
                pltpu.make_async_remote_copy(
                    src_ref=src[i].at[2 * px + py], dst_ref=land[i].at[mine], send_sem=send[i].at[k], recv_sem=recv[i].at[k],
                    device_id=(px, py, c), device_id_type=MESH).start()
        token[...] = jnp.zeros_like(token)

    hbm = [pltpu.HBM(p.shape, p.dtype) for p in pieces]
    out = pl.pallas_call(
        body, name=name, in_specs=[HBM_SPEC] * n,
        out_specs=[SEM_SPEC] * (2 * n) + [HBM_SPEC] * (2 * n) + [pl.BlockSpec(memory_space=pltpu.VMEM)],
        out_shape=[_peer_sems(False)] * (2 * n) + hbm + hbm + [_sds((SUBLANES, LANES), F32)],
        input_output_aliases={i: 2 * n + i for i in range(n)},
        compiler_params=pltpu.CompilerParams(has_side_effects=IN_FLIGHT),
    )(*[_in_hbm(p) for p in pieces])
    return [(out[i], out[n + i], out[2 * n + i], out[3 * n + i]) for i in range(n)], out[4 * n]


def _scatter_wait(send, recv, pieces, land, after, name):
    def body(src_ref, land_ref, send_ref, recv_ref, after_ref, src_out, land_out):
        x, y, c = _position()
        for k, (px, py) in enumerate(_other_chips(x, y)):
            cp = pltpu.make_async_remote_copy(
                src_ref=src_ref.at[2 * px + py], dst_ref=land_ref.at[2 * px + py], send_sem=send_ref.at[k],
                recv_sem=recv_ref.at[k], device_id=(px, py, c), device_id_type=MESH)
            cp.wait_send()
            cp.wait_recv()

    return pl.pallas_call(
        body, name=name, in_specs=[HBM_SPEC, HBM_SPEC, SEM_SPEC, SEM_SPEC, ANY_SPEC], out_specs=[HBM_SPEC, HBM_SPEC],
        out_shape=[pltpu.HBM(pieces.shape, pieces.dtype), pltpu.HBM(land.shape, land.dtype)],
        input_output_aliases={0: 0, 1: 1},
        compiler_params=pltpu.CompilerParams(has_side_effects=IN_FLIGHT),
    )(pieces, land, send, recv, after)


def _rope_tables(batch, rows_per_example):
    inv_freq = ROPE_THETA ** (-jnp.arange(0, 64, 2, dtype=F32) / 64)
    ang = jnp.arange(rows_per_example, dtype=F32)[:, None] * inv_freq[None, :]
    cos, sin = jnp.cos(ang), jnp.sin(ang)
    cos128 = jnp.concatenate([cos, cos, cos, cos], axis=1)
    sin128 = jnp.concatenate([-sin, sin, -sin, sin], axis=1)
    return jnp.tile(cos128, (batch, 1)), jnp.tile(sin128, (batch, 1))


def _block_diagonal(w):
    eye = jnp.eye(LRU_BLOCKS, dtype=w.dtype)
    return (w[:, :, None, :] * eye[:, None, :, None]).reshape(LRU_W, LRU_W)


def _diagonal_blocks(dense):
    d4 = dense.reshape(LRU_BLOCKS, LRU_BLOCK, LRU_BLOCKS, LRU_BLOCK)
    return jnp.stack([d4[n, :, n, :] for n in range(LRU_BLOCKS)])


def _row(v):
    return v.reshape(1, -1)


def _local_step(x, loss_target, meta_tokens, small, depth, big_weight, on_layer_grads):
    batch, seq, _ = x.shape
    n_real = N_META + seq
    n_blocks = -(-n_real // TIME_BLOCK)
    per_example = n_blocks * TIME_BLOCK
    pad = per_example - n_real
    meta = jnp.broadcast_to(meta_tokens[None], (batch, N_META, D_MODEL))
    h = jnp.concatenate([meta, x, jnp.zeros((batch, pad, D_MODEL), F32)], axis=1).reshape(batch * per_example, D_MODEL)
    rope_cos, rope_sin = _rope_tables(batch, per_example)

    saved = []
    for l in range(depth):
        wa = _block_diagonal(small['gate_a_w'][l]).astype(BF16)
        wx = _block_diagonal(small['gate_x_w'][l]).astype(BF16)
        lru_small = (small['conv_w'][l], _row(small['conv_b'][l]), wa, _row(small['gate_a_b'][l]), wx,
                     _row(small['gate_x_b'][l]), _row(small['lru_lambda'][l]))
        w_in = big_weight('w_in', l, h)
        z1, q, kv, xg = _in_proj_fwd(h, _row(small['pre_mix_norm'][l]), w_in, rope_cos, rope_sin)
        attn, lse = _attn_fwd(q, kv, small['attn_sinks'][l], batch, n_blocks)
        hs, lru = _lru_fwd(xg, *lru_small, batch, n_blocks)
        w_out = big_weight('w_out', l, xg)
        groups, o, h_mid = _out_proj_fwd(attn, lru, _row(small['attn_group_norm'][l]), _row(small['lru_group_norm'][l]),
                                         w_out, h, _row(small['post_mix_norm'][l]))
        w_up4 = big_weight('w_up', l, hs)
        w_down = big_weight('w_down', l, hs)
        z2, slope, act, y, h_out = _mlp_fwd(h_mid, _row(small['pre_mlp_norm'][l]), w_up4, w_down,
                                         _row(small['post_mlp_norm'][l]))
        saved.append(dict(h=h, z1=z1, q=q, kv=kv, xg=xg, attn=attn, lse=lse, hs=hs, lru=lru, groups=groups, o=o,
                          h_mid=h_mid, z2=z2, slope=slope, act=act, y=y, lru_small=lru_small, w_in=w_in, w_out=w_out,
                          w_up4=w_up4, w_down=w_down))
        h = h_out

    dh, sq_err = _loss_head(h.reshape(batch, per_example, D_MODEL), loss_target)
    dh = dh.reshape(batch * per_example, D_MODEL)

    gs = {n: [None] * depth for n in REPLICATED_NAMES + ('conv_w',)}
    handed_over = None
    for l in reversed(range(depth)):
        s = saved[l]

        def ordered_after(gain):
            return gain if handed_over is None else gain + handed_over[0, 0]

        dh_mid, dy, dup, g_mlp = _mlp_bwd(dh, s['h_mid'], s['y'], s['slope'], s['w_up4'], s['w_down'],
                                          _row(small['pre_mlp_norm'][l]), ordered_after(_row(small['post_mlp_norm'][l])))
        handed_over = on_layer_grads(l, {
            'w_down': _matmul_tn(s['act'], dy, 512, D_MODEL, "grad_w_down", BF16),
            'w_up': _matmul_tn(s['z2'], dup, 512, D_MODEL, "grad_w_up", BF16, column_blocks=True)})
        do, d_attn, d_lru, g_out = _out_proj_bwd(dh_mid, s['o'], s['attn'], s['lru'], _row(small['attn_group_norm'][l]),
                                                 _row(small['lru_group_norm'][l]),
                                                 ordered_after(_row(small['post_mix_norm'][l])), s['w_out'])
        dxg, dwa, dwx, g_lru = _lru_bwd(s['xg'], s['hs'], d_lru, *s['lru_small'], batch, n_blocks)
        dq, dkv, dsink = _attn_bwd(s['q'], s['kv'], small['attn_sinks'][l], s['attn'], s['lse'], d_attn, batch, n_blocks)
        dh, dproj, g_in = _in_proj_bwd(dh_mid, s['h'], dq, dkv, dxg, rope_cos, rope_sin, _row(small['pre_mix_norm'][l]),
                                       s['w_in'])
        handed_over = on_layer_grads(l, {
            'w_out': _matmul_tn(s['groups'], do, 512, D_MODEL, "grad_w_out", BF16),
            'w_in': _matmul_tn(dproj, s['z1'], IN_COLS // 2, D_MODEL, "grad_w_in", BF16)})
        gs['pre_mlp_norm'][l], gs['post_mlp_norm'][l] = g_mlp[0], g_mlp[1]
        gs['post_mix_norm'][l] = g_out[0]
        gs['attn_group_norm'][l], gs['lru_group_norm'][l] = g_out[1, :ATTN_W], g_out[1, ATTN_W:]
        gs['pre_mix_norm'][l] = g_in[0]
        gs['attn_sinks'][l] = dsink[:, 0]
        gs['conv_w'][l] = g_lru[:CONV_TAPS]
        gs['conv_b'][l], gs['gate_a_b'][l], gs['gate_x_b'][l], gs['lru_lambda'][l] = g_lru[4], g_lru[5], g_lru[6], g_lru[7]
        gs['gate_a_w'][l] = _diagonal_blocks(dwa)
        gs['gate_x_w'][l] = _diagonal_blocks(dwx)

    grad_x = dh.reshape(batch, per_example, D_MODEL)[:, N_META:n_real]
    grad_meta = _meta_grad(dh, batch, per_example)
    small_grads = {n: jnp.stack(v) for n, v in gs.items()}
    return sq_err, grad_x, grad_meta, small_grads, handed_over


PACK_UNIT = SUBLANES * LANES


def _size(shape):
    size = 1
    for d in shape:
        size *= d
    return size


def _pack(arrays):
    parts = []
    for a in arrays:
        flat = a.reshape(-1)
        padded = -(-flat.shape[0] // PACK_UNIT) * PACK_UNIT
        if padded != flat.shape[0]:
            flat = jnp.pad(flat, (0, padded - flat.shape[0]))
        parts.append(flat.reshape(-1, LANES))
    return jnp.concatenate(parts, axis=0)


def _unpack(buf, shapes):
    out, at = [], 0
    for shp in shapes:
        size = _size(shp)
        rows = -(-size // PACK_UNIT) * SUBLANES
        part = buf[at:at + rows]
        if rows * LANES != size:
            part = part.reshape(-1)[:size]
        out.append(part.reshape(shp))
        at += rows
    return out


def kernel(x, meta_tokens, pre_mix_norm, w_in, attn_sinks, conv_w, conv_b, gate_a_w, gate_a_b, gate_x_w, gate_x_b, lru_lambda, attn_group_norm, lru_group_norm, w_out, post_mix_norm, pre_mlp_norm, w_up, w_down, post_mlp_norm, loss_target, m_meta_tokens, m_pre_mix_norm, m_w_in, m_attn_sinks, m_conv_w, m_conv_b, m_gate_a_w, m_gate_a_b, m_gate_x_w, m_gate_x_b, m_lru_lambda, m_attn_group_norm, m_lru_group_norm, m_w_out, m_post_mix_norm, m_pre_mlp_norm, m_w_up, m_w_down, m_post_mlp_norm, v_meta_tokens, v_pre_mix_norm, v_w_in, v_attn_sinks, v_conv_w, v_conv_b, v_gate_a_w, v_gate_a_b, v_gate_x_w, v_gate_x_b, v_lru_lambda, v_attn_group_norm, v_lru_group_norm, v_w_out, v_post_mix_norm, v_pre_mlp_norm, v_w_up, v_w_down, v_post_mlp_norm):
    given = dict(locals())
    w = {n: given[n] for n in WEIGHT_NAMES}
    m = {n: given['m_' + n] for n in WEIGHT_NAMES}
    v = {n: given['v_' + n] for n in WEIGHT_NAMES}
    depth = w_in.shape[0]
    for d in (w, m, v):
        d['w_in'] = jnp.swapaxes(d['w_in'], 1, 2)
    chip = 2 * lax.axis_index("x") + lax.axis_index("y")
    chip1 = chip.reshape(1).astype(jnp.int32)

    in_flight = []
    all_started = chip1
    for l in range(depth):
        bufs = [_place_shard(w[n], l, chip1, "place_" + n) for n in BIG_NAMES]
        handles, all_started = _gather_start(bufs, all_started, "gather_start_%d" % l)
        in_flight.append(dict(zip(BIG_NAMES, handles)))

    def big_weight(name, l, after):
        send, recv, buf = in_flight[l][name]
        if l == 0 and name == 'w_in':
            after = all_started
        full = _gather_wait(send, recv, buf, after, "gather_wait_%d_%s" % (l, name))
        if name == 'w_in':
            return full.reshape(IN_COLS, D_MODEL)
        if name == 'w_out':
            return full.reshape(D_MODEL, D_MODEL)
        if name == 'w_down':
            return full.reshape(D_FF, D_MODEL)
        return full

    gathered = _exchange_chips([meta_tokens, conv_w], "gather_small_weights", scatter=False)
    full_meta = jnp.concatenate([gathered[0][s] for s in range(N_CHIPS)], axis=1)
    small = {n: w[n] for n in REPLICATED_NAMES}
    small['conv_w'] = jnp.concatenate([gathered[1][s] for s in range(N_CHIPS)], axis=2)

    scattering = [{} for _ in range(depth)]

    def on_layer_grads(l, big):
        names = list(big)
        pieces = [big[n].reshape(N_CHIPS, -1, D_MODEL) for n in names]
        handles, started = _scatter_start(pieces, "scatter_start_%d_%s" % (l, names[0]))
        scattering[l].update(zip(names, handles))
        return started

    sq_err, grad_x, grad_meta, small_grads, all_handed_over = _local_step(x, loss_target, full_meta, small, depth, big_weight,
                                                                           on_layer_grads)
    loss = lax.psum(sq_err[0, 0] * (0.5 / D_MODEL), ("x", "y", "c"))

    small_names = list(REPLICATED_NAMES) + list(COLUMN_SHARDED_SMALL)
    small_full = dict(small_grads)
    small_full['meta_tokens'] = grad_meta
    device1 = (2 * chip + lax.axis_index("c")).reshape(1).astype(jnp.int32)
    packed = _place_slot(_pack([small_full[n] for n in small_names]), device1, N_DEV, "place_small_grads")
    [small_flight], done = _gather_start([packed], all_handed_over, "gather_small_grads_start", all_devices=True)

    partial = {n: None for n in BIG_NAMES}
    for l in reversed(range(depth)):
        for n in BIG_NAMES:
            send, recv, pieces, land = scattering[l][n]
            pieces, land = _scatter_wait(send, recv, pieces, land, done, "scatter_wait_%d_%s" % (l, n))
            partial[n], done = _sum_pieces(pieces, land, chip1, l, depth, partial[n])
    sibling = _swap_sibling([partial[n] for n in BIG_NAMES], "swap_partial_grads")
    results = {n: _adamw(w[n], m[n], v[n], [partial[n], other], "adamw_" + n) for n, other in zip(BIG_NAMES, sibling)}

    slots = _gather_wait(*small_flight, results['w_down'][0], "gather_small_grads_wait", all_devices=True)
    summed = _unpack(_sum_slots(slots, "sum_small_grads"), [small_full[n].shape for n in small_names])
    grads = dict(zip(small_names, summed))
    grads['meta_tokens'] = lax.dynamic_slice_in_dim(grads['meta_tokens'], chip * meta_tokens.shape[1], meta_tokens.shape[1], 1)
    grads['conv_w'] = lax.dynamic_slice_in_dim(grads['conv_w'], chip * conv_w.shape[2], conv_w.shape[2], 2)

    out_g, out_d, out_m, out_v = {}, {}, {}, {}
    for n in BIG_NAMES:
        out_g[n], out_d[n], out_m[n], out_v[n] = [jnp.swapaxes(r, 1, 2) if n == 'w_in' else r for r in results[n][:4]]
    shapes = [w[n].shape for n in small_names]
    res = _adamw(_pack([w[n] for n in small_names]), _pack([m[n] for n in small_names]), _pack([v[n] for n in small_names]),
                 [_pack([grads[n] for n in small_names])], "adamw_small")
    for k, store in enumerate((out_g, out_d, out_m, out_v)):
        for n, a in zip(small_names, _unpack(res[k], shapes)):
            store[n] = a

    return (loss, grad_x, *[out_g[n] for n in WEIGHT_NAMES], *[out_d[n] for n in WEIGHT_NAMES],
            *[out_m[n] for n in WEIGHT_NAMES], *[out_v[n] for n in WEIGHT_NAMES])
```

```python
import functools

import jax
import jax.numpy as jnp
from jax import lax
from jax.experimental import pallas as pl
from jax.experimental.pallas import tpu as pltpu

F32 = jnp.float32
BF16 = jnp.bfloat16

D_MODEL = 1024
N_HEADS = 8
ATTN_W = 512
KV_W = 128
LRU_W = 512
LRU_BLOCKS = 8
LRU_BLOCK = 64
IN_COLS = 1792
D_FF = 4096
N_META = 16
CONV_TAPS = 4
LRU_C = 8.0
ROPE_THETA = 10000.0
EPS = 1e-6
ATTN_SCALE = 0.125

ADAM_LR = 0.001
ADAM_B1 = 0.9
ADAM_B2 = 0.999
ADAM_EPS = 1e-08
ADAM_WD = 0.01
ADAM_STEP = 10

N_CHIPS = 4
N_DEV = 8
TIME_BLOCK = 128
ROW_TILE = 256
PROJ_ROW_TILE = 544
LANES = 128
SUBLANES = 8
MASKED = -1e30
VMEM_LIMIT = 56 * 1024 * 1024

MESH = pl.DeviceIdType.MESH
HBM_SPEC = pl.BlockSpec(memory_space=pltpu.HBM)

WEIGHT_NAMES = ['meta_tokens', 'pre_mix_norm', 'w_in', 'attn_sinks', 'conv_w', 'conv_b', 'gate_a_w', 'gate_a_b',
                'gate_x_w', 'gate_x_b', 'lru_lambda', 'attn_group_norm', 'lru_group_norm', 'w_out', 'post_mix_norm',
                'pre_mlp_norm', 'w_up', 'w_down', 'post_mlp_norm']
BIG_NAMES = ('w_in', 'w_out', 'w_up', 'w_down')
COLUMN_SHARDED_SMALL = ('meta_tokens', 'conv_w')
REPLICATED_NAMES = tuple(n for n in WEIGHT_NAMES if n not in BIG_NAMES and n not in COLUMN_SHARDED_SMALL)


def _sds(shape, dtype):
    return jax.ShapeDtypeStruct(tuple(shape), dtype)


def _params(*sem):
    return pltpu.CompilerParams(dimension_semantics=sem, vmem_limit_bytes=VMEM_LIMIT)


def _row_spec(width, tile=ROW_TILE):
    return pl.BlockSpec((tile, width), lambda i: (i, 0))


def _proj_tile(rows):
    tile = PROJ_ROW_TILE
    while rows % tile:
        tile -= 16
    return tile


def _whole_spec(a):
    nd = a.ndim
    return pl.BlockSpec(a.shape, lambda *_: (0,) * nd)


def _rms(x, g):
    r = lax.rsqrt(jnp.mean(x * x, axis=-1, keepdims=True) + EPS)
    return x * r * g


def _rms_bwd(dy, x, g):
    r = lax.rsqrt(jnp.mean(x * x, axis=-1, keepdims=True) + EPS)
    xh = x * r
    dg = jnp.sum(dy * xh, axis=0, keepdims=True)
    dxh = dy * g
    dx = r * (dxh - xh * jnp.mean(dxh * xh, axis=-1, keepdims=True))
    return dx, dg


def _rope(x, cos, sin_signed):
    width = x.shape[1]
    reps = width // LANES
    if reps > 1:
        cos = jnp.tile(cos, (1, reps))
        sin_signed = jnp.tile(sin_signed, (1, reps))
    lane = lax.broadcasted_iota(jnp.int32, x.shape, 1)
    first_half = jnp.bitwise_and(lane, 32) == 0
    other = jnp.where(first_half, pltpu.roll(x, width - 32, 1), pltpu.roll(x, 32, 1))
    return x * cos + other * sin_signed


def _sigmoid(x):
    return 1.0 / (1.0 + jnp.exp(-x))


def _log1p(e):
    return jnp.where(e < 1e-3, e * (1.0 - e * (0.5 - e * (1.0 / 3.0))), jnp.log(1.0 + e))


def _one_minus_square(a, log_a):
    x = 2.0 * log_a
    return jnp.where(x > -0.002, x * (-1.0 - 0.5 * x), 1.0 - a * a)


GELU_K = 0.7978845608028654
GELU_C = 0.044715


def _gelu(x):
    t = jnp.tanh(GELU_K * (x + GELU_C * x * x * x))
    return 0.5 * x * (1.0 + t)


def _gelu_and_grad(x):
    x2 = x * x
    t = jnp.tanh(GELU_K * (x + GELU_C * x * x2))
    val = 0.5 * x * (1.0 + t)
    grad = 0.5 * (1.0 + t) + 0.5 * x * (1.0 - t * t) * GELU_K * (1.0 + 3.0 * GELU_C * x2)
    return val, grad


def _dot(a, b):
    return jnp.dot(a, b, preferred_element_type=F32)


def _dot_nt(a, b):
    return lax.dot_general(a, b, (((1,), (1,)), ((), ())), preferred_element_type=F32)


def _dot_tn(a, b):
    return lax.dot_general(a, b, (((0,), (0,)), ((), ())), preferred_element_type=F32)


def _put_rows(rows_8, values):
    d = values[0].shape[1]
    rowid = lax.broadcasted_iota(jnp.int32, (rows_8, d), 0)
    out = jnp.zeros((rows_8, d), F32)
    for k, v in enumerate(values):
        out = out + jnp.where(rowid == k, v, 0.0)
    return out


def _in_proj_fwd(h, gain, w_in, rope_cos, rope_sin):
    rows = h.shape[0]

    def body(h_ref, g_ref, w_ref, c_ref, s_ref, z_ref, q_ref, kv_ref, xg_ref):
        z = _rms(h_ref[...], g_ref[...]).astype(BF16)
        z_ref[...] = z
        proj = _dot_nt(z, w_ref[...])
        cos = c_ref[...]
        sin = s_ref[...]
        q_ref[...] = (_rope(proj[:, :ATTN_W], cos, sin) * ATTN_SCALE).astype(BF16)
        kv_ref[:, :KV_W] = _rope(proj[:, ATTN_W:ATTN_W + KV_W], cos, sin).astype(BF16)
        kv_ref[:, KV_W:] = proj[:, ATTN_W + KV_W:ATTN_W + 2 * KV_W].astype(BF16)
        xg_ref[...] = proj[:, ATTN_W + 2 * KV_W:]

    tile = _proj_tile(rows)
    rs = functools.partial(_row_spec, tile=tile)
    return pl.pallas_call(
        body, name="in_proj_fwd", grid=(rows // tile,),
        in_specs=[rs(D_MODEL), _whole_spec(gain), _whole_spec(w_in), rs(LANES), rs(LANES)],
        out_specs=[rs(D_MODEL), rs(ATTN_W), rs(2 * KV_W), rs(2 * LRU_W)],
        out_shape=[_sds((rows, D_MODEL), BF16), _sds((rows, ATTN_W), BF16), _sds((rows, 2 * KV_W), BF16),
                   _sds((rows, 2 * LRU_W), F32)],
        compiler_params=_params("parallel"),
    )(h, gain, w_in, rope_cos, rope_sin)


def _in_proj_bwd(dh_mid, h, dq, dkv, dxg, rope_cos, rope_sin, gain, w_in):
    rows = h.shape[0]

    def body(dhm_ref, h_ref, dq_ref, dkv_ref, dxg_ref, c_ref, s_ref, g_ref, w_ref, dh_ref, dp_ref, gacc_ref):
        @pl.when(pl.program_id(0) == 0)
        def _():
            gacc_ref[...] = jnp.zeros_like(gacc_ref)

        cos = c_ref[...]
        sin = -s_ref[...]
        dp_ref[:, :ATTN_W] = (_rope(dq_ref[...], cos, sin) * ATTN_SCALE).astype(BF16)
        dp_ref[:, ATTN_W:ATTN_W + KV_W] = _rope(dkv_ref[:, :KV_W], cos, sin).astype(BF16)
        dp_ref[:, ATTN_W + KV_W:ATTN_W + 2 * KV_W] = dkv_ref[:, KV_W:].astype(BF16)
        dp_ref[:, ATTN_W + 2 * KV_W:] = dxg_ref[...]
        dz = _dot(dp_ref[...], w_ref[...])
        dx, dg = _rms_bwd(dz, h_ref[...], g_ref[...])
        dh_ref[...] = dhm_ref[...] + dx
        gacc_ref[...] += _put_rows(SUBLANES, [dg])

    tile = _proj_tile(rows)
    rs = functools.partial(_row_spec, tile=tile)
    return pl.pallas_call(
        body, name="in_proj_bwd", grid=(rows // tile,),
        in_specs=[rs(D_MODEL), rs(D_MODEL), rs(ATTN_W), rs(2 * KV_W), rs(2 * LRU_W),
                  rs(LANES), rs(LANES), _whole_spec(gain), _whole_spec(w_in)],
        out_specs=[rs(D_MODEL), rs(IN_COLS), pl.BlockSpec((SUBLANES, D_MODEL), lambda i: (0, 0))],
        out_shape=[_sds((rows, D_MODEL), F32), _sds((rows, IN_COLS), BF16), _sds((SUBLANES, D_MODEL), F32)],
        compiler_params=_params("arbitrary"),
    )(dh_mid, h, dq, dkv, dxg, rope_cos, rope_sin, gain, w_in)


def _kv_lane_variants(t, group):
    lane = lax.broadcasted_iota(jnp.int32, t.shape, 1)
    low = lane < 64
    swapped = pltpu.roll(t, 64, 1)
    if group == 0:
        lo, hi = jnp.where(low, t, 0.0), jnp.where(low, 0.0, swapped)
    else:
        lo, hi = jnp.where(low, swapped, 0.0), jnp.where(low, 0.0, t)
    return jnp.concatenate([lo, hi], axis=0).astype(BF16)


GROUP_ROWS = 2 * TIME_BLOCK
KEYS = 2 * TIME_BLOCK


def _window_mask(j):
    r = jnp.bitwise_and(lax.broadcasted_iota(jnp.int32, (GROUP_ROWS, KEYS), 0), TIME_BLOCK - 1)
    c = lax.broadcasted_iota(jnp.int32, (GROUP_ROWS, KEYS), 1)
    return (c > r) & (c <= r + TIME_BLOCK) & ((c >= TIME_BLOCK) | (j > 0))


def _group_rows(ref, ex, group):
    lo = 2 * group * LANES
    return jnp.concatenate([ref[ex, :, lo:lo + LANES], ref[ex, :, lo + LANES:lo + 2 * LANES]], axis=0)


def _per_head(sink_ref, group, half):
    upper = lax.broadcasted_iota(jnp.int32, (GROUP_ROWS, 1), 0) < TIME_BLOCK
    return jnp.where(upper, sink_ref[4 * group + half], sink_ref[4 * group + 2 + half])


def _by_example(a, batch):
    return a.reshape(batch, a.shape[0] // batch, a.shape[1])


def _attn_fwd(q, kv, sinks, batch, n_blocks):
    rows = q.shape[0]

    def body(sink_ref, q_ref, kvc_ref, kvp_ref, o_ref, lse_ref):
        j = pl.program_id(0)
        mask = _window_mask(j)
        lane8 = lax.broadcasted_iota(jnp.int32, (TIME_BLOCK, N_HEADS), 1)
        for ex in range(batch):
            kv2 = jnp.concatenate([kvp_ref[ex], kvc_ref[ex]], axis=0).astype(F32)
            lse_tile = jnp.zeros((TIME_BLOCK, N_HEADS), F32)
            for group in range(2):
                k_cat = _kv_lane_variants(kv2[:, :KV_W], group)
                v_cat = _kv_lane_variants(kv2[:, KV_W:], group)
                s_all = _dot_nt(_group_rows(q_ref, ex, group), k_cat)
                probs = []
                for half in range(2):
                    sink = _per_head(sink_ref, group, half)
                    s = jnp.where(mask, s_all[:, half * KEYS:(half + 1) * KEYS], MASKED)
                    m = jnp.maximum(jnp.max(s, axis=1, keepdims=True), sink)
                    e = jnp.exp(s - m)
                    den = jnp.sum(e, axis=1, keepdims=True) + jnp.exp(sink - m)
                    probs.append((e / den).astype(BF16))
                    lse = m + jnp.log(den)
                    lse_tile = lse_tile + jnp.where(lane8 == 4 * group + half, lse[:TIME_BLOCK], 0.0)
                    lse_tile = lse_tile + jnp.where(lane8 == 4 * group + 2 + half, lse[TIME_BLOCK:], 0.0)
                out = _dot(jnp.concatenate(probs, axis=1), v_cat)
                o_ref[ex, :, 2 * group * LANES:(2 * group + 1) * LANES] = out[:TIME_BLOCK]
                o_ref[ex, :, (2 * group + 1) * LANES:(2 * group + 2) * LANES] = out[TIME_BLOCK:]
            lse_ref[ex] = lse_tile

    def blk(width):
        return pl.BlockSpec((batch, TIME_BLOCK, width), lambda j: (0, j, 0))

    prev = pl.BlockSpec((batch, TIME_BLOCK, 2 * KV_W), lambda j: (0, jnp.maximum(j - 1, 0), 0))
    kv3 = _by_example(kv, batch)
    out, lse = pl.pallas_call(
        body, name="attn_fwd", grid=(n_blocks,),
        in_specs=[pl.BlockSpec(memory_space=pltpu.SMEM), blk(ATTN_W), blk(2 * KV_W), prev],
        out_specs=[blk(ATTN_W), blk(N_HEADS)],
        out_shape=[_sds((batch, rows // batch, ATTN_W), F32), _sds((batch, rows // batch, N_HEADS), F32)],
        compiler_params=_params("parallel"),
    )(sinks, _by_example(q, batch), kv3, kv3)
    return out.reshape(rows, ATTN_W), lse.reshape(rows, N_HEADS)


def _attn_bwd(q, kv, sinks, out, lse, d_out, batch, n_blocks):
    rows = q.shape[0]

    def body(sink_ref, q_ref, kvc_ref, kvp_ref, o_ref, do_ref, lse_ref, dq_ref, dkv_ref, dsink_ref, carry):
        j = pl.program_id(0)

        @pl.when(j == 0)
        def _():
            dsink_ref[...] = jnp.zeros_like(dsink_ref)

        def one_example(ex, dsink_vals):
            kv2 = jnp.concatenate([kvp_ref[ex], kvc_ref[ex]], axis=0).astype(F32)
            mask = _window_mask(j)
            lane = lax.broadcasted_iota(jnp.int32, (GROUP_ROWS, LANES), 1)
            low = lax.broadcasted_iota(jnp.int32, (KEYS, LANES), 1) < 64
            upper = lax.broadcasted_iota(jnp.int32, (GROUP_ROWS, 1), 0) < TIME_BLOCK
            lse_tile = lse_ref[ex]
            dk_tile = jnp.zeros((KEYS, KV_W), F32)
            dv_tile = jnp.zeros((KEYS, KV_W), F32)
            for group in range(2):
                k_cat = _kv_lane_variants(kv2[:, :KV_W], group)
                v_cat = _kv_lane_variants(kv2[:, KV_W:], group)
                q_rows = _group_rows(q_ref, ex, group)
                do_rows = _group_rows(do_ref, ex, group)
                do_b = do_rows.astype(BF16)
                od = do_rows * _group_rows(o_ref, ex, group)
                s_all = _dot_nt(q_rows, k_cat)
                dp_all = _dot_nt(do_b, v_cat)
                probs, dss = [], []
                for half in range(2):
                    heads = (4 * group + half, 4 * group + 2 + half)
                    sink = _per_head(sink_ref, group, half)
                    lse_h = jnp.concatenate([lse_tile[:, h:h + 1] for h in heads], axis=0)
                    in_half = (lane < 64) if half == 0 else (lane >= 64)
                    delta = jnp.sum(jnp.where(in_half, od, 0.0), axis=1, keepdims=True)
                    cols = slice(half * KEYS, (half + 1) * KEYS)
                    prob = jnp.exp(jnp.where(mask, s_all[:, cols], MASKED) - lse_h)
                    probs.append(prob.astype(BF16))
                    dss.append((prob * (dp_all[:, cols] - delta)).astype(BF16))
                    dsink = -jnp.exp(sink - lse_h) * delta
                    dsink_vals[heads[0]] = dsink_vals[heads[0]] + jnp.sum(jnp.where(upper, dsink, 0.0), axis=0, keepdims=True)
                    dsink_vals[heads[1]] = dsink_vals[heads[1]] + jnp.sum(jnp.where(upper, 0.0, dsink), axis=0, keepdims=True)
                ds = jnp.concatenate(dss, axis=1)
                dq_rows = _dot(ds, k_cat)
                dq_ref[ex, :, 2 * group * LANES:(2 * group + 1) * LANES] = dq_rows[:TIME_BLOCK]
                dq_ref[ex, :, (2 * group + 1) * LANES:(2 * group + 2) * LANES] = dq_rows[TIME_BLOCK:]
                dk_cat = _dot_tn(ds, q_rows)
                dv_cat = _dot_tn(jnp.concatenate(probs, axis=1), do_b)
                if group == 0:
                    dk_tile = dk_tile + jnp.where(low, dk_cat[:KEYS] + pltpu.roll(dk_cat[KEYS:], 64, 1), 0.0)
                    dv_tile = dv_tile + jnp.where(low, dv_cat[:KEYS] + pltpu.roll(dv_cat[KEYS:], 64, 1), 0.0)
                else:
                    dk_tile = dk_tile + jnp.where(low, 0.0, pltpu.roll(dk_cat[:KEYS], 64, 1) + dk_cat[KEYS:])
                    dv_tile = dv_tile + jnp.where(low, 0.0, pltpu.roll(dv_cat[:KEYS], 64, 1) + dv_cat[KEYS:])

            @pl.when(j > 0)
            def _():
                dkv_ref[ex, :, :KV_W] = carry[ex, :, :KV_W] + dk_tile[:TIME_BLOCK]
                dkv_ref[ex, :, KV_W:] = carry[ex, :, KV_W:] + dv_tile[:TIME_BLOCK]

            carry[ex, :, :KV_W] = dk_tile[TIME_BLOCK:]
            carry[ex, :, KV_W:] = dv_tile[TIME_BLOCK:]

        @pl.when(j < n_blocks)
        def _():
            dsink_vals = {head: jnp.zeros((1, 1), F32) for head in range(N_HEADS)}
            for ex in range(batch):
                one_example(ex, dsink_vals)
            rowid = lax.broadcasted_iota(jnp.int32, (N_HEADS, LANES), 0)
            upd = jnp.zeros((N_HEADS, LANES), F32)
            for head, val in dsink_vals.items():
                upd = upd + jnp.where(rowid == head, val, 0.0)
            dsink_ref[...] += upd

        @pl.when(j == n_blocks)
        def _():
            dkv_ref[...] = carry[...]

    last = n_blocks - 1

    def blk(width):
        return pl.BlockSpec((batch, TIME_BLOCK, width), lambda j: (0, jnp.minimum(j, last), 0))

    prev = pl.BlockSpec((batch, TIME_BLOCK, 2 * KV_W), lambda j: (0, jnp.maximum(jnp.minimum(j, last) - 1, 0), 0))
    dkv_spec = pl.BlockSpec((batch, TIME_BLOCK, 2 * KV_W), lambda j: (0, jnp.maximum(j - 1, 0), 0))
    kv3 = _by_example(kv, batch)
    dq, dkv, dsink = pl.pallas_call(
        body, name="attn_bwd", grid=(n_blocks + 1,),
        in_specs=[pl.BlockSpec(memory_space=pltpu.SMEM), blk(ATTN_W), blk(2 * KV_W), prev, blk(ATTN_W), blk(ATTN_W),
                  blk(N_HEADS)],
        out_specs=[blk(ATTN_W), dkv_spec, pl.BlockSpec((N_HEADS, LANES), lambda j: (0, 0))],
        out_shape=[_sds((batch, rows // batch, ATTN_W), F32), _sds((batch, rows // batch, 2 * KV_W), F32),
                   _sds((N_HEADS, LANES), F32)],
        scratch_shapes=[pltpu.VMEM((batch, TIME_BLOCK, 2 * KV_W), F32)],
        compiler_params=_params("arbitrary"),
    )(sinks, _by_example(q, batch), kv3, kv3, _by_example(out, batch), _by_example(d_out, batch), _by_example(lse, batch))
    return dq.reshape(rows, ATTN_W), dkv.reshape(rows, 2 * KV_W), dsink


def _conv_taps(xb, prev8):
    ext = jnp.concatenate([prev8, xb], axis=0)
    n = ext.shape[0]
    return [xb] + [pltpu.roll(ext, k, 0)[SUBLANES:n] for k in range(1, CONV_TAPS)]


def _lru_gates(xc, wa, ba, wx, bx, lam):
    xcb = xc.astype(BF16)
    r = _sigmoid(_dot(xcb, wa) + ba)
    i = _sigmoid(_dot(xcb, wx) + bx)
    sp = jnp.maximum(-lam, 0.0) + _log1p(jnp.exp(-jnp.abs(lam)))
    log_a = -LRU_C * r * sp
    a = jnp.exp(log_a)
    mult = jnp.sqrt(_one_minus_square(a, log_a))
    return xcb, r, i, sp, a, mult


def _scan_fwd(a, u, h_before):
    n, d = a.shape
    groups = n // SUBLANES
    a = a.reshape(groups, SUBLANES, d)
    u = u.reshape(groups, SUBLANES, d)
    sub = lax.broadcasted_iota(jnp.int32, a.shape, 1)
    s = 1
    while s < SUBLANES:
        valid = sub >= s
        u = jnp.where(valid, u + a * pltpu.roll(u, s, 1), u)
        a = jnp.where(valid, a * pltpu.roll(a, s, 1), a)
        s *= 2
    out, prev = [], h_before
    for g in range(groups):
        out.append(u[g] + a[g] * prev)
        prev = out[-1][SUBLANES - 1:SUBLANES, :]
    return jnp.concatenate(out, axis=0)


def _scan_rev(cf, g, d_after):
    n, d = g.shape
    groups = n // SUBLANES
    cf = cf.reshape(groups, SUBLANES, d)
    g = g.reshape(groups, SUBLANES, d)
    sub = lax.broadcasted_iota(jnp.int32, g.shape, 1)
    s = 1
    while s < SUBLANES:
        valid = sub + s < SUBLANES
        g = jnp.where(valid, g + cf * pltpu.roll(g, SUBLANES - s, 1), g)
        cf = jnp.where(valid, cf * pltpu.roll(cf, SUBLANES - s, 1), cf)
        s *= 2
    out, nxt = [None] * groups, d_after
    for k in reversed(range(groups)):
        out[k] = g[k] + cf[k] * nxt
        nxt = out[k][0:1, :]
    return jnp.concatenate(out, axis=0)


def _lru_fwd(xg, conv_w, conv_b, wa, ba, wx, bx, lam, batch, n_blocks):
    rows = xg.shape[0]

    def body(xg_ref, cw_ref, cb_ref, wa_ref, ba_ref, wx_ref, bx_ref, lam_ref, hs_ref, lru_ref, x_prev, h_carry):
        @pl.when(pl.program_id(0) == 0)
        def _():
            x_prev[...] = jnp.zeros_like(x_prev)
            h_carry[...] = jnp.zeros_like(h_carry)

        for ex in range(batch):
            xb = xg_ref[ex, :, :LRU_W]
            taps = _conv_taps(xb, x_prev[ex])
            xc = cb_ref[...] + sum(cw_ref[CONV_TAPS - 1 - k:CONV_TAPS - k, :] * taps[k] for k in range(CONV_TAPS))
            _, _, i, _, a, mult = _lru_gates(xc, wa_ref[...], ba_ref[...], wx_ref[...], bx_ref[...], lam_ref[...])
            h = _scan_fwd(a, mult * i * xc, h_carry[ex])
            hs_ref[ex] = h
            lru_ref[ex] = h * _gelu(xg_ref[ex, :, LRU_W:])
            h_carry[ex] = h[TIME_BLOCK - 1:TIME_BLOCK, :]
            x_prev[ex] = xb[TIME_BLOCK - SUBLANES:TIME_BLOCK, :]

    def blk(width):
        return pl.BlockSpec((batch, TIME_BLOCK, width), lambda j: (0, j, 0))

    small = [conv_w, conv_b, wa, ba, wx, bx, lam]
    hs, lru = pl.pallas_call(
        body, name="lru_fwd", grid=(n_blocks,),
        in_specs=[blk(2 * LRU_W)] + [_whole_spec(a) for a in small],
        out_specs=[blk(LRU_W), blk(LRU_W)],
        out_shape=[_sds((batch, rows // batch, LRU_W), F32)] * 2,
        scratch_shapes=[pltpu.VMEM((batch, SUBLANES, LRU_W), F32), pltpu.VMEM((batch, 1, LRU_W), F32)],
        compiler_params=_params("arbitrary"),
    )(_by_example(xg, batch), *small)
    return hs.reshape(rows, LRU_W), lru.reshape(rows, LRU_W)


def _lru_bwd(xg, hs, d_lru, conv_w, conv_b, wa, ba, wx, bx, lam, batch, n_blocks):
    rows = xg.shape[0]
    last_row = TIME_BLOCK - 1

    def body(xg_ref, xgh_ref, hs_ref, hsh_ref, dl_ref, cw_ref, cb_ref, wa_ref, ba_ref, wx_ref, bx_ref, lam_ref,
             dxg_ref, dwa_ref, dwx_ref, vec_ref, dh_carry, dxc_next):
        j = pl.program_id(0)

        @pl.when(j == 0)
        def _():
            dwa_ref[...] = jnp.zeros_like(dwa_ref)
            dwx_ref[...] = jnp.zeros_like(dwx_ref)
            vec_ref[...] = jnp.zeros_like(vec_ref)
            dh_carry[...] = jnp.zeros_like(dh_carry)
            dxc_next[...] = jnp.zeros_like(dxc_next)

        for ex in range(batch):
            one_example(ex, j, xg_ref, xgh_ref, hs_ref, hsh_ref, dl_ref, cw_ref, cb_ref, wa_ref, ba_ref, wx_ref, bx_ref,
                        lam_ref, dxg_ref, dwa_ref, dwx_ref, vec_ref, dh_carry, dxc_next)

    def one_example(ex, j, xg_ref, xgh_ref, hs_ref, hsh_ref, dl_ref, cw_ref, cb_ref, wa_ref, ba_ref, wx_ref, bx_ref, lam_ref,
                    dxg_ref, dwa_ref, dwx_ref, vec_ref, dh_carry, dxc_next):
        first = j == n_blocks - 1
        xb = xg_ref[ex, :, :LRU_W]
        prev8 = jnp.where(first, 0.0, xgh_ref[ex, :, :LRU_W])
        h_before = jnp.where(first, 0.0, hsh_ref[ex, SUBLANES - 1:SUBLANES, :])
        cw = cw_ref[...]
        lam = lam_ref[...]
        wa = wa_ref[...]
        wx = wx_ref[...]
        taps = _conv_taps(xb, prev8)
        xc = cb_ref[...] + sum(cw[CONV_TAPS - 1 - k:CONV_TAPS - k, :] * taps[k] for k in range(CONV_TAPS))
        xcb, r, i, sp, a, mult = _lru_gates(xc, wa, ba_ref[...], wx, bx_ref[...], lam)
        hs = hs_ref[ex]
        row = lax.broadcasted_iota(jnp.int32, hs.shape, 0)
        h_prev = jnp.where(row == 0, h_before, pltpu.roll(hs, 1, 0))
        dl = dl_ref[ex]
        gate, dgate = _gelu_and_grad(xg_ref[ex, :, LRU_W:])
        dxg_ref[ex, :, LRU_W:] = (dl * hs * dgate).astype(BF16)
        cf = jnp.where(row == last_row, 1.0, pltpu.roll(a, last_row, 0))
        dh = _scan_rev(cf, dl * gate, dh_carry[ex])
        dh_carry[ex] = a[0:1, :] * dh[0:1, :]
        dmult = dh * i * xc
        di = dh * mult * xc
        dxc = dh * mult * i
        dlog_a = dh * h_prev * a - dmult * (a * a / mult)
        dr = dlog_a * (-LRU_C * sp)
        dlam = jnp.sum(dlog_a * (-LRU_C * r), axis=0, keepdims=True) * (-_sigmoid(-lam))
        dpr = dr * r * (1.0 - r)
        dpi = di * i * (1.0 - i)
        dprb = dpr.astype(BF16)
        dpib = dpi.astype(BF16)
        dxc = dxc + _dot_nt(dprb, wa) + _dot_nt(dpib, wx)
        dwa_ref[...] += _dot_tn(xcb, dprb)
        dwx_ref[...] += _dot_tn(xcb, dpib)
        ext = jnp.concatenate([dxc, dxc_next[ex]], axis=0)
        n = ext.shape[0]
        dxb = cw[CONV_TAPS - 1:CONV_TAPS, :] * dxc
        for k in range(1, CONV_TAPS):
            dxb = dxb + cw[CONV_TAPS - 1 - k:CONV_TAPS - k, :] * pltpu.roll(ext, n - k, 0)[:TIME_BLOCK]
        dxg_ref[ex, :, :LRU_W] = dxb.astype(BF16)
        dxc_next[ex] = dxc[:SUBLANES, :]
        vecs = [jnp.sum(dxc * taps[CONV_TAPS - 1 - t], axis=0, keepdims=True) for t in range(CONV_TAPS)]
        vecs += [jnp.sum(dxc, axis=0, keepdims=True), jnp.sum(dpr, axis=0, keepdims=True),
                 jnp.sum(dpi, axis=0, keepdims=True), dlam]
        vec_ref[...] += _put_rows(SUBLANES, vecs)

    def tblk(j):
        return n_blocks - 1 - j

    def blk(width):
        return pl.BlockSpec((batch, TIME_BLOCK, width), lambda j: (0, tblk(j), 0))

    per8 = TIME_BLOCK // SUBLANES

    def halo(width):
        return pl.BlockSpec((batch, SUBLANES, width), lambda j: (0, jnp.maximum(per8 * tblk(j) - 1, 0), 0))

    small = [conv_w, conv_b, wa, ba, wx, bx, lam]
    acc = lambda shape: pl.BlockSpec(shape, lambda j: (0, 0))
    xg3, hs3 = _by_example(xg, batch), _by_example(hs, batch)
    dxg, dwa, dwx, vec = pl.pallas_call(
        body, name="lru_bwd", grid=(n_blocks,),
        in_specs=[blk(2 * LRU_W), halo(2 * LRU_W), blk(LRU_W), halo(LRU_W), blk(LRU_W)] + [_whole_spec(a) for a in small],
        out_specs=[blk(2 * LRU_W), acc((LRU_W, LRU_W)), acc((LRU_W, LRU_W)), acc((SUBLANES, LRU_W))],
        out_shape=[_sds((batch, rows // batch, 2 * LRU_W), BF16), _sds((LRU_W, LRU_W), F32), _sds((LRU_W, LRU_W), F32),
                   _sds((SUBLANES, LRU_W), F32)],
        scratch_shapes=[pltpu.VMEM((batch, 1, LRU_W), F32), pltpu.VMEM((batch, SUBLANES, LRU_W), F32)],
        compiler_params=_params("arbitrary"),
    )(xg3, xg3, hs3, hs3, _by_example(d_lru, batch), *small)
    return dxg.reshape(rows, 2 * LRU_W), dwa, dwx, vec


def _out_proj_fwd(attn, lru, g_attn, g_lru, w_out, h, g_post):
    rows = h.shape[0]

    def body(at_ref, lr_ref, ga_ref, gl_ref, w_ref, h_ref, gp_ref, grp_ref, o_ref, hm_ref):
        a = _rms(at_ref[...], ga_ref[...]).astype(BF16)
        l = _rms(lr_ref[...], gl_ref[...]).astype(BF16)
        grp_ref[:, :ATTN_W] = a
        grp_ref[:, ATTN_W:] = l
        o = _dot(a, w_ref[:ATTN_W, :]) + _dot(l, w_ref[ATTN_W:, :])
        o_ref[...] = o
        hm_ref[...] = h_ref[...] + _rms(o, gp_ref[...])

    tile = _proj_tile(rows)
    rs = functools.partial(_row_spec, tile=tile)
    return pl.pallas_call(
        body, name="out_proj_fwd", grid=(rows // tile,),
        in_specs=[rs(ATTN_W), rs(LRU_W), _whole_spec(g_attn), _whole_spec(g_lru), _whole_spec(w_out),
                  rs(D_MODEL), _whole_spec(g_post)],
        out_specs=[rs(D_MODEL), rs(D_MODEL), rs(D_MODEL)],
        out_shape=[_sds((rows, D_MODEL), BF16), _sds((rows, D_MODEL), F32), _sds((rows, D_MODEL), F32)],
        compiler_params=_params("parallel"),
    )(attn, lru, g_attn, g_lru, w_out, h, g_post)


def _out_proj_bwd(dh_mid, o, attn, lru, g_attn, g_lru, g_post, w_out):
    rows = o.shape[0]

    def body(dhm_ref, o_ref, at_ref, lr_ref, ga_ref, gl_ref, gp_ref, w_ref, do_ref, dat_ref, dlr_ref, gacc_ref):
        @pl.when(pl.program_id(0) == 0)
        def _():
            gacc_ref[...] = jnp.zeros_like(gacc_ref)

        do, dgp = _rms_bwd(dhm_ref[...], o_ref[...], gp_ref[...])
        dob = do.astype(BF16)
        do_ref[...] = dob
        dat, dga = _rms_bwd(_dot_nt(dob, w_ref[:ATTN_W, :]), at_ref[...], ga_ref[...])
        dlr, dgl = _rms_bwd(_dot_nt(dob, w_ref[ATTN_W:, :]), lr_ref[...], gl_ref[...])
        dat_ref[...] = dat
        dlr_ref[...] = dlr
        gacc_ref[...] += _put_rows(SUBLANES, [dgp, jnp.concatenate([dga, dgl], axis=1)])

    tile = _proj_tile(rows)
    rs = functools.partial(_row_spec, tile=tile)
    return pl.pallas_call(
        body, name="out_proj_bwd", grid=(rows // tile,),
        in_specs=[rs(D_MODEL), rs(D_MODEL), rs(ATTN_W), rs(LRU_W), _whole_spec(g_attn),
                  _whole_spec(g_lru), _whole_spec(g_post), _whole_spec(w_out)],
        out_specs=[rs(D_MODEL), rs(ATTN_W), rs(LRU_W),
                   pl.BlockSpec((SUBLANES, D_MODEL), lambda i: (0, 0))],
        out_shape=[_sds((rows, D_MODEL), BF16), _sds((rows, ATTN_W), F32), _sds((rows, LRU_W), F32),
                   _sds((SUBLANES, D_MODEL), F32)],
        compiler_params=_params("arbitrary"),
    )(dh_mid, o, attn, lru, g_attn, g_lru, g_post, w_out)


def _mlp_fwd(h_mid, g_pre, w_up4, w_down, g_post):
    rows = h_mid.shape[0]

    def body(h_ref, g1_ref, wu_ref, wd_ref, g2_ref, z_ref, slope_ref, act_ref, y_ref, ho_ref):
        x = h_ref[...]
        z = _rms(x, g1_ref[...]).astype(BF16)
        z_ref[...] = z
        y = jnp.zeros((ROW_TILE, D_MODEL), F32)
        for s in range(N_CHIPS):
            cols = slice(s * D_MODEL, (s + 1) * D_MODEL)
            r = jnp.maximum(_dot(z, wu_ref[s]), 0.0)
            slope_ref[:, cols] = (2.0 * r).astype(BF16)
            a = jnp.square(r).astype(BF16)
            act_ref[:, cols] = a
            y = y + _dot(a, wd_ref[cols, :])
        y_ref[...] = y
        ho_ref[...] = x + _rms(y, g2_ref[...])

    return pl.pallas_call(
        body, name="mlp_fwd", grid=(rows // ROW_TILE,),
        in_specs=[_row_spec(D_MODEL), _whole_spec(g_pre), _whole_spec(w_up4), _whole_spec(w_down), _whole_spec(g_post)],
        out_specs=[_row_spec(D_MODEL), _row_spec(D_FF), _row_spec(D_FF), _row_spec(D_MODEL), _row_spec(D_MODEL)],
        out_shape=[_sds((rows, D_MODEL), BF16), _sds((rows, D_FF), BF16), _sds((rows, D_FF), BF16),
                   _sds((rows, D_MODEL), F32), _sds((rows, D_MODEL), F32)],
        compiler_params=_params("parallel"),
    )(h_mid, g_pre, w_up4, w_down, g_post)


def _mlp_bwd(dh_out, h_mid, y, slope, w_up4, w_down, g_pre, g_post):
    rows = y.shape[0]

    def body(dh_ref, hm_ref, y_ref, slope_ref, wu_ref, wd_ref, g1_ref, g2_ref, dhm_ref, dy_ref, dup_ref, gacc_ref):
        @pl.when(pl.program_id(0) == 0)
        def _():
            gacc_ref[...] = jnp.zeros_like(gacc_ref)

        dh = dh_ref[...]
        dy, dg2 = _rms_bwd(dh, y_ref[...], g2_ref[...])
        dyb = dy.astype(BF16)
        dy_ref[...] = dyb
        dz = jnp.zeros((ROW_TILE, D_MODEL), F32)
        for s in range(N_CHIPS):
            cols = slice(s * D_MODEL, (s + 1) * D_MODEL)
            dact = _dot_nt(dyb, wd_ref[cols, :])
            dup = (dact * slope_ref[:, cols].astype(F32)).astype(BF16)
            dup_ref[:, cols] = dup
            dz = dz + _dot_nt(dup, wu_ref[s])
        dx, dg1 = _rms_bwd(dz, hm_ref[...], g1_ref[...])
        dhm_ref[...] = dh + dx
        gacc_ref[...] += _put_rows(SUBLANES, [dg1, dg2])

    return pl.pallas_call(
        body, name="mlp_bwd", grid=(rows // ROW_TILE,),
        in_specs=[_row_spec(D_MODEL), _row_spec(D_MODEL), _row_spec(D_MODEL), _row_spec(D_FF), _whole_spec(w_up4),
                  _whole_spec(w_down), _whole_spec(g_pre), _whole_spec(g_post)],
        out_specs=[_row_spec(D_MODEL), _row_spec(D_MODEL), _row_spec(D_FF),
                   pl.BlockSpec((SUBLANES, D_MODEL), lambda i: (0, 0))],
        out_shape=[_sds((rows, D_MODEL), F32), _sds((rows, D_MODEL), BF16), _sds((rows, D_FF), BF16),
                   _sds((SUBLANES, D_MODEL), F32)],
        compiler_params=_params("arbitrary"),
    )(dh_out, h_mid, y, slope, w_up4, w_down, g_pre, g_post)


def _matmul_tn(a, b, tm, tn, name, out_dtype, column_blocks=False):
    rows, m = a.shape
    n = b.shape[1]

    def body(a_ref, b_ref, o_ref):
        o_ref[...] = _dot_tn(a_ref[...], b_ref[...]).astype(out_dtype)

    if column_blocks:
        out_spec = pl.BlockSpec((None, tm, tn), lambda i, j: (j, i, 0))
        out_shape = _sds((n // tn, m, tn), out_dtype)
    else:
        out_spec = pl.BlockSpec((tm, tn), lambda i, j: (i, j))
        out_shape = _sds((m, n), out_dtype)
    return pl.pallas_call(
        body, name=name, grid=(m // tm, n // tn),
        in_specs=[pl.BlockSpec((rows, tm), lambda i, j: (0, i)), pl.BlockSpec((rows, tn), lambda i, j: (0, j))],
        out_specs=out_spec, out_shape=out_shape,
        compiler_params=_params("parallel", "parallel"),
    )(a, b)


LOSS_COLS = 256


def _loss_head(h, loss_target):
    batch, per_example, _ = h.shape
    seq = loss_target.shape[1]
    n_real = N_META + seq
    assert seq % SUBLANES == 0

    def body(h_ref, t_ref, dh_ref, l_ref):
        @pl.when((pl.program_id(0) == 0) & (pl.program_id(1) == 0))
        def _():
            l_ref[...] = jnp.zeros_like(l_ref)

        d = h_ref[N_META:n_real, :] - t_ref[...]
        dh_ref[:N_META, :] = jnp.zeros((N_META, LOSS_COLS), F32)
        dh_ref[N_META:n_real, :] = d * (1.0 / D_MODEL)
        dh_ref[n_real:, :] = jnp.zeros((per_example - n_real, LOSS_COLS), F32)
        l_ref[...] += jnp.sum(jnp.sum(d * d, axis=0, keepdims=True), axis=1, keepdims=True)

    blk = pl.BlockSpec((None, per_example, LOSS_COLS), lambda b, j: (b, 0, j))
    return pl.pallas_call(
        body, name="loss_head", grid=(batch, D_MODEL // LOSS_COLS),
        in_specs=[blk, pl.BlockSpec((None, seq, LOSS_COLS), lambda b, j: (b, 0, j))],
        out_specs=[blk, pl.BlockSpec((SUBLANES, LANES), lambda b, j: (0, 0))],
        out_shape=[_sds(h.shape, F32), _sds((SUBLANES, LANES), F32)],
        compiler_params=_params("arbitrary", "arbitrary"),
    )(h, loss_target)


def _meta_grad(dh0, batch, rows_per_example):
    per = rows_per_example // N_META

    def body(d_ref, o_ref):
        @pl.when(pl.program_id(0) == 0)
        def _():
            o_ref[...] = jnp.zeros_like(o_ref)

        o_ref[...] += d_ref[...]

    return pl.pallas_call(
        body, name="meta_grad", grid=(batch,),
        in_specs=[pl.BlockSpec((N_META, D_MODEL), lambda b: (b * per, 0))],
        out_specs=pl.BlockSpec((N_META, D_MODEL), lambda b: (0, 0)),
        out_shape=_sds((N_META, D_MODEL), F32),
        compiler_params=_params("arbitrary"),
    )(dh0)


def _elementwise_tile(rows, cols):
    tile = rows
    while tile * cols * 4 > (1 << 20) and tile % 16 == 0:
        tile //= 2
    return tile


def _sum_slots(buf, name):
    k, rows, cols = buf.shape
    tile = _elementwise_tile(rows, cols)

    def body(*refs):
        total = refs[0][...].astype(F32)
        for r in refs[1:k]:
            total = total + r[...].astype(F32)
        refs[k][...] = total

    def slot(s):
        return pl.BlockSpec((None, tile, cols), lambda i: (s, i, 0))

    return pl.pallas_call(
        body, name=name, grid=(rows // tile,),
        in_specs=[slot(s) for s in range(k)], out_specs=pl.BlockSpec((tile, cols), lambda i: (i, 0)),
        out_shape=_sds((rows, cols), F32), compiler_params=_params("parallel"),
    )(*([buf] * k))


def _sum_pieces(pieces, landed, chip, layer, n_layers, stacked):
    _, rows, cols = pieces.shape
    tile = _elementwise_tile(rows, cols)

    def body(chip_ref, own_ref, a_ref, b_ref, c_ref, *rest):
        o_ref, token = rest[-2:]
        total = ((own_ref[...].astype(F32) + a_ref[...].astype(F32)) + b_ref[...].astype(F32)) + c_ref[...].astype(F32)
        o_ref[...] = total.astype(BF16)
        token[...] = jnp.zeros_like(token)

    def slot(offset):
        return pl.BlockSpec((None, tile, cols), lambda i, chip_ref: ((chip_ref[0] + offset) % N_CHIPS, i, 0))

    carried = [] if stacked is None else [stacked]
    grid_spec = pltpu.PrefetchScalarGridSpec(
        num_scalar_prefetch=1, grid=(rows // tile,),
        in_specs=[slot(0), slot(1), slot(2), slot(3)] + [pl.BlockSpec(memory_space=pl.ANY)] * len(carried),
        out_specs=[pl.BlockSpec((None, tile, cols), lambda i, chip_ref: (layer, i, 0)),
                   pl.BlockSpec((SUBLANES, LANES), lambda i, chip_ref: (0, 0))])
    return pl.pallas_call(
        body, name="sum_grad_pieces", grid_spec=grid_spec,
        out_shape=[_sds((n_layers, rows, cols), BF16), _sds((SUBLANES, LANES), F32)],
        input_output_aliases={5: 0} if carried else {},
        compiler_params=_params("arbitrary"),
    )(chip, pieces, landed, landed, landed, *carried)


def _adamw(w, m, v, grads, name):
    flat = w.ndim == 2
    if flat:
        w, m, v = w[None], m[None], v[None]
        grads = [g[None] for g in grads]
    layers, rows, cols = w.shape
    tile = _elementwise_tile(rows, cols)
    ng = len(grads)
    m_scale = 1.0 - ADAM_B1 ** ADAM_STEP
    v_scale = 1.0 - ADAM_B2 ** ADAM_STEP

    def body(*refs):
        w_ref, m_ref, v_ref = refs[:3]
        g_refs = refs[3:3 + ng]
        g_out, d_out, m_out, v_out = refs[-4:]
        g = g_refs[0][...].astype(F32)
        for r in g_refs[1:]:
            g = g + r[...].astype(F32)
        m_new = ADAM_B1 * m_ref[...] + (1.0 - ADAM_B1) * g
        v_new = ADAM_B2 * v_ref[...] + (1.0 - ADAM_B2) * (g * g)
        m_hat = m_new / m_scale
        v_hat = v_new / v_scale
        g_out[...] = g
        d_out[...] = -ADAM_LR * (m_hat / (jnp.sqrt(v_hat) + ADAM_EPS) + ADAM_WD * w_ref[...])
        m_out[...] = m_new
        v_out[...] = v_new

    spec = pl.BlockSpec((None, tile, cols), lambda l, i: (l, i, 0))
    out = pl.pallas_call(
        body, name=name, grid=(layers, rows // tile),
        in_specs=[spec] * (3 + ng), out_specs=[spec] * 4, out_shape=[_sds(w.shape, F32)] * 4,
        compiler_params=_params("parallel", "parallel"),
    )(w, m, v, *grads)
    return [o[0] for o in out] if flat else out


def _position():
    return lax.axis_index("x"), lax.axis_index("y"), lax.axis_index("c")


def _other_chips(x, y):
    return [(1 - x, y), (x, 1 - y), (1 - x, 1 - y)]


def _exchange_chips(arrays, name, scatter):
    n = len(arrays)

    def body(*refs):
        src, dst = refs[:n], refs[n:2 * n]
        send_sems, recv_sems, local_sems = refs[2 * n:]
        x, y, c = _position()
        mine = 2 * x + y
        copies = []
        for i in range(n):
            own = src[i].at[mine] if scatter else src[i]
            copies.append(pltpu.make_async_copy(own, dst[i].at[mine], local_sems.at[i]))
        for k, (px, py) in enumerate(_other_chips(x, y)):
            for i in range(n):
                piece = src[i].at[2 * px + py] if scatter else src[i]
                copies.append(pltpu.make_async_remote_copy(
                    src_ref=piece, dst_ref=dst[i].at[mine], send_sem=send_sems.at[k, i], recv_sem=recv_sems.at[k, i],
                    device_id=(px, py, c), device_id_type=MESH))
        for cp in copies:
            cp.start()
        for cp in copies:
            cp.wait()

    def out_shape(a):
        return _sds(a.shape if scatter else (N_CHIPS,) + a.shape, a.dtype)

    return pl.pallas_call(
        body, name=name, in_specs=[HBM_SPEC] * n, out_specs=[HBM_SPEC] * n, out_shape=[out_shape(a) for a in arrays],
        scratch_shapes=[pltpu.SemaphoreType.DMA((N_CHIPS - 1, n)), pltpu.SemaphoreType.DMA((N_CHIPS - 1, n)),
                        pltpu.SemaphoreType.DMA((n,))],
    )(*arrays)


def _swap_sibling(arrays, name):
    n = len(arrays)

    def body(*refs):
        src, dst = refs[:n], refs[n:2 * n]
        send_sems, recv_sems = refs[2 * n:]
        x, y, c = _position()
        copies = [pltpu.make_async_remote_copy(
            src_ref=src[i], dst_ref=dst[i], send_sem=send_sems.at[i], recv_sem=recv_sems.at[i],
            device_id=(x, y, 1 - c), device_id_type=MESH) for i in range(n)]
        for cp in copies:
            cp.start()
        for cp in copies:
            cp.wait()

    return pl.pallas_call(
        body, name=name, in_specs=[HBM_SPEC] * n, out_specs=[HBM_SPEC] * n,
        out_shape=[_sds(a.shape, a.dtype) for a in arrays],
        scratch_shapes=[pltpu.SemaphoreType.DMA((n,)), pltpu.SemaphoreType.DMA((n,))],
    )(*arrays)


SEM_SPEC = pl.BlockSpec(memory_space=pltpu.SEMAPHORE)
ANY_SPEC = pl.BlockSpec(memory_space=pl.ANY)
IN_FLIGHT = pltpu.SideEffectType.DATAFLOW_SIDE_EFFECTING


def _peer_sems(all_devices):
    return pltpu.SemaphoreType.DMA(((N_DEV if all_devices else N_CHIPS) - 1,))


def _own_slot(all_devices, x, y, c):
    return 4 * x + 2 * y + c if all_devices else 2 * x + y


def _peers(all_devices, x, y, c):
    if not all_devices:
        return [((px, py, c), 2 * px + py) for px, py in _other_chips(x, y)]
    out = []
    for fx in range(2):
        for fy in range(2):
            for fc in range(2):
                if fx or fy or fc:
                    px, py, pc = (1 - x if fx else x), (1 - y if fy else y), (1 - c if fc else c)
                    out.append(((px, py, pc), 4 * px + 2 * py + pc))
    return out


def _in_hbm(a):
    return pltpu.with_memory_space_constraint(a, pltpu.HBM)


def _place_slot(a, slot, n_slots, name):
    rows, cols = a.shape
    tile = _elementwise_tile(rows, cols)

    def body(slot_ref, a_ref, o_ref):
        o_ref[...] = a_ref[...]

    grid_spec = pltpu.PrefetchScalarGridSpec(
        num_scalar_prefetch=1, grid=(rows // tile,),
        in_specs=[pl.BlockSpec((tile, cols), lambda i, slot_ref: (i, 0))],
        out_specs=pl.BlockSpec((None, tile, cols), lambda i, slot_ref: (slot_ref[0], i, 0)))
    return pl.pallas_call(
        body, name=name, grid_spec=grid_spec, out_shape=_sds((n_slots, rows, cols), a.dtype),
        compiler_params=_params("parallel"),
    )(slot, a)


def _place_shard(w, layer, chip, name):
    _, rows, cols = w.shape
    tile = _elementwise_tile(rows, cols)

    def body(chip_ref, w_ref, o_ref):
        o_ref[...] = w_ref[...].astype(BF16)

    grid_spec = pltpu.PrefetchScalarGridSpec(
        num_scalar_prefetch=1, grid=(rows // tile,),
        in_specs=[pl.BlockSpec((None, tile, cols), lambda i, chip_ref: (layer, i, 0))],
        out_specs=pl.BlockSpec((None, tile, cols), lambda i, chip_ref: (chip_ref[0], i, 0)))
    return pl.pallas_call(
        body, name=name, grid_spec=grid_spec, out_shape=_sds((N_CHIPS, rows, cols), BF16),
        compiler_params=_params("parallel"),
    )(chip, w)


def _rows_of(ref, slot, core, halves):
    if not halves:
        return ref.at[slot]
    half = ref.shape[1] // 2
    return ref.at[slot, pl.ds(pl.multiple_of(core * half, half), half)]


def _gather_start(bufs, after, name, all_devices=False, halves=False):
    n = len(bufs)

    def body(*refs):
        buf = refs[:n]
        send, recv = refs[n + 1:2 * n + 1], refs[2 * n + 1:3 * n + 1]
        token = refs[4 * n + 1]
        x, y, c = _position()
        mine = _own_slot(all_devices, x, y, c)
        for i in range(n):
            for k, (peer, _) in enumerate(_peers(all_devices, x, y, c)):
                rows = _rows_of(buf[i], mine, c, halves)
                pltpu.make_async_remote_copy(
                    src_ref=rows, dst_ref=rows, send_sem=send[i].at[k], recv_sem=recv[i].at[k],
                    device_id=peer, device_id_type=MESH).start()
        token[...] = jnp.zeros_like(token)

    sems = _peer_sems(all_devices)
    out = pl.pallas_call(
        body, name=name, in_specs=[HBM_SPEC] * n + [ANY_SPEC],
        out_specs=[SEM_SPEC] * (2 * n) + [HBM_SPEC] * n + [pl.BlockSpec(memory_space=pltpu.VMEM)],
        out_shape=[sems] * (2 * n) + [pltpu.HBM(b.shape, b.dtype) for b in bufs] + [_sds((SUBLANES, LANES), F32)],
        input_output_aliases={i: 2 * n + i for i in range(n)},
        compiler_params=pltpu.CompilerParams(has_side_effects=IN_FLIGHT),
    )(*[_in_hbm(b) for b in bufs], after)
    return [(out[i], out[n + i], out[2 * n + i]) for i in range(n)], out[3 * n]


def _gather_relay(flights, after, name):
    n = len(flights)

    def body(*refs):
        buf, send, recv = refs[:n], refs[n:2 * n], refs[2 * n:3 * n]
        send2, recv2 = refs[3 * n + 1:4 * n + 1], refs[4 * n + 1:5 * n + 1]
        token = refs[6 * n + 1]
        x, y, c = _position()
        mine = 2 * x + y
        for i in range(n):
            for k, (peer, slot) in enumerate(_peers(False, x, y, c)):
                arrived = _rows_of(buf[i], slot, c, True)
                cp = pltpu.make_async_remote_copy(
                    src_ref=_rows_of(buf[i], mine, c, True), dst_ref=arrived, send_sem=send[i].at[k], recv_sem=recv[i].at[k],
                    device_id=peer, device_id_type=MESH)
                cp.wait_send()
                cp.wait_recv()
                pltpu.make_async_remote_copy(
                    src_ref=arrived, dst_ref=arrived, send_sem=send2[i].at[k], recv_sem=recv2[i].at[k],
                    device_id=(x, y, 1 - c), device_id_type=MESH).start()
        token[...] = jnp.zeros_like(token)

    bufs = [f[2] for f in flights]
    sems = _peer_sems(False)
    out = pl.pallas_call(
        body, name=name, in_specs=[HBM_SPEC] * n + [SEM_SPEC] * (2 * n) + [ANY_SPEC],
        out_specs=[SEM_SPEC] * (2 * n) + [HBM_SPEC] * n + [pl.BlockSpec(memory_space=pltpu.VMEM)],
        out_shape=[sems] * (2 * n) + [pltpu.HBM(b.shape, b.dtype) for b in bufs] + [_sds((SUBLANES, LANES), F32)],
        input_output_aliases={i: 2 * n + i for i in range(n)},
        compiler_params=pltpu.CompilerParams(has_side_effects=IN_FLIGHT),
    )(*bufs, *[f[0] for f in flights], *[f[1] for f in flights], after)
    return [(out[i], out[n + i], out[2 * n + i]) for i in range(n)], out[3 * n]


def _gather_wait(send, recv, buf, after, name, all_devices=False, relayed=False):
    def body(buf_ref, send_ref, recv_ref, after_ref, out_ref):
        x, y, c = _position()
        mine = _own_slot(all_devices, x, y, c)
        for k, (peer, slot) in enumerate(_peers(all_devices, x, y, c)):
            if relayed:
                cp = pltpu.make_async_remote_copy(
                    src_ref=_rows_of(buf_ref, slot, c, True), dst_ref=_rows_of(buf_ref, slot, 1 - c, True),
                    send_sem=send_ref.at[k], recv_sem=recv_ref.at[k], device_id=(x, y, 1 - c), device_id_type=MESH)
            else:
                cp = pltpu.make_async_remote_copy(
                    src_ref=buf_ref.at[mine], dst_ref=buf_ref.at[slot], send_sem=send_ref.at[k], recv_sem=recv_ref.at[k],
                    device_id=peer, device_id_type=MESH)
            cp.wait_send()
            cp.wait_recv()

    return pl.pallas_call(
        body, name=name, in_specs=[HBM_SPEC, SEM_SPEC, SEM_SPEC, ANY_SPEC], out_specs=HBM_SPEC,
        out_shape=pltpu.HBM(buf.shape, buf.dtype), input_output_aliases={0: 0},
        compiler_params=pltpu.CompilerParams(has_side_effects=IN_FLIGHT),
    )(buf, send, recv, after)


def _scatter_start(pieces, name):
    n = len(pieces)

    def body(*refs):
        src = refs[:n]
        send, recv = refs[n:2 * n], refs[2 * n:3 * n]
        land = refs[4 * n:5 * n]
        token = refs[5 * n]
        x, y, c = _position()
        mine = 2 * x + y
        for i in range(n):
            for k, (px, py) in enumerate(_other_chips(x, y)):
                pltpu.make_async_remote_copy(
                    src_ref=src[i].at[2 * px + py], dst_ref=land[i].at[mine], send_sem=send[i].at[k], recv_sem=recv[i].at[k],
                    device_id=(px, py, c), device_id_type=MESH).start()
        token[...] = jnp.zeros_like(token)

    hbm = [pltpu.HBM(p.shape, p.dtype) for p in pieces]
    out = pl.pallas_call(
        body, name=name, in_specs=[HBM_SPEC] * n,
        out_specs=[SEM_SPEC] * (2 * n) + [HBM_SPEC] * (2 * n) + [pl.BlockSpec(memory_space=pltpu.VMEM)],
        out_shape=[_peer_sems(False)] * (2 * n) + hbm + hbm + [_sds((SUBLANES, LANES), F32)],
        input_output_aliases={i: 2 * n + i for i in range(n)},
        compiler_params=pltpu.CompilerParams(has_side_effects=IN_FLIGHT),
    )(*[_in_hbm(p) for p in pieces])
    return [(out[i], out[n + i], out[2 * n + i], out[3 * n + i]) for i in range(n)], out[4 * n]


def _scatter_wait(send, recv, pieces, land, after, name):
    def body(src_ref, land_ref, send_ref, recv_ref, after_ref, src_out, land_out):
        x, y, c = _position()
        for k, (px, py) in enumerate(_other_chips(x, y)):
            cp = pltpu.make_async_remote_copy(
                src_ref=src_ref.at[2 * px + py], dst_ref=land_ref.at[2 * px + py], send_sem=send_ref.at[k],
                recv_sem=recv_ref.at[k], device_id=(px, py, c), device_id_type=MESH)
            cp.wait_send()
            cp.wait_recv()

    return pl.pallas_call(
        body, name=name, in_specs=[HBM_SPEC, HBM_SPEC, SEM_SPEC, SEM_SPEC, ANY_SPEC], out_specs=[HBM_SPEC, HBM_SPEC],
        out_shape=[pltpu.HBM(pieces.shape, pieces.dtype), pltpu.HBM(land.shape, land.dtype)],
        input_output_aliases={0: 0, 1: 1},
        compiler_params=pltpu.CompilerParams(has_side_effects=IN_FLIGHT),
    )(pieces, land, send, recv, after)


def _rope_tables(batch, rows_per_example):
    inv_freq = ROPE_THETA ** (-jnp.arange(0, 64, 2, dtype=F32) / 64)
    ang = jnp.arange(rows_per_example, dtype=F32)[:, None] * inv_freq[None, :]
    cos, sin = jnp.cos(ang), jnp.sin(ang)
    cos128 = jnp.concatenate([cos, cos, cos, cos], axis=1)
    sin128 = jnp.concatenate([-sin, sin, -sin, sin], axis=1)
    return jnp.tile(cos128, (batch, 1)), jnp.tile(sin128, (batch, 1))


def _block_diagonal(w):
    eye = jnp.eye(LRU_BLOCKS, dtype=w.dtype)
    return (w[:, :, None, :] * eye[:, None, :, None]).reshape(LRU_W, LRU_W)


def _diagonal_blocks(dense):
    d4 = dense.reshape(LRU_BLOCKS, LRU_BLOCK, LRU_BLOCKS, LRU_BLOCK)
    return jnp.stack([d4[n, :, n, :] for n in range(LRU_BLOCKS)])


def _row(v):
    return v.reshape(1, -1)


def _local_step(x, loss_target, meta_tokens, small, depth, big_weight, on_layer_grads):
    batch, seq, _ = x.shape
    n_real = N_META + seq
    n_blocks = -(-n_real // TIME_BLOCK)
    per_example = n_blocks * TIME_BLOCK
    pad = per_example - n_real
    meta = jnp.broadcast_to(meta_tokens[None], (batch, N_META, D_MODEL))
    h = jnp.concatenate([meta, x, jnp.zeros((batch, pad, D_MODEL), F32)], axis=1).reshape(batch * per_example, D_MODEL)
    rope_cos, rope_sin = _rope_tables(batch, per_example)

    saved = []
    for l in range(depth):
        wa = _block_diagonal(small['gate_a_w'][l]).astype(BF16)
        wx = _block_diagonal(small['gate_x_w'][l]).astype(BF16)
        lru_small = (small['conv_w'][l], _row(small['conv_b'][l]), wa, _row(small['gate_a_b'][l]), wx,
                     _row(small['gate_x_b'][l]), _row(small['lru_lambda'][l]))
        w_in = big_weight('w_in', l, h)
        z1, q, kv, xg = _in_proj_fwd(h, _row(small['pre_mix_norm'][l]), w_in, rope_cos, rope_sin)
        attn, lse = _attn_fwd(q, kv, small['attn_sinks'][l], batch, n_blocks)
        hs, lru = _lru_fwd(xg, *lru_small, batch, n_blocks)
        w_out = big_weight('w_out', l, xg)
        groups, o, h_mid = _out_proj_fwd(attn, lru, _row(small['attn_group_norm'][l]), _row(small['lru_group_norm'][l]),
                                         w_out, h, _row(small['post_mix_norm'][l]))
        w_up4 = big_weight('w_up', l, hs)
        w_down = big_weight('w_down', l, hs)
        z2, slope, act, y, h_out = _mlp_fwd(h_mid, _row(small['pre_mlp_norm'][l]), w_up4, w_down,
                                         _row(small['post_mlp_norm'][l]))
        saved.append(dict(h=h, z1=z1, q=q, kv=kv, xg=xg, attn=attn, lse=lse, hs=hs, lru=lru, groups=groups, o=o,
                          h_mid=h_mid, z2=z2, slope=slope, act=act, y=y, lru_small=lru_small, w_in=w_in, w_out=w_out,
                          w_up4=w_up4, w_down=w_down))
        h = h_out

    dh, sq_err = _loss_head(h.reshape(batch, per_example, D_MODEL), loss_target)
    dh = dh.reshape(batch * per_example, D_MODEL)

    gs = {n: [None] * depth for n in REPLICATED_NAMES + ('conv_w',)}
    handed_over = None
    for l in reversed(range(depth)):
        s = saved[l]

        def ordered_after(gain):
            return gain if handed_over is None else gain + handed_over[0, 0]

        dh_mid, dy, dup, g_mlp = _mlp_bwd(dh, s['h_mid'], s['y'], s['slope'], s['w_up4'], s['w_down'],
                                          _row(small['pre_mlp_norm'][l]), ordered_after(_row(small['post_mlp_norm'][l])))
        handed_over = on_layer_grads(l, {
            'w_down': _matmul_tn(s['act'], dy, 512, D_MODEL, "grad_w_down", BF16),
            'w_up': _matmul_tn(s['z2'], dup, 512, D_MODEL, "grad_w_up", BF16, column_blocks=True)})
        do, d_attn, d_lru, g_out = _out_proj_bwd(dh_mid, s['o'], s['attn'], s['lru'], _row(small['attn_group_norm'][l]),
                                                 _row(small['lru_group_norm'][l]),
                                                 ordered_after(_row(small['post_mix_norm'][l])), s['w_out'])
        dxg, dwa, dwx, g_lru = _lru_bwd(s['xg'], s['hs'], d_lru, *s['lru_small'], batch, n_blocks)
        dq, dkv, dsink = _attn_bwd(s['q'], s['kv'], small['attn_sinks'][l], s['attn'], s['lse'], d_attn, batch, n_blocks)
        dh, dproj, g_in = _in_proj_bwd(dh_mid, s['h'], dq, dkv, dxg, rope_cos, rope_sin, _row(small['pre_mix_norm'][l]),
                                       s['w_in'])
        handed_over = on_layer_grads(l, {
            'w_out': _matmul_tn(s['groups'], do, 512, D_MODEL, "grad_w_out", BF16),
            'w_in': _matmul_tn(dproj, s['z1'], IN_COLS // 2, D_MODEL, "grad_w_in", BF16)})
        gs['pre_mlp_norm'][l], gs['post_mlp_norm'][l] = g_mlp[0], g_mlp[1]
        gs['post_mix_norm'][l] = g_out[0]
        gs['attn_group_norm'][l], gs['lru_group_norm'][l] = g_out[1, :ATTN_W], g_out[1, ATTN_W:]
        gs['pre_mix_norm'][l] = g_in[0]
        gs['attn_sinks'][l] = dsink[:, 0]
        gs['conv_w'][l] = g_lru[:CONV_TAPS]
        gs['conv_b'][l], gs['gate_a_b'][l], gs['gate_x_b'][l], gs['lru_lambda'][l] = g_lru[4], g_lru[5], g_lru[6], g_lru[7]
        gs['gate_a_w'][l] = _diagonal_blocks(dwa)
        gs['gate_x_w'][l] = _diagonal_blocks(dwx)

    grad_x = dh.reshape(batch, per_example, D_MODEL)[:, N_META:n_real]
    grad_meta = _meta_grad(dh, batch, per_example)
    small_grads = {n: jnp.stack(v) for n, v in gs.items()}
    return sq_err, grad_x, grad_meta, small_grads, handed_over


PACK_UNIT = SUBLANES * LANES


def _size(shape):
    size = 1
    for d in shape:
        size *= d
    return size


def _pack(arrays):
    parts = []
    for a in arrays:
        flat = a.reshape(-1)
        padded = -(-flat.shape[0] // PACK_UNIT) * PACK_UNIT
        if padded != flat.shape[0]:
            flat = jnp.pad(flat, (0, padded - flat.shape[0]))
        parts.append(flat.reshape(-1, LANES))
    return jnp.concatenate(parts, axis=0)


def _unpack(buf, shapes):
    out, at = [], 0
    for shp in shapes:
        size = _size(shp)
        rows = -(-size // PACK_UNIT) * SUBLANES
        part = buf[at:at + rows]
        if rows * LANES != size:
            part = part.reshape(-1)[:size]
        out.append(part.reshape(shp))
        at += rows
    return out


def kernel(x, meta_tokens, pre_mix_norm, w_in, attn_sinks, conv_w, conv_b, gate_a_w, gate_a_b, gate_x_w, gate_x_b, lru_lambda, attn_group_norm, lru_group_norm, w_out, post_mix_norm, pre_mlp_norm, w_up, w_down, post_mlp_norm, loss_target, m_meta_tokens, m_pre_mix_norm, m_w_in, m_attn_sinks, m_conv_w, m_conv_b, m_gate_a_w, m_gate_a_b, m_gate_x_w, m_gate_x_b, m_lru_lambda, m_attn_group_norm, m_lru_group_norm, m_w_out, m_post_mix_norm, m_pre_mlp_norm, m_w_up, m_w_down, m_post_mlp_norm, v_meta_tokens, v_pre_mix_norm, v_w_in, v_attn_sinks, v_conv_w, v_conv_b, v_gate_a_w, v_gate_a_b, v_gate_x_w, v_gate_x_b, v_lru_lambda, v_attn_group_norm, v_lru_group_norm, v_w_out, v_post_mix_norm, v_pre_mlp_norm, v_w_up, v_w_down, v_post_mlp_norm):
    given = dict(locals())
    w = {n: given[n] for n in WEIGHT_NAMES}
    m = {n: given['m_' + n] for n in WEIGHT_NAMES}
    v = {n: given['v_' + n] for n in WEIGHT_NAMES}
    depth = w_in.shape[0]
    for d in (w, m, v):
        d['w_in'] = jnp.swapaxes(d['w_in'], 1, 2)
    chip = 2 * lax.axis_index("x") + lax.axis_index("y")
    chip1 = chip.reshape(1).astype(jnp.int32)

    in_flight = []
    all_started = chip1
    for l in range(depth):
        bufs = [_place_shard(w[n], l, chip1, "place_" + n) for n in BIG_NAMES]
        handles, all_started = _gather_start(bufs, all_started, "gather_start_%d" % l, halves=True)
        in_flight.append(dict(zip(BIG_NAMES, handles)))
    first_use = {'w_in': ('w_in',), 'w_out': ('w_out',), 'w_up': ('w_up', 'w_down')}

    def big_weight(name, l, after):
        if l == 0 and name == 'w_in':
            after = all_started
        names = first_use.get(name, ()) if l == 0 else (BIG_NAMES if name == 'w_in' else ())
        if names:
            relayed, _ = _gather_relay([in_flight[l][n] for n in names], after, "gather_relay_%d_%s" % (l, name))
            in_flight[l].update(zip(names, relayed))
        send, recv, buf = in_flight[l][name]
        full = _gather_wait(send, recv, buf, after, "gather_wait_%d_%s" % (l, name), relayed=True)
        if name == 'w_in':
            return full.reshape(IN_COLS, D_MODEL)
        if name == 'w_out':
            return full.reshape(D_MODEL, D_MODEL)
        if name == 'w_down':
            return full.reshape(D_FF, D_MODEL)
        return full

    gathered = _exchange_chips([meta_tokens, conv_w], "gather_small_weights", scatter=False)
    full_meta = jnp.concatenate([gathered[0][s] for s in range(N_CHIPS)], axis=1)
    small = {n: w[n] for n in REPLICATED_NAMES}
    small['conv_w'] = jnp.concatenate([gathered[1][s] for s in range(N_CHIPS)], axis=2)

    scattering = [{} for _ in range(depth)]

    def on_layer_grads(l, big):
        names = list(big)
        pieces = [big[n].reshape(N_CHIPS, -1, D_MODEL) for n in names]
        handles, started = _scatter_start(pieces, "scatter_start_%d_%s" % (l, names[0]))
        scattering[l].update(zip(names, handles))
        return started

    sq_err, grad_x, grad_meta, small_grads, all_handed_over = _local_step(x, loss_target, full_meta, small, depth, big_weight,
                                                                           on_layer_grads)
    loss = lax.psum(sq_err[0, 0] * (0.5 / D_MODEL), ("x", "y", "c"))

    small_names = list(REPLICATED_NAMES) + list(COLUMN_SHARDED_SMALL)
    small_full = dict(small_grads)
    small_full['meta_tokens'] = grad_meta
    device1 = (2 * chip + lax.axis_index("c")).reshape(1).astype(jnp.int32)
    packed = _place_slot(_pack([small_full[n] for n in small_names]), device1, N_DEV, "place_small_grads")
    [small_flight], done = _gather_start([packed], all_handed_over, "gather_small_grads_start", all_devices=True)

    partial = {n: None for n in BIG_NAMES}
    for l in reversed(range(depth)):
        for n in BIG_NAMES:
            send, recv, pieces, land = scattering[l][n]
            pieces, land = _scatter_wait(send, recv, pieces, land, done, "scatter_wait_%d_%s" % (l, n))
            partial[n], done = _sum_pieces(pieces, land, chip1, l, depth, partial[n])
    sibling = _swap_sibling([partial[n] for n in BIG_NAMES], "swap_partial_grads")
    results = {n: _adamw(w[n], m[n], v[n], [partial[n], other], "adamw_" + n) for n, other in zip(BIG_NAMES, sibling)}

    slots = _gather_wait(*small_flight, results['w_down'][0], "gather_small_grads_wait", all_devices=True)
    summed = _unpack(_sum_slots(slots, "sum_small_grads"), [small_full[n].shape for n in small_names])
    grads = dict(zip(small_names, summed))
    grads['meta_tokens'] = lax.dynamic_slice_in_dim(grads['meta_tokens'], chip * meta_tokens.shape[1], meta_tokens.shape[1], 1)
    grads['conv_w'] = lax.dynamic_slice_in_dim(grads['conv_w'], chip * conv_w.shape[2], conv_w.shape[2], 2)

    out_g, out_d, out_m, out_v = {}, {}, {}, {}
    for n in BIG_NAMES:
        out_g[n], out_d[n], out_m[n], out_v[n] = [jnp.swapaxes(r, 1, 2) if n == 'w_in' else r for r in results[n][:4]]
    shapes = [w[n].shape for n in small_names]
    res = _adamw(_pack([w[n] for n in small_names]), _pack([m[n] for n in small_names]), _pack([v[n] for n in small_names]),
                 [_pack([grads[n] for n in small_names])], "adamw_small")
    for k, store in enumerate((out_g, out_d, out_m, out_v)):
        for n, a in zip(small_names, _unpack(res[k], shapes)):
            store[n] = a

    return (loss, grad_x, *[out_g[n] for n in WEIGHT_NAMES], *[out_d[n] for n in WEIGHT_NAMES],
            *[out_m[n] for n in WEIGHT_NAMES], *[out_v[n] for n in WEIGHT_NAMES])
```

```python
import functools

import jax
import jax.numpy as jnp
from jax import lax
from jax.experimental import pallas as pl
from jax.experimental.pallas import tpu as pltpu

F32 = jnp.float32
BF16 = jnp.bfloat16

D_MODEL = 1024
N_HEADS = 8
ATTN_W = 512
KV_W = 128
LRU_W = 512
LRU_BLOCKS = 8
LRU_BLOCK = 64
IN_COLS = 1792
D_FF = 4096
N_META = 16
CONV_TAPS = 4
LRU_C = 8.0
ROPE_THETA = 10000.0
EPS = 1e-6
ATTN_SCALE = 0.125

ADAM_LR = 0.001
ADAM_B1 = 0.9
ADAM_B2 = 0.999
ADAM_EPS = 1e-08
ADAM_WD = 0.01
ADAM_STEP = 10

N_CHIPS = 4
N_DEV = 8
TIME_BLOCK = 128
ROW_TILE = 256
PROJ_ROW_TILE = 544
LANES = 128
SUBLANES = 8
MASKED = -1e30
VMEM_LIMIT = 56 * 1024 * 1024

MESH = pl.DeviceIdType.MESH
HBM_SPEC = pl.BlockSpec(memory_space=pltpu.HBM)

WEIGHT_NAMES = ['meta_tokens', 'pre_mix_norm', 'w_in', 'attn_sinks', 'conv_w', 'conv_b', 'gate_a_w', 'gate_a_b',
                'gate_x_w', 'gate_x_b', 'lru_lambda', 'attn_group_norm', 'lru_group_norm', 'w_out', 'post_mix_norm',
                'pre_mlp_norm', 'w_up', 'w_down', 'post_mlp_norm']
BIG_NAMES = ('w_in', 'w_out', 'w_up', 'w_down')
COLUMN_SHARDED_SMALL = ('meta_tokens', 'conv_w')
REPLICATED_NAMES = tuple(n for n in WEIGHT_NAMES if n not in BIG_NAMES and n not in COLUMN_SHARDED_SMALL)


def _sds(shape, dtype):
    return jax.ShapeDtypeStruct(tuple(shape), dtype)


def _params(*sem):
    return pltpu.CompilerParams(dimension_semantics=sem, vmem_limit_bytes=VMEM_LIMIT)


def _row_spec(width, tile=ROW_TILE):
    return pl.BlockSpec((tile, width), lambda i: (i, 0))


def _proj_tile(rows):
    tile = PROJ_ROW_TILE
    while rows % tile:
        tile -= 16
    return tile


def _whole_spec(a):
    nd = a.ndim
    return pl.BlockSpec(a.shape, lambda *_: (0,) * nd)


def _resident_spec(a):
    nd = a.ndim
    return pl.BlockSpec(a.shape, lambda *_: (0,) * nd, pipeline_mode=pl.Buffered(1))


def _rms(x, g):
    r = lax.rsqrt(jnp.mean(x * x, axis=-1, keepdims=True) + EPS)
    return x * r * g


def _rms_bwd(dy, x, g):
    r = lax.rsqrt(jnp.mean(x * x, axis=-1, keepdims=True) + EPS)
    xh = x * r
    dg = jnp.sum(dy * xh, axis=0, keepdims=True)
    dxh = dy * g
    dx = r * (dxh - xh * jnp.mean(dxh * xh, axis=-1, keepdims=True))
    return dx, dg


def _rope(x, cos, sin_signed):
    width = x.shape[1]
    reps = width // LANES
    if reps > 1:
        cos = jnp.tile(cos, (1, reps))
        sin_signed = jnp.tile(sin_signed, (1, reps))
    lane = lax.broadcasted_iota(jnp.int32, x.shape, 1)
    first_half = jnp.bitwise_and(lane, 32) == 0
    other = jnp.where(first_half, pltpu.roll(x, width - 32, 1), pltpu.roll(x, 32, 1))
    return x * cos + other * sin_signed


def _sigmoid(x):
    return 1.0 / (1.0 + jnp.exp(-x))


def _log1p(e):
    return jnp.where(e < 1e-3, e * (1.0 - e * (0.5 - e * (1.0 / 3.0))), jnp.log(1.0 + e))


def _one_minus_square(a, log_a):
    x = 2.0 * log_a
    return jnp.where(x > -0.002, x * (-1.0 - 0.5 * x), 1.0 - a * a)


GELU_K = 0.7978845608028654
GELU_C = 0.044715


def _gelu(x):
    t = jnp.tanh(GELU_K * (x + GELU_C * x * x * x))
    return 0.5 * x * (1.0 + t)


def _gelu_and_grad(x):
    x2 = x * x
    t = jnp.tanh(GELU_K * (x + GELU_C * x * x2))
    val = 0.5 * x * (1.0 + t)
    grad = 0.5 * (1.0 + t) + 0.5 * x * (1.0 - t * t) * GELU_K * (1.0 + 3.0 * GELU_C * x2)
    return val, grad


def _dot(a, b):
    return jnp.dot(a, b, preferred_element_type=F32)


def _dot_nt(a, b):
    return lax.dot_general(a, b, (((1,), (1,)), ((), ())), preferred_element_type=F32)


def _dot_tn(a, b):
    return lax.dot_general(a, b, (((0,), (0,)), ((), ())), preferred_element_type=F32)


def _put_rows(rows_8, values):
    d = values[0].shape[1]
    rowid = lax.broadcasted_iota(jnp.int32, (rows_8, d), 0)
    out = jnp.zeros((rows_8, d), F32)
    for k, v in enumerate(values):
        out = out + jnp.where(rowid == k, v, 0.0)
    return out


def _in_proj_fwd(h, gain, w_in, rope_cos, rope_sin):
    rows = h.shape[0]

    def body(h_ref, g_ref, w_ref, c_ref, s_ref, z_ref, q_ref, kv_ref, xg_ref):
        z = _rms(h_ref[...], g_ref[...]).astype(BF16)
        z_ref[...] = z
        proj = _dot_nt(z, w_ref[...])
        cos = c_ref[...]
        sin = s_ref[...]
        q_ref[...] = (_rope(proj[:, :ATTN_W], cos, sin) * ATTN_SCALE).astype(BF16)
        kv_ref[:, :KV_W] = _rope(proj[:, ATTN_W:ATTN_W + KV_W], cos, sin).astype(BF16)
        kv_ref[:, KV_W:] = proj[:, ATTN_W + KV_W:ATTN_W + 2 * KV_W].astype(BF16)
        xg_ref[...] = proj[:, ATTN_W + 2 * KV_W:]

    tile = _proj_tile(rows)
    rs = functools.partial(_row_spec, tile=tile)
    return pl.pallas_call(
        body, name="in_proj_fwd", grid=(rows // tile,),
        in_specs=[rs(D_MODEL), _whole_spec(gain), _whole_spec(w_in), rs(LANES), rs(LANES)],
        out_specs=[rs(D_MODEL), rs(ATTN_W), rs(2 * KV_W), rs(2 * LRU_W)],
        out_shape=[_sds((rows, D_MODEL), BF16), _sds((rows, ATTN_W), BF16), _sds((rows, 2 * KV_W), BF16),
                   _sds((rows, 2 * LRU_W), F32)],
        compiler_params=_params("parallel"),
    )(h, gain, w_in, rope_cos, rope_sin)


def _in_proj_bwd(dh_mid, h, dq, dkv, dxg, rope_cos, rope_sin, gain, w_in):
    rows = h.shape[0]

    def body(dhm_ref, h_ref, dq_ref, dkv_ref, dxg_ref, c_ref, s_ref, g_ref, w_ref, dh_ref, dp_ref, gacc_ref):
        @pl.when(pl.program_id(0) == 0)
        def _():
            gacc_ref[...] = jnp.zeros_like(gacc_ref)

        cos = c_ref[...]
        sin = -s_ref[...]
        dp_ref[:, :ATTN_W] = (_rope(dq_ref[...], cos, sin) * ATTN_SCALE).astype(BF16)
        dp_ref[:, ATTN_W:ATTN_W + KV_W] = _rope(dkv_ref[:, :KV_W], cos, sin).astype(BF16)
        dp_ref[:, ATTN_W + KV_W:ATTN_W + 2 * KV_W] = dkv_ref[:, KV_W:].astype(BF16)
        dp_ref[:, ATTN_W + 2 * KV_W:] = dxg_ref[...]
        dz = _dot(dp_ref[...], w_ref[...])
        dx, dg = _rms_bwd(dz, h_ref[...], g_ref[...])
        dh_ref[...] = dhm_ref[...] + dx
        gacc_ref[...] += _put_rows(SUBLANES, [dg])

    tile = _proj_tile(rows)
    rs = functools.partial(_row_spec, tile=tile)
    return pl.pallas_call(
        body, name="in_proj_bwd", grid=(rows // tile,),
        in_specs=[rs(D_MODEL), rs(D_MODEL), rs(ATTN_W), rs(2 * KV_W), rs(2 * LRU_W),
                  rs(LANES), rs(LANES), _whole_spec(gain), _whole_spec(w_in)],
        out_specs=[rs(D_MODEL), rs(IN_COLS), pl.BlockSpec((SUBLANES, D_MODEL), lambda i: (0, 0))],
        out_shape=[_sds((rows, D_MODEL), F32), _sds((rows, IN_COLS), BF16), _sds((SUBLANES, D_MODEL), F32)],
        compiler_params=_params("arbitrary"),
    )(dh_mid, h, dq, dkv, dxg, rope_cos, rope_sin, gain, w_in)


def _kv_lane_variants(t, group):
    lane = lax.broadcasted_iota(jnp.int32, t.shape, 1)
    low = lane < 64
    swapped = pltpu.roll(t, 64, 1)
    if group == 0:
        lo, hi = jnp.where(low, t, 0.0), jnp.where(low, 0.0, swapped)
    else:
        lo, hi = jnp.where(low, swapped, 0.0), jnp.where(low, 0.0, t)
    return jnp.concatenate([lo, hi], axis=0).astype(BF16)


GROUP_ROWS = 2 * TIME_BLOCK
KEYS = 2 * TIME_BLOCK


def _window_mask(j):
    r = jnp.bitwise_and(lax.broadcasted_iota(jnp.int32, (GROUP_ROWS, KEYS), 0), TIME_BLOCK - 1)
    c = lax.broadcasted_iota(jnp.int32, (GROUP_ROWS, KEYS), 1)
    return (c > r) & (c <= r + TIME_BLOCK) & ((c >= TIME_BLOCK) | (j > 0))


def _group_rows(ref, ex, group):
    lo = 2 * group * LANES
    return jnp.concatenate([ref[ex, :, lo:lo + LANES], ref[ex, :, lo + LANES:lo + 2 * LANES]], axis=0)


def _per_head(sink_ref, group, half):
    upper = lax.broadcasted_iota(jnp.int32, (GROUP_ROWS, 1), 0) < TIME_BLOCK
    return jnp.where(upper, sink_ref[4 * group + half], sink_ref[4 * group + 2 + half])


def _by_example(a, batch):
    return a.reshape(batch, a.shape[0] // batch, a.shape[1])


def _attn_fwd(q, kv, sinks, batch, n_blocks):
    rows = q.shape[0]

    def body(sink_ref, q_ref, kvc_ref, kvp_ref, o_ref, lse_ref):
        j = pl.program_id(0)
        mask = _window_mask(j)
        lane8 = lax.broadcasted_iota(jnp.int32, (TIME_BLOCK, N_HEADS), 1)
        for ex in range(batch):
            kv2 = jnp.concatenate([kvp_ref[ex], kvc_ref[ex]], axis=0).astype(F32)
            lse_tile = jnp.zeros((TIME_BLOCK, N_HEADS), F32)
            for group in range(2):
                k_cat = _kv_lane_variants(kv2[:, :KV_W], group)
                v_cat = _kv_lane_variants(kv2[:, KV_W:], group)
                s_all = _dot_nt(_group_rows(q_ref, ex, group), k_cat)
                probs = []
                for half in range(2):
                    sink = _per_head(sink_ref, group, half)
                    s = jnp.where(mask, s_all[:, half * KEYS:(half + 1) * KEYS], MASKED)
                    m = jnp.maximum(jnp.max(s, axis=1, keepdims=True), sink)
                    e = jnp.exp(s - m)
                    den = jnp.sum(e, axis=1, keepdims=True) + jnp.exp(sink - m)
                    probs.append((e / den).astype(BF16))
                    lse = m + jnp.log(den)
                    lse_tile = lse_tile + jnp.where(lane8 == 4 * group + half, lse[:TIME_BLOCK], 0.0)
                    lse_tile = lse_tile + jnp.where(lane8 == 4 * group + 2 + half, lse[TIME_BLOCK:], 0.0)
                out = _dot(jnp.concatenate(probs, axis=1), v_cat)
                o_ref[ex, :, 2 * group * LANES:(2 * group + 1) * LANES] = out[:TIME_BLOCK]
                o_ref[ex, :, (2 * group + 1) * LANES:(2 * group + 2) * LANES] = out[TIME_BLOCK:]
            lse_ref[ex] = lse_tile

    def blk(width):
        return pl.BlockSpec((batch, TIME_BLOCK, width), lambda j: (0, j, 0))

    prev = pl.BlockSpec((batch, TIME_BLOCK, 2 * KV_W), lambda j: (0, jnp.maximum(j - 1, 0), 0))
    kv3 = _by_example(kv, batch)
    out, lse = pl.pallas_call(
        body, name="attn_fwd", grid=(n_blocks,),
        in_specs=[pl.BlockSpec(memory_space=pltpu.SMEM), blk(ATTN_W), blk(2 * KV_W), prev],
        out_specs=[blk(ATTN_W), blk(N_HEADS)],
        out_shape=[_sds((batch, rows // batch, ATTN_W), F32), _sds((batch, rows // batch, N_HEADS), F32)],
        compiler_params=_params("parallel"),
    )(sinks, _by_example(q, batch), kv3, kv3)
    return out.reshape(rows, ATTN_W), lse.reshape(rows, N_HEADS)


def _attn_bwd(q, kv, sinks, out, lse, d_out, batch, n_blocks):
    rows = q.shape[0]

    def body(sink_ref, q_ref, kvc_ref, kvp_ref, o_ref, do_ref, lse_ref, dq_ref, dkv_ref, dsink_ref, carry):
        j = pl.program_id(0)

        @pl.when(j == 0)
        def _():
            dsink_ref[...] = jnp.zeros_like(dsink_ref)

        def one_example(ex, dsink_vals):
            kv2 = jnp.concatenate([kvp_ref[ex], kvc_ref[ex]], axis=0).astype(F32)
            mask = _window_mask(j)
            lane = lax.broadcasted_iota(jnp.int32, (GROUP_ROWS, LANES), 1)
            low = lax.broadcasted_iota(jnp.int32, (KEYS, LANES), 1) < 64
            upper = lax.broadcasted_iota(jnp.int32, (GROUP_ROWS, 1), 0) < TIME_BLOCK
            lse_tile = lse_ref[ex]
            dk_tile = jnp.zeros((KEYS, KV_W), F32)
            dv_tile = jnp.zeros((KEYS, KV_W), F32)
            for group in range(2):
                k_cat = _kv_lane_variants(kv2[:, :KV_W], group)
                v_cat = _kv_lane_variants(kv2[:, KV_W:], group)
                q_rows = _group_rows(q_ref, ex, group)
                do_rows = _group_rows(do_ref, ex, group)
                do_b = do_rows.astype(BF16)
                od = do_rows * _group_rows(o_ref, ex, group)
                s_all = _dot_nt(q_rows, k_cat)
                dp_all = _dot_nt(do_b, v_cat)
                probs, dss = [], []
                for half in range(2):
                    heads = (4 * group + half, 4 * group + 2 + half)
                    sink = _per_head(sink_ref, group, half)
                    lse_h = jnp.concatenate([lse_tile[:, h:h + 1] for h in heads], axis=0)
                    in_half = (lane < 64) if half == 0 else (lane >= 64)
                    delta = jnp.sum(jnp.where(in_half, od, 0.0), axis=1, keepdims=True)
                    cols = slice(half * KEYS, (half + 1) * KEYS)
                    prob = jnp.exp(jnp.where(mask, s_all[:, cols], MASKED) - lse_h)
                    probs.append(prob.astype(BF16))
                    dss.append((prob * (dp_all[:, cols] - delta)).astype(BF16))
                    dsink = -jnp.exp(sink - lse_h) * delta
                    dsink_vals[heads[0]] = dsink_vals[heads[0]] + jnp.sum(jnp.where(upper, dsink, 0.0), axis=0, keepdims=True)
                    dsink_vals[heads[1]] = dsink_vals[heads[1]] + jnp.sum(jnp.where(upper, 0.0, dsink), axis=0, keepdims=True)
                ds = jnp.concatenate(dss, axis=1)
                dq_rows = _dot(ds, k_cat)
                dq_ref[ex, :, 2 * group * LANES:(2 * group + 1) * LANES] = dq_rows[:TIME_BLOCK]
                dq_ref[ex, :, (2 * group + 1) * LANES:(2 * group + 2) * LANES] = dq_rows[TIME_BLOCK:]
                dk_cat = _dot_tn(ds, q_rows)
                dv_cat = _dot_tn(jnp.concatenate(probs, axis=1), do_b)
                if group == 0:
                    dk_tile = dk_tile + jnp.where(low, dk_cat[:KEYS] + pltpu.roll(dk_cat[KEYS:], 64, 1), 0.0)
                    dv_tile = dv_tile + jnp.where(low, dv_cat[:KEYS] + pltpu.roll(dv_cat[KEYS:], 64, 1), 0.0)
                else:
                    dk_tile = dk_tile + jnp.where(low, 0.0, pltpu.roll(dk_cat[:KEYS], 64, 1) + dk_cat[KEYS:])
                    dv_tile = dv_tile + jnp.where(low, 0.0, pltpu.roll(dv_cat[:KEYS], 64, 1) + dv_cat[KEYS:])

            @pl.when(j > 0)
            def _():
                dkv_ref[ex, :, :KV_W] = carry[ex, :, :KV_W] + dk_tile[:TIME_BLOCK]
                dkv_ref[ex, :, KV_W:] = carry[ex, :, KV_W:] + dv_tile[:TIME_BLOCK]

            carry[ex, :, :KV_W] = dk_tile[TIME_BLOCK:]
            carry[ex, :, KV_W:] = dv_tile[TIME_BLOCK:]

        @pl.when(j < n_blocks)
        def _():
            dsink_vals = {head: jnp.zeros((1, 1), F32) for head in range(N_HEADS)}
            for ex in range(batch):
                one_example(ex, dsink_vals)
            rowid = lax.broadcasted_iota(jnp.int32, (N_HEADS, LANES), 0)
            upd = jnp.zeros((N_HEADS, LANES), F32)
            for head, val in dsink_vals.items():
                upd = upd + jnp.where(rowid == head, val, 0.0)
            dsink_ref[...] += upd

        @pl.when(j == n_blocks)
        def _():
            dkv_ref[...] = carry[...]

    last = n_blocks - 1

    def blk(width):
        return pl.BlockSpec((batch, TIME_BLOCK, width), lambda j: (0, jnp.minimum(j, last), 0))

    prev = pl.BlockSpec((batch, TIME_BLOCK, 2 * KV_W), lambda j: (0, jnp.maximum(jnp.minimum(j, last) - 1, 0), 0))
    dkv_spec = pl.BlockSpec((batch, TIME_BLOCK, 2 * KV_W), lambda j: (0, jnp.maximum(j - 1, 0), 0))
    kv3 = _by_example(kv, batch)
    dq, dkv, dsink = pl.pallas_call(
        body, name="attn_bwd", grid=(n_blocks + 1,),
        in_specs=[pl.BlockSpec(memory_space=pltpu.SMEM), blk(ATTN_W), blk(2 * KV_W), prev, blk(ATTN_W), blk(ATTN_W),
                  blk(N_HEADS)],
        out_specs=[blk(ATTN_W), dkv_spec, pl.BlockSpec((N_HEADS, LANES), lambda j: (0, 0))],
        out_shape=[_sds((batch, rows // batch, ATTN_W), F32), _sds((batch, rows // batch, 2 * KV_W), F32),
                   _sds((N_HEADS, LANES), F32)],
        scratch_shapes=[pltpu.VMEM((batch, TIME_BLOCK, 2 * KV_W), F32)],
        compiler_params=_params("arbitrary"),
    )(sinks, _by_example(q, batch), kv3, kv3, _by_example(out, batch), _by_example(d_out, batch), _by_example(lse, batch))
    return dq.reshape(rows, ATTN_W), dkv.reshape(rows, 2 * KV_W), dsink


def _conv_taps(xb, prev8):
    ext = jnp.concatenate([prev8, xb], axis=0)
    n = ext.shape[0]
    return [xb] + [pltpu.roll(ext, k, 0)[SUBLANES:n] for k in range(1, CONV_TAPS)]


def _lru_gates(xc, wa, ba, wx, bx, lam):
    xcb = xc.astype(BF16)
    r = _sigmoid(_dot(xcb, wa) + ba)
    i = _sigmoid(_dot(xcb, wx) + bx)
    sp = jnp.maximum(-lam, 0.0) + _log1p(jnp.exp(-jnp.abs(lam)))
    log_a = -LRU_C * r * sp
    a = jnp.exp(log_a)
    mult = jnp.sqrt(_one_minus_square(a, log_a))
    return xcb, r, i, sp, a, mult


def _scan_fwd(a, u, h_before):
    n, d = a.shape
    groups = n // SUBLANES
    a = a.reshape(groups, SUBLANES, d)
    u = u.reshape(groups, SUBLANES, d)
    sub = lax.broadcasted_iota(jnp.int32, a.shape, 1)
    s = 1
    while s < SUBLANES:
        valid = sub >= s
        u = jnp.where(valid, u + a * pltpu.roll(u, s, 1), u)
        a = jnp.where(valid, a * pltpu.roll(a, s, 1), a)
        s *= 2
    out, prev = [], h_before
    for g in range(groups):
        out.append(u[g] + a[g] * prev)
        prev = out[-1][SUBLANES - 1:SUBLANES, :]
    return jnp.concatenate(out, axis=0)


def _scan_rev(cf, g, d_after):
    n, d = g.shape
    groups = n // SUBLANES
    cf = cf.reshape(groups, SUBLANES, d)
    g = g.reshape(groups, SUBLANES, d)
    sub = lax.broadcasted_iota(jnp.int32, g.shape, 1)
    s = 1
    while s < SUBLANES:
        valid = sub + s < SUBLANES
        g = jnp.where(valid, g + cf * pltpu.roll(g, SUBLANES - s, 1), g)
        cf = jnp.where(valid, cf * pltpu.roll(cf, SUBLANES - s, 1), cf)
        s *= 2
    out, nxt = [None] * groups, d_after
    for k in reversed(range(groups)):
        out[k] = g[k] + cf[k] * nxt
        nxt = out[k][0:1, :]
    return jnp.concatenate(out, axis=0)


def _lru_fwd(xg, conv_w, conv_b, wa, ba, wx, bx, lam, batch, n_blocks):
    rows = xg.shape[0]

    def body(xg_ref, cw_ref, cb_ref, wa_ref, ba_ref, wx_ref, bx_ref, lam_ref, hs_ref, lru_ref, x_prev, h_carry):
        @pl.when(pl.program_id(0) == 0)
        def _():
            x_prev[...] = jnp.zeros_like(x_prev)
            h_carry[...] = jnp.zeros_like(h_carry)

        for ex in range(batch):
            xb = xg_ref[ex, :, :LRU_W]
            taps = _conv_taps(xb, x_prev[ex])
            xc = cb_ref[...] + sum(cw_ref[CONV_TAPS - 1 - k:CONV_TAPS - k, :] * taps[k] for k in range(CONV_TAPS))
            _, _, i, _, a, mult = _lru_gates(xc, wa_ref[...], ba_ref[...], wx_ref[...], bx_ref[...], lam_ref[...])
            h = _scan_fwd(a, mult * i * xc, h_carry[ex])
            hs_ref[ex] = h
            lru_ref[ex] = h * _gelu(xg_ref[ex, :, LRU_W:])
            h_carry[ex] = h[TIME_BLOCK - 1:TIME_BLOCK, :]
            x_prev[ex] = xb[TIME_BLOCK - SUBLANES:TIME_BLOCK, :]

    def blk(width):
        return pl.BlockSpec((batch, TIME_BLOCK, width), lambda j: (0, j, 0))

    small = [conv_w, conv_b, wa, ba, wx, bx, lam]
    hs, lru = pl.pallas_call(
        body, name="lru_fwd", grid=(n_blocks,),
        in_specs=[blk(2 * LRU_W)] + [_whole_spec(a) for a in small],
        out_specs=[blk(LRU_W), blk(LRU_W)],
        out_shape=[_sds((batch, rows // batch, LRU_W), F32)] * 2,
        scratch_shapes=[pltpu.VMEM((batch, SUBLANES, LRU_W), F32), pltpu.VMEM((batch, 1, LRU_W), F32)],
        compiler_params=_params("arbitrary"),
    )(_by_example(xg, batch), *small)
    return hs.reshape(rows, LRU_W), lru.reshape(rows, LRU_W)


def _lru_bwd(xg, hs, d_lru, conv_w, conv_b, wa, ba, wx, bx, lam, batch, n_blocks):
    rows = xg.shape[0]
    last_row = TIME_BLOCK - 1

    def body(xg_ref, xgh_ref, hs_ref, hsh_ref, dl_ref, cw_ref, cb_ref, wa_ref, ba_ref, wx_ref, bx_ref, lam_ref,
             dxg_ref, dwa_ref, dwx_ref, vec_ref, dh_carry, dxc_next):
        j = pl.program_id(0)

        @pl.when(j == 0)
        def _():
            dwa_ref[...] = jnp.zeros_like(dwa_ref)
            dwx_ref[...] = jnp.zeros_like(dwx_ref)
            vec_ref[...] = jnp.zeros_like(vec_ref)
            dh_carry[...] = jnp.zeros_like(dh_carry)
            dxc_next[...] = jnp.zeros_like(dxc_next)

        for ex in range(batch):
            one_example(ex, j, xg_ref, xgh_ref, hs_ref, hsh_ref, dl_ref, cw_ref, cb_ref, wa_ref, ba_ref, wx_ref, bx_ref,
                        lam_ref, dxg_ref, dwa_ref, dwx_ref, vec_ref, dh_carry, dxc_next)

    def one_example(ex, j, xg_ref, xgh_ref, hs_ref, hsh_ref, dl_ref, cw_ref, cb_ref, wa_ref, ba_ref, wx_ref, bx_ref, lam_ref,
                    dxg_ref, dwa_ref, dwx_ref, vec_ref, dh_carry, dxc_next):
        first = j == n_blocks - 1
        xb = xg_ref[ex, :, :LRU_W]
        prev8 = jnp.where(first, 0.0, xgh_ref[ex, :, :LRU_W])
        h_before = jnp.where(first, 0.0, hsh_ref[ex, SUBLANES - 1:SUBLANES, :])
        cw = cw_ref[...]
        lam = lam_ref[...]
        wa = wa_ref[...]
        wx = wx_ref[...]
        taps = _conv_taps(xb, prev8)
        xc = cb_ref[...] + sum(cw[CONV_TAPS - 1 - k:CONV_TAPS - k, :] * taps[k] for k in range(CONV_TAPS))
        xcb, r, i, sp, a, mult = _lru_gates(xc, wa, ba_ref[...], wx, bx_ref[...], lam)
        hs = hs_ref[ex]
        row = lax.broadcasted_iota(jnp.int32, hs.shape, 0)
        h_prev = jnp.where(row == 0, h_before, pltpu.roll(hs, 1, 0))
        dl = dl_ref[ex]
        gate, dgate = _gelu_and_grad(xg_ref[ex, :, LRU_W:])
        dxg_ref[ex, :, LRU_W:] = (dl * hs * dgate).astype(BF16)
        cf = jnp.where(row == last_row, 1.0, pltpu.roll(a, last_row, 0))
        dh = _scan_rev(cf, dl * gate, dh_carry[ex])
        dh_carry[ex] = a[0:1, :] * dh[0:1, :]
        dmult = dh * i * xc
        di = dh * mult * xc
        dxc = dh * mult * i
        dlog_a = dh * h_prev * a - dmult * (a * a / mult)
        dr = dlog_a * (-LRU_C * sp)
        dlam = jnp.sum(dlog_a * (-LRU_C * r), axis=0, keepdims=True) * (-_sigmoid(-lam))
        dpr = dr * r * (1.0 - r)
        dpi = di * i * (1.0 - i)
        dprb = dpr.astype(BF16)
        dpib = dpi.astype(BF16)
        dxc = dxc + _dot_nt(dprb, wa) + _dot_nt(dpib, wx)
        dwa_ref[...] += _dot_tn(xcb, dprb)
        dwx_ref[...] += _dot_tn(xcb, dpib)
        ext = jnp.concatenate([dxc, dxc_next[ex]], axis=0)
        n = ext.shape[0]
        dxb = cw[CONV_TAPS - 1:CONV_TAPS, :] * dxc
        for k in range(1, CONV_TAPS):
            dxb = dxb + cw[CONV_TAPS - 1 - k:CONV_TAPS - k, :] * pltpu.roll(ext, n - k, 0)[:TIME_BLOCK]
        dxg_ref[ex, :, :LRU_W] = dxb.astype(BF16)
        dxc_next[ex] = dxc[:SUBLANES, :]
        vecs = [jnp.sum(dxc * taps[CONV_TAPS - 1 - t], axis=0, keepdims=True) for t in range(CONV_TAPS)]
        vecs += [jnp.sum(dxc, axis=0, keepdims=True), jnp.sum(dpr, axis=0, keepdims=True),
                 jnp.sum(dpi, axis=0, keepdims=True), dlam]
        vec_ref[...] += _put_rows(SUBLANES, vecs)

    def tblk(j):
        return n_blocks - 1 - j

    def blk(width):
        return pl.BlockSpec((batch, TIME_BLOCK, width), lambda j: (0, tblk(j), 0))

    per8 = TIME_BLOCK // SUBLANES

    def halo(width):
        return pl.BlockSpec((batch, SUBLANES, width), lambda j: (0, jnp.maximum(per8 * tblk(j) - 1, 0), 0))

    small = [conv_w, conv_b, wa, ba, wx, bx, lam]
    acc = lambda shape: pl.BlockSpec(shape, lambda j: (0, 0))
    xg3, hs3 = _by_example(xg, batch), _by_example(hs, batch)
    dxg, dwa, dwx, vec = pl.pallas_call(
        body, name="lru_bwd", grid=(n_blocks,),
        in_specs=[blk(2 * LRU_W), halo(2 * LRU_W), blk(LRU_W), halo(LRU_W), blk(LRU_W)] + [_whole_spec(a) for a in small],
        out_specs=[blk(2 * LRU_W), acc((LRU_W, LRU_W)), acc((LRU_W, LRU_W)), acc((SUBLANES, LRU_W))],
        out_shape=[_sds((batch, rows // batch, 2 * LRU_W), BF16), _sds((LRU_W, LRU_W), F32), _sds((LRU_W, LRU_W), F32),
                   _sds((SUBLANES, LRU_W), F32)],
        scratch_shapes=[pltpu.VMEM((batch, 1, LRU_W), F32), pltpu.VMEM((batch, SUBLANES, LRU_W), F32)],
        compiler_params=_params("arbitrary"),
    )(xg3, xg3, hs3, hs3, _by_example(d_lru, batch), *small)
    return dxg.reshape(rows, 2 * LRU_W), dwa, dwx, vec


def _out_mlp_fwd(attn, lru, h, w_out, w_up4, w_down, g_attn, g_lru, g_post_mix, g_pre_mlp, g_post_mlp):
    rows = h.shape[0]

    def body(at_ref, lr_ref, h_ref, wo_ref, wu_ref, wd_ref, ga_ref, gl_ref, gp_ref, g1_ref, g2_ref,
             grp_ref, o_ref, hm_ref, z_ref, slope_ref, act_ref, y_ref, ho_ref):
        a = _rms(at_ref[...], ga_ref[...]).astype(BF16)
        l = _rms(lr_ref[...], gl_ref[...]).astype(BF16)
        grp_ref[:, :ATTN_W] = a
        grp_ref[:, ATTN_W:] = l
        o = _dot(a, wo_ref[:ATTN_W, :]) + _dot(l, wo_ref[ATTN_W:, :])
        o_ref[...] = o
        x = h_ref[...] + _rms(o, gp_ref[...])
        hm_ref[...] = x
        z = _rms(x, g1_ref[...]).astype(BF16)
        z_ref[...] = z
        y = jnp.zeros((ROW_TILE, D_MODEL), F32)
        for s in range(N_CHIPS):
            cols = slice(s * D_MODEL, (s + 1) * D_MODEL)
            r = jnp.maximum(_dot(z, wu_ref[s]), 0.0)
            slope_ref[:, cols] = (2.0 * r).astype(BF16)
            act = jnp.square(r).astype(BF16)
            act_ref[:, cols] = act
            y = y + _dot(act, wd_ref[cols, :])
        y_ref[...] = y
        ho_ref[...] = x + _rms(y, g2_ref[...])

    gains = [g_attn, g_lru, g_post_mix, g_pre_mlp, g_post_mlp]
    return pl.pallas_call(
        body, name="out_mlp_fwd", grid=(rows // ROW_TILE,),
        in_specs=[_row_spec(ATTN_W), _row_spec(LRU_W), _row_spec(D_MODEL), _resident_spec(w_out), _resident_spec(w_up4),
                  _resident_spec(w_down)] + [_whole_spec(g) for g in gains],
        out_specs=[_row_spec(D_MODEL), _row_spec(D_MODEL), _row_spec(D_MODEL), _row_spec(D_MODEL), _row_spec(D_FF),
                   _row_spec(D_FF), _row_spec(D_MODEL), _row_spec(D_MODEL)],
        out_shape=[_sds((rows, D_MODEL), BF16), _sds((rows, D_MODEL), F32), _sds((rows, D_MODEL), F32),
                   _sds((rows, D_MODEL), BF16), _sds((rows, D_FF), BF16), _sds((rows, D_FF), BF16),
                   _sds((rows, D_MODEL), F32), _sds((rows, D_MODEL), F32)],
        compiler_params=_params("parallel"),
    )(attn, lru, h, w_out, w_up4, w_down, *gains)


def _mlp_out_bwd(dh_out, h_mid, y, slope, o, attn, lru, w_out, w_up4, w_down, g_attn, g_lru, g_post_mix, g_pre_mlp,
                 g_post_mlp):
    rows = y.shape[0]

    def body(dh_ref, hm_ref, y_ref, slope_ref, o_ref, at_ref, lr_ref, wo_ref, wu_ref, wd_ref, ga_ref, gl_ref, gp_ref,
             g1_ref, g2_ref, dhm_ref, dy_ref, dup_ref, do_ref, dat_ref, dlr_ref, gacc_ref):
        @pl.when(pl.program_id(0) == 0)
        def _():
            gacc_ref[...] = jnp.zeros_like(gacc_ref)

        dh = dh_ref[...]
        dy, dg2 = _rms_bwd(dh, y_ref[...], g2_ref[...])
        dyb = dy.astype(BF16)
        dy_ref[...] = dyb
        dz = jnp.zeros((ROW_TILE, D_MODEL), F32)
        for s in range(N_CHIPS):
            cols = slice(s * D_MODEL, (s + 1) * D_MODEL)
            dact = _dot_nt(dyb, wd_ref[cols, :])
            dup = (dact * slope_ref[:, cols].astype(F32)).astype(BF16)
            dup_ref[:, cols] = dup
            dz = dz + _dot_nt(dup, wu_ref[s])
        dx, dg1 = _rms_bwd(dz, hm_ref[...], g1_ref[...])
        dhm = dh + dx
        dhm_ref[...] = dhm
        do, dgp = _rms_bwd(dhm, o_ref[...], gp_ref[...])
        dob = do.astype(BF16)
        do_ref[...] = dob
        dat, dga = _rms_bwd(_dot_nt(dob, wo_ref[:ATTN_W, :]), at_ref[...], ga_ref[...])
        dlr, dgl = _rms_bwd(_dot_nt(dob, wo_ref[ATTN_W:, :]), lr_ref[...], gl_ref[...])
        dat_ref[...] = dat
        dlr_ref[...] = dlr
        gacc_ref[...] += _put_rows(SUBLANES, [dg1, dg2, dgp, jnp.concatenate([dga, dgl], axis=1)])

    gains = [g_attn, g_lru, g_post_mix, g_pre_mlp, g_post_mlp]
    return pl.pallas_call(
        body, name="mlp_out_bwd", grid=(rows // ROW_TILE,),
        in_specs=[_row_spec(D_MODEL), _row_spec(D_MODEL), _row_spec(D_MODEL), _row_spec(D_FF), _row_spec(D_MODEL),
                  _row_spec(ATTN_W), _row_spec(LRU_W), _resident_spec(w_out), _resident_spec(w_up4), _resident_spec(w_down)]
                 + [_whole_spec(g) for g in gains],
        out_specs=[_row_spec(D_MODEL), _row_spec(D_MODEL), _row_spec(D_FF), _row_spec(D_MODEL), _row_spec(ATTN_W),
                   _row_spec(LRU_W), pl.BlockSpec((SUBLANES, D_MODEL), lambda i: (0, 0))],
        out_shape=[_sds((rows, D_MODEL), F32), _sds((rows, D_MODEL), BF16), _sds((rows, D_FF), BF16),
                   _sds((rows, D_MODEL), BF16), _sds((rows, ATTN_W), F32), _sds((rows, LRU_W), F32),
                   _sds((SUBLANES, D_MODEL), F32)],
        compiler_params=_params("arbitrary"),
    )(dh_out, h_mid, y, slope, o, attn, lru, w_out, w_up4, w_down, *gains)


def _matmul_tn(a, b, tm, tn, name, out_dtype, column_blocks=False):
    rows, m = a.shape
    n = b.shape[1]

    def body(a_ref, b_ref, o_ref):
        o_ref[...] = _dot_tn(a_ref[...], b_ref[...]).astype(out_dtype)

    if column_blocks:
        out_spec = pl.BlockSpec((None, tm, tn), lambda i, j: (j, i, 0))
        out_shape = _sds((n // tn, m, tn), out_dtype)
    else:
        out_spec = pl.BlockSpec((tm, tn), lambda i, j: (i, j))
        out_shape = _sds((m, n), out_dtype)
    return pl.pallas_call(
        body, name=name, grid=(m // tm, n // tn),
        in_specs=[pl.BlockSpec((rows, tm), lambda i, j: (0, i)), pl.BlockSpec((rows, tn), lambda i, j: (0, j))],
        out_specs=out_spec, out_shape=out_shape,
        compiler_params=_params("parallel", "parallel"),
    )(a, b)


LOSS_COLS = 256


def _loss_head(h, loss_target):
    batch, per_example, _ = h.shape
    seq = loss_target.shape[1]
    n_real = N_META + seq
    assert seq % SUBLANES == 0

    def body(h_ref, t_ref, dh_ref, l_ref):
        @pl.when((pl.program_id(0) == 0) & (pl.program_id(1) == 0))
        def _():
            l_ref[...] = jnp.zeros_like(l_ref)

        d = h_ref[N_META:n_real, :] - t_ref[...]
        dh_ref[:N_META, :] = jnp.zeros((N_META, LOSS_COLS), F32)
        dh_ref[N_META:n_real, :] = d * (1.0 / D_MODEL)
        dh_ref[n_real:, :] = jnp.zeros((per_example - n_real, LOSS_COLS), F32)
        l_ref[...] += jnp.sum(jnp.sum(d * d, axis=0, keepdims=True), axis=1, keepdims=True)

    blk = pl.BlockSpec((None, per_example, LOSS_COLS), lambda b, j: (b, 0, j))
    return pl.pallas_call(
        body, name="loss_head", grid=(batch, D_MODEL // LOSS_COLS),
        in_specs=[blk, pl.BlockSpec((None, seq, LOSS_COLS), lambda b, j: (b, 0, j))],
        out_specs=[blk, pl.BlockSpec((SUBLANES, LANES), lambda b, j: (0, 0))],
        out_shape=[_sds(h.shape, F32), _sds((SUBLANES, LANES), F32)],
        compiler_params=_params("arbitrary", "arbitrary"),
    )(h, loss_target)


def _meta_grad(dh0, batch, rows_per_example):
    per = rows_per_example // N_META

    def body(d_ref, o_ref):
        @pl.when(pl.program_id(0) == 0)
        def _():
            o_ref[...] = jnp.zeros_like(o_ref)

        o_ref[...] += d_ref[...]

    return pl.pallas_call(
        body, name="meta_grad", grid=(batch,),
        in_specs=[pl.BlockSpec((N_META, D_MODEL), lambda b: (b * per, 0))],
        out_specs=pl.BlockSpec((N_META, D_MODEL), lambda b: (0, 0)),
        out_shape=_sds((N_META, D_MODEL), F32),
        compiler_params=_params("arbitrary"),
    )(dh0)


def _elementwise_tile(rows, cols):
    tile = rows
    while tile * cols * 4 > (1 << 20) and tile % 16 == 0:
        tile //= 2
    return tile


def _sum_slots(buf, name):
    k, rows, cols = buf.shape
    tile = _elementwise_tile(rows, cols)

    def body(*refs):
        total = refs[0][...].astype(F32)
        for r in refs[1:k]:
            total = total + r[...].astype(F32)
        refs[k][...] = total

    def slot(s):
        return pl.BlockSpec((None, tile, cols), lambda i: (s, i, 0))

    return pl.pallas_call(
        body, name=name, grid=(rows // tile,),
        in_specs=[slot(s) for s in range(k)], out_specs=pl.BlockSpec((tile, cols), lambda i: (i, 0)),
        out_shape=_sds((rows, cols), F32), compiler_params=_params("parallel"),
    )(*([buf] * k))


def _sum_pieces(pieces, landed, chip, layer, n_layers, stacked):
    _, rows, cols = pieces.shape
    tile = _elementwise_tile(rows, cols)

    def body(chip_ref, own_ref, a_ref, b_ref, c_ref, *rest):
        o_ref, token = rest[-2:]
        total = ((own_ref[...].astype(F32) + a_ref[...].astype(F32)) + b_ref[...].astype(F32)) + c_ref[...].astype(F32)
        o_ref[...] = total.astype(BF16)
        token[...] = jnp.zeros_like(token)

    def slot(offset):
        return pl.BlockSpec((None, tile, cols), lambda i, chip_ref: ((chip_ref[0] + offset) % N_CHIPS, i, 0))

    carried = [] if stacked is None else [stacked]
    grid_spec = pltpu.PrefetchScalarGridSpec(
        num_scalar_prefetch=1, grid=(rows // tile,),
        in_specs=[slot(0), slot(1), slot(2), slot(3)] + [pl.BlockSpec(memory_space=pl.ANY)] * len(carried),
        out_specs=[pl.BlockSpec((None, tile, cols), lambda i, chip_ref: (layer, i, 0)),
                   pl.BlockSpec((SUBLANES, LANES), lambda i, chip_ref: (0, 0))])
    return pl.pallas_call(
        body, name="sum_grad_pieces", grid_spec=grid_spec,
        out_shape=[_sds((n_layers, rows, cols), BF16), _sds((SUBLANES, LANES), F32)],
        input_output_aliases={5: 0} if carried else {},
        compiler_params=_params("arbitrary"),
    )(chip, pieces, landed, landed, landed, *carried)


def _adamw(w, m, v, grads, name):
    flat = w.ndim == 2
    if flat:
        w, m, v = w[None], m[None], v[None]
        grads = [g[None] for g in grads]
    layers, rows, cols = w.shape
    tile = _elementwise_tile(rows, cols)
    ng = len(grads)
    m_scale = 1.0 - ADAM_B1 ** ADAM_STEP
    v_scale = 1.0 - ADAM_B2 ** ADAM_STEP

    def body(*refs):
        w_ref, m_ref, v_ref = refs[:3]
        g_refs = refs[3:3 + ng]
        g_out, d_out, m_out, v_out = refs[-4:]
        g = g_refs[0][...].astype(F32)
        for r in g_refs[1:]:
            g = g + r[...].astype(F32)
        m_new = ADAM_B1 * m_ref[...] + (1.0 - ADAM_B1) * g
        v_new = ADAM_B2 * v_ref[...] + (1.0 - ADAM_B2) * (g * g)
        m_hat = m_new / m_scale
        v_hat = v_new / v_scale
        g_out[...] = g
        d_out[...] = -ADAM_LR * (m_hat / (jnp.sqrt(v_hat) + ADAM_EPS) + ADAM_WD * w_ref[...])
        m_out[...] = m_new
        v_out[...] = v_new

    spec = pl.BlockSpec((None, tile, cols), lambda l, i: (l, i, 0))
    out = pl.pallas_call(
        body, name=name, grid=(layers, rows // tile),
        in_specs=[spec] * (3 + ng), out_specs=[spec] * 4, out_shape=[_sds(w.shape, F32)] * 4,
        compiler_params=_params("parallel", "parallel"),
    )(w, m, v, *grads)
    return [o[0] for o in out] if flat else out


def _position():
    return lax.axis_index("x"), lax.axis_index("y"), lax.axis_index("c")


def _other_chips(x, y):
    return [(1 - x, y), (x, 1 - y), (1 - x, 1 - y)]


def _exchange_chips(arrays, name, scatter):
    n = len(arrays)

    def body(*refs):
        src, dst = refs[:n], refs[n:2 * n]
        send_sems, recv_sems, local_sems = refs[2 * n:]
        x, y, c = _position()
        mine = 2 * x + y
        copies = []
        for i in range(n):
            own = src[i].at[mine] if scatter else src[i]
            copies.append(pltpu.make_async_copy(own, dst[i].at[mine], local_sems.at[i]))
        for k, (px, py) in enumerate(_other_chips(x, y)):
            for i in range(n):
                piece = src[i].at[2 * px + py] if scatter else src[i]
                copies.append(pltpu.make_async_remote_copy(
                    src_ref=piece, dst_ref=dst[i].at[mine], send_sem=send_sems.at[k, i], recv_sem=recv_sems.at[k, i],
                    device_id=(px, py, c), device_id_type=MESH))
        for cp in copies:
            cp.start()
        for cp in copies:
            cp.wait()

    def out_shape(a):
        return _sds(a.shape if scatter else (N_CHIPS,) + a.shape, a.dtype)

    return pl.pallas_call(
        body, name=name, in_specs=[HBM_SPEC] * n, out_specs=[HBM_SPEC] * n, out_shape=[out_shape(a) for a in arrays],
        scratch_shapes=[pltpu.SemaphoreType.DMA((N_CHIPS - 1, n)), pltpu.SemaphoreType.DMA((N_CHIPS - 1, n)),
                        pltpu.SemaphoreType.DMA((n,))],
    )(*arrays)


def _swap_sibling(arrays, name):
    n = len(arrays)

    def body(*refs):
        src, dst = refs[:n], refs[n:2 * n]
        send_sems, recv_sems = refs[2 * n:]
        x, y, c = _position()
        copies = [pltpu.make_async_remote_copy(
            src_ref=src[i], dst_ref=dst[i], send_sem=send_sems.at[i], recv_sem=recv_sems.at[i],
            device_id=(x, y, 1 - c), device_id_type=MESH) for i in range(n)]
        for cp in copies:
            cp.start()
        for cp in copies:
            cp.wait()

    return pl.pallas_call(
        body, name=name, in_specs=[HBM_SPEC] * n, out_specs=[HBM_SPEC] * n,
        out_shape=[_sds(a.shape, a.dtype) for a in arrays],
        scratch_shapes=[pltpu.SemaphoreType.DMA((n,)), pltpu.SemaphoreType.DMA((n,))],
    )(*arrays)


SEM_SPEC = pl.BlockSpec(memory_space=pltpu.SEMAPHORE)
ANY_SPEC = pl.BlockSpec(memory_space=pl.ANY)
IN_FLIGHT = pltpu.SideEffectType.DATAFLOW_SIDE_EFFECTING


def _peer_sems(all_devices):
    return pltpu.SemaphoreType.DMA(((N_DEV if all_devices else N_CHIPS) - 1,))


def _own_slot(all_devices, x, y, c):
    return 4 * x + 2 * y + c if all_devices else 2 * x + y


def _peers(all_devices, x, y, c):
    if not all_devices:
        return [((px, py, c), 2 * px + py) for px, py in _other_chips(x, y)]
    out = []
    for fx in range(2):
        for fy in range(2):
            for fc in range(2):
                if fx or fy or fc:
                    px, py, pc = (1 - x if fx else x), (1 - y if fy else y), (1 - c if fc else c)
                    out.append(((px, py, pc), 4 * px + 2 * py + pc))
    return out


def _in_hbm(a):
    return pltpu.with_memory_space_constraint(a, pltpu.HBM)


def _place_slot(a, slot, n_slots, name):
    rows, cols = a.shape
    tile = _elementwise_tile(rows, cols)

    def body(slot_ref, a_ref, o_ref):
        o_ref[...] = a_ref[...]

    grid_spec = pltpu.PrefetchScalarGridSpec(
        num_scalar_prefetch=1, grid=(rows // tile,),
        in_specs=[pl.BlockSpec((tile, cols), lambda i, slot_ref: (i, 0))],
        out_specs=pl.BlockSpec((None, tile, cols), lambda i, slot_ref: (slot_ref[0], i, 0)))
    return pl.pallas_call(
        body, name=name, grid_spec=grid_spec, out_shape=_sds((n_slots, rows, cols), a.dtype),
        compiler_params=_params("parallel"),
    )(slot, a)


def _place_shard(w, layer, chip, name):
    _, rows, cols = w.shape
    tile = _elementwise_tile(rows, cols)

    def body(chip_ref, w_ref, o_ref):
        o_ref[...] = w_ref[...].astype(BF16)

    grid_spec = pltpu.PrefetchScalarGridSpec(
        num_scalar_prefetch=1, grid=(rows // tile,),
        in_specs=[pl.BlockSpec((None, tile, cols), lambda i, chip_ref: (layer, i, 0))],
        out_specs=pl.BlockSpec((None, tile, cols), lambda i, chip_ref: (chip_ref[0], i, 0)))
    return pl.pallas_call(
        body, name=name, grid_spec=grid_spec, out_shape=_sds((N_CHIPS, rows, cols), BF16),
        compiler_params=_params("parallel"),
    )(chip, w)


def _rows_of(ref, slot, core, halves):
    if not halves:
        return ref.at[slot]
    half = ref.shape[1] // 2
    return ref.at[slot, pl.ds(pl.multiple_of(core * half, half), half)]


def _gather_start(bufs, after, name, all_devices=False, halves=False):
    n = len(bufs)

    def body(*refs):
        buf = refs[:n]
        send, recv = refs[n + 1:2 * n + 1], refs[2 * n + 1:3 * n + 1]
        token = refs[4 * n + 1]
        x, y, c = _position()
        mine = _own_slot(all_devices, x, y, c)
        for i in range(n):
            for k, (peer, _) in enumerate(_peers(all_devices, x, y, c)):
                rows = _rows_of(buf[i], mine, c, halves)
                pltpu.make_async_remote_copy(
                    src_ref=rows, dst_ref=rows, send_sem=send[i].at[k], recv_sem=recv[i].at[k],
                    device_id=peer, device_id_type=MESH).start()
        token[...] = jnp.zeros_like(token)

    sems = _peer_sems(all_devices)
    out = pl.pallas_call(
        body, name=name, in_specs=[HBM_SPEC] * n + [ANY_SPEC],
        out_specs=[SEM_SPEC] * (2 * n) + [HBM_SPEC] * n + [pl.BlockSpec(memory_space=pltpu.VMEM)],
        out_shape=[sems] * (2 * n) + [pltpu.HBM(b.shape, b.dtype) for b in bufs] + [_sds((SUBLANES, LANES), F32)],
        input_output_aliases={i: 2 * n + i for i in range(n)},
        compiler_params=pltpu.CompilerParams(has_side_effects=IN_FLIGHT),
    )(*[_in_hbm(b) for b in bufs], after)
    return [(out[i], out[n + i], out[2 * n + i]) for i in range(n)], out[3 * n]


def _gather_relay(flights, after, name):
    n = len(flights)

    def body(*refs):
        buf, send, recv = refs[:n], refs[n:2 * n], refs[2 * n:3 * n]
        send2, recv2 = refs[3 * n + 1:4 * n + 1], refs[4 * n + 1:5 * n + 1]
        token = refs[6 * n + 1]
        x, y, c = _position()
        mine = 2 * x + y
        for i in range(n):
            for k, (peer, slot) in enumerate(_peers(False, x, y, c)):
                arrived = _rows_of(buf[i], slot, c, True)
                cp = pltpu.make_async_remote_copy(
                    src_ref=_rows_of(buf[i], mine, c, True), dst_ref=arrived, send_sem=send[i].at[k], recv_sem=recv[i].at[k],
                    device_id=peer, device_id_type=MESH)
                cp.wait_send()
                cp.wait_recv()
                pltpu.make_async_remote_copy(
                    src_ref=arrived, dst_ref=arrived, send_sem=send2[i].at[k], recv_sem=recv2[i].at[k],
                    device_id=(x, y, 1 - c), device_id_type=MESH).start()
        token[...] = jnp.zeros_like(token)

    bufs = [f[2] for f in flights]
    sems = _peer_sems(False)
    out = pl.pallas_call(
        body, name=name, in_specs=[HBM_SPEC] * n + [SEM_SPEC] * (2 * n) + [ANY_SPEC],
        out_specs=[SEM_SPEC] * (2 * n) + [HBM_SPEC] * n + [pl.BlockSpec(memory_space=pltpu.VMEM)],
        out_shape=[sems] * (2 * n) + [pltpu.HBM(b.shape, b.dtype) for b in bufs] + [_sds((SUBLANES, LANES), F32)],
        input_output_aliases={i: 2 * n + i for i in range(n)},
        compiler_params=pltpu.CompilerParams(has_side_effects=IN_FLIGHT),
    )(*bufs, *[f[0] for f in flights], *[f[1] for f in flights], after)
    return [(out[i], out[n + i], out[2 * n + i]) for i in range(n)], out[3 * n]


def _gather_wait(send, recv, buf, after, name, all_devices=False, relayed=False):
    def body(buf_ref, send_ref, recv_ref, after_ref, out_ref):
        x, y, c = _position()
        mine = _own_slot(all_devices, x, y, c)
        for k, (peer, slot) in enumerate(_peers(all_devices, x, y, c)):
            if relayed:
                cp = pltpu.make_async_remote_copy(
                    src_ref=_rows_of(buf_ref, slot, c, True), dst_ref=_rows_of(buf_ref, slot, 1 - c, True),
                    send_sem=send_ref.at[k], recv_sem=recv_ref.at[k], device_id=(x, y, 1 - c), device_id_type=MESH)
            else:
                cp = pltpu.make_async_remote_copy(
                    src_ref=buf_ref.at[mine], dst_ref=buf_ref.at[slot], send_sem=send_ref.at[k], recv_sem=recv_ref.at[k],
                    device_id=peer, device_id_type=MESH)
            cp.wait_send()
            cp.wait_recv()

    return pl.pallas_call(
        body, name=name, in_specs=[HBM_SPEC, SEM_SPEC, SEM_SPEC, ANY_SPEC], out_specs=HBM_SPEC,
        out_shape=pltpu.HBM(buf.shape, buf.dtype), input_output_aliases={0: 0},
        compiler_params=pltpu.CompilerParams(has_side_effects=IN_FLIGHT),
    )(buf, send, recv, after)


def _scatter_start(pieces, name):
    n = len(pieces)

    def body(*refs):
        src = refs[:n]
        send, recv = refs[n:2 * n], refs[2 * n:3 * n]
        land = refs[4 * n:5 * n]
        token = refs[5 * n]
        x, y, c = _position()
        mine = 2 * x + y
        for i in range(n):
            for k, (px, py) in enumerate(_other_chips(x, y)):
                pltpu.make_async_remote_copy(
                    src_ref=src[i].at[2 * px + py], dst_ref=land[i].at[mine], send_sem=send[i].at[k], recv_sem=recv[i].at[k],
                    device_id=(px, py, c), device_id_type=MESH).start()
        token[...] = jnp.zeros_like(token)

    hbm = [pltpu.HBM(p.shape, p.dtype) for p in pieces]
    out = pl.pallas_call(
        body, name=name, in_specs=[HBM_SPEC] * n,
        out_specs=[SEM_SPEC] * (2 * n) + [HBM_SPEC] * (2 * n) + [pl.BlockSpec(memory_space=pltpu.VMEM)],
        out_shape=[_peer_sems(False)] * (2 * n) + hbm + hbm + [_sds((SUBLANES, LANES), F32)],
        input_output_aliases={i: 2 * n + i for i in range(n)},
        compiler_params=pltpu.CompilerParams(has_side_effects=IN_FLIGHT),
    )(*[_in_hbm(p) for p in pieces])
    return [(out[i], out[n + i], out[2 * n + i], out[3 * n + i]) for i in range(n)], out[4 * n]


def _scatter_wait(send, recv, pieces, land, after, name):
    def body(src_ref, land_ref, send_ref, recv_ref, after_ref, src_out, land_out):
        x, y, c = _position()
        for k, (px, py) in enumerate(_other_chips(x, y)):
            cp = pltpu.make_async_remote_copy(
                src_ref=src_ref.at[2 * px + py], dst_ref=land_ref.at[2 * px + py], send_sem=send_ref.at[k],
                recv_sem=recv_ref.at[k], device_id=(px, py, c), device_id_type=MESH)
            cp.wait_send()
            cp.wait_recv()

    return pl.pallas_call(
        body, name=name, in_specs=[HBM_SPEC, HBM_SPEC, SEM_SPEC, SEM_SPEC, ANY_SPEC], out_specs=[HBM_SPEC, HBM_SPEC],
        out_shape=[pltpu.HBM(pieces.shape, pieces.dtype), pltpu.HBM(land.shape, land.dtype)],
        input_output_aliases={0: 0, 1: 1},
        compiler_params=pltpu.CompilerParams(has_side_effects=IN_FLIGHT),
    )(pieces, land, send, recv, after)


def _rope_tables(batch, rows_per_example):
    inv_freq = ROPE_THETA ** (-jnp.arange(0, 64, 2, dtype=F32) / 64)
    ang = jnp.arange(rows_per_example, dtype=F32)[:, None] * inv_freq[None, :]
    cos, sin = jnp.cos(ang), jnp.sin(ang)
    cos128 = jnp.concatenate([cos, cos, cos, cos], axis=1)
    sin128 = jnp.concatenate([-sin, sin, -sin, sin], axis=1)
    return jnp.tile(cos128, (batch, 1)), jnp.tile(sin128, (batch, 1))


def _block_diagonal(w):
    eye = jnp.eye(LRU_BLOCKS, dtype=w.dtype)
    return (w[:, :, None, :] * eye[:, None, :, None]).reshape(LRU_W, LRU_W)


def _diagonal_blocks(dense):
    d4 = dense.reshape(LRU_BLOCKS, LRU_BLOCK, LRU_BLOCKS, LRU_BLOCK)
    return jnp.stack([d4[n, :, n, :] for n in range(LRU_BLOCKS)])


def _row(v):
    return v.reshape(1, -1)


def _local_step(x, loss_target, meta_tokens, small, depth, big_weight, on_layer_grads):
    batch, seq, _ = x.shape
    n_real = N_META + seq
    n_blocks = -(-n_real // TIME_BLOCK)
    per_example = n_blocks * TIME_BLOCK
    pad = per_example - n_real
    meta = jnp.broadcast_to(meta_tokens[None], (batch, N_META, D_MODEL))
    h = jnp.concatenate([meta, x, jnp.zeros((batch, pad, D_MODEL), F32)], axis=1).reshape(batch * per_example, D_MODEL)
    rope_cos, rope_sin = _rope_tables(batch, per_example)

    saved = []
    for l in range(depth):
        wa = _block_diagonal(small['gate_a_w'][l]).astype(BF16)
        wx = _block_diagonal(small['gate_x_w'][l]).astype(BF16)
        lru_small = (small['conv_w'][l], _row(small['conv_b'][l]), wa, _row(small['gate_a_b'][l]), wx,
                     _row(small['gate_x_b'][l]), _row(small['lru_lambda'][l]))
        w_in = big_weight('w_in', l, h)
        z1, q, kv, xg = _in_proj_fwd(h, _row(small['pre_mix_norm'][l]), w_in, rope_cos, rope_sin)
        attn, lse = _attn_fwd(q, kv, small['attn_sinks'][l], batch, n_blocks)
        hs, lru = _lru_fwd(xg, *lru_small, batch, n_blocks)
        w_out = big_weight('w_out', l, xg)
        w_up4 = big_weight('w_up', l, xg)
        w_down = big_weight('w_down', l, xg)
        gains = [_row(small[n][l]) for n in ('attn_group_norm', 'lru_group_norm', 'post_mix_norm', 'pre_mlp_norm',
                                             'post_mlp_norm')]
        groups, o, h_mid, z2, slope, act, y, h_out = _out_mlp_fwd(attn, lru, h, w_out, w_up4, w_down, *gains)
        saved.append(dict(h=h, z1=z1, q=q, kv=kv, xg=xg, attn=attn, lse=lse, hs=hs, lru=lru, groups=groups, o=o,
                          h_mid=h_mid, z2=z2, slope=slope, act=act, y=y, lru_small=lru_small, w_in=w_in, w_out=w_out,
                          w_up4=w_up4, w_down=w_down, gains=gains))
        h = h_out

    dh, sq_err = _loss_head(h.reshape(batch, per_example, D_MODEL), loss_target)
    dh = dh.reshape(batch * per_example, D_MODEL)

    gs = {n: [None] * depth for n in REPLICATED_NAMES + ('conv_w',)}
    handed_over = None
    for l in reversed(range(depth)):
        s = saved[l]

        def ordered_after(gain):
            return gain if handed_over is None else gain + handed_over[0, 0]

        gains = list(s['gains'])
        gains[4] = ordered_after(gains[4])
        dh_mid, dy, dup, do, d_attn, d_lru, g_rows = _mlp_out_bwd(dh, s['h_mid'], s['y'], s['slope'], s['o'], s['attn'],
                                                                s['lru'], s['w_out'], s['w_up4'], s['w_down'], *gains)
        handed_over = on_layer_grads(l, {
            'w_down': _matmul_tn(s['act'], dy, 512, D_MODEL, "grad_w_down", BF16),
            'w_up': _matmul_tn(s['z2'], dup, 512, D_MODEL, "grad_w_up", BF16, column_blocks=True)})
        dxg, dwa, dwx, g_lru = _lru_bwd(s['xg'], s['hs'], d_lru, *s['lru_small'], batch, n_blocks)
        dq, dkv, dsink = _attn_bwd(s['q'], s['kv'], small['attn_sinks'][l], s['attn'], s['lse'], d_attn, batch, n_blocks)
        dh, dproj, g_in = _in_proj_bwd(dh_mid, s['h'], dq, dkv, dxg, rope_cos, rope_sin,
                                       ordered_after(_row(small['pre_mix_norm'][l])), s['w_in'])
        handed_over = on_layer_grads(l, {
            'w_out': _matmul_tn(s['groups'], do, 512, D_MODEL, "grad_w_out", BF16),
            'w_in': _matmul_tn(dproj, s['z1'], IN_COLS // 2, D_MODEL, "grad_w_in", BF16)})
        gs['pre_mlp_norm'][l], gs['post_mlp_norm'][l] = g_rows[0], g_rows[1]
        gs['post_mix_norm'][l] = g_rows[2]
        gs['attn_group_norm'][l], gs['lru_group_norm'][l] = g_rows[3, :ATTN_W], g_rows[3, ATTN_W:]
        gs['pre_mix_norm'][l] = g_in[0]
        gs['attn_sinks'][l] = dsink[:, 0]
        gs['conv_w'][l] = g_lru[:CONV_TAPS]
        gs['conv_b'][l], gs['gate_a_b'][l], gs['gate_x_b'][l], gs['lru_lambda'][l] = g_lru[4], g_lru[5], g_lru[6], g_lru[7]
        gs['gate_a_w'][l] = _diagonal_blocks(dwa)
        gs['gate_x_w'][l] = _diagonal_blocks(dwx)

    grad_x = dh.reshape(batch, per_example, D_MODEL)[:, N_META:n_real]
    grad_meta = _meta_grad(dh, batch, per_example)
    small_grads = {n: jnp.stack(v) for n, v in gs.items()}
    return sq_err, grad_x, grad_meta, small_grads, handed_over


PACK_UNIT = SUBLANES * LANES


def _size(shape):
    size = 1
    for d in shape:
        size *= d
    return size


def _pack(arrays):
    parts = []
    for a in arrays:
        flat = a.reshape(-1)
        padded = -(-flat.shape[0] // PACK_UNIT) * PACK_UNIT
        if padded != flat.shape[0]:
            flat = jnp.pad(flat, (0, padded - flat.shape[0]))
        parts.append(flat.reshape(-1, LANES))
    return jnp.concatenate(parts, axis=0)


def _unpack(buf, shapes):
    out, at = [], 0
    for shp in shapes:
        size = _size(shp)
        rows = -(-size // PACK_UNIT) * SUBLANES
        part = buf[at:at + rows]
        if rows * LANES != size:
            part = part.reshape(-1)[:size]
        out.append(part.reshape(shp))
        at += rows
    return out


def kernel(x, meta_tokens, pre_mix_norm, w_in, attn_sinks, conv_w, conv_b, gate_a_w, gate_a_b, gate_x_w, gate_x_b, lru_lambda, attn_group_norm, lru_group_norm, w_out, post_mix_norm, pre_mlp_norm, w_up, w_down, post_mlp_norm, loss_target, m_meta_tokens, m_pre_mix_norm, m_w_in, m_attn_sinks, m_conv_w, m_conv_b, m_gate_a_w, m_gate_a_b, m_gate_x_w, m_gate_x_b, m_lru_lambda, m_attn_group_norm, m_lru_group_norm, m_w_out, m_post_mix_norm, m_pre_mlp_norm, m_w_up, m_w_down, m_post_mlp_norm, v_meta_tokens, v_pre_mix_norm, v_w_in, v_attn_sinks, v_conv_w, v_conv_b, v_gate_a_w, v_gate_a_b, v_gate_x_w, v_gate_x_b, v_lru_lambda, v_attn_group_norm, v_lru_group_norm, v_w_out, v_post_mix_norm, v_pre_mlp_norm, v_w_up, v_w_down, v_post_mlp_norm):
    given = dict(locals())
    w = {n: given[n] for n in WEIGHT_NAMES}
    m = {n: given['m_' + n] for n in WEIGHT_NAMES}
    v = {n: given['v_' + n] for n in WEIGHT_NAMES}
    depth = w_in.shape[0]
    for d in (w, m, v):
        d['w_in'] = jnp.swapaxes(d['w_in'], 1, 2)
    chip = 2 * lax.axis_index("x") + lax.axis_index("y")
    chip1 = chip.reshape(1).astype(jnp.int32)

    in_flight = []
    all_started = chip1
    for l in range(depth):
        bufs = [_place_shard(w[n], l, chip1, "place_" + n) for n in BIG_NAMES]
        handles, all_started = _gather_start(bufs, all_started, "gather_start_%d" % l, halves=True)
        in_flight.append(dict(zip(BIG_NAMES, handles)))
    first_use = {'w_in': ('w_in',), 'w_out': ('w_out', 'w_up', 'w_down')}

    def big_weight(name, l, after):
        if l == 0 and name == 'w_in':
            after = all_started
        names = first_use.get(name, ()) if l == 0 else (BIG_NAMES if name == 'w_in' else ())
        if names:
            relayed, _ = _gather_relay([in_flight[l][n] for n in names], after, "gather_relay_%d_%s" % (l, name))
            in_flight[l].update(zip(names, relayed))
        send, recv, buf = in_flight[l][name]
        full = _gather_wait(send, recv, buf, after, "gather_wait_%d_%s" % (l, name), relayed=True)
        if name == 'w_in':
            return full.reshape(IN_COLS, D_MODEL)
        if name == 'w_out':
            return full.reshape(D_MODEL, D_MODEL)
        if name == 'w_down':
            return full.reshape(D_FF, D_MODEL)
        return full

    gathered = _exchange_chips([meta_tokens, conv_w], "gather_small_weights", scatter=False)
    full_meta = jnp.concatenate([gathered[0][s] for s in range(N_CHIPS)], axis=1)
    small = {n: w[n] for n in REPLICATED_NAMES}
    small['conv_w'] = jnp.concatenate([gathered[1][s] for s in range(N_CHIPS)], axis=2)

    scattering = [{} for _ in range(depth)]

    def on_layer_grads(l, big):
        names = list(big)
        pieces = [big[n].reshape(N_CHIPS, -1, D_MODEL) for n in names]
        handles, started = _scatter_start(pieces, "scatter_start_%d_%s" % (l, names[0]))
        scattering[l].update(zip(names, handles))
        return started

    sq_err, grad_x, grad_meta, small_grads, all_handed_over = _local_step(x, loss_target, full_meta, small, depth, big_weight,
                                                                           on_layer_grads)
    loss = lax.psum(sq_err[0, 0] * (0.5 / D_MODEL), ("x", "y", "c"))

    small_names = list(REPLICATED_NAMES) + list(COLUMN_SHARDED_SMALL)
    small_full = dict(small_grads)
    small_full['meta_tokens'] = grad_meta
    device1 = (2 * chip + lax.axis_index("c")).reshape(1).astype(jnp.int32)
    packed = _place_slot(_pack([small_full[n] for n in small_names]), device1, N_DEV, "place_small_grads")
    [small_flight], done = _gather_start([packed], all_handed_over, "gather_small_grads_start", all_devices=True)

    partial = {n: None for n in BIG_NAMES}
    for l in reversed(range(depth)):
        for n in BIG_NAMES:
            send, recv, pieces, land = scattering[l][n]
            pieces, land = _scatter_wait(send, recv, pieces, land, done, "scatter_wait_%d_%s" % (l, n))
            partial[n], done = _sum_pieces(pieces, land, chip1, l, depth, partial[n])
    sibling = _swap_sibling([partial[n] for n in BIG_NAMES], "swap_partial_grads")
    results = {n: _adamw(w[n], m[n], v[n], [partial[n], other], "adamw_" + n) for n, other in zip(BIG_NAMES, sibling)}

    slots = _gather_wait(*small_flight, results['w_down'][0], "gather_small_grads_wait", all_devices=True)
    summed = _unpack(_sum_slots(slots, "sum_small_grads"), [small_full[n].shape for n in small_names])
    grads = dict(zip(small_names, summed))
    grads['meta_tokens'] = lax.dynamic_slice_in_dim(grads['meta_tokens'], chip * meta_tokens.shape[1], meta_tokens.shape[1], 1)
    grads['conv_w'] = lax.dynamic_slice_in_dim(grads['conv_w'], chip * conv_w.shape[2], conv_w.shape[2], 2)

    out_g, out_d, out_m, out_v = {}, {}, {}, {}
    for n in BIG_NAMES:
        out_g[n], out_d[n], out_m[n], out_v[n] = [jnp.swapaxes(r, 1, 2) if n == 'w_in' else r for r in results[n][:4]]
    shapes = [w[n].shape for n in small_names]
    res = _adamw(_pack([w[n] for n in small_names]), _pack([m[n] for n in small_names]), _pack([v[n] for n in small_names]),
                 [_pack([grads[n] for n in small_names])], "adamw_small")
    for k, store in enumerate((out_g, out_d, out_m, out_v)):
        for n, a in zip(small_names, _unpack(res[k], shapes)):
            store[n] = a

    return (loss, grad_x, *[out_g[n] for n in WEIGHT_NAMES], *[out_d[n] for n in WEIGHT_NAMES],
            *[out_m[n] for n in WEIGHT_NAMES], *[out_v[n] for n in WEIGHT_NAMES])
```

```python
import functools

import jax
import jax.numpy as jnp
from jax import lax
from jax.experimental import pallas as pl
from jax.experimental.pallas import tpu as pltpu

F32 = jnp.float32
BF16 = jnp.bfloat16

D_MODEL = 1024
N_HEADS = 8
ATTN_W = 512
KV_W = 128
LRU_W = 512
LRU_BLOCKS = 8
LRU_BLOCK = 64
IN_COLS = 1792
D_FF = 4096
N_META = 16
CONV_TAPS = 4
LRU_C = 8.0
ROPE_THETA = 10000.0
EPS = 1e-6
ATTN_SCALE = 0.125

ADAM_LR = 0.001
ADAM_B1 = 0.9
ADAM_B2 = 0.999
ADAM_EPS = 1e-08
ADAM_WD = 0.01
ADAM_STEP = 10

N_CHIPS = 4
N_DEV = 8
TIME_BLOCK = 128
ROW_TILE = 256
PROJ_ROW_TILE = 544
LANES = 128
SUBLANES = 8
MASKED = -1e30
VMEM_LIMIT = 56 * 1024 * 1024

MESH = pl.DeviceIdType.MESH
HBM_SPEC = pl.BlockSpec(memory_space=pltpu.HBM)

WEIGHT_NAMES = ['meta_tokens', 'pre_mix_norm', 'w_in', 'attn_sinks', 'conv_w', 'conv_b', 'gate_a_w', 'gate_a_b',
                'gate_x_w', 'gate_x_b', 'lru_lambda', 'attn_group_norm', 'lru_group_norm', 'w_out', 'post_mix_norm',
                'pre_mlp_norm', 'w_up', 'w_down', 'post_mlp_norm']
BIG_NAMES = ('w_in', 'w_out', 'w_up', 'w_down')
COLUMN_SHARDED_SMALL = ('meta_tokens', 'conv_w')
REPLICATED_NAMES = tuple(n for n in WEIGHT_NAMES if n not in BIG_NAMES and n not in COLUMN_SHARDED_SMALL)


def _sds(shape, dtype):
    return jax.ShapeDtypeStruct(tuple(shape), dtype)


def _params(*sem):
    return pltpu.CompilerParams(dimension_semantics=sem, vmem_limit_bytes=VMEM_LIMIT)


def _row_spec(width, tile=ROW_TILE):
    return pl.BlockSpec((tile, width), lambda i: (i, 0))


def _proj_tile(rows):
    tile = PROJ_ROW_TILE
    while rows % tile:
        tile -= 16
    return tile


def _whole_spec(a):
    nd = a.ndim
    return pl.BlockSpec(a.shape, lambda *_: (0,) * nd)


def _resident_spec(a):
    nd = a.ndim
    return pl.BlockSpec(a.shape, lambda *_: (0,) * nd, pipeline_mode=pl.Buffered(1))


def _rms(x, g):
    r = lax.rsqrt(jnp.mean(x * x, axis=-1, keepdims=True) + EPS)
    return x * r * g


def _rms_bwd(dy, x, g):
    r = lax.rsqrt(jnp.mean(x * x, axis=-1, keepdims=True) + EPS)
    xh = x * r
    dg = jnp.sum(dy * xh, axis=0, keepdims=True)
    dxh = dy * g
    dx = r * (dxh - xh * jnp.mean(dxh * xh, axis=-1, keepdims=True))
    return dx, dg


def _rope(x, cos, sin_signed):
    width = x.shape[1]
    reps = width // LANES
    if reps > 1:
        cos = jnp.tile(cos, (1, reps))
        sin_signed = jnp.tile(sin_signed, (1, reps))
    lane = lax.broadcasted_iota(jnp.int32, x.shape, 1)
    first_half = jnp.bitwise_and(lane, 32) == 0
    other = jnp.where(first_half, pltpu.roll(x, width - 32, 1), pltpu.roll(x, 32, 1))
    return x * cos + other * sin_signed


def _sigmoid(x):
    return 1.0 / (1.0 + jnp.exp(-x))


def _log1p(e):
    return jnp.where(e < 1e-3, e * (1.0 - e * (0.5 - e * (1.0 / 3.0))), jnp.log(1.0 + e))


def _one_minus_square(a, log_a):
    x = 2.0 * log_a
    return jnp.where(x > -0.002, x * (-1.0 - 0.5 * x), 1.0 - a * a)


GELU_K = 0.7978845608028654
GELU_C = 0.044715


def _gelu(x):
    t = jnp.tanh(GELU_K * (x + GELU_C * x * x * x))
    return 0.5 * x * (1.0 + t)


def _gelu_and_grad(x):
    x2 = x * x
    t = jnp.tanh(GELU_K * (x + GELU_C * x * x2))
    val = 0.5 * x * (1.0 + t)
    grad = 0.5 * (1.0 + t) + 0.5 * x * (1.0 - t * t) * GELU_K * (1.0 + 3.0 * GELU_C * x2)
    return val, grad


def _dot(a, b):
    return jnp.dot(a, b, preferred_element_type=F32)


def _dot_nt(a, b):
    return lax.dot_general(a, b, (((1,), (1,)), ((), ())), preferred_element_type=F32)


def _dot_tn(a, b):
    return lax.dot_general(a, b, (((0,), (0,)), ((), ())), preferred_element_type=F32)


def _put_rows(rows_8, values):
    d = values[0].shape[1]
    rowid = lax.broadcasted_iota(jnp.int32, (rows_8, d), 0)
    out = jnp.zeros((rows_8, d), F32)
    for k, v in enumerate(values):
        out = out + jnp.where(rowid == k, v, 0.0)
    return out


def _in_proj_fwd(h, gain, w_in, rope_cos, rope_sin):
    rows = h.shape[0]

    def body(h_ref, g_ref, w_ref, c_ref, s_ref, z_ref, q_ref, kv_ref, xg_ref):
        z = _rms(h_ref[...], g_ref[...]).astype(BF16)
        z_ref[...] = z
        proj = _dot_nt(z, w_ref[...])
        cos = c_ref[...]
        sin = s_ref[...]
        q_ref[...] = (_rope(proj[:, :ATTN_W], cos, sin) * ATTN_SCALE).astype(BF16)
        kv_ref[:, :KV_W] = _rope(proj[:, ATTN_W:ATTN_W + KV_W], cos, sin).astype(BF16)
        kv_ref[:, KV_W:] = proj[:, ATTN_W + KV_W:ATTN_W + 2 * KV_W].astype(BF16)
        xg_ref[...] = proj[:, ATTN_W + 2 * KV_W:]

    tile = _proj_tile(rows)
    rs = functools.partial(_row_spec, tile=tile)
    return pl.pallas_call(
        body, name="in_proj_fwd", grid=(rows // tile,),
        in_specs=[rs(D_MODEL), _whole_spec(gain), _whole_spec(w_in), rs(LANES), rs(LANES)],
        out_specs=[rs(D_MODEL), rs(ATTN_W), rs(2 * KV_W), rs(2 * LRU_W)],
        out_shape=[_sds((rows, D_MODEL), BF16), _sds((rows, ATTN_W), BF16), _sds((rows, 2 * KV_W), BF16),
                   _sds((rows, 2 * LRU_W), F32)],
        compiler_params=_params("parallel"),
    )(h, gain, w_in, rope_cos, rope_sin)


def _in_proj_bwd(dh_mid, h, dq, dkv, dxg, rope_cos, rope_sin, gain, w_in):
    rows = h.shape[0]

    def body(dhm_ref, h_ref, dq_ref, dkv_ref, dxg_ref, c_ref, s_ref, g_ref, w_ref, dh_ref, dp_ref, gacc_ref):
        @pl.when(pl.program_id(0) == 0)
        def _():
            gacc_ref[...] = jnp.zeros_like(gacc_ref)

        cos = c_ref[...]
        sin = -s_ref[...]
        dp_ref[:, :ATTN_W] = (_rope(dq_ref[...], cos, sin) * ATTN_SCALE).astype(BF16)
        dp_ref[:, ATTN_W:ATTN_W + KV_W] = _rope(dkv_ref[:, :KV_W], cos, sin).astype(BF16)
        dp_ref[:, ATTN_W + KV_W:ATTN_W + 2 * KV_W] = dkv_ref[:, KV_W:].astype(BF16)
        dp_ref[:, ATTN_W + 2 * KV_W:] = dxg_ref[...]
        dz = _dot(dp_ref[...], w_ref[...])
        dx, dg = _rms_bwd(dz, h_ref[...], g_ref[...])
        dh_ref[...] = dhm_ref[...] + dx
        gacc_ref[...] += _put_rows(SUBLANES, [dg])

    tile = _proj_tile(rows)
    rs = functools.partial(_row_spec, tile=tile)
    return pl.pallas_call(
        body, name="in_proj_bwd", grid=(rows // tile,),
        in_specs=[rs(D_MODEL), rs(D_MODEL), rs(ATTN_W), rs(2 * KV_W), rs(2 * LRU_W),
                  rs(LANES), rs(LANES), _whole_spec(gain), _whole_spec(w_in)],
        out_specs=[rs(D_MODEL), rs(IN_COLS), pl.BlockSpec((SUBLANES, D_MODEL), lambda i: (0, 0))],
        out_shape=[_sds((rows, D_MODEL), F32), _sds((rows, IN_COLS), BF16), _sds((SUBLANES, D_MODEL), F32)],
        compiler_params=_params("arbitrary"),
    )(dh_mid, h, dq, dkv, dxg, rope_cos, rope_sin, gain, w_in)


def _kv_lane_variants(t, group):
    lane = lax.broadcasted_iota(jnp.int32, t.shape, 1)
    low = lane < 64
    swapped = pltpu.roll(t, 64, 1)
    if group == 0:
        lo, hi = jnp.where(low, t, 0.0), jnp.where(low, 0.0, swapped)
    else:
        lo, hi = jnp.where(low, swapped, 0.0), jnp.where(low, 0.0, t)
    return jnp.concatenate([lo, hi], axis=0).astype(BF16)


GROUP_ROWS = 2 * TIME_BLOCK
KEYS = 2 * TIME_BLOCK


def _window_mask(j):
    r = jnp.bitwise_and(lax.broadcasted_iota(jnp.int32, (GROUP_ROWS, KEYS), 0), TIME_BLOCK - 1)
    c = lax.broadcasted_iota(jnp.int32, (GROUP_ROWS, KEYS), 1)
    return (c > r) & (c <= r + TIME_BLOCK) & ((c >= TIME_BLOCK) | (j > 0))


def _group_rows(ref, ex, group):
    lo = 2 * group * LANES
    return jnp.concatenate([ref[ex, :, lo:lo + LANES], ref[ex, :, lo + LANES:lo + 2 * LANES]], axis=0)


def _per_head(sink_ref, group, half):
    upper = lax.broadcasted_iota(jnp.int32, (GROUP_ROWS, 1), 0) < TIME_BLOCK
    return jnp.where(upper, sink_ref[4 * group + half], sink_ref[4 * group + 2 + half])


def _by_example(a, batch):
    return a.reshape(batch, a.shape[0] // batch, a.shape[1])


def _attn_fwd(q, kv, sinks, batch, n_blocks):
    rows = q.shape[0]

    def body(sink_ref, q_ref, kvc_ref, kvp_ref, o_ref, lse_ref):
        j = pl.program_id(0)
        mask = _window_mask(j)
        for ex in range(batch):
            kv2 = jnp.concatenate([kvp_ref[ex], kvc_ref[ex]], axis=0).astype(F32)
            for group in range(2):
                k_cat = _kv_lane_variants(kv2[:, :KV_W], group)
                v_cat = _kv_lane_variants(kv2[:, KV_W:], group)
                s_all = _dot_nt(_group_rows(q_ref, ex, group), k_cat)
                probs = []
                for half in range(2):
                    sink = _per_head(sink_ref, group, half)
                    s = jnp.where(mask, s_all[:, half * KEYS:(half + 1) * KEYS], MASKED)
                    m = jnp.maximum(jnp.max(s, axis=1, keepdims=True), sink)
                    e = jnp.exp(s - m)
                    den = jnp.sum(e, axis=1, keepdims=True) + jnp.exp(sink - m)
                    probs.append((e / den).astype(BF16))
                    lse = m + jnp.log(den)
                    upper_head, lower_head = 4 * group + half, 4 * group + 2 + half
                    lse_ref[ex, :, upper_head:upper_head + 1] = lse[:TIME_BLOCK]
                    lse_ref[ex, :, lower_head:lower_head + 1] = lse[TIME_BLOCK:]
                out = _dot(jnp.concatenate(probs, axis=1), v_cat)
                o_ref[ex, :, 2 * group * LANES:(2 * group + 1) * LANES] = out[:TIME_BLOCK]
                o_ref[ex, :, (2 * group + 1) * LANES:(2 * group + 2) * LANES] = out[TIME_BLOCK:]

    def blk(width):
        return pl.BlockSpec((batch, TIME_BLOCK, width), lambda j: (0, j, 0))

    prev = pl.BlockSpec((batch, TIME_BLOCK, 2 * KV_W), lambda j: (0, jnp.maximum(j - 1, 0), 0))
    kv3 = _by_example(kv, batch)
    out, lse = pl.pallas_call(
        body, name="attn_fwd", grid=(n_blocks,),
        in_specs=[pl.BlockSpec(memory_space=pltpu.SMEM), blk(ATTN_W), blk(2 * KV_W), prev],
        out_specs=[blk(ATTN_W), blk(N_HEADS)],
        out_shape=[_sds((batch, rows // batch, ATTN_W), F32), _sds((batch, rows // batch, N_HEADS), F32)],
        compiler_params=_params("parallel"),
    )(sinks, _by_example(q, batch), kv3, kv3)
    return out.reshape(rows, ATTN_W), lse.reshape(rows, N_HEADS)


def _attn_bwd(q, kv, sinks, out, lse, d_out, batch, n_blocks):
    rows = q.shape[0]

    def body(sink_ref, q_ref, kvc_ref, kvp_ref, o_ref, do_ref, lse_ref, dq_ref, dkv_ref, dsink_ref, carry):
        j = pl.program_id(0)

        @pl.when(j == 0)
        def _():
            dsink_ref[...] = jnp.zeros_like(dsink_ref)

        def one_example(ex, dsink_vals):
            kv2 = jnp.concatenate([kvp_ref[ex], kvc_ref[ex]], axis=0).astype(F32)
            mask = _window_mask(j)
            lane = lax.broadcasted_iota(jnp.int32, (GROUP_ROWS, LANES), 1)
            low = lax.broadcasted_iota(jnp.int32, (KEYS, LANES), 1) < 64
            upper = lax.broadcasted_iota(jnp.int32, (GROUP_ROWS, 1), 0) < TIME_BLOCK
            lse_tile = lse_ref[ex]
            dk_tile = jnp.zeros((KEYS, KV_W), F32)
            dv_tile = jnp.zeros((KEYS, KV_W), F32)
            for group in range(2):
                k_cat = _kv_lane_variants(kv2[:, :KV_W], group)
                v_cat = _kv_lane_variants(kv2[:, KV_W:], group)
                q_rows = _group_rows(q_ref, ex, group)
                do_rows = _group_rows(do_ref, ex, group)
                do_b = do_rows.astype(BF16)
                od = do_rows * _group_rows(o_ref, ex, group)
                s_all = _dot_nt(q_rows, k_cat)
                dp_all = _dot_nt(do_b, v_cat)
                probs, dss = [], []
                for half in range(2):
                    heads = (4 * group + half, 4 * group + 2 + half)
                    sink = _per_head(sink_ref, group, half)
                    lse_h = jnp.concatenate([lse_tile[:, h:h + 1] for h in heads], axis=0)
                    in_half = (lane < 64) if half == 0 else (lane >= 64)
                    delta = jnp.sum(jnp.where(in_half, od, 0.0), axis=1, keepdims=True)
                    cols = slice(half * KEYS, (half + 1) * KEYS)
                    prob = jnp.exp(jnp.where(mask, s_all[:, cols], MASKED) - lse_h)
                    probs.append(prob.astype(BF16))
                    dss.append((prob * (dp_all[:, cols] - delta)).astype(BF16))
                    dsink = -jnp.exp(sink - lse_h) * delta
                    dsink_vals[heads[0]] = dsink_vals[heads[0]] + jnp.sum(jnp.where(upper, dsink, 0.0), axis=0, keepdims=True)
                    dsink_vals[heads[1]] = dsink_vals[heads[1]] + jnp.sum(jnp.where(upper, 0.0, dsink), axis=0, keepdims=True)
                ds = jnp.concatenate(dss, axis=1)
                dq_rows = _dot(ds, k_cat)
                dq_ref[ex, :, 2 * group * LANES:(2 * group + 1) * LANES] = dq_rows[:TIME_BLOCK]
                dq_ref[ex, :, (2 * group + 1) * LANES:(2 * group + 2) * LANES] = dq_rows[TIME_BLOCK:]
                dk_cat = _dot_tn(ds, q_rows)
                dv_cat = _dot_tn(jnp.concatenate(probs, axis=1), do_b)
                if group == 0:
                    dk_tile = dk_tile + jnp.where(low, dk_cat[:KEYS] + pltpu.roll(dk_cat[KEYS:], 64, 1), 0.0)
                    dv_tile = dv_tile + jnp.where(low, dv_cat[:KEYS] + pltpu.roll(dv_cat[KEYS:], 64, 1), 0.0)
                else:
                    dk_tile = dk_tile + jnp.where(low, 0.0, pltpu.roll(dk_cat[:KEYS], 64, 1) + dk_cat[KEYS:])
                    dv_tile = dv_tile + jnp.where(low, 0.0, pltpu.roll(dv_cat[:KEYS], 64, 1) + dv_cat[KEYS:])

            @pl.when(j > 0)
            def _():
                dkv_ref[ex, :, :KV_W] = carry[ex, :, :KV_W] + dk_tile[:TIME_BLOCK]
                dkv_ref[ex, :, KV_W:] = carry[ex, :, KV_W:] + dv_tile[:TIME_BLOCK]

            carry[ex, :, :KV_W] = dk_tile[TIME_BLOCK:]
            carry[ex, :, KV_W:] = dv_tile[TIME_BLOCK:]

        @pl.when(j < n_blocks)
        def _():
            dsink_vals = {head: jnp.zeros((1, 1), F32) for head in range(N_HEADS)}
            for ex in range(batch):
                one_example(ex, dsink_vals)
            rowid = lax.broadcasted_iota(jnp.int32, (N_HEADS, LANES), 0)
            upd = jnp.zeros((N_HEADS, LANES), F32)
            for head, val in dsink_vals.items():
                upd = upd + jnp.where(rowid == head, val, 0.0)
            dsink_ref[...] += upd

        @pl.when(j == n_blocks)
        def _():
            dkv_ref[...] = carry[...]

    last = n_blocks - 1

    def blk(width):
        return pl.BlockSpec((batch, TIME_BLOCK, width), lambda j: (0, jnp.minimum(j, last), 0))

    prev = pl.BlockSpec((batch, TIME_BLOCK, 2 * KV_W), lambda j: (0, jnp.maximum(jnp.minimum(j, last) - 1, 0), 0))
    dkv_spec = pl.BlockSpec((batch, TIME_BLOCK, 2 * KV_W), lambda j: (0, jnp.maximum(j - 1, 0), 0))
    kv3 = _by_example(kv, batch)
    dq, dkv, dsink = pl.pallas_call(
        body, name="attn_bwd", grid=(n_blocks + 1,),
        in_specs=[pl.BlockSpec(memory_space=pltpu.SMEM), blk(ATTN_W), blk(2 * KV_W), prev, blk(ATTN_W), blk(ATTN_W),
                  blk(N_HEADS)],
        out_specs=[blk(ATTN_W), dkv_spec, pl.BlockSpec((N_HEADS, LANES), lambda j: (0, 0))],
        out_shape=[_sds((batch, rows // batch, ATTN_W), F32), _sds((batch, rows // batch, 2 * KV_W), F32),
                   _sds((N_HEADS, LANES), F32)],
        scratch_shapes=[pltpu.VMEM((batch, TIME_BLOCK, 2 * KV_W), F32)],
        compiler_params=_params("arbitrary"),
    )(sinks, _by_example(q, batch), kv3, kv3, _by_example(out, batch), _by_example(d_out, batch), _by_example(lse, batch))
    return dq.reshape(rows, ATTN_W), dkv.reshape(rows, 2 * KV_W), dsink


def _conv_taps(xb, prev8):
    ext = jnp.concatenate([prev8, xb], axis=0)
    n = ext.shape[0]
    return [xb] + [pltpu.roll(ext, k, 0)[SUBLANES:n] for k in range(1, CONV_TAPS)]


def _lru_gates(xc, wa, ba, wx, bx, lam):
    xcb = xc.astype(BF16)
    r = _sigmoid(_dot(xcb, wa) + ba)
    i = _sigmoid(_dot(xcb, wx) + bx)
    sp = jnp.maximum(-lam, 0.0) + _log1p(jnp.exp(-jnp.abs(lam)))
    log_a = -LRU_C * r * sp
    a = jnp.exp(log_a)
    mult = jnp.sqrt(_one_minus_square(a, log_a))
    return xcb, r, i, sp, a, mult


def _scan_fwd(a, u, h_before):
    n, d = a.shape
    groups = n // SUBLANES
    a = a.reshape(groups, SUBLANES, d)
    u = u.reshape(groups, SUBLANES, d)
    sub = lax.broadcasted_iota(jnp.int32, a.shape, 1)
    s = 1
    while s < SUBLANES:
        valid = sub >= s
        u = jnp.where(valid, u + a * pltpu.roll(u, s, 1), u)
        a = jnp.where(valid, a * pltpu.roll(a, s, 1), a)
        s *= 2
    out, prev = [], h_before
    for g in range(groups):
        out.append(u[g] + a[g] * prev)
        prev = out[-1][SUBLANES - 1:SUBLANES, :]
    return jnp.concatenate(out, axis=0)


def _scan_rev(cf, g, d_after):
    n, d = g.shape
    groups = n // SUBLANES
    cf = cf.reshape(groups, SUBLANES, d)
    g = g.reshape(groups, SUBLANES, d)
    sub = lax.broadcasted_iota(jnp.int32, g.shape, 1)
    s = 1
    while s < SUBLANES:
        valid = sub + s < SUBLANES
        g = jnp.where(valid, g + cf * pltpu.roll(g, SUBLANES - s, 1), g)
        cf = jnp.where(valid, cf * pltpu.roll(cf, SUBLANES - s, 1), cf)
        s *= 2
    out, nxt = [None] * groups, d_after
    for k in reversed(range(groups)):
        out[k] = g[k] + cf[k] * nxt
        nxt = out[k][0:1, :]
    return jnp.concatenate(out, axis=0)


def _lru_fwd(xg, conv_w, conv_b, wa, ba, wx, bx, lam, batch, n_blocks):
    rows = xg.shape[0]

    def body(xg_ref, cw_ref, cb_ref, wa_ref, ba_ref, wx_ref, bx_ref, lam_ref, hs_ref, lru_ref, x_prev, h_carry):
        @pl.when(pl.program_id(0) == 0)
        def _():
            x_prev[...] = jnp.zeros_like(x_prev)
            h_carry[...] = jnp.zeros_like(h_carry)

        for ex in range(batch):
            xb = xg_ref[ex, :, :LRU_W]
            taps = _conv_taps(xb, x_prev[ex])
            xc = cb_ref[...] + sum(cw_ref[CONV_TAPS - 1 - k:CONV_TAPS - k, :] * taps[k] for k in range(CONV_TAPS))
            _, _, i, _, a, mult = _lru_gates(xc, wa_ref[...], ba_ref[...], wx_ref[...], bx_ref[...], lam_ref[...])
            h = _scan_fwd(a, mult * i * xc, h_carry[ex])
            hs_ref[ex] = h
            lru_ref[ex] = h * _gelu(xg_ref[ex, :, LRU_W:])
            h_carry[ex] = h[TIME_BLOCK - 1:TIME_BLOCK, :]
            x_prev[ex] = xb[TIME_BLOCK - SUBLANES:TIME_BLOCK, :]

    def blk(width):
        return pl.BlockSpec((batch, TIME_BLOCK, width), lambda j: (0, j, 0))

    small = [conv_w, conv_b, wa, ba, wx, bx, lam]
    hs, lru = pl.pallas_call(
        body, name="lru_fwd", grid=(n_blocks,),
        in_specs=[blk(2 * LRU_W)] + [_whole_spec(a) for a in small],
        out_specs=[blk(LRU_W), blk(LRU_W)],
        out_shape=[_sds((batch, rows // batch, LRU_W), F32)] * 2,
        scratch_shapes=[pltpu.VMEM((batch, SUBLANES, LRU_W), F32), pltpu.VMEM((batch, 1, LRU_W), F32)],
        compiler_params=_params("arbitrary"),
    )(_by_example(xg, batch), *small)
    return hs.reshape(rows, LRU_W), lru.reshape(rows, LRU_W)


def _lru_bwd(xg, hs, d_lru, conv_w, conv_b, wa, ba, wx, bx, lam, batch, n_blocks):
    rows = xg.shape[0]
    last_row = TIME_BLOCK - 1

    def body(xg_ref, xgh_ref, hs_ref, hsh_ref, dl_ref, cw_ref, cb_ref, wa_ref, ba_ref, wx_ref, bx_ref, lam_ref,
             dxg_ref, dwa_ref, dwx_ref, vec_ref, dh_carry, dxc_next):
        j = pl.program_id(0)

        @pl.when(j == 0)
        def _():
            dwa_ref[...] = jnp.zeros_like(dwa_ref)
            dwx_ref[...] = jnp.zeros_like(dwx_ref)
            vec_ref[...] = jnp.zeros_like(vec_ref)
            dh_carry[...] = jnp.zeros_like(dh_carry)
            dxc_next[...] = jnp.zeros_like(dxc_next)

        for ex in range(batch):
            one_example(ex, j, xg_ref, xgh_ref, hs_ref, hsh_ref, dl_ref, cw_ref, cb_ref, wa_ref, ba_ref, wx_ref, bx_ref,
                        lam_ref, dxg_ref, dwa_ref, dwx_ref, vec_ref, dh_carry, dxc_next)

    def one_example(ex, j, xg_ref, xgh_ref, hs_ref, hsh_ref, dl_ref, cw_ref, cb_ref, wa_ref, ba_ref, wx_ref, bx_ref, lam_ref,
                    dxg_ref, dwa_ref, dwx_ref, vec_ref, dh_carry, dxc_next):
        first = j == n_blocks - 1
        xb = xg_ref[ex, :, :LRU_W]
        prev8 = jnp.where(first, 0.0, xgh_ref[ex, :, :LRU_W])
        h_before = jnp.where(first, 0.0, hsh_ref[ex, SUBLANES - 1:SUBLANES, :])
        cw = cw_ref[...]
        lam = lam_ref[...]
        wa = wa_ref[...]
        wx = wx_ref[...]
        taps = _conv_taps(xb, prev8)
        xc = cb_ref[...] + sum(cw[CONV_TAPS - 1 - k:CONV_TAPS - k, :] * taps[k] for k in range(CONV_TAPS))
        xcb, r, i, sp, a, mult = _lru_gates(xc, wa, ba_ref[...], wx, bx_ref[...], lam)
        hs = hs_ref[ex]
        row = lax.broadcasted_iota(jnp.int32, hs.shape, 0)
        h_prev = jnp.where(row == 0, h_before, pltpu.roll(hs, 1, 0))
        dl = dl_ref[ex]
        gate, dgate = _gelu_and_grad(xg_ref[ex, :, LRU_W:])
        dxg_ref[ex, :, LRU_W:] = (dl * hs * dgate).astype(BF16)
        cf = jnp.where(row == last_row, 1.0, pltpu.roll(a, last_row, 0))
        dh = _scan_rev(cf, dl * gate, dh_carry[ex])
        dh_carry[ex] = a[0:1, :] * dh[0:1, :]
        dmult = dh * i * xc
        di = dh * mult * xc
        dxc = dh * mult * i
        dlog_a = dh * h_prev * a - dmult * (a * a / mult)
        dr = dlog_a * (-LRU_C * sp)
        dlam = jnp.sum(dlog_a * (-LRU_C * r), axis=0, keepdims=True) * (-_sigmoid(-lam))
        dpr = dr * r * (1.0 - r)
        dpi = di * i * (1.0 - i)
        dprb = dpr.astype(BF16)
        dpib = dpi.astype(BF16)
        dxc = dxc + _dot_nt(dprb, wa) + _dot_nt(dpib, wx)
        dwa_ref[...] += _dot_tn(xcb, dprb)
        dwx_ref[...] += _dot_tn(xcb, dpib)
        ext = jnp.concatenate([dxc, dxc_next[ex]], axis=0)
        n = ext.shape[0]
        dxb = cw[CONV_TAPS - 1:CONV_TAPS, :] * dxc
        for k in range(1, CONV_TAPS):
            dxb = dxb + cw[CONV_TAPS - 1 - k:CONV_TAPS - k, :] * pltpu.roll(ext, n - k, 0)[:TIME_BLOCK]
        dxg_ref[ex, :, :LRU_W] = dxb.astype(BF16)
        dxc_next[ex] = dxc[:SUBLANES, :]
        vecs = [jnp.sum(dxc * taps[CONV_TAPS - 1 - t], axis=0, keepdims=True) for t in range(CONV_TAPS)]
        vecs += [jnp.sum(dxc, axis=0, keepdims=True), jnp.sum(dpr, axis=0, keepdims=True),
                 jnp.sum(dpi, axis=0, keepdims=True), dlam]
        vec_ref[...] += _put_rows(SUBLANES, vecs)

    def tblk(j):
        return n_blocks - 1 - j

    def blk(width):
        return pl.BlockSpec((batch, TIME_BLOCK, width), lambda j: (0, tblk(j), 0))

    per8 = TIME_BLOCK // SUBLANES

    def halo(width):
        return pl.BlockSpec((batch, SUBLANES, width), lambda j: (0, jnp.maximum(per8 * tblk(j) - 1, 0), 0))

    small = [conv_w, conv_b, wa, ba, wx, bx, lam]
    acc = lambda shape: pl.BlockSpec(shape, lambda j: (0, 0))
    xg3, hs3 = _by_example(xg, batch), _by_example(hs, batch)
    dxg, dwa, dwx, vec = pl.pallas_call(
        body, name="lru_bwd", grid=(n_blocks,),
        in_specs=[blk(2 * LRU_W), halo(2 * LRU_W), blk(LRU_W), halo(LRU_W), blk(LRU_W)] + [_whole_spec(a) for a in small],
        out_specs=[blk(2 * LRU_W), acc((LRU_W, LRU_W)), acc((LRU_W, LRU_W)), acc((SUBLANES, LRU_W))],
        out_shape=[_sds((batch, rows // batch, 2 * LRU_W), BF16), _sds((LRU_W, LRU_W), F32), _sds((LRU_W, LRU_W), F32),
                   _sds((SUBLANES, LRU_W), F32)],
        scratch_shapes=[pltpu.VMEM((batch, 1, LRU_W), F32), pltpu.VMEM((batch, SUBLANES, LRU_W), F32)],
        compiler_params=_params("arbitrary"),
    )(xg3, xg3, hs3, hs3, _by_example(d_lru, batch), *small)
    return dxg.reshape(rows, 2 * LRU_W), dwa, dwx, vec


def _out_mlp_fwd(attn, lru, h, w_out, w_up4, w_down, g_attn, g_lru, g_post_mix, g_pre_mlp, g_post_mlp):
    rows = h.shape[0]

    def body(at_ref, lr_ref, h_ref, wo_ref, wu_ref, wd_ref, ga_ref, gl_ref, gp_ref, g1_ref, g2_ref,
             grp_ref, o_ref, hm_ref, z_ref, slope_ref, act_ref, y_ref, ho_ref):
        a = _rms(at_ref[...], ga_ref[...]).astype(BF16)
        l = _rms(lr_ref[...], gl_ref[...]).astype(BF16)
        grp_ref[:, :ATTN_W] = a
        grp_ref[:, ATTN_W:] = l
        o = _dot(a, wo_ref[:ATTN_W, :]) + _dot(l, wo_ref[ATTN_W:, :])
        o_ref[...] = o
        x = h_ref[...] + _rms(o, gp_ref[...])
        hm_ref[...] = x
        z = _rms(x, g1_ref[...]).astype(BF16)
        z_ref[...] = z
        y = jnp.zeros((ROW_TILE, D_MODEL), F32)
        for s in range(N_CHIPS):
            cols = slice(s * D_MODEL, (s + 1) * D_MODEL)
            r = jnp.maximum(_dot(z, wu_ref[s]), 0.0)
            slope_ref[:, cols] = (2.0 * r).astype(BF16)
            act = jnp.square(r).astype(BF16)
            act_ref[:, cols] = act
            y = y + _dot(act, wd_ref[cols, :])
        y_ref[...] = y
        ho_ref[...] = x + _rms(y, g2_ref[...])

    gains = [g_attn, g_lru, g_post_mix, g_pre_mlp, g_post_mlp]
    return pl.pallas_call(
        body, name="out_mlp_fwd", grid=(rows // ROW_TILE,),
        in_specs=[_row_spec(ATTN_W), _row_spec(LRU_W), _row_spec(D_MODEL), _resident_spec(w_out), _resident_spec(w_up4),
                  _resident_spec(w_down)] + [_whole_spec(g) for g in gains],
        out_specs=[_row_spec(D_MODEL), _row_spec(D_MODEL), _row_spec(D_MODEL), _row_spec(D_MODEL), _row_spec(D_FF),
                   _row_spec(D_FF), _row_spec(D_MODEL), _row_spec(D_MODEL)],
        out_shape=[_sds((rows, D_MODEL), BF16), _sds((rows, D_MODEL), F32), _sds((rows, D_MODEL), F32),
                   _sds((rows, D_MODEL), BF16), _sds((rows, D_FF), BF16), _sds((rows, D_FF), BF16),
                   _sds((rows, D_MODEL), F32), _sds((rows, D_MODEL), F32)],
        compiler_params=_params("parallel"),
    )(attn, lru, h, w_out, w_up4, w_down, *gains)


def _mlp_out_bwd(dh_out, h_mid, y, slope, o, attn, lru, w_out, w_up4, w_down, g_attn, g_lru, g_post_mix, g_pre_mlp,
                 g_post_mlp):
    rows = y.shape[0]

    def body(dh_ref, hm_ref, y_ref, slope_ref, o_ref, at_ref, lr_ref, wo_ref, wu_ref, wd_ref, ga_ref, gl_ref, gp_ref,
             g1_ref, g2_ref, dhm_ref, dy_ref, dup_ref, do_ref, dat_ref, dlr_ref, gacc_ref):
        @pl.when(pl.program_id(0) == 0)
        def _():
            gacc_ref[...] = jnp.zeros_like(gacc_ref)

        dh = dh_ref[...]
        dy, dg2 = _rms_bwd(dh, y_ref[...], g2_ref[...])
        dyb = dy.astype(BF16)
        dy_ref[...] = dyb
        dz = jnp.zeros((ROW_TILE, D_MODEL), F32)
        for s in range(N_CHIPS):
            cols = slice(s * D_MODEL, (s + 1) * D_MODEL)
            dact = _dot_nt(dyb, wd_ref[cols, :])
            dup = (dact * slope_ref[:, cols].astype(F32)).astype(BF16)
            dup_ref[:, cols] = dup
            dz = dz + _dot_nt(dup, wu_ref[s])
        dx, dg1 = _rms_bwd(dz, hm_ref[...], g1_ref[...])
        dhm = dh + dx
        dhm_ref[...] = dhm
        do, dgp = _rms_bwd(dhm, o_ref[...], gp_ref[...])
        dob = do.astype(BF16)
        do_ref[...] = dob
        dat, dga = _rms_bwd(_dot_nt(dob, wo_ref[:ATTN_W, :]), at_ref[...], ga_ref[...])
        dlr, dgl = _rms_bwd(_dot_nt(dob, wo_ref[ATTN_W:, :]), lr_ref[...], gl_ref[...])
        dat_ref[...] = dat
        dlr_ref[...] = dlr
        gacc_ref[...] += _put_rows(SUBLANES, [dg1, dg2, dgp, jnp.concatenate([dga, dgl], axis=1)])

    gains = [g_attn, g_lru, g_post_mix, g_pre_mlp, g_post_mlp]
    return pl.pallas_call(
        body, name="mlp_out_bwd", grid=(rows // ROW_TILE,),
        in_specs=[_row_spec(D_MODEL), _row_spec(D_MODEL), _row_spec(D_MODEL), _row_spec(D_FF), _row_spec(D_MODEL),
                  _row_spec(ATTN_W), _row_spec(LRU_W), _resident_spec(w_out), _resident_spec(w_up4), _resident_spec(w_down)]
                 + [_whole_spec(g) for g in gains],
        out_specs=[_row_spec(D_MODEL), _row_spec(D_MODEL), _row_spec(D_FF), _row_spec(D_MODEL), _row_spec(ATTN_W),
                   _row_spec(LRU_W), pl.BlockSpec((SUBLANES, D_MODEL), lambda i: (0, 0))],
        out_shape=[_sds((rows, D_MODEL), F32), _sds((rows, D_MODEL), BF16), _sds((rows, D_FF), BF16),
                   _sds((rows, D_MODEL), BF16), _sds((rows, ATTN_W), F32), _sds((rows, LRU_W), F32),
                   _sds((SUBLANES, D_MODEL), F32)],
        compiler_params=_params("arbitrary"),
    )(dh_out, h_mid, y, slope, o, attn, lru, w_out, w_up4, w_down, *gains)


def _matmul_tn(a, b, tm, tn, name, out_dtype, column_blocks=False):
    rows, m = a.shape
    n = b.shape[1]

    def body(a_ref, b_ref, o_ref):
        o_ref[...] = _dot_tn(a_ref[...], b_ref[...]).astype(out_dtype)

    if column_blocks:
        out_spec = pl.BlockSpec((None, tm, tn), lambda i, j: (j, i, 0))
        out_shape = _sds((n // tn, m, tn), out_dtype)
    else:
        out_spec = pl.BlockSpec((tm, tn), lambda i, j: (i, j))
        out_shape = _sds((m, n), out_dtype)
    return pl.pallas_call(
        body, name=name, grid=(m // tm, n // tn),
        in_specs=[pl.BlockSpec((rows, tm), lambda i, j: (0, i)), pl.BlockSpec((rows, tn), lambda i, j: (0, j))],
        out_specs=out_spec, out_shape=out_shape,
        compiler_params=_params("parallel", "parallel"),
    )(a, b)


LOSS_COLS = 256


def _loss_head(h, loss_target):
    batch, per_example, _ = h.shape
    seq = loss_target.shape[1]
    n_real = N_META + seq
    assert seq % SUBLANES == 0

    def body(h_ref, t_ref, dh_ref, l_ref):
        @pl.when((pl.program_id(0) == 0) & (pl.program_id(1) == 0))
        def _():
            l_ref[...] = jnp.zeros_like(l_ref)

        d = h_ref[N_META:n_real, :] - t_ref[...]
        dh_ref[:N_META, :] = jnp.zeros((N_META, LOSS_COLS), F32)
        dh_ref[N_META:n_real, :] = d * (1.0 / D_MODEL)
        dh_ref[n_real:, :] = jnp.zeros((per_example - n_real, LOSS_COLS), F32)
        l_ref[...] += jnp.sum(jnp.sum(d * d, axis=0, keepdims=True), axis=1, keepdims=True)

    blk = pl.BlockSpec((None, per_example, LOSS_COLS), lambda b, j: (b, 0, j))
    return pl.pallas_call(
        body, name="loss_head", grid=(batch, D_MODEL // LOSS_COLS),
        in_specs=[blk, pl.BlockSpec((None, seq, LOSS_COLS), lambda b, j: (b, 0, j))],
        out_specs=[blk, pl.BlockSpec((SUBLANES, LANES), lambda b, j: (0, 0))],
        out_shape=[_sds(h.shape, F32), _sds((SUBLANES, LANES), F32)],
        compiler_params=_params("arbitrary", "arbitrary"),
    )(h, loss_target)


def _meta_grad(dh0, batch, rows_per_example):
    per = rows_per_example // N_META

    def body(d_ref, o_ref):
        @pl.when(pl.program_id(0) == 0)
        def _():
            o_ref[...] = jnp.zeros_like(o_ref)

        o_ref[...] += d_ref[...]

    return pl.pallas_call(
        body, name="meta_grad", grid=(batch,),
        in_specs=[pl.BlockSpec((N_META, D_MODEL), lambda b: (b * per, 0))],
        out_specs=pl.BlockSpec((N_META, D_MODEL), lambda b: (0, 0)),
        out_shape=_sds((N_META, D_MODEL), F32),
        compiler_params=_params("arbitrary"),
    )(dh0)


def _elementwise_tile(rows, cols):
    tile = rows
    while tile * cols * 4 > (1 << 20) and tile % 16 == 0:
        tile //= 2
    return tile


def _sum_slots(buf, name):
    k, rows, cols = buf.shape
    tile = _elementwise_tile(rows, cols)

    def body(*refs):
        total = refs[0][...].astype(F32)
        for r in refs[1:k]:
            total = total + r[...].astype(F32)
        refs[k][...] = total

    def slot(s):
        return pl.BlockSpec((None, tile, cols), lambda i: (s, i, 0))

    return pl.pallas_call(
        body, name=name, grid=(rows // tile,),
        in_specs=[slot(s) for s in range(k)], out_specs=pl.BlockSpec((tile, cols), lambda i: (i, 0)),
        out_shape=_sds((rows, cols), F32), compiler_params=_params("parallel"),
    )(*([buf] * k))


def _sum_pieces(pieces, landed, chip, layer, n_layers, stacked):
    _, rows, cols = pieces.shape
    tile = _elementwise_tile(rows, cols)

    def body(chip_ref, own_ref, a_ref, b_ref, c_ref, *rest):
        o_ref, token = rest[-2:]
        total = ((own_ref[...].astype(F32) + a_ref[...].astype(F32)) + b_ref[...].astype(F32)) + c_ref[...].astype(F32)
        o_ref[...] = total.astype(BF16)
        token[...] = jnp.zeros_like(token)

    def slot(offset):
        return pl.BlockSpec((None, tile, cols), lambda i, chip_ref: ((chip_ref[0] + offset) % N_CHIPS, i, 0))

    carried = [] if stacked is None else [stacked]
    grid_spec = pltpu.PrefetchScalarGridSpec(
        num_scalar_prefetch=1, grid=(rows // tile,),
        in_specs=[slot(0), slot(1), slot(2), slot(3)] + [pl.BlockSpec(memory_space=pl.ANY)] * len(carried),
        out_specs=[pl.BlockSpec((None, tile, cols), lambda i, chip_ref: (layer, i, 0)),
                   pl.BlockSpec((SUBLANES, LANES), lambda i, chip_ref: (0, 0))])
    return pl.pallas_call(
        body, name="sum_grad_pieces", grid_spec=grid_spec,
        out_shape=[_sds((n_layers, rows, cols), BF16), _sds((SUBLANES, LANES), F32)],
        input_output_aliases={5: 0} if carried else {},
        compiler_params=_params("arbitrary"),
    )(chip, pieces, landed, landed, landed, *carried)


def _adamw(w, m, v, grads, name):
    flat = w.ndim == 2
    if flat:
        w, m, v = w[None], m[None], v[None]
        grads = [g[None] for g in grads]
    layers, rows, cols = w.shape
    tile = _elementwise_tile(rows, cols)
    ng = len(grads)
    m_scale = 1.0 - ADAM_B1 ** ADAM_STEP
    v_scale = 1.0 - ADAM_B2 ** ADAM_STEP

    def body(*refs):
        w_ref, m_ref, v_ref = refs[:3]
        g_refs = refs[3:3 + ng]
        g_out, d_out, m_out, v_out = refs[-4:]
        g = g_refs[0][...].astype(F32)
        for r in g_refs[1:]:
            g = g + r[...].astype(F32)
        m_new = ADAM_B1 * m_ref[...] + (1.0 - ADAM_B1) * g
        v_new = ADAM_B2 * v_ref[...] + (1.0 - ADAM_B2) * (g * g)
        m_hat = m_new / m_scale
        v_hat = v_new / v_scale
        g_out[...] = g
        d_out[...] = -ADAM_LR * (m_hat / (jnp.sqrt(v_hat) + ADAM_EPS) + ADAM_WD * w_ref[...])
        m_out[...] = m_new
        v_out[...] = v_new

    spec = pl.BlockSpec((None, tile, cols), lambda l, i: (l, i, 0))
    out = pl.pallas_call(
        body, name=name, grid=(layers, rows // tile),
        in_specs=[spec] * (3 + ng), out_specs=[spec] * 4, out_shape=[_sds(w.shape, F32)] * 4,
        compiler_params=_params("parallel", "parallel"),
    )(w, m, v, *grads)
    return [o[0] for o in out] if flat else out


def _position():
    return lax.axis_index("x"), lax.axis_index("y"), lax.axis_index("c")


def _other_chips(x, y):
    return [(1 - x, y), (x, 1 - y), (1 - x, 1 - y)]


def _exchange_chips(arrays, name, scatter):
    n = len(arrays)

    def body(*refs):
        src, dst = refs[:n], refs[n:2 * n]
        send_sems, recv_sems, local_sems = refs[2 * n:]
        x, y, c = _position()
        mine = 2 * x + y
        copies = []
        for i in range(n):
            own = src[i].at[mine] if scatter else src[i]
            copies.append(pltpu.make_async_copy(own, dst[i].at[mine], local_sems.at[i]))
        for k, (px, py) in enumerate(_other_chips(x, y)):
            for i in range(n):
                piece = src[i].at[2 * px + py] if scatter else src[i]
                copies.append(pltpu.make_async_remote_copy(
                    src_ref=piece, dst_ref=dst[i].at[mine], send_sem=send_sems.at[k, i], recv_sem=recv_sems.at[k, i],
                    device_id=(px, py, c), device_id_type=MESH))
        for cp in copies:
            cp.start()
        for cp in copies:
            cp.wait()

    def out_shape(a):
        return _sds(a.shape if scatter else (N_CHIPS,) + a.shape, a.dtype)

    return pl.pallas_call(
        body, name=name, in_specs=[HBM_SPEC] * n, out_specs=[HBM_SPEC] * n, out_shape=[out_shape(a) for a in arrays],
        scratch_shapes=[pltpu.SemaphoreType.DMA((N_CHIPS - 1, n)), pltpu.SemaphoreType.DMA((N_CHIPS - 1, n)),
                        pltpu.SemaphoreType.DMA((n,))],
    )(*arrays)


def _swap_sibling(arrays, name):
    n = len(arrays)

    def body(*refs):
        src, dst = refs[:n], refs[n:2 * n]
        send_sems, recv_sems = refs[2 * n:]
        x, y, c = _position()
        copies = [pltpu.make_async_remote_copy(
            src_ref=src[i], dst_ref=dst[i], send_sem=send_sems.at[i], recv_sem=recv_sems.at[i],
            device_id=(x, y, 1 - c), device_id_type=MESH) for i in range(n)]
        for cp in copies:
            cp.start()
        for cp in copies:
            cp.wait()

    return pl.pallas_call(
        body, name=name, in_specs=[HBM_SPEC] * n, out_specs=[HBM_SPEC] * n,
        out_shape=[_sds(a.shape, a.dtype) for a in arrays],
        scratch_shapes=[pltpu.SemaphoreType.DMA((n,)), pltpu.SemaphoreType.DMA((n,))],
    )(*arrays)


SEM_SPEC = pl.BlockSpec(memory_space=pltpu.SEMAPHORE)
ANY_SPEC = pl.BlockSpec(memory_space=pl.ANY)
IN_FLIGHT = pltpu.SideEffectType.DATAFLOW_SIDE_EFFECTING


def _peer_sems(all_devices):
    return pltpu.SemaphoreType.DMA(((N_DEV if all_devices else N_CHIPS) - 1,))


def _own_slot(all_devices, x, y, c):
    return 4 * x + 2 * y + c if all_devices else 2 * x + y


def _peers(all_devices, x, y, c):
    if not all_devices:
        return [((px, py, c), 2 * px + py) for px, py in _other_chips(x, y)]
    out = []
    for fx in range(2):
        for fy in range(2):
            for fc in range(2):
                if fx or fy or fc:
                    px, py, pc = (1 - x if fx else x), (1 - y if fy else y), (1 - c if fc else c)
                    out.append(((px, py, pc), 4 * px + 2 * py + pc))
    return out


def _in_hbm(a):
    return pltpu.with_memory_space_constraint(a, pltpu.HBM)


def _place_slot(a, slot, n_slots, name):
    rows, cols = a.shape
    tile = _elementwise_tile(rows, cols)

    def body(slot_ref, a_ref, o_ref):
        o_ref[...] = a_ref[...]

    grid_spec = pltpu.PrefetchScalarGridSpec(
        num_scalar_prefetch=1, grid=(rows // tile,),
        in_specs=[pl.BlockSpec((tile, cols), lambda i, slot_ref: (i, 0))],
        out_specs=pl.BlockSpec((None, tile, cols), lambda i, slot_ref: (slot_ref[0], i, 0)))
    return pl.pallas_call(
        body, name=name, grid_spec=grid_spec, out_shape=_sds((n_slots, rows, cols), a.dtype),
        compiler_params=_params("parallel"),
    )(slot, a)


def _place_shard(w, layer, chip, name):
    _, rows, cols = w.shape
    tile = _elementwise_tile(rows, cols)

    def body(chip_ref, w_ref, o_ref):
        o_ref[...] = w_ref[...].astype(BF16)

    grid_spec = pltpu.PrefetchScalarGridSpec(
        num_scalar_prefetch=1, grid=(rows // tile,),
        in_specs=[pl.BlockSpec((None, tile, cols), lambda i, chip_ref: (layer, i, 0))],
        out_specs=pl.BlockSpec((None, tile, cols), lambda i, chip_ref: (chip_ref[0], i, 0)))
    return pl.pallas_call(
        body, name=name, grid_spec=grid_spec, out_shape=_sds((N_CHIPS, rows, cols), BF16),
        compiler_params=_params("parallel"),
    )(chip, w)


def _rows_of(ref, slot, core, halves):
    if not halves:
        return ref.at[slot]
    half = ref.shape[1] // 2
    return ref.at[slot, pl.ds(pl.multiple_of(core * half, half), half)]


def _gather_start(bufs, after, name, all_devices=False, halves=False):
    n = len(bufs)

    def body(*refs):
        buf = refs[:n]
        send, recv = refs[n + 1:2 * n + 1], refs[2 * n + 1:3 * n + 1]
        token = refs[4 * n + 1]
        x, y, c = _position()
        mine = _own_slot(all_devices, x, y, c)
        for i in range(n):
            for k, (peer, _) in enumerate(_peers(all_devices, x, y, c)):
                rows = _rows_of(buf[i], mine, c, halves)
                pltpu.make_async_remote_copy(
                    src_ref=rows, dst_ref=rows, send_sem=send[i].at[k], recv_sem=recv[i].at[k],
                    device_id=peer, device_id_type=MESH).start()
        token[...] = jnp.zeros_like(token)

    sems = _peer_sems(all_devices)
    out = pl.pallas_call(
        body, name=name, in_specs=[HBM_SPEC] * n + [ANY_SPEC],
        out_specs=[SEM_SPEC] * (2 * n) + [HBM_SPEC] * n + [pl.BlockSpec(memory_space=pltpu.VMEM)],
        out_shape=[sems] * (2 * n) + [pltpu.HBM(b.shape, b.dtype) for b in bufs] + [_sds((SUBLANES, LANES), F32)],
        input_output_aliases={i: 2 * n + i for i in range(n)},
        compiler_params=pltpu.CompilerParams(has_side_effects=IN_FLIGHT),
    )(*[_in_hbm(b) for b in bufs], after)
    return [(out[i], out[n + i], out[2 * n + i]) for i in range(n)], out[3 * n]


def _gather_relay(flights, after, name):
    n = len(flights)

    def body(*refs):
        buf, send, recv = refs[:n], refs[n:2 * n], refs[2 * n:3 * n]
        send2, recv2 = refs[3 * n + 1:4 * n + 1], refs[4 * n + 1:5 * n + 1]
        token = refs[6 * n + 1]
        x, y, c = _position()
        mine = 2 * x + y
        for i in range(n):
            for k, (peer, slot) in enumerate(_peers(False, x, y, c)):
                arrived = _rows_of(buf[i], slot, c, True)
                cp = pltpu.make_async_remote_copy(
                    src_ref=_rows_of(buf[i], mine, c, True), dst_ref=arrived, send_sem=send[i].at[k], recv_sem=recv[i].at[k],
                    device_id=peer, device_id_type=MESH)
                cp.wait_send()
                cp.wait_recv()
                pltpu.make_async_remote_copy(
                    src_ref=arrived, dst_ref=arrived, send_sem=send2[i].at[k], recv_sem=recv2[i].at[k],
                    device_id=(x, y, 1 - c), device_id_type=MESH).start()
        token[...] = jnp.zeros_like(token)

    bufs = [f[2] for f in flights]
    sems = _peer_sems(False)
    out = pl.pallas_call(
        body, name=name, in_specs=[HBM_SPEC] * n + [SEM_SPEC] * (2 * n) + [ANY_SPEC],
        out_specs=[SEM_SPEC] * (2 * n) + [HBM_SPEC] * n + [pl.BlockSpec(memory_space=pltpu.VMEM)],
        out_shape=[sems] * (2 * n) + [pltpu.HBM(b.shape, b.dtype) for b in bufs] + [_sds((SUBLANES, LANES), F32)],
        input_output_aliases={i: 2 * n + i for i in range(n)},
        compiler_params=pltpu.CompilerParams(has_side_effects=IN_FLIGHT),
    )(*bufs, *[f[0] for f in flights], *[f[1] for f in flights], after)
    return [(out[i], out[n + i], out[2 * n + i]) for i in range(n)], out[3 * n]


def _gather_wait(send, recv, buf, after, name, all_devices=False, relayed=False):
    def body(buf_ref, send_ref, recv_ref, after_ref, out_ref):
        x, y, c = _position()
        mine = _own_slot(all_devices, x, y, c)
        for k, (peer, slot) in enumerate(_peers(all_devices, x, y, c)):
            if relayed:
                cp = pltpu.make_async_remote_copy(
                    src_ref=_rows_of(buf_ref, slot, c, True), dst_ref=_rows_of(buf_ref, slot, 1 - c, True),
                    send_sem=send_ref.at[k], recv_sem=recv_ref.at[k], device_id=(x, y, 1 - c), device_id_type=MESH)
            else:
                cp = pltpu.make_async_remote_copy(
                    src_ref=buf_ref.at[mine], dst_ref=buf_ref.at[slot], send_sem=send_ref.at[k], recv_sem=recv_ref.at[k],
                    device_id=peer, device_id_type=MESH)
            cp.wait_send()
            cp.wait_recv()

    return pl.pallas_call(
        body, name=name, in_specs=[HBM_SPEC, SEM_SPEC, SEM_SPEC, ANY_SPEC], out_specs=HBM_SPEC,
        out_shape=pltpu.HBM(buf.shape, buf.dtype), input_output_aliases={0: 0},
        compiler_params=pltpu.CompilerParams(has_side_effects=IN_FLIGHT),
    )(buf, send, recv, after)


def _scatter_start(pieces, name):
    n = len(pieces)

    def body(*refs):
        src = refs[:n]
        send, recv = refs[n:2 * n], refs[2 * n:3 * n]
        land = refs[4 * n:5 * n]
        token = refs[5 * n]
        x, y, c = _position()
        mine = 2 * x + y
        for i in range(n):
            for k, (px, py) in enumerate(_other_chips(x, y)):
                pltpu.make_async_remote_copy(
                    src_ref=src[i].at[2 * px + py], dst_ref=land[i].at[mine], send_sem=send[i].at[k], recv_sem=recv[i].at[k],
                    device_id=(px, py, c), device_id_type=MESH).start()
        token[...] = jnp.zeros_like(token)

    hbm = [pltpu.HBM(p.shape, p.dtype) for p in pieces]
    out = pl.pallas_call(
        body, name=name, in_specs=[HBM_SPEC] * n,
        out_specs=[SEM_SPEC] * (2 * n) + [HBM_SPEC] * (2 * n) + [pl.BlockSpec(memory_space=pltpu.VMEM)],
        out_shape=[_peer_sems(False)] * (2 * n) + hbm + hbm + [_sds((SUBLANES, LANES), F32)],
        input_output_aliases={i: 2 * n + i for i in range(n)},
        compiler_params=pltpu.CompilerParams(has_side_effects=IN_FLIGHT),
    )(*[_in_hbm(p) for p in pieces])
    return [(out[i], out[n + i], out[2 * n + i], out[3 * n + i]) for i in range(n)], out[4 * n]


def _scatter_wait(send, recv, pieces, land, after, name):
    def body(src_ref, land_ref, send_ref, recv_ref, after_ref, src_out, land_out):
        x, y, c = _position()
        for k, (px, py) in enumerate(_other_chips(x, y)):
            cp = pltpu.make_async_remote_copy(
                src_ref=src_ref.at[2 * px + py], dst_ref=land_ref.at[2 * px + py], send_sem=send_ref.at[k],
                recv_sem=recv_ref.at[k], device_id=(px, py, c), device_id_type=MESH)
            cp.wait_send()
            cp.wait_recv()

    return pl.pallas_call(
        body, name=name, in_specs=[HBM_SPEC, HBM_SPEC, SEM_SPEC, SEM_SPEC, ANY_SPEC], out_specs=[HBM_SPEC, HBM_SPEC],
        out_shape=[pltpu.HBM(pieces.shape, pieces.dtype), pltpu.HBM(land.shape, land.dtype)],
        input_output_aliases={0: 0, 1: 1},
        compiler_params=pltpu.CompilerParams(has_side_effects=IN_FLIGHT),
    )(pieces, land, send, recv, after)


def _rope_tables(batch, rows_per_example):
    inv_freq = ROPE_THETA ** (-jnp.arange(0, 64, 2, dtype=F32) / 64)
    ang = jnp.arange(rows_per_example, dtype=F32)[:, None] * inv_freq[None, :]
    cos, sin = jnp.cos(ang), jnp.sin(ang)
    cos128 = jnp.concatenate([cos, cos, cos, cos], axis=1)
    sin128 = jnp.concatenate([-sin, sin, -sin, sin], axis=1)
    return jnp.tile(cos128, (batch, 1)), jnp.tile(sin128, (batch, 1))


def _block_diagonal(w):
    eye = jnp.eye(LRU_BLOCKS, dtype=w.dtype)
    return (w[:, :, None, :] * eye[:, None, :, None]).reshape(LRU_W, LRU_W)


def _diagonal_blocks(dense):
    d4 = dense.reshape(LRU_BLOCKS, LRU_BLOCK, LRU_BLOCKS, LRU_BLOCK)
    return jnp.stack([d4[n, :, n, :] for n in range(LRU_BLOCKS)])


def _row(v):
    return v.reshape(1, -1)


def _local_step(x, loss_target, meta_tokens, small, depth, big_weight, on_layer_grads):
    batch, seq, _ = x.shape
    n_real = N_META + seq
    n_blocks = -(-n_real // TIME_BLOCK)
    per_example = n_blocks * TIME_BLOCK
    pad = per_example - n_real
    meta = jnp.broadcast_to(meta_tokens[None], (batch, N_META, D_MODEL))
    h = jnp.concatenate([meta, x, jnp.zeros((batch, pad, D_MODEL), F32)], axis=1).reshape(batch * per_example, D_MODEL)
    rope_cos, rope_sin = _rope_tables(batch, per_example)

    saved = []
    for l in range(depth):
        wa = _block_diagonal(small['gate_a_w'][l]).astype(BF16)
        wx = _block_diagonal(small['gate_x_w'][l]).astype(BF16)
        lru_small = (small['conv_w'][l], _row(small['conv_b'][l]), wa, _row(small['gate_a_b'][l]), wx,
                     _row(small['gate_x_b'][l]), _row(small['lru_lambda'][l]))
        w_in = big_weight('w_in', l, h)
        z1, q, kv, xg = _in_proj_fwd(h, _row(small['pre_mix_norm'][l]), w_in, rope_cos, rope_sin)
        attn, lse = _attn_fwd(q, kv, small['attn_sinks'][l], batch, n_blocks)
        hs, lru = _lru_fwd(xg, *lru_small, batch, n_blocks)
        w_out = big_weight('w_out', l, xg)
        w_up4 = big_weight('w_up', l, xg)
        w_down = big_weight('w_down', l, xg)
        gains = [_row(small[n][l]) for n in ('attn_group_norm', 'lru_group_norm', 'post_mix_norm', 'pre_mlp_norm',
                                             'post_mlp_norm')]
        groups, o, h_mid, z2, slope, act, y, h_out = _out_mlp_fwd(attn, lru, h, w_out, w_up4, w_down, *gains)
        saved.append(dict(h=h, z1=z1, q=q, kv=kv, xg=xg, attn=attn, lse=lse, hs=hs, lru=lru, groups=groups, o=o,
                          h_mid=h_mid, z2=z2, slope=slope, act=act, y=y, lru_small=lru_small, w_in=w_in, w_out=w_out,
                          w_up4=w_up4, w_down=w_down, gains=gains))
        h = h_out

    dh, sq_err = _loss_head(h.reshape(batch, per_example, D_MODEL), loss_target)
    dh = dh.reshape(batch * per_example, D_MODEL)

    gs = {n: [None] * depth for n in REPLICATED_NAMES + ('conv_w',)}
    handed_over = None
    for l in reversed(range(depth)):
        s = saved[l]

        def ordered_after(gain):
            return gain if handed_over is None else gain + handed_over[0, 0]

        gains = list(s['gains'])
        gains[4] = ordered_after(gains[4])
        dh_mid, dy, dup, do, d_attn, d_lru, g_rows = _mlp_out_bwd(dh, s['h_mid'], s['y'], s['slope'], s['o'], s['attn'],
                                                                s['lru'], s['w_out'], s['w_up4'], s['w_down'], *gains)
        handed_over = on_layer_grads(l, {
            'w_down': _matmul_tn(s['act'], dy, 512, D_MODEL, "grad_w_down", BF16),
            'w_up': _matmul_tn(s['z2'], dup, 512, D_MODEL, "grad_w_up", BF16, column_blocks=True)})
        lru_small = list(s['lru_small'])
        lru_small[1] = ordered_after(lru_small[1])
        dxg, dwa, dwx, g_lru = _lru_bwd(s['xg'], s['hs'], d_lru, *lru_small, batch, n_blocks)
        dq, dkv, dsink = _attn_bwd(s['q'], s['kv'], small['attn_sinks'][l], s['attn'], s['lse'], d_attn, batch, n_blocks)
        dh, dproj, g_in = _in_proj_bwd(dh_mid, s['h'], dq, dkv, dxg, rope_cos, rope_sin, _row(small['pre_mix_norm'][l]),
                                       s['w_in'])
        handed_over = on_layer_grads(l, {
            'w_out': _matmul_tn(s['groups'], do, 512, D_MODEL, "grad_w_out", BF16),
            'w_in': _matmul_tn(dproj, s['z1'], IN_COLS // 2, D_MODEL, "grad_w_in", BF16)})
        gs['pre_mlp_norm'][l], gs['post_mlp_norm'][l] = g_rows[0], g_rows[1]
        gs['post_mix_norm'][l] = g_rows[2]
        gs['attn_group_norm'][l], gs['lru_group_norm'][l] = g_rows[3, :ATTN_W], g_rows[3, ATTN_W:]
        gs['pre_mix_norm'][l] = g_in[0]
        gs['attn_sinks'][l] = dsink[:, 0]
        gs['conv_w'][l] = g_lru[:CONV_TAPS]
        gs['conv_b'][l], gs['gate_a_b'][l], gs['gate_x_b'][l], gs['lru_lambda'][l] = g_lru[4], g_lru[5], g_lru[6], g_lru[7]
        gs['gate_a_w'][l] = _diagonal_blocks(dwa)
        gs['gate_x_w'][l] = _diagonal_blocks(dwx)

    grad_x = dh.reshape(batch, per_example, D_MODEL)[:, N_META:n_real]
    grad_meta = _meta_grad(dh, batch, per_example)
    small_grads = {n: jnp.stack(v) for n, v in gs.items()}
    return sq_err, grad_x, grad_meta, small_grads, handed_over


PACK_UNIT = SUBLANES * LANES


def _size(shape):
    size = 1
    for d in shape:
        size *= d
    return size


def _pack(arrays):
    parts = []
    for a in arrays:
        flat = a.reshape(-1)
        padded = -(-flat.shape[0] // PACK_UNIT) * PACK_UNIT
        if padded != flat.shape[0]:
            flat = jnp.pad(flat, (0, padded - flat.shape[0]))
        parts.append(flat.reshape(-1, LANES))
    return jnp.concatenate(parts, axis=0)


def _unpack(buf, shapes):
    out, at = [], 0
    for shp in shapes:
        size = _size(shp)
        rows = -(-size // PACK_UNIT) * SUBLANES
        part = buf[at:at + rows]
        if rows * LANES != size:
            part = part.reshape(-1)[:size]
        out.append(part.reshape(shp))
        at += rows
    return out


def kernel(x, meta_tokens, pre_mix_norm, w_in, attn_sinks, conv_w, conv_b, gate_a_w, gate_a_b, gate_x_w, gate_x_b, lru_lambda, attn_group_norm, lru_group_norm, w_out, post_mix_norm, pre_mlp_norm, w_up, w_down, post_mlp_norm, loss_target, m_meta_tokens, m_pre_mix_norm, m_w_in, m_attn_sinks, m_conv_w, m_conv_b, m_gate_a_w, m_gate_a_b, m_gate_x_w, m_gate_x_b, m_lru_lambda, m_attn_group_norm, m_lru_group_norm, m_w_out, m_post_mix_norm, m_pre_mlp_norm, m_w_up, m_w_down, m_post_mlp_norm, v_meta_tokens, v_pre_mix_norm, v_w_in, v_attn_sinks, v_conv_w, v_conv_b, v_gate_a_w, v_gate_a_b, v_gate_x_w, v_gate_x_b, v_lru_lambda, v_attn_group_norm, v_lru_group_norm, v_w_out, v_post_mix_norm, v_pre_mlp_norm, v_w_up, v_w_down, v_post_mlp_norm):
    given = dict(locals())
    w = {n: given[n] for n in WEIGHT_NAMES}
    m = {n: given['m_' + n] for n in WEIGHT_NAMES}
    v = {n: given['v_' + n] for n in WEIGHT_NAMES}
    depth = w_in.shape[0]
    for d in (w, m, v):
        d['w_in'] = jnp.swapaxes(d['w_in'], 1, 2)
    chip = 2 * lax.axis_index("x") + lax.axis_index("y")
    chip1 = chip.reshape(1).astype(jnp.int32)

    in_flight = []
    all_started = chip1
    for l in range(depth):
        bufs = [_place_shard(w[n], l, chip1, "place_" + n) for n in BIG_NAMES]
        handles, all_started = _gather_start(bufs, all_started, "gather_start_%d" % l, halves=True)
        in_flight.append(dict(zip(BIG_NAMES, handles)))
    first_use = {'w_in': ('w_in',), 'w_out': ('w_out', 'w_up', 'w_down')}

    def big_weight(name, l, after):
        if l == 0 and name == 'w_in':
            after = all_started
        names = first_use.get(name, ()) if l == 0 else (BIG_NAMES if name == 'w_in' else ())
        if names:
            relayed, _ = _gather_relay([in_flight[l][n] for n in names], after, "gather_relay_%d_%s" % (l, name))
            in_flight[l].update(zip(names, relayed))
        send, recv, buf = in_flight[l][name]
        full = _gather_wait(send, recv, buf, after, "gather_wait_%d_%s" % (l, name), relayed=True)
        if name == 'w_in':
            return full.reshape(IN_COLS, D_MODEL)
        if name == 'w_out':
            return full.reshape(D_MODEL, D_MODEL)
        if name == 'w_down':
            return full.reshape(D_FF, D_MODEL)
        return full

    gathered = _exchange_chips([meta_tokens, conv_w], "gather_small_weights", scatter=False)
    full_meta = jnp.concatenate([gathered[0][s] for s in range(N_CHIPS)], axis=1)
    small = {n: w[n] for n in REPLICATED_NAMES}
    small['conv_w'] = jnp.concatenate([gathered[1][s] for s in range(N_CHIPS)], axis=2)

    scattering = [{} for _ in range(depth)]

    def on_layer_grads(l, big):
        names = list(big)
        pieces = [big[n].reshape(N_CHIPS, -1, D_MODEL) for n in names]
        handles, started = _scatter_start(pieces, "scatter_start_%d_%s" % (l, names[0]))
        scattering[l].update(zip(names, handles))
        return started

    sq_err, grad_x, grad_meta, small_grads, all_handed_over = _local_step(x, loss_target, full_meta, small, depth, big_weight,
                                                                           on_layer_grads)
    loss = lax.psum(sq_err[0, 0] * (0.5 / D_MODEL), ("x", "y", "c"))

    small_names = list(REPLICATED_NAMES) + list(COLUMN_SHARDED_SMALL)
    small_full = dict(small_grads)
    small_full['meta_tokens'] = grad_meta
    device1 = (2 * chip + lax.axis_index("c")).reshape(1).astype(jnp.int32)
    packed = _place_slot(_pack([small_full[n] for n in small_names]), device1, N_DEV, "place_small_grads")
    [small_flight], done = _gather_start([packed], all_handed_over, "gather_small_grads_start", all_devices=True)

    partial = {n: None for n in BIG_NAMES}
    for l in reversed(range(depth)):
        for n in BIG_NAMES:
            send, recv, pieces, land = scattering[l][n]
            pieces, land = _scatter_wait(send, recv, pieces, land, done, "scatter_wait_%d_%s" % (l, n))
            partial[n], done = _sum_pieces(pieces, land, chip1, l, depth, partial[n])
    sibling = _swap_sibling([partial[n] for n in BIG_NAMES], "swap_partial_grads")
    results = {n: _adamw(w[n], m[n], v[n], [partial[n], other], "adamw_" + n) for n, other in zip(BIG_NAMES, sibling)}

    slots = _gather_wait(*small_flight, results['w_down'][0], "gather_small_grads_wait", all_devices=True)
    summed = _unpack(_sum_slots(slots, "sum_small_grads"), [small_full[n].shape for n in small_names])
    grads = dict(zip(small_names, summed))
    grads['meta_tokens'] = lax.dynamic_slice_in_dim(grads['meta_tokens'], chip * meta_tokens.shape[1], meta_tokens.shape[1], 1)
    grads['conv_w'] = lax.dynamic_slice_in_dim(grads['conv_w'], chip * conv_w.shape[2], conv_w.shape[2], 2)

    out_g, out_d, out_m, out_v = {}, {}, {}, {}
    for n in BIG_NAMES:
        out_g[n], out_d[n], out_m[n], out_v[n] = [jnp.swapaxes(r, 1, 2) if n == 'w_in' else r for r in results[n][:4]]
    shapes = [w[n].shape for n in small_names]
    res = _adamw(_pack([w[n] for n in small_names]), _pack([m[n] for n in small_names]), _pack([v[n] for n in small_names]),
                 [_pack([grads[n] for n in small_names])], "adamw_small")
    for k, store in enumerate((out_g, out_d, out_m, out_v)):
        for n, a in zip(small_names, _unpack(res[k], shapes)):
            store[n] = a

    return (loss, grad_x, *[out_g[n] for n in WEIGHT_NAMES], *[out_d[n] for n in WEIGHT_NAMES],
            *[out_m[n] for n in WEIGHT_NAMES], *[out_v[n] for n in WEIGHT_NAMES])
```

```python
import functools

import jax
import jax.numpy as jnp
from jax import lax
from jax.experimental import pallas as pl
from jax.experimental.pallas import tpu as pltpu

F32 = jnp.float32
BF16 = jnp.bfloat16

D_MODEL = 1024
N_HEADS = 8
ATTN_W = 512
KV_W = 128
LRU_W = 512
LRU_BLOCKS = 8
LRU_BLOCK = 64
IN_COLS = 1792
D_FF = 4096
N_META = 16
CONV_TAPS = 4
LRU_C = 8.0
ROPE_THETA = 10000.0
EPS = 1e-6
ATTN_SCALE = 0.125

ADAM_LR = 0.001
ADAM_B1 = 0.9
ADAM_B2 = 0.999
ADAM_EPS = 1e-08
ADAM_WD = 0.01
ADAM_STEP = 10

N_CHIPS = 4
N_DEV = 8
TIME_BLOCK = 128
ROW_TILE = 256
PROJ_ROW_TILE = 544
LANES = 128
SUBLANES = 8
MASKED = -1e30
VMEM_LIMIT = 56 * 1024 * 1024

MESH = pl.DeviceIdType.MESH
HBM_SPEC = pl.BlockSpec(memory_space=pltpu.HBM)

WEIGHT_NAMES = ['meta_tokens', 'pre_mix_norm', 'w_in', 'attn_sinks', 'conv_w', 'conv_b', 'gate_a_w', 'gate_a_b',
                'gate_x_w', 'gate_x_b', 'lru_lambda', 'attn_group_norm', 'lru_group_norm', 'w_out', 'post_mix_norm',
                'pre_mlp_norm', 'w_up', 'w_down', 'post_mlp_norm']
BIG_NAMES = ('w_in', 'w_out', 'w_up', 'w_down')
COLUMN_SHARDED_SMALL = ('meta_tokens', 'conv_w')
REPLICATED_NAMES = tuple(n for n in WEIGHT_NAMES if n not in BIG_NAMES and n not in COLUMN_SHARDED_SMALL)


def _sds(shape, dtype):
    return jax.ShapeDtypeStruct(tuple(shape), dtype)


def _params(*sem):
    return pltpu.CompilerParams(dimension_semantics=sem, vmem_limit_bytes=VMEM_LIMIT)


def _row_spec(width, tile=ROW_TILE):
    return pl.BlockSpec((tile, width), lambda i: (i, 0))


def _proj_tile(rows):
    tile = PROJ_ROW_TILE
    while rows % tile:
        tile -= 16
    return tile


def _whole_spec(a):
    nd = a.ndim
    return pl.BlockSpec(a.shape, lambda *_: (0,) * nd)


def _resident_spec(a):
    nd = a.ndim
    return pl.BlockSpec(a.shape, lambda *_: (0,) * nd, pipeline_mode=pl.Buffered(1))


def _rms(x, g):
    r = lax.rsqrt(jnp.mean(x * x, axis=-1, keepdims=True) + EPS)
    return x * r * g


def _rms_bwd(dy, x, g):
    r = lax.rsqrt(jnp.mean(x * x, axis=-1, keepdims=True) + EPS)
    xh = x * r
    dg = jnp.sum(dy * xh, axis=0, keepdims=True)
    dxh = dy * g
    dx = r * (dxh - xh * jnp.mean(dxh * xh, axis=-1, keepdims=True))
    return dx, dg


def _rope(x, cos, sin_signed):
    width = x.shape[1]
    reps = width // LANES
    if reps > 1:
        cos = jnp.tile(cos, (1, reps))
        sin_signed = jnp.tile(sin_signed, (1, reps))
    lane = lax.broadcasted_iota(jnp.int32, x.shape, 1)
    first_half = jnp.bitwise_and(lane, 32) == 0
    other = jnp.where(first_half, pltpu.roll(x, width - 32, 1), pltpu.roll(x, 32, 1))
    return x * cos + other * sin_signed


def _sigmoid(x):
    return 1.0 / (1.0 + jnp.exp(-x))


def _log1p(e):
    return jnp.where(e < 1e-3, e * (1.0 - e * (0.5 - e * (1.0 / 3.0))), jnp.log(1.0 + e))


def _one_minus_square(a, log_a):
    x = 2.0 * log_a
    return jnp.where(x > -0.002, x * (-1.0 - 0.5 * x), 1.0 - a * a)


GELU_K = 0.7978845608028654
GELU_C = 0.044715


def _gelu(x):
    t = jnp.tanh(GELU_K * (x + GELU_C * x * x * x))
    return 0.5 * x * (1.0 + t)


def _gelu_and_grad(x):
    x2 = x * x
    t = jnp.tanh(GELU_K * (x + GELU_C * x * x2))
    val = 0.5 * x * (1.0 + t)
    grad = 0.5 * (1.0 + t) + 0.5 * x * (1.0 - t * t) * GELU_K * (1.0 + 3.0 * GELU_C * x2)
    return val, grad


def _dot(a, b):
    return jnp.dot(a, b, preferred_element_type=F32)


def _dot_nt(a, b):
    return lax.dot_general(a, b, (((1,), (1,)), ((), ())), preferred_element_type=F32)


def _dot_tn(a, b):
    return lax.dot_general(a, b, (((0,), (0,)), ((), ())), preferred_element_type=F32)


def _put_rows(rows_8, values):
    d = values[0].shape[1]
    rowid = lax.broadcasted_iota(jnp.int32, (rows_8, d), 0)
    out = jnp.zeros((rows_8, d), F32)
    for k, v in enumerate(values):
        out = out + jnp.where(rowid == k, v, 0.0)
    return out


def _in_proj_fwd(h, gain, w_in, rope_cos, rope_sin):
    rows = h.shape[0]

    def body(h_ref, g_ref, w_ref, c_ref, s_ref, z_ref, q_ref, kv_ref, xg_ref):
        z = _rms(h_ref[...], g_ref[...]).astype(BF16)
        z_ref[...] = z
        proj = _dot_nt(z, w_ref[...])
        cos = c_ref[...]
        sin = s_ref[...]
        q_ref[...] = (_rope(proj[:, :ATTN_W], cos, sin) * ATTN_SCALE).astype(BF16)
        kv_ref[:, :KV_W] = _rope(proj[:, ATTN_W:ATTN_W + KV_W], cos, sin).astype(BF16)
        kv_ref[:, KV_W:] = proj[:, ATTN_W + KV_W:ATTN_W + 2 * KV_W].astype(BF16)
        xg_ref[...] = proj[:, ATTN_W + 2 * KV_W:]

    tile = _proj_tile(rows)
    rs = functools.partial(_row_spec, tile=tile)
    return pl.pallas_call(
        body, name="in_proj_fwd", grid=(rows // tile,),
        in_specs=[rs(D_MODEL), _whole_spec(gain), _whole_spec(w_in), rs(LANES), rs(LANES)],
        out_specs=[rs(D_MODEL), rs(ATTN_W), rs(2 * KV_W), rs(2 * LRU_W)],
        out_shape=[_sds((rows, D_MODEL), BF16), _sds((rows, ATTN_W), BF16), _sds((rows, 2 * KV_W), BF16),
                   _sds((rows, 2 * LRU_W), F32)],
        compiler_params=_params("parallel"),
    )(h, gain, w_in, rope_cos, rope_sin)


def _in_proj_bwd(dh_mid, h, dq, dkv, dxg, rope_cos, rope_sin, gain, w_in):
    rows = h.shape[0]

    def body(dhm_ref, h_ref, dq_ref, dkv_ref, dxg_ref, c_ref, s_ref, g_ref, w_ref, dh_ref, dp_ref, gacc_ref):
        @pl.when(pl.program_id(0) == 0)
        def _():
            gacc_ref[...] = jnp.zeros_like(gacc_ref)

        cos = c_ref[...]
        sin = -s_ref[...]
        dp_ref[:, :ATTN_W] = (_rope(dq_ref[...], cos, sin) * ATTN_SCALE).astype(BF16)
        dp_ref[:, ATTN_W:ATTN_W + KV_W] = _rope(dkv_ref[:, :KV_W], cos, sin).astype(BF16)
        dp_ref[:, ATTN_W + KV_W:ATTN_W + 2 * KV_W] = dkv_ref[:, KV_W:].astype(BF16)
        dp_ref[:, ATTN_W + 2 * KV_W:] = dxg_ref[...]
        dz = _dot(dp_ref[...], w_ref[...])
        dx, dg = _rms_bwd(dz, h_ref[...], g_ref[...])
        dh_ref[...] = dhm_ref[...] + dx
        gacc_ref[...] += _put_rows(SUBLANES, [dg])

    tile = _proj_tile(rows)
    rs = functools.partial(_row_spec, tile=tile)
    return pl.pallas_call(
        body, name="in_proj_bwd", grid=(rows // tile,),
        in_specs=[rs(D_MODEL), rs(D_MODEL), rs(ATTN_W), rs(2 * KV_W), rs(2 * LRU_W),
                  rs(LANES), rs(LANES), _whole_spec(gain), _whole_spec(w_in)],
        out_specs=[rs(D_MODEL), rs(IN_COLS), pl.BlockSpec((SUBLANES, D_MODEL), lambda i: (0, 0))],
        out_shape=[_sds((rows, D_MODEL), F32), _sds((rows, IN_COLS), BF16), _sds((SUBLANES, D_MODEL), F32)],
        compiler_params=_params("arbitrary"),
    )(dh_mid, h, dq, dkv, dxg, rope_cos, rope_sin, gain, w_in)


def _kv_lane_variants(t, group):
    lane = lax.broadcasted_iota(jnp.int32, t.shape, 1)
    low = lane < 64
    swapped = pltpu.roll(t, 64, 1)
    if group == 0:
        lo, hi = jnp.where(low, t, 0.0), jnp.where(low, 0.0, swapped)
    else:
        lo, hi = jnp.where(low, swapped, 0.0), jnp.where(low, 0.0, t)
    return jnp.concatenate([lo, hi], axis=0).astype(BF16)


GROUP_ROWS = 2 * TIME_BLOCK
KEYS = 2 * TIME_BLOCK


def _window_mask(j):
    r = jnp.bitwise_and(lax.broadcasted_iota(jnp.int32, (GROUP_ROWS, KEYS), 0), TIME_BLOCK - 1)
    c = lax.broadcasted_iota(jnp.int32, (GROUP_ROWS, KEYS), 1)
    return (c > r) & (c <= r + TIME_BLOCK) & ((c >= TIME_BLOCK) | (j > 0))


def _group_rows(ref, ex, group):
    lo = 2 * group * LANES
    return jnp.concatenate([ref[ex, :, lo:lo + LANES], ref[ex, :, lo + LANES:lo + 2 * LANES]], axis=0)


def _per_head(sink_ref, group, half):
    upper = lax.broadcasted_iota(jnp.int32, (GROUP_ROWS, 1), 0) < TIME_BLOCK
    return jnp.where(upper, sink_ref[4 * group + half], sink_ref[4 * group + 2 + half])


def _by_example(a, batch):
    return a.reshape(batch, a.shape[0] // batch, a.shape[1])


def _attn_fwd(q, kv, sinks, batch, n_blocks):
    rows = q.shape[0]

    def body(sink_ref, q_ref, kvc_ref, kvp_ref, o_ref, lse_ref):
        j = pl.program_id(0)
        mask = _window_mask(j)
        for ex in range(batch):
            kv2 = jnp.concatenate([kvp_ref[ex], kvc_ref[ex]], axis=0).astype(F32)
            for group in range(2):
                k_cat = _kv_lane_variants(kv2[:, :KV_W], group)
                v_cat = _kv_lane_variants(kv2[:, KV_W:], group)
                s_all = _dot_nt(_group_rows(q_ref, ex, group), k_cat)
                probs = []
                for half in range(2):
                    sink = _per_head(sink_ref, group, half)
                    s = jnp.where(mask, s_all[:, half * KEYS:(half + 1) * KEYS], MASKED)
                    m = jnp.maximum(jnp.max(s, axis=1, keepdims=True), sink)
                    e = jnp.exp(s - m)
                    den = jnp.sum(e, axis=1, keepdims=True) + jnp.exp(sink - m)
                    probs.append((e / den).astype(BF16))
                    lse = m + jnp.log(den)
                    upper_head, lower_head = 4 * group + half, 4 * group + 2 + half
                    lse_ref[ex, :, upper_head:upper_head + 1] = lse[:TIME_BLOCK]
                    lse_ref[ex, :, lower_head:lower_head + 1] = lse[TIME_BLOCK:]
                out = _dot(jnp.concatenate(probs, axis=1), v_cat)
                o_ref[ex, :, 2 * group * LANES:(2 * group + 1) * LANES] = out[:TIME_BLOCK]
                o_ref[ex, :, (2 * group + 1) * LANES:(2 * group + 2) * LANES] = out[TIME_BLOCK:]

    def blk(width):
        return pl.BlockSpec((batch, TIME_BLOCK, width), lambda j: (0, j, 0))

    prev = pl.BlockSpec((batch, TIME_BLOCK, 2 * KV_W), lambda j: (0, jnp.maximum(j - 1, 0), 0))
    kv3 = _by_example(kv, batch)
    out, lse = pl.pallas_call(
        body, name="attn_fwd", grid=(n_blocks,),
        in_specs=[pl.BlockSpec(memory_space=pltpu.SMEM), blk(ATTN_W), blk(2 * KV_W), prev],
        out_specs=[blk(ATTN_W), blk(N_HEADS)],
        out_shape=[_sds((batch, rows // batch, ATTN_W), F32), _sds((batch, rows // batch, N_HEADS), F32)],
        compiler_params=_params("parallel"),
    )(sinks, _by_example(q, batch), kv3, kv3)
    return out.reshape(rows, ATTN_W), lse.reshape(rows, N_HEADS)


def _attn_bwd(q, kv, sinks, out, lse, d_out, batch, n_blocks):
    rows = q.shape[0]

    def body(sink_ref, q_ref, kvc_ref, kvp_ref, o_ref, do_ref, lse_ref, dq_ref, dkv_ref, dsink_ref, carry):
        j = pl.program_id(0)

        @pl.when(j == 0)
        def _():
            dsink_ref[...] = jnp.zeros_like(dsink_ref)

        def one_example(ex, dsink_vals):
            kv2 = jnp.concatenate([kvp_ref[ex], kvc_ref[ex]], axis=0).astype(F32)
            mask = _window_mask(j)
            lane = lax.broadcasted_iota(jnp.int32, (GROUP_ROWS, LANES), 1)
            low = lax.broadcasted_iota(jnp.int32, (KEYS, LANES), 1) < 64
            upper = lax.broadcasted_iota(jnp.int32, (GROUP_ROWS, 1), 0) < TIME_BLOCK
            lse_tile = lse_ref[ex]
            dk_tile = jnp.zeros((KEYS, KV_W), F32)
            dv_tile = jnp.zeros((KEYS, KV_W), F32)
            for group in range(2):
                k_cat = _kv_lane_variants(kv2[:, :KV_W], group)
                v_cat = _kv_lane_variants(kv2[:, KV_W:], group)
                q_rows = _group_rows(q_ref, ex, group)
                do_rows = _group_rows(do_ref, ex, group)
                do_b = do_rows.astype(BF16)
                od = do_rows * _group_rows(o_ref, ex, group)
                s_all = _dot_nt(q_rows, k_cat)
                dp_all = _dot_nt(do_b, v_cat)
                probs, dss = [], []
                for half in range(2):
                    heads = (4 * group + half, 4 * group + 2 + half)
                    sink = _per_head(sink_ref, group, half)
                    lse_h = jnp.concatenate([lse_tile[:, h:h + 1] for h in heads], axis=0)
                    in_half = (lane < 64) if half == 0 else (lane >= 64)
                    delta = jnp.sum(jnp.where(in_half, od, 0.0), axis=1, keepdims=True)
                    cols = slice(half * KEYS, (half + 1) * KEYS)
                    prob = jnp.exp(jnp.where(mask, s_all[:, cols], MASKED) - lse_h)
                    probs.append(prob.astype(BF16))
                    dss.append((prob * (dp_all[:, cols] - delta)).astype(BF16))
                    dsink = -jnp.exp(sink - lse_h) * delta
                    dsink_vals[heads[0]] = dsink_vals[heads[0]] + jnp.sum(jnp.where(upper, dsink, 0.0), axis=0, keepdims=True)
                    dsink_vals[heads[1]] = dsink_vals[heads[1]] + jnp.sum(jnp.where(upper, 0.0, dsink), axis=0, keepdims=True)
                ds = jnp.concatenate(dss, axis=1)
                dq_rows = _dot(ds, k_cat)
                dq_ref[ex, :, 2 * group * LANES:(2 * group + 1) * LANES] = dq_rows[:TIME_BLOCK]
                dq_ref[ex, :, (2 * group + 1) * LANES:(2 * group + 2) * LANES] = dq_rows[TIME_BLOCK:]
                dk_cat = _dot_tn(ds, q_rows)
                dv_cat = _dot_tn(jnp.concatenate(probs, axis=1), do_b)
                if group == 0:
                    dk_tile = dk_tile + jnp.where(low, dk_cat[:KEYS] + pltpu.roll(dk_cat[KEYS:], 64, 1), 0.0)
                    dv_tile = dv_tile + jnp.where(low, dv_cat[:KEYS] + pltpu.roll(dv_cat[KEYS:], 64, 1), 0.0)
                else:
                    dk_tile = dk_tile + jnp.where(low, 0.0, pltpu.roll(dk_cat[:KEYS], 64, 1) + dk_cat[KEYS:])
                    dv_tile = dv_tile + jnp.where(low, 0.0, pltpu.roll(dv_cat[:KEYS], 64, 1) + dv_cat[KEYS:])

            @pl.when(j > 0)
            def _():
                dkv_ref[ex, :, :KV_W] = carry[ex, :, :KV_W] + dk_tile[:TIME_BLOCK]
                dkv_ref[ex, :, KV_W:] = carry[ex, :, KV_W:] + dv_tile[:TIME_BLOCK]

            carry[ex, :, :KV_W] = dk_tile[TIME_BLOCK:]
            carry[ex, :, KV_W:] = dv_tile[TIME_BLOCK:]

        @pl.when(j < n_blocks)
        def _():
            dsink_vals = {head: jnp.zeros((1, 1), F32) for head in range(N_HEADS)}
            for ex in range(batch):
                one_example(ex, dsink_vals)
            rowid = lax.broadcasted_iota(jnp.int32, (N_HEADS, LANES), 0)
            upd = jnp.zeros((N_HEADS, LANES), F32)
            for head, val in dsink_vals.items():
                upd = upd + jnp.where(rowid == head, val, 0.0)
            dsink_ref[...] += upd

        @pl.when(j == n_blocks)
        def _():
            dkv_ref[...] = carry[...]

    last = n_blocks - 1

    def blk(width):
        return pl.BlockSpec((batch, TIME_BLOCK, width), lambda j: (0, jnp.minimum(j, last), 0))

    prev = pl.BlockSpec((batch, TIME_BLOCK, 2 * KV_W), lambda j: (0, jnp.maximum(jnp.minimum(j, last) - 1, 0), 0))
    dkv_spec = pl.BlockSpec((batch, TIME_BLOCK, 2 * KV_W), lambda j: (0, jnp.maximum(j - 1, 0), 0))
    kv3 = _by_example(kv, batch)
    dq, dkv, dsink = pl.pallas_call(
        body, name="attn_bwd", grid=(n_blocks + 1,),
        in_specs=[pl.BlockSpec(memory_space=pltpu.SMEM), blk(ATTN_W), blk(2 * KV_W), prev, blk(ATTN_W), blk(ATTN_W),
                  blk(N_HEADS)],
        out_specs=[blk(ATTN_W), dkv_spec, pl.BlockSpec((N_HEADS, LANES), lambda j: (0, 0))],
        out_shape=[_sds((batch, rows // batch, ATTN_W), F32), _sds((batch, rows // batch, 2 * KV_W), F32),
                   _sds((N_HEADS, LANES), F32)],
        scratch_shapes=[pltpu.VMEM((batch, TIME_BLOCK, 2 * KV_W), F32)],
        compiler_params=_params("arbitrary"),
    )(sinks, _by_example(q, batch), kv3, kv3, _by_example(out, batch), _by_example(d_out, batch), _by_example(lse, batch))
    return dq.reshape(rows, ATTN_W), dkv.reshape(rows, 2 * KV_W), dsink


def _conv_taps(xb, prev8):
    ext = jnp.concatenate([prev8, xb], axis=0)
    n = ext.shape[0]
    return [xb] + [pltpu.roll(ext, k, 0)[SUBLANES:n] for k in range(1, CONV_TAPS)]


def _lru_gates(xc, wa, ba, wx, bx, lam):
    xcb = xc.astype(BF16)
    r = _sigmoid(_dot(xcb, wa) + ba)
    i = _sigmoid(_dot(xcb, wx) + bx)
    sp = jnp.maximum(-lam, 0.0) + _log1p(jnp.exp(-jnp.abs(lam)))
    log_a = -LRU_C * r * sp
    a = jnp.exp(log_a)
    mult = jnp.sqrt(_one_minus_square(a, log_a))
    return xcb, r, i, sp, a, mult


def _scan_fwd(a, u, h_before):
    n, d = a.shape
    groups = n // SUBLANES
    a = a.reshape(groups, SUBLANES, d)
    u = u.reshape(groups, SUBLANES, d)
    sub = lax.broadcasted_iota(jnp.int32, a.shape, 1)
    s = 1
    while s < SUBLANES:
        valid = sub >= s
        u = jnp.where(valid, u + a * pltpu.roll(u, s, 1), u)
        a = jnp.where(valid, a * pltpu.roll(a, s, 1), a)
        s *= 2
    out, prev = [], h_before
    for g in range(groups):
        out.append(u[g] + a[g] * prev)
        prev = out[-1][SUBLANES - 1:SUBLANES, :]
    return jnp.concatenate(out, axis=0)


def _scan_rev(cf, g, d_after):
    n, d = g.shape
    groups = n // SUBLANES
    cf = cf.reshape(groups, SUBLANES, d)
    g = g.reshape(groups, SUBLANES, d)
    sub = lax.broadcasted_iota(jnp.int32, g.shape, 1)
    s = 1
    while s < SUBLANES:
        valid = sub + s < SUBLANES
        g = jnp.where(valid, g + cf * pltpu.roll(g, SUBLANES - s, 1), g)
        cf = jnp.where(valid, cf * pltpu.roll(cf, SUBLANES - s, 1), cf)
        s *= 2
    out, nxt = [None] * groups, d_after
    for k in reversed(range(groups)):
        out[k] = g[k] + cf[k] * nxt
        nxt = out[k][0:1, :]
    return jnp.concatenate(out, axis=0)


def _lru_fwd(xg, conv_w, conv_b, wa, ba, wx, bx, lam, batch, n_blocks):
    rows = xg.shape[0]

    def body(xg_ref, cw_ref, cb_ref, wa_ref, ba_ref, wx_ref, bx_ref, lam_ref, hs_ref, lru_ref, x_prev, h_carry):
        @pl.when(pl.program_id(0) == 0)
        def _():
            x_prev[...] = jnp.zeros_like(x_prev)
            h_carry[...] = jnp.zeros_like(h_carry)

        for ex in range(batch):
            xb = xg_ref[ex, :, :LRU_W]
            taps = _conv_taps(xb, x_prev[ex])
            xc = cb_ref[...] + sum(cw_ref[CONV_TAPS - 1 - k:CONV_TAPS - k, :] * taps[k] for k in range(CONV_TAPS))
            _, _, i, _, a, mult = _lru_gates(xc, wa_ref[...], ba_ref[...], wx_ref[...], bx_ref[...], lam_ref[...])
            h = _scan_fwd(a, mult * i * xc, h_carry[ex])
            hs_ref[ex] = h
            lru_ref[ex] = h * _gelu(xg_ref[ex, :, LRU_W:])
            h_carry[ex] = h[TIME_BLOCK - 1:TIME_BLOCK, :]
            x_prev[ex] = xb[TIME_BLOCK - SUBLANES:TIME_BLOCK, :]

    def blk(width):
        return pl.BlockSpec((batch, TIME_BLOCK, width), lambda j: (0, j, 0))

    small = [conv_w, conv_b, wa, ba, wx, bx, lam]
    hs, lru = pl.pallas_call(
        body, name="lru_fwd", grid=(n_blocks,),
        in_specs=[blk(2 * LRU_W)] + [_whole_spec(a) for a in small],
        out_specs=[blk(LRU_W), blk(LRU_W)],
        out_shape=[_sds((batch, rows // batch, LRU_W), F32)] * 2,
        scratch_shapes=[pltpu.VMEM((batch, SUBLANES, LRU_W), F32), pltpu.VMEM((batch, 1, LRU_W), F32)],
        compiler_params=_params("arbitrary"),
    )(_by_example(xg, batch), *small)
    return hs.reshape(rows, LRU_W), lru.reshape(rows, LRU_W)


def _lru_bwd(xg, hs, d_lru, conv_w, conv_b, wa, ba, wx, bx, lam, batch, n_blocks):
    rows = xg.shape[0]
    last_row = TIME_BLOCK - 1

    def body(xg_ref, xgh_ref, hs_ref, hsh_ref, dl_ref, cw_ref, cb_ref, wa_ref, ba_ref, wx_ref, bx_ref, lam_ref,
             dxg_ref, dwa_ref, dwx_ref, vec_ref, dh_carry, dxc_next):
        j = pl.program_id(0)

        @pl.when(j == 0)
        def _():
            dwa_ref[...] = jnp.zeros_like(dwa_ref)
            dwx_ref[...] = jnp.zeros_like(dwx_ref)
            vec_ref[...] = jnp.zeros_like(vec_ref)
            dh_carry[...] = jnp.zeros_like(dh_carry)
            dxc_next[...] = jnp.zeros_like(dxc_next)

        for ex in range(batch):
            one_example(ex, j, xg_ref, xgh_ref, hs_ref, hsh_ref, dl_ref, cw_ref, cb_ref, wa_ref, ba_ref, wx_ref, bx_ref,
                        lam_ref, dxg_ref, dwa_ref, dwx_ref, vec_ref, dh_carry, dxc_next)

    def one_example(ex, j, xg_ref, xgh_ref, hs_ref, hsh_ref, dl_ref, cw_ref, cb_ref, wa_ref, ba_ref, wx_ref, bx_ref, lam_ref,
                    dxg_ref, dwa_ref, dwx_ref, vec_ref, dh_carry, dxc_next):
        first = j == n_blocks - 1
        xb = xg_ref[ex, :, :LRU_W]
        prev8 = jnp.where(first, 0.0, xgh_ref[ex, :, :LRU_W])
        h_before = jnp.where(first, 0.0, hsh_ref[ex, SUBLANES - 1:SUBLANES, :])
        cw = cw_ref[...]
        lam = lam_ref[...]
        wa = wa_ref[...]
        wx = wx_ref[...]
        taps = _conv_taps(xb, prev8)
        xc = cb_ref[...] + sum(cw[CONV_TAPS - 1 - k:CONV_TAPS - k, :] * taps[k] for k in range(CONV_TAPS))
        xcb, r, i, sp, a, mult = _lru_gates(xc, wa, ba_ref[...], wx, bx_ref[...], lam)
        hs = hs_ref[ex]
        row = lax.broadcasted_iota(jnp.int32, hs.shape, 0)
        h_prev = jnp.where(row == 0, h_before, pltpu.roll(hs, 1, 0))
        dl = dl_ref[ex]
        gate, dgate = _gelu_and_grad(xg_ref[ex, :, LRU_W:])
        dxg_ref[ex, :, LRU_W:] = (dl * hs * dgate).astype(BF16)
        cf = jnp.where(row == last_row, 1.0, pltpu.roll(a, last_row, 0))
        dh = _scan_rev(cf, dl * gate, dh_carry[ex])
        dh_carry[ex] = a[0:1, :] * dh[0:1, :]
        dmult = dh * i * xc
        di = dh * mult * xc
        dxc = dh * mult * i
        dlog_a = dh * h_prev * a - dmult * (a * a / mult)
        dr = dlog_a * (-LRU_C * sp)
        dlam = jnp.sum(dlog_a * (-LRU_C * r), axis=0, keepdims=True) * (-_sigmoid(-lam))
        dpr = dr * r * (1.0 - r)
        dpi = di * i * (1.0 - i)
        dprb = dpr.astype(BF16)
        dpib = dpi.astype(BF16)
        dxc = dxc + _dot_nt(dprb, wa) + _dot_nt(dpib, wx)
        dwa_ref[...] += _dot_tn(xcb, dprb)
        dwx_ref[...] += _dot_tn(xcb, dpib)
        ext = jnp.concatenate([dxc, dxc_next[ex]], axis=0)
        n = ext.shape[0]
        dxb = cw[CONV_TAPS - 1:CONV_TAPS, :] * dxc
        for k in range(1, CONV_TAPS):
            dxb = dxb + cw[CONV_TAPS - 1 - k:CONV_TAPS - k, :] * pltpu.roll(ext, n - k, 0)[:TIME_BLOCK]
        dxg_ref[ex, :, :LRU_W] = dxb.astype(BF16)
        dxc_next[ex] = dxc[:SUBLANES, :]
        vecs = [jnp.sum(dxc * taps[CONV_TAPS - 1 - t], axis=0, keepdims=True) for t in range(CONV_TAPS)]
        vecs += [jnp.sum(dxc, axis=0, keepdims=True), jnp.sum(dpr, axis=0, keepdims=True),
                 jnp.sum(dpi, axis=0, keepdims=True), dlam]
        vec_ref[...] += _put_rows(SUBLANES, vecs)

    def tblk(j):
        return n_blocks - 1 - j

    def blk(width):
        return pl.BlockSpec((batch, TIME_BLOCK, width), lambda j: (0, tblk(j), 0))

    per8 = TIME_BLOCK // SUBLANES

    def halo(width):
        return pl.BlockSpec((batch, SUBLANES, width), lambda j: (0, jnp.maximum(per8 * tblk(j) - 1, 0), 0))

    small = [conv_w, conv_b, wa, ba, wx, bx, lam]
    acc = lambda shape: pl.BlockSpec(shape, lambda j: (0, 0))
    xg3, hs3 = _by_example(xg, batch), _by_example(hs, batch)
    dxg, dwa, dwx, vec = pl.pallas_call(
        body, name="lru_bwd", grid=(n_blocks,),
        in_specs=[blk(2 * LRU_W), halo(2 * LRU_W), blk(LRU_W), halo(LRU_W), blk(LRU_W)] + [_whole_spec(a) for a in small],
        out_specs=[blk(2 * LRU_W), acc((LRU_W, LRU_W)), acc((LRU_W, LRU_W)), acc((SUBLANES, LRU_W))],
        out_shape=[_sds((batch, rows // batch, 2 * LRU_W), BF16), _sds((LRU_W, LRU_W), F32), _sds((LRU_W, LRU_W), F32),
                   _sds((SUBLANES, LRU_W), F32)],
        scratch_shapes=[pltpu.VMEM((batch, 1, LRU_W), F32), pltpu.VMEM((batch, SUBLANES, LRU_W), F32)],
        compiler_params=_params("arbitrary"),
    )(xg3, xg3, hs3, hs3, _by_example(d_lru, batch), *small)
    return dxg.reshape(rows, 2 * LRU_W), dwa, dwx, vec


def _out_mlp_fwd(attn, lru, h, w_out, w_up4, w_down, g_attn, g_lru, g_post_mix, g_pre_mlp, g_post_mlp):
    rows = h.shape[0]

    def body(at_ref, lr_ref, h_ref, wo_ref, wu_ref, wd_ref, ga_ref, gl_ref, gp_ref, g1_ref, g2_ref,
             grp_ref, o_ref, hm_ref, z_ref, slope_ref, act_ref, y_ref, ho_ref):
        a = _rms(at_ref[...], ga_ref[...]).astype(BF16)
        l = _rms(lr_ref[...], gl_ref[...]).astype(BF16)
        grp_ref[:, :ATTN_W] = a
        grp_ref[:, ATTN_W:] = l
        o = _dot(a, wo_ref[:ATTN_W, :]) + _dot(l, wo_ref[ATTN_W:, :])
        o_ref[...] = o
        x = h_ref[...] + _rms(o, gp_ref[...])
        hm_ref[...] = x
        z = _rms(x, g1_ref[...]).astype(BF16)
        z_ref[...] = z
        y = jnp.zeros((ROW_TILE, D_MODEL), F32)
        for s in range(N_CHIPS):
            cols = slice(s * D_MODEL, (s + 1) * D_MODEL)
            r = jnp.maximum(_dot(z, wu_ref[s]), 0.0)
            slope_ref[:, cols] = (2.0 * r).astype(BF16)
            act = jnp.square(r).astype(BF16)
            act_ref[:, cols] = act
            y = y + _dot(act, wd_ref[cols, :])
        y_ref[...] = y
        ho_ref[...] = x + _rms(y, g2_ref[...])

    gains = [g_attn, g_lru, g_post_mix, g_pre_mlp, g_post_mlp]
    return pl.pallas_call(
        body, name="out_mlp_fwd", grid=(rows // ROW_TILE,),
        in_specs=[_row_spec(ATTN_W), _row_spec(LRU_W), _row_spec(D_MODEL), _resident_spec(w_out), _resident_spec(w_up4),
                  _resident_spec(w_down)] + [_whole_spec(g) for g in gains],
        out_specs=[_row_spec(D_MODEL), _row_spec(D_MODEL), _row_spec(D_MODEL), _row_spec(D_MODEL), _row_spec(D_FF),
                   _row_spec(D_FF), _row_spec(D_MODEL), _row_spec(D_MODEL)],
        out_shape=[_sds((rows, D_MODEL), BF16), _sds((rows, D_MODEL), F32), _sds((rows, D_MODEL), F32),
                   _sds((rows, D_MODEL), BF16), _sds((rows, D_FF), BF16), _sds((rows, D_FF), BF16),
                   _sds((rows, D_MODEL), F32), _sds((rows, D_MODEL), F32)],
        compiler_params=_params("parallel"),
    )(attn, lru, h, w_out, w_up4, w_down, *gains)


def _mlp_out_bwd(dh_out, h_mid, y, slope, o, attn, lru, w_out, w_up4, w_down, g_attn, g_lru, g_post_mix, g_pre_mlp,
                 g_post_mlp):
    rows = y.shape[0]

    def body(dh_ref, hm_ref, y_ref, slope_ref, o_ref, at_ref, lr_ref, wo_ref, wu_ref, wd_ref, ga_ref, gl_ref, gp_ref,
             g1_ref, g2_ref, dhm_ref, dy_ref, dup_ref, do_ref, dat_ref, dlr_ref, gacc_ref):
        @pl.when(pl.program_id(0) == 0)
        def _():
            gacc_ref[...] = jnp.zeros_like(gacc_ref)

        dh = dh_ref[...]
        dy, dg2 = _rms_bwd(dh, y_ref[...], g2_ref[...])
        dyb = dy.astype(BF16)
        dy_ref[...] = dyb
        dz = jnp.zeros((ROW_TILE, D_MODEL), F32)
        for s in range(N_CHIPS):
            cols = slice(s * D_MODEL, (s + 1) * D_MODEL)
            dact = _dot_nt(dyb, wd_ref[cols, :])
            dup = (dact * slope_ref[:, cols].astype(F32)).astype(BF16)
            dup_ref[:, cols] = dup
            dz = dz + _dot_nt(dup, wu_ref[s])
        dx, dg1 = _rms_bwd(dz, hm_ref[...], g1_ref[...])
        dhm = dh + dx
        dhm_ref[...] = dhm
        do, dgp = _rms_bwd(dhm, o_ref[...], gp_ref[...])
        dob = do.astype(BF16)
        do_ref[...] = dob
        dat, dga = _rms_bwd(_dot_nt(dob, wo_ref[:ATTN_W, :]), at_ref[...], ga_ref[...])
        dlr, dgl = _rms_bwd(_dot_nt(dob, wo_ref[ATTN_W:, :]), lr_ref[...], gl_ref[...])
        dat_ref[...] = dat
        dlr_ref[...] = dlr
        gacc_ref[...] += _put_rows(SUBLANES, [dg1, dg2, dgp, jnp.concatenate([dga, dgl], axis=1)])

    gains = [g_attn, g_lru, g_post_mix, g_pre_mlp, g_post_mlp]
    return pl.pallas_call(
        body, name="mlp_out_bwd", grid=(rows // ROW_TILE,),
        in_specs=[_row_spec(D_MODEL), _row_spec(D_MODEL), _row_spec(D_MODEL), _row_spec(D_FF), _row_spec(D_MODEL),
                  _row_spec(ATTN_W), _row_spec(LRU_W), _resident_spec(w_out), _resident_spec(w_up4), _resident_spec(w_down)]
                 + [_whole_spec(g) for g in gains],
        out_specs=[_row_spec(D_MODEL), _row_spec(D_MODEL), _row_spec(D_FF), _row_spec(D_MODEL), _row_spec(ATTN_W),
                   _row_spec(LRU_W), pl.BlockSpec((SUBLANES, D_MODEL), lambda i: (0, 0))],
        out_shape=[_sds((rows, D_MODEL), F32), _sds((rows, D_MODEL), BF16), _sds((rows, D_FF), BF16),
                   _sds((rows, D_MODEL), BF16), _sds((rows, ATTN_W), F32), _sds((rows, LRU_W), F32),
                   _sds((SUBLANES, D_MODEL), F32)],
        compiler_params=_params("arbitrary"),
    )(dh_out, h_mid, y, slope, o, attn, lru, w_out, w_up4, w_down, *gains)


def _matmul_tn(a, b, tm, tn, name, out_dtype, column_blocks=False):
    rows, m = a.shape
    n = b.shape[1]

    def body(a_ref, b_ref, o_ref):
        o_ref[...] = _dot_tn(a_ref[...], b_ref[...]).astype(out_dtype)

    if column_blocks:
        out_spec = pl.BlockSpec((None, tm, tn), lambda i, j: (j, i, 0))
        out_shape = _sds((n // tn, m, tn), out_dtype)
    else:
        out_spec = pl.BlockSpec((tm, tn), lambda i, j: (i, j))
        out_shape = _sds((m, n), out_dtype)
    return pl.pallas_call(
        body, name=name, grid=(m // tm, n // tn),
        in_specs=[pl.BlockSpec((rows, tm), lambda i, j: (0, i)), pl.BlockSpec((rows, tn), lambda i, j: (0, j))],
        out_specs=out_spec, out_shape=out_shape,
        compiler_params=_params("parallel", "parallel"),
    )(a, b)


LOSS_COLS = 256


def _loss_head(h, loss_target):
    batch, per_example, _ = h.shape
    seq = loss_target.shape[1]
    n_real = N_META + seq
    assert seq % SUBLANES == 0

    def body(h_ref, t_ref, dh_ref, l_ref):
        @pl.when((pl.program_id(0) == 0) & (pl.program_id(1) == 0))
        def _():
            l_ref[...] = jnp.zeros_like(l_ref)

        d = h_ref[N_META:n_real, :] - t_ref[...]
        dh_ref[:N_META, :] = jnp.zeros((N_META, LOSS_COLS), F32)
        dh_ref[N_META:n_real, :] = d * (1.0 / D_MODEL)
        dh_ref[n_real:, :] = jnp.zeros((per_example - n_real, LOSS_COLS), F32)
        l_ref[...] += jnp.sum(jnp.sum(d * d, axis=0, keepdims=True), axis=1, keepdims=True)

    blk = pl.BlockSpec((None, per_example, LOSS_COLS), lambda b, j: (b, 0, j))
    return pl.pallas_call(
        body, name="loss_head", grid=(batch, D_MODEL // LOSS_COLS),
        in_specs=[blk, pl.BlockSpec((None, seq, LOSS_COLS), lambda b, j: (b, 0, j))],
        out_specs=[blk, pl.BlockSpec((SUBLANES, LANES), lambda b, j: (0, 0))],
        out_shape=[_sds(h.shape, F32), _sds((SUBLANES, LANES), F32)],
        compiler_params=_params("arbitrary", "arbitrary"),
    )(h, loss_target)


def _meta_grad(dh0, batch, rows_per_example):
    per = rows_per_example // N_META

    def body(d_ref, o_ref):
        @pl.when(pl.program_id(0) == 0)
        def _():
            o_ref[...] = jnp.zeros_like(o_ref)

        o_ref[...] += d_ref[...]

    return pl.pallas_call(
        body, name="meta_grad", grid=(batch,),
        in_specs=[pl.BlockSpec((N_META, D_MODEL), lambda b: (b * per, 0))],
        out_specs=pl.BlockSpec((N_META, D_MODEL), lambda b: (0, 0)),
        out_shape=_sds((N_META, D_MODEL), F32),
        compiler_params=_params("arbitrary"),
    )(dh0)


def _elementwise_tile(rows, cols):
    tile = rows
    while tile * cols * 4 > (1 << 20) and tile % 16 == 0:
        tile //= 2
    return tile


def _sum_slots(buf, name):
    k, rows, cols = buf.shape
    tile = _elementwise_tile(rows, cols)

    def body(*refs):
        total = refs[0][...].astype(F32)
        for r in refs[1:k]:
            total = total + r[...].astype(F32)
        refs[k][...] = total

    def slot(s):
        return pl.BlockSpec((None, tile, cols), lambda i: (s, i, 0))

    return pl.pallas_call(
        body, name=name, grid=(rows // tile,),
        in_specs=[slot(s) for s in range(k)], out_specs=pl.BlockSpec((tile, cols), lambda i: (i, 0)),
        out_shape=_sds((rows, cols), F32), compiler_params=_params("parallel"),
    )(*([buf] * k))


def _sum_pieces(pieces, landed, chip, layer, n_layers, stacked):
    _, rows, cols = pieces.shape
    tile = _elementwise_tile(rows, cols)

    def body(chip_ref, own_ref, a_ref, b_ref, c_ref, *rest):
        o_ref, token = rest[-2:]
        total = ((own_ref[...].astype(F32) + a_ref[...].astype(F32)) + b_ref[...].astype(F32)) + c_ref[...].astype(F32)
        o_ref[...] = total.astype(BF16)
        token[...] = jnp.zeros_like(token)

    def slot(offset):
        return pl.BlockSpec((None, tile, cols), lambda i, chip_ref: ((chip_ref[0] + offset) % N_CHIPS, i, 0))

    carried = [] if stacked is None else [stacked]
    grid_spec = pltpu.PrefetchScalarGridSpec(
        num_scalar_prefetch=1, grid=(rows // tile,),
        in_specs=[slot(0), slot(1), slot(2), slot(3)] + [pl.BlockSpec(memory_space=pl.ANY)] * len(carried),
        out_specs=[pl.BlockSpec((None, tile, cols), lambda i, chip_ref: (layer, i, 0)),
                   pl.BlockSpec((SUBLANES, LANES), lambda i, chip_ref: (0, 0))])
    return pl.pallas_call(
        body, name="sum_grad_pieces", grid_spec=grid_spec,
        out_shape=[_sds((n_layers, rows, cols), BF16), _sds((SUBLANES, LANES), F32)],
        input_output_aliases={5: 0} if carried else {},
        compiler_params=_params("arbitrary"),
    )(chip, pieces, landed, landed, landed, *carried)


def _adamw(w, m, v, grads, name):
    flat = w.ndim == 2
    if flat:
        w, m, v = w[None], m[None], v[None]
        grads = [g[None] for g in grads]
    layers, rows, cols = w.shape
    tile = _elementwise_tile(rows, cols)
    ng = len(grads)
    m_scale = 1.0 - ADAM_B1 ** ADAM_STEP
    v_scale = 1.0 - ADAM_B2 ** ADAM_STEP

    def body(*refs):
        w_ref, m_ref, v_ref = refs[:3]
        g_refs = refs[3:3 + ng]
        g_out, d_out, m_out, v_out = refs[-4:]
        g = g_refs[0][...].astype(F32)
        for r in g_refs[1:]:
            g = g + r[...].astype(F32)
        m_new = ADAM_B1 * m_ref[...] + (1.0 - ADAM_B1) * g
        v_new = ADAM_B2 * v_ref[...] + (1.0 - ADAM_B2) * (g * g)
        m_hat = m_new / m_scale
        v_hat = v_new / v_scale
        g_out[...] = g
        d_out[...] = -ADAM_LR * (m_hat / (jnp.sqrt(v_hat) + ADAM_EPS) + ADAM_WD * w_ref[...])
        m_out[...] = m_new
        v_out[...] = v_new

    spec = pl.BlockSpec((None, tile, cols), lambda l, i: (l, i, 0))
    out = pl.pallas_call(
        body, name=name, grid=(layers, rows // tile),
        in_specs=[spec] * (3 + ng), out_specs=[spec] * 4, out_shape=[_sds(w.shape, F32)] * 4,
        compiler_params=_params("parallel", "parallel"),
    )(w, m, v, *grads)
    return [o[0] for o in out] if flat else out


def _position():
    return lax.axis_index("x"), lax.axis_index("y"), lax.axis_index("c")


def _other_chips(x, y):
    return [(1 - x, y), (x, 1 - y), (1 - x, 1 - y)]


def _exchange_chips(arrays, name, scatter):
    n = len(arrays)

    def body(*refs):
        src, dst = refs[:n], refs[n:2 * n]
        send_sems, recv_sems, local_sems = refs[2 * n:]
        x, y, c = _position()
        mine = 2 * x + y
        copies = []
        for i in range(n):
            own = src[i].at[mine] if scatter else src[i]
            copies.append(pltpu.make_async_copy(own, dst[i].at[mine], local_sems.at[i]))
        for k, (px, py) in enumerate(_other_chips(x, y)):
            for i in range(n):
                piece = src[i].at[2 * px + py] if scatter else src[i]
                copies.append(pltpu.make_async_remote_copy(
                    src_ref=piece, dst_ref=dst[i].at[mine], send_sem=send_sems.at[k, i], recv_sem=recv_sems.at[k, i],
                    device_id=(px, py, c), device_id_type=MESH))
        for cp in copies:
            cp.start()
        for cp in copies:
            cp.wait()

    def out_shape(a):
        return _sds(a.shape if scatter else (N_CHIPS,) + a.shape, a.dtype)

    return pl.pallas_call(
        body, name=name, in_specs=[HBM_SPEC] * n, out_specs=[HBM_SPEC] * n, out_shape=[out_shape(a) for a in arrays],
        scratch_shapes=[pltpu.SemaphoreType.DMA((N_CHIPS - 1, n)), pltpu.SemaphoreType.DMA((N_CHIPS - 1, n)),
                        pltpu.SemaphoreType.DMA((n,))],
    )(*arrays)


def _swap_sibling(arrays, name):
    n = len(arrays)

    def body(*refs):
        src, dst = refs[:n], refs[n:2 * n]
        send_sems, recv_sems = refs[2 * n:]
        x, y, c = _position()
        copies = [pltpu.make_async_remote_copy(
            src_ref=src[i], dst_ref=dst[i], send_sem=send_sems.at[i], recv_sem=recv_sems.at[i],
            device_id=(x, y, 1 - c), device_id_type=MESH) for i in range(n)]
        for cp in copies:
            cp.start()
        for cp in copies:
            cp.wait()

    return pl.pallas_call(
        body, name=name, in_specs=[HBM_SPEC] * n, out_specs=[HBM_SPEC] * n,
        out_shape=[_sds(a.shape, a.dtype) for a in arrays],
        scratch_shapes=[pltpu.SemaphoreType.DMA((n,)), pltpu.SemaphoreType.DMA((n,))],
    )(*arrays)


SEM_SPEC = pl.BlockSpec(memory_space=pltpu.SEMAPHORE)
ANY_SPEC = pl.BlockSpec(memory_space=pl.ANY)
IN_FLIGHT = pltpu.SideEffectType.DATAFLOW_SIDE_EFFECTING


def _peer_sems(all_devices):
    return pltpu.SemaphoreType.DMA(((N_DEV if all_devices else N_CHIPS) - 1,))


def _own_slot(all_devices, x, y, c):
    return 4 * x + 2 * y + c if all_devices else 2 * x + y


def _peers(all_devices, x, y, c):
    if not all_devices:
        return [((px, py, c), 2 * px + py) for px, py in _other_chips(x, y)]
    out = []
    for fx in range(2):
        for fy in range(2):
            for fc in range(2):
                if fx or fy or fc:
                    px, py, pc = (1 - x if fx else x), (1 - y if fy else y), (1 - c if fc else c)
                    out.append(((px, py, pc), 4 * px + 2 * py + pc))
    return out


def _in_hbm(a):
    return pltpu.with_memory_space_constraint(a, pltpu.HBM)


def _place_slot(a, slot, n_slots, name):
    rows, cols = a.shape
    tile = _elementwise_tile(rows, cols)

    def body(slot_ref, a_ref, o_ref):
        o_ref[...] = a_ref[...]

    grid_spec = pltpu.PrefetchScalarGridSpec(
        num_scalar_prefetch=1, grid=(rows // tile,),
        in_specs=[pl.BlockSpec((tile, cols), lambda i, slot_ref: (i, 0))],
        out_specs=pl.BlockSpec((None, tile, cols), lambda i, slot_ref: (slot_ref[0], i, 0)))
    return pl.pallas_call(
        body, name=name, grid_spec=grid_spec, out_shape=_sds((n_slots, rows, cols), a.dtype),
        compiler_params=_params("parallel"),
    )(slot, a)


def _place_shard(w, layer, chip, name):
    _, rows, cols = w.shape
    tile = _elementwise_tile(rows, cols)

    def body(chip_ref, w_ref, o_ref):
        o_ref[...] = w_ref[...].astype(BF16)

    grid_spec = pltpu.PrefetchScalarGridSpec(
        num_scalar_prefetch=1, grid=(rows // tile,),
        in_specs=[pl.BlockSpec((None, tile, cols), lambda i, chip_ref: (layer, i, 0))],
        out_specs=pl.BlockSpec((None, tile, cols), lambda i, chip_ref: (chip_ref[0], i, 0)))
    return pl.pallas_call(
        body, name=name, grid_spec=grid_spec, out_shape=_sds((N_CHIPS, rows, cols), BF16),
        compiler_params=_params("parallel"),
    )(chip, w)


def _rows_of(ref, slot, core, halves):
    if not halves:
        return ref.at[slot]
    half = ref.shape[1] // 2
    return ref.at[slot, pl.ds(pl.multiple_of(core * half, half), half)]


def _gather_start(bufs, after, name, all_devices=False, halves=False):
    n = len(bufs)

    def body(*refs):
        buf = refs[:n]
        send, recv = refs[n + 1:2 * n + 1], refs[2 * n + 1:3 * n + 1]
        token = refs[4 * n + 1]
        x, y, c = _position()
        mine = _own_slot(all_devices, x, y, c)
        for i in range(n):
            for k, (peer, _) in enumerate(_peers(all_devices, x, y, c)):
                rows = _rows_of(buf[i], mine, c, halves)
                pltpu.make_async_remote_copy(
                    src_ref=rows, dst_ref=rows, send_sem=send[i].at[k], recv_sem=recv[i].at[k],
                    device_id=peer, device_id_type=MESH).start()
        token[...] = jnp.zeros_like(token)

    sems = _peer_sems(all_devices)
    out = pl.pallas_call(
        body, name=name, in_specs=[HBM_SPEC] * n + [ANY_SPEC],
        out_specs=[SEM_SPEC] * (2 * n) + [HBM_SPEC] * n + [pl.BlockSpec(memory_space=pltpu.VMEM)],
        out_shape=[sems] * (2 * n) + [pltpu.HBM(b.shape, b.dtype) for b in bufs] + [_sds((SUBLANES, LANES), F32)],
        input_output_aliases={i: 2 * n + i for i in range(n)},
        compiler_params=pltpu.CompilerParams(has_side_effects=IN_FLIGHT),
    )(*[_in_hbm(b) for b in bufs], after)
    return [(out[i], out[n + i], out[2 * n + i]) for i in range(n)], out[3 * n]


def _gather_relay(flights, after, name):
    n = len(flights)

    def body(*refs):
        buf, send, recv = refs[:n], refs[n:2 * n], refs[2 * n:3 * n]
        send2, recv2 = refs[3 * n + 1:4 * n + 1], refs[4 * n + 1:5 * n + 1]
        token = refs[6 * n + 1]
        x, y, c = _position()
        mine = 2 * x + y
        for i in range(n):
            for k, (peer, slot) in enumerate(_peers(False, x, y, c)):
                arrived = _rows_of(buf[i], slot, c, True)
                cp = pltpu.make_async_remote_copy(
                    src_ref=_rows_of(buf[i], mine, c, True), dst_ref=arrived, send_sem=send[i].at[k], recv_sem=recv[i].at[k],
                    device_id=peer, device_id_type=MESH)
                cp.wait_send()
                cp.wait_recv()
                pltpu.make_async_remote_copy(
                    src_ref=arrived, dst_ref=arrived, send_sem=send2[i].at[k], recv_sem=recv2[i].at[k],
                    device_id=(x, y, 1 - c), device_id_type=MESH).start()
        token[...] = jnp.zeros_like(token)

    bufs = [f[2] for f in flights]
    sems = _peer_sems(False)
    out = pl.pallas_call(
        body, name=name, in_specs=[HBM_SPEC] * n + [SEM_SPEC] * (2 * n) + [ANY_SPEC],
        out_specs=[SEM_SPEC] * (2 * n) + [HBM_SPEC] * n + [pl.BlockSpec(memory_space=pltpu.VMEM)],
        out_shape=[sems] * (2 * n) + [pltpu.HBM(b.shape, b.dtype) for b in bufs] + [_sds((SUBLANES, LANES), F32)],
        input_output_aliases={i: 2 * n + i for i in range(n)},
        compiler_params=pltpu.CompilerParams(has_side_effects=IN_FLIGHT),
    )(*bufs, *[f[0] for f in flights], *[f[1] for f in flights], after)
    return [(out[i], out[n + i], out[2 * n + i]) for i in range(n)], out[3 * n]


def _gather_wait(send, recv, buf, after, name, all_devices=False, relayed=False):
    def body(buf_ref, send_ref, recv_ref, after_ref, out_ref):
        x, y, c = _position()
        mine = _own_slot(all_devices, x, y, c)
        for k, (peer, slot) in enumerate(_peers(all_devices, x, y, c)):
            if relayed:
                cp = pltpu.make_async_remote_copy(
                    src_ref=_rows_of(buf_ref, slot, c, True), dst_ref=_rows_of(buf_ref, slot, 1 - c, True),
                    send_sem=send_ref.at[k], recv_sem=recv_ref.at[k], device_id=(x, y, 1 - c), device_id_type=MESH)
            else:
                cp = pltpu.make_async_remote_copy(
                    src_ref=buf_ref.at[mine], dst_ref=buf_ref.at[slot], send_sem=send_ref.at[k], recv_sem=recv_ref.at[k],
                    device_id=peer, device_id_type=MESH)
            cp.wait_send()
            cp.wait_recv()

    return pl.pallas_call(
        body, name=name, in_specs=[HBM_SPEC, SEM_SPEC, SEM_SPEC, ANY_SPEC], out_specs=HBM_SPEC,
        out_shape=pltpu.HBM(buf.shape, buf.dtype), input_output_aliases={0: 0},
        compiler_params=pltpu.CompilerParams(has_side_effects=IN_FLIGHT),
    )(buf, send, recv, after)


def _scatter_start(pieces, name):
    n = len(pieces)

    def body(*refs):
        src = refs[:n]
        send, recv = refs[n:2 * n], refs[2 * n:3 * n]
        land = refs[4 * n:5 * n]
        token = refs[5 * n]
        x, y, c = _position()
        mine = 2 * x + y
        for i in range(n):
            for k, (px, py) in enumerate(_other_chips(x, y)):
                pltpu.make_async_remote_copy(
                    src_ref=src[i].at[2 * px + py], dst_ref=land[i].at[mine], send_sem=send[i].at[k], recv_sem=recv[i].at[k],
                    device_id=(px, py, c), device_id_type=MESH).start()
        token[...] = jnp.zeros_like(token)

    hbm = [pltpu.HBM(p.shape, p.dtype) for p in pieces]
    out = pl.pallas_call(
        body, name=name, in_specs=[HBM_SPEC] * n,
        out_specs=[SEM_SPEC] * (2 * n) + [HBM_SPEC] * (2 * n) + [pl.BlockSpec(memory_space=pltpu.VMEM)],
        out_shape=[_peer_sems(False)] * (2 * n) + hbm + hbm + [_sds((SUBLANES, LANES), F32)],
        input_output_aliases={i: 2 * n + i for i in range(n)},
        compiler_params=pltpu.CompilerParams(has_side_effects=IN_FLIGHT),
    )(*[_in_hbm(p) for p in pieces])
    return [(out[i], out[n + i], out[2 * n + i], out[3 * n + i]) for i in range(n)], out[4 * n]


def _scatter_wait(send, recv, pieces, land, after, name):
    def body(src_ref, land_ref, send_ref, recv_ref, after_ref, src_out, land_out):
        x, y, c = _position()
        for k, (px, py) in enumerate(_other_chips(x, y)):
            cp = pltpu.make_async_remote_copy(
                src_ref=src_ref.at[2 * px + py], dst_ref=land_ref.at[2 * px + py], send_sem=send_ref.at[k],
                recv_sem=recv_ref.at[k], device_id=(px, py, c), device_id_type=MESH)
            cp.wait_send()
            cp.wait_recv()

    return pl.pallas_call(
        body, name=name, in_specs=[HBM_SPEC, HBM_SPEC, SEM_SPEC, SEM_SPEC, ANY_SPEC], out_specs=[HBM_SPEC, HBM_SPEC],
        out_shape=[pltpu.HBM(pieces.shape, pieces.dtype), pltpu.HBM(land.shape, land.dtype)],
        input_output_aliases={0: 0, 1: 1},
        compiler_params=pltpu.CompilerParams(has_side_effects=IN_FLIGHT),
    )(pieces, land, send, recv, after)


def _rope_tables(batch, rows_per_example):
    inv_freq = ROPE_THETA ** (-jnp.arange(0, 64, 2, dtype=F32) / 64)
    ang = jnp.arange(rows_per_example, dtype=F32)[:, None] * inv_freq[None, :]
    cos, sin = jnp.cos(ang), jnp.sin(ang)
    cos128 = jnp.concatenate([cos, cos, cos, cos], axis=1)
    sin128 = jnp.concatenate([-sin, sin, -sin, sin], axis=1)
    return jnp.tile(cos128, (batch, 1)), jnp.tile(sin128, (batch, 1))


def _block_diagonal(w):
    eye = jnp.eye(LRU_BLOCKS, dtype=w.dtype)
    return (w[:, :, None, :] * eye[:, None, :, None]).reshape(LRU_W, LRU_W)


def _diagonal_blocks(dense):
    d4 = dense.reshape(LRU_BLOCKS, LRU_BLOCK, LRU_BLOCKS, LRU_BLOCK)
    return jnp.stack([d4[n, :, n, :] for n in range(LRU_BLOCKS)])


def _row(v):
    return v.reshape(1, -1)


def _local_step(x, loss_target, meta_tokens, small, depth, big_weight, on_layer_grads):
    batch, seq, _ = x.shape
    n_real = N_META + seq
    n_blocks = -(-n_real // TIME_BLOCK)
    per_example = n_blocks * TIME_BLOCK
    pad = per_example - n_real
    meta = jnp.broadcast_to(meta_tokens[None], (batch, N_META, D_MODEL))
    h = jnp.concatenate([meta, x, jnp.zeros((batch, pad, D_MODEL), F32)], axis=1).reshape(batch * per_example, D_MODEL)
    rope_cos, rope_sin = _rope_tables(batch, per_example)

    saved = []
    for l in range(depth):
        wa = _block_diagonal(small['gate_a_w'][l]).astype(BF16)
        wx = _block_diagonal(small['gate_x_w'][l]).astype(BF16)
        lru_small = (small['conv_w'][l], _row(small['conv_b'][l]), wa, _row(small['gate_a_b'][l]), wx,
                     _row(small['gate_x_b'][l]), _row(small['lru_lambda'][l]))
        w_in = big_weight('w_in', l, h)
        z1, q, kv, xg = _in_proj_fwd(h, _row(small['pre_mix_norm'][l]), w_in, rope_cos, rope_sin)
        attn, lse = _attn_fwd(q, kv, small['attn_sinks'][l], batch, n_blocks)
        hs, lru = _lru_fwd(xg, *lru_small, batch, n_blocks)
        w_out = big_weight('w_out', l, xg)
        w_up4 = big_weight('w_up', l, xg)
        w_down = big_weight('w_down', l, xg)
        gains = [_row(small[n][l]) for n in ('attn_group_norm', 'lru_group_norm', 'post_mix_norm', 'pre_mlp_norm',
                                             'post_mlp_norm')]
        groups, o, h_mid, z2, slope, act, y, h_out = _out_mlp_fwd(attn, lru, h, w_out, w_up4, w_down, *gains)
        saved.append(dict(h=h, z1=z1, q=q, kv=kv, xg=xg, attn=attn, lse=lse, hs=hs, lru=lru, groups=groups, o=o,
                          h_mid=h_mid, z2=z2, slope=slope, act=act, y=y, lru_small=lru_small, w_in=w_in, w_out=w_out,
                          w_up4=w_up4, w_down=w_down, gains=gains))
        h = h_out

    dh, sq_err = _loss_head(h.reshape(batch, per_example, D_MODEL), loss_target)
    dh = dh.reshape(batch * per_example, D_MODEL)

    gs = {n: [None] * depth for n in REPLICATED_NAMES + ('conv_w',)}
    handed_over = None
    for l in reversed(range(depth)):
        s = saved[l]

        def ordered_after(gain):
            return gain if handed_over is None else gain + handed_over[0, 0]

        gains = list(s['gains'])
        gains[4] = ordered_after(gains[4])
        dh_mid, dy, dup, do, d_attn, d_lru, g_rows = _mlp_out_bwd(dh, s['h_mid'], s['y'], s['slope'], s['o'], s['attn'],
                                                                s['lru'], s['w_out'], s['w_up4'], s['w_down'], *gains)
        handed_over = on_layer_grads(l, {
            'w_down': _matmul_tn(s['act'], dy, 512, D_MODEL, "grad_w_down", BF16),
            'w_up': _matmul_tn(s['z2'], dup, 512, D_MODEL, "grad_w_up", BF16, column_blocks=True)})
        lru_small = list(s['lru_small'])
        lru_small[1] = ordered_after(lru_small[1])
        dxg, dwa, dwx, g_lru = _lru_bwd(s['xg'], s['hs'], d_lru, *lru_small, batch, n_blocks)
        dq, dkv, dsink = _attn_bwd(s['q'], s['kv'], small['attn_sinks'][l], s['attn'], s['lse'], d_attn, batch, n_blocks)
        dh, dproj, g_in = _in_proj_bwd(dh_mid, s['h'], dq, dkv, dxg, rope_cos, rope_sin, _row(small['pre_mix_norm'][l]),
                                       s['w_in'])
        handed_over = on_layer_grads(l, {
            'w_out': _matmul_tn(s['groups'], do, 512, D_MODEL, "grad_w_out", BF16),
            'w_in': _matmul_tn(dproj, s['z1'], IN_COLS // 2, D_MODEL, "grad_w_in", BF16)})
        gs['pre_mlp_norm'][l], gs['post_mlp_norm'][l] = g_rows[0], g_rows[1]
        gs['post_mix_norm'][l] = g_rows[2]
        gs['attn_group_norm'][l], gs['lru_group_norm'][l] = g_rows[3, :ATTN_W], g_rows[3, ATTN_W:]
        gs['pre_mix_norm'][l] = g_in[0]
        gs['attn_sinks'][l] = dsink[:, 0]
        gs['conv_w'][l] = g_lru[:CONV_TAPS]
        gs['conv_b'][l], gs['gate_a_b'][l], gs['gate_x_b'][l], gs['lru_lambda'][l] = g_lru[4], g_lru[5], g_lru[6], g_lru[7]
        gs['gate_a_w'][l] = _diagonal_blocks(dwa)
        gs['gate_x_w'][l] = _diagonal_blocks(dwx)

    grad_x = dh.reshape(batch, per_example, D_MODEL)[:, N_META:n_real]
    grad_meta = _meta_grad(dh, batch, per_example)
    small_grads = {n: jnp.stack(v) for n, v in gs.items()}
    return sq_err, grad_x, grad_meta, small_grads, handed_over


PACK_UNIT = SUBLANES * LANES


def _size(shape):
    size = 1
    for d in shape:
        size *= d
    return size


def _pack(arrays):
    parts = []
    for a in arrays:
        flat = a.reshape(-1)
        padded = -(-flat.shape[0] // PACK_UNIT) * PACK_UNIT
        if padded != flat.shape[0]:
            flat = jnp.pad(flat, (0, padded - flat.shape[0]))
        parts.append(flat.reshape(-1, LANES))
    return jnp.concatenate(parts, axis=0)


def _unpack(buf, shapes):
    out, at = [], 0
    for shp in shapes:
        size = _size(shp)
        rows = -(-size // PACK_UNIT) * SUBLANES
        part = buf[at:at + rows]
        if rows * LANES != size:
            part = part.reshape(-1)[:size]
        out.append(part.reshape(shp))
        at += rows
    return out


def kernel(x, meta_tokens, pre_mix_norm, w_in, attn_sinks, conv_w, conv_b, gate_a_w, gate_a_b, gate_x_w, gate_x_b, lru_lambda, attn_group_norm, lru_group_norm, w_out, post_mix_norm, pre_mlp_norm, w_up, w_down, post_mlp_norm, loss_target, m_meta_tokens, m_pre_mix_norm, m_w_in, m_attn_sinks, m_conv_w, m_conv_b, m_gate_a_w, m_gate_a_b, m_gate_x_w, m_gate_x_b, m_lru_lambda, m_attn_group_norm, m_lru_group_norm, m_w_out, m_post_mix_norm, m_pre_mlp_norm, m_w_up, m_w_down, m_post_mlp_norm, v_meta_tokens, v_pre_mix_norm, v_w_in, v_attn_sinks, v_conv_w, v_conv_b, v_gate_a_w, v_gate_a_b, v_gate_x_w, v_gate_x_b, v_lru_lambda, v_attn_group_norm, v_lru_group_norm, v_w_out, v_post_mix_norm, v_pre_mlp_norm, v_w_up, v_w_down, v_post_mlp_norm):
    given = dict(locals())
    w = {n: given[n] for n in WEIGHT_NAMES}
    m = {n: given['m_' + n] for n in WEIGHT_NAMES}
    v = {n: given['v_' + n] for n in WEIGHT_NAMES}
    depth = w_in.shape[0]
    for d in (w, m, v):
        d['w_in'] = jnp.swapaxes(d['w_in'], 1, 2)
    chip = 2 * lax.axis_index("x") + lax.axis_index("y")
    chip1 = chip.reshape(1).astype(jnp.int32)

    in_flight = []
    all_started = chip1
    for l in range(depth):
        bufs = [_place_shard(w[n], l, chip1, "place_" + n) for n in BIG_NAMES]
        handles, all_started = _gather_start(bufs, all_started, "gather_start_%d" % l, halves=True)
        in_flight.append(dict(zip(BIG_NAMES, handles)))
        if l == 0:
            first_started = all_started
    first_use = {'w_in': ('w_in',), 'w_out': ('w_out', 'w_up', 'w_down')}

    def big_weight(name, l, after):
        if l == 0 and name == 'w_in':
            after = first_started
        elif l == 0 and name == 'w_out':
            after = after[:SUBLANES, :LANES] + all_started
        names = first_use.get(name, ()) if l == 0 else (BIG_NAMES if name == 'w_in' else ())
        if names:
            relayed, _ = _gather_relay([in_flight[l][n] for n in names], after, "gather_relay_%d_%s" % (l, name))
            in_flight[l].update(zip(names, relayed))
        send, recv, buf = in_flight[l][name]
        full = _gather_wait(send, recv, buf, after, "gather_wait_%d_%s" % (l, name), relayed=True)
        if name == 'w_in':
            return full.reshape(IN_COLS, D_MODEL)
        if name == 'w_out':
            return full.reshape(D_MODEL, D_MODEL)
        if name == 'w_down':
            return full.reshape(D_FF, D_MODEL)
        return full

    gathered = _exchange_chips([meta_tokens, conv_w], "gather_small_weights", scatter=False)
    full_meta = jnp.concatenate([gathered[0][s] for s in range(N_CHIPS)], axis=1)
    small = {n: w[n] for n in REPLICATED_NAMES}
    small['conv_w'] = jnp.concatenate([gathered[1][s] for s in range(N_CHIPS)], axis=2)

    scattering = [{} for _ in range(depth)]

    def on_layer_grads(l, big):
        names = list(big)
        pieces = [big[n].reshape(N_CHIPS, -1, D_MODEL) for n in names]
        handles, started = _scatter_start(pieces, "scatter_start_%d_%s" % (l, names[0]))
        scattering[l].update(zip(names, handles))
        return started

    sq_err, grad_x, grad_meta, small_grads, all_handed_over = _local_step(x, loss_target, full_meta, small, depth, big_weight,
                                                                           on_layer_grads)
    loss = lax.psum(sq_err[0, 0] * (0.5 / D_MODEL), ("x", "y", "c"))

    small_names = list(REPLICATED_NAMES) + list(COLUMN_SHARDED_SMALL)
    small_full = dict(small_grads)
    small_full['meta_tokens'] = grad_meta
    device1 = (2 * chip + lax.axis_index("c")).reshape(1).astype(jnp.int32)
    packed = _place_slot(_pack([small_full[n] for n in small_names]), device1, N_DEV, "place_small_grads")
    [small_flight], done = _gather_start([packed], all_handed_over, "gather_small_grads_start", all_devices=True)

    partial = {n: None for n in BIG_NAMES}
    for l in reversed(range(depth)):
        for n in BIG_NAMES:
            send, recv, pieces, land = scattering[l][n]
            pieces, land = _scatter_wait(send, recv, pieces, land, done, "scatter_wait_%d_%s" % (l, n))
            partial[n], done = _sum_pieces(pieces, land, chip1, l, depth, partial[n])
    sibling = _swap_sibling([partial[n] for n in BIG_NAMES], "swap_partial_grads")
    results = {n: _adamw(w[n], m[n], v[n], [partial[n], other], "adamw_" + n) for n, other in zip(BIG_NAMES, sibling)}

    slots = _gather_wait(*small_flight, results['w_down'][0], "gather_small_grads_wait", all_devices=True)
    summed = _unpack(_sum_slots(slots, "sum_small_grads"), [small_full[n].shape for n in small_names])
    grads = dict(zip(small_names, summed))
    grads['meta_tokens'] = lax.dynamic_slice_in_dim(grads['meta_tokens'], chip * meta_tokens.shape[1], meta_tokens.shape[1], 1)
    grads['conv_w'] = lax.dynamic_slice_in_dim(grads['conv_w'], chip * conv_w.shape[2], conv_w.shape[2], 2)

    out_g, out_d, out_m, out_v = {}, {}, {}, {}
    for n in BIG_NAMES:
        out_g[n], out_d[n], out_m[n], out_v[n] = [jnp.swapaxes(r, 1, 2) if n == 'w_in' else r for r in results[n][:4]]
    shapes = [w[n].shape for n in small_names]
    res = _adamw(_pack([w[n] for n in small_names]), _pack([m[n] for n in small_names]), _pack([v[n] for n in small_names]),
                 [_pack([grads[n] for n in small_names])], "adamw_small")
    for k, store in enumerate((out_g, out_d, out_m, out_v)):
        for n, a in zip(small_names, _unpack(res[k], shapes)):
            store[n] = a

    return (loss, grad_x, *[out_g[n] for n in WEIGHT_NAMES], *[out_d[n] for n in WEIGHT_NAMES],
            *[out_m[n] for n in WEIGHT_NAMES], *[out_v[n] for n in WEIGHT_NAMES])
```

```python
import functools

import jax
import jax.numpy as jnp
from jax import lax
from jax.experimental import pallas as pl
from jax.experimental.pallas import tpu as pltpu

F32 = jnp.float32
BF16 = jnp.bfloat16

D_MODEL = 1024
N_HEADS = 8
ATTN_W = 512
KV_W = 128
LRU_W = 512
LRU_BLOCKS = 8
LRU_BLOCK = 64
IN_COLS = 1792
D_FF = 4096
N_META = 16
CONV_TAPS = 4
LRU_C = 8.0
ROPE_THETA = 10000.0
EPS = 1e-6
ATTN_SCALE = 0.125

ADAM_LR = 0.001
ADAM_B1 = 0.9
ADAM_B2 = 0.999
ADAM_EPS = 1e-08
ADAM_WD = 0.01
ADAM_STEP = 10

N_CHIPS = 4
N_DEV = 8
TIME_BLOCK = 128
ROW_TILE = 256
PROJ_ROW_TILE = 544
LANES = 128
SUBLANES = 8
MASKED = -1e30
VMEM_LIMIT = 56 * 1024 * 1024

MESH = pl.DeviceIdType.MESH
HBM_SPEC = pl.BlockSpec(memory_space=pltpu.HBM)

WEIGHT_NAMES = ['meta_tokens', 'pre_mix_norm', 'w_in', 'attn_sinks', 'conv_w', 'conv_b', 'gate_a_w', 'gate_a_b',
                'gate_x_w', 'gate_x_b', 'lru_lambda', 'attn_group_norm', 'lru_group_norm', 'w_out', 'post_mix_norm',
                'pre_mlp_norm', 'w_up', 'w_down', 'post_mlp_norm']
BIG_NAMES = ('w_in', 'w_out', 'w_up', 'w_down')
COLUMN_SHARDED_SMALL = ('meta_tokens', 'conv_w')
REPLICATED_NAMES = tuple(n for n in WEIGHT_NAMES if n not in BIG_NAMES and n not in COLUMN_SHARDED_SMALL)


def _sds(shape, dtype):
    return jax.ShapeDtypeStruct(tuple(shape), dtype)


def _params(*sem):
    return pltpu.CompilerParams(dimension_semantics=sem, vmem_limit_bytes=VMEM_LIMIT)


def _row_spec(width, tile=ROW_TILE):
    return pl.BlockSpec((tile, width), lambda i: (i, 0))


def _proj_tile(rows):
    tile = PROJ_ROW_TILE
    while rows % tile:
        tile -= 16
    return tile


def _whole_spec(a):
    nd = a.ndim
    return pl.BlockSpec(a.shape, lambda *_: (0,) * nd)


def _resident_spec(a):
    nd = a.ndim
    return pl.BlockSpec(a.shape, lambda *_: (0,) * nd, pipeline_mode=pl.Buffered(1))


def _rms(x, g):
    r = lax.rsqrt(jnp.mean(x * x, axis=-1, keepdims=True) + EPS)
    return x * r * g


def _rms_bwd(dy, x, g):
    r = lax.rsqrt(jnp.mean(x * x, axis=-1, keepdims=True) + EPS)
    xh = x * r
    dg = jnp.sum(dy * xh, axis=0, keepdims=True)
    dxh = dy * g
    dx = r * (dxh - xh * jnp.mean(dxh * xh, axis=-1, keepdims=True))
    return dx, dg


def _rope(x, cos, sin_signed):
    width = x.shape[1]
    reps = width // LANES
    if reps > 1:
        cos = jnp.tile(cos, (1, reps))
        sin_signed = jnp.tile(sin_signed, (1, reps))
    lane = lax.broadcasted_iota(jnp.int32, x.shape, 1)
    first_half = jnp.bitwise_and(lane, 32) == 0
    other = jnp.where(first_half, pltpu.roll(x, width - 32, 1), pltpu.roll(x, 32, 1))
    return x * cos + other * sin_signed


def _sigmoid(x):
    return 1.0 / (1.0 + jnp.exp(-x))


def _log1p(e):
    return jnp.where(e < 1e-3, e * (1.0 - e * (0.5 - e * (1.0 / 3.0))), jnp.log(1.0 + e))


def _one_minus_square(a, log_a):
    x = 2.0 * log_a
    return jnp.where(x > -0.002, x * (-1.0 - 0.5 * x), 1.0 - a * a)


GELU_K = 0.7978845608028654
GELU_C = 0.044715


def _gelu(x):
    t = jnp.tanh(GELU_K * (x + GELU_C * x * x * x))
    return 0.5 * x * (1.0 + t)


def _gelu_and_grad(x):
    x2 = x * x
    t = jnp.tanh(GELU_K * (x + GELU_C * x * x2))
    val = 0.5 * x * (1.0 + t)
    grad = 0.5 * (1.0 + t) + 0.5 * x * (1.0 - t * t) * GELU_K * (1.0 + 3.0 * GELU_C * x2)
    return val, grad


def _dot(a, b):
    return jnp.dot(a, b, preferred_element_type=F32)


def _dot_nt(a, b):
    return lax.dot_general(a, b, (((1,), (1,)), ((), ())), preferred_element_type=F32)


def _dot_tn(a, b):
    return lax.dot_general(a, b, (((0,), (0,)), ((), ())), preferred_element_type=F32)


def _put_rows(rows_8, values):
    d = values[0].shape[1]
    rowid = lax.broadcasted_iota(jnp.int32, (rows_8, d), 0)
    out = jnp.zeros((rows_8, d), F32)
    for k, v in enumerate(values):
        out = out + jnp.where(rowid == k, v, 0.0)
    return out


def _in_proj_fwd(h, gain, w_in, rope_cos, rope_sin):
    rows = h.shape[0]

    def body(h_ref, g_ref, w_ref, c_ref, s_ref, z_ref, q_ref, kv_ref, xg_ref):
        z = _rms(h_ref[...], g_ref[...]).astype(BF16)
        z_ref[...] = z
        proj = _dot_nt(z, w_ref[...])
        cos = c_ref[...]
        sin = s_ref[...]
        q_ref[...] = (_rope(proj[:, :ATTN_W], cos, sin) * ATTN_SCALE).astype(BF16)
        kv_ref[:, :KV_W] = _rope(proj[:, ATTN_W:ATTN_W + KV_W], cos, sin).astype(BF16)
        kv_ref[:, KV_W:] = proj[:, ATTN_W + KV_W:ATTN_W + 2 * KV_W].astype(BF16)
        xg_ref[...] = proj[:, ATTN_W + 2 * KV_W:]

    tile = _proj_tile(rows)
    rs = functools.partial(_row_spec, tile=tile)
    return pl.pallas_call(
        body, name="in_proj_fwd", grid=(rows // tile,),
        in_specs=[rs(D_MODEL), _whole_spec(gain), _whole_spec(w_in), rs(LANES), rs(LANES)],
        out_specs=[rs(D_MODEL), rs(ATTN_W), rs(2 * KV_W), rs(2 * LRU_W)],
        out_shape=[_sds((rows, D_MODEL), BF16), _sds((rows, ATTN_W), BF16), _sds((rows, 2 * KV_W), BF16),
                   _sds((rows, 2 * LRU_W), F32)],
        compiler_params=_params("parallel"),
    )(h, gain, w_in, rope_cos, rope_sin)


def _in_proj_bwd(dh_mid, h, dq, dkv, dxg, rope_cos, rope_sin, gain, w_in):
    rows = h.shape[0]

    def body(dhm_ref, h_ref, dq_ref, dkv_ref, dxg_ref, c_ref, s_ref, g_ref, w_ref, dh_ref, dp_ref, gacc_ref):
        @pl.when(pl.program_id(0) == 0)
        def _():
            gacc_ref[...] = jnp.zeros_like(gacc_ref)

        cos = c_ref[...]
        sin = -s_ref[...]
        dp_ref[:, :ATTN_W] = (_rope(dq_ref[...], cos, sin) * ATTN_SCALE).astype(BF16)
        dp_ref[:, ATTN_W:ATTN_W + KV_W] = _rope(dkv_ref[:, :KV_W], cos, sin).astype(BF16)
        dp_ref[:, ATTN_W + KV_W:ATTN_W + 2 * KV_W] = dkv_ref[:, KV_W:].astype(BF16)
        dp_ref[:, ATTN_W + 2 * KV_W:] = dxg_ref[...]
        dz = _dot(dp_ref[...], w_ref[...])
        dx, dg = _rms_bwd(dz, h_ref[...], g_ref[...])
        dh_ref[...] = dhm_ref[...] + dx
        gacc_ref[...] += _put_rows(SUBLANES, [dg])

    tile = _proj_tile(rows)
    rs = functools.partial(_row_spec, tile=tile)
    return pl.pallas_call(
        body, name="in_proj_bwd", grid=(rows // tile,),
        in_specs=[rs(D_MODEL), rs(D_MODEL), rs(ATTN_W), rs(2 * KV_W), rs(2 * LRU_W),
                  rs(LANES), rs(LANES), _whole_spec(gain), _whole_spec(w_in)],
        out_specs=[rs(D_MODEL), rs(IN_COLS), pl.BlockSpec((SUBLANES, D_MODEL), lambda i: (0, 0))],
        out_shape=[_sds((rows, D_MODEL), F32), _sds((rows, IN_COLS), BF16), _sds((SUBLANES, D_MODEL), F32)],
        compiler_params=_params("arbitrary"),
    )(dh_mid, h, dq, dkv, dxg, rope_cos, rope_sin, gain, w_in)


def _kv_lane_variants(t, group):
    lane = lax.broadcasted_iota(jnp.int32, t.shape, 1)
    low = lane < 64
    swapped = pltpu.roll(t, 64, 1)
    if group == 0:
        lo, hi = jnp.where(low, t, 0.0), jnp.where(low, 0.0, swapped)
    else:
        lo, hi = jnp.where(low, swapped, 0.0), jnp.where(low, 0.0, t)
    return jnp.concatenate([lo, hi], axis=0).astype(BF16)


GROUP_ROWS = 2 * TIME_BLOCK
KEYS = 2 * TIME_BLOCK


def _window_mask(j):
    r = jnp.bitwise_and(lax.broadcasted_iota(jnp.int32, (GROUP_ROWS, KEYS), 0), TIME_BLOCK - 1)
    c = lax.broadcasted_iota(jnp.int32, (GROUP_ROWS, KEYS), 1)
    return (c > r) & (c <= r + TIME_BLOCK) & ((c >= TIME_BLOCK) | (j > 0))


def _group_rows(ref, ex, group):
    lo = 2 * group * LANES
    return jnp.concatenate([ref[ex, :, lo:lo + LANES], ref[ex, :, lo + LANES:lo + 2 * LANES]], axis=0)


def _per_head(sink_ref, group, half):
    upper = lax.broadcasted_iota(jnp.int32, (GROUP_ROWS, 1), 0) < TIME_BLOCK
    return jnp.where(upper, sink_ref[4 * group + half], sink_ref[4 * group + 2 + half])


def _by_example(a, batch):
    return a.reshape(batch, a.shape[0] // batch, a.shape[1])


def _attn_fwd(q, kv, sinks, batch, n_blocks):
    rows = q.shape[0]

    def body(sink_ref, q_ref, kvc_ref, kvp_ref, o_ref, lse_ref):
        j = pl.program_id(0)
        mask = _window_mask(j)
        chains = [(ex, group) for ex in range(batch) for group in range(2)]
        v_cat, s_all, probs = {}, {}, {ch: [] for ch in chains}
        for ex, group in chains:
            kv2 = jnp.concatenate([kvp_ref[ex], kvc_ref[ex]], axis=0).astype(F32)
            k_cat = _kv_lane_variants(kv2[:, :KV_W], group)
            v_cat[ex, group] = _kv_lane_variants(kv2[:, KV_W:], group)
            s_all[ex, group] = _dot_nt(_group_rows(q_ref, ex, group), k_cat)
        for half in range(2):
            for ex, group in chains:
                sink = _per_head(sink_ref, group, half)
                s = jnp.where(mask, s_all[ex, group][:, half * KEYS:(half + 1) * KEYS], MASKED)
                m = jnp.maximum(jnp.max(s, axis=1, keepdims=True), sink)
                e = jnp.exp(s - m)
                den = jnp.sum(e, axis=1, keepdims=True) + jnp.exp(sink - m)
                probs[ex, group].append((e / den).astype(BF16))
                lse = m + jnp.log(den)
                upper_head, lower_head = 4 * group + half, 4 * group + 2 + half
                lse_ref[ex, :, upper_head:upper_head + 1] = lse[:TIME_BLOCK]
                lse_ref[ex, :, lower_head:lower_head + 1] = lse[TIME_BLOCK:]
        for ex, group in chains:
            out = _dot(jnp.concatenate(probs[ex, group], axis=1), v_cat[ex, group])
            o_ref[ex, :, 2 * group * LANES:(2 * group + 1) * LANES] = out[:TIME_BLOCK]
            o_ref[ex, :, (2 * group + 1) * LANES:(2 * group + 2) * LANES] = out[TIME_BLOCK:]

    def blk(width):
        return pl.BlockSpec((batch, TIME_BLOCK, width), lambda j: (0, j, 0))

    prev = pl.BlockSpec((batch, TIME_BLOCK, 2 * KV_W), lambda j: (0, jnp.maximum(j - 1, 0), 0))
    kv3 = _by_example(kv, batch)
    out, lse = pl.pallas_call(
        body, name="attn_fwd", grid=(n_blocks,),
        in_specs=[pl.BlockSpec(memory_space=pltpu.SMEM), blk(ATTN_W), blk(2 * KV_W), prev],
        out_specs=[blk(ATTN_W), blk(N_HEADS)],
        out_shape=[_sds((batch, rows // batch, ATTN_W), F32), _sds((batch, rows // batch, N_HEADS), F32)],
        compiler_params=_params("parallel"),
    )(sinks, _by_example(q, batch), kv3, kv3)
    return out.reshape(rows, ATTN_W), lse.reshape(rows, N_HEADS)


def _attn_bwd(q, kv, sinks, out, lse, d_out, batch, n_blocks):
    rows = q.shape[0]

    def body(sink_ref, q_ref, kvc_ref, kvp_ref, o_ref, do_ref, lse_ref, dq_ref, dkv_ref, dsink_ref, carry):
        j = pl.program_id(0)

        @pl.when(j == 0)
        def _():
            dsink_ref[...] = jnp.zeros_like(dsink_ref)

        def all_chains(dsink_vals):
            mask = _window_mask(j)
            lane = lax.broadcasted_iota(jnp.int32, (GROUP_ROWS, LANES), 1)
            low = lax.broadcasted_iota(jnp.int32, (KEYS, LANES), 1) < 64
            upper = lax.broadcasted_iota(jnp.int32, (GROUP_ROWS, 1), 0) < TIME_BLOCK
            chains = [(ex, group) for ex in range(batch) for group in range(2)]
            k_cat, v_cat, q_rows, do_b, od, s_all, dp_all = {}, {}, {}, {}, {}, {}, {}
            for ch in chains:
                ex, group = ch
                kv2 = jnp.concatenate([kvp_ref[ex], kvc_ref[ex]], axis=0).astype(F32)
                k_cat[ch] = _kv_lane_variants(kv2[:, :KV_W], group)
                v_cat[ch] = _kv_lane_variants(kv2[:, KV_W:], group)
                q_rows[ch] = _group_rows(q_ref, ex, group)
                do_rows = _group_rows(do_ref, ex, group)
                do_b[ch] = do_rows.astype(BF16)
                od[ch] = do_rows * _group_rows(o_ref, ex, group)
                s_all[ch] = _dot_nt(q_rows[ch], k_cat[ch])
                dp_all[ch] = _dot_nt(do_b[ch], v_cat[ch])
            probs, dss = {ch: [] for ch in chains}, {ch: [] for ch in chains}
            for half in range(2):
                for ch in chains:
                    ex, group = ch
                    heads = (4 * group + half, 4 * group + 2 + half)
                    sink = _per_head(sink_ref, group, half)
                    lse_h = jnp.concatenate([lse_ref[ex, :, h:h + 1] for h in heads], axis=0)
                    in_half = (lane < 64) if half == 0 else (lane >= 64)
                    delta = jnp.sum(jnp.where(in_half, od[ch], 0.0), axis=1, keepdims=True)
                    cols = slice(half * KEYS, (half + 1) * KEYS)
                    prob = jnp.exp(jnp.where(mask, s_all[ch][:, cols], MASKED) - lse_h)
                    probs[ch].append(prob.astype(BF16))
                    dss[ch].append((prob * (dp_all[ch][:, cols] - delta)).astype(BF16))
                    dsink = -jnp.exp(sink - lse_h) * delta
                    dsink_vals[heads[0]] = dsink_vals[heads[0]] + jnp.sum(jnp.where(upper, dsink, 0.0), axis=0, keepdims=True)
                    dsink_vals[heads[1]] = dsink_vals[heads[1]] + jnp.sum(jnp.where(upper, 0.0, dsink), axis=0, keepdims=True)
            dk_tile = {ex: jnp.zeros((KEYS, KV_W), F32) for ex in range(batch)}
            dv_tile = {ex: jnp.zeros((KEYS, KV_W), F32) for ex in range(batch)}
            for ch in chains:
                ex, group = ch
                ds = jnp.concatenate(dss[ch], axis=1)
                dq_rows = _dot(ds, k_cat[ch])
                dq_ref[ex, :, 2 * group * LANES:(2 * group + 1) * LANES] = dq_rows[:TIME_BLOCK]
                dq_ref[ex, :, (2 * group + 1) * LANES:(2 * group + 2) * LANES] = dq_rows[TIME_BLOCK:]
                dk_cat = _dot_tn(ds, q_rows[ch])
                dv_cat = _dot_tn(jnp.concatenate(probs[ch], axis=1), do_b[ch])
                if group == 0:
                    dk_tile[ex] = dk_tile[ex] + jnp.where(low, dk_cat[:KEYS] + pltpu.roll(dk_cat[KEYS:], 64, 1), 0.0)
                    dv_tile[ex] = dv_tile[ex] + jnp.where(low, dv_cat[:KEYS] + pltpu.roll(dv_cat[KEYS:], 64, 1), 0.0)
                else:
                    dk_tile[ex] = dk_tile[ex] + jnp.where(low, 0.0, pltpu.roll(dk_cat[:KEYS], 64, 1) + dk_cat[KEYS:])
                    dv_tile[ex] = dv_tile[ex] + jnp.where(low, 0.0, pltpu.roll(dv_cat[:KEYS], 64, 1) + dv_cat[KEYS:])

            @pl.when(j > 0)
            def _():
                for ex in range(batch):
                    dkv_ref[ex, :, :KV_W] = carry[ex, :, :KV_W] + dk_tile[ex][:TIME_BLOCK]
                    dkv_ref[ex, :, KV_W:] = carry[ex, :, KV_W:] + dv_tile[ex][:TIME_BLOCK]

            for ex in range(batch):
                carry[ex, :, :KV_W] = dk_tile[ex][TIME_BLOCK:]
                carry[ex, :, KV_W:] = dv_tile[ex][TIME_BLOCK:]

        @pl.when(j < n_blocks)
        def _():
            dsink_vals = {head: jnp.zeros((1, 1), F32) for head in range(N_HEADS)}
            all_chains(dsink_vals)
            rowid = lax.broadcasted_iota(jnp.int32, (N_HEADS, LANES), 0)
            upd = jnp.zeros((N_HEADS, LANES), F32)
            for head, val in dsink_vals.items():
                upd = upd + jnp.where(rowid == head, val, 0.0)
            dsink_ref[...] += upd

        @pl.when(j == n_blocks)
        def _():
            dkv_ref[...] = carry[...]

    last = n_blocks - 1

    def blk(width):
        return pl.BlockSpec((batch, TIME_BLOCK, width), lambda j: (0, jnp.minimum(j, last), 0))

    prev = pl.BlockSpec((batch, TIME_BLOCK, 2 * KV_W), lambda j: (0, jnp.maximum(jnp.minimum(j, last) - 1, 0), 0))
    dkv_spec = pl.BlockSpec((batch, TIME_BLOCK, 2 * KV_W), lambda j: (0, jnp.maximum(j - 1, 0), 0))
    kv3 = _by_example(kv, batch)
    dq, dkv, dsink = pl.pallas_call(
        body, name="attn_bwd", grid=(n_blocks + 1,),
        in_specs=[pl.BlockSpec(memory_space=pltpu.SMEM), blk(ATTN_W), blk(2 * KV_W), prev, blk(ATTN_W), blk(ATTN_W),
                  blk(N_HEADS)],
        out_specs=[blk(ATTN_W), dkv_spec, pl.BlockSpec((N_HEADS, LANES), lambda j: (0, 0))],
        out_shape=[_sds((batch, rows // batch, ATTN_W), F32), _sds((batch, rows // batch, 2 * KV_W), F32),
                   _sds((N_HEADS, LANES), F32)],
        scratch_shapes=[pltpu.VMEM((batch, TIME_BLOCK, 2 * KV_W), F32)],
        compiler_params=_params("arbitrary"),
    )(sinks, _by_example(q, batch), kv3, kv3, _by_example(out, batch), _by_example(d_out, batch), _by_example(lse, batch))
    return dq.reshape(rows, ATTN_W), dkv.reshape(rows, 2 * KV_W), dsink


def _conv_taps(xb, prev8):
    ext = jnp.concatenate([prev8, xb], axis=0)
    n = ext.shape[0]
    return [xb] + [pltpu.roll(ext, k, 0)[SUBLANES:n] for k in range(1, CONV_TAPS)]


def _lru_gates(xc, wa, ba, wx, bx, lam):
    xcb = xc.astype(BF16)
    r = _sigmoid(_dot(xcb, wa) + ba)
    i = _sigmoid(_dot(xcb, wx) + bx)
    sp = jnp.maximum(-lam, 0.0) + _log1p(jnp.exp(-jnp.abs(lam)))
    log_a = -LRU_C * r * sp
    a = jnp.exp(log_a)
    mult = jnp.sqrt(_one_minus_square(a, log_a))
    return xcb, r, i, sp, a, mult


def _scan_fwd(a, u, h_before):
    n, d = a.shape
    groups = n // SUBLANES
    per_seq = groups // len(h_before)
    a = a.reshape(groups, SUBLANES, d)
    u = u.reshape(groups, SUBLANES, d)
    sub = lax.broadcasted_iota(jnp.int32, a.shape, 1)
    s = 1
    while s < SUBLANES:
        valid = sub >= s
        u = jnp.where(valid, u + a * pltpu.roll(u, s, 1), u)
        a = jnp.where(valid, a * pltpu.roll(a, s, 1), a)
        s *= 2
    out, prev = [None] * groups, list(h_before)
    for g in range(per_seq):
        for k in range(len(h_before)):
            at = k * per_seq + g
            out[at] = u[at] + a[at] * prev[k]
            prev[k] = out[at][SUBLANES - 1:SUBLANES, :]
    return jnp.concatenate(out, axis=0)


def _scan_rev(cf, g, d_after):
    n, d = g.shape
    groups = n // SUBLANES
    per_seq = groups // len(d_after)
    cf = cf.reshape(groups, SUBLANES, d)
    g = g.reshape(groups, SUBLANES, d)
    sub = lax.broadcasted_iota(jnp.int32, g.shape, 1)
    s = 1
    while s < SUBLANES:
        valid = sub + s < SUBLANES
        g = jnp.where(valid, g + cf * pltpu.roll(g, SUBLANES - s, 1), g)
        cf = jnp.where(valid, cf * pltpu.roll(cf, SUBLANES - s, 1), cf)
        s *= 2
    out, nxt = [None] * groups, list(d_after)
    for i in reversed(range(per_seq)):
        for k in range(len(d_after)):
            at = k * per_seq + i
            out[at] = g[at] + cf[at] * nxt[k]
            nxt[k] = out[at][0:1, :]
    return jnp.concatenate(out, axis=0)


def _lru_fwd(xg, conv_w, conv_b, wa, ba, wx, bx, lam, batch, n_blocks):
    rows = xg.shape[0]

    def body(xg_ref, cw_ref, cb_ref, wa_ref, ba_ref, wx_ref, bx_ref, lam_ref, hs_ref, lru_ref, x_prev, h_carry):
        @pl.when(pl.program_id(0) == 0)
        def _():
            x_prev[...] = jnp.zeros_like(x_prev)
            h_carry[...] = jnp.zeros_like(h_carry)

        xb = [xg_ref[ex, :, :LRU_W] for ex in range(batch)]
        taps = [_conv_taps(xb[ex], x_prev[ex]) for ex in range(batch)]
        taps = [jnp.concatenate([t[k] for t in taps], axis=0) for k in range(CONV_TAPS)]
        xc = cb_ref[...] + sum(cw_ref[CONV_TAPS - 1 - k:CONV_TAPS - k, :] * taps[k] for k in range(CONV_TAPS))
        _, _, i, _, a, mult = _lru_gates(xc, wa_ref[...], ba_ref[...], wx_ref[...], bx_ref[...], lam_ref[...])
        h = _scan_fwd(a, mult * i * xc, [h_carry[ex] for ex in range(batch)])
        gate = _gelu(jnp.concatenate([xg_ref[ex, :, LRU_W:] for ex in range(batch)], axis=0))
        lru = h * gate
        for ex in range(batch):
            rows_ex = slice(ex * TIME_BLOCK, (ex + 1) * TIME_BLOCK)
            hs_ref[ex] = h[rows_ex]
            lru_ref[ex] = lru[rows_ex]
            h_carry[ex] = h[(ex + 1) * TIME_BLOCK - 1:(ex + 1) * TIME_BLOCK, :]
            x_prev[ex] = xb[ex][TIME_BLOCK - SUBLANES:TIME_BLOCK, :]

    def blk(width):
        return pl.BlockSpec((batch, TIME_BLOCK, width), lambda j: (0, j, 0))

    small = [conv_w, conv_b, wa, ba, wx, bx, lam]
    hs, lru = pl.pallas_call(
        body, name="lru_fwd", grid=(n_blocks,),
        in_specs=[blk(2 * LRU_W)] + [_whole_spec(a) for a in small],
        out_specs=[blk(LRU_W), blk(LRU_W)],
        out_shape=[_sds((batch, rows // batch, LRU_W), F32)] * 2,
        scratch_shapes=[pltpu.VMEM((batch, SUBLANES, LRU_W), F32), pltpu.VMEM((batch, 1, LRU_W), F32)],
        compiler_params=_params("arbitrary"),
    )(_by_example(xg, batch), *small)
    return hs.reshape(rows, LRU_W), lru.reshape(rows, LRU_W)


def _lru_bwd(xg, hs, d_lru, conv_w, conv_b, wa, ba, wx, bx, lam, batch, n_blocks):
    rows = xg.shape[0]
    last_row = TIME_BLOCK - 1

    def body(xg_ref, xgh_ref, hs_ref, hsh_ref, dl_ref, cw_ref, cb_ref, wa_ref, ba_ref, wx_ref, bx_ref, lam_ref,
             dxg_ref, dwa_ref, dwx_ref, vec_ref, dh_carry, dxc_next):
        j = pl.program_id(0)

        @pl.when(j == 0)
        def _():
            dwa_ref[...] = jnp.zeros_like(dwa_ref)
            dwx_ref[...] = jnp.zeros_like(dwx_ref)
            vec_ref[...] = jnp.zeros_like(vec_ref)
            dh_carry[...] = jnp.zeros_like(dh_carry)
            dxc_next[...] = jnp.zeros_like(dxc_next)

        examples = range(batch)
        first = j == n_blocks - 1
        xb = [xg_ref[ex, :, :LRU_W] for ex in examples]
        taps = [_conv_taps(xb[ex], jnp.where(first, 0.0, xgh_ref[ex, :, :LRU_W])) for ex in examples]
        taps = [jnp.concatenate([t[k] for t in taps], axis=0) for k in range(CONV_TAPS)]
        cw = cw_ref[...]
        lam = lam_ref[...]
        wa = wa_ref[...]
        wx = wx_ref[...]
        xc = cb_ref[...] + sum(cw[CONV_TAPS - 1 - k:CONV_TAPS - k, :] * taps[k] for k in range(CONV_TAPS))
        xcb, r, i, sp, a, mult = _lru_gates(xc, wa, ba_ref[...], wx, bx_ref[...], lam)
        hs = jnp.concatenate([hs_ref[ex] for ex in examples], axis=0)
        row = jnp.bitwise_and(lax.broadcasted_iota(jnp.int32, hs.shape, 0), TIME_BLOCK - 1)
        h_before = jnp.concatenate([jnp.broadcast_to(jnp.where(first, 0.0, hsh_ref[ex, SUBLANES - 1:SUBLANES, :]),
                                                     (TIME_BLOCK, LRU_W)) for ex in examples], axis=0)
        h_prev = jnp.where(row == 0, h_before, pltpu.roll(hs, 1, 0))
        dl = jnp.concatenate([dl_ref[ex] for ex in examples], axis=0)
        gate, dgate = _gelu_and_grad(jnp.concatenate([xg_ref[ex, :, LRU_W:] for ex in examples], axis=0))
        dgb = (dl * hs * dgate).astype(BF16)
        cf = jnp.where(row == last_row, 1.0, pltpu.roll(a, batch * TIME_BLOCK - 1, 0))
        dh = _scan_rev(cf, dl * gate, [dh_carry[ex] for ex in examples])
        dmult = dh * i * xc
        di = dh * mult * xc
        dxc = dh * mult * i
        dlog_a = dh * h_prev * a - dmult * (a * a / mult)
        dr = dlog_a * (-LRU_C * sp)
        dlam = jnp.sum(dlog_a * (-LRU_C * r), axis=0, keepdims=True) * (-_sigmoid(-lam))
        dpr = dr * r * (1.0 - r)
        dpi = di * i * (1.0 - i)
        dprb = dpr.astype(BF16)
        dpib = dpi.astype(BF16)
        dxc = dxc + _dot_nt(dprb, wa) + _dot_nt(dpib, wx)
        dwa_ref[...] += _dot_tn(xcb, dprb)
        dwx_ref[...] += _dot_tn(xcb, dpib)
        n = TIME_BLOCK + SUBLANES
        later = []
        for k in range(1, CONV_TAPS):
            shifted = [pltpu.roll(jnp.concatenate([dxc[ex * TIME_BLOCK:(ex + 1) * TIME_BLOCK], dxc_next[ex]], axis=0),
                                  n - k, 0)[:TIME_BLOCK] for ex in examples]
            later.append(jnp.concatenate(shifted, axis=0))
        dxb = cw[CONV_TAPS - 1:CONV_TAPS, :] * dxc
        for k in range(1, CONV_TAPS):
            dxb = dxb + cw[CONV_TAPS - 1 - k:CONV_TAPS - k, :] * later[k - 1]
        dxb = dxb.astype(BF16)
        for ex in examples:
            rows_ex = slice(ex * TIME_BLOCK, (ex + 1) * TIME_BLOCK)
            dxg_ref[ex, :, :LRU_W] = dxb[rows_ex]
            dxg_ref[ex, :, LRU_W:] = dgb[rows_ex]
            dh_carry[ex] = a[ex * TIME_BLOCK:ex * TIME_BLOCK + 1, :] * dh[ex * TIME_BLOCK:ex * TIME_BLOCK + 1, :]
            dxc_next[ex] = dxc[ex * TIME_BLOCK:ex * TIME_BLOCK + SUBLANES, :]
        vecs = [jnp.sum(dxc * taps[CONV_TAPS - 1 - t], axis=0, keepdims=True) for t in range(CONV_TAPS)]
        vecs += [jnp.sum(dxc, axis=0, keepdims=True), jnp.sum(dpr, axis=0, keepdims=True),
                 jnp.sum(dpi, axis=0, keepdims=True), dlam]
        vec_ref[...] += _put_rows(SUBLANES, vecs)

    def tblk(j):
        return n_blocks - 1 - j

    def blk(width):
        return pl.BlockSpec((batch, TIME_BLOCK, width), lambda j: (0, tblk(j), 0))

    per8 = TIME_BLOCK // SUBLANES

    def halo(width):
        return pl.BlockSpec((batch, SUBLANES, width), lambda j: (0, jnp.maximum(per8 * tblk(j) - 1, 0), 0))

    small = [conv_w, conv_b, wa, ba, wx, bx, lam]
    acc = lambda shape: pl.BlockSpec(shape, lambda j: (0, 0))
    xg3, hs3 = _by_example(xg, batch), _by_example(hs, batch)
    dxg, dwa, dwx, vec = pl.pallas_call(
        body, name="lru_bwd", grid=(n_blocks,),
        in_specs=[blk(2 * LRU_W), halo(2 * LRU_W), blk(LRU_W), halo(LRU_W), blk(LRU_W)] + [_whole_spec(a) for a in small],
        out_specs=[blk(2 * LRU_W), acc((LRU_W, LRU_W)), acc((LRU_W, LRU_W)), acc((SUBLANES, LRU_W))],
        out_shape=[_sds((batch, rows // batch, 2 * LRU_W), BF16), _sds((LRU_W, LRU_W), F32), _sds((LRU_W, LRU_W), F32),
                   _sds((SUBLANES, LRU_W), F32)],
        scratch_shapes=[pltpu.VMEM((batch, 1, LRU_W), F32), pltpu.VMEM((batch, SUBLANES, LRU_W), F32)],
        compiler_params=_params("arbitrary"),
    )(xg3, xg3, hs3, hs3, _by_example(d_lru, batch), *small)
    return dxg.reshape(rows, 2 * LRU_W), dwa, dwx, vec


def _out_mlp_fwd(attn, lru, h, w_out, w_up4, w_down, g_attn, g_lru, g_post_mix, g_pre_mlp, g_post_mlp):
    rows = h.shape[0]

    def body(at_ref, lr_ref, h_ref, wo_ref, wu_ref, wd_ref, ga_ref, gl_ref, gp_ref, g1_ref, g2_ref,
             grp_ref, o_ref, hm_ref, z_ref, slope_ref, act_ref, y_ref, ho_ref):
        a = _rms(at_ref[...], ga_ref[...]).astype(BF16)
        l = _rms(lr_ref[...], gl_ref[...]).astype(BF16)
        grp_ref[:, :ATTN_W] = a
        grp_ref[:, ATTN_W:] = l
        o = _dot(a, wo_ref[:ATTN_W, :]) + _dot(l, wo_ref[ATTN_W:, :])
        o_ref[...] = o
        x = h_ref[...] + _rms(o, gp_ref[...])
        hm_ref[...] = x
        z = _rms(x, g1_ref[...]).astype(BF16)
        z_ref[...] = z
        y = jnp.zeros((ROW_TILE, D_MODEL), F32)
        for s in range(N_CHIPS):
            cols = slice(s * D_MODEL, (s + 1) * D_MODEL)
            r = jnp.maximum(_dot(z, wu_ref[s]), 0.0)
            slope_ref[:, cols] = (2.0 * r).astype(BF16)
            act = jnp.square(r).astype(BF16)
            act_ref[:, cols] = act
            y = y + _dot(act, wd_ref[cols, :])
        y_ref[...] = y
        ho_ref[...] = x + _rms(y, g2_ref[...])

    gains = [g_attn, g_lru, g_post_mix, g_pre_mlp, g_post_mlp]
    return pl.pallas_call(
        body, name="out_mlp_fwd", grid=(rows // ROW_TILE,),
        in_specs=[_row_spec(ATTN_W), _row_spec(LRU_W), _row_spec(D_MODEL), _resident_spec(w_out), _resident_spec(w_up4),
                  _resident_spec(w_down)] + [_whole_spec(g) for g in gains],
        out_specs=[_row_spec(D_MODEL), _row_spec(D_MODEL), _row_spec(D_MODEL), _row_spec(D_MODEL), _row_spec(D_FF),
                   _row_spec(D_FF), _row_spec(D_MODEL), _row_spec(D_MODEL)],
        out_shape=[_sds((rows, D_MODEL), BF16), _sds((rows, D_MODEL), F32), _sds((rows, D_MODEL), F32),
                   _sds((rows, D_MODEL), BF16), _sds((rows, D_FF), BF16), _sds((rows, D_FF), BF16),
                   _sds((rows, D_MODEL), F32), _sds((rows, D_MODEL), F32)],
        compiler_params=_params("parallel"),
    )(attn, lru, h, w_out, w_up4, w_down, *gains)


def _mlp_out_bwd(dh_out, h_mid, y, slope, o, attn, lru, w_out, w_up4, w_down, g_attn, g_lru, g_post_mix, g_pre_mlp,
                 g_post_mlp):
    rows = y.shape[0]

    def body(dh_ref, hm_ref, y_ref, slope_ref, o_ref, at_ref, lr_ref, wo_ref, wu_ref, wd_ref, ga_ref, gl_ref, gp_ref,
             g1_ref, g2_ref, dhm_ref, dy_ref, dup_ref, do_ref, dat_ref, dlr_ref, gacc_ref):
        @pl.when(pl.program_id(0) == 0)
        def _():
            gacc_ref[...] = jnp.zeros_like(gacc_ref)

        dh = dh_ref[...]
        dy, dg2 = _rms_bwd(dh, y_ref[...], g2_ref[...])
        dyb = dy.astype(BF16)
        dy_ref[...] = dyb
        dz = jnp.zeros((ROW_TILE, D_MODEL), F32)
        for s in range(N_CHIPS):
            cols = slice(s * D_MODEL, (s + 1) * D_MODEL)
            dact = _dot_nt(dyb, wd_ref[cols, :])
            dup = (dact * slope_ref[:, cols].astype(F32)).astype(BF16)
            dup_ref[:, cols] = dup
            dz = dz + _dot_nt(dup, wu_ref[s])
        dx, dg1 = _rms_bwd(dz, hm_ref[...], g1_ref[...])
        dhm = dh + dx
        dhm_ref[...] = dhm
        do, dgp = _rms_bwd(dhm, o_ref[...], gp_ref[...])
        dob = do.astype(BF16)
        do_ref[...] = dob
        dat, dga = _rms_bwd(_dot_nt(dob, wo_ref[:ATTN_W, :]), at_ref[...], ga_ref[...])
        dlr, dgl = _rms_bwd(_dot_nt(dob, wo_ref[ATTN_W:, :]), lr_ref[...], gl_ref[...])
        dat_ref[...] = dat
        dlr_ref[...] = dlr
        gacc_ref[...] += _put_rows(SUBLANES, [dg1, dg2, dgp, jnp.concatenate([dga, dgl], axis=1)])

    gains = [g_attn, g_lru, g_post_mix, g_pre_mlp, g_post_mlp]
    return pl.pallas_call(
        body, name="mlp_out_bwd", grid=(rows // ROW_TILE,),
        in_specs=[_row_spec(D_MODEL), _row_spec(D_MODEL), _row_spec(D_MODEL), _row_spec(D_FF), _row_spec(D_MODEL),
                  _row_spec(ATTN_W), _row_spec(LRU_W), _resident_spec(w_out), _resident_spec(w_up4), _resident_spec(w_down)]
                 + [_whole_spec(g) for g in gains],
        out_specs=[_row_spec(D_MODEL), _row_spec(D_MODEL), _row_spec(D_FF), _row_spec(D_MODEL), _row_spec(ATTN_W),
                   _row_spec(LRU_W), pl.BlockSpec((SUBLANES, D_MODEL), lambda i: (0, 0))],
        out_shape=[_sds((rows, D_MODEL), F32), _sds((rows, D_MODEL), BF16), _sds((rows, D_FF), BF16),
                   _sds((rows, D_MODEL), BF16), _sds((rows, ATTN_W), F32), _sds((rows, LRU_W), F32),
                   _sds((SUBLANES, D_MODEL), F32)],
        compiler_params=_params("arbitrary"),
    )(dh_out, h_mid, y, slope, o, attn, lru, w_out, w_up4, w_down, *gains)


def _matmul_tn(a, b, tm, tn, name, out_dtype, column_blocks=False):
    rows, m = a.shape
    n = b.shape[1]

    def body(a_ref, b_ref, o_ref):
        o_ref[...] = _dot_tn(a_ref[...], b_ref[...]).astype(out_dtype)

    if column_blocks:
        out_spec = pl.BlockSpec((None, tm, tn), lambda i, j: (j, i, 0))
        out_shape = _sds((n // tn, m, tn), out_dtype)
    else:
        out_spec = pl.BlockSpec((tm, tn), lambda i, j: (i, j))
        out_shape = _sds((m, n), out_dtype)
    return pl.pallas_call(
        body, name=name, grid=(m // tm, n // tn),
        in_specs=[pl.BlockSpec((rows, tm), lambda i, j: (0, i)), pl.BlockSpec((rows, tn), lambda i, j: (0, j))],
        out_specs=out_spec, out_shape=out_shape,
        compiler_params=_params("parallel", "parallel"),
    )(a, b)


LOSS_COLS = 256


def _loss_head(h, loss_target):
    batch, per_example, _ = h.shape
    seq = loss_target.shape[1]
    n_real = N_META + seq
    assert seq % SUBLANES == 0

    def body(h_ref, t_ref, dh_ref, l_ref):
        @pl.when((pl.program_id(0) == 0) & (pl.program_id(1) == 0))
        def _():
            l_ref[...] = jnp.zeros_like(l_ref)

        d = h_ref[N_META:n_real, :] - t_ref[...]
        dh_ref[:N_META, :] = jnp.zeros((N_META, LOSS_COLS), F32)
        dh_ref[N_META:n_real, :] = d * (1.0 / D_MODEL)
        dh_ref[n_real:, :] = jnp.zeros((per_example - n_real, LOSS_COLS), F32)
        l_ref[...] += jnp.sum(jnp.sum(d * d, axis=0, keepdims=True), axis=1, keepdims=True)

    blk = pl.BlockSpec((None, per_example, LOSS_COLS), lambda b, j: (b, 0, j))
    return pl.pallas_call(
        body, name="loss_head", grid=(batch, D_MODEL // LOSS_COLS),
        in_specs=[blk, pl.BlockSpec((None, seq, LOSS_COLS), lambda b, j: (b, 0, j))],
        out_specs=[blk, pl.BlockSpec((SUBLANES, LANES), lambda b, j: (0, 0))],
        out_shape=[_sds(h.shape, F32), _sds((SUBLANES, LANES), F32)],
        compiler_params=_params("arbitrary", "arbitrary"),
    )(h, loss_target)


def _meta_grad(dh0, batch, rows_per_example):
    per = rows_per_example // N_META

    def body(d_ref, o_ref):
        @pl.when(pl.program_id(0) == 0)
        def _():
            o_ref[...] = jnp.zeros_like(o_ref)

        o_ref[...] += d_ref[...]

    return pl.pallas_call(
        body, name="meta_grad", grid=(batch,),
        in_specs=[pl.BlockSpec((N_META, D_MODEL), lambda b: (b * per, 0))],
        out_specs=pl.BlockSpec((N_META, D_MODEL), lambda b: (0, 0)),
        out_shape=_sds((N_META, D_MODEL), F32),
        compiler_params=_params("arbitrary"),
    )(dh0)


def _elementwise_tile(rows, cols):
    tile = rows
    while tile * cols * 4 > (1 << 20) and tile % 16 == 0:
        tile //= 2
    return tile


def _sum_slots(buf, name):
    k, rows, cols = buf.shape
    tile = _elementwise_tile(rows, cols)

    def body(*refs):
        total = refs[0][...].astype(F32)
        for r in refs[1:k]:
            total = total + r[...].astype(F32)
        refs[k][...] = total

    def slot(s):
        return pl.BlockSpec((None, tile, cols), lambda i: (s, i, 0))

    return pl.pallas_call(
        body, name=name, grid=(rows // tile,),
        in_specs=[slot(s) for s in range(k)], out_specs=pl.BlockSpec((tile, cols), lambda i: (i, 0)),
        out_shape=_sds((rows, cols), F32), compiler_params=_params("parallel"),
    )(*([buf] * k))


def _sum_pieces(pieces, landed, chip, layer, n_layers, stacked):
    _, rows, cols = pieces.shape
    tile = _elementwise_tile(rows, cols)

    def body(chip_ref, own_ref, a_ref, b_ref, c_ref, *rest):
        o_ref, token = rest[-2:]
        total = ((own_ref[...].astype(F32) + a_ref[...].astype(F32)) + b_ref[...].astype(F32)) + c_ref[...].astype(F32)
        o_ref[...] = total.astype(BF16)
        token[...] = jnp.zeros_like(token)

    def slot(offset):
        return pl.BlockSpec((None, tile, cols), lambda i, chip_ref: ((chip_ref[0] + offset) % N_CHIPS, i, 0))

    carried = [] if stacked is None else [stacked]
    grid_spec = pltpu.PrefetchScalarGridSpec(
        num_scalar_prefetch=1, grid=(rows // tile,),
        in_specs=[slot(0), slot(1), slot(2), slot(3)] + [pl.BlockSpec(memory_space=pl.ANY)] * len(carried),
        out_specs=[pl.BlockSpec((None, tile, cols), lambda i, chip_ref: (layer, i, 0)),
                   pl.BlockSpec((SUBLANES, LANES), lambda i, chip_ref: (0, 0))])
    return pl.pallas_call(
        body, name="sum_grad_pieces", grid_spec=grid_spec,
        out_shape=[_sds((n_layers, rows, cols), BF16), _sds((SUBLANES, LANES), F32)],
        input_output_aliases={5: 0} if carried else {},
        compiler_params=_params("arbitrary"),
    )(chip, pieces, landed, landed, landed, *carried)


def _adamw(w, m, v, grads, name):
    flat = w.ndim == 2
    if flat:
        w, m, v = w[None], m[None], v[None]
        grads = [g[None] for g in grads]
    layers, rows, cols = w.shape
    tile = _elementwise_tile(rows, cols)
    ng = len(grads)
    m_scale = 1.0 - ADAM_B1 ** ADAM_STEP
    v_scale = 1.0 - ADAM_B2 ** ADAM_STEP

    def body(*refs):
        w_ref, m_ref, v_ref = refs[:3]
        g_refs = refs[3:3 + ng]
        g_out, d_out, m_out, v_out = refs[-4:]
        g = g_refs[0][...].astype(F32)
        for r in g_refs[1:]:
            g = g + r[...].astype(F32)
        m_new = ADAM_B1 * m_ref[...] + (1.0 - ADAM_B1) * g
        v_new = ADAM_B2 * v_ref[...] + (1.0 - ADAM_B2) * (g * g)
        m_hat = m_new / m_scale
        v_hat = v_new / v_scale
        g_out[...] = g
        d_out[...] = -ADAM_LR * (m_hat / (jnp.sqrt(v_hat) + ADAM_EPS) + ADAM_WD * w_ref[...])
        m_out[...] = m_new
        v_out[...] = v_new

    spec = pl.BlockSpec((None, tile, cols), lambda l, i: (l, i, 0))
    out = pl.pallas_call(
        body, name=name, grid=(layers, rows // tile),
        in_specs=[spec] * (3 + ng), out_specs=[spec] * 4, out_shape=[_sds(w.shape, F32)] * 4,
        compiler_params=_params("parallel", "parallel"),
    )(w, m, v, *grads)
    return [o[0] for o in out] if flat else out


def _position():
    return lax.axis_index("x"), lax.axis_index("y"), lax.axis_index("c")


def _other_chips(x, y):
    return [(1 - x, y), (x, 1 - y), (1 - x, 1 - y)]


def _exchange_chips(arrays, name, scatter):
    n = len(arrays)

    def body(*refs):
        src, dst = refs[:n], refs[n:2 * n]
        send_sems, recv_sems, local_sems = refs[2 * n:]
        x, y, c = _position()
        mine = 2 * x + y
        copies = []
        for i in range(n):
            own = src[i].at[mine] if scatter else src[i]
            copies.append(pltpu.make_async_copy(own, dst[i].at[mine], local_sems.at[i]))
        for k, (px, py) in enumerate(_other_chips(x, y)):
            for i in range(n):
                piece = src[i].at[2 * px + py] if scatter else src[i]
                copies.append(pltpu.make_async_remote_copy(
                    src_ref=piece, dst_ref=dst[i].at[mine], send_sem=send_sems.at[k, i], recv_sem=recv_sems.at[k, i],
                    device_id=(px, py, c), device_id_type=MESH))
        for cp in copies:
            cp.start()
        for cp in copies:
            cp.wait()

    def out_shape(a):
        return _sds(a.shape if scatter else (N_CHIPS,) + a.shape, a.dtype)

    return pl.pallas_call(
        body, name=name, in_specs=[HBM_SPEC] * n, out_specs=[HBM_SPEC] * n, out_shape=[out_shape(a) for a in arrays],
        scratch_shapes=[pltpu.SemaphoreType.DMA((N_CHIPS - 1, n)), pltpu.SemaphoreType.DMA((N_CHIPS - 1, n)),
                        pltpu.SemaphoreType.DMA((n,))],
    )(*arrays)


def _swap_sibling(arrays, name):
    n = len(arrays)

    def body(*refs):
        src, dst = refs[:n], refs[n:2 * n]
        send_sems, recv_sems = refs[2 * n:]
        x, y, c = _position()
        copies = [pltpu.make_async_remote_copy(
            src_ref=src[i], dst_ref=dst[i], send_sem=send_sems.at[i], recv_sem=recv_sems.at[i],
            device_id=(x, y, 1 - c), device_id_type=MESH) for i in range(n)]
        for cp in copies:
            cp.start()
        for cp in copies:
            cp.wait()

    return pl.pallas_call(
        body, name=name, in_specs=[HBM_SPEC] * n, out_specs=[HBM_SPEC] * n,
        out_shape=[_sds(a.shape, a.dtype) for a in arrays],
        scratch_shapes=[pltpu.SemaphoreType.DMA((n,)), pltpu.SemaphoreType.DMA((n,))],
    )(*arrays)


SEM_SPEC = pl.BlockSpec(memory_space=pltpu.SEMAPHORE)
ANY_SPEC = pl.BlockSpec(memory_space=pl.ANY)
IN_FLIGHT = pltpu.SideEffectType.DATAFLOW_SIDE_EFFECTING


def _peer_sems(all_devices):
    return pltpu.SemaphoreType.DMA(((N_DEV if all_devices else N_CHIPS) - 1,))


def _own_slot(all_devices, x, y, c):
    return 4 * x + 2 * y + c if all_devices else 2 * x + y


def _peers(all_devices, x, y, c):
    if not all_devices:
        return [((px, py, c), 2 * px + py) for px, py in _other_chips(x, y)]
    out = []
    for fx in range(2):
        for fy in range(2):
            for fc in range(2):
                if fx or fy or fc:
                    px, py, pc = (1 - x if fx else x), (1 - y if fy else y), (1 - c if fc else c)
                    out.append(((px, py, pc), 4 * px + 2 * py + pc))
    return out


def _in_hbm(a):
    return pltpu.with_memory_space_constraint(a, pltpu.HBM)


def _place_slot(a, slot, n_slots, name):
    rows, cols = a.shape
    tile = _elementwise_tile(rows, cols)

    def body(slot_ref, a_ref, o_ref):
        o_ref[...] = a_ref[...]

    grid_spec = pltpu.PrefetchScalarGridSpec(
        num_scalar_prefetch=1, grid=(rows // tile,),
        in_specs=[pl.BlockSpec((tile, cols), lambda i, slot_ref: (i, 0))],
        out_specs=pl.BlockSpec((None, tile, cols), lambda i, slot_ref: (slot_ref[0], i, 0)))
    return pl.pallas_call(
        body, name=name, grid_spec=grid_spec, out_shape=_sds((n_slots, rows, cols), a.dtype),
        compiler_params=_params("parallel"),
    )(slot, a)


def _place_shard(w, layer, chip, name):
    _, rows, cols = w.shape
    tile = _elementwise_tile(rows, cols)

    def body(chip_ref, w_ref, o_ref):
        o_ref[...] = w_ref[...].astype(BF16)

    grid_spec = pltpu.PrefetchScalarGridSpec(
        num_scalar_prefetch=1, grid=(rows // tile,),
        in_specs=[pl.BlockSpec((None, tile, cols), lambda i, chip_ref: (layer, i, 0))],
        out_specs=pl.BlockSpec((None, tile, cols), lambda i, chip_ref: (chip_ref[0], i, 0)))
    return pl.pallas_call(
        body, name=name, grid_spec=grid_spec, out_shape=_sds((N_CHIPS, rows, cols), BF16),
        compiler_params=_params("parallel"),
    )(chip, w)


def _rows_of(ref, slot, core, halves):
    if not halves:
        return ref.at[slot]
    half = ref.shape[1] // 2
    return ref.at[slot, pl.ds(pl.multiple_of(core * half, half), half)]


def _gather_start(bufs, after, name, all_devices=False, halves=False):
    n = len(bufs)

    def body(*refs):
        buf = refs[:n]
        send, recv = refs[n + 1:2 * n + 1], refs[2 * n + 1:3 * n + 1]
        token = refs[4 * n + 1]
        x, y, c = _position()
        mine = _own_slot(all_devices, x, y, c)
        for i in range(n):
            for k, (peer, _) in enumerate(_peers(all_devices, x, y, c)):
                rows = _rows_of(buf[i], mine, c, halves)
                pltpu.make_async_remote_copy(
                    src_ref=rows, dst_ref=rows, send_sem=send[i].at[k], recv_sem=recv[i].at[k],
                    device_id=peer, device_id_type=MESH).start()
        token[...] = jnp.zeros_like(token)

    sems = _peer_sems(all_devices)
    out = pl.pallas_call(
        body, name=name, in_specs=[HBM_SPEC] * n + [ANY_SPEC],
        out_specs=[SEM_SPEC] * (2 * n) + [HBM_SPEC] * n + [pl.BlockSpec(memory_space=pltpu.VMEM)],
        out_shape=[sems] * (2 * n) + [pltpu.HBM(b.shape, b.dtype) for b in bufs] + [_sds((SUBLANES, LANES), F32)],
        input_output_aliases={i: 2 * n + i for i in range(n)},
        compiler_params=pltpu.CompilerParams(has_side_effects=IN_FLIGHT),
    )(*[_in_hbm(b) for b in bufs], after)
    return [(out[i], out[n + i], out[2 * n + i]) for i in range(n)], out[3 * n]


def _gather_relay(flights, after, name):
    n = len(flights)

    def body(*refs):
        buf, send, recv = refs[:n], refs[n:2 * n], refs[2 * n:3 * n]
        send2, recv2 = refs[3 * n + 1:4 * n + 1], refs[4 * n + 1:5 * n + 1]
        token = refs[6 * n + 1]
        x, y, c = _position()
        mine = 2 * x + y
        for i in range(n):
            for k, (peer, slot) in enumerate(_peers(False, x, y, c)):
                arrived = _rows_of(buf[i], slot, c, True)
                cp = pltpu.make_async_remote_copy(
                    src_ref=_rows_of(buf[i], mine, c, True), dst_ref=arrived, send_sem=send[i].at[k], recv_sem=recv[i].at[k],
                    device_id=peer, device_id_type=MESH)
                cp.wait_send()
                cp.wait_recv()
                pltpu.make_async_remote_copy(
                    src_ref=arrived, dst_ref=arrived, send_sem=send2[i].at[k], recv_sem=recv2[i].at[k],
                    device_id=(x, y, 1 - c), device_id_type=MESH).start()
        token[...] = jnp.zeros_like(token)

    bufs = [f[2] for f in flights]
    sems = _peer_sems(False)
    out = pl.pallas_call(
        body, name=name, in_specs=[HBM_SPEC] * n + [SEM_SPEC] * (2 * n) + [ANY_SPEC],
        out_specs=[SEM_SPEC] * (2 * n) + [HBM_SPEC] * n + [pl.BlockSpec(memory_space=pltpu.VMEM)],
        out_shape=[sems] * (2 * n) + [pltpu.HBM(b.shape, b.dtype) for b in bufs] + [_sds((SUBLANES, LANES), F32)],
        input_output_aliases={i: 2 * n + i for i in range(n)},
        compiler_params=pltpu.CompilerParams(has_side_effects=IN_FLIGHT),
    )(*bufs, *[f[0] for f in flights], *[f[1] for f in flights], after)
    return [(out[i], out[n + i], out[2 * n + i]) for i in range(n)], out[3 * n]


def _gather_wait(send, recv, buf, after, name, all_devices=False, relayed=False):
    def body(buf_ref, send_ref, recv_ref, after_ref, out_ref):
        x, y, c = _position()
        mine = _own_slot(all_devices, x, y, c)
        for k, (peer, slot) in enumerate(_peers(all_devices, x, y, c)):
            if relayed:
                cp = pltpu.make_async_remote_copy(
                    src_ref=_rows_of(buf_ref, slot, c, True), dst_ref=_rows_of(buf_ref, slot, 1 - c, True),
                    send_sem=send_ref.at[k], recv_sem=recv_ref.at[k], device_id=(x, y, 1 - c), device_id_type=MESH)
            else:
                cp = pltpu.make_async_remote_copy(
                    src_ref=buf_ref.at[mine], dst_ref=buf_ref.at[slot], send_sem=send_ref.at[k], recv_sem=recv_ref.at[k],
                    device_id=peer, device_id_type=MESH)
            cp.wait_send()
            cp.wait_recv()

    return pl.pallas_call(
        body, name=name, in_specs=[HBM_SPEC, SEM_SPEC, SEM_SPEC, ANY_SPEC], out_specs=HBM_SPEC,
        out_shape=pltpu.HBM(buf.shape, buf.dtype), input_output_aliases={0: 0},
        compiler_params=pltpu.CompilerParams(has_side_effects=IN_FLIGHT),
    )(buf, send, recv, after)


def _scatter_start(pieces, name):
    n = len(pieces)

    def body(*refs):
        src = refs[:n]
        send, recv = refs[n:2 * n], refs[2 * n:3 * n]
        land = refs[4 * n:5 * n]
        token = refs[5 * n]
        x, y, c = _position()
        mine = 2 * x + y
        for i in range(n):
            for k, (px, py) in enumerate(_other_chips(x, y)):
                pltpu.make_async_remote_copy(
                    src_ref=src[i].at[2 * px + py], dst_ref=land[i].at[mine], send_sem=send[i].at[k], recv_sem=recv[i].at[k],
                    device_id=(px, py, c), device_id_type=MESH).start()
        token[...] = jnp.zeros_like(token)

    hbm = [pltpu.HBM(p.shape, p.dtype) for p in pieces]
    out = pl.pallas_call(
        body, name=name, in_specs=[HBM_SPEC] * n,
        out_specs=[SEM_SPEC] * (2 * n) + [HBM_SPEC] * (2 * n) + [pl.BlockSpec(memory_space=pltpu.VMEM)],
        out_shape=[_peer_sems(False)] * (2 * n) + hbm + hbm + [_sds((SUBLANES, LANES), F32)],
        input_output_aliases={i: 2 * n + i for i in range(n)},
        compiler_params=pltpu.CompilerParams(has_side_effects=IN_FLIGHT),
    )(*[_in_hbm(p) for p in pieces])
    return [(out[i], out[n + i], out[2 * n + i], out[3 * n + i]) for i in range(n)], out[4 * n]


def _scatter_wait(send, recv, pieces, land, after, name):
    def body(src_ref, land_ref, send_ref, recv_ref, after_ref, src_out, land_out):
        x, y, c = _position()
        for k, (px, py) in enumerate(_other_chips(x, y)):
            cp = pltpu.make_async_remote_copy(
                src_ref=src_ref.at[2 * px + py], dst_ref=land_ref.at[2 * px + py], send_sem=send_ref.at[k],
                recv_sem=recv_ref.at[k], device_id=(px, py, c), device_id_type=MESH)
            cp.wait_send()
            cp.wait_recv()

    return pl.pallas_call(
        body, name=name, in_specs=[HBM_SPEC, HBM_SPEC, SEM_SPEC, SEM_SPEC, ANY_SPEC], out_specs=[HBM_SPEC, HBM_SPEC],
        out_shape=[pltpu.HBM(pieces.shape, pieces.dtype), pltpu.HBM(land.shape, land.dtype)],
        input_output_aliases={0: 0, 1: 1},
        compiler_params=pltpu.CompilerParams(has_side_effects=IN_FLIGHT),
    )(pieces, land, send, recv, after)


def _rope_tables(batch, rows_per_example):
    inv_freq = ROPE_THETA ** (-jnp.arange(0, 64, 2, dtype=F32) / 64)
    ang = jnp.arange(rows_per_example, dtype=F32)[:, None] * inv_freq[None, :]
    cos, sin = jnp.cos(ang), jnp.sin(ang)
    cos128 = jnp.concatenate([cos, cos, cos, cos], axis=1)
    sin128 = jnp.concatenate([-sin, sin, -sin, sin], axis=1)
    return jnp.tile(cos128, (batch, 1)), jnp.tile(sin128, (batch, 1))


def _block_diagonal(w):
    eye = jnp.eye(LRU_BLOCKS, dtype=w.dtype)
    return (w[:, :, None, :] * eye[:, None, :, None]).reshape(LRU_W, LRU_W)


def _diagonal_blocks(dense):
    d4 = dense.reshape(LRU_BLOCKS, LRU_BLOCK, LRU_BLOCKS, LRU_BLOCK)
    return jnp.stack([d4[n, :, n, :] for n in range(LRU_BLOCKS)])


def _row(v):
    return v.reshape(1, -1)


def _local_step(x, loss_target, meta_tokens, small, depth, big_weight, on_layer_grads):
    batch, seq, _ = x.shape
    n_real = N_META + seq
    n_blocks = -(-n_real // TIME_BLOCK)
    per_example = n_blocks * TIME_BLOCK
    pad = per_example - n_real
    meta = jnp.broadcast_to(meta_tokens[None], (batch, N_META, D_MODEL))
    h = jnp.concatenate([meta, x, jnp.zeros((batch, pad, D_MODEL), F32)], axis=1).reshape(batch * per_example, D_MODEL)
    rope_cos, rope_sin = _rope_tables(batch, per_example)

    saved = []
    for l in range(depth):
        wa = _block_diagonal(small['gate_a_w'][l]).astype(BF16)
        wx = _block_diagonal(small['gate_x_w'][l]).astype(BF16)
        lru_small = (small['conv_w'][l], _row(small['conv_b'][l]), wa, _row(small['gate_a_b'][l]), wx,
                     _row(small['gate_x_b'][l]), _row(small['lru_lambda'][l]))
        w_in = big_weight('w_in', l, h)
        z1, q, kv, xg = _in_proj_fwd(h, _row(small['pre_mix_norm'][l]), w_in, rope_cos, rope_sin)
        attn, lse = _attn_fwd(q, kv, small['attn_sinks'][l], batch, n_blocks)
        hs, lru = _lru_fwd(xg, *lru_small, batch, n_blocks)
        w_out = big_weight('w_out', l, xg)
        w_up4 = big_weight('w_up', l, xg)
        w_down = big_weight('w_down', l, xg)
        gains = [_row(small[n][l]) for n in ('attn_group_norm', 'lru_group_norm', 'post_mix_norm', 'pre_mlp_norm',
                                             'post_mlp_norm')]
        groups, o, h_mid, z2, slope, act, y, h_out = _out_mlp_fwd(attn, lru, h, w_out, w_up4, w_down, *gains)
        saved.append(dict(h=h, z1=z1, q=q, kv=kv, xg=xg, attn=attn, lse=lse, hs=hs, lru=lru, groups=groups, o=o,
                          h_mid=h_mid, z2=z2, slope=slope, act=act, y=y, lru_small=lru_small, w_in=w_in, w_out=w_out,
                          w_up4=w_up4, w_down=w_down, gains=gains))
        h = h_out

    dh, sq_err = _loss_head(h.reshape(batch, per_example, D_MODEL), loss_target)
    dh = dh.reshape(batch * per_example, D_MODEL)

    gs = {n: [None] * depth for n in REPLICATED_NAMES + ('conv_w',)}
    handed_over = None
    for l in reversed(range(depth)):
        s = saved[l]

        def ordered_after(gain):
            return gain if handed_over is None else gain + handed_over[0, 0]

        gains = list(s['gains'])
        gains[4] = ordered_after(gains[4])
        dh_mid, dy, dup, do, d_attn, d_lru, g_rows = _mlp_out_bwd(dh, s['h_mid'], s['y'], s['slope'], s['o'], s['attn'],
                                                                s['lru'], s['w_out'], s['w_up4'], s['w_down'], *gains)
        handed_over = on_layer_grads(l, {
            'w_down': _matmul_tn(s['act'], dy, 512, D_MODEL, "grad_w_down", BF16),
            'w_up': _matmul_tn(s['z2'], dup, 512, D_MODEL, "grad_w_up", BF16, column_blocks=True)})
        lru_small = list(s['lru_small'])
        lru_small[1] = ordered_after(lru_small[1])
        dxg, dwa, dwx, g_lru = _lru_bwd(s['xg'], s['hs'], d_lru, *lru_small, batch, n_blocks)
        dq, dkv, dsink = _attn_bwd(s['q'], s['kv'], small['attn_sinks'][l], s['attn'], s['lse'], d_attn, batch, n_blocks)
        dh, dproj, g_in = _in_proj_bwd(dh_mid, s['h'], dq, dkv, dxg, rope_cos, rope_sin, _row(small['pre_mix_norm'][l]),
                                       s['w_in'])
        handed_over = on_layer_grads(l, {
            'w_out': _matmul_tn(s['groups'], do, 512, D_MODEL, "grad_w_out", BF16),
            'w_in': _matmul_tn(dproj, s['z1'], IN_COLS // 2, D_MODEL, "grad_w_in", BF16)})
        gs['pre_mlp_norm'][l], gs['post_mlp_norm'][l] = g_rows[0], g_rows[1]
        gs['post_mix_norm'][l] = g_rows[2]
        gs['attn_group_norm'][l], gs['lru_group_norm'][l] = g_rows[3, :ATTN_W], g_rows[3, ATTN_W:]
        gs['pre_mix_norm'][l] = g_in[0]
        gs['attn_sinks'][l] = dsink[:, 0]
        gs['conv_w'][l] = g_lru[:CONV_TAPS]
        gs['conv_b'][l], gs['gate_a_b'][l], gs['gate_x_b'][l], gs['lru_lambda'][l] = g_lru[4], g_lru[5], g_lru[6], g_lru[7]
        gs['gate_a_w'][l] = _diagonal_blocks(dwa)
        gs['gate_x_w'][l] = _diagonal_blocks(dwx)

    grad_x = dh.reshape(batch, per_example, D_MODEL)[:, N_META:n_real]
    grad_meta = _meta_grad(dh, batch, per_example)
    small_grads = {n: jnp.stack(v) for n, v in gs.items()}
    return sq_err, grad_x, grad_meta, small_grads, handed_over


PACK_UNIT = SUBLANES * LANES


def _size(shape):
    size = 1
    for d in shape:
        size *= d
    return size


def _pack(arrays):
    parts = []
    for a in arrays:
        flat = a.reshape(-1)
        padded = -(-flat.shape[0] // PACK_UNIT) * PACK_UNIT
        if padded != flat.shape[0]:
            flat = jnp.pad(flat, (0, padded - flat.shape[0]))
        parts.append(flat.reshape(-1, LANES))
    return jnp.concatenate(parts, axis=0)


def _unpack(buf, shapes):
    out, at = [], 0
    for shp in shapes:
        size = _size(shp)
        rows = -(-size // PACK_UNIT) * SUBLANES
        part = buf[at:at + rows]
        if rows * LANES != size:
            part = part.reshape(-1)[:size]
        out.append(part.reshape(shp))
        at += rows
    return out


def kernel(x, meta_tokens, pre_mix_norm, w_in, attn_sinks, conv_w, conv_b, gate_a_w, gate_a_b, gate_x_w, gate_x_b, lru_lambda, attn_group_norm, lru_group_norm, w_out, post_mix_norm, pre_mlp_norm, w_up, w_down, post_mlp_norm, loss_target, m_meta_tokens, m_pre_mix_norm, m_w_in, m_attn_sinks, m_conv_w, m_conv_b, m_gate_a_w, m_gate_a_b, m_gate_x_w, m_gate_x_b, m_lru_lambda, m_attn_group_norm, m_lru_group_norm, m_w_out, m_post_mix_norm, m_pre_mlp_norm, m_w_up, m_w_down, m_post_mlp_norm, v_meta_tokens, v_pre_mix_norm, v_w_in, v_attn_sinks, v_conv_w, v_conv_b, v_gate_a_w, v_gate_a_b, v_gate_x_w, v_gate_x_b, v_lru_lambda, v_attn_group_norm, v_lru_group_norm, v_w_out, v_post_mix_norm, v_pre_mlp_norm, v_w_up, v_w_down, v_post_mlp_norm):
    given = dict(locals())
    w = {n: given[n] for n in WEIGHT_NAMES}
    m = {n: given['m_' + n] for n in WEIGHT_NAMES}
    v = {n: given['v_' + n] for n in WEIGHT_NAMES}
    depth = w_in.shape[0]
    for d in (w, m, v):
        d['w_in'] = jnp.swapaxes(d['w_in'], 1, 2)
    chip = 2 * lax.axis_index("x") + lax.axis_index("y")
    chip1 = chip.reshape(1).astype(jnp.int32)

    in_flight = []
    all_started = chip1
    for l in range(depth):
        bufs = [_place_shard(w[n], l, chip1, "place_" + n) for n in BIG_NAMES]
        handles, all_started = _gather_start(bufs, all_started, "gather_start_%d" % l, halves=True)
        in_flight.append(dict(zip(BIG_NAMES, handles)))
    first_use = {'w_in': ('w_in',), 'w_out': ('w_out', 'w_up', 'w_down')}

    def big_weight(name, l, after):
        if l == 0 and name == 'w_in':
            after = all_started
        names = first_use.get(name, ()) if l == 0 else (BIG_NAMES if name == 'w_in' else ())
        if names:
            relayed, _ = _gather_relay([in_flight[l][n] for n in names], after, "gather_relay_%d_%s" % (l, name))
            in_flight[l].update(zip(names, relayed))
        send, recv, buf = in_flight[l][name]
        full = _gather_wait(send, recv, buf, after, "gather_wait_%d_%s" % (l, name), relayed=True)
        if name == 'w_in':
            return full.reshape(IN_COLS, D_MODEL)
        if name == 'w_out':
            return full.reshape(D_MODEL, D_MODEL)
        if name == 'w_down':
            return full.reshape(D_FF, D_MODEL)
        return full

    gathered = _exchange_chips([meta_tokens, conv_w], "gather_small_weights", scatter=False)
    full_meta = jnp.concatenate([gathered[0][s] for s in range(N_CHIPS)], axis=1)
    small = {n: w[n] for n in REPLICATED_NAMES}
    small['conv_w'] = jnp.concatenate([gathered[1][s] for s in range(N_CHIPS)], axis=2)

    scattering = [{} for _ in range(depth)]

    def on_layer_grads(l, big):
        names = list(big)
        pieces = [big[n].reshape(N_CHIPS, -1, D_MODEL) for n in names]
        handles, started = _scatter_start(pieces, "scatter_start_%d_%s" % (l, names[0]))
        scattering[l].update(zip(names, handles))
        return started

    sq_err, grad_x, grad_meta, small_grads, all_handed_over = _local_step(x, loss_target, full_meta, small, depth, big_weight,
                                                                           on_layer_grads)
    loss = lax.psum(sq_err[0, 0] * (0.5 / D_MODEL), ("x", "y", "c"))

    small_names = list(REPLICATED_NAMES) + list(COLUMN_SHARDED_SMALL)
    small_full = dict(small_grads)
    small_full['meta_tokens'] = grad_meta
    device1 = (2 * chip + lax.axis_index("c")).reshape(1).astype(jnp.int32)
    packed = _place_slot(_pack([small_full[n] for n in small_names]), device1, N_DEV, "place_small_grads")
    [small_flight], done = _gather_start([packed], all_handed_over, "gather_small_grads_start", all_devices=True)

    partial = {n: None for n in BIG_NAMES}
    for l in reversed(range(depth)):
        for n in BIG_NAMES:
            send, recv, pieces, land = scattering[l][n]
            pieces, land = _scatter_wait(send, recv, pieces, land, done, "scatter_wait_%d_%s" % (l, n))
            partial[n], done = _sum_pieces(pieces, land, chip1, l, depth, partial[n])
    sibling = _swap_sibling([partial[n] for n in BIG_NAMES], "swap_partial_grads")
    results = {n: _adamw(w[n], m[n], v[n], [partial[n], other], "adamw_" + n) for n, other in zip(BIG_NAMES, sibling)}

    slots = _gather_wait(*small_flight, results['w_down'][0], "gather_small_grads_wait", all_devices=True)
    summed = _unpack(_sum_slots(slots, "sum_small_grads"), [small_full[n].shape for n in small_names])
    grads = dict(zip(small_names, summed))
    grads['meta_tokens'] = lax.dynamic_slice_in_dim(grads['meta_tokens'], chip * meta_tokens.shape[1], meta_tokens.shape[1], 1)
    grads['conv_w'] = lax.dynamic_slice_in_dim(grads['conv_w'], chip * conv_w.shape[2], conv_w.shape[2], 2)

    out_g, out_d, out_m, out_v = {}, {}, {}, {}
    for n in BIG_NAMES:
        out_g[n], out_d[n], out_m[n], out_v[n] = [jnp.swapaxes(r, 1, 2) if n == 'w_in' else r for r in results[n][:4]]
    shapes = [w[n].shape for n in small_names]
    res = _adamw(_pack([w[n] for n in small_names]), _pack([m[n] for n in small_names]), _pack([v[n] for n in small_names]),
                 [_pack([grads[n] for n in small_names])], "adamw_small")
    for k, store in enumerate((out_g, out_d, out_m, out_v)):
        for n, a in zip(small_names, _unpack(res[k], shapes)):
            store[n] = a

    return (loss, grad_x, *[out_g[n] for n in WEIGHT_NAMES], *[out_d[n] for n in WEIGHT_NAMES],
            *[out_m[n] for n in WEIGHT_NAMES], *[out_v[n] for n in WEIGHT_NAMES])
```

```python
import functools

import jax
import jax.numpy as jnp
from jax import lax
from jax.experimental import pallas as pl
from jax.experimental.pallas import tpu as pltpu

F32 = jnp.float32
BF16 = jnp.bfloat16

D_MODEL = 1024
N_HEADS = 8
ATTN_W = 512
KV_W = 128
LRU_W = 512
LRU_BLOCKS = 8
LRU_BLOCK = 64
IN_COLS = 1792
D_FF = 4096
N_META = 16
CONV_TAPS = 4
LRU_C = 8.0
ROPE_THETA = 10000.0
EPS = 1e-6
ATTN_SCALE = 0.125

ADAM_LR = 0.001
ADAM_B1 = 0.9
ADAM_B2 = 0.999
ADAM_EPS = 1e-08
ADAM_WD = 0.01
ADAM_STEP = 10

N_CHIPS = 4
N_DEV = 8
TIME_BLOCK = 128
ROW_TILE = 256
PROJ_ROW_TILE = 544
LANES = 128
SUBLANES = 8
MASKED = -1e30
VMEM_LIMIT = 56 * 1024 * 1024

MESH = pl.DeviceIdType.MESH
HBM_SPEC = pl.BlockSpec(memory_space=pltpu.HBM)

WEIGHT_NAMES = ['meta_tokens', 'pre_mix_norm', 'w_in', 'attn_sinks', 'conv_w', 'conv_b', 'gate_a_w', 'gate_a_b',
                'gate_x_w', 'gate_x_b', 'lru_lambda', 'attn_group_norm', 'lru_group_norm', 'w_out', 'post_mix_norm',
                'pre_mlp_norm', 'w_up', 'w_down', 'post_mlp_norm']
BIG_NAMES = ('w_in', 'w_out', 'w_up', 'w_down')
COLUMN_SHARDED_SMALL = ('meta_tokens', 'conv_w')
REPLICATED_NAMES = tuple(n for n in WEIGHT_NAMES if n not in BIG_NAMES and n not in COLUMN_SHARDED_SMALL)


def _sds(shape, dtype):
    return jax.ShapeDtypeStruct(tuple(shape), dtype)


def _params(*sem):
    return pltpu.CompilerParams(dimension_semantics=sem, vmem_limit_bytes=VMEM_LIMIT)


def _row_spec(width, tile=ROW_TILE):
    return pl.BlockSpec((tile, width), lambda i: (i, 0))


def _proj_tile(rows):
    tile = PROJ_ROW_TILE
    while rows % tile:
        tile -= 16
    return tile


def _whole_spec(a):
    nd = a.ndim
    return pl.BlockSpec(a.shape, lambda *_: (0,) * nd)


def _resident_spec(a):
    nd = a.ndim
    return pl.BlockSpec(a.shape, lambda *_: (0,) * nd, pipeline_mode=pl.Buffered(1))


def _rms(x, g):
    r = lax.rsqrt(jnp.mean(x * x, axis=-1, keepdims=True) + EPS)
    return x * r * g


def _rms_bwd(dy, x, g):
    r = lax.rsqrt(jnp.mean(x * x, axis=-1, keepdims=True) + EPS)
    xh = x * r
    dg = jnp.sum(dy * xh, axis=0, keepdims=True)
    dxh = dy * g
    dx = r * (dxh - xh * jnp.mean(dxh * xh, axis=-1, keepdims=True))
    return dx, dg


def _rope(x, cos, sin_signed):
    width = x.shape[1]
    reps = width // LANES
    if reps > 1:
        cos = jnp.tile(cos, (1, reps))
        sin_signed = jnp.tile(sin_signed, (1, reps))
    lane = lax.broadcasted_iota(jnp.int32, x.shape, 1)
    first_half = jnp.bitwise_and(lane, 32) == 0
    other = jnp.where(first_half, pltpu.roll(x, width - 32, 1), pltpu.roll(x, 32, 1))
    return x * cos + other * sin_signed


def _sigmoid(x):
    return 1.0 / (1.0 + jnp.exp(-x))


def _log1p(e):
    return jnp.where(e < 1e-3, e * (1.0 - e * (0.5 - e * (1.0 / 3.0))), jnp.log(1.0 + e))


def _one_minus_square(a, log_a):
    x = 2.0 * log_a
    return jnp.where(x > -0.002, x * (-1.0 - 0.5 * x), 1.0 - a * a)


GELU_K = 0.7978845608028654
GELU_C = 0.044715


def _gelu(x):
    t = jnp.tanh(GELU_K * (x + GELU_C * x * x * x))
    return 0.5 * x * (1.0 + t)


def _gelu_and_grad(x):
    x2 = x * x
    t = jnp.tanh(GELU_K * (x + GELU_C * x * x2))
    val = 0.5 * x * (1.0 + t)
    grad = 0.5 * (1.0 + t) + 0.5 * x * (1.0 - t * t) * GELU_K * (1.0 + 3.0 * GELU_C * x2)
    return val, grad


def _dot(a, b):
    return jnp.dot(a, b, preferred_element_type=F32)


def _dot_nt(a, b):
    return lax.dot_general(a, b, (((1,), (1,)), ((), ())), preferred_element_type=F32)


def _dot_tn(a, b):
    return lax.dot_general(a, b, (((0,), (0,)), ((), ())), preferred_element_type=F32)


def _put_rows(rows_8, values):
    d = values[0].shape[1]
    rowid = lax.broadcasted_iota(jnp.int32, (rows_8, d), 0)
    out = jnp.zeros((rows_8, d), F32)
    for k, v in enumerate(values):
        out = out + jnp.where(rowid == k, v, 0.0)
    return out


def _in_proj_fwd(h, gain, w_in, rope_cos, rope_sin):
    rows = h.shape[0]

    def body(h_ref, g_ref, w_ref, c_ref, s_ref, z_ref, q_ref, kv_ref, xg_ref):
        z = _rms(h_ref[...], g_ref[...]).astype(BF16)
        z_ref[...] = z
        proj = _dot_nt(z, w_ref[...])
        cos = c_ref[...]
        sin = s_ref[...]
        q_ref[...] = (_rope(proj[:, :ATTN_W], cos, sin) * ATTN_SCALE).astype(BF16)
        kv_ref[:, :KV_W] = _rope(proj[:, ATTN_W:ATTN_W + KV_W], cos, sin).astype(BF16)
        kv_ref[:, KV_W:] = proj[:, ATTN_W + KV_W:ATTN_W + 2 * KV_W].astype(BF16)
        xg_ref[...] = proj[:, ATTN_W + 2 * KV_W:]

    tile = _proj_tile(rows)
    rs = functools.partial(_row_spec, tile=tile)
    return pl.pallas_call(
        body, name="in_proj_fwd", grid=(rows // tile,),
        in_specs=[rs(D_MODEL), _whole_spec(gain), _whole_spec(w_in), rs(LANES), rs(LANES)],
        out_specs=[rs(D_MODEL), rs(ATTN_W), rs(2 * KV_W), rs(2 * LRU_W)],
        out_shape=[_sds((rows, D_MODEL), BF16), _sds((rows, ATTN_W), BF16), _sds((rows, 2 * KV_W), BF16),
                   _sds((rows, 2 * LRU_W), F32)],
        compiler_params=_params("parallel"),
    )(h, gain, w_in, rope_cos, rope_sin)


def _in_proj_bwd(dh_mid, h, dq, dkv, dxg, rope_cos, rope_sin, gain, w_in):
    rows = h.shape[0]

    def body(dhm_ref, h_ref, dq_ref, dkv_ref, dxg_ref, c_ref, s_ref, g_ref, w_ref, dh_ref, dp_ref, gacc_ref):
        @pl.when(pl.program_id(0) == 0)
        def _():
            gacc_ref[...] = jnp.zeros_like(gacc_ref)

        cos = c_ref[...]
        sin = -s_ref[...]
        dp_ref[:, :ATTN_W] = (_rope(dq_ref[...], cos, sin) * ATTN_SCALE).astype(BF16)
        dp_ref[:, ATTN_W:ATTN_W + KV_W] = _rope(dkv_ref[:, :KV_W], cos, sin).astype(BF16)
        dp_ref[:, ATTN_W + KV_W:ATTN_W + 2 * KV_W] = dkv_ref[:, KV_W:].astype(BF16)
        dp_ref[:, ATTN_W + 2 * KV_W:] = dxg_ref[...]
        dz = _dot(dp_ref[...], w_ref[...])
        dx, dg = _rms_bwd(dz, h_ref[...], g_ref[...])
        dh_ref[...] = dhm_ref[...] + dx
        gacc_ref[...] += _put_rows(SUBLANES, [dg])

    tile = _proj_tile(rows)
    rs = functools.partial(_row_spec, tile=tile)
    return pl.pallas_call(
        body, name="in_proj_bwd", grid=(rows // tile,),
        in_specs=[rs(D_MODEL), rs(D_MODEL), rs(ATTN_W), rs(2 * KV_W), rs(2 * LRU_W),
                  rs(LANES), rs(LANES), _whole_spec(gain), _whole_spec(w_in)],
        out_specs=[rs(D_MODEL), rs(IN_COLS), pl.BlockSpec((SUBLANES, D_MODEL), lambda i: (0, 0))],
        out_shape=[_sds((rows, D_MODEL), F32), _sds((rows, IN_COLS), BF16), _sds((SUBLANES, D_MODEL), F32)],
        compiler_params=_params("arbitrary"),
    )(dh_mid, h, dq, dkv, dxg, rope_cos, rope_sin, gain, w_in)


def _kv_lane_variants(t, group):
    lane = lax.broadcasted_iota(jnp.int32, t.shape, 1)
    low = lane < 64
    swapped = pltpu.roll(t, 64, 1)
    if group == 0:
        lo, hi = jnp.where(low, t, 0.0), jnp.where(low, 0.0, swapped)
    else:
        lo, hi = jnp.where(low, swapped, 0.0), jnp.where(low, 0.0, t)
    return jnp.concatenate([lo, hi], axis=0).astype(BF16)


GROUP_ROWS = 2 * TIME_BLOCK
KEYS = 2 * TIME_BLOCK


def _window_mask(j):
    r = jnp.bitwise_and(lax.broadcasted_iota(jnp.int32, (GROUP_ROWS, KEYS), 0), TIME_BLOCK - 1)
    c = lax.broadcasted_iota(jnp.int32, (GROUP_ROWS, KEYS), 1)
    return (c > r) & (c <= r + TIME_BLOCK) & ((c >= TIME_BLOCK) | (j > 0))


def _group_rows(ref, ex, group):
    lo = 2 * group * LANES
    return jnp.concatenate([ref[ex, :, lo:lo + LANES], ref[ex, :, lo + LANES:lo + 2 * LANES]], axis=0)


def _per_head(sink_ref, group, half):
    upper = lax.broadcasted_iota(jnp.int32, (GROUP_ROWS, 1), 0) < TIME_BLOCK
    return jnp.where(upper, sink_ref[4 * group + half], sink_ref[4 * group + 2 + half])


def _by_example(a, batch):
    return a.reshape(batch, a.shape[0] // batch, a.shape[1])


def _attn_fwd(q, kv, sinks, batch, n_blocks):
    rows = q.shape[0]

    def body(sink_ref, q_ref, kvc_ref, kvp_ref, o_ref, lse_ref):
        j = pl.program_id(0)
        mask = _window_mask(j)
        chains = [(ex, group) for ex in range(batch) for group in range(2)]
        v_cat, s_all, probs = {}, {}, {ch: [] for ch in chains}
        for ex, group in chains:
            kv2 = jnp.concatenate([kvp_ref[ex], kvc_ref[ex]], axis=0).astype(F32)
            k_cat = _kv_lane_variants(kv2[:, :KV_W], group)
            v_cat[ex, group] = _kv_lane_variants(kv2[:, KV_W:], group)
            s_all[ex, group] = _dot_nt(_group_rows(q_ref, ex, group), k_cat)
        for half in range(2):
            for ex, group in chains:
                sink = _per_head(sink_ref, group, half)
                s = jnp.where(mask, s_all[ex, group][:, half * KEYS:(half + 1) * KEYS], MASKED)
                m = jnp.maximum(jnp.max(s, axis=1, keepdims=True), sink)
                e = jnp.exp(s - m)
                den = jnp.sum(e, axis=1, keepdims=True) + jnp.exp(sink - m)
                probs[ex, group].append((e / den).astype(BF16))
                lse = m + jnp.log(den)
                upper_head, lower_head = 4 * group + half, 4 * group + 2 + half
                lse_ref[ex, :, upper_head:upper_head + 1] = lse[:TIME_BLOCK]
                lse_ref[ex, :, lower_head:lower_head + 1] = lse[TIME_BLOCK:]
        for ex, group in chains:
            out = _dot(jnp.concatenate(probs[ex, group], axis=1), v_cat[ex, group])
            o_ref[ex, :, 2 * group * LANES:(2 * group + 1) * LANES] = out[:TIME_BLOCK]
            o_ref[ex, :, (2 * group + 1) * LANES:(2 * group + 2) * LANES] = out[TIME_BLOCK:]

    def blk(width):
        return pl.BlockSpec((batch, TIME_BLOCK, width), lambda j: (0, j, 0))

    prev = pl.BlockSpec((batch, TIME_BLOCK, 2 * KV_W), lambda j: (0, jnp.maximum(j - 1, 0), 0))
    kv3 = _by_example(kv, batch)
    out, lse = pl.pallas_call(
        body, name="attn_fwd", grid=(n_blocks,),
        in_specs=[pl.BlockSpec(memory_space=pltpu.SMEM), blk(ATTN_W), blk(2 * KV_W), prev],
        out_specs=[blk(ATTN_W), blk(N_HEADS)],
        out_shape=[_sds((batch, rows // batch, ATTN_W), F32), _sds((batch, rows // batch, N_HEADS), F32)],
        compiler_params=_params("parallel"),
    )(sinks, _by_example(q, batch), kv3, kv3)
    return out.reshape(rows, ATTN_W), lse.reshape(rows, N_HEADS)


def _attn_bwd(q, kv, sinks, out, lse, d_out, batch, n_blocks):
    rows = q.shape[0]

    def body(sink_ref, q_ref, kvc_ref, kvp_ref, o_ref, do_ref, lse_ref, dq_ref, dkv_ref, dsink_ref, carry):
        j = pl.program_id(0)

        @pl.when(j == 0)
        def _():
            dsink_ref[...] = jnp.zeros_like(dsink_ref)

        def all_chains(dsink_vals):
            mask = _window_mask(j)
            lane = lax.broadcasted_iota(jnp.int32, (GROUP_ROWS, LANES), 1)
            low = lax.broadcasted_iota(jnp.int32, (KEYS, LANES), 1) < 64
            upper = lax.broadcasted_iota(jnp.int32, (GROUP_ROWS, 1), 0) < TIME_BLOCK
            chains = [(ex, group) for ex in range(batch) for group in range(2)]
            k_cat, v_cat, q_rows, do_b, od, s_all, dp_all = {}, {}, {}, {}, {}, {}, {}
            for ch in chains:
                ex, group = ch
                kv2 = jnp.concatenate([kvp_ref[ex], kvc_ref[ex]], axis=0).astype(F32)
                k_cat[ch] = _kv_lane_variants(kv2[:, :KV_W], group)
                v_cat[ch] = _kv_lane_variants(kv2[:, KV_W:], group)
                q_rows[ch] = _group_rows(q_ref, ex, group)
                do_rows = _group_rows(do_ref, ex, group)
                do_b[ch] = do_rows.astype(BF16)
                od[ch] = do_rows * _group_rows(o_ref, ex, group)
                s_all[ch] = _dot_nt(q_rows[ch], k_cat[ch])
                dp_all[ch] = _dot_nt(do_b[ch], v_cat[ch])
            probs, dss = {ch: [] for ch in chains}, {ch: [] for ch in chains}
            for half in range(2):
                for ch in chains:
                    ex, group = ch
                    heads = (4 * group + half, 4 * group + 2 + half)
                    sink = _per_head(sink_ref, group, half)
                    lse_h = jnp.concatenate([lse_ref[ex, :, h:h + 1] for h in heads], axis=0)
                    in_half = (lane < 64) if half == 0 else (lane >= 64)
                    delta = jnp.sum(jnp.where(in_half, od[ch], 0.0), axis=1, keepdims=True)
                    cols = slice(half * KEYS, (half + 1) * KEYS)
                    prob = jnp.exp(jnp.where(mask, s_all[ch][:, cols], MASKED) - lse_h)
                    probs[ch].append(prob.astype(BF16))
                    dss[ch].append((prob * (dp_all[ch][:, cols] - delta)).astype(BF16))
                    dsink = -jnp.exp(sink - lse_h) * delta
                    dsink_vals[heads[0]] = dsink_vals[heads[0]] + jnp.sum(jnp.where(upper, dsink, 0.0), axis=0, keepdims=True)
                    dsink_vals[heads[1]] = dsink_vals[heads[1]] + jnp.sum(jnp.where(upper, 0.0, dsink), axis=0, keepdims=True)
            dk_tile = {ex: jnp.zeros((KEYS, KV_W), F32) for ex in range(batch)}
            dv_tile = {ex: jnp.zeros((KEYS, KV_W), F32) for ex in range(batch)}
            for ch in chains:
                ex, group = ch
                ds = jnp.concatenate(dss[ch], axis=1)
                dq_rows = _dot(ds, k_cat[ch])
                dq_ref[ex, :, 2 * group * LANES:(2 * group + 1) * LANES] = dq_rows[:TIME_BLOCK]
                dq_ref[ex, :, (2 * group + 1) * LANES:(2 * group + 2) * LANES] = dq_rows[TIME_BLOCK:]
                dk_cat = _dot_tn(ds, q_rows[ch])
                dv_cat = _dot_tn(jnp.concatenate(probs[ch], axis=1), do_b[ch])
                if group == 0:
                    dk_tile[ex] = dk_tile[ex] + jnp.where(low, dk_cat[:KEYS] + pltpu.roll(dk_cat[KEYS:], 64, 1), 0.0)
                    dv_tile[ex] = dv_tile[ex] + jnp.where(low, dv_cat[:KEYS] + pltpu.roll(dv_cat[KEYS:], 64, 1), 0.0)
                else:
                    dk_tile[ex] = dk_tile[ex] + jnp.where(low, 0.0, pltpu.roll(dk_cat[:KEYS], 64, 1) + dk_cat[KEYS:])
                    dv_tile[ex] = dv_tile[ex] + jnp.where(low, 0.0, pltpu.roll(dv_cat[:KEYS], 64, 1) + dv_cat[KEYS:])

            @pl.when(j > 0)
            def _():
                for ex in range(batch):
                    dkv_ref[ex, :, :KV_W] = carry[ex, :, :KV_W] + dk_tile[ex][:TIME_BLOCK]
                    dkv_ref[ex, :, KV_W:] = carry[ex, :, KV_W:] + dv_tile[ex][:TIME_BLOCK]

            for ex in range(batch):
                carry[ex, :, :KV_W] = dk_tile[ex][TIME_BLOCK:]
                carry[ex, :, KV_W:] = dv_tile[ex][TIME_BLOCK:]

        @pl.when(j < n_blocks)
        def _():
            dsink_vals = {head: jnp.zeros((1, 1), F32) for head in range(N_HEADS)}
            all_chains(dsink_vals)
            rowid = lax.broadcasted_iota(jnp.int32, (N_HEADS, LANES), 0)
            upd = jnp.zeros((N_HEADS, LANES), F32)
            for head, val in dsink_vals.items():
                upd = upd + jnp.where(rowid == head, val, 0.0)
            dsink_ref[...] += upd

        @pl.when(j == n_blocks)
        def _():
            dkv_ref[...] = carry[...]

    last = n_blocks - 1

    def blk(width):
        return pl.BlockSpec((batch, TIME_BLOCK, width), lambda j: (0, jnp.minimum(j, last), 0))

    prev = pl.BlockSpec((batch, TIME_BLOCK, 2 * KV_W), lambda j: (0, jnp.maximum(jnp.minimum(j, last) - 1, 0), 0))
    dkv_spec = pl.BlockSpec((batch, TIME_BLOCK, 2 * KV_W), lambda j: (0, jnp.maximum(j - 1, 0), 0))
    kv3 = _by_example(kv, batch)
    dq, dkv, dsink = pl.pallas_call(
        body, name="attn_bwd", grid=(n_blocks + 1,),
        in_specs=[pl.BlockSpec(memory_space=pltpu.SMEM), blk(ATTN_W), blk(2 * KV_W), prev, blk(ATTN_W), blk(ATTN_W),
                  blk(N_HEADS)],
        out_specs=[blk(ATTN_W), dkv_spec, pl.BlockSpec((N_HEADS, LANES), lambda j: (0, 0))],
        out_shape=[_sds((batch, rows // batch, ATTN_W), F32), _sds((batch, rows // batch, 2 * KV_W), F32),
                   _sds((N_HEADS, LANES), F32)],
        scratch_shapes=[pltpu.VMEM((batch, TIME_BLOCK, 2 * KV_W), F32)],
        compiler_params=_params("arbitrary"),
    )(sinks, _by_example(q, batch), kv3, kv3, _by_example(out, batch), _by_example(d_out, batch), _by_example(lse, batch))
    return dq.reshape(rows, ATTN_W), dkv.reshape(rows, 2 * KV_W), dsink


def _conv_taps(xb, prev8):
    ext = jnp.concatenate([prev8, xb], axis=0)
    n = ext.shape[0]
    return [xb] + [pltpu.roll(ext, k, 0)[SUBLANES:n] for k in range(1, CONV_TAPS)]


def _lru_gates(xc, wa, ba, wx, bx, lam):
    xcb = xc.astype(BF16)
    r = _sigmoid(_dot(xcb, wa) + ba)
    i = _sigmoid(_dot(xcb, wx) + bx)
    sp = jnp.maximum(-lam, 0.0) + _log1p(jnp.exp(-jnp.abs(lam)))
    log_a = -LRU_C * r * sp
    a = jnp.exp(log_a)
    mult = jnp.sqrt(_one_minus_square(a, log_a))
    return xcb, r, i, sp, a, mult


def _scan_fwd(a, u, h_before):
    n, d = a.shape
    groups = n // SUBLANES
    per_seq = groups // len(h_before)
    a = a.reshape(groups, SUBLANES, d)
    u = u.reshape(groups, SUBLANES, d)
    sub = lax.broadcasted_iota(jnp.int32, a.shape, 1)
    s = 1
    while s < SUBLANES:
        valid = sub >= s
        u = jnp.where(valid, u + a * pltpu.roll(u, s, 1), u)
        a = jnp.where(valid, a * pltpu.roll(a, s, 1), a)
        s *= 2
    out, prev = [None] * groups, list(h_before)
    for g in range(per_seq):
        for k in range(len(h_before)):
            at = k * per_seq + g
            out[at] = u[at] + a[at] * prev[k]
            prev[k] = out[at][SUBLANES - 1:SUBLANES, :]
    return jnp.concatenate(out, axis=0)


def _scan_rev(cf, g, d_after):
    n, d = g.shape
    groups = n // SUBLANES
    per_seq = groups // len(d_after)
    cf = cf.reshape(groups, SUBLANES, d)
    g = g.reshape(groups, SUBLANES, d)
    sub = lax.broadcasted_iota(jnp.int32, g.shape, 1)
    s = 1
    while s < SUBLANES:
        valid = sub + s < SUBLANES
        g = jnp.where(valid, g + cf * pltpu.roll(g, SUBLANES - s, 1), g)
        cf = jnp.where(valid, cf * pltpu.roll(cf, SUBLANES - s, 1), cf)
        s *= 2
    out, nxt = [None] * groups, list(d_after)
    for i in reversed(range(per_seq)):
        for k in range(len(d_after)):
            at = k * per_seq + i
            out[at] = g[at] + cf[at] * nxt[k]
            nxt[k] = out[at][0:1, :]
    return jnp.concatenate(out, axis=0)


def _lru_fwd(xg, conv_w, conv_b, wa, ba, wx, bx, lam, batch, n_blocks):
    rows = xg.shape[0]

    def body(xg_ref, cw_ref, cb_ref, wa_ref, ba_ref, wx_ref, bx_ref, lam_ref, hs_ref, lru_ref, x_prev, h_carry):
        @pl.when(pl.program_id(0) == 0)
        def _():
            x_prev[...] = jnp.zeros_like(x_prev)
            h_carry[...] = jnp.zeros_like(h_carry)

        xb = [xg_ref[ex, :, :LRU_W] for ex in range(batch)]
        taps = [_conv_taps(xb[ex], x_prev[ex]) for ex in range(batch)]
        taps = [jnp.concatenate([t[k] for t in taps], axis=0) for k in range(CONV_TAPS)]
        xc = cb_ref[...] + sum(cw_ref[CONV_TAPS - 1 - k:CONV_TAPS - k, :] * taps[k] for k in range(CONV_TAPS))
        _, _, i, _, a, mult = _lru_gates(xc, wa_ref[...], ba_ref[...], wx_ref[...], bx_ref[...], lam_ref[...])
        h = _scan_fwd(a, mult * i * xc, [h_carry[ex] for ex in range(batch)])
        gate = _gelu(jnp.concatenate([xg_ref[ex, :, LRU_W:] for ex in range(batch)], axis=0))
        lru = h * gate
        for ex in range(batch):
            rows_ex = slice(ex * TIME_BLOCK, (ex + 1) * TIME_BLOCK)
            hs_ref[ex] = h[rows_ex]
            lru_ref[ex] = lru[rows_ex]
            h_carry[ex] = h[(ex + 1) * TIME_BLOCK - 1:(ex + 1) * TIME_BLOCK, :]
            x_prev[ex] = xb[ex][TIME_BLOCK - SUBLANES:TIME_BLOCK, :]

    def blk(width):
        return pl.BlockSpec((batch, TIME_BLOCK, width), lambda j: (0, j, 0))

    small = [conv_w, conv_b, wa, ba, wx, bx, lam]
    hs, lru = pl.pallas_call(
        body, name="lru_fwd", grid=(n_blocks,),
        in_specs=[blk(2 * LRU_W)] + [_whole_spec(a) for a in small],
        out_specs=[blk(LRU_W), blk(LRU_W)],
        out_shape=[_sds((batch, rows // batch, LRU_W), F32)] * 2,
        scratch_shapes=[pltpu.VMEM((batch, SUBLANES, LRU_W), F32), pltpu.VMEM((batch, 1, LRU_W), F32)],
        compiler_params=_params("arbitrary"),
    )(_by_example(xg, batch), *small)
    return hs.reshape(rows, LRU_W), lru.reshape(rows, LRU_W)


def _lru_bwd(xg, hs, d_lru, conv_w, conv_b, wa, ba, wx, bx, lam, batch, n_blocks):
    rows = xg.shape[0]
    last_row = TIME_BLOCK - 1

    def body(xg_ref, xgh_ref, hs_ref, hsh_ref, dl_ref, cw_ref, cb_ref, wa_ref, ba_ref, wx_ref, bx_ref, lam_ref,
             dxg_ref, dwa_ref, dwx_ref, vec_ref, dh_carry, dxc_next):
        j = pl.program_id(0)

        @pl.when(j == 0)
        def _():
            dwa_ref[...] = jnp.zeros_like(dwa_ref)
            dwx_ref[...] = jnp.zeros_like(dwx_ref)
            vec_ref[...] = jnp.zeros_like(vec_ref)
            dh_carry[...] = jnp.zeros_like(dh_carry)
            dxc_next[...] = jnp.zeros_like(dxc_next)

        examples = range(batch)
        first = j == n_blocks - 1
        xb = [xg_ref[ex, :, :LRU_W] for ex in examples]
        taps = [_conv_taps(xb[ex], jnp.where(first, 0.0, xgh_ref[ex, :, :LRU_W])) for ex in examples]
        taps = [jnp.concatenate([t[k] for t in taps], axis=0) for k in range(CONV_TAPS)]
        cw = cw_ref[...]
        lam = lam_ref[...]
        wa = wa_ref[...]
        wx = wx_ref[...]
        xc = cb_ref[...] + sum(cw[CONV_TAPS - 1 - k:CONV_TAPS - k, :] * taps[k] for k in range(CONV_TAPS))
        xcb, r, i, sp, a, mult = _lru_gates(xc, wa, ba_ref[...], wx, bx_ref[...], lam)
        hs = jnp.concatenate([hs_ref[ex] for ex in examples], axis=0)
        row = jnp.bitwise_and(lax.broadcasted_iota(jnp.int32, hs.shape, 0), TIME_BLOCK - 1)
        h_before = jnp.concatenate([jnp.broadcast_to(jnp.where(first, 0.0, hsh_ref[ex, SUBLANES - 1:SUBLANES, :]),
                                                     (TIME_BLOCK, LRU_W)) for ex in examples], axis=0)
        h_prev = jnp.where(row == 0, h_before, pltpu.roll(hs, 1, 0))
        dl = jnp.concatenate([dl_ref[ex] for ex in examples], axis=0)
        gate, dgate = _gelu_and_grad(jnp.concatenate([xg_ref[ex, :, LRU_W:] for ex in examples], axis=0))
        dgb = (dl * hs * dgate).astype(BF16)
        cf = jnp.where(row == last_row, 1.0, pltpu.roll(a, batch * TIME_BLOCK - 1, 0))
        dh = _scan_rev(cf, dl * gate, [dh_carry[ex] for ex in examples])
        dmult = dh * i * xc
        di = dh * mult * xc
        dxc = dh * mult * i
        dlog_a = dh * h_prev * a - dmult * (a * a / mult)
        dr = dlog_a * (-LRU_C * sp)
        dlam = jnp.sum(dlog_a * (-LRU_C * r), axis=0, keepdims=True) * (-_sigmoid(-lam))
        dpr = dr * r * (1.0 - r)
        dpi = di * i * (1.0 - i)
        dprb = dpr.astype(BF16)
        dpib = dpi.astype(BF16)
        dxc = dxc + _dot_nt(dprb, wa) + _dot_nt(dpib, wx)
        dwa_ref[...] += _dot_tn(xcb, dprb)
        dwx_ref[...] += _dot_tn(xcb, dpib)
        n = TIME_BLOCK + SUBLANES
        later = []
        for k in range(1, CONV_TAPS):
            shifted = [pltpu.roll(jnp.concatenate([dxc[ex * TIME_BLOCK:(ex + 1) * TIME_BLOCK], dxc_next[ex]], axis=0),
                                  n - k, 0)[:TIME_BLOCK] for ex in examples]
            later.append(jnp.concatenate(shifted, axis=0))
        dxb = cw[CONV_TAPS - 1:CONV_TAPS, :] * dxc
        for k in range(1, CONV_TAPS):
            dxb = dxb + cw[CONV_TAPS - 1 - k:CONV_TAPS - k, :] * later[k - 1]
        dxb = dxb.astype(BF16)
        for ex in examples:
            rows_ex = slice(ex * TIME_BLOCK, (ex + 1) * TIME_BLOCK)
            dxg_ref[ex, :, :LRU_W] = dxb[rows_ex]
            dxg_ref[ex, :, LRU_W:] = dgb[rows_ex]
            dh_carry[ex] = a[ex * TIME_BLOCK:ex * TIME_BLOCK + 1, :] * dh[ex * TIME_BLOCK:ex * TIME_BLOCK + 1, :]
            dxc_next[ex] = dxc[ex * TIME_BLOCK:ex * TIME_BLOCK + SUBLANES, :]
        vecs = [jnp.sum(dxc * taps[CONV_TAPS - 1 - t], axis=0, keepdims=True) for t in range(CONV_TAPS)]
        vecs += [jnp.sum(dxc, axis=0, keepdims=True), jnp.sum(dpr, axis=0, keepdims=True),
                 jnp.sum(dpi, axis=0, keepdims=True), dlam]
        vec_ref[...] += _put_rows(SUBLANES, vecs)

    def tblk(j):
        return n_blocks - 1 - j

    def blk(width):
        return pl.BlockSpec((batch, TIME_BLOCK, width), lambda j: (0, tblk(j), 0))

    per8 = TIME_BLOCK // SUBLANES

    def halo(width):
        return pl.BlockSpec((batch, SUBLANES, width), lambda j: (0, jnp.maximum(per8 * tblk(j) - 1, 0), 0))

    small = [conv_w, conv_b, wa, ba, wx, bx, lam]
    acc = lambda shape: pl.BlockSpec(shape, lambda j: (0, 0))
    xg3, hs3 = _by_example(xg, batch), _by_example(hs, batch)
    dxg, dwa, dwx, vec = pl.pallas_call(
        body, name="lru_bwd", grid=(n_blocks,),
        in_specs=[blk(2 * LRU_W), halo(2 * LRU_W), blk(LRU_W), halo(LRU_W), blk(LRU_W)] + [_whole_spec(a) for a in small],
        out_specs=[blk(2 * LRU_W), acc((LRU_W, LRU_W)), acc((LRU_W, LRU_W)), acc((SUBLANES, LRU_W))],
        out_shape=[_sds((batch, rows // batch, 2 * LRU_W), BF16), _sds((LRU_W, LRU_W), F32), _sds((LRU_W, LRU_W), F32),
                   _sds((SUBLANES, LRU_W), F32)],
        scratch_shapes=[pltpu.VMEM((batch, 1, LRU_W), F32), pltpu.VMEM((batch, SUBLANES, LRU_W), F32)],
        compiler_params=_params("arbitrary"),
    )(xg3, xg3, hs3, hs3, _by_example(d_lru, batch), *small)
    return dxg.reshape(rows, 2 * LRU_W), dwa, dwx, vec


def _out_mlp_fwd(attn, lru, h, w_out, w_up4, w_down, g_attn, g_lru, g_post_mix, g_pre_mlp, g_post_mlp):
    rows = h.shape[0]

    def body(at_ref, lr_ref, h_ref, wo_ref, wu_ref, wd_ref, ga_ref, gl_ref, gp_ref, g1_ref, g2_ref,
             grp_ref, o_ref, hm_ref, z_ref, slope_ref, act_ref, y_ref, ho_ref):
        a = _rms(at_ref[...], ga_ref[...]).astype(BF16)
        l = _rms(lr_ref[...], gl_ref[...]).astype(BF16)
        grp_ref[:, :ATTN_W] = a
        grp_ref[:, ATTN_W:] = l
        o = _dot(a, wo_ref[:ATTN_W, :]) + _dot(l, wo_ref[ATTN_W:, :])
        o_ref[...] = o
        x = h_ref[...] + _rms(o, gp_ref[...])
        hm_ref[...] = x
        z = _rms(x, g1_ref[...]).astype(BF16)
        z_ref[...] = z
        y = jnp.zeros((ROW_TILE, D_MODEL), F32)
        for s in range(N_CHIPS):
            cols = slice(s * D_MODEL, (s + 1) * D_MODEL)
            r = jnp.maximum(_dot(z, wu_ref[s]), 0.0)
            slope_ref[:, cols] = (2.0 * r).astype(BF16)
            act = jnp.square(r).astype(BF16)
            act_ref[:, cols] = act
            y = y + _dot(act, wd_ref[cols, :])
        y_ref[...] = y
        ho_ref[...] = x + _rms(y, g2_ref[...])

    gains = [g_attn, g_lru, g_post_mix, g_pre_mlp, g_post_mlp]
    return pl.pallas_call(
        body, name="out_mlp_fwd", grid=(rows // ROW_TILE,),
        in_specs=[_row_spec(ATTN_W), _row_spec(LRU_W), _row_spec(D_MODEL), _resident_spec(w_out), _resident_spec(w_up4),
                  _resident_spec(w_down)] + [_whole_spec(g) for g in gains],
        out_specs=[_row_spec(D_MODEL), _row_spec(D_MODEL), _row_spec(D_MODEL), _row_spec(D_MODEL), _row_spec(D_FF),
                   _row_spec(D_FF), _row_spec(D_MODEL), _row_spec(D_MODEL)],
        out_shape=[_sds((rows, D_MODEL), BF16), _sds((rows, D_MODEL), F32), _sds((rows, D_MODEL), F32),
                   _sds((rows, D_MODEL), BF16), _sds((rows, D_FF), BF16), _sds((rows, D_FF), BF16),
                   _sds((rows, D_MODEL), F32), _sds((rows, D_MODEL), F32)],
        compiler_params=_params("parallel"),
    )(attn, lru, h, w_out, w_up4, w_down, *gains)


def _mlp_out_bwd(dh_out, h_mid, y, slope, o, attn, lru, w_out, w_up4, w_down, g_attn, g_lru, g_post_mix, g_pre_mlp,
                 g_post_mlp):
    rows = y.shape[0]

    def body(dh_ref, hm_ref, y_ref, slope_ref, o_ref, at_ref, lr_ref, wo_ref, wu_ref, wd_ref, ga_ref, gl_ref, gp_ref,
             g1_ref, g2_ref, dhm_ref, dy_ref, dup_ref, do_ref, dat_ref, dlr_ref, gacc_ref):
        @pl.when(pl.program_id(0) == 0)
        def _():
            gacc_ref[...] = jnp.zeros_like(gacc_ref)

        dh = dh_ref[...]
        dy, dg2 = _rms_bwd(dh, y_ref[...], g2_ref[...])
        dyb = dy.astype(BF16)
        dy_ref[...] = dyb
        dz = jnp.zeros((ROW_TILE, D_MODEL), F32)
        for s in range(N_CHIPS):
            cols = slice(s * D_MODEL, (s + 1) * D_MODEL)
            dact = _dot_nt(dyb, wd_ref[cols, :])
            dup = (dact * slope_ref[:, cols].astype(F32)).astype(BF16)
            dup_ref[:, cols] = dup
            dz = dz + _dot_nt(dup, wu_ref[s])
        dx, dg1 = _rms_bwd(dz, hm_ref[...], g1_ref[...])
        dhm = dh + dx
        dhm_ref[...] = dhm
        do, dgp = _rms_bwd(dhm, o_ref[...], gp_ref[...])
        dob = do.astype(BF16)
        do_ref[...] = dob
        dat, dga = _rms_bwd(_dot_nt(dob, wo_ref[:ATTN_W, :]), at_ref[...], ga_ref[...])
        dlr, dgl = _rms_bwd(_dot_nt(dob, wo_ref[ATTN_W:, :]), lr_ref[...], gl_ref[...])
        dat_ref[...] = dat
        dlr_ref[...] = dlr
        gacc_ref[...] += _put_rows(SUBLANES, [dg1, dg2, dgp, jnp.concatenate([dga, dgl], axis=1)])

    gains = [g_attn, g_lru, g_post_mix, g_pre_mlp, g_post_mlp]
    return pl.pallas_call(
        body, name="mlp_out_bwd", grid=(rows // ROW_TILE,),
        in_specs=[_row_spec(D_MODEL), _row_spec(D_MODEL), _row_spec(D_MODEL), _row_spec(D_FF), _row_spec(D_MODEL),
                  _row_spec(ATTN_W), _row_spec(LRU_W), _resident_spec(w_out), _resident_spec(w_up4), _resident_spec(w_down)]
                 + [_whole_spec(g) for g in gains],
        out_specs=[_row_spec(D_MODEL), _row_spec(D_MODEL), _row_spec(D_FF), _row_spec(D_MODEL), _row_spec(ATTN_W),
                   _row_spec(LRU_W), pl.BlockSpec((SUBLANES, D_MODEL), lambda i: (0, 0))],
        out_shape=[_sds((rows, D_MODEL), F32), _sds((rows, D_MODEL), BF16), _sds((rows, D_FF), BF16),
                   _sds((rows, D_MODEL), BF16), _sds((rows, ATTN_W), F32), _sds((rows, LRU_W), F32),
                   _sds((SUBLANES, D_MODEL), F32)],
        compiler_params=_params("arbitrary"),
    )(dh_out, h_mid, y, slope, o, attn, lru, w_out, w_up4, w_down, *gains)


def _matmul_tn(a, b, tm, tn, name, out_dtype, column_blocks=False):
    rows, m = a.shape
    n = b.shape[1]

    def body(a_ref, b_ref, o_ref):
        o_ref[...] = _dot_tn(a_ref[...], b_ref[...]).astype(out_dtype)

    if column_blocks:
        out_spec = pl.BlockSpec((None, tm, tn), lambda i, j: (j, i, 0))
        out_shape = _sds((n // tn, m, tn), out_dtype)
    else:
        out_spec = pl.BlockSpec((tm, tn), lambda i, j: (i, j))
        out_shape = _sds((m, n), out_dtype)
    return pl.pallas_call(
        body, name=name, grid=(m // tm, n // tn),
        in_specs=[pl.BlockSpec((rows, tm), lambda i, j: (0, i)), pl.BlockSpec((rows, tn), lambda i, j: (0, j))],
        out_specs=out_spec, out_shape=out_shape,
        compiler_params=_params("parallel", "parallel"),
    )(a, b)


LOSS_COLS = 256


def _loss_head(h, loss_target):
    batch, per_example, _ = h.shape
    seq = loss_target.shape[1]
    n_real = N_META + seq
    assert seq % SUBLANES == 0

    def body(h_ref, t_ref, dh_ref, l_ref):
        @pl.when((pl.program_id(0) == 0) & (pl.program_id(1) == 0))
        def _():
            l_ref[...] = jnp.zeros_like(l_ref)

        d = h_ref[N_META:n_real, :] - t_ref[...]
        dh_ref[:N_META, :] = jnp.zeros((N_META, LOSS_COLS), F32)
        dh_ref[N_META:n_real, :] = d * (1.0 / D_MODEL)
        dh_ref[n_real:, :] = jnp.zeros((per_example - n_real, LOSS_COLS), F32)
        l_ref[...] += jnp.sum(jnp.sum(d * d, axis=0, keepdims=True), axis=1, keepdims=True)

    blk = pl.BlockSpec((None, per_example, LOSS_COLS), lambda b, j: (b, 0, j))
    return pl.pallas_call(
        body, name="loss_head", grid=(batch, D_MODEL // LOSS_COLS),
        in_specs=[blk, pl.BlockSpec((None, seq, LOSS_COLS), lambda b, j: (b, 0, j))],
        out_specs=[blk, pl.BlockSpec((SUBLANES, LANES), lambda b, j: (0, 0))],
        out_shape=[_sds(h.shape, F32), _sds((SUBLANES, LANES), F32)],
        compiler_params=_params("arbitrary", "arbitrary"),
    )(h, loss_target)


def _meta_grad(dh0, batch, rows_per_example):
    per = rows_per_example // N_META

    def body(d_ref, o_ref):
        @pl.when(pl.program_id(0) == 0)
        def _():
            o_ref[...] = jnp.zeros_like(o_ref)

        o_ref[...] += d_ref[...]

    return pl.pallas_call(
        body, name="meta_grad", grid=(batch,),
        in_specs=[pl.BlockSpec((N_META, D_MODEL), lambda b: (b * per, 0))],
        out_specs=pl.BlockSpec((N_META, D_MODEL), lambda b: (0, 0)),
        out_shape=_sds((N_META, D_MODEL), F32),
        compiler_params=_params("arbitrary"),
    )(dh0)


def _elementwise_tile(rows, cols):
    tile = rows
    while tile * cols * 4 > (1 << 20) and tile % 16 == 0:
        tile //= 2
    return tile


def _sum_slots(buf, name):
    k, rows, cols = buf.shape
    tile = _elementwise_tile(rows, cols)

    def body(*refs):
        total = refs[0][...].astype(F32)
        for r in refs[1:k]:
            total = total + r[...].astype(F32)
        refs[k][...] = total

    def slot(s):
        return pl.BlockSpec((None, tile, cols), lambda i: (s, i, 0))

    return pl.pallas_call(
        body, name=name, grid=(rows // tile,),
        in_specs=[slot(s) for s in range(k)], out_specs=pl.BlockSpec((tile, cols), lambda i: (i, 0)),
        out_shape=_sds((rows, cols), F32), compiler_params=_params("parallel"),
    )(*([buf] * k))


def _sum_pieces(pieces, landed, chip, layer, n_layers, stacked):
    _, rows, cols = pieces.shape
    tile = _elementwise_tile(rows, cols)

    def body(chip_ref, own_ref, a_ref, b_ref, c_ref, *rest):
        o_ref, token = rest[-2:]
        total = ((own_ref[...].astype(F32) + a_ref[...].astype(F32)) + b_ref[...].astype(F32)) + c_ref[...].astype(F32)
        o_ref[...] = total.astype(BF16)
        token[...] = jnp.zeros_like(token)

    def slot(offset):
        return pl.BlockSpec((None, tile, cols), lambda i, chip_ref: ((chip_ref[0] + offset) % N_CHIPS, i, 0))

    carried = [] if stacked is None else [stacked]
    grid_spec = pltpu.PrefetchScalarGridSpec(
        num_scalar_prefetch=1, grid=(rows // tile,),
        in_specs=[slot(0), slot(1), slot(2), slot(3)] + [pl.BlockSpec(memory_space=pl.ANY)] * len(carried),
        out_specs=[pl.BlockSpec((None, tile, cols), lambda i, chip_ref: (layer, i, 0)),
                   pl.BlockSpec((SUBLANES, LANES), lambda i, chip_ref: (0, 0))])
    return pl.pallas_call(
        body, name="sum_grad_pieces", grid_spec=grid_spec,
        out_shape=[_sds((n_layers, rows, cols), BF16), _sds((SUBLANES, LANES), F32)],
        input_output_aliases={5: 0} if carried else {},
        compiler_params=_params("arbitrary"),
    )(chip, pieces, landed, landed, landed, *carried)


def _adamw(w, m, v, grads, name):
    flat = w.ndim == 2
    if flat:
        w, m, v = w[None], m[None], v[None]
        grads = [g[None] for g in grads]
    layers, rows, cols = w.shape
    tile = _elementwise_tile(rows, cols)
    ng = len(grads)
    m_scale = 1.0 - ADAM_B1 ** ADAM_STEP
    v_scale = 1.0 - ADAM_B2 ** ADAM_STEP

    def body(*refs):
        w_ref, m_ref, v_ref = refs[:3]
        g_refs = refs[3:3 + ng]
        g_out, d_out, m_out, v_out = refs[-4:]
        g = g_refs[0][...].astype(F32)
        for r in g_refs[1:]:
            g = g + r[...].astype(F32)
        m_new = ADAM_B1 * m_ref[...] + (1.0 - ADAM_B1) * g
        v_new = ADAM_B2 * v_ref[...] + (1.0 - ADAM_B2) * (g * g)
        m_hat = m_new / m_scale
        v_hat = v_new / v_scale
        g_out[...] = g
        d_out[...] = -ADAM_LR * (m_hat / (jnp.sqrt(v_hat) + ADAM_EPS) + ADAM_WD * w_ref[...])
        m_out[...] = m_new
        v_out[...] = v_new

    spec = pl.BlockSpec((None, tile, cols), lambda l, i: (l, i, 0))
    out = pl.pallas_call(
        body, name=name, grid=(layers, rows // tile),
        in_specs=[spec] * (3 + ng), out_specs=[spec] * 4, out_shape=[_sds(w.shape, F32)] * 4,
        compiler_params=_params("parallel", "parallel"),
    )(w, m, v, *grads)
    return [o[0] for o in out] if flat else out


def _position():
    return lax.axis_index("x"), lax.axis_index("y"), lax.axis_index("c")


def _other_chips(x, y):
    return [(1 - x, y), (x, 1 - y), (1 - x, 1 - y)]


def _exchange_chips(arrays, name, scatter):
    n = len(arrays)

    def body(*refs):
        src, dst = refs[:n], refs[n:2 * n]
        send_sems, recv_sems, local_sems = refs[2 * n:]
        x, y, c = _position()
        mine = 2 * x + y
        copies = []
        for i in range(n):
            own = src[i].at[mine] if scatter else src[i]
            copies.append(pltpu.make_async_copy(own, dst[i].at[mine], local_sems.at[i]))
        for k, (px, py) in enumerate(_other_chips(x, y)):
            for i in range(n):
                piece = src[i].at[2 * px + py] if scatter else src[i]
                copies.append(pltpu.make_async_remote_copy(
                    src_ref=piece, dst_ref=dst[i].at[mine], send_sem=send_sems.at[k, i], recv_sem=recv_sems.at[k, i],
                    device_id=(px, py, c), device_id_type=MESH))
        for cp in copies:
            cp.start()
        for cp in copies:
            cp.wait()

    def out_shape(a):
        return _sds(a.shape if scatter else (N_CHIPS,) + a.shape, a.dtype)

    return pl.pallas_call(
        body, name=name, in_specs=[HBM_SPEC] * n, out_specs=[HBM_SPEC] * n, out_shape=[out_shape(a) for a in arrays],
        scratch_shapes=[pltpu.SemaphoreType.DMA((N_CHIPS - 1, n)), pltpu.SemaphoreType.DMA((N_CHIPS - 1, n)),
                        pltpu.SemaphoreType.DMA((n,))],
    )(*arrays)


def _swap_sibling(arrays, name):
    n = len(arrays)

    def body(*refs):
        src, dst = refs[:n], refs[n:2 * n]
        send_sems, recv_sems = refs[2 * n:]
        x, y, c = _position()
        copies = [pltpu.make_async_remote_copy(
            src_ref=src[i], dst_ref=dst[i], send_sem=send_sems.at[i], recv_sem=recv_sems.at[i],
            device_id=(x, y, 1 - c), device_id_type=MESH) for i in range(n)]
        for cp in copies:
            cp.start()
        for cp in copies:
            cp.wait()

    return pl.pallas_call(
        body, name=name, in_specs=[HBM_SPEC] * n, out_specs=[HBM_SPEC] * n,
        out_shape=[_sds(a.shape, a.dtype) for a in arrays],
        scratch_shapes=[pltpu.SemaphoreType.DMA((n,)), pltpu.SemaphoreType.DMA((n,))],
    )(*arrays)


SEM_SPEC = pl.BlockSpec(memory_space=pltpu.SEMAPHORE)
ANY_SPEC = pl.BlockSpec(memory_space=pl.ANY)
IN_FLIGHT = pltpu.SideEffectType.DATAFLOW_SIDE_EFFECTING


def _peer_sems(all_devices):
    return pltpu.SemaphoreType.DMA(((N_DEV if all_devices else N_CHIPS) - 1,))


def _own_slot(all_devices, x, y, c):
    return 4 * x + 2 * y + c if all_devices else 2 * x + y


def _peers(all_devices, x, y, c):
    if not all_devices:
        return [((px, py, c), 2 * px + py) for px, py in _other_chips(x, y)]
    out = []
    for fx in range(2):
        for fy in range(2):
            for fc in range(2):
                if fx or fy or fc:
                    px, py, pc = (1 - x if fx else x), (1 - y if fy else y), (1 - c if fc else c)
                    out.append(((px, py, pc), 4 * px + 2 * py + pc))
    return out


def _in_hbm(a):
    return pltpu.with_memory_space_constraint(a, pltpu.HBM)


def _place_slot(a, slot, n_slots, name):
    rows, cols = a.shape
    tile = _elementwise_tile(rows, cols)

    def body(slot_ref, a_ref, o_ref):
        o_ref[...] = a_ref[...]

    grid_spec = pltpu.PrefetchScalarGridSpec(
        num_scalar_prefetch=1, grid=(rows // tile,),
        in_specs=[pl.BlockSpec((tile, cols), lambda i, slot_ref: (i, 0))],
        out_specs=pl.BlockSpec((None, tile, cols), lambda i, slot_ref: (slot_ref[0], i, 0)))
    return pl.pallas_call(
        body, name=name, grid_spec=grid_spec, out_shape=_sds((n_slots, rows, cols), a.dtype),
        compiler_params=_params("parallel"),
    )(slot, a)


def _place_shard(w, layer, chip, name):
    _, rows, cols = w.shape
    tile = _elementwise_tile(rows, cols)

    def body(chip_ref, w_ref, o_ref):
        o_ref[...] = w_ref[...].astype(BF16)

    grid_spec = pltpu.PrefetchScalarGridSpec(
        num_scalar_prefetch=1, grid=(rows // tile,),
        in_specs=[pl.BlockSpec((None, tile, cols), lambda i, chip_ref: (layer, i, 0))],
        out_specs=pl.BlockSpec((None, tile, cols), lambda i, chip_ref: (chip_ref[0], i, 0)))
    return pl.pallas_call(
        body, name=name, grid_spec=grid_spec, out_shape=_sds((N_CHIPS, rows, cols), BF16),
        compiler_params=_params("parallel"),
    )(chip, w)


def _rows_of(ref, slot, core, halves):
    if not halves:
        return ref.at[slot]
    half = ref.shape[1] // 2
    return ref.at[slot, pl.ds(pl.multiple_of(core * half, half), half)]


def _gather_start(bufs, after, name, all_devices=False, halves=False):
    n = len(bufs)

    def body(*refs):
        buf = refs[:n]
        send, recv = refs[n + 1:2 * n + 1], refs[2 * n + 1:3 * n + 1]
        token = refs[4 * n + 1]
        x, y, c = _position()
        mine = _own_slot(all_devices, x, y, c)
        for i in range(n):
            for k, (peer, _) in enumerate(_peers(all_devices, x, y, c)):
                rows = _rows_of(buf[i], mine, c, halves)
                pltpu.make_async_remote_copy(
                    src_ref=rows, dst_ref=rows, send_sem=send[i].at[k], recv_sem=recv[i].at[k],
                    device_id=peer, device_id_type=MESH).start()
        token[...] = jnp.zeros_like(token)

    sems = _peer_sems(all_devices)
    out = pl.pallas_call(
        body, name=name, in_specs=[HBM_SPEC] * n + [ANY_SPEC],
        out_specs=[SEM_SPEC] * (2 * n) + [HBM_SPEC] * n + [pl.BlockSpec(memory_space=pltpu.VMEM)],
        out_shape=[sems] * (2 * n) + [pltpu.HBM(b.shape, b.dtype) for b in bufs] + [_sds((SUBLANES, LANES), F32)],
        input_output_aliases={i: 2 * n + i for i in range(n)},
        compiler_params=pltpu.CompilerParams(has_side_effects=IN_FLIGHT),
    )(*[_in_hbm(b) for b in bufs], after)
    return [(out[i], out[n + i], out[2 * n + i]) for i in range(n)], out[3 * n]


def _gather_relay(flights, after, name):
    n = len(flights)

    def body(*refs):
        buf, send, recv = refs[:n], refs[n:2 * n], refs[2 * n:3 * n]
        send2, recv2 = refs[3 * n + 1:4 * n + 1], refs[4 * n + 1:5 * n + 1]
        token = refs[6 * n + 1]
        x, y, c = _position()
        mine = 2 * x + y
        for i in range(n):
            for k, (peer, slot) in enumerate(_peers(False, x, y, c)):
                arrived = _rows_of(buf[i], slot, c, True)
                cp = pltpu.make_async_remote_copy(
                    src_ref=_rows_of(buf[i], mine, c, True), dst_ref=arrived, send_sem=send[i].at[k], recv_sem=recv[i].at[k],
                    device_id=peer, device_id_type=MESH)
                cp.wait_send()
                cp.wait_recv()
                pltpu.make_async_remote_copy(
                    src_ref=arrived, dst_ref=arrived, send_sem=send2[i].at[k], recv_sem=recv2[i].at[k],
                    device_id=(x, y, 1 - c), device_id_type=MESH).start()
        token[...] = jnp.zeros_like(token)

    bufs = [f[2] for f in flights]
    sems = _peer_sems(False)
    out = pl.pallas_call(
        body, name=name, in_specs=[HBM_SPEC] * n + [SEM_SPEC] * (2 * n) + [ANY_SPEC],
        out_specs=[SEM_SPEC] * (2 * n) + [HBM_SPEC] * n + [pl.BlockSpec(memory_space=pltpu.VMEM)],
        out_shape=[sems] * (2 * n) + [pltpu.HBM(b.shape, b.dtype) for b in bufs] + [_sds((SUBLANES, LANES), F32)],
        input_output_aliases={i: 2 * n + i for i in range(n)},
        compiler_params=pltpu.CompilerParams(has_side_effects=IN_FLIGHT),
    )(*bufs, *[f[0] for f in flights], *[f[1] for f in flights], after)
    return [(out[i], out[n + i], out[2 * n + i]) for i in range(n)], out[3 * n]


def _gather_wait(send, recv, buf, after, name, all_devices=False, relayed=False):
    def body(buf_ref, send_ref, recv_ref, after_ref, out_ref):
        x, y, c = _position()
        mine = _own_slot(all_devices, x, y, c)
        for k, (peer, slot) in enumerate(_peers(all_devices, x, y, c)):
            if relayed:
                cp = pltpu.make_async_remote_copy(
                    src_ref=_rows_of(buf_ref, slot, c, True), dst_ref=_rows_of(buf_ref, slot, 1 - c, True),
                    send_sem=send_ref.at[k], recv_sem=recv_ref.at[k], device_id=(x, y, 1 - c), device_id_type=MESH)
            else:
                cp = pltpu.make_async_remote_copy(
                    src_ref=buf_ref.at[mine], dst_ref=buf_ref.at[slot], send_sem=send_ref.at[k], recv_sem=recv_ref.at[k],
                    device_id=peer, device_id_type=MESH)
            cp.wait_send()
            cp.wait_recv()

    return pl.pallas_call(
        body, name=name, in_specs=[HBM_SPEC, SEM_SPEC, SEM_SPEC, ANY_SPEC], out_specs=HBM_SPEC,
        out_shape=pltpu.HBM(buf.shape, buf.dtype), input_output_aliases={0: 0},
        compiler_params=pltpu.CompilerParams(has_side_effects=IN_FLIGHT),
    )(buf, send, recv, after)


def _scatter_start(pieces, name):
    n = len(pieces)

    def body(*refs):
        src = refs[:n]
        send, recv = refs[n:2 * n], refs[2 * n:3 * n]
        land = refs[4 * n:5 * n]
        token = refs[5 * n]
        x, y, c = _position()
        mine = 2 * x + y
        for i in range(n):
            for k, (px, py) in enumerate(_other_chips(x, y)):
                pltpu.make_async_remote_copy(
                    src_ref=src[i].at[2 * px + py], dst_ref=land[i].at[mine], send_sem=send[i].at[k], recv_sem=recv[i].at[k],
                    device_id=(px, py, c), device_id_type=MESH).start()
        token[...] = jnp.zeros_like(token)

    hbm = [pltpu.HBM(p.shape, p.dtype) for p in pieces]
    out = pl.pallas_call(
        body, name=name, in_specs=[HBM_SPEC] * n,
        out_specs=[SEM_SPEC] * (2 * n) + [HBM_SPEC] * (2 * n) + [pl.BlockSpec(memory_space=pltpu.VMEM)],
        out_shape=[_peer_sems(False)] * (2 * n) + hbm + hbm + [_sds((SUBLANES, LANES), F32)],
        input_output_aliases={i: 2 * n + i for i in range(n)},
        compiler_params=pltpu.CompilerParams(has_side_effects=IN_FLIGHT),
    )(*[_in_hbm(p) for p in pieces])
    return [(out[i], out[n + i], out[2 * n + i], out[3 * n + i]) for i in range(n)], out[4 * n]


def _scatter_wait(send, recv, pieces, land, after, name):
    def body(src_ref, land_ref, send_ref, recv_ref, after_ref, src_out, land_out):
        x, y, c = _position()
        for k, (px, py) in enumerate(_other_chips(x, y)):
            cp = pltpu.make_async_remote_copy(
                src_ref=src_ref.at[2 * px + py], dst_ref=land_ref.at[2 * px + py], send_sem=send_ref.at[k],
                recv_sem=recv_ref.at[k], device_id=(px, py, c), device_id_type=MESH)
            cp.wait_send()
            cp.wait_recv()

    return pl.pallas_call(
        body, name=name, in_specs=[HBM_SPEC, HBM_SPEC, SEM_SPEC, SEM_SPEC, ANY_SPEC], out_specs=[HBM_SPEC, HBM_SPEC],
        out_shape=[pltpu.HBM(pieces.shape, pieces.dtype), pltpu.HBM(land.shape, land.dtype)],
        input_output_aliases={0: 0, 1: 1},
        compiler_params=pltpu.CompilerParams(has_side_effects=IN_FLIGHT),
    )(pieces, land, send, recv, after)


def _rope_tables(batch, rows_per_example):
    inv_freq = ROPE_THETA ** (-jnp.arange(0, 64, 2, dtype=F32) / 64)
    ang = jnp.arange(rows_per_example, dtype=F32)[:, None] * inv_freq[None, :]
    cos, sin = jnp.cos(ang), jnp.sin(ang)
    cos128 = jnp.concatenate([cos, cos, cos, cos], axis=1)
    sin128 = jnp.concatenate([-sin, sin, -sin, sin], axis=1)
    return jnp.tile(cos128, (batch, 1)), jnp.tile(sin128, (batch, 1))


def _block_diagonal(w):
    eye = jnp.eye(LRU_BLOCKS, dtype=w.dtype)
    return (w[:, :, None, :] * eye[:, None, :, None]).reshape(LRU_W, LRU_W)


def _diagonal_blocks(dense):
    d4 = dense.reshape(LRU_BLOCKS, LRU_BLOCK, LRU_BLOCKS, LRU_BLOCK)
    return jnp.stack([d4[n, :, n, :] for n in range(LRU_BLOCKS)])


def _row(v):
    return v.reshape(1, -1)


def _local_step(x, loss_target, meta_tokens, small, depth, big_weight, on_layer_grads):
    batch, seq, _ = x.shape
    n_real = N_META + seq
    n_blocks = -(-n_real // TIME_BLOCK)
    per_example = n_blocks * TIME_BLOCK
    pad = per_example - n_real
    meta = jnp.broadcast_to(meta_tokens[None], (batch, N_META, D_MODEL))
    h = jnp.concatenate([meta, x, jnp.zeros((batch, pad, D_MODEL), F32)], axis=1).reshape(batch * per_example, D_MODEL)
    rope_cos, rope_sin = _rope_tables(batch, per_example)

    saved = []
    for l in range(depth):
        wa = _block_diagonal(small['gate_a_w'][l]).astype(BF16)
        wx = _block_diagonal(small['gate_x_w'][l]).astype(BF16)
        lru_small = (small['conv_w'][l], _row(small['conv_b'][l]), wa, _row(small['gate_a_b'][l]), wx,
                     _row(small['gate_x_b'][l]), _row(small['lru_lambda'][l]))
        w_in = big_weight('w_in', l, h)
        z1, q, kv, xg = _in_proj_fwd(h, _row(small['pre_mix_norm'][l]), w_in, rope_cos, rope_sin)
        attn, lse = _attn_fwd(q, kv, small['attn_sinks'][l], batch, n_blocks)
        hs, lru = _lru_fwd(xg, *lru_small, batch, n_blocks)
        w_out = big_weight('w_out', l, xg)
        w_up4 = big_weight('w_up', l, xg)
        w_down = big_weight('w_down', l, xg)
        gains = [_row(small[n][l]) for n in ('attn_group_norm', 'lru_group_norm', 'post_mix_norm', 'pre_mlp_norm',
                                             'post_mlp_norm')]
        groups, o, h_mid, z2, slope, act, y, h_out = _out_mlp_fwd(attn, lru, h, w_out, w_up4, w_down, *gains)
        saved.append(dict(h=h, z1=z1, q=q, kv=kv, xg=xg, attn=attn, lse=lse, hs=hs, lru=lru, groups=groups, o=o,
                          h_mid=h_mid, z2=z2, slope=slope, act=act, y=y, lru_small=lru_small, w_in=w_in, w_out=w_out,
                          w_up4=w_up4, w_down=w_down, gains=gains))
        h = h_out

    dh, sq_err = _loss_head(h.reshape(batch, per_example, D_MODEL), loss_target)
    dh = dh.reshape(batch * per_example, D_MODEL)

    gs = {n: [None] * depth for n in REPLICATED_NAMES + ('conv_w',)}
    handed_over = None
    for l in reversed(range(depth)):
        s = saved[l]

        def ordered_after(gain):
            return gain if handed_over is None else gain + handed_over[0, 0]

        gains = list(s['gains'])
        gains[4] = ordered_after(gains[4])
        dh_mid, dy, dup, do, d_attn, d_lru, g_rows = _mlp_out_bwd(dh, s['h_mid'], s['y'], s['slope'], s['o'], s['attn'],
                                                                s['lru'], s['w_out'], s['w_up4'], s['w_down'], *gains)
        handed_over = on_layer_grads(l, {
            'w_down': _matmul_tn(s['act'], dy, 512, D_MODEL, "grad_w_down", BF16),
            'w_up': _matmul_tn(s['z2'], dup, 512, D_MODEL, "grad_w_up", BF16, column_blocks=True)})
        lru_small = list(s['lru_small'])
        lru_small[1] = ordered_after(lru_small[1])
        dxg, dwa, dwx, g_lru = _lru_bwd(s['xg'], s['hs'], d_lru, *lru_small, batch, n_blocks)
        dq, dkv, dsink = _attn_bwd(s['q'], s['kv'], small['attn_sinks'][l], s['attn'], s['lse'], d_attn, batch, n_blocks)
        dh, dproj, g_in = _in_proj_bwd(dh_mid, s['h'], dq, dkv, dxg, rope_cos, rope_sin, _row(small['pre_mix_norm'][l]),
                                       s['w_in'])
        handed_over = on_layer_grads(l, {
            'w_out': _matmul_tn(s['groups'], do, 512, D_MODEL, "grad_w_out", BF16),
            'w_in': _matmul_tn(dproj, s['z1'], IN_COLS // 2, D_MODEL, "grad_w_in", BF16)})
        gs['pre_mlp_norm'][l], gs['post_mlp_norm'][l] = g_rows[0], g_rows[1]
        gs['post_mix_norm'][l] = g_rows[2]
        gs['attn_group_norm'][l], gs['lru_group_norm'][l] = g_rows[3, :ATTN_W], g_rows[3, ATTN_W:]
        gs['pre_mix_norm'][l] = g_in[0]
        gs['attn_sinks'][l] = dsink[:, 0]
        gs['conv_w'][l] = g_lru[:CONV_TAPS]
        gs['conv_b'][l], gs['gate_a_b'][l], gs['gate_x_b'][l], gs['lru_lambda'][l] = g_lru[4], g_lru[5], g_lru[6], g_lru[7]
        gs['gate_a_w'][l] = _diagonal_blocks(dwa)
        gs['gate_x_w'][l] = _diagonal_blocks(dwx)

    grad_x = dh.reshape(batch, per_example, D_MODEL)[:, N_META:n_real]
    grad_meta = _meta_grad(dh, batch, per_example)
    small_grads = {n: jnp.stack(v) for n, v in gs.items()}
    return sq_err, grad_x, grad_meta, small_grads, handed_over


PACK_UNIT = SUBLANES * LANES


def _size(shape):
    size = 1
    for d in shape:
        size *= d
    return size


def _pack(arrays):
    parts = []
    for a in arrays:
        flat = a.reshape(-1)
        padded = -(-flat.shape[0] // PACK_UNIT) * PACK_UNIT
        if padded != flat.shape[0]:
            flat = jnp.pad(flat, (0, padded - flat.shape[0]))
        parts.append(flat.reshape(-1, LANES))
    return jnp.concatenate(parts, axis=0)


def _unpack(buf, shapes):
    out, at = [], 0
    for shp in shapes:
        size = _size(shp)
        rows = -(-size // PACK_UNIT) * SUBLANES
        part = buf[at:at + rows]
        if rows * LANES != size:
            part = part.reshape(-1)[:size]
        out.append(part.reshape(shp))
        at += rows
    return out


def kernel(x, meta_tokens, pre_mix_norm, w_in, attn_sinks, conv_w, conv_b, gate_a_w, gate_a_b, gate_x_w, gate_x_b, lru_lambda, attn_group_norm, lru_group_norm, w_out, post_mix_norm, pre_mlp_norm, w_up, w_down, post_mlp_norm, loss_target, m_meta_tokens, m_pre_mix_norm, m_w_in, m_attn_sinks, m_conv_w, m_conv_b, m_gate_a_w, m_gate_a_b, m_gate_x_w, m_gate_x_b, m_lru_lambda, m_attn_group_norm, m_lru_group_norm, m_w_out, m_post_mix_norm, m_pre_mlp_norm, m_w_up, m_w_down, m_post_mlp_norm, v_meta_tokens, v_pre_mix_norm, v_w_in, v_attn_sinks, v_conv_w, v_conv_b, v_gate_a_w, v_gate_a_b, v_gate_x_w, v_gate_x_b, v_lru_lambda, v_attn_group_norm, v_lru_group_norm, v_w_out, v_post_mix_norm, v_pre_mlp_norm, v_w_up, v_w_down, v_post_mlp_norm):
    given = dict(locals())
    w = {n: given[n] for n in WEIGHT_NAMES}
    m = {n: given['m_' + n] for n in WEIGHT_NAMES}
    v = {n: given['v_' + n] for n in WEIGHT_NAMES}
    depth = w_in.shape[0]
    for d in (w, m, v):
        d['w_in'] = jnp.swapaxes(d['w_in'], 1, 2)
    chip = 2 * lax.axis_index("x") + lax.axis_index("y")
    chip1 = chip.reshape(1).astype(jnp.int32)

    in_flight = []
    all_started = chip1
    for l in range(depth):
        bufs = [_place_shard(w[n], l, chip1, "place_" + n) for n in BIG_NAMES]
        handles, all_started = _gather_start(bufs, all_started, "gather_start_%d" % l, halves=True)
        in_flight.append(dict(zip(BIG_NAMES, handles)))
    first_use = {'w_in': ('w_in',), 'w_out': ('w_out', 'w_up', 'w_down')}

    def big_weight(name, l, after):
        if l == 0 and name == 'w_in':
            after = all_started
        names = first_use.get(name, ()) if l == 0 else (BIG_NAMES if name == 'w_in' else ())
        if names:
            relayed, _ = _gather_relay([in_flight[l][n] for n in names], after, "gather_relay_%d_%s" % (l, name))
            in_flight[l].update(zip(names, relayed))
        send, recv, buf = in_flight[l][name]
        full = _gather_wait(send, recv, buf, after, "gather_wait_%d_%s" % (l, name), relayed=True)
        if name == 'w_in':
            return full.reshape(IN_COLS, D_MODEL)
        if name == 'w_out':
            return full.reshape(D_MODEL, D_MODEL)
        if name == 'w_down':
            return full.reshape(D_FF, D_MODEL)
        return full

    gathered = _exchange_chips([meta_tokens, conv_w], "gather_small_weights", scatter=False)
    full_meta = jnp.concatenate([gathered[0][s] for s in range(N_CHIPS)], axis=1)
    small = {n: w[n] for n in REPLICATED_NAMES}
    small['conv_w'] = jnp.concatenate([gathered[1][s] for s in range(N_CHIPS)], axis=2)

    scattering = [{} for _ in range(depth)]

    pending = {}

    def on_layer_grads(l, big):
        if l > 0 and 'w_in' not in big:
            pending.update(big)
            return None
        big = {**pending, **big}
        pending.clear()
        names = list(big)
        pieces = [big[n].reshape(N_CHIPS, -1, D_MODEL) for n in names]
        handles, started = _scatter_start(pieces, "scatter_start_%d_%s" % (l, names[0]))
        scattering[l].update(zip(names, handles))
        return started

    sq_err, grad_x, grad_meta, small_grads, all_handed_over = _local_step(x, loss_target, full_meta, small, depth, big_weight,
                                                                           on_layer_grads)
    loss = lax.psum(sq_err[0, 0] * (0.5 / D_MODEL), ("x", "y", "c"))

    small_names = list(REPLICATED_NAMES) + list(COLUMN_SHARDED_SMALL)
    small_full = dict(small_grads)
    small_full['meta_tokens'] = grad_meta
    device1 = (2 * chip + lax.axis_index("c")).reshape(1).astype(jnp.int32)
    packed = _place_slot(_pack([small_full[n] for n in small_names]), device1, N_DEV, "place_small_grads")
    [small_flight], done = _gather_start([packed], all_handed_over, "gather_small_grads_start", all_devices=True)

    partial = {n: None for n in BIG_NAMES}
    for l in reversed(range(depth)):
        for n in BIG_NAMES:
            send, recv, pieces, land = scattering[l][n]
            pieces, land = _scatter_wait(send, recv, pieces, land, done, "scatter_wait_%d_%s" % (l, n))
            partial[n], done = _sum_pieces(pieces, land, chip1, l, depth, partial[n])
    sibling = _swap_sibling([partial[n] for n in BIG_NAMES], "swap_partial_grads")
    results = {n: _adamw(w[n], m[n], v[n], [partial[n], other], "adamw_" + n) for n, other in zip(BIG_NAMES, sibling)}

    slots = _gather_wait(*small_flight, results['w_down'][0], "gather_small_grads_wait", all_devices=True)
    summed = _unpack(_sum_slots(slots, "sum_small_grads"), [small_full[n].shape for n in small_names])
    grads = dict(zip(small_names, summed))
    grads['meta_tokens'] = lax.dynamic_slice_in_dim(grads['meta_tokens'], chip * meta_tokens.shape[1], meta_tokens.shape[1], 1)
    grads['conv_w'] = lax.dynamic_slice_in_dim(grads['conv_w'], chip * conv_w.shape[2], conv_w.shape[2], 2)

    out_g, out_d, out_m, out_v = {}, {}, {}, {}
    for n in BIG_NAMES:
        out_g[n], out_d[n], out_m[n], out_v[n] = [jnp.swapaxes(r, 1, 2) if n == 'w_in' else r for r in results[n][:4]]
    shapes = [w[n].shape for n in small_names]
    res = _adamw(_pack([w[n] for n in small_names]), _pack([m[n] for n in small_names]), _pack([v[n] for n in small_names]),
                 [_pack([grads[n] for n in small_names])], "adamw_small")
    for k, store in enumerate((out_g, out_d, out_m, out_v)):
        for n, a in zip(small_names, _unpack(res[k], shapes)):
            store[n] = a

    return (loss, grad_x, *[out_g[n] for n in WEIGHT_NAMES], *[out_d[n] for n in WEIGHT_NAMES],
            *[out_m[n] for n in WEIGHT_NAMES], *[out_v[n] for n in WEIGHT_NAMES])
```

```python
import functools

import jax
import jax.numpy as jnp
from jax import lax
from jax.experimental import pallas as pl
from jax.experimental.pallas import tpu as pltpu

F32 = jnp.float32
BF16 = jnp.bfloat16

D_MODEL = 1024
N_HEADS = 8
ATTN_W = 512
KV_W = 128
LRU_W = 512
LRU_BLOCKS = 8
LRU_BLOCK = 64
IN_COLS = 1792
D_FF = 4096
N_META = 16
CONV_TAPS = 4
LRU_C = 8.0
ROPE_THETA = 10000.0
EPS = 1e-6
ATTN_SCALE = 0.125

ADAM_LR = 0.001
ADAM_B1 = 0.9
ADAM_B2 = 0.999
ADAM_EPS = 1e-08
ADAM_WD = 0.01
ADAM_STEP = 10

N_CHIPS = 4
N_DEV = 8
TIME_BLOCK = 128
ROW_TILE = 256
PROJ_ROW_TILE = 544
LANES = 128
SUBLANES = 8
MASKED = -1e30
VMEM_LIMIT = 56 * 1024 * 1024

MESH = pl.DeviceIdType.MESH
HBM_SPEC = pl.BlockSpec(memory_space=pltpu.HBM)

WEIGHT_NAMES = ['meta_tokens', 'pre_mix_norm', 'w_in', 'attn_sinks', 'conv_w', 'conv_b', 'gate_a_w', 'gate_a_b',
                'gate_x_w', 'gate_x_b', 'lru_lambda', 'attn_group_norm', 'lru_group_norm', 'w_out', 'post_mix_norm',
                'pre_mlp_norm', 'w_up', 'w_down', 'post_mlp_norm']
BIG_NAMES = ('w_in', 'w_out', 'w_up', 'w_down')
SAME_SHAPE = (('w_in',), ('w_out',), ('w_up', 'w_down'))
COLUMN_SHARDED_SMALL = ('meta_tokens', 'conv_w')
REPLICATED_NAMES = tuple(n for n in WEIGHT_NAMES if n not in BIG_NAMES and n not in COLUMN_SHARDED_SMALL)


def _sds(shape, dtype):
    return jax.ShapeDtypeStruct(tuple(shape), dtype)


def _params(*sem):
    return pltpu.CompilerParams(dimension_semantics=sem, vmem_limit_bytes=VMEM_LIMIT)


def _row_spec(width, tile=ROW_TILE):
    return pl.BlockSpec((tile, width), lambda i: (i, 0))


def _proj_tile(rows):
    tile = PROJ_ROW_TILE
    while rows % tile:
        tile -= 16
    return tile


def _whole_spec(a):
    nd = a.ndim
    return pl.BlockSpec(a.shape, lambda *_: (0,) * nd)


def _resident_spec(a):
    nd = a.ndim
    return pl.BlockSpec(a.shape, lambda *_: (0,) * nd, pipeline_mode=pl.Buffered(1))


def _rms(x, g):
    r = lax.rsqrt(jnp.mean(x * x, axis=-1, keepdims=True) + EPS)
    return x * r * g


def _rms_bwd(dy, x, g):
    r = lax.rsqrt(jnp.mean(x * x, axis=-1, keepdims=True) + EPS)
    xh = x * r
    dg = jnp.sum(dy * xh, axis=0, keepdims=True)
    dxh = dy * g
    dx = r * (dxh - xh * jnp.mean(dxh * xh, axis=-1, keepdims=True))
    return dx, dg


def _rope(x, cos, sin_signed):
    width = x.shape[1]
    reps = width // LANES
    if reps > 1:
        cos = jnp.tile(cos, (1, reps))
        sin_signed = jnp.tile(sin_signed, (1, reps))
    lane = lax.broadcasted_iota(jnp.int32, x.shape, 1)
    first_half = jnp.bitwise_and(lane, 32) == 0
    other = jnp.where(first_half, pltpu.roll(x, width - 32, 1), pltpu.roll(x, 32, 1))
    return x * cos + other * sin_signed


def _sigmoid(x):
    return 1.0 / (1.0 + jnp.exp(-x))


def _log1p(e):
    return jnp.where(e < 1e-3, e * (1.0 - e * (0.5 - e * (1.0 / 3.0))), jnp.log(1.0 + e))


def _one_minus_square(a, log_a):
    x = 2.0 * log_a
    return jnp.where(x > -0.002, x * (-1.0 - 0.5 * x), 1.0 - a * a)


GELU_K = 0.7978845608028654
GELU_C = 0.044715


def _gelu(x):
    t = jnp.tanh(GELU_K * (x + GELU_C * x * x * x))
    return 0.5 * x * (1.0 + t)


def _gelu_and_grad(x):
    x2 = x * x
    t = jnp.tanh(GELU_K * (x + GELU_C * x * x2))
    val = 0.5 * x * (1.0 + t)
    grad = 0.5 * (1.0 + t) + 0.5 * x * (1.0 - t * t) * GELU_K * (1.0 + 3.0 * GELU_C * x2)
    return val, grad


def _dot(a, b):
    return jnp.dot(a, b, preferred_element_type=F32)


def _dot_nt(a, b):
    return lax.dot_general(a, b, (((1,), (1,)), ((), ())), preferred_element_type=F32)


def _dot_tn(a, b):
    return lax.dot_general(a, b, (((0,), (0,)), ((), ())), preferred_element_type=F32)


def _put_rows(rows_8, values):
    d = values[0].shape[1]
    rowid = lax.broadcasted_iota(jnp.int32, (rows_8, d), 0)
    out = jnp.zeros((rows_8, d), F32)
    for k, v in enumerate(values):
        out = out + jnp.where(rowid == k, v, 0.0)
    return out


def _in_proj_fwd(h, gain, w_in, rope_cos, rope_sin):
    rows = h.shape[0]

    def body(h_ref, g_ref, w_ref, c_ref, s_ref, z_ref, q_ref, kv_ref, xg_ref):
        z = _rms(h_ref[...], g_ref[...]).astype(BF16)
        z_ref[...] = z
        proj = _dot_nt(z, w_ref[...])
        cos = c_ref[...]
        sin = s_ref[...]
        q_ref[...] = (_rope(proj[:, :ATTN_W], cos, sin) * ATTN_SCALE).astype(BF16)
        kv_ref[:, :KV_W] = _rope(proj[:, ATTN_W:ATTN_W + KV_W], cos, sin).astype(BF16)
        kv_ref[:, KV_W:] = proj[:, ATTN_W + KV_W:ATTN_W + 2 * KV_W].astype(BF16)
        xg_ref[...] = proj[:, ATTN_W + 2 * KV_W:]

    tile = _proj_tile(rows)
    rs = functools.partial(_row_spec, tile=tile)
    return pl.pallas_call(
        body, name="in_proj_fwd", grid=(rows // tile,),
        in_specs=[rs(D_MODEL), _whole_spec(gain), _whole_spec(w_in), rs(LANES), rs(LANES)],
        out_specs=[rs(D_MODEL), rs(ATTN_W), rs(2 * KV_W), rs(2 * LRU_W)],
        out_shape=[_sds((rows, D_MODEL), BF16), _sds((rows, ATTN_W), BF16), _sds((rows, 2 * KV_W), BF16),
                   _sds((rows, 2 * LRU_W), F32)],
        compiler_params=_params("parallel"),
    )(h, gain, w_in, rope_cos, rope_sin)


def _in_proj_bwd(dh_mid, h, dq, dkv, dxg, rope_cos, rope_sin, gain, w_in):
    rows = h.shape[0]

    def body(dhm_ref, h_ref, dq_ref, dkv_ref, dxg_ref, c_ref, s_ref, g_ref, w_ref, dh_ref, dp_ref, gacc_ref):
        @pl.when(pl.program_id(0) == 0)
        def _():
            gacc_ref[...] = jnp.zeros_like(gacc_ref)

        cos = c_ref[...]
        sin = -s_ref[...]
        dp_ref[:, :ATTN_W] = (_rope(dq_ref[...], cos, sin) * ATTN_SCALE).astype(BF16)
        dp_ref[:, ATTN_W:ATTN_W + KV_W] = _rope(dkv_ref[:, :KV_W], cos, sin).astype(BF16)
        dp_ref[:, ATTN_W + KV_W:ATTN_W + 2 * KV_W] = dkv_ref[:, KV_W:].astype(BF16)
        dp_ref[:, ATTN_W + 2 * KV_W:] = dxg_ref[...]
        dz = _dot(dp_ref[...], w_ref[...])
        dx, dg = _rms_bwd(dz, h_ref[...], g_ref[...])
        dh_ref[...] = dhm_ref[...] + dx
        gacc_ref[...] += _put_rows(SUBLANES, [dg])

    tile = _proj_tile(rows)
    rs = functools.partial(_row_spec, tile=tile)
    return pl.pallas_call(
        body, name="in_proj_bwd", grid=(rows // tile,),
        in_specs=[rs(D_MODEL), rs(D_MODEL), rs(ATTN_W), rs(2 * KV_W), rs(2 * LRU_W),
                  rs(LANES), rs(LANES), _whole_spec(gain), _whole_spec(w_in)],
        out_specs=[rs(D_MODEL), rs(IN_COLS), pl.BlockSpec((SUBLANES, D_MODEL), lambda i: (0, 0))],
        out_shape=[_sds((rows, D_MODEL), F32), _sds((rows, IN_COLS), BF16), _sds((SUBLANES, D_MODEL), F32)],
        compiler_params=_params("arbitrary"),
    )(dh_mid, h, dq, dkv, dxg, rope_cos, rope_sin, gain, w_in)


def _kv_lane_variants(t, group):
    lane = lax.broadcasted_iota(jnp.int32, t.shape, 1)
    low = lane < 64
    swapped = pltpu.roll(t, 64, 1)
    if group == 0:
        lo, hi = jnp.where(low, t, 0.0), jnp.where(low, 0.0, swapped)
    else:
        lo, hi = jnp.where(low, swapped, 0.0), jnp.where(low, 0.0, t)
    return jnp.concatenate([lo, hi], axis=0).astype(BF16)


GROUP_ROWS = 2 * TIME_BLOCK
KEYS = 2 * TIME_BLOCK


def _window_mask(j):
    r = jnp.bitwise_and(lax.broadcasted_iota(jnp.int32, (GROUP_ROWS, KEYS), 0), TIME_BLOCK - 1)
    c = lax.broadcasted_iota(jnp.int32, (GROUP_ROWS, KEYS), 1)
    return (c > r) & (c <= r + TIME_BLOCK) & ((c >= TIME_BLOCK) | (j > 0))


def _group_rows(ref, ex, group):
    lo = 2 * group * LANES
    return jnp.concatenate([ref[ex, :, lo:lo + LANES], ref[ex, :, lo + LANES:lo + 2 * LANES]], axis=0)


def _per_head(sink_ref, group, half):
    upper = lax.broadcasted_iota(jnp.int32, (GROUP_ROWS, 1), 0) < TIME_BLOCK
    return jnp.where(upper, sink_ref[4 * group + half], sink_ref[4 * group + 2 + half])


def _by_example(a, batch):
    return a.reshape(batch, a.shape[0] // batch, a.shape[1])


def _attn_fwd(q, kv, sinks, batch, n_blocks):
    rows = q.shape[0]

    def body(sink_ref, q_ref, kvc_ref, kvp_ref, o_ref, lse_ref):
        j = pl.program_id(0)
        mask = _window_mask(j)
        chains = [(ex, group) for ex in range(batch) for group in range(2)]
        v_cat, s_all, probs = {}, {}, {ch: [] for ch in chains}
        for ex, group in chains:
            kv2 = jnp.concatenate([kvp_ref[ex], kvc_ref[ex]], axis=0).astype(F32)
            k_cat = _kv_lane_variants(kv2[:, :KV_W], group)
            v_cat[ex, group] = _kv_lane_variants(kv2[:, KV_W:], group)
            s_all[ex, group] = _dot_nt(_group_rows(q_ref, ex, group), k_cat)
        for half in range(2):
            for ex, group in chains:
                sink = _per_head(sink_ref, group, half)
                s = jnp.where(mask, s_all[ex, group][:, half * KEYS:(half + 1) * KEYS], MASKED)
                m = jnp.maximum(jnp.max(s, axis=1, keepdims=True), sink)
                e = jnp.exp(s - m)
                den = jnp.sum(e, axis=1, keepdims=True) + jnp.exp(sink - m)
                probs[ex, group].append((e / den).astype(BF16))
                lse = m + jnp.log(den)
                upper_head, lower_head = 4 * group + half, 4 * group + 2 + half
                lse_ref[ex, :, upper_head:upper_head + 1] = lse[:TIME_BLOCK]
                lse_ref[ex, :, lower_head:lower_head + 1] = lse[TIME_BLOCK:]
        for ex, group in chains:
            out = _dot(jnp.concatenate(probs[ex, group], axis=1), v_cat[ex, group])
            o_ref[ex, :, 2 * group * LANES:(2 * group + 1) * LANES] = out[:TIME_BLOCK]
            o_ref[ex, :, (2 * group + 1) * LANES:(2 * group + 2) * LANES] = out[TIME_BLOCK:]

    def blk(width):
        return pl.BlockSpec((batch, TIME_BLOCK, width), lambda j: (0, j, 0))

    prev = pl.BlockSpec((batch, TIME_BLOCK, 2 * KV_W), lambda j: (0, jnp.maximum(j - 1, 0), 0))
    kv3 = _by_example(kv, batch)
    out, lse = pl.pallas_call(
        body, name="attn_fwd", grid=(n_blocks,),
        in_specs=[pl.BlockSpec(memory_space=pltpu.SMEM), blk(ATTN_W), blk(2 * KV_W), prev],
        out_specs=[blk(ATTN_W), blk(N_HEADS)],
        out_shape=[_sds((batch, rows // batch, ATTN_W), F32), _sds((batch, rows // batch, N_HEADS), F32)],
        compiler_params=_params("parallel"),
    )(sinks, _by_example(q, batch), kv3, kv3)
    return out.reshape(rows, ATTN_W), lse.reshape(rows, N_HEADS)


def _attn_bwd(q, kv, sinks, out, lse, d_out, batch, n_blocks):
    rows = q.shape[0]

    def body(sink_ref, q_ref, kvc_ref, kvp_ref, o_ref, do_ref, lse_ref, dq_ref, dkv_ref, dsink_ref, carry):
        j = pl.program_id(0)

        @pl.when(j == 0)
        def _():
            dsink_ref[...] = jnp.zeros_like(dsink_ref)

        def all_chains(dsink_vals):
            mask = _window_mask(j)
            lane = lax.broadcasted_iota(jnp.int32, (GROUP_ROWS, LANES), 1)
            low = lax.broadcasted_iota(jnp.int32, (KEYS, LANES), 1) < 64
            upper = lax.broadcasted_iota(jnp.int32, (GROUP_ROWS, 1), 0) < TIME_BLOCK
            chains = [(ex, group) for ex in range(batch) for group in range(2)]
            k_cat, v_cat, q_rows, do_b, od, s_all, dp_all = {}, {}, {}, {}, {}, {}, {}
            for ch in chains:
                ex, group = ch
                kv2 = jnp.concatenate([kvp_ref[ex], kvc_ref[ex]], axis=0).astype(F32)
                k_cat[ch] = _kv_lane_variants(kv2[:, :KV_W], group)
                v_cat[ch] = _kv_lane_variants(kv2[:, KV_W:], group)
                q_rows[ch] = _group_rows(q_ref, ex, group)
                do_rows = _group_rows(do_ref, ex, group)
                do_b[ch] = do_rows.astype(BF16)
                od[ch] = do_rows * _group_rows(o_ref, ex, group)
                s_all[ch] = _dot_nt(q_rows[ch], k_cat[ch])
                dp_all[ch] = _dot_nt(do_b[ch], v_cat[ch])
            probs, dss = {ch: [] for ch in chains}, {ch: [] for ch in chains}
            for half in range(2):
                for ch in chains:
                    ex, group = ch
                    heads = (4 * group + half, 4 * group + 2 + half)
                    sink = _per_head(sink_ref, group, half)
                    lse_h = jnp.concatenate([lse_ref[ex, :, h:h + 1] for h in heads], axis=0)
                    in_half = (lane < 64) if half == 0 else (lane >= 64)
                    delta = jnp.sum(jnp.where(in_half, od[ch], 0.0), axis=1, keepdims=True)
                    cols = slice(half * KEYS, (half + 1) * KEYS)
                    prob = jnp.exp(jnp.where(mask, s_all[ch][:, cols], MASKED) - lse_h)
                    probs[ch].append(prob.astype(BF16))
                    dss[ch].append((prob * (dp_all[ch][:, cols] - delta)).astype(BF16))
                    dsink = -jnp.exp(sink - lse_h) * delta
                    dsink_vals[heads[0]] = dsink_vals[heads[0]] + jnp.sum(jnp.where(upper, dsink, 0.0), axis=0, keepdims=True)
                    dsink_vals[heads[1]] = dsink_vals[heads[1]] + jnp.sum(jnp.where(upper, 0.0, dsink), axis=0, keepdims=True)
            dk_tile = {ex: jnp.zeros((KEYS, KV_W), F32) for ex in range(batch)}
            dv_tile = {ex: jnp.zeros((KEYS, KV_W), F32) for ex in range(batch)}
            for ch in chains:
                ex, group = ch
                ds = jnp.concatenate(dss[ch], axis=1)
                dq_rows = _dot(ds, k_cat[ch])
                dq_ref[ex, :, 2 * group * LANES:(2 * group + 1) * LANES] = dq_rows[:TIME_BLOCK]
                dq_ref[ex, :, (2 * group + 1) * LANES:(2 * group + 2) * LANES] = dq_rows[TIME_BLOCK:]
                dk_cat = _dot_tn(ds, q_rows[ch])
                dv_cat = _dot_tn(jnp.concatenate(probs[ch], axis=1), do_b[ch])
                if group == 0:
                    dk_tile[ex] = dk_tile[ex] + jnp.where(low, dk_cat[:KEYS] + pltpu.roll(dk_cat[KEYS:], 64, 1), 0.0)
                    dv_tile[ex] = dv_tile[ex] + jnp.where(low, dv_cat[:KEYS] + pltpu.roll(dv_cat[KEYS:], 64, 1), 0.0)
                else:
                    dk_tile[ex] = dk_tile[ex] + jnp.where(low, 0.0, pltpu.roll(dk_cat[:KEYS], 64, 1) + dk_cat[KEYS:])
                    dv_tile[ex] = dv_tile[ex] + jnp.where(low, 0.0, pltpu.roll(dv_cat[:KEYS], 64, 1) + dv_cat[KEYS:])

            @pl.when(j > 0)
            def _():
                for ex in range(batch):
                    dkv_ref[ex, :, :KV_W] = carry[ex, :, :KV_W] + dk_tile[ex][:TIME_BLOCK]
                    dkv_ref[ex, :, KV_W:] = carry[ex, :, KV_W:] + dv_tile[ex][:TIME_BLOCK]

            for ex in range(batch):
                carry[ex, :, :KV_W] = dk_tile[ex][TIME_BLOCK:]
                carry[ex, :, KV_W:] = dv_tile[ex][TIME_BLOCK:]

        @pl.when(j < n_blocks)
        def _():
            dsink_vals = {head: jnp.zeros((1, 1), F32) for head in range(N_HEADS)}
            all_chains(dsink_vals)
            rowid = lax.broadcasted_iota(jnp.int32, (N_HEADS, LANES), 0)
            upd = jnp.zeros((N_HEADS, LANES), F32)
            for head, val in dsink_vals.items():
                upd = upd + jnp.where(rowid == head, val, 0.0)
            dsink_ref[...] += upd

        @pl.when(j == n_blocks)
        def _():
            dkv_ref[...] = carry[...]

    last = n_blocks - 1

    def blk(width):
        return pl.BlockSpec((batch, TIME_BLOCK, width), lambda j: (0, jnp.minimum(j, last), 0))

    prev = pl.BlockSpec((batch, TIME_BLOCK, 2 * KV_W), lambda j: (0, jnp.maximum(jnp.minimum(j, last) - 1, 0), 0))
    dkv_spec = pl.BlockSpec((batch, TIME_BLOCK, 2 * KV_W), lambda j: (0, jnp.maximum(j - 1, 0), 0))
    kv3 = _by_example(kv, batch)
    dq, dkv, dsink = pl.pallas_call(
        body, name="attn_bwd", grid=(n_blocks + 1,),
        in_specs=[pl.BlockSpec(memory_space=pltpu.SMEM), blk(ATTN_W), blk(2 * KV_W), prev, blk(ATTN_W), blk(ATTN_W),
                  blk(N_HEADS)],
        out_specs=[blk(ATTN_W), dkv_spec, pl.BlockSpec((N_HEADS, LANES), lambda j: (0, 0))],
        out_shape=[_sds((batch, rows // batch, ATTN_W), F32), _sds((batch, rows // batch, 2 * KV_W), F32),
                   _sds((N_HEADS, LANES), F32)],
        scratch_shapes=[pltpu.VMEM((batch, TIME_BLOCK, 2 * KV_W), F32)],
        compiler_params=_params("arbitrary"),
    )(sinks, _by_example(q, batch), kv3, kv3, _by_example(out, batch), _by_example(d_out, batch), _by_example(lse, batch))
    return dq.reshape(rows, ATTN_W), dkv.reshape(rows, 2 * KV_W), dsink


def _conv_taps(xb, prev8):
    ext = jnp.concatenate([prev8, xb], axis=0)
    n = ext.shape[0]
    return [xb] + [pltpu.roll(ext, k, 0)[SUBLANES:n] for k in range(1, CONV_TAPS)]


def _lru_gates(xc, wa, ba, wx, bx, lam):
    xcb = xc.astype(BF16)
    r = _sigmoid(_dot(xcb, wa) + ba)
    i = _sigmoid(_dot(xcb, wx) + bx)
    sp = jnp.maximum(-lam, 0.0) + _log1p(jnp.exp(-jnp.abs(lam)))
    log_a = -LRU_C * r * sp
    a = jnp.exp(log_a)
    mult = jnp.sqrt(_one_minus_square(a, log_a))
    return xcb, r, i, sp, a, mult


def _scan_fwd(a, u, h_before):
    n, d = a.shape
    groups = n // SUBLANES
    per_seq = groups // len(h_before)
    a = a.reshape(groups, SUBLANES, d)
    u = u.reshape(groups, SUBLANES, d)
    sub = lax.broadcasted_iota(jnp.int32, a.shape, 1)
    s = 1
    while s < SUBLANES:
        valid = sub >= s
        u = jnp.where(valid, u + a * pltpu.roll(u, s, 1), u)
        a = jnp.where(valid, a * pltpu.roll(a, s, 1), a)
        s *= 2
    out, prev = [None] * groups, list(h_before)
    for g in range(per_seq):
        for k in range(len(h_before)):
            at = k * per_seq + g
            out[at] = u[at] + a[at] * prev[k]
            prev[k] = out[at][SUBLANES - 1:SUBLANES, :]
    return jnp.concatenate(out, axis=0)


def _scan_rev(cf, g, d_after):
    n, d = g.shape
    groups = n // SUBLANES
    per_seq = groups // len(d_after)
    cf = cf.reshape(groups, SUBLANES, d)
    g = g.reshape(groups, SUBLANES, d)
    sub = lax.broadcasted_iota(jnp.int32, g.shape, 1)
    s = 1
    while s < SUBLANES:
        valid = sub + s < SUBLANES
        g = jnp.where(valid, g + cf * pltpu.roll(g, SUBLANES - s, 1), g)
        cf = jnp.where(valid, cf * pltpu.roll(cf, SUBLANES - s, 1), cf)
        s *= 2
    out, nxt = [None] * groups, list(d_after)
    for i in reversed(range(per_seq)):
        for k in range(len(d_after)):
            at = k * per_seq + i
            out[at] = g[at] + cf[at] * nxt[k]
            nxt[k] = out[at][0:1, :]
    return jnp.concatenate(out, axis=0)


def _lru_fwd(xg, conv_w, conv_b, wa, ba, wx, bx, lam, batch, n_blocks):
    rows = xg.shape[0]

    def body(xg_ref, cw_ref, cb_ref, wa_ref, ba_ref, wx_ref, bx_ref, lam_ref, hs_ref, lru_ref, x_prev, h_carry):
        @pl.when(pl.program_id(0) == 0)
        def _():
            x_prev[...] = jnp.zeros_like(x_prev)
            h_carry[...] = jnp.zeros_like(h_carry)

        xb = [xg_ref[ex, :, :LRU_W] for ex in range(batch)]
        taps = [_conv_taps(xb[ex], x_prev[ex]) for ex in range(batch)]
        taps = [jnp.concatenate([t[k] for t in taps], axis=0) for k in range(CONV_TAPS)]
        xc = cb_ref[...] + sum(cw_ref[CONV_TAPS - 1 - k:CONV_TAPS - k, :] * taps[k] for k in range(CONV_TAPS))
        _, _, i, _, a, mult = _lru_gates(xc, wa_ref[...], ba_ref[...], wx_ref[...], bx_ref[...], lam_ref[...])
        h = _scan_fwd(a, mult * i * xc, [h_carry[ex] for ex in range(batch)])
        gate = _gelu(jnp.concatenate([xg_ref[ex, :, LRU_W:] for ex in range(batch)], axis=0))
        lru = h * gate
        for ex in range(batch):
            rows_ex = slice(ex * TIME_BLOCK, (ex + 1) * TIME_BLOCK)
            hs_ref[ex] = h[rows_ex]
            lru_ref[ex] = lru[rows_ex]
            h_carry[ex] = h[(ex + 1) * TIME_BLOCK - 1:(ex + 1) * TIME_BLOCK, :]
            x_prev[ex] = xb[ex][TIME_BLOCK - SUBLANES:TIME_BLOCK, :]

    def blk(width):
        return pl.BlockSpec((batch, TIME_BLOCK, width), lambda j: (0, j, 0))

    small = [conv_w, conv_b, wa, ba, wx, bx, lam]
    hs, lru = pl.pallas_call(
        body, name="lru_fwd", grid=(n_blocks,),
        in_specs=[blk(2 * LRU_W)] + [_whole_spec(a) for a in small],
        out_specs=[blk(LRU_W), blk(LRU_W)],
        out_shape=[_sds((batch, rows // batch, LRU_W), F32)] * 2,
        scratch_shapes=[pltpu.VMEM((batch, SUBLANES, LRU_W), F32), pltpu.VMEM((batch, 1, LRU_W), F32)],
        compiler_params=_params("arbitrary"),
    )(_by_example(xg, batch), *small)
    return hs.reshape(rows, LRU_W), lru.reshape(rows, LRU_W)


def _lru_bwd(xg, hs, d_lru, conv_w, conv_b, wa, ba, wx, bx, lam, batch, n_blocks):
    rows = xg.shape[0]
    last_row = TIME_BLOCK - 1

    def body(xg_ref, xgh_ref, hs_ref, hsh_ref, dl_ref, cw_ref, cb_ref, wa_ref, ba_ref, wx_ref, bx_ref, lam_ref,
             dxg_ref, dwa_ref, dwx_ref, vec_ref, dh_carry, dxc_next):
        j = pl.program_id(0)

        @pl.when(j == 0)
        def _():
            dwa_ref[...] = jnp.zeros_like(dwa_ref)
            dwx_ref[...] = jnp.zeros_like(dwx_ref)
            vec_ref[...] = jnp.zeros_like(vec_ref)
            dh_carry[...] = jnp.zeros_like(dh_carry)
            dxc_next[...] = jnp.zeros_like(dxc_next)

        examples = range(batch)
        first = j == n_blocks - 1
        xb = [xg_ref[ex, :, :LRU_W] for ex in examples]
        taps = [_conv_taps(xb[ex], jnp.where(first, 0.0, xgh_ref[ex, :, :LRU_W])) for ex in examples]
        taps = [jnp.concatenate([t[k] for t in taps], axis=0) for k in range(CONV_TAPS)]
        cw = cw_ref[...]
        lam = lam_ref[...]
        wa = wa_ref[...]
        wx = wx_ref[...]
        xc = cb_ref[...] + sum(cw[CONV_TAPS - 1 - k:CONV_TAPS - k, :] * taps[k] for k in range(CONV_TAPS))
        xcb, r, i, sp, a, mult = _lru_gates(xc, wa, ba_ref[...], wx, bx_ref[...], lam)
        hs = jnp.concatenate([hs_ref[ex] for ex in examples], axis=0)
        row = jnp.bitwise_and(lax.broadcasted_iota(jnp.int32, hs.shape, 0), TIME_BLOCK - 1)
        h_before = jnp.concatenate([jnp.broadcast_to(jnp.where(first, 0.0, hsh_ref[ex, SUBLANES - 1:SUBLANES, :]),
                                                     (TIME_BLOCK, LRU_W)) for ex in examples], axis=0)
        h_prev = jnp.where(row == 0, h_before, pltpu.roll(hs, 1, 0))
        dl = jnp.concatenate([dl_ref[ex] for ex in examples], axis=0)
        gate, dgate = _gelu_and_grad(jnp.concatenate([xg_ref[ex, :, LRU_W:] for ex in examples], axis=0))
        dgb = (dl * hs * dgate).astype(BF16)
        cf = jnp.where(row == last_row, 1.0, pltpu.roll(a, batch * TIME_BLOCK - 1, 0))
        dh = _scan_rev(cf, dl * gate, [dh_carry[ex] for ex in examples])
        dmult = dh * i * xc
        di = dh * mult * xc
        dxc = dh * mult * i
        dlog_a = dh * h_prev * a - dmult * (a * a / mult)
        dr = dlog_a * (-LRU_C * sp)
        dlam = jnp.sum(dlog_a * (-LRU_C * r), axis=0, keepdims=True) * (-_sigmoid(-lam))
        dpr = dr * r * (1.0 - r)
        dpi = di * i * (1.0 - i)
        dprb = dpr.astype(BF16)
        dpib = dpi.astype(BF16)
        dxc = dxc + _dot_nt(dprb, wa) + _dot_nt(dpib, wx)
        dwa_ref[...] += _dot_tn(xcb, dprb)
        dwx_ref[...] += _dot_tn(xcb, dpib)
        n = TIME_BLOCK + SUBLANES
        later = []
        for k in range(1, CONV_TAPS):
            shifted = [pltpu.roll(jnp.concatenate([dxc[ex * TIME_BLOCK:(ex + 1) * TIME_BLOCK], dxc_next[ex]], axis=0),
                                  n - k, 0)[:TIME_BLOCK] for ex in examples]
            later.append(jnp.concatenate(shifted, axis=0))
        dxb = cw[CONV_TAPS - 1:CONV_TAPS, :] * dxc
        for k in range(1, CONV_TAPS):
            dxb = dxb + cw[CONV_TAPS - 1 - k:CONV_TAPS - k, :] * later[k - 1]
        dxb = dxb.astype(BF16)
        for ex in examples:
            rows_ex = slice(ex * TIME_BLOCK, (ex + 1) * TIME_BLOCK)
            dxg_ref[ex, :, :LRU_W] = dxb[rows_ex]
            dxg_ref[ex, :, LRU_W:] = dgb[rows_ex]
            dh_carry[ex] = a[ex * TIME_BLOCK:ex * TIME_BLOCK + 1, :] * dh[ex * TIME_BLOCK:ex * TIME_BLOCK + 1, :]
            dxc_next[ex] = dxc[ex * TIME_BLOCK:ex * TIME_BLOCK + SUBLANES, :]
        vecs = [jnp.sum(dxc * taps[CONV_TAPS - 1 - t], axis=0, keepdims=True) for t in range(CONV_TAPS)]
        vecs += [jnp.sum(dxc, axis=0, keepdims=True), jnp.sum(dpr, axis=0, keepdims=True),
                 jnp.sum(dpi, axis=0, keepdims=True), dlam]
        vec_ref[...] += _put_rows(SUBLANES, vecs)

    def tblk(j):
        return n_blocks - 1 - j

    def blk(width):
        return pl.BlockSpec((batch, TIME_BLOCK, width), lambda j: (0, tblk(j), 0))

    per8 = TIME_BLOCK // SUBLANES

    def halo(width):
        return pl.BlockSpec((batch, SUBLANES, width), lambda j: (0, jnp.maximum(per8 * tblk(j) - 1, 0), 0))

    small = [conv_w, conv_b, wa, ba, wx, bx, lam]
    acc = lambda shape: pl.BlockSpec(shape, lambda j: (0, 0))
    xg3, hs3 = _by_example(xg, batch), _by_example(hs, batch)
    dxg, dwa, dwx, vec = pl.pallas_call(
        body, name="lru_bwd", grid=(n_blocks,),
        in_specs=[blk(2 * LRU_W), halo(2 * LRU_W), blk(LRU_W), halo(LRU_W), blk(LRU_W)] + [_whole_spec(a) for a in small],
        out_specs=[blk(2 * LRU_W), acc((LRU_W, LRU_W)), acc((LRU_W, LRU_W)), acc((SUBLANES, LRU_W))],
        out_shape=[_sds((batch, rows // batch, 2 * LRU_W), BF16), _sds((LRU_W, LRU_W), F32), _sds((LRU_W, LRU_W), F32),
                   _sds((SUBLANES, LRU_W), F32)],
        scratch_shapes=[pltpu.VMEM((batch, 1, LRU_W), F32), pltpu.VMEM((batch, SUBLANES, LRU_W), F32)],
        compiler_params=_params("arbitrary"),
    )(xg3, xg3, hs3, hs3, _by_example(d_lru, batch), *small)
    return dxg.reshape(rows, 2 * LRU_W), dwa, dwx, vec


def _out_mlp_fwd(attn, lru, h, w_out, w_up4, w_down, g_attn, g_lru, g_post_mix, g_pre_mlp, g_post_mlp):
    rows = h.shape[0]

    def body(at_ref, lr_ref, h_ref, wo_ref, wu_ref, wd_ref, ga_ref, gl_ref, gp_ref, g1_ref, g2_ref,
             grp_ref, o_ref, hm_ref, z_ref, slope_ref, act_ref, y_ref, ho_ref):
        a = _rms(at_ref[...], ga_ref[...]).astype(BF16)
        l = _rms(lr_ref[...], gl_ref[...]).astype(BF16)
        grp_ref[:, :ATTN_W] = a
        grp_ref[:, ATTN_W:] = l
        o = _dot(a, wo_ref[:ATTN_W, :]) + _dot(l, wo_ref[ATTN_W:, :])
        o_ref[...] = o
        x = h_ref[...] + _rms(o, gp_ref[...])
        hm_ref[...] = x
        z = _rms(x, g1_ref[...]).astype(BF16)
        z_ref[...] = z
        y = jnp.zeros((ROW_TILE, D_MODEL), F32)
        for s in range(N_CHIPS):
            cols = slice(s * D_MODEL, (s + 1) * D_MODEL)
            r = jnp.maximum(_dot(z, wu_ref[s]), 0.0)
            slope_ref[:, cols] = (2.0 * r).astype(BF16)
            act = jnp.square(r).astype(BF16)
            act_ref[:, cols] = act
            y = y + _dot(act, wd_ref[cols, :])
        y_ref[...] = y
        ho_ref[...] = x + _rms(y, g2_ref[...])

    gains = [g_attn, g_lru, g_post_mix, g_pre_mlp, g_post_mlp]
    return pl.pallas_call(
        body, name="out_mlp_fwd", grid=(rows // ROW_TILE,),
        in_specs=[_row_spec(ATTN_W), _row_spec(LRU_W), _row_spec(D_MODEL), _resident_spec(w_out), _resident_spec(w_up4),
                  _resident_spec(w_down)] + [_whole_spec(g) for g in gains],
        out_specs=[_row_spec(D_MODEL), _row_spec(D_MODEL), _row_spec(D_MODEL), _row_spec(D_MODEL), _row_spec(D_FF),
                   _row_spec(D_FF), _row_spec(D_MODEL), _row_spec(D_MODEL)],
        out_shape=[_sds((rows, D_MODEL), BF16), _sds((rows, D_MODEL), F32), _sds((rows, D_MODEL), F32),
                   _sds((rows, D_MODEL), BF16), _sds((rows, D_FF), BF16), _sds((rows, D_FF), BF16),
                   _sds((rows, D_MODEL), F32), _sds((rows, D_MODEL), F32)],
        compiler_params=_params("parallel"),
    )(attn, lru, h, w_out, w_up4, w_down, *gains)


def _mlp_out_bwd(dh_out, h_mid, y, slope, o, attn, lru, w_out, w_up4, w_down, g_attn, g_lru, g_post_mix, g_pre_mlp,
                 g_post_mlp):
    rows = y.shape[0]

    def body(dh_ref, hm_ref, y_ref, slope_ref, o_ref, at_ref, lr_ref, wo_ref, wu_ref, wd_ref, ga_ref, gl_ref, gp_ref,
             g1_ref, g2_ref, dhm_ref, dy_ref, dup_ref, do_ref, dat_ref, dlr_ref, gacc_ref):
        @pl.when(pl.program_id(0) == 0)
        def _():
            gacc_ref[...] = jnp.zeros_like(gacc_ref)

        dh = dh_ref[...]
        dy, dg2 = _rms_bwd(dh, y_ref[...], g2_ref[...])
        dyb = dy.astype(BF16)
        dy_ref[...] = dyb
        dz = jnp.zeros((ROW_TILE, D_MODEL), F32)
        for s in range(N_CHIPS):
            cols = slice(s * D_MODEL, (s + 1) * D_MODEL)
            dact = _dot_nt(dyb, wd_ref[cols, :])
            dup = (dact * slope_ref[:, cols].astype(F32)).astype(BF16)
            dup_ref[:, cols] = dup
            dz = dz + _dot_nt(dup, wu_ref[s])
        dx, dg1 = _rms_bwd(dz, hm_ref[...], g1_ref[...])
        dhm = dh + dx
        dhm_ref[...] = dhm
        do, dgp = _rms_bwd(dhm, o_ref[...], gp_ref[...])
        dob = do.astype(BF16)
        do_ref[...] = dob
        dat, dga = _rms_bwd(_dot_nt(dob, wo_ref[:ATTN_W, :]), at_ref[...], ga_ref[...])
        dlr, dgl = _rms_bwd(_dot_nt(dob, wo_ref[ATTN_W:, :]), lr_ref[...], gl_ref[...])
        dat_ref[...] = dat
        dlr_ref[...] = dlr
        gacc_ref[...] += _put_rows(SUBLANES, [dg1, dg2, dgp, jnp.concatenate([dga, dgl], axis=1)])

    gains = [g_attn, g_lru, g_post_mix, g_pre_mlp, g_post_mlp]
    return pl.pallas_call(
        body, name="mlp_out_bwd", grid=(rows // ROW_TILE,),
        in_specs=[_row_spec(D_MODEL), _row_spec(D_MODEL), _row_spec(D_MODEL), _row_spec(D_FF), _row_spec(D_MODEL),
                  _row_spec(ATTN_W), _row_spec(LRU_W), _resident_spec(w_out), _resident_spec(w_up4), _resident_spec(w_down)]
                 + [_whole_spec(g) for g in gains],
        out_specs=[_row_spec(D_MODEL), _row_spec(D_MODEL), _row_spec(D_FF), _row_spec(D_MODEL), _row_spec(ATTN_W),
                   _row_spec(LRU_W), pl.BlockSpec((SUBLANES, D_MODEL), lambda i: (0, 0))],
        out_shape=[_sds((rows, D_MODEL), F32), _sds((rows, D_MODEL), BF16), _sds((rows, D_FF), BF16),
                   _sds((rows, D_MODEL), BF16), _sds((rows, ATTN_W), F32), _sds((rows, LRU_W), F32),
                   _sds((SUBLANES, D_MODEL), F32)],
        compiler_params=_params("arbitrary"),
    )(dh_out, h_mid, y, slope, o, attn, lru, w_out, w_up4, w_down, *gains)


def _matmul_tn(a, b, tm, tn, name, out_dtype, column_blocks=False):
    rows, m = a.shape
    n = b.shape[1]

    def body(a_ref, b_ref, o_ref):
        o_ref[...] = _dot_tn(a_ref[...], b_ref[...]).astype(out_dtype)

    if column_blocks:
        out_spec = pl.BlockSpec((None, tm, tn), lambda i, j: (j, i, 0))
        out_shape = _sds((n // tn, m, tn), out_dtype)
    else:
        out_spec = pl.BlockSpec((tm, tn), lambda i, j: (i, j))
        out_shape = _sds((m, n), out_dtype)
    return pl.pallas_call(
        body, name=name, grid=(m // tm, n // tn),
        in_specs=[pl.BlockSpec((rows, tm), lambda i, j: (0, i)), pl.BlockSpec((rows, tn), lambda i, j: (0, j))],
        out_specs=out_spec, out_shape=out_shape,
        compiler_params=_params("parallel", "parallel"),
    )(a, b)


LOSS_COLS = 256


def _loss_head(h, loss_target):
    batch, per_example, _ = h.shape
    seq = loss_target.shape[1]
    n_real = N_META + seq
    assert seq % SUBLANES == 0

    def body(h_ref, t_ref, dh_ref, l_ref):
        @pl.when((pl.program_id(0) == 0) & (pl.program_id(1) == 0))
        def _():
            l_ref[...] = jnp.zeros_like(l_ref)

        d = h_ref[N_META:n_real, :] - t_ref[...]
        dh_ref[:N_META, :] = jnp.zeros((N_META, LOSS_COLS), F32)
        dh_ref[N_META:n_real, :] = d * (1.0 / D_MODEL)
        dh_ref[n_real:, :] = jnp.zeros((per_example - n_real, LOSS_COLS), F32)
        l_ref[...] += jnp.sum(jnp.sum(d * d, axis=0, keepdims=True), axis=1, keepdims=True)

    blk = pl.BlockSpec((None, per_example, LOSS_COLS), lambda b, j: (b, 0, j))
    return pl.pallas_call(
        body, name="loss_head", grid=(batch, D_MODEL // LOSS_COLS),
        in_specs=[blk, pl.BlockSpec((None, seq, LOSS_COLS), lambda b, j: (b, 0, j))],
        out_specs=[blk, pl.BlockSpec((SUBLANES, LANES), lambda b, j: (0, 0))],
        out_shape=[_sds(h.shape, F32), _sds((SUBLANES, LANES), F32)],
        compiler_params=_params("arbitrary", "arbitrary"),
    )(h, loss_target)


def _meta_grad(dh0, batch, rows_per_example):
    per = rows_per_example // N_META

    def body(d_ref, o_ref):
        @pl.when(pl.program_id(0) == 0)
        def _():
            o_ref[...] = jnp.zeros_like(o_ref)

        o_ref[...] += d_ref[...]

    return pl.pallas_call(
        body, name="meta_grad", grid=(batch,),
        in_specs=[pl.BlockSpec((N_META, D_MODEL), lambda b: (b * per, 0))],
        out_specs=pl.BlockSpec((N_META, D_MODEL), lambda b: (0, 0)),
        out_shape=_sds((N_META, D_MODEL), F32),
        compiler_params=_params("arbitrary"),
    )(dh0)


def _elementwise_tile(rows, cols):
    tile = rows
    while tile * cols * 4 > (1 << 20) and tile % 16 == 0:
        tile //= 2
    return tile


def _sum_slots(buf, name):
    k, rows, cols = buf.shape
    tile = _elementwise_tile(rows, cols)

    def body(*refs):
        total = refs[0][...].astype(F32)
        for r in refs[1:k]:
            total = total + r[...].astype(F32)
        refs[k][...] = total

    def slot(s):
        return pl.BlockSpec((None, tile, cols), lambda i: (s, i, 0))

    return pl.pallas_call(
        body, name=name, grid=(rows // tile,),
        in_specs=[slot(s) for s in range(k)], out_specs=pl.BlockSpec((tile, cols), lambda i: (i, 0)),
        out_shape=_sds((rows, cols), F32), compiler_params=_params("parallel"),
    )(*([buf] * k))


def _sum_pieces(pieces, landed, chip, layer, n_layers, stacked):
    count = len(pieces)
    _, rows, cols = pieces[0].shape
    tile = _elementwise_tile(rows, cols)

    def body(chip_ref, *refs):
        token = refs[-1]
        for t in range(count):
            own_ref, a_ref, b_ref, c_ref = refs[4 * t:4 * t + 4]
            total = ((own_ref[...].astype(F32) + a_ref[...].astype(F32)) + b_ref[...].astype(F32)) + c_ref[...].astype(F32)
            refs[-1 - count + t][...] = total.astype(BF16)
        token[...] = jnp.zeros_like(token)

    def slot(offset):
        return pl.BlockSpec((None, tile, cols), lambda i, chip_ref: ((chip_ref[0] + offset) % N_CHIPS, i, 0))

    carried = [] if stacked is None else list(stacked)
    operands = []
    for p, l in zip(pieces, landed):
        operands += [p, l, l, l]
    grid_spec = pltpu.PrefetchScalarGridSpec(
        num_scalar_prefetch=1, grid=(rows // tile,),
        in_specs=[slot(0), slot(1), slot(2), slot(3)] * count + [pl.BlockSpec(memory_space=pl.ANY)] * len(carried),
        out_specs=[pl.BlockSpec((None, tile, cols), lambda i, chip_ref: (layer, i, 0))] * count
                  + [pl.BlockSpec((SUBLANES, LANES), lambda i, chip_ref: (0, 0))])
    out = pl.pallas_call(
        body, name="sum_grad_pieces", grid_spec=grid_spec,
        out_shape=[_sds((n_layers, rows, cols), BF16)] * count + [_sds((SUBLANES, LANES), F32)],
        input_output_aliases={1 + 4 * count + t: t for t in range(len(carried))},
        compiler_params=_params("arbitrary"),
    )(chip, *operands, *carried)
    return out[:count], out[count]


def _adamw(w, m, v, grads, name):
    flat = w.ndim == 2
    if flat:
        w, m, v = w[None], m[None], v[None]
        grads = [g[None] for g in grads]
    layers, rows, cols = w.shape
    tile = _elementwise_tile(rows, cols)
    ng = len(grads)
    m_scale = 1.0 - ADAM_B1 ** ADAM_STEP
    v_scale = 1.0 - ADAM_B2 ** ADAM_STEP

    def body(*refs):
        w_ref, m_ref, v_ref = refs[:3]
        g_refs = refs[3:3 + ng]
        g_out, d_out, m_out, v_out = refs[-4:]
        g = g_refs[0][...].astype(F32)
        for r in g_refs[1:]:
            g = g + r[...].astype(F32)
        m_new = ADAM_B1 * m_ref[...] + (1.0 - ADAM_B1) * g
        v_new = ADAM_B2 * v_ref[...] + (1.0 - ADAM_B2) * (g * g)
        m_hat = m_new / m_scale
        v_hat = v_new / v_scale
        g_out[...] = g
        d_out[...] = -ADAM_LR * (m_hat / (jnp.sqrt(v_hat) + ADAM_EPS) + ADAM_WD * w_ref[...])
        m_out[...] = m_new
        v_out[...] = v_new

    spec = pl.BlockSpec((None, tile, cols), lambda l, i: (l, i, 0))
    out = pl.pallas_call(
        body, name=name, grid=(layers, rows // tile),
        in_specs=[spec] * (3 + ng), out_specs=[spec] * 4, out_shape=[_sds(w.shape, F32)] * 4,
        compiler_params=_params("parallel", "parallel"),
    )(w, m, v, *grads)
    return [o[0] for o in out] if flat else out


def _position():
    return lax.axis_index("x"), lax.axis_index("y"), lax.axis_index("c")


def _other_chips(x, y):
    return [(1 - x, y), (x, 1 - y), (1 - x, 1 - y)]


def _exchange_chips(arrays, name, scatter):
    n = len(arrays)

    def body(*refs):
        src, dst = refs[:n], refs[n:2 * n]
        send_sems, recv_sems, local_sems = refs[2 * n:]
        x, y, c = _position()
        mine = 2 * x + y
        copies = []
        for i in range(n):
            own = src[i].at[mine] if scatter else src[i]
            copies.append(pltpu.make_async_copy(own, dst[i].at[mine], local_sems.at[i]))
        for k, (px, py) in enumerate(_other_chips(x, y)):
            for i in range(n):
                piece = src[i].at[2 * px + py] if scatter else src[i]
                copies.append(pltpu.make_async_remote_copy(
                    src_ref=piece, dst_ref=dst[i].at[mine], send_sem=send_sems.at[k, i], recv_sem=recv_sems.at[k, i],
                    device_id=(px, py, c), device_id_type=MESH))
        for cp in copies:
            cp.start()
        for cp in copies:
            cp.wait()

    def out_shape(a):
        return _sds(a.shape if scatter else (N_CHIPS,) + a.shape, a.dtype)

    return pl.pallas_call(
        body, name=name, in_specs=[HBM_SPEC] * n, out_specs=[HBM_SPEC] * n, out_shape=[out_shape(a) for a in arrays],
        scratch_shapes=[pltpu.SemaphoreType.DMA((N_CHIPS - 1, n)), pltpu.SemaphoreType.DMA((N_CHIPS - 1, n)),
                        pltpu.SemaphoreType.DMA((n,))],
    )(*arrays)


def _swap_sibling(arrays, name):
    n = len(arrays)

    def body(*refs):
        src, dst = refs[:n], refs[n:2 * n]
        send_sems, recv_sems = refs[2 * n:]
        x, y, c = _position()
        copies = [pltpu.make_async_remote_copy(
            src_ref=src[i], dst_ref=dst[i], send_sem=send_sems.at[i], recv_sem=recv_sems.at[i],
            device_id=(x, y, 1 - c), device_id_type=MESH) for i in range(n)]
        for cp in copies:
            cp.start()
        for cp in copies:
            cp.wait()

    return pl.pallas_call(
        body, name=name, in_specs=[HBM_SPEC] * n, out_specs=[HBM_SPEC] * n,
        out_shape=[_sds(a.shape, a.dtype) for a in arrays],
        scratch_shapes=[pltpu.SemaphoreType.DMA((n,)), pltpu.SemaphoreType.DMA((n,))],
    )(*arrays)


SEM_SPEC = pl.BlockSpec(memory_space=pltpu.SEMAPHORE)
ANY_SPEC = pl.BlockSpec(memory_space=pl.ANY)
IN_FLIGHT = pltpu.SideEffectType.DATAFLOW_SIDE_EFFECTING


def _peer_sems(all_devices):
    return pltpu.SemaphoreType.DMA(((N_DEV if all_devices else N_CHIPS) - 1,))


def _own_slot(all_devices, x, y, c):
    return 4 * x + 2 * y + c if all_devices else 2 * x + y


def _peers(all_devices, x, y, c):
    if not all_devices:
        return [((px, py, c), 2 * px + py) for px, py in _other_chips(x, y)]
    out = []
    for fx in range(2):
        for fy in range(2):
            for fc in range(2):
                if fx or fy or fc:
                    px, py, pc = (1 - x if fx else x), (1 - y if fy else y), (1 - c if fc else c)
                    out.append(((px, py, pc), 4 * px + 2 * py + pc))
    return out


def _in_hbm(a):
    return pltpu.with_memory_space_constraint(a, pltpu.HBM)


def _place_slot(a, slot, n_slots, name):
    rows, cols = a.shape
    tile = _elementwise_tile(rows, cols)

    def body(slot_ref, a_ref, o_ref):
        o_ref[...] = a_ref[...]

    grid_spec = pltpu.PrefetchScalarGridSpec(
        num_scalar_prefetch=1, grid=(rows // tile,),
        in_specs=[pl.BlockSpec((tile, cols), lambda i, slot_ref: (i, 0))],
        out_specs=pl.BlockSpec((None, tile, cols), lambda i, slot_ref: (slot_ref[0], i, 0)))
    return pl.pallas_call(
        body, name=name, grid_spec=grid_spec, out_shape=_sds((n_slots, rows, cols), a.dtype),
        compiler_params=_params("parallel"),
    )(slot, a)


def _place_shard(ws, layer, chip, name):
    count = len(ws)
    _, rows, cols = ws[0].shape
    tile = _elementwise_tile(rows, cols)

    def body(chip_ref, *refs):
        for t in range(count):
            refs[count + t][...] = refs[t][...].astype(BF16)

    grid_spec = pltpu.PrefetchScalarGridSpec(
        num_scalar_prefetch=1, grid=(rows // tile,),
        in_specs=[pl.BlockSpec((None, tile, cols), lambda i, chip_ref: (layer, i, 0))] * count,
        out_specs=[pl.BlockSpec((None, tile, cols), lambda i, chip_ref: (chip_ref[0], i, 0))] * count)
    return pl.pallas_call(
        body, name=name, grid_spec=grid_spec, out_shape=[_sds((N_CHIPS, rows, cols), BF16)] * count,
        compiler_params=_params("parallel"),
    )(chip, *ws)


def _rows_of(ref, slot, core, halves):
    if not halves:
        return ref.at[slot]
    half = ref.shape[1] // 2
    return ref.at[slot, pl.ds(pl.multiple_of(core * half, half), half)]


def _gather_start(bufs, after, name, all_devices=False, halves=False):
    n = len(bufs)

    def body(*refs):
        buf = refs[:n]
        send, recv = refs[n + 1:2 * n + 1], refs[2 * n + 1:3 * n + 1]
        token = refs[4 * n + 1]
        x, y, c = _position()
        mine = _own_slot(all_devices, x, y, c)
        for i in range(n):
            for k, (peer, _) in enumerate(_peers(all_devices, x, y, c)):
                rows = _rows_of(buf[i], mine, c, halves)
                pltpu.make_async_remote_copy(
                    src_ref=rows, dst_ref=rows, send_sem=send[i].at[k], recv_sem=recv[i].at[k],
                    device_id=peer, device_id_type=MESH).start()
        token[...] = jnp.zeros_like(token)

    sems = _peer_sems(all_devices)
    out = pl.pallas_call(
        body, name=name, in_specs=[HBM_SPEC] * n + [ANY_SPEC],
        out_specs=[SEM_SPEC] * (2 * n) + [HBM_SPEC] * n + [pl.BlockSpec(memory_space=pltpu.VMEM)],
        out_shape=[sems] * (2 * n) + [pltpu.HBM(b.shape, b.dtype) for b in bufs] + [_sds((SUBLANES, LANES), F32)],
        input_output_aliases={i: 2 * n + i for i in range(n)},
        compiler_params=pltpu.CompilerParams(has_side_effects=IN_FLIGHT),
    )(*[_in_hbm(b) for b in bufs], after)
    return [(out[i], out[n + i], out[2 * n + i]) for i in range(n)], out[3 * n]


def _gather_relay(flights, after, name):
    n = len(flights)

    def body(*refs):
        buf, send, recv = refs[:n], refs[n:2 * n], refs[2 * n:3 * n]
        send2, recv2 = refs[3 * n + 1:4 * n + 1], refs[4 * n + 1:5 * n + 1]
        token = refs[6 * n + 1]
        x, y, c = _position()
        mine = 2 * x + y
        for i in range(n):
            for k, (peer, slot) in enumerate(_peers(False, x, y, c)):
                arrived = _rows_of(buf[i], slot, c, True)
                cp = pltpu.make_async_remote_copy(
                    src_ref=_rows_of(buf[i], mine, c, True), dst_ref=arrived, send_sem=send[i].at[k], recv_sem=recv[i].at[k],
                    device_id=peer, device_id_type=MESH)
                cp.wait_send()
                cp.wait_recv()
                pltpu.make_async_remote_copy(
                    src_ref=arrived, dst_ref=arrived, send_sem=send2[i].at[k], recv_sem=recv2[i].at[k],
                    device_id=(x, y, 1 - c), device_id_type=MESH).start()
        token[...] = jnp.zeros_like(token)

    bufs = [f[2] for f in flights]
    sems = _peer_sems(False)
    out = pl.pallas_call(
        body, name=name, in_specs=[HBM_SPEC] * n + [SEM_SPEC] * (2 * n) + [ANY_SPEC],
        out_specs=[SEM_SPEC] * (2 * n) + [HBM_SPEC] * n + [pl.BlockSpec(memory_space=pltpu.VMEM)],
        out_shape=[sems] * (2 * n) + [pltpu.HBM(b.shape, b.dtype) for b in bufs] + [_sds((SUBLANES, LANES), F32)],
        input_output_aliases={i: 2 * n + i for i in range(n)},
        compiler_params=pltpu.CompilerParams(has_side_effects=IN_FLIGHT),
    )(*bufs, *[f[0] for f in flights], *[f[1] for f in flights], after)
    return [(out[i], out[n + i], out[2 * n + i]) for i in range(n)], out[3 * n]


def _gather_wait(send, recv, buf, after, name, all_devices=False, relayed=False):
    def body(buf_ref, send_ref, recv_ref, after_ref, out_ref):
        x, y, c = _position()
        mine = _own_slot(all_devices, x, y, c)
        for k, (peer, slot) in enumerate(_peers(all_devices, x, y, c)):
            if relayed:
                cp = pltpu.make_async_remote_copy(
                    src_ref=_rows_of(buf_ref, slot, c, True), dst_ref=_rows_of(buf_ref, slot, 1 - c, True),
                    send_sem=send_ref.at[k], recv_sem=recv_ref.at[k], device_id=(x, y, 1 - c), device_id_type=MESH)
            else:
                cp = pltpu.make_async_remote_copy(
                    src_ref=buf_ref.at[mine], dst_ref=buf_ref.at[slot], send_sem=send_ref.at[k], recv_sem=recv_ref.at[k],
                    device_id=peer, device_id_type=MESH)
            cp.wait_send()
            cp.wait_recv()

    return pl.pallas_call(
        body, name=name, in_specs=[HBM_SPEC, SEM_SPEC, SEM_SPEC, ANY_SPEC], out_specs=HBM_SPEC,
        out_shape=pltpu.HBM(buf.shape, buf.dtype), input_output_aliases={0: 0},
        compiler_params=pltpu.CompilerParams(has_side_effects=IN_FLIGHT),
    )(buf, send, recv, after)


def _scatter_start(pieces, name):
    n = len(pieces)

    def body(*refs):
        src = refs[:n]
        send, recv = refs[n:2 * n], refs[2 * n:3 * n]
        land = refs[4 * n:5 * n]
        token = refs[5 * n]
        x, y, c = _position()
        mine = 2 * x + y
        for i in range(n):
            for k, (px, py) in enumerate(_other_chips(x, y)):
                pltpu.make_async_remote_copy(
                    src_ref=src[i].at[2 * px + py], dst_ref=land[i].at[mine], send_sem=send[i].at[k], recv_sem=recv[i].at[k],
                    device_id=(px, py, c), device_id_type=MESH).start()
        token[...] = jnp.zeros_like(token)

    hbm = [pltpu.HBM(p.shape, p.dtype) for p in pieces]
    out = pl.pallas_call(
        body, name=name, in_specs=[HBM_SPEC] * n,
        out_specs=[SEM_SPEC] * (2 * n) + [HBM_SPEC] * (2 * n) + [pl.BlockSpec(memory_space=pltpu.VMEM)],
        out_shape=[_peer_sems(False)] * (2 * n) + hbm + hbm + [_sds((SUBLANES, LANES), F32)],
        input_output_aliases={i: 2 * n + i for i in range(n)},
        compiler_params=pltpu.CompilerParams(has_side_effects=IN_FLIGHT),
    )(*[_in_hbm(p) for p in pieces])
    return [(out[i], out[n + i], out[2 * n + i], out[3 * n + i]) for i in range(n)], out[4 * n]


def _scatter_wait(send, recv, pieces, land, after, name):
    def body(src_ref, land_ref, send_ref, recv_ref, after_ref, src_out, land_out):
        x, y, c = _position()
        for k, (px, py) in enumerate(_other_chips(x, y)):
            cp = pltpu.make_async_remote_copy(
                src_ref=src_ref.at[2 * px + py], dst_ref=land_ref.at[2 * px + py], send_sem=send_ref.at[k],
                recv_sem=recv_ref.at[k], device_id=(px, py, c), device_id_type=MESH)
            cp.wait_send()
            cp.wait_recv()

    return pl.pallas_call(
        body, name=name, in_specs=[HBM_SPEC, HBM_SPEC, SEM_SPEC, SEM_SPEC, ANY_SPEC], out_specs=[HBM_SPEC, HBM_SPEC],
        out_shape=[pltpu.HBM(pieces.shape, pieces.dtype), pltpu.HBM(land.shape, land.dtype)],
        input_output_aliases={0: 0, 1: 1},
        compiler_params=pltpu.CompilerParams(has_side_effects=IN_FLIGHT),
    )(pieces, land, send, recv, after)


def _rope_tables(batch, rows_per_example):
    inv_freq = ROPE_THETA ** (-jnp.arange(0, 64, 2, dtype=F32) / 64)
    ang = jnp.arange(rows_per_example, dtype=F32)[:, None] * inv_freq[None, :]
    cos, sin = jnp.cos(ang), jnp.sin(ang)
    cos128 = jnp.concatenate([cos, cos, cos, cos], axis=1)
    sin128 = jnp.concatenate([-sin, sin, -sin, sin], axis=1)
    return jnp.tile(cos128, (batch, 1)), jnp.tile(sin128, (batch, 1))


def _block_diagonal(w):
    eye = jnp.eye(LRU_BLOCKS, dtype=w.dtype)
    return (w[:, :, None, :] * eye[:, None, :, None]).reshape(LRU_W, LRU_W)


def _diagonal_blocks(dense):
    d4 = dense.reshape(LRU_BLOCKS, LRU_BLOCK, LRU_BLOCKS, LRU_BLOCK)
    return jnp.stack([d4[n, :, n, :] for n in range(LRU_BLOCKS)])


def _row(v):
    return v.reshape(1, -1)


def _local_step(x, loss_target, meta_tokens, small, depth, big_weight, on_layer_grads):
    batch, seq, _ = x.shape
    n_real = N_META + seq
    n_blocks = -(-n_real // TIME_BLOCK)
    per_example = n_blocks * TIME_BLOCK
    pad = per_example - n_real
    meta = jnp.broadcast_to(meta_tokens[None], (batch, N_META, D_MODEL))
    h = jnp.concatenate([meta, x, jnp.zeros((batch, pad, D_MODEL), F32)], axis=1).reshape(batch * per_example, D_MODEL)
    rope_cos, rope_sin = _rope_tables(batch, per_example)

    saved = []
    for l in range(depth):
        wa = _block_diagonal(small['gate_a_w'][l]).astype(BF16)
        wx = _block_diagonal(small['gate_x_w'][l]).astype(BF16)
        lru_small = (small['conv_w'][l], _row(small['conv_b'][l]), wa, _row(small['gate_a_b'][l]), wx,
                     _row(small['gate_x_b'][l]), _row(small['lru_lambda'][l]))
        w_in = big_weight('w_in', l, h)
        z1, q, kv, xg = _in_proj_fwd(h, _row(small['pre_mix_norm'][l]), w_in, rope_cos, rope_sin)
        attn, lse = _attn_fwd(q, kv, small['attn_sinks'][l], batch, n_blocks)
        hs, lru = _lru_fwd(xg, *lru_small, batch, n_blocks)
        w_out = big_weight('w_out', l, xg)
        w_up4 = big_weight('w_up', l, xg)
        w_down = big_weight('w_down', l, xg)
        gains = [_row(small[n][l]) for n in ('attn_group_norm', 'lru_group_norm', 'post_mix_norm', 'pre_mlp_norm',
                                             'post_mlp_norm')]
        groups, o, h_mid, z2, slope, act, y, h_out = _out_mlp_fwd(attn, lru, h, w_out, w_up4, w_down, *gains)
        saved.append(dict(h=h, z1=z1, q=q, kv=kv, xg=xg, attn=attn, lse=lse, hs=hs, lru=lru, groups=groups, o=o,
                          h_mid=h_mid, z2=z2, slope=slope, act=act, y=y, lru_small=lru_small, w_in=w_in, w_out=w_out,
                          w_up4=w_up4, w_down=w_down, gains=gains))
        h = h_out

    dh, sq_err = _loss_head(h.reshape(batch, per_example, D_MODEL), loss_target)
    dh = dh.reshape(batch * per_example, D_MODEL)

    gs = {n: [None] * depth for n in REPLICATED_NAMES + ('conv_w',)}
    handed_over = None
    for l in reversed(range(depth)):
        s = saved[l]

        def ordered_after(gain):
            return gain if handed_over is None else gain + handed_over[0, 0]

        gains = list(s['gains'])
        gains[4] = ordered_after(gains[4])
        dh_mid, dy, dup, do, d_attn, d_lru, g_rows = _mlp_out_bwd(dh, s['h_mid'], s['y'], s['slope'], s['o'], s['attn'],
                                                                s['lru'], s['w_out'], s['w_up4'], s['w_down'], *gains)
        handed_over = on_layer_grads(l, {
            'w_down': _matmul_tn(s['act'], dy, 512, D_MODEL, "grad_w_down", BF16),
            'w_up': _matmul_tn(s['z2'], dup, 512, D_MODEL, "grad_w_up", BF16, column_blocks=True)})
        lru_small = list(s['lru_small'])
        lru_small[1] = ordered_after(lru_small[1])
        dxg, dwa, dwx, g_lru = _lru_bwd(s['xg'], s['hs'], d_lru, *lru_small, batch, n_blocks)
        dq, dkv, dsink = _attn_bwd(s['q'], s['kv'], small['attn_sinks'][l], s['attn'], s['lse'], d_attn, batch, n_blocks)
        dh, dproj, g_in = _in_proj_bwd(dh_mid, s['h'], dq, dkv, dxg, rope_cos, rope_sin, _row(small['pre_mix_norm'][l]),
                                       s['w_in'])
        handed_over = on_layer_grads(l, {
            'w_out': _matmul_tn(s['groups'], do, 512, D_MODEL, "grad_w_out", BF16),
            'w_in': _matmul_tn(dproj, s['z1'], IN_COLS // 2, D_MODEL, "grad_w_in", BF16)})
        gs['pre_mlp_norm'][l], gs['post_mlp_norm'][l] = g_rows[0], g_rows[1]
        gs['post_mix_norm'][l] = g_rows[2]
        gs['attn_group_norm'][l], gs['lru_group_norm'][l] = g_rows[3, :ATTN_W], g_rows[3, ATTN_W:]
        gs['pre_mix_norm'][l] = g_in[0]
        gs['attn_sinks'][l] = dsink[:, 0]
        gs['conv_w'][l] = g_lru[:CONV_TAPS]
        gs['conv_b'][l], gs['gate_a_b'][l], gs['gate_x_b'][l], gs['lru_lambda'][l] = g_lru[4], g_lru[5], g_lru[6], g_lru[7]
        gs['gate_a_w'][l] = _diagonal_blocks(dwa)
        gs['gate_x_w'][l] = _diagonal_blocks(dwx)

    grad_x = dh.reshape(batch, per_example, D_MODEL)[:, N_META:n_real]
    grad_meta = _meta_grad(dh, batch, per_example)
    small_grads = {n: jnp.stack(v) for n, v in gs.items()}
    return sq_err, grad_x, grad_meta, small_grads, handed_over


PACK_UNIT = SUBLANES * LANES


def _size(shape):
    size = 1
    for d in shape:
        size *= d
    return size


def _pack(arrays):
    parts = []
    for a in arrays:
        flat = a.reshape(-1)
        padded = -(-flat.shape[0] // PACK_UNIT) * PACK_UNIT
        if padded != flat.shape[0]:
            flat = jnp.pad(flat, (0, padded - flat.shape[0]))
        parts.append(flat.reshape(-1, LANES))
    return jnp.concatenate(parts, axis=0)


def _unpack(buf, shapes):
    out, at = [], 0
    for shp in shapes:
        size = _size(shp)
        rows = -(-size // PACK_UNIT) * SUBLANES
        part = buf[at:at + rows]
        if rows * LANES != size:
            part = part.reshape(-1)[:size]
        out.append(part.reshape(shp))
        at += rows
    return out


def kernel(x, meta_tokens, pre_mix_norm, w_in, attn_sinks, conv_w, conv_b, gate_a_w, gate_a_b, gate_x_w, gate_x_b, lru_lambda, attn_group_norm, lru_group_norm, w_out, post_mix_norm, pre_mlp_norm, w_up, w_down, post_mlp_norm, loss_target, m_meta_tokens, m_pre_mix_norm, m_w_in, m_attn_sinks, m_conv_w, m_conv_b, m_gate_a_w, m_gate_a_b, m_gate_x_w, m_gate_x_b, m_lru_lambda, m_attn_group_norm, m_lru_group_norm, m_w_out, m_post_mix_norm, m_pre_mlp_norm, m_w_up, m_w_down, m_post_mlp_norm, v_meta_tokens, v_pre_mix_norm, v_w_in, v_attn_sinks, v_conv_w, v_conv_b, v_gate_a_w, v_gate_a_b, v_gate_x_w, v_gate_x_b, v_lru_lambda, v_attn_group_norm, v_lru_group_norm, v_w_out, v_post_mix_norm, v_pre_mlp_norm, v_w_up, v_w_down, v_post_mlp_norm):
    given = dict(locals())
    w = {n: given[n] for n in WEIGHT_NAMES}
    m = {n: given['m_' + n] for n in WEIGHT_NAMES}
    v = {n: given['v_' + n] for n in WEIGHT_NAMES}
    depth = w_in.shape[0]
    for d in (w, m, v):
        d['w_in'] = jnp.swapaxes(d['w_in'], 1, 2)
    chip = 2 * lax.axis_index("x") + lax.axis_index("y")
    chip1 = chip.reshape(1).astype(jnp.int32)

    in_flight = []
    all_started = chip1
    for l in range(depth):
        bufs = []
        for names in SAME_SHAPE:
            bufs += _place_shard([w[n] for n in names], l, chip1, "place_" + names[0])
        handles, all_started = _gather_start(bufs, all_started, "gather_start_%d" % l, halves=True)
        in_flight.append(dict(zip(BIG_NAMES, handles)))
    first_use = {'w_in': ('w_in',), 'w_out': ('w_out', 'w_up', 'w_down')}

    def big_weight(name, l, after):
        if l == 0 and name == 'w_in':
            after = all_started
        names = first_use.get(name, ()) if l == 0 else (BIG_NAMES if name == 'w_in' else ())
        if names:
            relayed, _ = _gather_relay([in_flight[l][n] for n in names], after, "gather_relay_%d_%s" % (l, name))
            in_flight[l].update(zip(names, relayed))
        send, recv, buf = in_flight[l][name]
        full = _gather_wait(send, recv, buf, after, "gather_wait_%d_%s" % (l, name), relayed=True)
        if name == 'w_in':
            return full.reshape(IN_COLS, D_MODEL)
        if name == 'w_out':
            return full.reshape(D_MODEL, D_MODEL)
        if name == 'w_down':
            return full.reshape(D_FF, D_MODEL)
        return full

    gathered = _exchange_chips([meta_tokens, conv_w], "gather_small_weights", scatter=False)
    full_meta = jnp.concatenate([gathered[0][s] for s in range(N_CHIPS)], axis=1)
    small = {n: w[n] for n in REPLICATED_NAMES}
    small['conv_w'] = jnp.concatenate([gathered[1][s] for s in range(N_CHIPS)], axis=2)

    scattering = [{} for _ in range(depth)]

    pending = {}

    def on_layer_grads(l, big):
        if l > 0 and 'w_in' not in big:
            pending.update(big)
            return None
        big = {**pending, **big}
        pending.clear()
        names = list(big)
        pieces = [big[n].reshape(N_CHIPS, -1, D_MODEL) for n in names]
        handles, started = _scatter_start(pieces, "scatter_start_%d_%s" % (l, names[0]))
        scattering[l].update(zip(names, handles))
        return started

    sq_err, grad_x, grad_meta, small_grads, all_handed_over = _local_step(x, loss_target, full_meta, small, depth, big_weight,
                                                                           on_layer_grads)
    loss = lax.psum(sq_err[0, 0] * (0.5 / D_MODEL), ("x", "y", "c"))

    small_names = list(REPLICATED_NAMES) + list(COLUMN_SHARDED_SMALL)
    small_full = dict(small_grads)
    small_full['meta_tokens'] = grad_meta
    device1 = (2 * chip + lax.axis_index("c")).reshape(1).astype(jnp.int32)
    packed = _place_slot(_pack([small_full[n] for n in small_names]), device1, N_DEV, "place_small_grads")
    [small_flight], done = _gather_start([packed], all_handed_over, "gather_small_grads_start", all_devices=True)

    partial = {n: None for n in BIG_NAMES}
    for l in reversed(range(depth)):
        for names in SAME_SHAPE:
            arrived = []
            for n in names:
                send, recv, pieces, land = scattering[l][n]
                arrived.append(_scatter_wait(send, recv, pieces, land, done, "scatter_wait_%d_%s" % (l, n)))
            stacked = None if partial[names[0]] is None else [partial[n] for n in names]
            sums, done = _sum_pieces([a[0] for a in arrived], [a[1] for a in arrived], chip1, l, depth, stacked)
            partial.update(zip(names, sums))
    sibling = _swap_sibling([partial[n] for n in BIG_NAMES], "swap_partial_grads")
    results = {n: _adamw(w[n], m[n], v[n], [partial[n], other], "adamw_" + n) for n, other in zip(BIG_NAMES, sibling)}

    slots = _gather_wait(*small_flight, results['w_down'][0], "gather_small_grads_wait", all_devices=True)
    summed = _unpack(_sum_slots(slots, "sum_small_grads"), [small_full[n].shape for n in small_names])
    grads = dict(zip(small_names, summed))
    grads['meta_tokens'] = lax.dynamic_slice_in_dim(grads['meta_tokens'], chip * meta_tokens.shape[1], meta_tokens.shape[1], 1)
    grads['conv_w'] = lax.dynamic_slice_in_dim(grads['conv_w'], chip * conv_w.shape[2], conv_w.shape[2], 2)

    out_g, out_d, out_m, out_v = {}, {}, {}, {}
    for n in BIG_NAMES:
        out_g[n], out_d[n], out_m[n], out_v[n] = [jnp.swapaxes(r, 1, 2) if n == 'w_in' else r for r in results[n][:4]]
    shapes = [w[n].shape for n in small_names]
    res = _adamw(_pack([w[n] for n in small_names]), _pack([m[n] for n in small_names]), _pack([v[n] for n in small_names]),
                 [_pack([grads[n] for n in small_names])], "adamw_small")
    for k, store in enumerate((out_g, out_d, out_m, out_v)):
        for n, a in zip(small_names, _unpack(res[k], shapes)):
            store[n] = a

    return (loss, grad_x, *[out_g[n] for n in WEIGHT_NAMES], *[out_d[n] for n in WEIGHT_NAMES],
            *[out_m[n] for n in WEIGHT_NAMES], *[out_v[n] for n in WEIGHT_NAMES])
```

```python
import functools

import jax
import jax.numpy as jnp
from jax import lax
from jax.experimental import pallas as pl
from jax.experimental.pallas import tpu as pltpu

F32 = jnp.float32
BF16 = jnp.bfloat16

D_MODEL = 1024
N_HEADS = 8
ATTN_W = 512
KV_W = 128
LRU_W = 512
LRU_BLOCKS = 8
LRU_BLOCK = 64
IN_COLS = 1792
D_FF = 4096
N_META = 16
CONV_TAPS = 4
LRU_C = 8.0
ROPE_THETA = 10000.0
EPS = 1e-6
ATTN_SCALE = 0.125

ADAM_LR = 0.001
ADAM_B1 = 0.9
ADAM_B2 = 0.999
ADAM_EPS = 1e-08
ADAM_WD = 0.01
ADAM_STEP = 10

N_CHIPS = 4
N_DEV = 8
TIME_BLOCK = 128
ROW_TILE = 256
PROJ_ROW_TILE = 544
LANES = 128
SUBLANES = 8
MASKED = -1e30
VMEM_LIMIT = 56 * 1024 * 1024

MESH = pl.DeviceIdType.MESH
HBM_SPEC = pl.BlockSpec(memory_space=pltpu.HBM)

WEIGHT_NAMES = ['meta_tokens', 'pre_mix_norm', 'w_in', 'attn_sinks', 'conv_w', 'conv_b', 'gate_a_w', 'gate_a_b',
                'gate_x_w', 'gate_x_b', 'lru_lambda', 'attn_group_norm', 'lru_group_norm', 'w_out', 'post_mix_norm',
                'pre_mlp_norm', 'w_up', 'w_down', 'post_mlp_norm']
BIG_NAMES = ('w_in', 'w_out', 'w_up', 'w_down')
SAME_SHAPE = (('w_in',), ('w_out',), ('w_up', 'w_down'))
COLUMN_SHARDED_SMALL = ('meta_tokens', 'conv_w')
REPLICATED_NAMES = tuple(n for n in WEIGHT_NAMES if n not in BIG_NAMES and n not in COLUMN_SHARDED_SMALL)


def _sds(shape, dtype):
    return jax.ShapeDtypeStruct(tuple(shape), dtype)


def _params(*sem):
    return pltpu.CompilerParams(dimension_semantics=sem, vmem_limit_bytes=VMEM_LIMIT)


def _row_spec(width, tile=ROW_TILE):
    return pl.BlockSpec((tile, width), lambda i: (i, 0))


def _proj_tile(rows):
    tile = PROJ_ROW_TILE
    while rows % tile:
        tile -= 16
    return tile


def _whole_spec(a):
    nd = a.ndim
    return pl.BlockSpec(a.shape, lambda *_: (0,) * nd)


def _resident_spec(a):
    nd = a.ndim
    return pl.BlockSpec(a.shape, lambda *_: (0,) * nd, pipeline_mode=pl.Buffered(1))


def _rms(x, g):
    r = lax.rsqrt(jnp.mean(x * x, axis=-1, keepdims=True) + EPS)
    return x * r * g


def _rms_bwd(dy, x, g):
    r = lax.rsqrt(jnp.mean(x * x, axis=-1, keepdims=True) + EPS)
    xh = x * r
    dg = jnp.sum(dy * xh, axis=0, keepdims=True)
    dxh = dy * g
    dx = r * (dxh - xh * jnp.mean(dxh * xh, axis=-1, keepdims=True))
    return dx, dg


def _rope(x, cos, sin_signed):
    width = x.shape[1]
    reps = width // LANES
    if reps > 1:
        cos = jnp.tile(cos, (1, reps))
        sin_signed = jnp.tile(sin_signed, (1, reps))
    lane = lax.broadcasted_iota(jnp.int32, x.shape, 1)
    first_half = jnp.bitwise_and(lane, 32) == 0
    other = jnp.where(first_half, pltpu.roll(x, width - 32, 1), pltpu.roll(x, 32, 1))
    return x * cos + other * sin_signed


def _sigmoid(x):
    return 1.0 / (1.0 + jnp.exp(-x))


def _log1p(e):
    return jnp.where(e < 1e-3, e * (1.0 - e * (0.5 - e * (1.0 / 3.0))), jnp.log(1.0 + e))


def _one_minus_square(a, log_a):
    x = 2.0 * log_a
    return jnp.where(x > -0.002, x * (-1.0 - 0.5 * x), 1.0 - a * a)


GELU_K = 0.7978845608028654
GELU_C = 0.044715


def _gelu(x):
    t = jnp.tanh(GELU_K * (x + GELU_C * x * x * x))
    return 0.5 * x * (1.0 + t)


def _gelu_and_grad(x):
    x2 = x * x
    t = jnp.tanh(GELU_K * (x + GELU_C * x * x2))
    val = 0.5 * x * (1.0 + t)
    grad = 0.5 * (1.0 + t) + 0.5 * x * (1.0 - t * t) * GELU_K * (1.0 + 3.0 * GELU_C * x2)
    return val, grad


def _dot(a, b):
    return jnp.dot(a, b, preferred_element_type=F32)


def _dot_nt(a, b):
    return lax.dot_general(a, b, (((1,), (1,)), ((), ())), preferred_element_type=F32)


def _dot_tn(a, b):
    return lax.dot_general(a, b, (((0,), (0,)), ((), ())), preferred_element_type=F32)


def _put_rows(rows_8, values):
    d = values[0].shape[1]
    rowid = lax.broadcasted_iota(jnp.int32, (rows_8, d), 0)
    out = jnp.zeros((rows_8, d), F32)
    for k, v in enumerate(values):
        out = out + jnp.where(rowid == k, v, 0.0)
    return out


def _in_proj_fwd(h, gain, w_in, rope_cos, rope_sin):
    rows = h.shape[0]

    def body(h_ref, g_ref, w_ref, c_ref, s_ref, z_ref, q_ref, kv_ref, xg_ref):
        z = _rms(h_ref[...], g_ref[...]).astype(BF16)
        z_ref[...] = z
        proj = _dot_nt(z, w_ref[...])
        cos = c_ref[...]
        sin = s_ref[...]
        q_ref[...] = (_rope(proj[:, :ATTN_W], cos, sin) * ATTN_SCALE).astype(BF16)
        kv_ref[:, :KV_W] = _rope(proj[:, ATTN_W:ATTN_W + KV_W], cos, sin).astype(BF16)
        kv_ref[:, KV_W:] = proj[:, ATTN_W + KV_W:ATTN_W + 2 * KV_W].astype(BF16)
        xg_ref[...] = proj[:, ATTN_W + 2 * KV_W:]

    tile = _proj_tile(rows)
    rs = functools.partial(_row_spec, tile=tile)
    return pl.pallas_call(
        body, name="in_proj_fwd", grid=(rows // tile,),
        in_specs=[rs(D_MODEL), _whole_spec(gain), _whole_spec(w_in), rs(LANES), rs(LANES)],
        out_specs=[rs(D_MODEL), rs(ATTN_W), rs(2 * KV_W), rs(2 * LRU_W)],
        out_shape=[_sds((rows, D_MODEL), BF16), _sds((rows, ATTN_W), BF16), _sds((rows, 2 * KV_W), BF16),
                   _sds((rows, 2 * LRU_W), F32)],
        compiler_params=_params("parallel"),
    )(h, gain, w_in, rope_cos, rope_sin)


def _in_proj_bwd(dh_mid, h, dq, dkv, dxg, rope_cos, rope_sin, gain, w_in):
    rows = h.shape[0]

    def body(dhm_ref, h_ref, dq_ref, dkv_ref, dxg_ref, c_ref, s_ref, g_ref, w_ref, dh_ref, dp_ref, gacc_ref):
        @pl.when(pl.program_id(0) == 0)
        def _():
            gacc_ref[...] = jnp.zeros_like(gacc_ref)

        cos = c_ref[...]
        sin = -s_ref[...]
        dp_ref[:, :ATTN_W] = (_rope(dq_ref[...], cos, sin) * ATTN_SCALE).astype(BF16)
        dp_ref[:, ATTN_W:ATTN_W + KV_W] = _rope(dkv_ref[:, :KV_W], cos, sin).astype(BF16)
        dp_ref[:, ATTN_W + KV_W:ATTN_W + 2 * KV_W] = dkv_ref[:, KV_W:].astype(BF16)
        dp_ref[:, ATTN_W + 2 * KV_W:] = dxg_ref[...]
        dz = _dot(dp_ref[...], w_ref[...])
        dx, dg = _rms_bwd(dz, h_ref[...], g_ref[...])
        dh_ref[...] = dhm_ref[...] + dx
        gacc_ref[...] += _put_rows(SUBLANES, [dg])

    tile = _proj_tile(rows)
    rs = functools.partial(_row_spec, tile=tile)
    return pl.pallas_call(
        body, name="in_proj_bwd", grid=(rows // tile,),
        in_specs=[rs(D_MODEL), rs(D_MODEL), rs(ATTN_W), rs(2 * KV_W), rs(2 * LRU_W),
                  rs(LANES), rs(LANES), _whole_spec(gain), _whole_spec(w_in)],
        out_specs=[rs(D_MODEL), rs(IN_COLS), pl.BlockSpec((SUBLANES, D_MODEL), lambda i: (0, 0))],
        out_shape=[_sds((rows, D_MODEL), F32), _sds((rows, IN_COLS), BF16), _sds((SUBLANES, D_MODEL), F32)],
        compiler_params=_params("arbitrary"),
    )(dh_mid, h, dq, dkv, dxg, rope_cos, rope_sin, gain, w_in)


def _kv_lane_variants(t, group):
    lane = lax.broadcasted_iota(jnp.int32, t.shape, 1)
    low = lane < 64
    swapped = pltpu.roll(t, 64, 1)
    if group == 0:
        lo, hi = jnp.where(low, t, 0.0), jnp.where(low, 0.0, swapped)
    else:
        lo, hi = jnp.where(low, swapped, 0.0), jnp.where(low, 0.0, t)
    return jnp.concatenate([lo, hi], axis=0).astype(BF16)


GROUP_ROWS = 2 * TIME_BLOCK
KEYS = 2 * TIME_BLOCK


def _window_mask(j):
    r = jnp.bitwise_and(lax.broadcasted_iota(jnp.int32, (GROUP_ROWS, KEYS), 0), TIME_BLOCK - 1)
    c = lax.broadcasted_iota(jnp.int32, (GROUP_ROWS, KEYS), 1)
    return (c > r) & (c <= r + TIME_BLOCK) & ((c >= TIME_BLOCK) | (j > 0))


def _group_rows(ref, ex, group):
    lo = 2 * group * LANES
    return jnp.concatenate([ref[ex, :, lo:lo + LANES], ref[ex, :, lo + LANES:lo + 2 * LANES]], axis=0)


def _per_head(sink_ref, group, half):
    upper = lax.broadcasted_iota(jnp.int32, (GROUP_ROWS, 1), 0) < TIME_BLOCK
    return jnp.where(upper, sink_ref[4 * group + half], sink_ref[4 * group + 2 + half])


def _by_example(a, batch):
    return a.reshape(batch, a.shape[0] // batch, a.shape[1])


def _attn_fwd(q, kv, sinks, batch, n_blocks):
    rows = q.shape[0]

    def body(sink_ref, q_ref, kvc_ref, kvp_ref, o_ref, lse_ref):
        j = pl.program_id(0)
        mask = _window_mask(j)
        chains = [(ex, group) for ex in range(batch) for group in range(2)]
        v_cat, s_all, probs = {}, {}, {ch: [] for ch in chains}
        for ex, group in chains:
            kv2 = jnp.concatenate([kvp_ref[ex], kvc_ref[ex]], axis=0).astype(F32)
            k_cat = _kv_lane_variants(kv2[:, :KV_W], group)
            v_cat[ex, group] = _kv_lane_variants(kv2[:, KV_W:], group)
            s_all[ex, group] = _dot_nt(_group_rows(q_ref, ex, group), k_cat)
        for half in range(2):
            for ex, group in chains:
                sink = _per_head(sink_ref, group, half)
                s = jnp.where(mask, s_all[ex, group][:, half * KEYS:(half + 1) * KEYS], MASKED)
                m = jnp.maximum(jnp.max(s, axis=1, keepdims=True), sink)
                e = jnp.exp(s - m)
                den = jnp.sum(e, axis=1, keepdims=True) + jnp.exp(sink - m)
                probs[ex, group].append((e / den).astype(BF16))
                lse = m + jnp.log(den)
                upper_head, lower_head = 4 * group + half, 4 * group + 2 + half
                lse_ref[ex, :, upper_head:upper_head + 1] = lse[:TIME_BLOCK]
                lse_ref[ex, :, lower_head:lower_head + 1] = lse[TIME_BLOCK:]
        for ex, group in chains:
            out = _dot(jnp.concatenate(probs[ex, group], axis=1), v_cat[ex, group])
            o_ref[ex, :, 2 * group * LANES:(2 * group + 1) * LANES] = out[:TIME_BLOCK]
            o_ref[ex, :, (2 * group + 1) * LANES:(2 * group + 2) * LANES] = out[TIME_BLOCK:]

    def blk(width):
        return pl.BlockSpec((batch, TIME_BLOCK, width), lambda j: (0, j, 0))

    prev = pl.BlockSpec((batch, TIME_BLOCK, 2 * KV_W), lambda j: (0, jnp.maximum(j - 1, 0), 0))
    kv3 = _by_example(kv, batch)
    out, lse = pl.pallas_call(
        body, name="attn_fwd", grid=(n_blocks,),
        in_specs=[pl.BlockSpec(memory_space=pltpu.SMEM), blk(ATTN_W), blk(2 * KV_W), prev],
        out_specs=[blk(ATTN_W), blk(N_HEADS)],
        out_shape=[_sds((batch, rows // batch, ATTN_W), F32), _sds((batch, rows // batch, N_HEADS), F32)],
        compiler_params=_params("parallel"),
    )(sinks, _by_example(q, batch), kv3, kv3)
    return out.reshape(rows, ATTN_W), lse.reshape(rows, N_HEADS)


def _attn_bwd(q, kv, sinks, out, lse, d_out, batch, n_blocks):
    rows = q.shape[0]

    def body(sink_ref, q_ref, kvc_ref, kvp_ref, o_ref, do_ref, lse_ref, dq_ref, dkv_ref, dsink_ref, carry):
        j = pl.program_id(0)

        @pl.when(j == 0)
        def _():
            dsink_ref[...] = jnp.zeros_like(dsink_ref)

        def all_chains(dsink_vals):
            mask = _window_mask(j)
            lane = lax.broadcasted_iota(jnp.int32, (GROUP_ROWS, LANES), 1)
            low = lax.broadcasted_iota(jnp.int32, (KEYS, LANES), 1) < 64
            upper = lax.broadcasted_iota(jnp.int32, (GROUP_ROWS, 1), 0) < TIME_BLOCK
            chains = [(ex, group) for ex in range(batch) for group in range(2)]
            k_cat, v_cat, q_rows, do_b, od, s_all, dp_all = {}, {}, {}, {}, {}, {}, {}
            for ch in chains:
                ex, group = ch
                kv2 = jnp.concatenate([kvp_ref[ex], kvc_ref[ex]], axis=0).astype(F32)
                k_cat[ch] = _kv_lane_variants(kv2[:, :KV_W], group)
                v_cat[ch] = _kv_lane_variants(kv2[:, KV_W:], group)
                q_rows[ch] = _group_rows(q_ref, ex, group)
                do_rows = _group_rows(do_ref, ex, group)
                do_b[ch] = do_rows.astype(BF16)
                od[ch] = do_rows * _group_rows(o_ref, ex, group)
                s_all[ch] = _dot_nt(q_rows[ch], k_cat[ch])
                dp_all[ch] = _dot_nt(do_b[ch], v_cat[ch])
            probs, dss = {ch: [] for ch in chains}, {ch: [] for ch in chains}
            for half in range(2):
                for ch in chains:
                    ex, group = ch
                    heads = (4 * group + half, 4 * group + 2 + half)
                    sink = _per_head(sink_ref, group, half)
                    lse_h = jnp.concatenate([lse_ref[ex, :, h:h + 1] for h in heads], axis=0)
                    in_half = (lane < 64) if half == 0 else (lane >= 64)
                    delta = jnp.sum(jnp.where(in_half, od[ch], 0.0), axis=1, keepdims=True)
                    cols = slice(half * KEYS, (half + 1) * KEYS)
                    prob = jnp.exp(jnp.where(mask, s_all[ch][:, cols], MASKED) - lse_h)
                    probs[ch].append(prob.astype(BF16))
                    dss[ch].append((prob * (dp_all[ch][:, cols] - delta)).astype(BF16))
                    dsink = -jnp.exp(sink - lse_h) * delta
                    dsink_vals[heads[0]] = dsink_vals[heads[0]] + jnp.sum(jnp.where(upper, dsink, 0.0), axis=0, keepdims=True)
                    dsink_vals[heads[1]] = dsink_vals[heads[1]] + jnp.sum(jnp.where(upper, 0.0, dsink), axis=0, keepdims=True)
            dk_tile = {ex: jnp.zeros((KEYS, KV_W), F32) for ex in range(batch)}
            dv_tile = {ex: jnp.zeros((KEYS, KV_W), F32) for ex in range(batch)}
            for ch in chains:
                ex, group = ch
                ds = jnp.concatenate(dss[ch], axis=1)
                dq_rows = _dot(ds, k_cat[ch])
                dq_ref[ex, :, 2 * group * LANES:(2 * group + 1) * LANES] = dq_rows[:TIME_BLOCK]
                dq_ref[ex, :, (2 * group + 1) * LANES:(2 * group + 2) * LANES] = dq_rows[TIME_BLOCK:]
                dk_cat = _dot_tn(ds, q_rows[ch])
                dv_cat = _dot_tn(jnp.concatenate(probs[ch], axis=1), do_b[ch])
                if group == 0:
                    dk_tile[ex] = dk_tile[ex] + jnp.where(low, dk_cat[:KEYS] + pltpu.roll(dk_cat[KEYS:], 64, 1), 0.0)
                    dv_tile[ex] = dv_tile[ex] + jnp.where(low, dv_cat[:KEYS] + pltpu.roll(dv_cat[KEYS:], 64, 1), 0.0)
                else:
                    dk_tile[ex] = dk_tile[ex] + jnp.where(low, 0.0, pltpu.roll(dk_cat[:KEYS], 64, 1) + dk_cat[KEYS:])
                    dv_tile[ex] = dv_tile[ex] + jnp.where(low, 0.0, pltpu.roll(dv_cat[:KEYS], 64, 1) + dv_cat[KEYS:])

            @pl.when(j > 0)
            def _():
                for ex in range(batch):
                    dkv_ref[ex, :, :KV_W] = carry[ex, :, :KV_W] + dk_tile[ex][:TIME_BLOCK]
                    dkv_ref[ex, :, KV_W:] = carry[ex, :, KV_W:] + dv_tile[ex][:TIME_BLOCK]

            for ex in range(batch):
                carry[ex, :, :KV_W] = dk_tile[ex][TIME_BLOCK:]
                carry[ex, :, KV_W:] = dv_tile[ex][TIME_BLOCK:]

        @pl.when(j < n_blocks)
        def _():
            dsink_vals = {head: jnp.zeros((1, 1), F32) for head in range(N_HEADS)}
            all_chains(dsink_vals)
            rowid = lax.broadcasted_iota(jnp.int32, (N_HEADS, LANES), 0)
            upd = jnp.zeros((N_HEADS, LANES), F32)
            for head, val in dsink_vals.items():
                upd = upd + jnp.where(rowid == head, val, 0.0)
            dsink_ref[...] += upd

        @pl.when(j == n_blocks)
        def _():
            dkv_ref[...] = carry[...]

    last = n_blocks - 1

    def blk(width):
        return pl.BlockSpec((batch, TIME_BLOCK, width), lambda j: (0, jnp.minimum(j, last), 0))

    prev = pl.BlockSpec((batch, TIME_BLOCK, 2 * KV_W), lambda j: (0, jnp.maximum(jnp.minimum(j, last) - 1, 0), 0))
    dkv_spec = pl.BlockSpec((batch, TIME_BLOCK, 2 * KV_W), lambda j: (0, jnp.maximum(j - 1, 0), 0))
    kv3 = _by_example(kv, batch)
    dq, dkv, dsink = pl.pallas_call(
        body, name="attn_bwd", grid=(n_blocks + 1,),
        in_specs=[pl.BlockSpec(memory_space=pltpu.SMEM), blk(ATTN_W), blk(2 * KV_W), prev, blk(ATTN_W), blk(ATTN_W),
                  blk(N_HEADS)],
        out_specs=[blk(ATTN_W), dkv_spec, pl.BlockSpec((N_HEADS, LANES), lambda j: (0, 0))],
        out_shape=[_sds((batch, rows // batch, ATTN_W), F32), _sds((batch, rows // batch, 2 * KV_W), F32),
                   _sds((N_HEADS, LANES), F32)],
        scratch_shapes=[pltpu.VMEM((batch, TIME_BLOCK, 2 * KV_W), F32)],
        compiler_params=_params("arbitrary"),
    )(sinks, _by_example(q, batch), kv3, kv3, _by_example(out, batch), _by_example(d_out, batch), _by_example(lse, batch))
    return dq.reshape(rows, ATTN_W), dkv.reshape(rows, 2 * KV_W), dsink


def _conv_taps(xb, prev8):
    ext = jnp.concatenate([prev8, xb], axis=0)
    n = ext.shape[0]
    return [xb] + [pltpu.roll(ext, k, 0)[SUBLANES:n] for k in range(1, CONV_TAPS)]


def _lru_gates(xc, wa, ba, wx, bx, lam):
    xcb = xc.astype(BF16)
    r = _sigmoid(_dot(xcb, wa) + ba)
    i = _sigmoid(_dot(xcb, wx) + bx)
    sp = jnp.maximum(-lam, 0.0) + _log1p(jnp.exp(-jnp.abs(lam)))
    log_a = -LRU_C * r * sp
    a = jnp.exp(log_a)
    mult = jnp.sqrt(_one_minus_square(a, log_a))
    return xcb, r, i, sp, a, mult


def _scan_fwd(a, u, h_before):
    n, d = a.shape
    groups = n // SUBLANES
    per_seq = groups // len(h_before)
    a = a.reshape(groups, SUBLANES, d)
    u = u.reshape(groups, SUBLANES, d)
    sub = lax.broadcasted_iota(jnp.int32, a.shape, 1)
    s = 1
    while s < SUBLANES:
        valid = sub >= s
        u = jnp.where(valid, u + a * pltpu.roll(u, s, 1), u)
        a = jnp.where(valid, a * pltpu.roll(a, s, 1), a)
        s *= 2
    out, prev = [None] * groups, list(h_before)
    for g in range(per_seq):
        for k in range(len(h_before)):
            at = k * per_seq + g
            out[at] = u[at] + a[at] * prev[k]
            prev[k] = out[at][SUBLANES - 1:SUBLANES, :]
    return jnp.concatenate(out, axis=0)


def _scan_rev(cf, g, d_after):
    n, d = g.shape
    groups = n // SUBLANES
    per_seq = groups // len(d_after)
    cf = cf.reshape(groups, SUBLANES, d)
    g = g.reshape(groups, SUBLANES, d)
    sub = lax.broadcasted_iota(jnp.int32, g.shape, 1)
    s = 1
    while s < SUBLANES:
        valid = sub + s < SUBLANES
        g = jnp.where(valid, g + cf * pltpu.roll(g, SUBLANES - s, 1), g)
        cf = jnp.where(valid, cf * pltpu.roll(cf, SUBLANES - s, 1), cf)
        s *= 2
    out, nxt = [None] * groups, list(d_after)
    for i in reversed(range(per_seq)):
        for k in range(len(d_after)):
            at = k * per_seq + i
            out[at] = g[at] + cf[at] * nxt[k]
            nxt[k] = out[at][0:1, :]
    return jnp.concatenate(out, axis=0)


def _lru_fwd(xg, conv_w, conv_b, wa, ba, wx, bx, lam, batch, n_blocks):
    rows = xg.shape[0]

    def body(xg_ref, cw_ref, cb_ref, wa_ref, ba_ref, wx_ref, bx_ref, lam_ref, hs_ref, lru_ref, x_prev, h_carry):
        @pl.when(pl.program_id(0) == 0)
        def _():
            x_prev[...] = jnp.zeros_like(x_prev)
            h_carry[...] = jnp.zeros_like(h_carry)

        xb = [xg_ref[ex, :, :LRU_W] for ex in range(batch)]
        taps = [_conv_taps(xb[ex], x_prev[ex]) for ex in range(batch)]
        taps = [jnp.concatenate([t[k] for t in taps], axis=0) for k in range(CONV_TAPS)]
        xc = cb_ref[...] + sum(cw_ref[CONV_TAPS - 1 - k:CONV_TAPS - k, :] * taps[k] for k in range(CONV_TAPS))
        _, _, i, _, a, mult = _lru_gates(xc, wa_ref[...], ba_ref[...], wx_ref[...], bx_ref[...], lam_ref[...])
        h = _scan_fwd(a, mult * i * xc, [h_carry[ex] for ex in range(batch)])
        gate = _gelu(jnp.concatenate([xg_ref[ex, :, LRU_W:] for ex in range(batch)], axis=0))
        lru = h * gate
        for ex in range(batch):
            rows_ex = slice(ex * TIME_BLOCK, (ex + 1) * TIME_BLOCK)
            hs_ref[ex] = h[rows_ex]
            lru_ref[ex] = lru[rows_ex]
            h_carry[ex] = h[(ex + 1) * TIME_BLOCK - 1:(ex + 1) * TIME_BLOCK, :]
            x_prev[ex] = xb[ex][TIME_BLOCK - SUBLANES:TIME_BLOCK, :]

    def blk(width):
        return pl.BlockSpec((batch, TIME_BLOCK, width), lambda j: (0, j, 0))

    small = [conv_w, conv_b, wa, ba, wx, bx, lam]
    hs, lru = pl.pallas_call(
        body, name="lru_fwd", grid=(n_blocks,),
        in_specs=[blk(2 * LRU_W)] + [_whole_spec(a) for a in small],
        out_specs=[blk(LRU_W), blk(LRU_W)],
        out_shape=[_sds((batch, rows // batch, LRU_W), F32)] * 2,
        scratch_shapes=[pltpu.VMEM((batch, SUBLANES, LRU_W), F32), pltpu.VMEM((batch, 1, LRU_W), F32)],
        compiler_params=_params("arbitrary"),
    )(_by_example(xg, batch), *small)
    return hs.reshape(rows, LRU_W), lru.reshape(rows, LRU_W)


def _lru_bwd(xg, hs, d_lru, conv_w, conv_b, wa, ba, wx, bx, lam, batch, n_blocks):
    rows = xg.shape[0]
    last_row = TIME_BLOCK - 1

    def body(xg_ref, xgh_ref, hs_ref, hsh_ref, dl_ref, cw_ref, cb_ref, wa_ref, ba_ref, wx_ref, bx_ref, lam_ref,
             dxg_ref, dwa_ref, dwx_ref, vec_ref, dh_carry, dxc_next):
        j = pl.program_id(0)

        @pl.when(j == 0)
        def _():
            dwa_ref[...] = jnp.zeros_like(dwa_ref)
            dwx_ref[...] = jnp.zeros_like(dwx_ref)
            vec_ref[...] = jnp.zeros_like(vec_ref)
            dh_carry[...] = jnp.zeros_like(dh_carry)
            dxc_next[...] = jnp.zeros_like(dxc_next)

        examples = range(batch)
        first = j == n_blocks - 1
        xb = [xg_ref[ex, :, :LRU_W] for ex in examples]
        taps = [_conv_taps(xb[ex], jnp.where(first, 0.0, xgh_ref[ex, :, :LRU_W])) for ex in examples]
        taps = [jnp.concatenate([t[k] for t in taps], axis=0) for k in range(CONV_TAPS)]
        cw = cw_ref[...]
        lam = lam_ref[...]
        wa = wa_ref[...]
        wx = wx_ref[...]
        xc = cb_ref[...] + sum(cw[CONV_TAPS - 1 - k:CONV_TAPS - k, :] * taps[k] for k in range(CONV_TAPS))
        xcb, r, i, sp, a, mult = _lru_gates(xc, wa, ba_ref[...], wx, bx_ref[...], lam)
        hs = jnp.concatenate([hs_ref[ex] for ex in examples], axis=0)
        row = jnp.bitwise_and(lax.broadcasted_iota(jnp.int32, hs.shape, 0), TIME_BLOCK - 1)
        h_before = jnp.concatenate([jnp.broadcast_to(jnp.where(first, 0.0, hsh_ref[ex, SUBLANES - 1:SUBLANES, :]),
                                                     (TIME_BLOCK, LRU_W)) for ex in examples], axis=0)
        h_prev = jnp.where(row == 0, h_before, pltpu.roll(hs, 1, 0))
        dl = jnp.concatenate([dl_ref[ex] for ex in examples], axis=0)
        gate, dgate = _gelu_and_grad(jnp.concatenate([xg_ref[ex, :, LRU_W:] for ex in examples], axis=0))
        dgb = (dl * hs * dgate).astype(BF16)
        cf = jnp.where(row == last_row, 1.0, pltpu.roll(a, batch * TIME_BLOCK - 1, 0))
        dh = _scan_rev(cf, dl * gate, [dh_carry[ex] for ex in examples])
        dmult = dh * i * xc
        di = dh * mult * xc
        dxc = dh * mult * i
        dlog_a = dh * h_prev * a - dmult * (a * a / mult)
        dr = dlog_a * (-LRU_C * sp)
        dlam = jnp.sum(dlog_a * (-LRU_C * r), axis=0, keepdims=True) * (-_sigmoid(-lam))
        dpr = dr * r * (1.0 - r)
        dpi = di * i * (1.0 - i)
        dprb = dpr.astype(BF16)
        dpib = dpi.astype(BF16)
        dxc = dxc + _dot_nt(dprb, wa) + _dot_nt(dpib, wx)
        dwa_ref[...] += _dot_tn(xcb, dprb)
        dwx_ref[...] += _dot_tn(xcb, dpib)
        n = TIME_BLOCK + SUBLANES
        later = []
        for k in range(1, CONV_TAPS):
            shifted = [pltpu.roll(jnp.concatenate([dxc[ex * TIME_BLOCK:(ex + 1) * TIME_BLOCK], dxc_next[ex]], axis=0),
                                  n - k, 0)[:TIME_BLOCK] for ex in examples]
            later.append(jnp.concatenate(shifted, axis=0))
        dxb = cw[CONV_TAPS - 1:CONV_TAPS, :] * dxc
        for k in range(1, CONV_TAPS):
            dxb = dxb + cw[CONV_TAPS - 1 - k:CONV_TAPS - k, :] * later[k - 1]
        dxb = dxb.astype(BF16)
        for ex in examples:
            rows_ex = slice(ex * TIME_BLOCK, (ex + 1) * TIME_BLOCK)
            dxg_ref[ex, :, :LRU_W] = dxb[rows_ex]
            dxg_ref[ex, :, LRU_W:] = dgb[rows_ex]
            dh_carry[ex] = a[ex * TIME_BLOCK:ex * TIME_BLOCK + 1, :] * dh[ex * TIME_BLOCK:ex * TIME_BLOCK + 1, :]
            dxc_next[ex] = dxc[ex * TIME_BLOCK:ex * TIME_BLOCK + SUBLANES, :]
        vecs = [jnp.sum(dxc * taps[CONV_TAPS - 1 - t], axis=0, keepdims=True) for t in range(CONV_TAPS)]
        vecs += [jnp.sum(dxc, axis=0, keepdims=True), jnp.sum(dpr, axis=0, keepdims=True),
                 jnp.sum(dpi, axis=0, keepdims=True), dlam]
        vec_ref[...] += _put_rows(SUBLANES, vecs)

    def tblk(j):
        return n_blocks - 1 - j

    def blk(width):
        return pl.BlockSpec((batch, TIME_BLOCK, width), lambda j: (0, tblk(j), 0))

    per8 = TIME_BLOCK // SUBLANES

    def halo(width):
        return pl.BlockSpec((batch, SUBLANES, width), lambda j: (0, jnp.maximum(per8 * tblk(j) - 1, 0), 0))

    small = [conv_w, conv_b, wa, ba, wx, bx, lam]
    acc = lambda shape: pl.BlockSpec(shape, lambda j: (0, 0))
    xg3, hs3 = _by_example(xg, batch), _by_example(hs, batch)
    dxg, dwa, dwx, vec = pl.pallas_call(
        body, name="lru_bwd", grid=(n_blocks,),
        in_specs=[blk(2 * LRU_W), halo(2 * LRU_W), blk(LRU_W), halo(LRU_W), blk(LRU_W)] + [_whole_spec(a) for a in small],
        out_specs=[blk(2 * LRU_W), acc((LRU_W, LRU_W)), acc((LRU_W, LRU_W)), acc((SUBLANES, LRU_W))],
        out_shape=[_sds((batch, rows // batch, 2 * LRU_W), BF16), _sds((LRU_W, LRU_W), F32), _sds((LRU_W, LRU_W), F32),
                   _sds((SUBLANES, LRU_W), F32)],
        scratch_shapes=[pltpu.VMEM((batch, 1, LRU_W), F32), pltpu.VMEM((batch, SUBLANES, LRU_W), F32)],
        compiler_params=_params("arbitrary"),
    )(xg3, xg3, hs3, hs3, _by_example(d_lru, batch), *small)
    return dxg.reshape(rows, 2 * LRU_W), dwa, dwx, vec


def _out_mlp_fwd(attn, lru, h, w_out, w_up4, w_down, g_attn, g_lru, g_post_mix, g_pre_mlp, g_post_mlp):
    rows = h.shape[0]

    def body(at_ref, lr_ref, h_ref, wo_ref, wu_ref, wd_ref, ga_ref, gl_ref, gp_ref, g1_ref, g2_ref,
             grp_ref, o_ref, hm_ref, z_ref, slope_ref, act_ref, y_ref, ho_ref):
        a = _rms(at_ref[...], ga_ref[...]).astype(BF16)
        l = _rms(lr_ref[...], gl_ref[...]).astype(BF16)
        grp_ref[:, :ATTN_W] = a
        grp_ref[:, ATTN_W:] = l
        o = _dot(a, wo_ref[:ATTN_W, :]) + _dot(l, wo_ref[ATTN_W:, :])
        o_ref[...] = o
        x = h_ref[...] + _rms(o, gp_ref[...])
        hm_ref[...] = x
        z = _rms(x, g1_ref[...]).astype(BF16)
        z_ref[...] = z
        y = jnp.zeros((ROW_TILE, D_MODEL), F32)
        for s in range(N_CHIPS):
            cols = slice(s * D_MODEL, (s + 1) * D_MODEL)
            r = jnp.maximum(_dot(z, wu_ref[s]), 0.0)
            slope_ref[:, cols] = (2.0 * r).astype(BF16)
            act = jnp.square(r).astype(BF16)
            act_ref[:, cols] = act
            y = y + _dot(act, wd_ref[cols, :])
        y_ref[...] = y
        ho_ref[...] = x + _rms(y, g2_ref[...])

    gains = [g_attn, g_lru, g_post_mix, g_pre_mlp, g_post_mlp]
    return pl.pallas_call(
        body, name="out_mlp_fwd", grid=(rows // ROW_TILE,),
        in_specs=[_row_spec(ATTN_W), _row_spec(LRU_W), _row_spec(D_MODEL), _resident_spec(w_out), _resident_spec(w_up4),
                  _resident_spec(w_down)] + [_whole_spec(g) for g in gains],
        out_specs=[_row_spec(D_MODEL), _row_spec(D_MODEL), _row_spec(D_MODEL), _row_spec(D_MODEL), _row_spec(D_FF),
                   _row_spec(D_FF), _row_spec(D_MODEL), _row_spec(D_MODEL)],
        out_shape=[_sds((rows, D_MODEL), BF16), _sds((rows, D_MODEL), F32), _sds((rows, D_MODEL), F32),
                   _sds((rows, D_MODEL), BF16), _sds((rows, D_FF), BF16), _sds((rows, D_FF), BF16),
                   _sds((rows, D_MODEL), F32), _sds((rows, D_MODEL), F32)],
        compiler_params=_params("parallel"),
    )(attn, lru, h, w_out, w_up4, w_down, *gains)


def _mlp_out_bwd(dh_out, h_mid, y, slope, o, attn, lru, w_out, w_up4, w_down, g_attn, g_lru, g_post_mix, g_pre_mlp,
                 g_post_mlp):
    rows = y.shape[0]

    def body(dh_ref, hm_ref, y_ref, slope_ref, o_ref, at_ref, lr_ref, wo_ref, wu_ref, wd_ref, ga_ref, gl_ref, gp_ref,
             g1_ref, g2_ref, dhm_ref, dy_ref, dup_ref, do_ref, dat_ref, dlr_ref, gacc_ref):
        @pl.when(pl.program_id(0) == 0)
        def _():
            gacc_ref[...] = jnp.zeros_like(gacc_ref)

        dh = dh_ref[...]
        dy, dg2 = _rms_bwd(dh, y_ref[...], g2_ref[...])
        dyb = dy.astype(BF16)
        dy_ref[...] = dyb
        dz = jnp.zeros((ROW_TILE, D_MODEL), F32)
        for s in range(N_CHIPS):
            cols = slice(s * D_MODEL, (s + 1) * D_MODEL)
            dact = _dot_nt(dyb, wd_ref[cols, :])
            dup = (dact * slope_ref[:, cols].astype(F32)).astype(BF16)
            dup_ref[:, cols] = dup
            dz = dz + _dot_nt(dup, wu_ref[s])
        dx, dg1 = _rms_bwd(dz, hm_ref[...], g1_ref[...])
        dhm = dh + dx
        dhm_ref[...] = dhm
        do, dgp = _rms_bwd(dhm, o_ref[...], gp_ref[...])
        dob = do.astype(BF16)
        do_ref[...] = dob
        dat, dga = _rms_bwd(_dot_nt(dob, wo_ref[:ATTN_W, :]), at_ref[...], ga_ref[...])
        dlr, dgl = _rms_bwd(_dot_nt(dob, wo_ref[ATTN_W:, :]), lr_ref[...], gl_ref[...])
        dat_ref[...] = dat
        dlr_ref[...] = dlr
        gacc_ref[...] += _put_rows(SUBLANES, [dg1, dg2, dgp, jnp.concatenate([dga, dgl], axis=1)])

    gains = [g_attn, g_lru, g_post_mix, g_pre_mlp, g_post_mlp]
    return pl.pallas_call(
        body, name="mlp_out_bwd", grid=(rows // ROW_TILE,),
        in_specs=[_row_spec(D_MODEL), _row_spec(D_MODEL), _row_spec(D_MODEL), _row_spec(D_FF), _row_spec(D_MODEL),
                  _row_spec(ATTN_W), _row_spec(LRU_W), _resident_spec(w_out), _resident_spec(w_up4), _resident_spec(w_down)]
                 + [_whole_spec(g) for g in gains],
        out_specs=[_row_spec(D_MODEL), _row_spec(D_MODEL), _row_spec(D_FF), _row_spec(D_MODEL), _row_spec(ATTN_W),
                   _row_spec(LRU_W), pl.BlockSpec((SUBLANES, D_MODEL), lambda i: (0, 0))],
        out_shape=[_sds((rows, D_MODEL), F32), _sds((rows, D_MODEL), BF16), _sds((rows, D_FF), BF16),
                   _sds((rows, D_MODEL), BF16), _sds((rows, ATTN_W), F32), _sds((rows, LRU_W), F32),
                   _sds((SUBLANES, D_MODEL), F32)],
        compiler_params=_params("arbitrary"),
    )(dh_out, h_mid, y, slope, o, attn, lru, w_out, w_up4, w_down, *gains)


def _matmul_tn(a, b, tm, tn, name, out_dtype, column_blocks=False):
    rows, m = a.shape
    n = b.shape[1]

    def body(a_ref, b_ref, o_ref):
        o_ref[...] = _dot_tn(a_ref[...], b_ref[...]).astype(out_dtype)

    if column_blocks:
        out_spec = pl.BlockSpec((None, tm, tn), lambda i, j: (j, i, 0))
        out_shape = _sds((n // tn, m, tn), out_dtype)
    else:
        out_spec = pl.BlockSpec((tm, tn), lambda i, j: (i, j))
        out_shape = _sds((m, n), out_dtype)
    return pl.pallas_call(
        body, name=name, grid=(m // tm, n // tn),
        in_specs=[pl.BlockSpec((rows, tm), lambda i, j: (0, i)), pl.BlockSpec((rows, tn), lambda i, j: (0, j))],
        out_specs=out_spec, out_shape=out_shape,
        compiler_params=_params("parallel", "parallel"),
    )(a, b)


LOSS_COLS = 256


def _loss_head(h, loss_target):
    batch, per_example, _ = h.shape
    seq = loss_target.shape[1]
    n_real = N_META + seq
    assert seq % SUBLANES == 0

    def body(h_ref, t_ref, dh_ref, l_ref):
        @pl.when((pl.program_id(0) == 0) & (pl.program_id(1) == 0))
        def _():
            l_ref[...] = jnp.zeros_like(l_ref)

        d = h_ref[N_META:n_real, :] - t_ref[...]
        dh_ref[:N_META, :] = jnp.zeros((N_META, LOSS_COLS), F32)
        dh_ref[N_META:n_real, :] = d * (1.0 / D_MODEL)
        dh_ref[n_real:, :] = jnp.zeros((per_example - n_real, LOSS_COLS), F32)
        l_ref[...] += jnp.sum(jnp.sum(d * d, axis=0, keepdims=True), axis=1, keepdims=True)

    blk = pl.BlockSpec((None, per_example, LOSS_COLS), lambda b, j: (b, 0, j))
    return pl.pallas_call(
        body, name="loss_head", grid=(batch, D_MODEL // LOSS_COLS),
        in_specs=[blk, pl.BlockSpec((None, seq, LOSS_COLS), lambda b, j: (b, 0, j))],
        out_specs=[blk, pl.BlockSpec((SUBLANES, LANES), lambda b, j: (0, 0))],
        out_shape=[_sds(h.shape, F32), _sds((SUBLANES, LANES), F32)],
        compiler_params=_params("arbitrary", "arbitrary"),
    )(h, loss_target)


def _meta_grad(dh0, batch, rows_per_example):
    per = rows_per_example // N_META

    def body(d_ref, o_ref):
        @pl.when(pl.program_id(0) == 0)
        def _():
            o_ref[...] = jnp.zeros_like(o_ref)

        o_ref[...] += d_ref[...]

    return pl.pallas_call(
        body, name="meta_grad", grid=(batch,),
        in_specs=[pl.BlockSpec((N_META, D_MODEL), lambda b: (b * per, 0))],
        out_specs=pl.BlockSpec((N_META, D_MODEL), lambda b: (0, 0)),
        out_shape=_sds((N_META, D_MODEL), F32),
        compiler_params=_params("arbitrary"),
    )(dh0)


def _elementwise_tile(rows, cols):
    tile = rows
    while tile * cols * 4 > (2 << 20) and tile % 16 == 0:
        tile //= 2
    return tile


def _sum_slots(buf, name):
    k, rows, cols = buf.shape
    tile = _elementwise_tile(rows, cols)

    def body(*refs):
        total = refs[0][...].astype(F32)
        for r in refs[1:k]:
            total = total + r[...].astype(F32)
        refs[k][...] = total

    def slot(s):
        return pl.BlockSpec((None, tile, cols), lambda i: (s, i, 0))

    return pl.pallas_call(
        body, name=name, grid=(rows // tile,),
        in_specs=[slot(s) for s in range(k)], out_specs=pl.BlockSpec((tile, cols), lambda i: (i, 0)),
        out_shape=_sds((rows, cols), F32), compiler_params=_params("parallel"),
    )(*([buf] * k))


def _sum_pieces(pieces, landed, chip, layer, n_layers, stacked):
    count = len(pieces)
    _, rows, cols = pieces[0].shape
    tile = _elementwise_tile(rows, cols)

    def body(chip_ref, *refs):
        token = refs[-1]
        for t in range(count):
            own_ref, a_ref, b_ref, c_ref = refs[4 * t:4 * t + 4]
            total = ((own_ref[...].astype(F32) + a_ref[...].astype(F32)) + b_ref[...].astype(F32)) + c_ref[...].astype(F32)
            refs[-1 - count + t][...] = total.astype(BF16)
        token[...] = jnp.zeros_like(token)

    def slot(offset):
        return pl.BlockSpec((None, tile, cols), lambda i, chip_ref: ((chip_ref[0] + offset) % N_CHIPS, i, 0))

    carried = [] if stacked is None else list(stacked)
    operands = []
    for p, l in zip(pieces, landed):
        operands += [p, l, l, l]
    grid_spec = pltpu.PrefetchScalarGridSpec(
        num_scalar_prefetch=1, grid=(rows // tile,),
        in_specs=[slot(0), slot(1), slot(2), slot(3)] * count + [pl.BlockSpec(memory_space=pl.ANY)] * len(carried),
        out_specs=[pl.BlockSpec((None, tile, cols), lambda i, chip_ref: (layer, i, 0))] * count
                  + [pl.BlockSpec((SUBLANES, LANES), lambda i, chip_ref: (0, 0))])
    out = pl.pallas_call(
        body, name="sum_grad_pieces", grid_spec=grid_spec,
        out_shape=[_sds((n_layers, rows, cols), BF16)] * count + [_sds((SUBLANES, LANES), F32)],
        input_output_aliases={1 + 4 * count + t: t for t in range(len(carried))},
        compiler_params=_params("arbitrary"),
    )(chip, *operands, *carried)
    return out[:count], out[count]


def _adamw(w, m, v, grads, name):
    flat = w.ndim == 2
    if flat:
        w, m, v = w[None], m[None], v[None]
        grads = [g[None] for g in grads]
    layers, rows, cols = w.shape
    tile = _elementwise_tile(rows, cols)
    ng = len(grads)
    m_scale = 1.0 - ADAM_B1 ** ADAM_STEP
    v_scale = 1.0 - ADAM_B2 ** ADAM_STEP

    def body(*refs):
        w_ref, m_ref, v_ref = refs[:3]
        g_refs = refs[3:3 + ng]
        g_out, d_out, m_out, v_out = refs[-4:]
        g = g_refs[0][...].astype(F32)
        for r in g_refs[1:]:
            g = g + r[...].astype(F32)
        m_new = ADAM_B1 * m_ref[...] + (1.0 - ADAM_B1) * g
        v_new = ADAM_B2 * v_ref[...] + (1.0 - ADAM_B2) * (g * g)
        m_hat = m_new / m_scale
        v_hat = v_new / v_scale
        g_out[...] = g
        d_out[...] = -ADAM_LR * (m_hat / (jnp.sqrt(v_hat) + ADAM_EPS) + ADAM_WD * w_ref[...])
        m_out[...] = m_new
        v_out[...] = v_new

    spec = pl.BlockSpec((None, tile, cols), lambda l, i: (l, i, 0))
    out = pl.pallas_call(
        body, name=name, grid=(layers, rows // tile),
        in_specs=[spec] * (3 + ng), out_specs=[spec] * 4, out_shape=[_sds(w.shape, F32)] * 4,
        compiler_params=_params("parallel", "parallel"),
    )(w, m, v, *grads)
    return [o[0] for o in out] if flat else out


def _position():
    return lax.axis_index("x"), lax.axis_index("y"), lax.axis_index("c")


def _other_chips(x, y):
    return [(1 - x, y), (x, 1 - y), (1 - x, 1 - y)]


def _exchange_chips(arrays, name, scatter):
    n = len(arrays)

    def body(*refs):
        src, dst = refs[:n], refs[n:2 * n]
        send_sems, recv_sems, local_sems = refs[2 * n:]
        x, y, c = _position()
        mine = 2 * x + y
        copies = []
        for i in range(n):
            own = src[i].at[mine] if scatter else src[i]
            copies.append(pltpu.make_async_copy(own, dst[i].at[mine], local_sems.at[i]))
        for k, (px, py) in enumerate(_other_chips(x, y)):
            for i in range(n):
                piece = src[i].at[2 * px + py] if scatter else src[i]
                copies.append(pltpu.make_async_remote_copy(
                    src_ref=piece, dst_ref=dst[i].at[mine], send_sem=send_sems.at[k, i], recv_sem=recv_sems.at[k, i],
                    device_id=(px, py, c), device_id_type=MESH))
        for cp in copies:
            cp.start()
        for cp in copies:
            cp.wait()

    def out_shape(a):
        return _sds(a.shape if scatter else (N_CHIPS,) + a.shape, a.dtype)

    return pl.pallas_call(
        body, name=name, in_specs=[HBM_SPEC] * n, out_specs=[HBM_SPEC] * n, out_shape=[out_shape(a) for a in arrays],
        scratch_shapes=[pltpu.SemaphoreType.DMA((N_CHIPS - 1, n)), pltpu.SemaphoreType.DMA((N_CHIPS - 1, n)),
                        pltpu.SemaphoreType.DMA((n,))],
    )(*arrays)


def _swap_sibling(arrays, name):
    n = len(arrays)

    def body(*refs):
        src, dst = refs[:n], refs[n:2 * n]
        send_sems, recv_sems = refs[2 * n:]
        x, y, c = _position()
        copies = [pltpu.make_async_remote_copy(
            src_ref=src[i], dst_ref=dst[i], send_sem=send_sems.at[i], recv_sem=recv_sems.at[i],
            device_id=(x, y, 1 - c), device_id_type=MESH) for i in range(n)]
        for cp in copies:
            cp.start()
        for cp in copies:
            cp.wait()

    return pl.pallas_call(
        body, name=name, in_specs=[HBM_SPEC] * n, out_specs=[HBM_SPEC] * n,
        out_shape=[_sds(a.shape, a.dtype) for a in arrays],
        scratch_shapes=[pltpu.SemaphoreType.DMA((n,)), pltpu.SemaphoreType.DMA((n,))],
    )(*arrays)


SEM_SPEC = pl.BlockSpec(memory_space=pltpu.SEMAPHORE)
ANY_SPEC = pl.BlockSpec(memory_space=pl.ANY)
IN_FLIGHT = pltpu.SideEffectType.DATAFLOW_SIDE_EFFECTING


def _peer_sems(all_devices):
    return pltpu.SemaphoreType.DMA(((N_DEV if all_devices else N_CHIPS) - 1,))


def _own_slot(all_devices, x, y, c):
    return 4 * x + 2 * y + c if all_devices else 2 * x + y


def _peers(all_devices, x, y, c):
    if not all_devices:
        return [((px, py, c), 2 * px + py) for px, py in _other_chips(x, y)]
    out = []
    for fx in range(2):
        for fy in range(2):
            for fc in range(2):
                if fx or fy or fc:
                    px, py, pc = (1 - x if fx else x), (1 - y if fy else y), (1 - c if fc else c)
                    out.append(((px, py, pc), 4 * px + 2 * py + pc))
    return out


def _in_hbm(a):
    return pltpu.with_memory_space_constraint(a, pltpu.HBM)


def _place_slot(a, slot, n_slots, name):
    rows, cols = a.shape
    tile = _elementwise_tile(rows, cols)

    def body(slot_ref, a_ref, o_ref):
        o_ref[...] = a_ref[...]

    grid_spec = pltpu.PrefetchScalarGridSpec(
        num_scalar_prefetch=1, grid=(rows // tile,),
        in_specs=[pl.BlockSpec((tile, cols), lambda i, slot_ref: (i, 0))],
        out_specs=pl.BlockSpec((None, tile, cols), lambda i, slot_ref: (slot_ref[0], i, 0)))
    return pl.pallas_call(
        body, name=name, grid_spec=grid_spec, out_shape=_sds((n_slots, rows, cols), a.dtype),
        compiler_params=_params("parallel"),
    )(slot, a)


def _place_shard(ws, layer, chip, name):
    count = len(ws)
    _, rows, cols = ws[0].shape
    tile = _elementwise_tile(rows, cols)

    def body(chip_ref, *refs):
        for t in range(count):
            refs[count + t][...] = refs[t][...].astype(BF16)

    grid_spec = pltpu.PrefetchScalarGridSpec(
        num_scalar_prefetch=1, grid=(rows // tile,),
        in_specs=[pl.BlockSpec((None, tile, cols), lambda i, chip_ref: (layer, i, 0))] * count,
        out_specs=[pl.BlockSpec((None, tile, cols), lambda i, chip_ref: (chip_ref[0], i, 0))] * count)
    return pl.pallas_call(
        body, name=name, grid_spec=grid_spec, out_shape=[_sds((N_CHIPS, rows, cols), BF16)] * count,
        compiler_params=_params("parallel"),
    )(chip, *ws)


def _rows_of(ref, slot, core, halves):
    if not halves:
        return ref.at[slot]
    half = ref.shape[1] // 2
    return ref.at[slot, pl.ds(pl.multiple_of(core * half, half), half)]


def _gather_start(bufs, after, name, all_devices=False, halves=False):
    n = len(bufs)

    def body(*refs):
        buf = refs[:n]
        send, recv = refs[n + 1:2 * n + 1], refs[2 * n + 1:3 * n + 1]
        token = refs[4 * n + 1]
        x, y, c = _position()
        mine = _own_slot(all_devices, x, y, c)
        for i in range(n):
            for k, (peer, _) in enumerate(_peers(all_devices, x, y, c)):
                rows = _rows_of(buf[i], mine, c, halves)
                pltpu.make_async_remote_copy(
                    src_ref=rows, dst_ref=rows, send_sem=send[i].at[k], recv_sem=recv[i].at[k],
                    device_id=peer, device_id_type=MESH).start()
        token[...] = jnp.zeros_like(token)

    sems = _peer_sems(all_devices)
    out = pl.pallas_call(
        body, name=name, in_specs=[HBM_SPEC] * n + [ANY_SPEC],
        out_specs=[SEM_SPEC] * (2 * n) + [HBM_SPEC] * n + [pl.BlockSpec(memory_space=pltpu.VMEM)],
        out_shape=[sems] * (2 * n) + [pltpu.HBM(b.shape, b.dtype) for b in bufs] + [_sds((SUBLANES, LANES), F32)],
        input_output_aliases={i: 2 * n + i for i in range(n)},
        compiler_params=pltpu.CompilerParams(has_side_effects=IN_FLIGHT),
    )(*[_in_hbm(b) for b in bufs], after)
    return [(out[i], out[n + i], out[2 * n + i]) for i in range(n)], out[3 * n]


def _gather_relay(flights, after, name):
    n = len(flights)

    def body(*refs):
        buf, send, recv = refs[:n], refs[n:2 * n], refs[2 * n:3 * n]
        send2, recv2 = refs[3 * n + 1:4 * n + 1], refs[4 * n + 1:5 * n + 1]
        token = refs[6 * n + 1]
        x, y, c = _position()
        mine = 2 * x + y
        for i in range(n):
            for k, (peer, slot) in enumerate(_peers(False, x, y, c)):
                arrived = _rows_of(buf[i], slot, c, True)
                cp = pltpu.make_async_remote_copy(
                    src_ref=_rows_of(buf[i], mine, c, True), dst_ref=arrived, send_sem=send[i].at[k], recv_sem=recv[i].at[k],
                    device_id=peer, device_id_type=MESH)
                cp.wait_send()
                cp.wait_recv()
                pltpu.make_async_remote_copy(
                    src_ref=arrived, dst_ref=arrived, send_sem=send2[i].at[k], recv_sem=recv2[i].at[k],
                    device_id=(x, y, 1 - c), device_id_type=MESH).start()
        token[...] = jnp.zeros_like(token)

    bufs = [f[2] for f in flights]
    sems = _peer_sems(False)
    out = pl.pallas_call(
        body, name=name, in_specs=[HBM_SPEC] * n + [SEM_SPEC] * (2 * n) + [ANY_SPEC],
        out_specs=[SEM_SPEC] * (2 * n) + [HBM_SPEC] * n + [pl.BlockSpec(memory_space=pltpu.VMEM)],
        out_shape=[sems] * (2 * n) + [pltpu.HBM(b.shape, b.dtype) for b in bufs] + [_sds((SUBLANES, LANES), F32)],
        input_output_aliases={i: 2 * n + i for i in range(n)},
        compiler_params=pltpu.CompilerParams(has_side_effects=IN_FLIGHT),
    )(*bufs, *[f[0] for f in flights], *[f[1] for f in flights], after)
    return [(out[i], out[n + i], out[2 * n + i]) for i in range(n)], out[3 * n]


def _gather_wait(send, recv, buf, after, name, all_devices=False, relayed=False):
    def body(buf_ref, send_ref, recv_ref, after_ref, out_ref):
        x, y, c = _position()
        mine = _own_slot(all_devices, x, y, c)
        for k, (peer, slot) in enumerate(_peers(all_devices, x, y, c)):
            if relayed:
                cp = pltpu.make_async_remote_copy(
                    src_ref=_rows_of(buf_ref, slot, c, True), dst_ref=_rows_of(buf_ref, slot, 1 - c, True),
                    send_sem=send_ref.at[k], recv_sem=recv_ref.at[k], device_id=(x, y, 1 - c), device_id_type=MESH)
            else:
                cp = pltpu.make_async_remote_copy(
                    src_ref=buf_ref.at[mine], dst_ref=buf_ref.at[slot], send_sem=send_ref.at[k], recv_sem=recv_ref.at[k],
                    device_id=peer, device_id_type=MESH)
            cp.wait_send()
            cp.wait_recv()

    return pl.pallas_call(
        body, name=name, in_specs=[HBM_SPEC, SEM_SPEC, SEM_SPEC, ANY_SPEC], out_specs=HBM_SPEC,
        out_shape=pltpu.HBM(buf.shape, buf.dtype), input_output_aliases={0: 0},
        compiler_params=pltpu.CompilerParams(has_side_effects=IN_FLIGHT),
    )(buf, send, recv, after)


def _scatter_start(pieces, name):
    n = len(pieces)

    def body(*refs):
        src = refs[:n]
        send, recv = refs[n:2 * n], refs[2 * n:3 * n]
        land = refs[4 * n:5 * n]
        token = refs[5 * n]
        x, y, c = _position()
        mine = 2 * x + y
        for i in range(n):
            for k, (px, py) in enumerate(_other_chips(x, y)):
                pltpu.make_async_remote_copy(
                    src_ref=src[i].at[2 * px + py], dst_ref=land[i].at[mine], send_sem=send[i].at[k], recv_sem=recv[i].at[k],
                    device_id=(px, py, c), device_id_type=MESH).start()
        token[...] = jnp.zeros_like(token)

    hbm = [pltpu.HBM(p.shape, p.dtype) for p in pieces]
    out = pl.pallas_call(
        body, name=name, in_specs=[HBM_SPEC] * n,
        out_specs=[SEM_SPEC] * (2 * n) + [HBM_SPEC] * (2 * n) + [pl.BlockSpec(memory_space=pltpu.VMEM)],
        out_shape=[_peer_sems(False)] * (2 * n) + hbm + hbm + [_sds((SUBLANES, LANES), F32)],
        input_output_aliases={i: 2 * n + i for i in range(n)},
        compiler_params=pltpu.CompilerParams(has_side_effects=IN_FLIGHT),
    )(*[_in_hbm(p) for p in pieces])
    return [(out[i], out[n + i], out[2 * n + i], out[3 * n + i]) for i in range(n)], out[4 * n]


def _scatter_wait(send, recv, pieces, land, after, name):
    def body(src_ref, land_ref, send_ref, recv_ref, after_ref, src_out, land_out):
        x, y, c = _position()
        for k, (px, py) in enumerate(_other_chips(x, y)):
            cp = pltpu.make_async_remote_copy(
                src_ref=src_ref.at[2 * px + py], dst_ref=land_ref.at[2 * px + py], send_sem=send_ref.at[k],
                recv_sem=recv_ref.at[k], device_id=(px, py, c), device_id_type=MESH)
            cp.wait_send()
            cp.wait_recv()

    return pl.pallas_call(
        body, name=name, in_specs=[HBM_SPEC, HBM_SPEC, SEM_SPEC, SEM_SPEC, ANY_SPEC], out_specs=[HBM_SPEC, HBM_SPEC],
        out_shape=[pltpu.HBM(pieces.shape, pieces.dtype), pltpu.HBM(land.shape, land.dtype)],
        input_output_aliases={0: 0, 1: 1},
        compiler_params=pltpu.CompilerParams(has_side_effects=IN_FLIGHT),
    )(pieces, land, send, recv, after)


def _rope_tables(batch, rows_per_example):
    inv_freq = ROPE_THETA ** (-jnp.arange(0, 64, 2, dtype=F32) / 64)
    ang = jnp.arange(rows_per_example, dtype=F32)[:, None] * inv_freq[None, :]
    cos, sin = jnp.cos(ang), jnp.sin(ang)
    cos128 = jnp.concatenate([cos, cos, cos, cos], axis=1)
    sin128 = jnp.concatenate([-sin, sin, -sin, sin], axis=1)
    return jnp.tile(cos128, (batch, 1)), jnp.tile(sin128, (batch, 1))


def _block_diagonal(w):
    eye = jnp.eye(LRU_BLOCKS, dtype=w.dtype)
    return (w[:, :, None, :] * eye[:, None, :, None]).reshape(LRU_W, LRU_W)


def _diagonal_blocks(dense):
    d4 = dense.reshape(LRU_BLOCKS, LRU_BLOCK, LRU_BLOCKS, LRU_BLOCK)
    return jnp.stack([d4[n, :, n, :] for n in range(LRU_BLOCKS)])


def _row(v):
    return v.reshape(1, -1)


def _local_step(x, loss_target, meta_tokens, small, depth, big_weight, on_layer_grads):
    batch, seq, _ = x.shape
    n_real = N_META + seq
    n_blocks = -(-n_real // TIME_BLOCK)
    per_example = n_blocks * TIME_BLOCK
    pad = per_example - n_real
    meta = jnp.broadcast_to(meta_tokens[None], (batch, N_META, D_MODEL))
    h = jnp.concatenate([meta, x, jnp.zeros((batch, pad, D_MODEL), F32)], axis=1).reshape(batch * per_example, D_MODEL)
    rope_cos, rope_sin = _rope_tables(batch, per_example)

    saved = []
    for l in range(depth):
        wa = _block_diagonal(small['gate_a_w'][l]).astype(BF16)
        wx = _block_diagonal(small['gate_x_w'][l]).astype(BF16)
        lru_small = (small['conv_w'][l], _row(small['conv_b'][l]), wa, _row(small['gate_a_b'][l]), wx,
                     _row(small['gate_x_b'][l]), _row(small['lru_lambda'][l]))
        w_in = big_weight('w_in', l, h)
        z1, q, kv, xg = _in_proj_fwd(h, _row(small['pre_mix_norm'][l]), w_in, rope_cos, rope_sin)
        attn, lse = _attn_fwd(q, kv, small['attn_sinks'][l], batch, n_blocks)
        hs, lru = _lru_fwd(xg, *lru_small, batch, n_blocks)
        w_out = big_weight('w_out', l, xg)
        w_up4 = big_weight('w_up', l, xg)
        w_down = big_weight('w_down', l, xg)
        gains = [_row(small[n][l]) for n in ('attn_group_norm', 'lru_group_norm', 'post_mix_norm', 'pre_mlp_norm',
                                             'post_mlp_norm')]
        groups, o, h_mid, z2, slope, act, y, h_out = _out_mlp_fwd(attn, lru, h, w_out, w_up4, w_down, *gains)
        saved.append(dict(h=h, z1=z1, q=q, kv=kv, xg=xg, attn=attn, lse=lse, hs=hs, lru=lru, groups=groups, o=o,
                          h_mid=h_mid, z2=z2, slope=slope, act=act, y=y, lru_small=lru_small, w_in=w_in, w_out=w_out,
                          w_up4=w_up4, w_down=w_down, gains=gains))
        h = h_out

    dh, sq_err = _loss_head(h.reshape(batch, per_example, D_MODEL), loss_target)
    dh = dh.reshape(batch * per_example, D_MODEL)

    gs = {n: [None] * depth for n in REPLICATED_NAMES + ('conv_w',)}
    handed_over = None
    for l in reversed(range(depth)):
        s = saved[l]

        def ordered_after(gain):
            return gain if handed_over is None else gain + handed_over[0, 0]

        gains = list(s['gains'])
        gains[4] = ordered_after(gains[4])
        dh_mid, dy, dup, do, d_attn, d_lru, g_rows = _mlp_out_bwd(dh, s['h_mid'], s['y'], s['slope'], s['o'], s['attn'],
                                                                s['lru'], s['w_out'], s['w_up4'], s['w_down'], *gains)
        handed_over = on_layer_grads(l, {
            'w_down': _matmul_tn(s['act'], dy, 512, D_MODEL, "grad_w_down", BF16),
            'w_up': _matmul_tn(s['z2'], dup, 512, D_MODEL, "grad_w_up", BF16, column_blocks=True)})
        lru_small = list(s['lru_small'])
        lru_small[1] = ordered_after(lru_small[1])
        dxg, dwa, dwx, g_lru = _lru_bwd(s['xg'], s['hs'], d_lru, *lru_small, batch, n_blocks)
        dq, dkv, dsink = _attn_bwd(s['q'], s['kv'], small['attn_sinks'][l], s['attn'], s['lse'], d_attn, batch, n_blocks)
        dh, dproj, g_in = _in_proj_bwd(dh_mid, s['h'], dq, dkv, dxg, rope_cos, rope_sin, _row(small['pre_mix_norm'][l]),
                                       s['w_in'])
        handed_over = on_layer_grads(l, {
            'w_out': _matmul_tn(s['groups'], do, 512, D_MODEL, "grad_w_out", BF16),
            'w_in': _matmul_tn(dproj, s['z1'], IN_COLS // 2, D_MODEL, "grad_w_in", BF16)})
        gs['pre_mlp_norm'][l], gs['post_mlp_norm'][l] = g_rows[0], g_rows[1]
        gs['post_mix_norm'][l] = g_rows[2]
        gs['attn_group_norm'][l], gs['lru_group_norm'][l] = g_rows[3, :ATTN_W], g_rows[3, ATTN_W:]
        gs['pre_mix_norm'][l] = g_in[0]
        gs['attn_sinks'][l] = dsink[:, 0]
        gs['conv_w'][l] = g_lru[:CONV_TAPS]
        gs['conv_b'][l], gs['gate_a_b'][l], gs['gate_x_b'][l], gs['lru_lambda'][l] = g_lru[4], g_lru[5], g_lru[6], g_lru[7]
        gs['gate_a_w'][l] = _diagonal_blocks(dwa)
        gs['gate_x_w'][l] = _diagonal_blocks(dwx)

    grad_x = dh.reshape(batch, per_example, D_MODEL)[:, N_META:n_real]
    grad_meta = _meta_grad(dh, batch, per_example)
    small_grads = {n: jnp.stack(v) for n, v in gs.items()}
    return sq_err, grad_x, grad_meta, small_grads, handed_over


PACK_UNIT = SUBLANES * LANES


def _size(shape):
    size = 1
    for d in shape:
        size *= d
    return size


def _pack(arrays):
    parts = []
    for a in arrays:
        flat = a.reshape(-1)
        padded = -(-flat.shape[0] // PACK_UNIT) * PACK_UNIT
        if padded != flat.shape[0]:
            flat = jnp.pad(flat, (0, padded - flat.shape[0]))
        parts.append(flat.reshape(-1, LANES))
    return jnp.concatenate(parts, axis=0)


def _unpack(buf, shapes):
    out, at = [], 0
    for shp in shapes:
        size = _size(shp)
        rows = -(-size // PACK_UNIT) * SUBLANES
        part = buf[at:at + rows]
        if rows * LANES != size:
            part = part.reshape(-1)[:size]
        out.append(part.reshape(shp))
        at += rows
    return out


def kernel(x, meta_tokens, pre_mix_norm, w_in, attn_sinks, conv_w, conv_b, gate_a_w, gate_a_b, gate_x_w, gate_x_b, lru_lambda, attn_group_norm, lru_group_norm, w_out, post_mix_norm, pre_mlp_norm, w_up, w_down, post_mlp_norm, loss_target, m_meta_tokens, m_pre_mix_norm, m_w_in, m_attn_sinks, m_conv_w, m_conv_b, m_gate_a_w, m_gate_a_b, m_gate_x_w, m_gate_x_b, m_lru_lambda, m_attn_group_norm, m_lru_group_norm, m_w_out, m_post_mix_norm, m_pre_mlp_norm, m_w_up, m_w_down, m_post_mlp_norm, v_meta_tokens, v_pre_mix_norm, v_w_in, v_attn_sinks, v_conv_w, v_conv_b, v_gate_a_w, v_gate_a_b, v_gate_x_w, v_gate_x_b, v_lru_lambda, v_attn_group_norm, v_lru_group_norm, v_w_out, v_post_mix_norm, v_pre_mlp_norm, v_w_up, v_w_down, v_post_mlp_norm):
    given = dict(locals())
    w = {n: given[n] for n in WEIGHT_NAMES}
    m = {n: given['m_' + n] for n in WEIGHT_NAMES}
    v = {n: given['v_' + n] for n in WEIGHT_NAMES}
    depth = w_in.shape[0]
    for d in (w, m, v):
        d['w_in'] = jnp.swapaxes(d['w_in'], 1, 2)
    chip = 2 * lax.axis_index("x") + lax.axis_index("y")
    chip1 = chip.reshape(1).astype(jnp.int32)

    in_flight = []
    all_started = chip1
    for l in range(depth):
        bufs = []
        for names in SAME_SHAPE:
            bufs += _place_shard([w[n] for n in names], l, chip1, "place_" + names[0])
        handles, all_started = _gather_start(bufs, all_started, "gather_start_%d" % l, halves=True)
        in_flight.append(dict(zip(BIG_NAMES, handles)))
    first_use = {'w_in': ('w_in',), 'w_out': ('w_out', 'w_up', 'w_down')}

    def big_weight(name, l, after):
        if l == 0 and name == 'w_in':
            after = all_started
        names = first_use.get(name, ()) if l == 0 else (BIG_NAMES if name == 'w_in' else ())
        if names:
            relayed, _ = _gather_relay([in_flight[l][n] for n in names], after, "gather_relay_%d_%s" % (l, name))
            in_flight[l].update(zip(names, relayed))
        send, recv, buf = in_flight[l][name]
        full = _gather_wait(send, recv, buf, after, "gather_wait_%d_%s" % (l, name), relayed=True)
        if name == 'w_in':
            return full.reshape(IN_COLS, D_MODEL)
        if name == 'w_out':
            return full.reshape(D_MODEL, D_MODEL)
        if name == 'w_down':
            return full.reshape(D_FF, D_MODEL)
        return full

    gathered = _exchange_chips([meta_tokens, conv_w], "gather_small_weights", scatter=False)
    full_meta = jnp.concatenate([gathered[0][s] for s in range(N_CHIPS)], axis=1)
    small = {n: w[n] for n in REPLICATED_NAMES}
    small['conv_w'] = jnp.concatenate([gathered[1][s] for s in range(N_CHIPS)], axis=2)

    scattering = [{} for _ in range(depth)]

    pending = {}

    def on_layer_grads(l, big):
        if l > 0 and 'w_in' not in big:
            pending.update(big)
            return None
        big = {**pending, **big}
        pending.clear()
        names = list(big)
        pieces = [big[n].reshape(N_CHIPS, -1, D_MODEL) for n in names]
        handles, started = _scatter_start(pieces, "scatter_start_%d_%s" % (l, names[0]))
        scattering[l].update(zip(names, handles))
        return started

    sq_err, grad_x, grad_meta, small_grads, all_handed_over = _local_step(x, loss_target, full_meta, small, depth, big_weight,
                                                                           on_layer_grads)

    small_names = list(REPLICATED_NAMES) + list(COLUMN_SHARDED_SMALL)
    small_full = dict(small_grads)
    small_full['meta_tokens'] = grad_meta
    device1 = (2 * chip + lax.axis_index("c")).reshape(1).astype(jnp.int32)
    packed = _place_slot(_pack([small_full[n] for n in small_names] + [sq_err]), device1, N_DEV, "place_small_grads")
    [small_flight], done = _gather_start([packed], all_handed_over, "gather_small_grads_start", all_devices=True)

    partial = {n: None for n in BIG_NAMES}
    for l in reversed(range(depth)):
        for names in SAME_SHAPE:
            arrived = []
            for n in names:
                send, recv, pieces, land = scattering[l][n]
                arrived.append(_scatter_wait(send, recv, pieces, land, done, "scatter_wait_%d_%s" % (l, n)))
            stacked = None if partial[names[0]] is None else [partial[n] for n in names]
            sums, done = _sum_pieces([a[0] for a in arrived], [a[1] for a in arrived], chip1, l, depth, stacked)
            partial.update(zip(names, sums))
    sibling = _swap_sibling([partial[n] for n in BIG_NAMES], "swap_partial_grads")
    results = {n: _adamw(w[n], m[n], v[n], [partial[n], other], "adamw_" + n) for n, other in zip(BIG_NAMES, sibling)}

    slots = _gather_wait(*small_flight, results['w_down'][0], "gather_small_grads_wait", all_devices=True)
    summed = _unpack(_sum_slots(slots, "sum_small_grads"), [small_full[n].shape for n in small_names] + [sq_err.shape])
    loss = summed[-1][0, 0] * (0.5 / D_MODEL)
    grads = dict(zip(small_names, summed))
    grads['meta_tokens'] = lax.dynamic_slice_in_dim(grads['meta_tokens'], chip * meta_tokens.shape[1], meta_tokens.shape[1], 1)
    grads['conv_w'] = lax.dynamic_slice_in_dim(grads['conv_w'], chip * conv_w.shape[2], conv_w.shape[2], 2)

    out_g, out_d, out_m, out_v = {}, {}, {}, {}
    for n in BIG_NAMES:
        out_g[n], out_d[n], out_m[n], out_v[n] = [jnp.swapaxes(r, 1, 2) if n == 'w_in' else r for r in results[n][:4]]
    shapes = [w[n].shape for n in small_names]
    res = _adamw(_pack([w[n] for n in small_names]), _pack([m[n] for n in small_names]), _pack([v[n] for n in small_names]),
                 [_pack([grads[n] for n in small_names])], "adamw_small")
    for k, store in enumerate((out_g, out_d, out_m, out_v)):
        for n, a in zip(small_names, _unpack(res[k], shapes)):
            store[n] = a

    return (loss, grad_x, *[out_g[n] for n in WEIGHT_NAMES], *[out_d[n] for n in WEIGHT_NAMES],
            *[out_m[n] for n in WEIGHT_NAMES], *[out_v[n] for n in WEIGHT_NAMES])
```

```python
import functools

import jax
import jax.numpy as jnp
from jax import lax
from jax.experimental import pallas as pl
from jax.experimental.pallas import tpu as pltpu

F32 = jnp.float32
BF16 = jnp.bfloat16

D_MODEL = 1024
N_HEADS = 8
ATTN_W = 512
KV_W = 128
LRU_W = 512
LRU_BLOCKS = 8
LRU_BLOCK = 64
IN_COLS = 1792
D_FF = 4096
N_META = 16
CONV_TAPS = 4
LRU_C = 8.0
ROPE_THETA = 10000.0
EPS = 1e-6
ATTN_SCALE = 0.125

ADAM_LR = 0.001
ADAM_B1 = 0.9
ADAM_B2 = 0.999
ADAM_EPS = 1e-08
ADAM_WD = 0.01
ADAM_STEP = 10

N_CHIPS = 4
N_DEV = 8
TIME_BLOCK = 128
ROW_TILE = 256
PROJ_ROW_TILE = 544
LANES = 128
SUBLANES = 8
MASKED = -1e30
VMEM_LIMIT = 56 * 1024 * 1024

MESH = pl.DeviceIdType.MESH
HBM_SPEC = pl.BlockSpec(memory_space=pltpu.HBM)

WEIGHT_NAMES = ['meta_tokens', 'pre_mix_norm', 'w_in', 'attn_sinks', 'conv_w', 'conv_b', 'gate_a_w', 'gate_a_b',
                'gate_x_w', 'gate_x_b', 'lru_lambda', 'attn_group_norm', 'lru_group_norm', 'w_out', 'post_mix_norm',
                'pre_mlp_norm', 'w_up', 'w_down', 'post_mlp_norm']
BIG_NAMES = ('w_in', 'w_out', 'w_up', 'w_down')
SAME_SHAPE = (('w_in',), ('w_out',), ('w_up', 'w_down'))
COLUMN_SHARDED_SMALL = ('meta_tokens', 'conv_w')
REPLICATED_NAMES = tuple(n for n in WEIGHT_NAMES if n not in BIG_NAMES and n not in COLUMN_SHARDED_SMALL)


def _sds(shape, dtype):
    return jax.ShapeDtypeStruct(tuple(shape), dtype)


def _params(*sem):
    return pltpu.CompilerParams(dimension_semantics=sem, vmem_limit_bytes=VMEM_LIMIT)


def _row_spec(width, tile=ROW_TILE):
    return pl.BlockSpec((tile, width), lambda i: (i, 0))


def _proj_tile(rows):
    tile = PROJ_ROW_TILE
    while rows % tile:
        tile -= 16
    return tile


def _whole_spec(a):
    nd = a.ndim
    return pl.BlockSpec(a.shape, lambda *_: (0,) * nd)


def _resident_spec(a):
    nd = a.ndim
    return pl.BlockSpec(a.shape, lambda *_: (0,) * nd, pipeline_mode=pl.Buffered(1))


def _rms(x, g):
    r = lax.rsqrt(jnp.mean(x * x, axis=-1, keepdims=True) + EPS)
    return x * r * g


def _rms_bwd(dy, x, g):
    r = lax.rsqrt(jnp.mean(x * x, axis=-1, keepdims=True) + EPS)
    xh = x * r
    dg = jnp.sum(dy * xh, axis=0, keepdims=True)
    dxh = dy * g
    dx = r * (dxh - xh * jnp.mean(dxh * xh, axis=-1, keepdims=True))
    return dx, dg


def _rope(x, cos, sin_signed):
    width = x.shape[1]
    reps = width // LANES
    if reps > 1:
        cos = jnp.tile(cos, (1, reps))
        sin_signed = jnp.tile(sin_signed, (1, reps))
    lane = lax.broadcasted_iota(jnp.int32, x.shape, 1)
    first_half = jnp.bitwise_and(lane, 32) == 0
    other = jnp.where(first_half, pltpu.roll(x, width - 32, 1), pltpu.roll(x, 32, 1))
    return x * cos + other * sin_signed


def _sigmoid(x):
    return 1.0 / (1.0 + jnp.exp(-x))


def _log1p(e):
    return jnp.where(e < 1e-3, e * (1.0 - e * (0.5 - e * (1.0 / 3.0))), jnp.log(1.0 + e))


def _one_minus_square(a, log_a):
    x = 2.0 * log_a
    return jnp.where(x > -0.002, x * (-1.0 - 0.5 * x), 1.0 - a * a)


GELU_K = 0.7978845608028654
GELU_C = 0.044715


def _gelu(x):
    t = jnp.tanh(GELU_K * (x + GELU_C * x * x * x))
    return 0.5 * x * (1.0 + t)


def _gelu_and_grad(x):
    x2 = x * x
    t = jnp.tanh(GELU_K * (x + GELU_C * x * x2))
    val = 0.5 * x * (1.0 + t)
    grad = 0.5 * (1.0 + t) + 0.5 * x * (1.0 - t * t) * GELU_K * (1.0 + 3.0 * GELU_C * x2)
    return val, grad


def _dot(a, b):
    return jnp.dot(a, b, preferred_element_type=F32)


def _dot_nt(a, b):
    return lax.dot_general(a, b, (((1,), (1,)), ((), ())), preferred_element_type=F32)


def _dot_tn(a, b):
    return lax.dot_general(a, b, (((0,), (0,)), ((), ())), preferred_element_type=F32)


def _put_rows(rows_8, values):
    d = values[0].shape[1]
    rowid = lax.broadcasted_iota(jnp.int32, (rows_8, d), 0)
    out = jnp.zeros((rows_8, d), F32)
    for k, v in enumerate(values):
        out = out + jnp.where(rowid == k, v, 0.0)
    return out


def _in_proj_fwd(h, gain, w_in, rope_cos, rope_sin):
    rows = h.shape[0]

    def body(h_ref, g_ref, w_ref, c_ref, s_ref, z_ref, q_ref, kv_ref, xg_ref):
        z = _rms(h_ref[...], g_ref[...]).astype(BF16)
        z_ref[...] = z
        proj = _dot_nt(z, w_ref[...])
        cos = c_ref[...]
        sin = s_ref[...]
        q_ref[...] = (_rope(proj[:, :ATTN_W], cos, sin) * ATTN_SCALE).astype(BF16)
        kv_ref[:, :KV_W] = _rope(proj[:, ATTN_W:ATTN_W + KV_W], cos, sin).astype(BF16)
        kv_ref[:, KV_W:] = proj[:, ATTN_W + KV_W:ATTN_W + 2 * KV_W].astype(BF16)
        xg_ref[...] = proj[:, ATTN_W + 2 * KV_W:]

    tile = _proj_tile(rows)
    rs = functools.partial(_row_spec, tile=tile)
    return pl.pallas_call(
        body, name="in_proj_fwd", grid=(rows // tile,),
        in_specs=[rs(D_MODEL), _whole_spec(gain), _whole_spec(w_in), rs(LANES), rs(LANES)],
        out_specs=[rs(D_MODEL), rs(ATTN_W), rs(2 * KV_W), rs(2 * LRU_W)],
        out_shape=[_sds((rows, D_MODEL), BF16), _sds((rows, ATTN_W), BF16), _sds((rows, 2 * KV_W), BF16),
                   _sds((rows, 2 * LRU_W), F32)],
        compiler_params=_params("parallel"),
    )(h, gain, w_in, rope_cos, rope_sin)


def _in_proj_bwd(dh_mid, h, dq, dkv, dxg, rope_cos, rope_sin, gain, w_in):
    rows = h.shape[0]

    def body(dhm_ref, h_ref, dq_ref, dkv_ref, dxg_ref, c_ref, s_ref, g_ref, w_ref, dh_ref, dp_ref, gacc_ref):
        @pl.when(pl.program_id(0) == 0)
        def _():
            gacc_ref[...] = jnp.zeros_like(gacc_ref)

        cos = c_ref[...]
        sin = -s_ref[...]
        dp_ref[:, :ATTN_W] = (_rope(dq_ref[...], cos, sin) * ATTN_SCALE).astype(BF16)
        dp_ref[:, ATTN_W:ATTN_W + KV_W] = _rope(dkv_ref[:, :KV_W], cos, sin).astype(BF16)
        dp_ref[:, ATTN_W + KV_W:ATTN_W + 2 * KV_W] = dkv_ref[:, KV_W:].astype(BF16)
        dp_ref[:, ATTN_W + 2 * KV_W:] = dxg_ref[...]
        dz = _dot(dp_ref[...], w_ref[...])
        dx, dg = _rms_bwd(dz, h_ref[...], g_ref[...])
        dh_ref[...] = dhm_ref[...] + dx
        gacc_ref[...] += _put_rows(SUBLANES, [dg])

    tile = _proj_tile(rows)
    rs = functools.partial(_row_spec, tile=tile)
    return pl.pallas_call(
        body, name="in_proj_bwd", grid=(rows // tile,),
        in_specs=[rs(D_MODEL), rs(D_MODEL), rs(ATTN_W), rs(2 * KV_W), rs(2 * LRU_W),
                  rs(LANES), rs(LANES), _whole_spec(gain), _whole_spec(w_in)],
        out_specs=[rs(D_MODEL), rs(IN_COLS), pl.BlockSpec((SUBLANES, D_MODEL), lambda i: (0, 0))],
        out_shape=[_sds((rows, D_MODEL), F32), _sds((rows, IN_COLS), BF16), _sds((SUBLANES, D_MODEL), F32)],
        compiler_params=_params("arbitrary"),
    )(dh_mid, h, dq, dkv, dxg, rope_cos, rope_sin, gain, w_in)


def _kv_lane_variants(t, group):
    lane = lax.broadcasted_iota(jnp.int32, t.shape, 1)
    low = lane < 64
    swapped = pltpu.roll(t, 64, 1)
    if group == 0:
        lo, hi = jnp.where(low, t, 0.0), jnp.where(low, 0.0, swapped)
    else:
        lo, hi = jnp.where(low, swapped, 0.0), jnp.where(low, 0.0, t)
    return jnp.concatenate([lo, hi], axis=0).astype(BF16)


GROUP_ROWS = 2 * TIME_BLOCK
KEYS = 2 * TIME_BLOCK


def _window_mask(j):
    r = jnp.bitwise_and(lax.broadcasted_iota(jnp.int32, (GROUP_ROWS, KEYS), 0), TIME_BLOCK - 1)
    c = lax.broadcasted_iota(jnp.int32, (GROUP_ROWS, KEYS), 1)
    return (c > r) & (c <= r + TIME_BLOCK) & ((c >= TIME_BLOCK) | (j > 0))


def _group_rows(ref, ex, group):
    lo = 2 * group * LANES
    return jnp.concatenate([ref[ex, :, lo:lo + LANES], ref[ex, :, lo + LANES:lo + 2 * LANES]], axis=0)


def _per_head(sink_ref, group, half):
    upper = lax.broadcasted_iota(jnp.int32, (GROUP_ROWS, 1), 0) < TIME_BLOCK
    return jnp.where(upper, sink_ref[4 * group + half], sink_ref[4 * group + 2 + half])


def _by_example(a, batch):
    return a.reshape(batch, a.shape[0] // batch, a.shape[1])


def _attn_fwd(q, kv, sinks, batch, n_blocks):
    rows = q.shape[0]

    def body(sink_ref, q_ref, kvc_ref, kvp_ref, o_ref, lse_ref):
        j = pl.program_id(0)
        mask = _window_mask(j)
        chains = [(ex, group) for ex in range(batch) for group in range(2)]
        v_cat, s_all, probs = {}, {}, {ch: [] for ch in chains}
        for ex, group in chains:
            kv2 = jnp.concatenate([kvp_ref[ex], kvc_ref[ex]], axis=0).astype(F32)
            k_cat = _kv_lane_variants(kv2[:, :KV_W], group)
            v_cat[ex, group] = _kv_lane_variants(kv2[:, KV_W:], group)
            s_all[ex, group] = _dot_nt(_group_rows(q_ref, ex, group), k_cat)
        for half in range(2):
            for ex, group in chains:
                sink = _per_head(sink_ref, group, half)
                s = jnp.where(mask, s_all[ex, group][:, half * KEYS:(half + 1) * KEYS], MASKED)
                m = jnp.maximum(jnp.max(s, axis=1, keepdims=True), sink)
                e = jnp.exp(s - m)
                den = jnp.sum(e, axis=1, keepdims=True) + jnp.exp(sink - m)
                probs[ex, group].append((e / den).astype(BF16))
                lse = m + jnp.log(den)
                upper_head, lower_head = 4 * group + half, 4 * group + 2 + half
                lse_ref[ex, :, upper_head:upper_head + 1] = lse[:TIME_BLOCK]
                lse_ref[ex, :, lower_head:lower_head + 1] = lse[TIME_BLOCK:]
        for ex, group in chains:
            out = _dot(jnp.concatenate(probs[ex, group], axis=1), v_cat[ex, group])
            o_ref[ex, :, 2 * group * LANES:(2 * group + 1) * LANES] = out[:TIME_BLOCK]
            o_ref[ex, :, (2 * group + 1) * LANES:(2 * group + 2) * LANES] = out[TIME_BLOCK:]

    def blk(width):
        return pl.BlockSpec((batch, TIME_BLOCK, width), lambda j: (0, j, 0))

    prev = pl.BlockSpec((batch, TIME_BLOCK, 2 * KV_W), lambda j: (0, jnp.maximum(j - 1, 0), 0))
    kv3 = _by_example(kv, batch)
    out, lse = pl.pallas_call(
        body, name="attn_fwd", grid=(n_blocks,),
        in_specs=[pl.BlockSpec(memory_space=pltpu.SMEM), blk(ATTN_W), blk(2 * KV_W), prev],
        out_specs=[blk(ATTN_W), blk(N_HEADS)],
        out_shape=[_sds((batch, rows // batch, ATTN_W), F32), _sds((batch, rows // batch, N_HEADS), F32)],
        compiler_params=_params("parallel"),
    )(sinks, _by_example(q, batch), kv3, kv3)
    return out.reshape(rows, ATTN_W), lse.reshape(rows, N_HEADS)


def _attn_bwd(q, kv, sinks, out, lse, d_out, batch, n_blocks):
    rows = q.shape[0]

    def body(sink_ref, q_ref, kvc_ref, kvp_ref, o_ref, do_ref, lse_ref, dq_ref, dkv_ref, dsink_ref, carry):
        j = pl.program_id(0)

        @pl.when(j == 0)
        def _():
            dsink_ref[...] = jnp.zeros_like(dsink_ref)

        def all_chains(dsink_vals):
            mask = _window_mask(j)
            lane = lax.broadcasted_iota(jnp.int32, (GROUP_ROWS, LANES), 1)
            low = lax.broadcasted_iota(jnp.int32, (KEYS, LANES), 1) < 64
            upper = lax.broadcasted_iota(jnp.int32, (GROUP_ROWS, 1), 0) < TIME_BLOCK
            chains = [(ex, group) for ex in range(batch) for group in range(2)]
            k_cat, v_cat, q_rows, do_b, od, s_all, dp_all = {}, {}, {}, {}, {}, {}, {}
            for ch in chains:
                ex, group = ch
                kv2 = jnp.concatenate([kvp_ref[ex], kvc_ref[ex]], axis=0).astype(F32)
                k_cat[ch] = _kv_lane_variants(kv2[:, :KV_W], group)
                v_cat[ch] = _kv_lane_variants(kv2[:, KV_W:], group)
                q_rows[ch] = _group_rows(q_ref, ex, group)
                do_rows = _group_rows(do_ref, ex, group)
                do_b[ch] = do_rows.astype(BF16)
                od[ch] = do_rows * _group_rows(o_ref, ex, group)
                s_all[ch] = _dot_nt(q_rows[ch], k_cat[ch])
                dp_all[ch] = _dot_nt(do_b[ch], v_cat[ch])
            probs, dss = {ch: [] for ch in chains}, {ch: [] for ch in chains}
            for half in range(2):
                for ch in chains:
                    ex, group = ch
                    heads = (4 * group + half, 4 * group + 2 + half)
                    sink = _per_head(sink_ref, group, half)
                    lse_h = jnp.concatenate([lse_ref[ex, :, h:h + 1] for h in heads], axis=0)
                    in_half = (lane < 64) if half == 0 else (lane >= 64)
                    delta = jnp.sum(jnp.where(in_half, od[ch], 0.0), axis=1, keepdims=True)
                    cols = slice(half * KEYS, (half + 1) * KEYS)
                    prob = jnp.exp(jnp.where(mask, s_all[ch][:, cols], MASKED) - lse_h)
                    probs[ch].append(prob.astype(BF16))
                    dss[ch].append((prob * (dp_all[ch][:, cols] - delta)).astype(BF16))
                    dsink = -jnp.exp(sink - lse_h) * delta
                    dsink_vals[heads[0]] = dsink_vals[heads[0]] + jnp.sum(jnp.where(upper, dsink, 0.0), axis=0, keepdims=True)
                    dsink_vals[heads[1]] = dsink_vals[heads[1]] + jnp.sum(jnp.where(upper, 0.0, dsink), axis=0, keepdims=True)
            dk_tile = {ex: jnp.zeros((KEYS, KV_W), F32) for ex in range(batch)}
            dv_tile = {ex: jnp.zeros((KEYS, KV_W), F32) for ex in range(batch)}
            for ch in chains:
                ex, group = ch
                ds = jnp.concatenate(dss[ch], axis=1)
                dq_rows = _dot(ds, k_cat[ch])
                dq_ref[ex, :, 2 * group * LANES:(2 * group + 1) * LANES] = dq_rows[:TIME_BLOCK]
                dq_ref[ex, :, (2 * group + 1) * LANES:(2 * group + 2) * LANES] = dq_rows[TIME_BLOCK:]
                dk_cat = _dot_tn(ds, q_rows[ch])
                dv_cat = _dot_tn(jnp.concatenate(probs[ch], axis=1), do_b[ch])
                if group == 0:
                    dk_tile[ex] = dk_tile[ex] + jnp.where(low, dk_cat[:KEYS] + pltpu.roll(dk_cat[KEYS:], 64, 1), 0.0)
                    dv_tile[ex] = dv_tile[ex] + jnp.where(low, dv_cat[:KEYS] + pltpu.roll(dv_cat[KEYS:], 64, 1), 0.0)
                else:
                    dk_tile[ex] = dk_tile[ex] + jnp.where(low, 0.0, pltpu.roll(dk_cat[:KEYS], 64, 1) + dk_cat[KEYS:])
                    dv_tile[ex] = dv_tile[ex] + jnp.where(low, 0.0, pltpu.roll(dv_cat[:KEYS], 64, 1) + dv_cat[KEYS:])

            @pl.when(j > 0)
            def _():
                for ex in range(batch):
                    dkv_ref[ex, :, :KV_W] = carry[ex, :, :KV_W] + dk_tile[ex][:TIME_BLOCK]
                    dkv_ref[ex, :, KV_W:] = carry[ex, :, KV_W:] + dv_tile[ex][:TIME_BLOCK]

            for ex in range(batch):
                carry[ex, :, :KV_W] = dk_tile[ex][TIME_BLOCK:]
                carry[ex, :, KV_W:] = dv_tile[ex][TIME_BLOCK:]

        @pl.when(j < n_blocks)
        def _():
            dsink_vals = {head: jnp.zeros((1, 1), F32) for head in range(N_HEADS)}
            all_chains(dsink_vals)
            rowid = lax.broadcasted_iota(jnp.int32, (N_HEADS, LANES), 0)
            upd = jnp.zeros((N_HEADS, LANES), F32)
            for head, val in dsink_vals.items():
                upd = upd + jnp.where(rowid == head, val, 0.0)
            dsink_ref[...] += upd

        @pl.when(j == n_blocks)
        def _():
            dkv_ref[...] = carry[...]

    last = n_blocks - 1

    def blk(width):
        return pl.BlockSpec((batch, TIME_BLOCK, width), lambda j: (0, jnp.minimum(j, last), 0))

    prev = pl.BlockSpec((batch, TIME_BLOCK, 2 * KV_W), lambda j: (0, jnp.maximum(jnp.minimum(j, last) - 1, 0), 0))
    dkv_spec = pl.BlockSpec((batch, TIME_BLOCK, 2 * KV_W), lambda j: (0, jnp.maximum(j - 1, 0), 0))
    kv3 = _by_example(kv, batch)
    dq, dkv, dsink = pl.pallas_call(
        body, name="attn_bwd", grid=(n_blocks + 1,),
        in_specs=[pl.BlockSpec(memory_space=pltpu.SMEM), blk(ATTN_W), blk(2 * KV_W), prev, blk(ATTN_W), blk(ATTN_W),
                  blk(N_HEADS)],
        out_specs=[blk(ATTN_W), dkv_spec, pl.BlockSpec((N_HEADS, LANES), lambda j: (0, 0))],
        out_shape=[_sds((batch, rows // batch, ATTN_W), F32), _sds((batch, rows // batch, 2 * KV_W), F32),
                   _sds((N_HEADS, LANES), F32)],
        scratch_shapes=[pltpu.VMEM((batch, TIME_BLOCK, 2 * KV_W), F32)],
        compiler_params=_params("arbitrary"),
    )(sinks, _by_example(q, batch), kv3, kv3, _by_example(out, batch), _by_example(d_out, batch), _by_example(lse, batch))
    return dq.reshape(rows, ATTN_W), dkv.reshape(rows, 2 * KV_W), dsink


def _conv_taps(xb, prev8):
    ext = jnp.concatenate([prev8, xb], axis=0)
    n = ext.shape[0]
    return [xb] + [pltpu.roll(ext, k, 0)[SUBLANES:n] for k in range(1, CONV_TAPS)]


def _lru_gates(xc, wa, ba, wx, bx, lam):
    xcb = xc.astype(BF16)
    r = _sigmoid(_dot(xcb, wa) + ba)
    i = _sigmoid(_dot(xcb, wx) + bx)
    sp = jnp.maximum(-lam, 0.0) + _log1p(jnp.exp(-jnp.abs(lam)))
    log_a = -LRU_C * r * sp
    a = jnp.exp(log_a)
    mult = jnp.sqrt(_one_minus_square(a, log_a))
    return xcb, r, i, sp, a, mult


def _scan_fwd(a, u, h_before):
    n, d = a.shape
    groups = n // SUBLANES
    per_seq = groups // len(h_before)
    a = a.reshape(groups, SUBLANES, d)
    u = u.reshape(groups, SUBLANES, d)
    sub = lax.broadcasted_iota(jnp.int32, a.shape, 1)
    s = 1
    while s < SUBLANES:
        valid = sub >= s
        u = jnp.where(valid, u + a * pltpu.roll(u, s, 1), u)
        a = jnp.where(valid, a * pltpu.roll(a, s, 1), a)
        s *= 2
    out, prev = [None] * groups, list(h_before)
    for g in range(per_seq):
        for k in range(len(h_before)):
            at = k * per_seq + g
            out[at] = u[at] + a[at] * prev[k]
            prev[k] = out[at][SUBLANES - 1:SUBLANES, :]
    return jnp.concatenate(out, axis=0)


def _scan_rev(cf, g, d_after):
    n, d = g.shape
    groups = n // SUBLANES
    per_seq = groups // len(d_after)
    cf = cf.reshape(groups, SUBLANES, d)
    g = g.reshape(groups, SUBLANES, d)
    sub = lax.broadcasted_iota(jnp.int32, g.shape, 1)
    s = 1
    while s < SUBLANES:
        valid = sub + s < SUBLANES
        g = jnp.where(valid, g + cf * pltpu.roll(g, SUBLANES - s, 1), g)
        cf = jnp.where(valid, cf * pltpu.roll(cf, SUBLANES - s, 1), cf)
        s *= 2
    out, nxt = [None] * groups, list(d_after)
    for i in reversed(range(per_seq)):
        for k in range(len(d_after)):
            at = k * per_seq + i
            out[at] = g[at] + cf[at] * nxt[k]
            nxt[k] = out[at][0:1, :]
    return jnp.concatenate(out, axis=0)


def _lru_fwd(xg, conv_w, conv_b, wa, ba, wx, bx, lam, batch, n_blocks):
    rows = xg.shape[0]

    def body(xg_ref, cw_ref, cb_ref, wa_ref, ba_ref, wx_ref, bx_ref, lam_ref, hs_ref, lru_ref, x_prev, h_carry):
        @pl.when(pl.program_id(0) == 0)
        def _():
            x_prev[...] = jnp.zeros_like(x_prev)
            h_carry[...] = jnp.zeros_like(h_carry)

        xb = [xg_ref[ex, :, :LRU_W] for ex in range(batch)]
        taps = [_conv_taps(xb[ex], x_prev[ex]) for ex in range(batch)]
        taps = [jnp.concatenate([t[k] for t in taps], axis=0) for k in range(CONV_TAPS)]
        xc = cb_ref[...] + sum(cw_ref[CONV_TAPS - 1 - k:CONV_TAPS - k, :] * taps[k] for k in range(CONV_TAPS))
        _, _, i, _, a, mult = _lru_gates(xc, wa_ref[...], ba_ref[...], wx_ref[...], bx_ref[...], lam_ref[...])
        h = _scan_fwd(a, mult * i * xc, [h_carry[ex] for ex in range(batch)])
        gate = _gelu(jnp.concatenate([xg_ref[ex, :, LRU_W:] for ex in range(batch)], axis=0))
        lru = h * gate
        for ex in range(batch):
            rows_ex = slice(ex * TIME_BLOCK, (ex + 1) * TIME_BLOCK)
            hs_ref[ex] = h[rows_ex]
            lru_ref[ex] = lru[rows_ex]
            h_carry[ex] = h[(ex + 1) * TIME_BLOCK - 1:(ex + 1) * TIME_BLOCK, :]
            x_prev[ex] = xb[ex][TIME_BLOCK - SUBLANES:TIME_BLOCK, :]

    def blk(width):
        return pl.BlockSpec((batch, TIME_BLOCK, width), lambda j: (0, j, 0))

    small = [conv_w, conv_b, wa, ba, wx, bx, lam]
    hs, lru = pl.pallas_call(
        body, name="lru_fwd", grid=(n_blocks,),
        in_specs=[blk(2 * LRU_W)] + [_whole_spec(a) for a in small],
        out_specs=[blk(LRU_W), blk(LRU_W)],
        out_shape=[_sds((batch, rows // batch, LRU_W), F32)] * 2,
        scratch_shapes=[pltpu.VMEM((batch, SUBLANES, LRU_W), F32), pltpu.VMEM((batch, 1, LRU_W), F32)],
        compiler_params=_params("arbitrary"),
    )(_by_example(xg, batch), *small)
    return hs.reshape(rows, LRU_W), lru.reshape(rows, LRU_W)


def _lru_bwd(xg, hs, d_lru, conv_w, conv_b, wa, ba, wx, bx, lam, batch, n_blocks):
    rows = xg.shape[0]
    last_row = TIME_BLOCK - 1

    def body(xg_ref, xgh_ref, hs_ref, hsh_ref, dl_ref, cw_ref, cb_ref, wa_ref, ba_ref, wx_ref, bx_ref, lam_ref,
             dxg_ref, dwa_ref, dwx_ref, vec_ref, dh_carry, dxc_next):
        j = pl.program_id(0)

        @pl.when(j == 0)
        def _():
            dwa_ref[...] = jnp.zeros_like(dwa_ref)
            dwx_ref[...] = jnp.zeros_like(dwx_ref)
            vec_ref[...] = jnp.zeros_like(vec_ref)
            dh_carry[...] = jnp.zeros_like(dh_carry)
            dxc_next[...] = jnp.zeros_like(dxc_next)

        examples = range(batch)
        first = j == n_blocks - 1
        xb = [xg_ref[ex, :, :LRU_W] for ex in examples]
        taps = [_conv_taps(xb[ex], jnp.where(first, 0.0, xgh_ref[ex, :, :LRU_W])) for ex in examples]
        taps = [jnp.concatenate([t[k] for t in taps], axis=0) for k in range(CONV_TAPS)]
        cw = cw_ref[...]
        lam = lam_ref[...]
        wa = wa_ref[...]
        wx = wx_ref[...]
        xc = cb_ref[...] + sum(cw[CONV_TAPS - 1 - k:CONV_TAPS - k, :] * taps[k] for k in range(CONV_TAPS))
        xcb, r, i, sp, a, mult = _lru_gates(xc, wa, ba_ref[...], wx, bx_ref[...], lam)
        hs = jnp.concatenate([hs_ref[ex] for ex in examples], axis=0)
        row = jnp.bitwise_and(lax.broadcasted_iota(jnp.int32, hs.shape, 0), TIME_BLOCK - 1)
        h_before = jnp.concatenate([jnp.broadcast_to(jnp.where(first, 0.0, hsh_ref[ex, SUBLANES - 1:SUBLANES, :]),
                                                     (TIME_BLOCK, LRU_W)) for ex in examples], axis=0)
        h_prev = jnp.where(row == 0, h_before, pltpu.roll(hs, 1, 0))
        dl = jnp.concatenate([dl_ref[ex] for ex in examples], axis=0)
        gate, dgate = _gelu_and_grad(jnp.concatenate([xg_ref[ex, :, LRU_W:] for ex in examples], axis=0))
        dgb = (dl * hs * dgate).astype(BF16)
        cf = jnp.where(row == last_row, 1.0, pltpu.roll(a, batch * TIME_BLOCK - 1, 0))
        dh = _scan_rev(cf, dl * gate, [dh_carry[ex] for ex in examples])
        dmult = dh * i * xc
        di = dh * mult * xc
        dxc = dh * mult * i
        dlog_a = dh * h_prev * a - dmult * (a * a / mult)
        dr = dlog_a * (-LRU_C * sp)
        dlam = jnp.sum(dlog_a * (-LRU_C * r), axis=0, keepdims=True) * (-_sigmoid(-lam))
        dpr = dr * r * (1.0 - r)
        dpi = di * i * (1.0 - i)
        dprb = dpr.astype(BF16)
        dpib = dpi.astype(BF16)
        dxc = dxc + _dot_nt(dprb, wa) + _dot_nt(dpib, wx)
        dwa_ref[...] += _dot_tn(xcb, dprb)
        dwx_ref[...] += _dot_tn(xcb, dpib)
        n = TIME_BLOCK + SUBLANES
        later = []
        for k in range(1, CONV_TAPS):
            shifted = [pltpu.roll(jnp.concatenate([dxc[ex * TIME_BLOCK:(ex + 1) * TIME_BLOCK], dxc_next[ex]], axis=0),
                                  n - k, 0)[:TIME_BLOCK] for ex in examples]
            later.append(jnp.concatenate(shifted, axis=0))
        dxb = cw[CONV_TAPS - 1:CONV_TAPS, :] * dxc
        for k in range(1, CONV_TAPS):
            dxb = dxb + cw[CONV_TAPS - 1 - k:CONV_TAPS - k, :] * later[k - 1]
        dxb = dxb.astype(BF16)
        for ex in examples:
            rows_ex = slice(ex * TIME_BLOCK, (ex + 1) * TIME_BLOCK)
            dxg_ref[ex, :, :LRU_W] = dxb[rows_ex]
            dxg_ref[ex, :, LRU_W:] = dgb[rows_ex]
            dh_carry[ex] = a[ex * TIME_BLOCK:ex * TIME_BLOCK + 1, :] * dh[ex * TIME_BLOCK:ex * TIME_BLOCK + 1, :]
            dxc_next[ex] = dxc[ex * TIME_BLOCK:ex * TIME_BLOCK + SUBLANES, :]
        vecs = [jnp.sum(dxc * taps[CONV_TAPS - 1 - t], axis=0, keepdims=True) for t in range(CONV_TAPS)]
        vecs += [jnp.sum(dxc, axis=0, keepdims=True), jnp.sum(dpr, axis=0, keepdims=True),
                 jnp.sum(dpi, axis=0, keepdims=True), dlam]
        vec_ref[...] += _put_rows(SUBLANES, vecs)

    def tblk(j):
        return n_blocks - 1 - j

    def blk(width):
        return pl.BlockSpec((batch, TIME_BLOCK, width), lambda j: (0, tblk(j), 0))

    per8 = TIME_BLOCK // SUBLANES

    def halo(width):
        return pl.BlockSpec((batch, SUBLANES, width), lambda j: (0, jnp.maximum(per8 * tblk(j) - 1, 0), 0))

    small = [conv_w, conv_b, wa, ba, wx, bx, lam]
    acc = lambda shape: pl.BlockSpec(shape, lambda j: (0, 0))
    xg3, hs3 = _by_example(xg, batch), _by_example(hs, batch)
    dxg, dwa, dwx, vec = pl.pallas_call(
        body, name="lru_bwd", grid=(n_blocks,),
        in_specs=[blk(2 * LRU_W), halo(2 * LRU_W), blk(LRU_W), halo(LRU_W), blk(LRU_W)] + [_whole_spec(a) for a in small],
        out_specs=[blk(2 * LRU_W), acc((LRU_W, LRU_W)), acc((LRU_W, LRU_W)), acc((SUBLANES, LRU_W))],
        out_shape=[_sds((batch, rows // batch, 2 * LRU_W), BF16), _sds((LRU_W, LRU_W), F32), _sds((LRU_W, LRU_W), F32),
                   _sds((SUBLANES, LRU_W), F32)],
        scratch_shapes=[pltpu.VMEM((batch, 1, LRU_W), F32), pltpu.VMEM((batch, SUBLANES, LRU_W), F32)],
        compiler_params=_params("arbitrary"),
    )(xg3, xg3, hs3, hs3, _by_example(d_lru, batch), *small)
    return dxg.reshape(rows, 2 * LRU_W), dwa, dwx, vec


def _out_mlp_fwd(attn, lru, h, w_out, w_up4, w_down, g_attn, g_lru, g_post_mix, g_pre_mlp, g_post_mlp):
    rows = h.shape[0]

    def body(at_ref, lr_ref, h_ref, wo_ref, wu_ref, wd_ref, ga_ref, gl_ref, gp_ref, g1_ref, g2_ref,
             grp_ref, o_ref, hm_ref, z_ref, slope_ref, act_ref, y_ref, ho_ref):
        a = _rms(at_ref[...], ga_ref[...]).astype(BF16)
        l = _rms(lr_ref[...], gl_ref[...]).astype(BF16)
        grp_ref[:, :ATTN_W] = a
        grp_ref[:, ATTN_W:] = l
        o = _dot(a, wo_ref[:ATTN_W, :]) + _dot(l, wo_ref[ATTN_W:, :])
        o_ref[...] = o
        x = h_ref[...] + _rms(o, gp_ref[...])
        hm_ref[...] = x
        z = _rms(x, g1_ref[...]).astype(BF16)
        z_ref[...] = z
        y = jnp.zeros((ROW_TILE, D_MODEL), F32)
        for s in range(N_CHIPS):
            cols = slice(s * D_MODEL, (s + 1) * D_MODEL)
            r = jnp.maximum(_dot(z, wu_ref[s]), 0.0)
            slope_ref[:, cols] = (2.0 * r).astype(BF16)
            act = jnp.square(r).astype(BF16)
            act_ref[:, cols] = act
            y = y + _dot(act, wd_ref[cols, :])
        y_ref[...] = y
        ho_ref[...] = x + _rms(y, g2_ref[...])

    gains = [g_attn, g_lru, g_post_mix, g_pre_mlp, g_post_mlp]
    return pl.pallas_call(
        body, name="out_mlp_fwd", grid=(rows // ROW_TILE,),
        in_specs=[_row_spec(ATTN_W), _row_spec(LRU_W), _row_spec(D_MODEL), _resident_spec(w_out), _resident_spec(w_up4),
                  _resident_spec(w_down)] + [_whole_spec(g) for g in gains],
        out_specs=[_row_spec(D_MODEL), _row_spec(D_MODEL), _row_spec(D_MODEL), _row_spec(D_MODEL), _row_spec(D_FF),
                   _row_spec(D_FF), _row_spec(D_MODEL), _row_spec(D_MODEL)],
        out_shape=[_sds((rows, D_MODEL), BF16), _sds((rows, D_MODEL), F32), _sds((rows, D_MODEL), F32),
                   _sds((rows, D_MODEL), BF16), _sds((rows, D_FF), BF16), _sds((rows, D_FF), BF16),
                   _sds((rows, D_MODEL), F32), _sds((rows, D_MODEL), F32)],
        compiler_params=_params("parallel"),
    )(attn, lru, h, w_out, w_up4, w_down, *gains)


def _mlp_out_bwd(dh_out, h_mid, y, slope, o, attn, lru, w_out, w_up4, w_down, g_attn, g_lru, g_post_mix, g_pre_mlp,
                 g_post_mlp):
    rows = y.shape[0]

    def body(dh_ref, hm_ref, y_ref, slope_ref, o_ref, at_ref, lr_ref, wo_ref, wu_ref, wd_ref, ga_ref, gl_ref, gp_ref,
             g1_ref, g2_ref, dhm_ref, dy_ref, dup_ref, do_ref, dat_ref, dlr_ref, gacc_ref):
        @pl.when(pl.program_id(0) == 0)
        def _():
            gacc_ref[...] = jnp.zeros_like(gacc_ref)

        dh = dh_ref[...]
        dy, dg2 = _rms_bwd(dh, y_ref[...], g2_ref[...])
        dyb = dy.astype(BF16)
        dy_ref[...] = dyb
        dz = jnp.zeros((ROW_TILE, D_MODEL), F32)
        for s in range(N_CHIPS):
            cols = slice(s * D_MODEL, (s + 1) * D_MODEL)
            dact = _dot_nt(dyb, wd_ref[cols, :])
            dup = (dact * slope_ref[:, cols].astype(F32)).astype(BF16)
            dup_ref[:, cols] = dup
            dz = dz + _dot_nt(dup, wu_ref[s])
        dx, dg1 = _rms_bwd(dz, hm_ref[...], g1_ref[...])
        dhm = dh + dx
        dhm_ref[...] = dhm
        do, dgp = _rms_bwd(dhm, o_ref[...], gp_ref[...])
        dob = do.astype(BF16)
        do_ref[...] = dob
        dat, dga = _rms_bwd(_dot_nt(dob, wo_ref[:ATTN_W, :]), at_ref[...], ga_ref[...])
        dlr, dgl = _rms_bwd(_dot_nt(dob, wo_ref[ATTN_W:, :]), lr_ref[...], gl_ref[...])
        dat_ref[...] = dat
        dlr_ref[...] = dlr
        gacc_ref[...] += _put_rows(SUBLANES, [dg1, dg2, dgp, jnp.concatenate([dga, dgl], axis=1)])

    gains = [g_attn, g_lru, g_post_mix, g_pre_mlp, g_post_mlp]
    return pl.pallas_call(
        body, name="mlp_out_bwd", grid=(rows // ROW_TILE,),
        in_specs=[_row_spec(D_MODEL), _row_spec(D_MODEL), _row_spec(D_MODEL), _row_spec(D_FF), _row_spec(D_MODEL),
                  _row_spec(ATTN_W), _row_spec(LRU_W), _resident_spec(w_out), _resident_spec(w_up4), _resident_spec(w_down)]
                 + [_whole_spec(g) for g in gains],
        out_specs=[_row_spec(D_MODEL), _row_spec(D_MODEL), _row_spec(D_FF), _row_spec(D_MODEL), _row_spec(ATTN_W),
                   _row_spec(LRU_W), pl.BlockSpec((SUBLANES, D_MODEL), lambda i: (0, 0))],
        out_shape=[_sds((rows, D_MODEL), F32), _sds((rows, D_MODEL), BF16), _sds((rows, D_FF), BF16),
                   _sds((rows, D_MODEL), BF16), _sds((rows, ATTN_W), F32), _sds((rows, LRU_W), F32),
                   _sds((SUBLANES, D_MODEL), F32)],
        compiler_params=_params("arbitrary"),
    )(dh_out, h_mid, y, slope, o, attn, lru, w_out, w_up4, w_down, *gains)


def _matmul_tn(a, b, tm, tn, name, out_dtype, column_blocks=False):
    rows, m = a.shape
    n = b.shape[1]

    def body(a_ref, b_ref, o_ref):
        o_ref[...] = _dot_tn(a_ref[...], b_ref[...]).astype(out_dtype)

    if column_blocks:
        out_spec = pl.BlockSpec((None, tm, tn), lambda i, j: (j, i, 0))
        out_shape = _sds((n // tn, m, tn), out_dtype)
    else:
        out_spec = pl.BlockSpec((tm, tn), lambda i, j: (i, j))
        out_shape = _sds((m, n), out_dtype)
    return pl.pallas_call(
        body, name=name, grid=(m // tm, n // tn),
        in_specs=[pl.BlockSpec((rows, tm), lambda i, j: (0, i)), pl.BlockSpec((rows, tn), lambda i, j: (0, j))],
        out_specs=out_spec, out_shape=out_shape,
        compiler_params=_params("parallel", "parallel"),
    )(a, b)


LOSS_COLS = 256


def _loss_head(h, loss_target):
    batch, per_example, _ = h.shape
    seq = loss_target.shape[1]
    n_real = N_META + seq
    assert seq % SUBLANES == 0

    def body(h_ref, t_ref, dh_ref, l_ref):
        @pl.when((pl.program_id(0) == 0) & (pl.program_id(1) == 0))
        def _():
            l_ref[...] = jnp.zeros_like(l_ref)

        d = h_ref[N_META:n_real, :] - t_ref[...]
        dh_ref[:N_META, :] = jnp.zeros((N_META, LOSS_COLS), F32)
        dh_ref[N_META:n_real, :] = d * (1.0 / D_MODEL)
        dh_ref[n_real:, :] = jnp.zeros((per_example - n_real, LOSS_COLS), F32)
        l_ref[...] += jnp.sum(jnp.sum(d * d, axis=0, keepdims=True), axis=1, keepdims=True)

    blk = pl.BlockSpec((None, per_example, LOSS_COLS), lambda b, j: (b, 0, j))
    return pl.pallas_call(
        body, name="loss_head", grid=(batch, D_MODEL // LOSS_COLS),
        in_specs=[blk, pl.BlockSpec((None, seq, LOSS_COLS), lambda b, j: (b, 0, j))],
        out_specs=[blk, pl.BlockSpec((SUBLANES, LANES), lambda b, j: (0, 0))],
        out_shape=[_sds(h.shape, F32), _sds((SUBLANES, LANES), F32)],
        compiler_params=_params("arbitrary", "arbitrary"),
    )(h, loss_target)


def _meta_grad(dh0, batch, rows_per_example):
    per = rows_per_example // N_META

    def body(d_ref, o_ref):
        @pl.when(pl.program_id(0) == 0)
        def _():
            o_ref[...] = jnp.zeros_like(o_ref)

        o_ref[...] += d_ref[...]

    return pl.pallas_call(
        body, name="meta_grad", grid=(batch,),
        in_specs=[pl.BlockSpec((N_META, D_MODEL), lambda b: (b * per, 0))],
        out_specs=pl.BlockSpec((N_META, D_MODEL), lambda b: (0, 0)),
        out_shape=_sds((N_META, D_MODEL), F32),
        compiler_params=_params("arbitrary"),
    )(dh0)


def _elementwise_tile(rows, cols):
    tile = rows
    while tile * cols * 4 > (2 << 20) and tile % 16 == 0:
        tile //= 2
    return tile


def _sum_slots(buf, name):
    k, rows, cols = buf.shape
    tile = _elementwise_tile(rows, cols)

    def body(*refs):
        total = refs[0][...].astype(F32)
        for r in refs[1:k]:
            total = total + r[...].astype(F32)
        refs[k][...] = total

    def slot(s):
        return pl.BlockSpec((None, tile, cols), lambda i: (s, i, 0))

    return pl.pallas_call(
        body, name=name, grid=(rows // tile,),
        in_specs=[slot(s) for s in range(k)], out_specs=pl.BlockSpec((tile, cols), lambda i: (i, 0)),
        out_shape=_sds((rows, cols), F32), compiler_params=_params("parallel"),
    )(*([buf] * k))


def _sum_pieces(pieces, landed, chip, layer, n_layers, stacked):
    count = len(pieces)
    _, rows, cols = pieces[0].shape
    tile = _elementwise_tile(rows, cols)

    def body(chip_ref, *refs):
        token = refs[-1]
        for t in range(count):
            own_ref, a_ref, b_ref, c_ref = refs[4 * t:4 * t + 4]
            total = ((own_ref[...].astype(F32) + a_ref[...].astype(F32)) + b_ref[...].astype(F32)) + c_ref[...].astype(F32)
            refs[-1 - count + t][...] = total.astype(BF16)
        token[...] = jnp.zeros_like(token)

    def slot(offset):
        return pl.BlockSpec((None, tile, cols), lambda i, chip_ref: ((chip_ref[0] + offset) % N_CHIPS, i, 0))

    carried = [] if stacked is None else list(stacked)
    operands = []
    for p, l in zip(pieces, landed):
        operands += [p, l, l, l]
    grid_spec = pltpu.PrefetchScalarGridSpec(
        num_scalar_prefetch=1, grid=(rows // tile,),
        in_specs=[slot(0), slot(1), slot(2), slot(3)] * count + [pl.BlockSpec(memory_space=pl.ANY)] * len(carried),
        out_specs=[pl.BlockSpec((None, tile, cols), lambda i, chip_ref: (layer, i, 0))] * count
                  + [pl.BlockSpec((SUBLANES, LANES), lambda i, chip_ref: (0, 0))])
    out = pl.pallas_call(
        body, name="sum_grad_pieces", grid_spec=grid_spec,
        out_shape=[_sds((n_layers, rows, cols), BF16)] * count + [_sds((SUBLANES, LANES), F32)],
        input_output_aliases={1 + 4 * count + t: t for t in range(len(carried))},
        compiler_params=_params("arbitrary"),
    )(chip, *operands, *carried)
    return out[:count], out[count]


def _adamw(w, m, v, grads, name):
    flat = w.ndim == 2
    if flat:
        w, m, v = w[None], m[None], v[None]
        grads = [g[None] for g in grads]
    layers, rows, cols = w.shape
    tile = _elementwise_tile(rows, cols)
    ng = len(grads)
    m_scale = 1.0 - ADAM_B1 ** ADAM_STEP
    v_scale = 1.0 - ADAM_B2 ** ADAM_STEP

    def body(*refs):
        w_ref, m_ref, v_ref = refs[:3]
        g_refs = refs[3:3 + ng]
        g_out, d_out, m_out, v_out = refs[-4:]
        g = g_refs[0][...].astype(F32)
        for r in g_refs[1:]:
            g = g + r[...].astype(F32)
        m_new = ADAM_B1 * m_ref[...] + (1.0 - ADAM_B1) * g
        v_new = ADAM_B2 * v_ref[...] + (1.0 - ADAM_B2) * (g * g)
        m_hat = m_new / m_scale
        v_hat = v_new / v_scale
        g_out[...] = g
        d_out[...] = -ADAM_LR * (m_hat / (jnp.sqrt(v_hat) + ADAM_EPS) + ADAM_WD * w_ref[...])
        m_out[...] = m_new
        v_out[...] = v_new

    spec = pl.BlockSpec((None, tile, cols), lambda l, i: (l, i, 0))
    out = pl.pallas_call(
        body, name=name, grid=(layers, rows // tile),
        in_specs=[spec] * (3 + ng), out_specs=[spec] * 4, out_shape=[_sds(w.shape, F32)] * 4,
        compiler_params=_params("parallel", "parallel"),
    )(w, m, v, *grads)
    return [o[0] for o in out] if flat else out


def _position():
    return lax.axis_index("x"), lax.axis_index("y"), lax.axis_index("c")


def _other_chips(x, y):
    return [(1 - x, y), (x, 1 - y), (1 - x, 1 - y)]


def _exchange_chips(arrays, name, scatter):
    n = len(arrays)

    def body(*refs):
        src, dst = refs[:n], refs[n:2 * n]
        send_sems, recv_sems, local_sems = refs[2 * n:]
        x, y, c = _position()
        mine = 2 * x + y
        copies = []
        for i in range(n):
            own = src[i].at[mine] if scatter else src[i]
            copies.append(pltpu.make_async_copy(own, dst[i].at[mine], local_sems.at[i]))
        for k, (px, py) in enumerate(_other_chips(x, y)):
            for i in range(n):
                piece = src[i].at[2 * px + py] if scatter else src[i]
                copies.append(pltpu.make_async_remote_copy(
                    src_ref=piece, dst_ref=dst[i].at[mine], send_sem=send_sems.at[k, i], recv_sem=recv_sems.at[k, i],
                    device_id=(px, py, c), device_id_type=MESH))
        for cp in copies:
            cp.start()
        for cp in copies:
            cp.wait()

    def out_shape(a):
        return _sds(a.shape if scatter else (N_CHIPS,) + a.shape, a.dtype)

    return pl.pallas_call(
        body, name=name, in_specs=[HBM_SPEC] * n, out_specs=[HBM_SPEC] * n, out_shape=[out_shape(a) for a in arrays],
        scratch_shapes=[pltpu.SemaphoreType.DMA((N_CHIPS - 1, n)), pltpu.SemaphoreType.DMA((N_CHIPS - 1, n)),
                        pltpu.SemaphoreType.DMA((n,))],
    )(*arrays)


def _swap_sibling(arrays, name):
    n = len(arrays)

    def body(*refs):
        src, dst = refs[:n], refs[n:2 * n]
        send_sems, recv_sems = refs[2 * n:]
        x, y, c = _position()
        copies = [pltpu.make_async_remote_copy(
            src_ref=src[i], dst_ref=dst[i], send_sem=send_sems.at[i], recv_sem=recv_sems.at[i],
            device_id=(x, y, 1 - c), device_id_type=MESH) for i in range(n)]
        for cp in copies:
            cp.start()
        for cp in copies:
            cp.wait()

    return pl.pallas_call(
        body, name=name, in_specs=[HBM_SPEC] * n, out_specs=[HBM_SPEC] * n,
        out_shape=[_sds(a.shape, a.dtype) for a in arrays],
        scratch_shapes=[pltpu.SemaphoreType.DMA((n,)), pltpu.SemaphoreType.DMA((n,))],
    )(*arrays)


SEM_SPEC = pl.BlockSpec(memory_space=pltpu.SEMAPHORE)
ANY_SPEC = pl.BlockSpec(memory_space=pl.ANY)
IN_FLIGHT = pltpu.SideEffectType.DATAFLOW_SIDE_EFFECTING


def _peer_sems(all_devices):
    return pltpu.SemaphoreType.DMA(((N_DEV if all_devices else N_CHIPS) - 1,))


def _own_slot(all_devices, x, y, c):
    return 4 * x + 2 * y + c if all_devices else 2 * x + y


def _peers(all_devices, x, y, c):
    if not all_devices:
        return [((px, py, c), 2 * px + py) for px, py in _other_chips(x, y)]
    out = []
    for fx in range(2):
        for fy in range(2):
            for fc in range(2):
                if fx or fy or fc:
                    px, py, pc = (1 - x if fx else x), (1 - y if fy else y), (1 - c if fc else c)
                    out.append(((px, py, pc), 4 * px + 2 * py + pc))
    return out


def _in_hbm(a):
    return pltpu.with_memory_space_constraint(a, pltpu.HBM)


def _place_slot(a, slot, n_slots, name):
    rows, cols = a.shape
    tile = _elementwise_tile(rows, cols)

    def body(slot_ref, a_ref, o_ref):
        o_ref[...] = a_ref[...]

    grid_spec = pltpu.PrefetchScalarGridSpec(
        num_scalar_prefetch=1, grid=(rows // tile,),
        in_specs=[pl.BlockSpec((tile, cols), lambda i, slot_ref: (i, 0))],
        out_specs=pl.BlockSpec((None, tile, cols), lambda i, slot_ref: (slot_ref[0], i, 0)))
    return pl.pallas_call(
        body, name=name, grid_spec=grid_spec, out_shape=_sds((n_slots, rows, cols), a.dtype),
        compiler_params=_params("parallel"),
    )(slot, a)


def _place_shard(ws, layer, chip, name):
    count = len(ws)
    _, rows, cols = ws[0].shape
    tile = _elementwise_tile(rows, cols)

    def body(chip_ref, *refs):
        for t in range(count):
            refs[count + t][...] = refs[t][...].astype(BF16)

    grid_spec = pltpu.PrefetchScalarGridSpec(
        num_scalar_prefetch=1, grid=(rows // tile,),
        in_specs=[pl.BlockSpec((None, tile, cols), lambda i, chip_ref: (layer, i, 0))] * count,
        out_specs=[pl.BlockSpec((None, tile, cols), lambda i, chip_ref: (chip_ref[0], i, 0))] * count)
    return pl.pallas_call(
        body, name=name, grid_spec=grid_spec, out_shape=[_sds((N_CHIPS, rows, cols), BF16)] * count,
        compiler_params=_params("parallel"),
    )(chip, *ws)


def _rows_of(ref, slot, core, halves):
    if not halves:
        return ref.at[slot]
    half = ref.shape[1] // 2
    return ref.at[slot, pl.ds(pl.multiple_of(core * half, half), half)]


def _gather_start(bufs, after, name, all_devices=False, halves=False):
    n = len(bufs)

    def body(*refs):
        buf = refs[:n]
        send, recv = refs[n + 1:2 * n + 1], refs[2 * n + 1:3 * n + 1]
        token = refs[4 * n + 1]
        x, y, c = _position()
        mine = _own_slot(all_devices, x, y, c)
        for i in range(n):
            for k, (peer, _) in enumerate(_peers(all_devices, x, y, c)):
                rows = _rows_of(buf[i], mine, c, halves)
                pltpu.make_async_remote_copy(
                    src_ref=rows, dst_ref=rows, send_sem=send[i].at[k], recv_sem=recv[i].at[k],
                    device_id=peer, device_id_type=MESH).start()
        token[...] = jnp.zeros_like(token)

    sems = _peer_sems(all_devices)
    out = pl.pallas_call(
        body, name=name, in_specs=[HBM_SPEC] * n + [ANY_SPEC],
        out_specs=[SEM_SPEC] * (2 * n) + [HBM_SPEC] * n + [pl.BlockSpec(memory_space=pltpu.VMEM)],
        out_shape=[sems] * (2 * n) + [pltpu.HBM(b.shape, b.dtype) for b in bufs] + [_sds((SUBLANES, LANES), F32)],
        input_output_aliases={i: 2 * n + i for i in range(n)},
        compiler_params=pltpu.CompilerParams(has_side_effects=IN_FLIGHT),
    )(*[_in_hbm(b) for b in bufs], after)
    return [(out[i], out[n + i], out[2 * n + i]) for i in range(n)], out[3 * n]


def _gather_relay(flights, after, name):
    n = len(flights)

    def body(*refs):
        buf, send, recv = refs[:n], refs[n:2 * n], refs[2 * n:3 * n]
        send2, recv2 = refs[3 * n + 1:4 * n + 1], refs[4 * n + 1:5 * n + 1]
        token = refs[6 * n + 1]
        x, y, c = _position()
        mine = 2 * x + y
        for i in range(n):
            for k, (peer, slot) in enumerate(_peers(False, x, y, c)):
                arrived = _rows_of(buf[i], slot, c, True)
                cp = pltpu.make_async_remote_copy(
                    src_ref=_rows_of(buf[i], mine, c, True), dst_ref=arrived, send_sem=send[i].at[k], recv_sem=recv[i].at[k],
                    device_id=peer, device_id_type=MESH)
                cp.wait_send()
                cp.wait_recv()
                pltpu.make_async_remote_copy(
                    src_ref=arrived, dst_ref=arrived, send_sem=send2[i].at[k], recv_sem=recv2[i].at[k],
                    device_id=(x, y, 1 - c), device_id_type=MESH).start()
        token[...] = jnp.zeros_like(token)

    bufs = [f[2] for f in flights]
    sems = _peer_sems(False)
    out = pl.pallas_call(
        body, name=name, in_specs=[HBM_SPEC] * n + [SEM_SPEC] * (2 * n) + [ANY_SPEC],
        out_specs=[SEM_SPEC] * (2 * n) + [HBM_SPEC] * n + [pl.BlockSpec(memory_space=pltpu.VMEM)],
        out_shape=[sems] * (2 * n) + [pltpu.HBM(b.shape, b.dtype) for b in bufs] + [_sds((SUBLANES, LANES), F32)],
        input_output_aliases={i: 2 * n + i for i in range(n)},
        compiler_params=pltpu.CompilerParams(has_side_effects=IN_FLIGHT),
    )(*bufs, *[f[0] for f in flights], *[f[1] for f in flights], after)
    return [(out[i], out[n + i], out[2 * n + i]) for i in range(n)], out[3 * n]


def _gather_wait(send, recv, buf, after, name, all_devices=False, relayed=False):
    def body(buf_ref, send_ref, recv_ref, after_ref, out_ref):
        x, y, c = _position()
        mine = _own_slot(all_devices, x, y, c)
        for k, (peer, slot) in enumerate(_peers(all_devices, x, y, c)):
            if relayed:
                cp = pltpu.make_async_remote_copy(
                    src_ref=_rows_of(buf_ref, slot, c, True), dst_ref=_rows_of(buf_ref, slot, 1 - c, True),
                    send_sem=send_ref.at[k], recv_sem=recv_ref.at[k], device_id=(x, y, 1 - c), device_id_type=MESH)
            else:
                cp = pltpu.make_async_remote_copy(
                    src_ref=buf_ref.at[mine], dst_ref=buf_ref.at[slot], send_sem=send_ref.at[k], recv_sem=recv_ref.at[k],
                    device_id=peer, device_id_type=MESH)
            cp.wait_send()
            cp.wait_recv()

    return pl.pallas_call(
        body, name=name, in_specs=[HBM_SPEC, SEM_SPEC, SEM_SPEC, ANY_SPEC], out_specs=HBM_SPEC,
        out_shape=pltpu.HBM(buf.shape, buf.dtype), input_output_aliases={0: 0},
        compiler_params=pltpu.CompilerParams(has_side_effects=IN_FLIGHT),
    )(buf, send, recv, after)


def _scatter_start(pieces, name):
    n = len(pieces)

    def body(*refs):
        src = refs[:n]
        send, recv = refs[n:2 * n], refs[2 * n:3 * n]
        land = refs[4 * n:5 * n]
        token = refs[5 * n]
        x, y, c = _position()
        mine = 2 * x + y
        for i in range(n):
            for k, (px, py) in enumerate(_other_chips(x, y)):
                pltpu.make_async_remote_copy(
                    src_ref=src[i].at[2 * px + py], dst_ref=land[i].at[mine], send_sem=send[i].at[k], recv_sem=recv[i].at[k],
                    device_id=(px, py, c), device_id_type=MESH).start()
        token[...] = jnp.zeros_like(token)

    hbm = [pltpu.HBM(p.shape, p.dtype) for p in pieces]
    out = pl.pallas_call(
        body, name=name, in_specs=[HBM_SPEC] * n,
        out_specs=[SEM_SPEC] * (2 * n) + [HBM_SPEC] * (2 * n) + [pl.BlockSpec(memory_space=pltpu.VMEM)],
        out_shape=[_peer_sems(False)] * (2 * n) + hbm + hbm + [_sds((SUBLANES, LANES), F32)],
        input_output_aliases={i: 2 * n + i for i in range(n)},
        compiler_params=pltpu.CompilerParams(has_side_effects=IN_FLIGHT),
    )(*[_in_hbm(p) for p in pieces])
    return [(out[i], out[n + i], out[2 * n + i], out[3 * n + i]) for i in range(n)], out[4 * n]


def _scatter_wait(send, recv, pieces, land, after, name):
    def body(src_ref, land_ref, send_ref, recv_ref, after_ref, src_out, land_out):
        x, y, c = _position()
        for k, (px, py) in enumerate(_other_chips(x, y)):
            cp = pltpu.make_async_remote_copy(
                src_ref=src_ref.at[2 * px + py], dst_ref=land_ref.at[2 * px + py], send_sem=send_ref.at[k],
                recv_sem=recv_ref.at[k], device_id=(px, py, c), device_id_type=MESH)
            cp.wait_send()
            cp.wait_recv()

    return pl.pallas_call(
        body, name=name, in_specs=[HBM_SPEC, HBM_SPEC, SEM_SPEC, SEM_SPEC, ANY_SPEC], out_specs=[HBM_SPEC, HBM_SPEC],
        out_shape=[pltpu.HBM(pieces.shape, pieces.dtype), pltpu.HBM(land.shape, land.dtype)],
        input_output_aliases={0: 0, 1: 1},
        compiler_params=pltpu.CompilerParams(has_side_effects=IN_FLIGHT),
    )(pieces, land, send, recv, after)


def _rope_tables(batch, rows_per_example):
    inv_freq = ROPE_THETA ** (-jnp.arange(0, 64, 2, dtype=F32) / 64)
    ang = jnp.arange(rows_per_example, dtype=F32)[:, None] * inv_freq[None, :]
    cos, sin = jnp.cos(ang), jnp.sin(ang)
    cos128 = jnp.concatenate([cos, cos, cos, cos], axis=1)
    sin128 = jnp.concatenate([-sin, sin, -sin, sin], axis=1)
    return jnp.tile(cos128, (batch, 1)), jnp.tile(sin128, (batch, 1))


def _block_diagonal(w):
    eye = jnp.eye(LRU_BLOCKS, dtype=w.dtype)
    return (w[:, :, None, :] * eye[:, None, :, None]).reshape(LRU_W, LRU_W)


def _diagonal_blocks(dense):
    d4 = dense.reshape(LRU_BLOCKS, LRU_BLOCK, LRU_BLOCKS, LRU_BLOCK)
    return jnp.stack([d4[n, :, n, :] for n in range(LRU_BLOCKS)])


def _row(v):
    return v.reshape(1, -1)


def _local_step(x, loss_target, meta_tokens, small, depth, big_weight, on_layer_grads):
    batch, seq, _ = x.shape
    n_real = N_META + seq
    n_blocks = -(-n_real // TIME_BLOCK)
    per_example = n_blocks * TIME_BLOCK
    pad = per_example - n_real
    meta = jnp.broadcast_to(meta_tokens[None], (batch, N_META, D_MODEL))
    h = jnp.concatenate([meta, x, jnp.zeros((batch, pad, D_MODEL), F32)], axis=1).reshape(batch * per_example, D_MODEL)
    rope_cos, rope_sin = _rope_tables(batch, per_example)

    saved = []
    for l in range(depth):
        wa = _block_diagonal(small['gate_a_w'][l]).astype(BF16)
        wx = _block_diagonal(small['gate_x_w'][l]).astype(BF16)
        lru_small = (small['conv_w'][l], _row(small['conv_b'][l]), wa, _row(small['gate_a_b'][l]), wx,
                     _row(small['gate_x_b'][l]), _row(small['lru_lambda'][l]))
        w_in = big_weight('w_in', l, h)
        z1, q, kv, xg = _in_proj_fwd(h, _row(small['pre_mix_norm'][l]), w_in, rope_cos, rope_sin)
        attn, lse = _attn_fwd(q, kv, small['attn_sinks'][l], batch, n_blocks)
        hs, lru = _lru_fwd(xg, *lru_small, batch, n_blocks)
        w_out = big_weight('w_out', l, xg)
        w_up4 = big_weight('w_up', l, xg)
        w_down = big_weight('w_down', l, xg)
        gains = [_row(small[n][l]) for n in ('attn_group_norm', 'lru_group_norm', 'post_mix_norm', 'pre_mlp_norm',
                                             'post_mlp_norm')]
        groups, o, h_mid, z2, slope, act, y, h_out = _out_mlp_fwd(attn, lru, h, w_out, w_up4, w_down, *gains)
        saved.append(dict(h=h, z1=z1, q=q, kv=kv, xg=xg, attn=attn, lse=lse, hs=hs, lru=lru, groups=groups, o=o,
                          h_mid=h_mid, z2=z2, slope=slope, act=act, y=y, lru_small=lru_small, w_in=w_in, w_out=w_out,
                          w_up4=w_up4, w_down=w_down, gains=gains))
        h = h_out

    dh, sq_err = _loss_head(h.reshape(batch, per_example, D_MODEL), loss_target)
    dh = dh.reshape(batch * per_example, D_MODEL)

    gs = {n: [None] * depth for n in REPLICATED_NAMES + ('conv_w',)}
    handed_over = None
    for l in reversed(range(depth)):
        s = saved[l]

        def ordered_after(gain):
            return gain if handed_over is None else gain + handed_over[0, 0]

        gains = list(s['gains'])
        gains[4] = ordered_after(gains[4])
        dh_mid, dy, dup, do, d_attn, d_lru, g_rows = _mlp_out_bwd(dh, s['h_mid'], s['y'], s['slope'], s['o'], s['attn'],
                                                                s['lru'], s['w_out'], s['w_up4'], s['w_down'], *gains)
        handed_over = on_layer_grads(l, {
            'w_down': _matmul_tn(s['act'], dy, 512, D_MODEL, "grad_w_down", BF16),
            'w_up': _matmul_tn(s['z2'], dup, 512, D_MODEL, "grad_w_up", BF16, column_blocks=True)})
        lru_small = list(s['lru_small'])
        lru_small[1] = ordered_after(lru_small[1])
        dxg, dwa, dwx, g_lru = _lru_bwd(s['xg'], s['hs'], d_lru, *lru_small, batch, n_blocks)
        dq, dkv, dsink = _attn_bwd(s['q'], s['kv'], small['attn_sinks'][l], s['attn'], s['lse'], d_attn, batch, n_blocks)
        dh, dproj, g_in = _in_proj_bwd(dh_mid, s['h'], dq, dkv, dxg, rope_cos, rope_sin, _row(small['pre_mix_norm'][l]),
                                       s['w_in'])
        handed_over = on_layer_grads(l, {
            'w_out': _matmul_tn(s['groups'], do, 512, D_MODEL, "grad_w_out", BF16),
            'w_in': _matmul_tn(dproj, s['z1'], IN_COLS // 2, D_MODEL, "grad_w_in", BF16)})
        gs['pre_mlp_norm'][l], gs['post_mlp_norm'][l] = g_rows[0], g_rows[1]
        gs['post_mix_norm'][l] = g_rows[2]
        gs['attn_group_norm'][l], gs['lru_group_norm'][l] = g_rows[3, :ATTN_W], g_rows[3, ATTN_W:]
        gs['pre_mix_norm'][l] = g_in[0]
        gs['attn_sinks'][l] = dsink[:, 0]
        gs['conv_w'][l] = g_lru[:CONV_TAPS]
        gs['conv_b'][l], gs['gate_a_b'][l], gs['gate_x_b'][l], gs['lru_lambda'][l] = g_lru[4], g_lru[5], g_lru[6], g_lru[7]
        gs['gate_a_w'][l] = _diagonal_blocks(dwa)
        gs['gate_x_w'][l] = _diagonal_blocks(dwx)

    grad_x = dh.reshape(batch, per_example, D_MODEL)[:, N_META:n_real]
    grad_meta = _meta_grad(dh, batch, per_example)
    small_grads = {n: jnp.stack(v) for n, v in gs.items()}
    return sq_err, grad_x, grad_meta, small_grads, handed_over


PACK_UNIT = SUBLANES * LANES


def _size(shape):
    size = 1
    for d in shape:
        size *= d
    return size


def _pack(arrays):
    parts = []
    for a in arrays:
        flat = a.reshape(-1)
        padded = -(-flat.shape[0] // PACK_UNIT) * PACK_UNIT
        if padded != flat.shape[0]:
            flat = jnp.pad(flat, (0, padded - flat.shape[0]))
        parts.append(flat.reshape(-1, LANES))
    return jnp.concatenate(parts, axis=0)


def _unpack(buf, shapes):
    out, at = [], 0
    for shp in shapes:
        size = _size(shp)
        rows = -(-size // PACK_UNIT) * SUBLANES
        part = buf[at:at + rows]
        if rows * LANES != size:
            part = part.reshape(-1)[:size]
        out.append(part.reshape(shp))
        at += rows
    return out


def kernel(x, meta_tokens, pre_mix_norm, w_in, attn_sinks, conv_w, conv_b, gate_a_w, gate_a_b, gate_x_w, gate_x_b, lru_lambda, attn_group_norm, lru_group_norm, w_out, post_mix_norm, pre_mlp_norm, w_up, w_down, post_mlp_norm, loss_target, m_meta_tokens, m_pre_mix_norm, m_w_in, m_attn_sinks, m_conv_w, m_conv_b, m_gate_a_w, m_gate_a_b, m_gate_x_w, m_gate_x_b, m_lru_lambda, m_attn_group_norm, m_lru_group_norm, m_w_out, m_post_mix_norm, m_pre_mlp_norm, m_w_up, m_w_down, m_post_mlp_norm, v_meta_tokens, v_pre_mix_norm, v_w_in, v_attn_sinks, v_conv_w, v_conv_b, v_gate_a_w, v_gate_a_b, v_gate_x_w, v_gate_x_b, v_lru_lambda, v_attn_group_norm, v_lru_group_norm, v_w_out, v_post_mix_norm, v_pre_mlp_norm, v_w_up, v_w_down, v_post_mlp_norm):
    given = dict(locals())
    w = {n: given[n] for n in WEIGHT_NAMES}
    m = {n: given['m_' + n] for n in WEIGHT_NAMES}
    v = {n: given['v_' + n] for n in WEIGHT_NAMES}
    depth = w_in.shape[0]
    for d in (w, m, v):
        d['w_in'] = jnp.swapaxes(d['w_in'], 1, 2)
    chip = 2 * lax.axis_index("x") + lax.axis_index("y")
    chip1 = chip.reshape(1).astype(jnp.int32)

    in_flight = []
    all_started = chip1
    for l in range(depth):
        bufs = []
        for names in SAME_SHAPE:
            bufs += _place_shard([w[n] for n in names], l, chip1, "place_" + names[0])
        handles, all_started = _gather_start(bufs, all_started, "gather_start_%d" % l, halves=True)
        in_flight.append(dict(zip(BIG_NAMES, handles)))
    first_use = {'w_in': ('w_in',), 'w_out': ('w_out', 'w_up', 'w_down')}

    def big_weight(name, l, after):
        if l == 0 and name == 'w_in':
            after = all_started
        names = first_use.get(name, ()) if l == 0 else (BIG_NAMES if name == 'w_in' else ())
        if names:
            relayed, _ = _gather_relay([in_flight[l][n] for n in names], after, "gather_relay_%d_%s" % (l, name))
            in_flight[l].update(zip(names, relayed))
        send, recv, buf = in_flight[l][name]
        full = _gather_wait(send, recv, buf, after, "gather_wait_%d_%s" % (l, name), relayed=True)
        if name == 'w_in':
            return full.reshape(IN_COLS, D_MODEL)
        if name == 'w_out':
            return full.reshape(D_MODEL, D_MODEL)
        if name == 'w_down':
            return full.reshape(D_FF, D_MODEL)
        return full

    gathered = _exchange_chips([meta_tokens, conv_w], "gather_small_weights", scatter=False)
    full_meta = jnp.concatenate([gathered[0][s] for s in range(N_CHIPS)], axis=1)
    small = {n: w[n] for n in REPLICATED_NAMES}
    small['conv_w'] = jnp.concatenate([gathered[1][s] for s in range(N_CHIPS)], axis=2)

    scattering = [{} for _ in range(depth)]

    pending = {}

    def on_layer_grads(l, big):
        if l > 0 and 'w_in' not in big:
            pending.update(big)
            return None
        big = {**pending, **big}
        pending.clear()
        names = list(big)
        pieces = [big[n].reshape(N_CHIPS, -1, D_MODEL) for n in names]
        handles, started = _scatter_start(pieces, "scatter_start_%d_%s" % (l, names[0]))
        scattering[l].update(zip(names, handles))
        return started

    sq_err, grad_x, grad_meta, small_grads, all_handed_over = _local_step(x, loss_target, full_meta, small, depth, big_weight,
                                                                           on_layer_grads)
    loss = lax.psum(sq_err[0, 0] * (0.5 / D_MODEL), ("x", "y", "c"))

    small_names = list(REPLICATED_NAMES) + list(COLUMN_SHARDED_SMALL)
    small_full = dict(small_grads)
    small_full['meta_tokens'] = grad_meta
    device1 = (2 * chip + lax.axis_index("c")).reshape(1).astype(jnp.int32)
    packed = _place_slot(_pack([small_full[n] for n in small_names]), device1, N_DEV, "place_small_grads")
    [small_flight], done = _gather_start([packed], all_handed_over, "gather_small_grads_start", all_devices=True)

    partial = {n: None for n in BIG_NAMES}
    for l in reversed(range(depth)):
        for names in SAME_SHAPE:
            arrived = []
            for n in names:
                send, recv, pieces, land = scattering[l][n]
                arrived.append(_scatter_wait(send, recv, pieces, land, done, "scatter_wait_%d_%s" % (l, n)))
            stacked = None if partial[names[0]] is None else [partial[n] for n in names]
            sums, done = _sum_pieces([a[0] for a in arrived], [a[1] for a in arrived], chip1, l, depth, stacked)
            partial.update(zip(names, sums))
    sibling = _swap_sibling([partial[n] for n in BIG_NAMES], "swap_partial_grads")
    results = {n: _adamw(w[n], m[n], v[n], [partial[n], other], "adamw_" + n) for n, other in zip(BIG_NAMES, sibling)}

    slots = _gather_wait(*small_flight, results['w_down'][0], "gather_small_grads_wait", all_devices=True)
    summed = _unpack(_sum_slots(slots, "sum_small_grads"), [small_full[n].shape for n in small_names])
    grads = dict(zip(small_names, summed))
    grads['meta_tokens'] = lax.dynamic_slice_in_dim(grads['meta_tokens'], chip * meta_tokens.shape[1], meta_tokens.shape[1], 1)
    grads['conv_w'] = lax.dynamic_slice_in_dim(grads['conv_w'], chip * conv_w.shape[2], conv_w.shape[2], 2)

    out_g, out_d, out_m, out_v = {}, {}, {}, {}
    for n in BIG_NAMES:
        out_g[n], out_d[n], out_m[n], out_v[n] = [jnp.swapaxes(r, 1, 2) if n == 'w_in' else r for r in results[n][:4]]
    shapes = [w[n].shape for n in small_names]
    res = _adamw(_pack([w[n] for n in small_names]), _pack([m[n] for n in small_names]), _pack([v[n] for n in small_names]),
                 [_pack([grads[n] for n in small_names])], "adamw_small")
    for k, store in enumerate((out_g, out_d, out_m, out_v)):
        for n, a in zip(small_names, _unpack(res[k], shapes)):
            store[n] = a

    return (loss, grad_x, *[out_g[n] for n in WEIGHT_NAMES], *[out_d[n] for n in WEIGHT_NAMES],
            *[out_m[n] for n in WEIGHT_NAMES], *[out_v[n] for n in WEIGHT_NAMES])
```

```python
import functools

import jax
import jax.numpy as jnp
from jax import lax
from jax.experimental import pallas as pl
from jax.experimental.pallas import tpu as pltpu

F32 = jnp.float32
BF16 = jnp.bfloat16

D_MODEL = 1024
N_HEADS = 8
ATTN_W = 512
KV_W = 128
LRU_W = 512
LRU_BLOCKS = 8
LRU_BLOCK = 64
IN_COLS = 1792
D_FF = 4096
N_META = 16
CONV_TAPS = 4
LRU_C = 8.0
ROPE_THETA = 10000.0
EPS = 1e-6
ATTN_SCALE = 0.125

ADAM_LR = 0.001
ADAM_B1 = 0.9
ADAM_B2 = 0.999
ADAM_EPS = 1e-08
ADAM_WD = 0.01
ADAM_STEP = 10

N_CHIPS = 4
N_DEV = 8
TIME_BLOCK = 128
ROW_TILE = 272
PROJ_ROW_TILE = 544
LANES = 128
SUBLANES = 8
MASKED = -1e30
VMEM_LIMIT = 56 * 1024 * 1024

MESH = pl.DeviceIdType.MESH
HBM_SPEC = pl.BlockSpec(memory_space=pltpu.HBM)

WEIGHT_NAMES = ['meta_tokens', 'pre_mix_norm', 'w_in', 'attn_sinks', 'conv_w', 'conv_b', 'gate_a_w', 'gate_a_b',
                'gate_x_w', 'gate_x_b', 'lru_lambda', 'attn_group_norm', 'lru_group_norm', 'w_out', 'post_mix_norm',
                'pre_mlp_norm', 'w_up', 'w_down', 'post_mlp_norm']
BIG_NAMES = ('w_in', 'w_out', 'w_up', 'w_down')
SAME_SHAPE = (('w_in',), ('w_out',), ('w_up', 'w_down'))
COLUMN_SHARDED_SMALL = ('meta_tokens', 'conv_w')
REPLICATED_NAMES = tuple(n for n in WEIGHT_NAMES if n not in BIG_NAMES and n not in COLUMN_SHARDED_SMALL)


def _sds(shape, dtype):
    return jax.ShapeDtypeStruct(tuple(shape), dtype)


def _params(*sem):
    return pltpu.CompilerParams(dimension_semantics=sem, vmem_limit_bytes=VMEM_LIMIT)


def _row_spec(width, tile=ROW_TILE):
    return pl.BlockSpec((tile, width), lambda i: (i, 0))


def _row_tile(rows, bound):
    tile = bound
    while rows % tile:
        tile -= 16
    return tile


def _proj_tile(rows):
    return _row_tile(rows, PROJ_ROW_TILE)


def _whole_spec(a):
    nd = a.ndim
    return pl.BlockSpec(a.shape, lambda *_: (0,) * nd)


def _resident_spec(a):
    nd = a.ndim
    return pl.BlockSpec(a.shape, lambda *_: (0,) * nd, pipeline_mode=pl.Buffered(1))


def _rms(x, g):
    r = lax.rsqrt(jnp.mean(x * x, axis=-1, keepdims=True) + EPS)
    return x * r * g


def _rms_bwd(dy, x, g):
    r = lax.rsqrt(jnp.mean(x * x, axis=-1, keepdims=True) + EPS)
    xh = x * r
    dg = jnp.sum(dy * xh, axis=0, keepdims=True)
    dxh = dy * g
    dx = r * (dxh - xh * jnp.mean(dxh * xh, axis=-1, keepdims=True))
    return dx, dg


def _rope(x, cos, sin_signed):
    width = x.shape[1]
    reps = width // LANES
    if reps > 1:
        cos = jnp.tile(cos, (1, reps))
        sin_signed = jnp.tile(sin_signed, (1, reps))
    lane = lax.broadcasted_iota(jnp.int32, x.shape, 1)
    first_half = jnp.bitwise_and(lane, 32) == 0
    other = jnp.where(first_half, pltpu.roll(x, width - 32, 1), pltpu.roll(x, 32, 1))
    return x * cos + other * sin_signed


def _sigmoid(x):
    return 1.0 / (1.0 + jnp.exp(-x))


def _log1p(e):
    return jnp.where(e < 1e-3, e * (1.0 - e * (0.5 - e * (1.0 / 3.0))), jnp.log(1.0 + e))


def _one_minus_square(a, log_a):
    x = 2.0 * log_a
    return jnp.where(x > -0.002, x * (-1.0 - 0.5 * x), 1.0 - a * a)


GELU_K = 0.7978845608028654
GELU_C = 0.044715


def _gelu(x):
    t = jnp.tanh(GELU_K * (x + GELU_C * x * x * x))
    return 0.5 * x * (1.0 + t)


def _gelu_and_grad(x):
    x2 = x * x
    t = jnp.tanh(GELU_K * (x + GELU_C * x * x2))
    val = 0.5 * x * (1.0 + t)
    grad = 0.5 * (1.0 + t) + 0.5 * x * (1.0 - t * t) * GELU_K * (1.0 + 3.0 * GELU_C * x2)
    return val, grad


def _dot(a, b):
    return jnp.dot(a, b, preferred_element_type=F32)


def _dot_nt(a, b):
    return lax.dot_general(a, b, (((1,), (1,)), ((), ())), preferred_element_type=F32)


def _dot_tn(a, b):
    return lax.dot_general(a, b, (((0,), (0,)), ((), ())), preferred_element_type=F32)


def _put_rows(rows_8, values):
    d = values[0].shape[1]
    rowid = lax.broadcasted_iota(jnp.int32, (rows_8, d), 0)
    out = jnp.zeros((rows_8, d), F32)
    for k, v in enumerate(values):
        out = out + jnp.where(rowid == k, v, 0.0)
    return out


def _in_proj_fwd(h, gain, w_in, rope_cos, rope_sin):
    rows = h.shape[0]

    def body(h_ref, g_ref, w_ref, c_ref, s_ref, z_ref, q_ref, kv_ref, xg_ref):
        z = _rms(h_ref[...], g_ref[...]).astype(BF16)
        z_ref[...] = z
        proj = _dot_nt(z, w_ref[...])
        cos = c_ref[...]
        sin = s_ref[...]
        q_ref[...] = (_rope(proj[:, :ATTN_W], cos, sin) * ATTN_SCALE).astype(BF16)
        kv_ref[:, :KV_W] = _rope(proj[:, ATTN_W:ATTN_W + KV_W], cos, sin).astype(BF16)
        kv_ref[:, KV_W:] = proj[:, ATTN_W + KV_W:ATTN_W + 2 * KV_W].astype(BF16)
        xg_ref[...] = proj[:, ATTN_W + 2 * KV_W:]

    tile = _proj_tile(rows)
    rs = functools.partial(_row_spec, tile=tile)
    return pl.pallas_call(
        body, name="in_proj_fwd", grid=(rows // tile,),
        in_specs=[rs(D_MODEL), _whole_spec(gain), _whole_spec(w_in), rs(LANES), rs(LANES)],
        out_specs=[rs(D_MODEL), rs(ATTN_W), rs(2 * KV_W), rs(2 * LRU_W)],
        out_shape=[_sds((rows, D_MODEL), BF16), _sds((rows, ATTN_W), BF16), _sds((rows, 2 * KV_W), BF16),
                   _sds((rows, 2 * LRU_W), F32)],
        compiler_params=_params("parallel"),
    )(h, gain, w_in, rope_cos, rope_sin)


def _in_proj_bwd(dh_mid, h, dq, dkv, dxg, rope_cos, rope_sin, gain, w_in):
    rows = h.shape[0]

    def body(dhm_ref, h_ref, dq_ref, dkv_ref, dxg_ref, c_ref, s_ref, g_ref, w_ref, dh_ref, dp_ref, gacc_ref):
        @pl.when(pl.program_id(0) == 0)
        def _():
            gacc_ref[...] = jnp.zeros_like(gacc_ref)

        cos = c_ref[...]
        sin = -s_ref[...]
        dp_ref[:, :ATTN_W] = (_rope(dq_ref[...], cos, sin) * ATTN_SCALE).astype(BF16)
        dp_ref[:, ATTN_W:ATTN_W + KV_W] = _rope(dkv_ref[:, :KV_W], cos, sin).astype(BF16)
        dp_ref[:, ATTN_W + KV_W:ATTN_W + 2 * KV_W] = dkv_ref[:, KV_W:].astype(BF16)
        dp_ref[:, ATTN_W + 2 * KV_W:] = dxg_ref[...]
        dz = _dot(dp_ref[...], w_ref[...])
        dx, dg = _rms_bwd(dz, h_ref[...], g_ref[...])
        dh_ref[...] = dhm_ref[...] + dx
        gacc_ref[...] += _put_rows(SUBLANES, [dg])

    tile = _proj_tile(rows)
    rs = functools.partial(_row_spec, tile=tile)
    return pl.pallas_call(
        body, name="in_proj_bwd", grid=(rows // tile,),
        in_specs=[rs(D_MODEL), rs(D_MODEL), rs(ATTN_W), rs(2 * KV_W), rs(2 * LRU_W),
                  rs(LANES), rs(LANES), _whole_spec(gain), _whole_spec(w_in)],
        out_specs=[rs(D_MODEL), rs(IN_COLS), pl.BlockSpec((SUBLANES, D_MODEL), lambda i: (0, 0))],
        out_shape=[_sds((rows, D_MODEL), F32), _sds((rows, IN_COLS), BF16), _sds((SUBLANES, D_MODEL), F32)],
        compiler_params=_params("arbitrary"),
    )(dh_mid, h, dq, dkv, dxg, rope_cos, rope_sin, gain, w_in)


def _kv_lane_variants(t, group):
    lane = lax.broadcasted_iota(jnp.int32, t.shape, 1)
    low = lane < 64
    swapped = pltpu.roll(t, 64, 1)
    if group == 0:
        lo, hi = jnp.where(low, t, 0.0), jnp.where(low, 0.0, swapped)
    else:
        lo, hi = jnp.where(low, swapped, 0.0), jnp.where(low, 0.0, t)
    return jnp.concatenate([lo, hi], axis=0).astype(BF16)


GROUP_ROWS = 2 * TIME_BLOCK
KEYS = 2 * TIME_BLOCK


def _window_mask(j):
    r = jnp.bitwise_and(lax.broadcasted_iota(jnp.int32, (GROUP_ROWS, KEYS), 0), TIME_BLOCK - 1)
    c = lax.broadcasted_iota(jnp.int32, (GROUP_ROWS, KEYS), 1)
    return (c > r) & (c <= r + TIME_BLOCK) & ((c >= TIME_BLOCK) | (j > 0))


def _group_rows(ref, ex, group):
    lo = 2 * group * LANES
    return jnp.concatenate([ref[ex, :, lo:lo + LANES], ref[ex, :, lo + LANES:lo + 2 * LANES]], axis=0)


def _per_head(sink_ref, group, half):
    upper = lax.broadcasted_iota(jnp.int32, (GROUP_ROWS, 1), 0) < TIME_BLOCK
    return jnp.where(upper, sink_ref[4 * group + half], sink_ref[4 * group + 2 + half])


def _by_example(a, batch):
    return a.reshape(batch, a.shape[0] // batch, a.shape[1])


def _attn_fwd(q, kv, sinks, batch, n_blocks):
    rows = q.shape[0]

    def body(sink_ref, q_ref, kvc_ref, kvp_ref, o_ref, lse_ref):
        j = pl.program_id(0)
        mask = _window_mask(j)
        chains = [(ex, group) for ex in range(batch) for group in range(2)]
        v_cat, s_all, probs = {}, {}, {ch: [] for ch in chains}
        for ex, group in chains:
            kv2 = jnp.concatenate([kvp_ref[ex], kvc_ref[ex]], axis=0).astype(F32)
            k_cat = _kv_lane_variants(kv2[:, :KV_W], group)
            v_cat[ex, group] = _kv_lane_variants(kv2[:, KV_W:], group)
            s_all[ex, group] = _dot_nt(_group_rows(q_ref, ex, group), k_cat)
        for half in range(2):
            for ex, group in chains:
                sink = _per_head(sink_ref, group, half)
                s = jnp.where(mask, s_all[ex, group][:, half * KEYS:(half + 1) * KEYS], MASKED)
                m = jnp.maximum(jnp.max(s, axis=1, keepdims=True), sink)
                e = jnp.exp(s - m)
                den = jnp.sum(e, axis=1, keepdims=True) + jnp.exp(sink - m)
                probs[ex, group].append((e / den).astype(BF16))
                lse = m + jnp.log(den)
                upper_head, lower_head = 4 * group + half, 4 * group + 2 + half
                lse_ref[ex, :, upper_head:upper_head + 1] = lse[:TIME_BLOCK]
                lse_ref[ex, :, lower_head:lower_head + 1] = lse[TIME_BLOCK:]
        for ex, group in chains:
            out = _dot(jnp.concatenate(probs[ex, group], axis=1), v_cat[ex, group])
            o_ref[ex, :, 2 * group * LANES:(2 * group + 1) * LANES] = out[:TIME_BLOCK]
            o_ref[ex, :, (2 * group + 1) * LANES:(2 * group + 2) * LANES] = out[TIME_BLOCK:]

    def blk(width):
        return pl.BlockSpec((batch, TIME_BLOCK, width), lambda j: (0, j, 0))

    prev = pl.BlockSpec((batch, TIME_BLOCK, 2 * KV_W), lambda j: (0, jnp.maximum(j - 1, 0), 0))
    kv3 = _by_example(kv, batch)
    out, lse = pl.pallas_call(
        body, name="attn_fwd", grid=(n_blocks,),
        in_specs=[pl.BlockSpec(memory_space=pltpu.SMEM), blk(ATTN_W), blk(2 * KV_W), prev],
        out_specs=[blk(ATTN_W), blk(N_HEADS)],
        out_shape=[_sds((batch, rows // batch, ATTN_W), F32), _sds((batch, rows // batch, N_HEADS), F32)],
        compiler_params=_params("parallel"),
    )(sinks, _by_example(q, batch), kv3, kv3)
    return out.reshape(rows, ATTN_W), lse.reshape(rows, N_HEADS)


def _attn_bwd(q, kv, sinks, out, lse, d_out, batch, n_blocks):
    rows = q.shape[0]

    def body(sink_ref, q_ref, kvc_ref, kvp_ref, o_ref, do_ref, lse_ref, dq_ref, dkv_ref, dsink_ref, carry):
        j = pl.program_id(0)

        @pl.when(j == 0)
        def _():
            dsink_ref[...] = jnp.zeros_like(dsink_ref)

        def all_chains(dsink_vals):
            mask = _window_mask(j)
            lane = lax.broadcasted_iota(jnp.int32, (GROUP_ROWS, LANES), 1)
            low = lax.broadcasted_iota(jnp.int32, (KEYS, LANES), 1) < 64
            upper = lax.broadcasted_iota(jnp.int32, (GROUP_ROWS, 1), 0) < TIME_BLOCK
            chains = [(ex, group) for ex in range(batch) for group in range(2)]
            k_cat, v_cat, q_rows, do_b, od, s_all, dp_all = {}, {}, {}, {}, {}, {}, {}
            for ch in chains:
                ex, group = ch
                kv2 = jnp.concatenate([kvp_ref[ex], kvc_ref[ex]], axis=0).astype(F32)
                k_cat[ch] = _kv_lane_variants(kv2[:, :KV_W], group)
                v_cat[ch] = _kv_lane_variants(kv2[:, KV_W:], group)
                q_rows[ch] = _group_rows(q_ref, ex, group)
                do_rows = _group_rows(do_ref, ex, group)
                do_b[ch] = do_rows.astype(BF16)
                od[ch] = do_rows * _group_rows(o_ref, ex, group)
                s_all[ch] = _dot_nt(q_rows[ch], k_cat[ch])
                dp_all[ch] = _dot_nt(do_b[ch], v_cat[ch])
            probs, dss = {ch: [] for ch in chains}, {ch: [] for ch in chains}
            for half in range(2):
                for ch in chains:
                    ex, group = ch
                    heads = (4 * group + half, 4 * group + 2 + half)
                    sink = _per_head(sink_ref, group, half)
                    lse_h = jnp.concatenate([lse_ref[ex, :, h:h + 1] for h in heads], axis=0)
                    in_half = (lane < 64) if half == 0 else (lane >= 64)
                    delta = jnp.sum(jnp.where(in_half, od[ch], 0.0), axis=1, keepdims=True)
                    cols = slice(half * KEYS, (half + 1) * KEYS)
                    prob = jnp.exp(jnp.where(mask, s_all[ch][:, cols], MASKED) - lse_h)
                    probs[ch].append(prob.astype(BF16))
                    dss[ch].append((prob * (dp_all[ch][:, cols] - delta)).astype(BF16))
                    dsink = -jnp.exp(sink - lse_h) * delta
                    dsink_vals[heads[0]] = dsink_vals[heads[0]] + jnp.sum(jnp.where(upper, dsink, 0.0), axis=0, keepdims=True)
                    dsink_vals[heads[1]] = dsink_vals[heads[1]] + jnp.sum(jnp.where(upper, 0.0, dsink), axis=0, keepdims=True)
            dk_tile = {ex: jnp.zeros((KEYS, KV_W), F32) for ex in range(batch)}
            dv_tile = {ex: jnp.zeros((KEYS, KV_W), F32) for ex in range(batch)}
            for ch in chains:
                ex, group = ch
                ds = jnp.concatenate(dss[ch], axis=1)
                dq_rows = _dot(ds, k_cat[ch])
                dq_ref[ex, :, 2 * group * LANES:(2 * group + 1) * LANES] = dq_rows[:TIME_BLOCK]
                dq_ref[ex, :, (2 * group + 1) * LANES:(2 * group + 2) * LANES] = dq_rows[TIME_BLOCK:]
                dk_cat = _dot_tn(ds, q_rows[ch])
                dv_cat = _dot_tn(jnp.concatenate(probs[ch], axis=1), do_b[ch])
                if group == 0:
                    dk_tile[ex] = dk_tile[ex] + jnp.where(low, dk_cat[:KEYS] + pltpu.roll(dk_cat[KEYS:], 64, 1), 0.0)
                    dv_tile[ex] = dv_tile[ex] + jnp.where(low, dv_cat[:KEYS] + pltpu.roll(dv_cat[KEYS:], 64, 1), 0.0)
                else:
                    dk_tile[ex] = dk_tile[ex] + jnp.where(low, 0.0, pltpu.roll(dk_cat[:KEYS], 64, 1) + dk_cat[KEYS:])
                    dv_tile[ex] = dv_tile[ex] + jnp.where(low, 0.0, pltpu.roll(dv_cat[:KEYS], 64, 1) + dv_cat[KEYS:])

            @pl.when(j > 0)
            def _():
                for ex in range(batch):
                    dkv_ref[ex, :, :KV_W] = carry[ex, :, :KV_W] + dk_tile[ex][:TIME_BLOCK]
                    dkv_ref[ex, :, KV_W:] = carry[ex, :, KV_W:] + dv_tile[ex][:TIME_BLOCK]

            for ex in range(batch):
                carry[ex, :, :KV_W] = dk_tile[ex][TIME_BLOCK:]
                carry[ex, :, KV_W:] = dv_tile[ex][TIME_BLOCK:]

        @pl.when(j < n_blocks)
        def _():
            dsink_vals = {head: jnp.zeros((1, 1), F32) for head in range(N_HEADS)}
            all_chains(dsink_vals)
            rowid = lax.broadcasted_iota(jnp.int32, (N_HEADS, LANES), 0)
            upd = jnp.zeros((N_HEADS, LANES), F32)
            for head, val in dsink_vals.items():
                upd = upd + jnp.where(rowid == head, val, 0.0)
            dsink_ref[...] += upd

        @pl.when(j == n_blocks)
        def _():
            dkv_ref[...] = carry[...]

    last = n_blocks - 1

    def blk(width):
        return pl.BlockSpec((batch, TIME_BLOCK, width), lambda j: (0, jnp.minimum(j, last), 0))

    prev = pl.BlockSpec((batch, TIME_BLOCK, 2 * KV_W), lambda j: (0, jnp.maximum(jnp.minimum(j, last) - 1, 0), 0))
    dkv_spec = pl.BlockSpec((batch, TIME_BLOCK, 2 * KV_W), lambda j: (0, jnp.maximum(j - 1, 0), 0))
    kv3 = _by_example(kv, batch)
    dq, dkv, dsink = pl.pallas_call(
        body, name="attn_bwd", grid=(n_blocks + 1,),
        in_specs=[pl.BlockSpec(memory_space=pltpu.SMEM), blk(ATTN_W), blk(2 * KV_W), prev, blk(ATTN_W), blk(ATTN_W),
                  blk(N_HEADS)],
        out_specs=[blk(ATTN_W), dkv_spec, pl.BlockSpec((N_HEADS, LANES), lambda j: (0, 0))],
        out_shape=[_sds((batch, rows // batch, ATTN_W), F32), _sds((batch, rows // batch, 2 * KV_W), F32),
                   _sds((N_HEADS, LANES), F32)],
        scratch_shapes=[pltpu.VMEM((batch, TIME_BLOCK, 2 * KV_W), F32)],
        compiler_params=_params("arbitrary"),
    )(sinks, _by_example(q, batch), kv3, kv3, _by_example(out, batch), _by_example(d_out, batch), _by_example(lse, batch))
    return dq.reshape(rows, ATTN_W), dkv.reshape(rows, 2 * KV_W), dsink


def _conv_taps(xb, prev8):
    ext = jnp.concatenate([prev8, xb], axis=0)
    n = ext.shape[0]
    return [xb] + [pltpu.roll(ext, k, 0)[SUBLANES:n] for k in range(1, CONV_TAPS)]


def _lru_gates(xc, wa, ba, wx, bx, lam):
    xcb = xc.astype(BF16)
    r = _sigmoid(_dot(xcb, wa) + ba)
    i = _sigmoid(_dot(xcb, wx) + bx)
    sp = jnp.maximum(-lam, 0.0) + _log1p(jnp.exp(-jnp.abs(lam)))
    log_a = -LRU_C * r * sp
    a = jnp.exp(log_a)
    mult = jnp.sqrt(_one_minus_square(a, log_a))
    return xcb, r, i, sp, a, mult


def _scan_fwd(a, u, h_before):
    n, d = a.shape
    groups = n // SUBLANES
    per_seq = groups // len(h_before)
    a = a.reshape(groups, SUBLANES, d)
    u = u.reshape(groups, SUBLANES, d)
    sub = lax.broadcasted_iota(jnp.int32, a.shape, 1)
    s = 1
    while s < SUBLANES:
        valid = sub >= s
        u = jnp.where(valid, u + a * pltpu.roll(u, s, 1), u)
        a = jnp.where(valid, a * pltpu.roll(a, s, 1), a)
        s *= 2
    out, prev = [None] * groups, list(h_before)
    for g in range(per_seq):
        for k in range(len(h_before)):
            at = k * per_seq + g
            out[at] = u[at] + a[at] * prev[k]
            prev[k] = out[at][SUBLANES - 1:SUBLANES, :]
    return jnp.concatenate(out, axis=0)


def _scan_rev(cf, g, d_after):
    n, d = g.shape
    groups = n // SUBLANES
    per_seq = groups // len(d_after)
    cf = cf.reshape(groups, SUBLANES, d)
    g = g.reshape(groups, SUBLANES, d)
    sub = lax.broadcasted_iota(jnp.int32, g.shape, 1)
    s = 1
    while s < SUBLANES:
        valid = sub + s < SUBLANES
        g = jnp.where(valid, g + cf * pltpu.roll(g, SUBLANES - s, 1), g)
        cf = jnp.where(valid, cf * pltpu.roll(cf, SUBLANES - s, 1), cf)
        s *= 2
    out, nxt = [None] * groups, list(d_after)
    for i in reversed(range(per_seq)):
        for k in range(len(d_after)):
            at = k * per_seq + i
            out[at] = g[at] + cf[at] * nxt[k]
            nxt[k] = out[at][0:1, :]
    return jnp.concatenate(out, axis=0)


def _lru_fwd(xg, conv_w, conv_b, wa, ba, wx, bx, lam, batch, n_blocks):
    rows = xg.shape[0]

    def body(xg_ref, cw_ref, cb_ref, wa_ref, ba_ref, wx_ref, bx_ref, lam_ref, hs_ref, lru_ref, x_prev, h_carry):
        @pl.when(pl.program_id(0) == 0)
        def _():
            x_prev[...] = jnp.zeros_like(x_prev)
            h_carry[...] = jnp.zeros_like(h_carry)

        xb = [xg_ref[ex, :, :LRU_W] for ex in range(batch)]
        taps = [_conv_taps(xb[ex], x_prev[ex]) for ex in range(batch)]
        taps = [jnp.concatenate([t[k] for t in taps], axis=0) for k in range(CONV_TAPS)]
        xc = cb_ref[...] + sum(cw_ref[CONV_TAPS - 1 - k:CONV_TAPS - k, :] * taps[k] for k in range(CONV_TAPS))
        _, _, i, _, a, mult = _lru_gates(xc, wa_ref[...], ba_ref[...], wx_ref[...], bx_ref[...], lam_ref[...])
        h = _scan_fwd(a, mult * i * xc, [h_carry[ex] for ex in range(batch)])
        gate = _gelu(jnp.concatenate([xg_ref[ex, :, LRU_W:] for ex in range(batch)], axis=0))
        lru = h * gate
        for ex in range(batch):
            rows_ex = slice(ex * TIME_BLOCK, (ex + 1) * TIME_BLOCK)
            hs_ref[ex] = h[rows_ex]
            lru_ref[ex] = lru[rows_ex]
            h_carry[ex] = h[(ex + 1) * TIME_BLOCK - 1:(ex + 1) * TIME_BLOCK, :]
            x_prev[ex] = xb[ex][TIME_BLOCK - SUBLANES:TIME_BLOCK, :]

    def blk(width):
        return pl.BlockSpec((batch, TIME_BLOCK, width), lambda j: (0, j, 0))

    small = [conv_w, conv_b, wa, ba, wx, bx, lam]
    hs, lru = pl.pallas_call(
        body, name="lru_fwd", grid=(n_blocks,),
        in_specs=[blk(2 * LRU_W)] + [_whole_spec(a) for a in small],
        out_specs=[blk(LRU_W), blk(LRU_W)],
        out_shape=[_sds((batch, rows // batch, LRU_W), F32)] * 2,
        scratch_shapes=[pltpu.VMEM((batch, SUBLANES, LRU_W), F32), pltpu.VMEM((batch, 1, LRU_W), F32)],
        compiler_params=_params("arbitrary"),
    )(_by_example(xg, batch), *small)
    return hs.reshape(rows, LRU_W), lru.reshape(rows, LRU_W)


def _lru_bwd(xg, hs, d_lru, conv_w, conv_b, wa, ba, wx, bx, lam, batch, n_blocks):
    rows = xg.shape[0]
    last_row = TIME_BLOCK - 1

    def body(xg_ref, xgh_ref, hs_ref, hsh_ref, dl_ref, cw_ref, cb_ref, wa_ref, ba_ref, wx_ref, bx_ref, lam_ref,
             dxg_ref, dwa_ref, dwx_ref, vec_ref, dh_carry, dxc_next):
        j = pl.program_id(0)

        @pl.when(j == 0)
        def _():
            dwa_ref[...] = jnp.zeros_like(dwa_ref)
            dwx_ref[...] = jnp.zeros_like(dwx_ref)
            vec_ref[...] = jnp.zeros_like(vec_ref)
            dh_carry[...] = jnp.zeros_like(dh_carry)
            dxc_next[...] = jnp.zeros_like(dxc_next)

        examples = range(batch)
        first = j == n_blocks - 1
        xb = [xg_ref[ex, :, :LRU_W] for ex in examples]
        taps = [_conv_taps(xb[ex], jnp.where(first, 0.0, xgh_ref[ex, :, :LRU_W])) for ex in examples]
        taps = [jnp.concatenate([t[k] for t in taps], axis=0) for k in range(CONV_TAPS)]
        cw = cw_ref[...]
        lam = lam_ref[...]
        wa = wa_ref[...]
        wx = wx_ref[...]
        xc = cb_ref[...] + sum(cw[CONV_TAPS - 1 - k:CONV_TAPS - k, :] * taps[k] for k in range(CONV_TAPS))
        xcb, r, i, sp, a, mult = _lru_gates(xc, wa, ba_ref[...], wx, bx_ref[...], lam)
        hs = jnp.concatenate([hs_ref[ex] for ex in examples], axis=0)
        row = jnp.bitwise_and(lax.broadcasted_iota(jnp.int32, hs.shape, 0), TIME_BLOCK - 1)
        h_before = jnp.concatenate([jnp.broadcast_to(jnp.where(first, 0.0, hsh_ref[ex, SUBLANES - 1:SUBLANES, :]),
                                                     (TIME_BLOCK, LRU_W)) for ex in examples], axis=0)
        h_prev = jnp.where(row == 0, h_before, pltpu.roll(hs, 1, 0))
        dl = jnp.concatenate([dl_ref[ex] for ex in examples], axis=0)
        gate, dgate = _gelu_and_grad(jnp.concatenate([xg_ref[ex, :, LRU_W:] for ex in examples], axis=0))
        dgb = (dl * hs * dgate).astype(BF16)
        cf = jnp.where(row == last_row, 1.0, pltpu.roll(a, batch * TIME_BLOCK - 1, 0))
        dh = _scan_rev(cf, dl * gate, [dh_carry[ex] for ex in examples])
        dmult = dh * i * xc
        di = dh * mult * xc
        dxc = dh * mult * i
        dlog_a = dh * h_prev * a - dmult * (a * a / mult)
        dr = dlog_a * (-LRU_C * sp)
        dlam = jnp.sum(dlog_a * (-LRU_C * r), axis=0, keepdims=True) * (-_sigmoid(-lam))
        dpr = dr * r * (1.0 - r)
        dpi = di * i * (1.0 - i)
        dprb = dpr.astype(BF16)
        dpib = dpi.astype(BF16)
        dxc = dxc + _dot_nt(dprb, wa) + _dot_nt(dpib, wx)
        dwa_ref[...] += _dot_tn(xcb, dprb)
        dwx_ref[...] += _dot_tn(xcb, dpib)
        n = TIME_BLOCK + SUBLANES
        later = []
        for k in range(1, CONV_TAPS):
            shifted = [pltpu.roll(jnp.concatenate([dxc[ex * TIME_BLOCK:(ex + 1) * TIME_BLOCK], dxc_next[ex]], axis=0),
                                  n - k, 0)[:TIME_BLOCK] for ex in examples]
            later.append(jnp.concatenate(shifted, axis=0))
        dxb = cw[CONV_TAPS - 1:CONV_TAPS, :] * dxc
        for k in range(1, CONV_TAPS):
            dxb = dxb + cw[CONV_TAPS - 1 - k:CONV_TAPS - k, :] * later[k - 1]
        dxb = dxb.astype(BF16)
        for ex in examples:
            rows_ex = slice(ex * TIME_BLOCK, (ex + 1) * TIME_BLOCK)
            dxg_ref[ex, :, :LRU_W] = dxb[rows_ex]
            dxg_ref[ex, :, LRU_W:] = dgb[rows_ex]
            dh_carry[ex] = a[ex * TIME_BLOCK:ex * TIME_BLOCK + 1, :] * dh[ex * TIME_BLOCK:ex * TIME_BLOCK + 1, :]
            dxc_next[ex] = dxc[ex * TIME_BLOCK:ex * TIME_BLOCK + SUBLANES, :]
        vecs = [jnp.sum(dxc * taps[CONV_TAPS - 1 - t], axis=0, keepdims=True) for t in range(CONV_TAPS)]
        vecs += [jnp.sum(dxc, axis=0, keepdims=True), jnp.sum(dpr, axis=0, keepdims=True),
                 jnp.sum(dpi, axis=0, keepdims=True), dlam]
        vec_ref[...] += _put_rows(SUBLANES, vecs)

    def tblk(j):
        return n_blocks - 1 - j

    def blk(width):
        return pl.BlockSpec((batch, TIME_BLOCK, width), lambda j: (0, tblk(j), 0))

    per8 = TIME_BLOCK // SUBLANES

    def halo(width):
        return pl.BlockSpec((batch, SUBLANES, width), lambda j: (0, jnp.maximum(per8 * tblk(j) - 1, 0), 0))

    small = [conv_w, conv_b, wa, ba, wx, bx, lam]
    acc = lambda shape: pl.BlockSpec(shape, lambda j: (0, 0))
    xg3, hs3 = _by_example(xg, batch), _by_example(hs, batch)
    dxg, dwa, dwx, vec = pl.pallas_call(
        body, name="lru_bwd", grid=(n_blocks,),
        in_specs=[blk(2 * LRU_W), halo(2 * LRU_W), blk(LRU_W), halo(LRU_W), blk(LRU_W)] + [_whole_spec(a) for a in small],
        out_specs=[blk(2 * LRU_W), acc((LRU_W, LRU_W)), acc((LRU_W, LRU_W)), acc((SUBLANES, LRU_W))],
        out_shape=[_sds((batch, rows // batch, 2 * LRU_W), BF16), _sds((LRU_W, LRU_W), F32), _sds((LRU_W, LRU_W), F32),
                   _sds((SUBLANES, LRU_W), F32)],
        scratch_shapes=[pltpu.VMEM((batch, 1, LRU_W), F32), pltpu.VMEM((batch, SUBLANES, LRU_W), F32)],
        compiler_params=_params("arbitrary"),
    )(xg3, xg3, hs3, hs3, _by_example(d_lru, batch), *small)
    return dxg.reshape(rows, 2 * LRU_W), dwa, dwx, vec


def _out_mlp_fwd(attn, lru, h, w_out, w_up4, w_down, g_attn, g_lru, g_post_mix, g_pre_mlp, g_post_mlp):
    rows = h.shape[0]
    tile = _row_tile(rows, ROW_TILE)
    rs = functools.partial(_row_spec, tile=tile)

    def body(at_ref, lr_ref, h_ref, wo_ref, wu_ref, wd_ref, ga_ref, gl_ref, gp_ref, g1_ref, g2_ref,
             grp_ref, o_ref, hm_ref, z_ref, slope_ref, act_ref, y_ref, ho_ref):
        a = _rms(at_ref[...], ga_ref[...]).astype(BF16)
        l = _rms(lr_ref[...], gl_ref[...]).astype(BF16)
        grp_ref[:, :ATTN_W] = a
        grp_ref[:, ATTN_W:] = l
        o = _dot(a, wo_ref[:ATTN_W, :]) + _dot(l, wo_ref[ATTN_W:, :])
        o_ref[...] = o
        x = h_ref[...] + _rms(o, gp_ref[...])
        hm_ref[...] = x
        z = _rms(x, g1_ref[...]).astype(BF16)
        z_ref[...] = z
        y = jnp.zeros((tile, D_MODEL), F32)
        for s in range(N_CHIPS):
            cols = slice(s * D_MODEL, (s + 1) * D_MODEL)
            r = jnp.maximum(_dot(z, wu_ref[s]), 0.0)
            slope_ref[:, cols] = (2.0 * r).astype(BF16)
            act = jnp.square(r).astype(BF16)
            act_ref[:, cols] = act
            y = y + _dot(act, wd_ref[cols, :])
        y_ref[...] = y
        ho_ref[...] = x + _rms(y, g2_ref[...])

    gains = [g_attn, g_lru, g_post_mix, g_pre_mlp, g_post_mlp]
    return pl.pallas_call(
        body, name="out_mlp_fwd", grid=(rows // tile,),
        in_specs=[rs(ATTN_W), rs(LRU_W), rs(D_MODEL), _resident_spec(w_out), _resident_spec(w_up4),
                  _resident_spec(w_down)] + [_whole_spec(g) for g in gains],
        out_specs=[rs(D_MODEL), rs(D_MODEL), rs(D_MODEL), rs(D_MODEL), rs(D_FF),
                   rs(D_FF), rs(D_MODEL), rs(D_MODEL)],
        out_shape=[_sds((rows, D_MODEL), BF16), _sds((rows, D_MODEL), F32), _sds((rows, D_MODEL), F32),
                   _sds((rows, D_MODEL), BF16), _sds((rows, D_FF), BF16), _sds((rows, D_FF), BF16),
                   _sds((rows, D_MODEL), F32), _sds((rows, D_MODEL), F32)],
        compiler_params=_params("parallel"),
    )(attn, lru, h, w_out, w_up4, w_down, *gains)


def _mlp_out_bwd(dh_out, h_mid, y, slope, o, attn, lru, w_out, w_up4, w_down, g_attn, g_lru, g_post_mix, g_pre_mlp,
                 g_post_mlp):
    rows = y.shape[0]
    tile = _row_tile(rows, ROW_TILE)
    rs = functools.partial(_row_spec, tile=tile)

    def body(dh_ref, hm_ref, y_ref, slope_ref, o_ref, at_ref, lr_ref, wo_ref, wu_ref, wd_ref, ga_ref, gl_ref, gp_ref,
             g1_ref, g2_ref, dhm_ref, dy_ref, dup_ref, do_ref, dat_ref, dlr_ref, gacc_ref):
        @pl.when(pl.program_id(0) == 0)
        def _():
            gacc_ref[...] = jnp.zeros_like(gacc_ref)

        dh = dh_ref[...]
        dy, dg2 = _rms_bwd(dh, y_ref[...], g2_ref[...])
        dyb = dy.astype(BF16)
        dy_ref[...] = dyb
        dz = jnp.zeros((tile, D_MODEL), F32)
        for s in range(N_CHIPS):
            cols = slice(s * D_MODEL, (s + 1) * D_MODEL)
            dact = _dot_nt(dyb, wd_ref[cols, :])
            dup = (dact * slope_ref[:, cols].astype(F32)).astype(BF16)
            dup_ref[:, cols] = dup
            dz = dz + _dot_nt(dup, wu_ref[s])
        dx, dg1 = _rms_bwd(dz, hm_ref[...], g1_ref[...])
        dhm = dh + dx
        dhm_ref[...] = dhm
        do, dgp = _rms_bwd(dhm, o_ref[...], gp_ref[...])
        dob = do.astype(BF16)
        do_ref[...] = dob
        dat, dga = _rms_bwd(_dot_nt(dob, wo_ref[:ATTN_W, :]), at_ref[...], ga_ref[...])
        dlr, dgl = _rms_bwd(_dot_nt(dob, wo_ref[ATTN_W:, :]), lr_ref[...], gl_ref[...])
        dat_ref[...] = dat
        dlr_ref[...] = dlr
        gacc_ref[...] += _put_rows(SUBLANES, [dg1, dg2, dgp, jnp.concatenate([dga, dgl], axis=1)])

    gains = [g_attn, g_lru, g_post_mix, g_pre_mlp, g_post_mlp]
    return pl.pallas_call(
        body, name="mlp_out_bwd", grid=(rows // tile,),
        in_specs=[rs(D_MODEL), rs(D_MODEL), rs(D_MODEL), rs(D_FF), rs(D_MODEL),
                  rs(ATTN_W), rs(LRU_W), _resident_spec(w_out), _resident_spec(w_up4), _resident_spec(w_down)]
                 + [_whole_spec(g) for g in gains],
        out_specs=[rs(D_MODEL), rs(D_MODEL), rs(D_FF), rs(D_MODEL), rs(ATTN_W),
                   rs(LRU_W), pl.BlockSpec((SUBLANES, D_MODEL), lambda i: (0, 0))],
        out_shape=[_sds((rows, D_MODEL), F32), _sds((rows, D_MODEL), BF16), _sds((rows, D_FF), BF16),
                   _sds((rows, D_MODEL), BF16), _sds((rows, ATTN_W), F32), _sds((rows, LRU_W), F32),
                   _sds((SUBLANES, D_MODEL), F32)],
        compiler_params=_params("arbitrary"),
    )(dh_out, h_mid, y, slope, o, attn, lru, w_out, w_up4, w_down, *gains)


def _matmul_tn(a, b, tm, tn, name, out_dtype, column_blocks=False):
    rows, m = a.shape
    n = b.shape[1]

    def body(a_ref, b_ref, o_ref):
        o_ref[...] = _dot_tn(a_ref[...], b_ref[...]).astype(out_dtype)

    if column_blocks:
        out_spec = pl.BlockSpec((None, tm, tn), lambda i, j: (j, i, 0))
        out_shape = _sds((n // tn, m, tn), out_dtype)
    else:
        out_spec = pl.BlockSpec((tm, tn), lambda i, j: (i, j))
        out_shape = _sds((m, n), out_dtype)
    return pl.pallas_call(
        body, name=name, grid=(m // tm, n // tn),
        in_specs=[pl.BlockSpec((rows, tm), lambda i, j: (0, i)), pl.BlockSpec((rows, tn), lambda i, j: (0, j))],
        out_specs=out_spec, out_shape=out_shape,
        compiler_params=_params("parallel", "parallel"),
    )(a, b)


LOSS_COLS = 256


def _loss_head(h, loss_target):
    batch, per_example, _ = h.shape
    seq = loss_target.shape[1]
    n_real = N_META + seq
    assert seq % SUBLANES == 0

    def body(h_ref, t_ref, dh_ref, l_ref):
        @pl.when((pl.program_id(0) == 0) & (pl.program_id(1) == 0))
        def _():
            l_ref[...] = jnp.zeros_like(l_ref)

        d = h_ref[N_META:n_real, :] - t_ref[...]
        dh_ref[:N_META, :] = jnp.zeros((N_META, LOSS_COLS), F32)
        dh_ref[N_META:n_real, :] = d * (1.0 / D_MODEL)
        dh_ref[n_real:, :] = jnp.zeros((per_example - n_real, LOSS_COLS), F32)
        l_ref[...] += jnp.sum(jnp.sum(d * d, axis=0, keepdims=True), axis=1, keepdims=True)

    blk = pl.BlockSpec((None, per_example, LOSS_COLS), lambda b, j: (b, 0, j))
    return pl.pallas_call(
        body, name="loss_head", grid=(batch, D_MODEL // LOSS_COLS),
        in_specs=[blk, pl.BlockSpec((None, seq, LOSS_COLS), lambda b, j: (b, 0, j))],
        out_specs=[blk, pl.BlockSpec((SUBLANES, LANES), lambda b, j: (0, 0))],
        out_shape=[_sds(h.shape, F32), _sds((SUBLANES, LANES), F32)],
        compiler_params=_params("arbitrary", "arbitrary"),
    )(h, loss_target)


def _meta_grad(dh0, batch, rows_per_example):
    per = rows_per_example // N_META

    def body(d_ref, o_ref):
        @pl.when(pl.program_id(0) == 0)
        def _():
            o_ref[...] = jnp.zeros_like(o_ref)

        o_ref[...] += d_ref[...]

    return pl.pallas_call(
        body, name="meta_grad", grid=(batch,),
        in_specs=[pl.BlockSpec((N_META, D_MODEL), lambda b: (b * per, 0))],
        out_specs=pl.BlockSpec((N_META, D_MODEL), lambda b: (0, 0)),
        out_shape=_sds((N_META, D_MODEL), F32),
        compiler_params=_params("arbitrary"),
    )(dh0)


def _elementwise_tile(rows, cols):
    tile = rows
    while tile * cols * 4 > (2 << 20) and tile % 16 == 0:
        tile //= 2
    return tile


def _sum_slots(buf, name):
    k, rows, cols = buf.shape
    tile = _elementwise_tile(rows, cols)

    def body(*refs):
        total = refs[0][...].astype(F32)
        for r in refs[1:k]:
            total = total + r[...].astype(F32)
        refs[k][...] = total

    def slot(s):
        return pl.BlockSpec((None, tile, cols), lambda i: (s, i, 0))

    return pl.pallas_call(
        body, name=name, grid=(rows // tile,),
        in_specs=[slot(s) for s in range(k)], out_specs=pl.BlockSpec((tile, cols), lambda i: (i, 0)),
        out_shape=_sds((rows, cols), F32), compiler_params=_params("parallel"),
    )(*([buf] * k))


def _sum_pieces(pieces, landed, chip, layer, n_layers, stacked):
    count = len(pieces)
    _, rows, cols = pieces[0].shape
    tile = _elementwise_tile(rows, cols)

    def body(chip_ref, *refs):
        token = refs[-1]
        for t in range(count):
            own_ref, a_ref, b_ref, c_ref = refs[4 * t:4 * t + 4]
            total = ((own_ref[...].astype(F32) + a_ref[...].astype(F32)) + b_ref[...].astype(F32)) + c_ref[...].astype(F32)
            refs[-1 - count + t][...] = total.astype(BF16)
        token[...] = jnp.zeros_like(token)

    def slot(offset):
        return pl.BlockSpec((None, tile, cols), lambda i, chip_ref: ((chip_ref[0] + offset) % N_CHIPS, i, 0))

    carried = [] if stacked is None else list(stacked)
    operands = []
    for p, l in zip(pieces, landed):
        operands += [p, l, l, l]
    grid_spec = pltpu.PrefetchScalarGridSpec(
        num_scalar_prefetch=1, grid=(rows // tile,),
        in_specs=[slot(0), slot(1), slot(2), slot(3)] * count + [pl.BlockSpec(memory_space=pl.ANY)] * len(carried),
        out_specs=[pl.BlockSpec((None, tile, cols), lambda i, chip_ref: (layer, i, 0))] * count
                  + [pl.BlockSpec((SUBLANES, LANES), lambda i, chip_ref: (0, 0))])
    out = pl.pallas_call(
        body, name="sum_grad_pieces", grid_spec=grid_spec,
        out_shape=[_sds((n_layers, rows, cols), BF16)] * count + [_sds((SUBLANES, LANES), F32)],
        input_output_aliases={1 + 4 * count + t: t for t in range(len(carried))},
        compiler_params=_params("arbitrary"),
    )(chip, *operands, *carried)
    return out[:count], out[count]


def _adamw(w, m, v, grads, name):
    flat = w.ndim == 2
    if flat:
        w, m, v = w[None], m[None], v[None]
        grads = [g[None] for g in grads]
    layers, rows, cols = w.shape
    tile = _elementwise_tile(rows, cols)
    ng = len(grads)
    m_scale = 1.0 - ADAM_B1 ** ADAM_STEP
    v_scale = 1.0 - ADAM_B2 ** ADAM_STEP

    def body(*refs):
        w_ref, m_ref, v_ref = refs[:3]
        g_refs = refs[3:3 + ng]
        g_out, d_out, m_out, v_out = refs[-4:]
        g = g_refs[0][...].astype(F32)
        for r in g_refs[1:]:
            g = g + r[...].astype(F32)
        m_new = ADAM_B1 * m_ref[...] + (1.0 - ADAM_B1) * g
        v_new = ADAM_B2 * v_ref[...] + (1.0 - ADAM_B2) * (g * g)
        m_hat = m_new / m_scale
        v_hat = v_new / v_scale
        g_out[...] = g
        d_out[...] = -ADAM_LR * (m_hat / (jnp.sqrt(v_hat) + ADAM_EPS) + ADAM_WD * w_ref[...])
        m_out[...] = m_new
        v_out[...] = v_new

    spec = pl.BlockSpec((None, tile, cols), lambda l, i: (l, i, 0))
    out = pl.pallas_call(
        body, name=name, grid=(layers, rows // tile),
        in_specs=[spec] * (3 + ng), out_specs=[spec] * 4, out_shape=[_sds(w.shape, F32)] * 4,
        compiler_params=_params("parallel", "parallel"),
    )(w, m, v, *grads)
    return [o[0] for o in out] if flat else out


def _position():
    return lax.axis_index("x"), lax.axis_index("y"), lax.axis_index("c")


def _other_chips(x, y):
    return [(1 - x, y), (x, 1 - y), (1 - x, 1 - y)]


def _exchange_chips(arrays, name, scatter):
    n = len(arrays)

    def body(*refs):
        src, dst = refs[:n], refs[n:2 * n]
        send_sems, recv_sems, local_sems = refs[2 * n:]
        x, y, c = _position()
        mine = 2 * x + y
        copies = []
        for i in range(n):
            own = src[i].at[mine] if scatter else src[i]
            copies.append(pltpu.make_async_copy(own, dst[i].at[mine], local_sems.at[i]))
        for k, (px, py) in enumerate(_other_chips(x, y)):
            for i in range(n):
                piece = src[i].at[2 * px + py] if scatter else src[i]
                copies.append(pltpu.make_async_remote_copy(
                    src_ref=piece, dst_ref=dst[i].at[mine], send_sem=send_sems.at[k, i], recv_sem=recv_sems.at[k, i],
                    device_id=(px, py, c), device_id_type=MESH))
        for cp in copies:
            cp.start()
        for cp in copies:
            cp.wait()

    def out_shape(a):
        return _sds(a.shape if scatter else (N_CHIPS,) + a.shape, a.dtype)

    return pl.pallas_call(
        body, name=name, in_specs=[HBM_SPEC] * n, out_specs=[HBM_SPEC] * n, out_shape=[out_shape(a) for a in arrays],
        scratch_shapes=[pltpu.SemaphoreType.DMA((N_CHIPS - 1, n)), pltpu.SemaphoreType.DMA((N_CHIPS - 1, n)),
                        pltpu.SemaphoreType.DMA((n,))],
    )(*arrays)


def _swap_sibling(arrays, name):
    n = len(arrays)

    def body(*refs):
        src, dst = refs[:n], refs[n:2 * n]
        send_sems, recv_sems = refs[2 * n:]
        x, y, c = _position()
        copies = [pltpu.make_async_remote_copy(
            src_ref=src[i], dst_ref=dst[i], send_sem=send_sems.at[i], recv_sem=recv_sems.at[i],
            device_id=(x, y, 1 - c), device_id_type=MESH) for i in range(n)]
        for cp in copies:
            cp.start()
        for cp in copies:
            cp.wait()

    return pl.pallas_call(
        body, name=name, in_specs=[HBM_SPEC] * n, out_specs=[HBM_SPEC] * n,
        out_shape=[_sds(a.shape, a.dtype) for a in arrays],
        scratch_shapes=[pltpu.SemaphoreType.DMA((n,)), pltpu.SemaphoreType.DMA((n,))],
    )(*arrays)


SEM_SPEC = pl.BlockSpec(memory_space=pltpu.SEMAPHORE)
ANY_SPEC = pl.BlockSpec(memory_space=pl.ANY)
IN_FLIGHT = pltpu.SideEffectType.DATAFLOW_SIDE_EFFECTING


def _peer_sems(all_devices):
    return pltpu.SemaphoreType.DMA(((N_DEV if all_devices else N_CHIPS) - 1,))


def _own_slot(all_devices, x, y, c):
    return 4 * x + 2 * y + c if all_devices else 2 * x + y


def _peers(all_devices, x, y, c):
    if not all_devices:
        return [((px, py, c), 2 * px + py) for px, py in _other_chips(x, y)]
    out = []
    for fx in range(2):
        for fy in range(2):
            for fc in range(2):
                if fx or fy or fc:
                    px, py, pc = (1 - x if fx else x), (1 - y if fy else y), (1 - c if fc else c)
                    out.append(((px, py, pc), 4 * px + 2 * py + pc))
    return out


def _in_hbm(a):
    return pltpu.with_memory_space_constraint(a, pltpu.HBM)


def _place_slot(a, slot, n_slots, name):
    rows, cols = a.shape
    tile = _elementwise_tile(rows, cols)

    def body(slot_ref, a_ref, o_ref):
        o_ref[...] = a_ref[...]

    grid_spec = pltpu.PrefetchScalarGridSpec(
        num_scalar_prefetch=1, grid=(rows // tile,),
        in_specs=[pl.BlockSpec((tile, cols), lambda i, slot_ref: (i, 0))],
        out_specs=pl.BlockSpec((None, tile, cols), lambda i, slot_ref: (slot_ref[0], i, 0)))
    return pl.pallas_call(
        body, name=name, grid_spec=grid_spec, out_shape=_sds((n_slots, rows, cols), a.dtype),
        compiler_params=_params("parallel"),
    )(slot, a)


def _place_shard(ws, layer, chip, name):
    count = len(ws)
    _, rows, cols = ws[0].shape
    tile = _elementwise_tile(rows, cols)

    def body(chip_ref, *refs):
        for t in range(count):
            refs[count + t][...] = refs[t][...].astype(BF16)

    grid_spec = pltpu.PrefetchScalarGridSpec(
        num_scalar_prefetch=1, grid=(rows // tile,),
        in_specs=[pl.BlockSpec((None, tile, cols), lambda i, chip_ref: (layer, i, 0))] * count,
        out_specs=[pl.BlockSpec((None, tile, cols), lambda i, chip_ref: (chip_ref[0], i, 0))] * count)
    return pl.pallas_call(
        body, name=name, grid_spec=grid_spec, out_shape=[_sds((N_CHIPS, rows, cols), BF16)] * count,
        compiler_params=_params("parallel"),
    )(chip, *ws)


def _rows_of(ref, slot, core, halves):
    if not halves:
        return ref.at[slot]
    half = ref.shape[1] // 2
    return ref.at[slot, pl.ds(pl.multiple_of(core * half, half), half)]


def _gather_start(bufs, after, name, all_devices=False, halves=False):
    n = len(bufs)

    def body(*refs):
        buf = refs[:n]
        send, recv = refs[n + 1:2 * n + 1], refs[2 * n + 1:3 * n + 1]
        token = refs[4 * n + 1]
        x, y, c = _position()
        mine = _own_slot(all_devices, x, y, c)
        for i in range(n):
            for k, (peer, _) in enumerate(_peers(all_devices, x, y, c)):
                rows = _rows_of(buf[i], mine, c, halves)
                pltpu.make_async_remote_copy(
                    src_ref=rows, dst_ref=rows, send_sem=send[i].at[k], recv_sem=recv[i].at[k],
                    device_id=peer, device_id_type=MESH).start()
        token[...] = jnp.zeros_like(token)

    sems = _peer_sems(all_devices)
    out = pl.pallas_call(
        body, name=name, in_specs=[HBM_SPEC] * n + [ANY_SPEC],
        out_specs=[SEM_SPEC] * (2 * n) + [HBM_SPEC] * n + [pl.BlockSpec(memory_space=pltpu.VMEM)],
        out_shape=[sems] * (2 * n) + [pltpu.HBM(b.shape, b.dtype) for b in bufs] + [_sds((SUBLANES, LANES), F32)],
        input_output_aliases={i: 2 * n + i for i in range(n)},
        compiler_params=pltpu.CompilerParams(has_side_effects=IN_FLIGHT),
    )(*[_in_hbm(b) for b in bufs], after)
    return [(out[i], out[n + i], out[2 * n + i]) for i in range(n)], out[3 * n]


def _gather_relay(flights, after, name):
    n = len(flights)

    def body(*refs):
        buf, send, recv = refs[:n], refs[n:2 * n], refs[2 * n:3 * n]
        send2, recv2 = refs[3 * n + 1:4 * n + 1], refs[4 * n + 1:5 * n + 1]
        token = refs[6 * n + 1]
        x, y, c = _position()
        mine = 2 * x + y
        for i in range(n):
            for k, (peer, slot) in enumerate(_peers(False, x, y, c)):
                arrived = _rows_of(buf[i], slot, c, True)
                cp = pltpu.make_async_remote_copy(
                    src_ref=_rows_of(buf[i], mine, c, True), dst_ref=arrived, send_sem=send[i].at[k], recv_sem=recv[i].at[k],
                    device_id=peer, device_id_type=MESH)
                cp.wait_send()
                cp.wait_recv()
                pltpu.make_async_remote_copy(
                    src_ref=arrived, dst_ref=arrived, send_sem=send2[i].at[k], recv_sem=recv2[i].at[k],
                    device_id=(x, y, 1 - c), device_id_type=MESH).start()
        token[...] = jnp.zeros_like(token)

    bufs = [f[2] for f in flights]
    sems = _peer_sems(False)
    out = pl.pallas_call(
        body, name=name, in_specs=[HBM_SPEC] * n + [SEM_SPEC] * (2 * n) + [ANY_SPEC],
        out_specs=[SEM_SPEC] * (2 * n) + [HBM_SPEC] * n + [pl.BlockSpec(memory_space=pltpu.VMEM)],
        out_shape=[sems] * (2 * n) + [pltpu.HBM(b.shape, b.dtype) for b in bufs] + [_sds((SUBLANES, LANES), F32)],
        input_output_aliases={i: 2 * n + i for i in range(n)},
        compiler_params=pltpu.CompilerParams(has_side_effects=IN_FLIGHT),
    )(*bufs, *[f[0] for f in flights], *[f[1] for f in flights], after)
    return [(out[i], out[n + i], out[2 * n + i]) for i in range(n)], out[3 * n]


def _gather_wait(send, recv, buf, after, name, all_devices=False, relayed=False):
    def body(buf_ref, send_ref, recv_ref, after_ref, out_ref):
        x, y, c = _position()
        mine = _own_slot(all_devices, x, y, c)
        for k, (peer, slot) in enumerate(_peers(all_devices, x, y, c)):
            if relayed:
                cp = pltpu.make_async_remote_copy(
                    src_ref=_rows_of(buf_ref, slot, c, True), dst_ref=_rows_of(buf_ref, slot, 1 - c, True),
                    send_sem=send_ref.at[k], recv_sem=recv_ref.at[k], device_id=(x, y, 1 - c), device_id_type=MESH)
            else:
                cp = pltpu.make_async_remote_copy(
                    src_ref=buf_ref.at[mine], dst_ref=buf_ref.at[slot], send_sem=send_ref.at[k], recv_sem=recv_ref.at[k],
                    device_id=peer, device_id_type=MESH)
            cp.wait_send()
            cp.wait_recv()

    return pl.pallas_call(
        body, name=name, in_specs=[HBM_SPEC, SEM_SPEC, SEM_SPEC, ANY_SPEC], out_specs=HBM_SPEC,
        out_shape=pltpu.HBM(buf.shape, buf.dtype), input_output_aliases={0: 0},
        compiler_params=pltpu.CompilerParams(has_side_effects=IN_FLIGHT),
    )(buf, send, recv, after)


def _scatter_start(pieces, name):
    n = len(pieces)

    def body(*refs):
        src = refs[:n]
        send, recv = refs[n:2 * n], refs[2 * n:3 * n]
        land = refs[4 * n:5 * n]
        token = refs[5 * n]
        x, y, c = _position()
        mine = 2 * x + y
        for i in range(n):
            for k, (px, py) in enumerate(_other_chips(x, y)):
                pltpu.make_async_remote_copy(
                    src_ref=src[i].at[2 * px + py], dst_ref=land[i].at[mine], send_sem=send[i].at[k], recv_sem=recv[i].at[k],
                    device_id=(px, py, c), device_id_type=MESH).start()
        token[...] = jnp.zeros_like(token)

    hbm = [pltpu.HBM(p.shape, p.dtype) for p in pieces]
    out = pl.pallas_call(
        body, name=name, in_specs=[HBM_SPEC] * n,
        out_specs=[SEM_SPEC] * (2 * n) + [HBM_SPEC] * (2 * n) + [pl.BlockSpec(memory_space=pltpu.VMEM)],
        out_shape=[_peer_sems(False)] * (2 * n) + hbm + hbm + [_sds((SUBLANES, LANES), F32)],
        input_output_aliases={i: 2 * n + i for i in range(n)},
        compiler_params=pltpu.CompilerParams(has_side_effects=IN_FLIGHT),
    )(*[_in_hbm(p) for p in pieces])
    return [(out[i], out[n + i], out[2 * n + i], out[3 * n + i]) for i in range(n)], out[4 * n]


def _scatter_wait(send, recv, pieces, land, after, name):
    def body(src_ref, land_ref, send_ref, recv_ref, after_ref, src_out, land_out):
        x, y, c = _position()
        for k, (px, py) in enumerate(_other_chips(x, y)):
            cp = pltpu.make_async_remote_copy(
                src_ref=src_ref.at[2 * px + py], dst_ref=land_ref.at[2 * px + py], send_sem=send_ref.at[k],
                recv_sem=recv_ref.at[k], device_id=(px, py, c), device_id_type=MESH)
            cp.wait_send()
            cp.wait_recv()

    return pl.pallas_call(
        body, name=name, in_specs=[HBM_SPEC, HBM_SPEC, SEM_SPEC, SEM_SPEC, ANY_SPEC], out_specs=[HBM_SPEC, HBM_SPEC],
        out_shape=[pltpu.HBM(pieces.shape, pieces.dtype), pltpu.HBM(land.shape, land.dtype)],
        input_output_aliases={0: 0, 1: 1},
        compiler_params=pltpu.CompilerParams(has_side_effects=IN_FLIGHT),
    )(pieces, land, send, recv, after)


def _rope_tables(batch, rows_per_example):
    inv_freq = ROPE_THETA ** (-jnp.arange(0, 64, 2, dtype=F32) / 64)
    ang = jnp.arange(rows_per_example, dtype=F32)[:, None] * inv_freq[None, :]
    cos, sin = jnp.cos(ang), jnp.sin(ang)
    cos128 = jnp.concatenate([cos, cos, cos, cos], axis=1)
    sin128 = jnp.concatenate([-sin, sin, -sin, sin], axis=1)
    return jnp.tile(cos128, (batch, 1)), jnp.tile(sin128, (batch, 1))


def _block_diagonal(w):
    eye = jnp.eye(LRU_BLOCKS, dtype=w.dtype)
    return (w[:, :, None, :] * eye[:, None, :, None]).reshape(LRU_W, LRU_W)


def _diagonal_blocks(dense):
    d4 = dense.reshape(LRU_BLOCKS, LRU_BLOCK, LRU_BLOCKS, LRU_BLOCK)
    return jnp.stack([d4[n, :, n, :] for n in range(LRU_BLOCKS)])


def _row(v):
    return v.reshape(1, -1)


def _local_step(x, loss_target, meta_tokens, small, depth, big_weight, on_layer_grads):
    batch, seq, _ = x.shape
    n_real = N_META + seq
    n_blocks = -(-n_real // TIME_BLOCK)
    per_example = n_blocks * TIME_BLOCK
    pad = per_example - n_real
    meta = jnp.broadcast_to(meta_tokens[None], (batch, N_META, D_MODEL))
    h = jnp.concatenate([meta, x, jnp.zeros((batch, pad, D_MODEL), F32)], axis=1).reshape(batch * per_example, D_MODEL)
    rope_cos, rope_sin = _rope_tables(batch, per_example)

    saved = []
    for l in range(depth):
        wa = _block_diagonal(small['gate_a_w'][l]).astype(BF16)
        wx = _block_diagonal(small['gate_x_w'][l]).astype(BF16)
        lru_small = (small['conv_w'][l], _row(small['conv_b'][l]), wa, _row(small['gate_a_b'][l]), wx,
                     _row(small['gate_x_b'][l]), _row(small['lru_lambda'][l]))
        w_in = big_weight('w_in', l, h)
        z1, q, kv, xg = _in_proj_fwd(h, _row(small['pre_mix_norm'][l]), w_in, rope_cos, rope_sin)
        attn, lse = _attn_fwd(q, kv, small['attn_sinks'][l], batch, n_blocks)
        hs, lru = _lru_fwd(xg, *lru_small, batch, n_blocks)
        w_out = big_weight('w_out', l, xg)
        w_up4 = big_weight('w_up', l, xg)
        w_down = big_weight('w_down', l, xg)
        gains = [_row(small[n][l]) for n in ('attn_group_norm', 'lru_group_norm', 'post_mix_norm', 'pre_mlp_norm',
                                             'post_mlp_norm')]
        groups, o, h_mid, z2, slope, act, y, h_out = _out_mlp_fwd(attn, lru, h, w_out, w_up4, w_down, *gains)
        saved.append(dict(h=h, z1=z1, q=q, kv=kv, xg=xg, attn=attn, lse=lse, hs=hs, lru=lru, groups=groups, o=o,
                          h_mid=h_mid, z2=z2, slope=slope, act=act, y=y, lru_small=lru_small, w_in=w_in, w_out=w_out,
                          w_up4=w_up4, w_down=w_down, gains=gains))
        h = h_out

    dh, sq_err = _loss_head(h.reshape(batch, per_example, D_MODEL), loss_target)
    dh = dh.reshape(batch * per_example, D_MODEL)

    gs = {n: [None] * depth for n in REPLICATED_NAMES + ('conv_w',)}
    handed_over = None
    for l in reversed(range(depth)):
        s = saved[l]

        def ordered_after(gain):
            return gain if handed_over is None else gain + handed_over[0, 0]

        gains = list(s['gains'])
        gains[4] = ordered_after(gains[4])
        dh_mid, dy, dup, do, d_attn, d_lru, g_rows = _mlp_out_bwd(dh, s['h_mid'], s['y'], s['slope'], s['o'], s['attn'],
                                                                s['lru'], s['w_out'], s['w_up4'], s['w_down'], *gains)
        handed_over = on_layer_grads(l, {
            'w_down': _matmul_tn(s['act'], dy, 512, D_MODEL, "grad_w_down", BF16),
            'w_up': _matmul_tn(s['z2'], dup, 512, D_MODEL, "grad_w_up", BF16, column_blocks=True)})
        lru_small = list(s['lru_small'])
        lru_small[1] = ordered_after(lru_small[1])
        dxg, dwa, dwx, g_lru = _lru_bwd(s['xg'], s['hs'], d_lru, *lru_small, batch, n_blocks)
        dq, dkv, dsink = _attn_bwd(s['q'], s['kv'], small['attn_sinks'][l], s['attn'], s['lse'], d_attn, batch, n_blocks)
        dh, dproj, g_in = _in_proj_bwd(dh_mid, s['h'], dq, dkv, dxg, rope_cos, rope_sin, _row(small['pre_mix_norm'][l]),
                                       s['w_in'])
        handed_over = on_layer_grads(l, {
            'w_out': _matmul_tn(s['groups'], do, 512, D_MODEL, "grad_w_out", BF16),
            'w_in': _matmul_tn(dproj, s['z1'], IN_COLS // 2, D_MODEL, "grad_w_in", BF16)})
        gs['pre_mlp_norm'][l], gs['post_mlp_norm'][l] = g_rows[0], g_rows[1]
        gs['post_mix_norm'][l] = g_rows[2]
        gs['attn_group_norm'][l], gs['lru_group_norm'][l] = g_rows[3, :ATTN_W], g_rows[3, ATTN_W:]
        gs['pre_mix_norm'][l] = g_in[0]
        gs['attn_sinks'][l] = dsink[:, 0]
        gs['conv_w'][l] = g_lru[:CONV_TAPS]
        gs['conv_b'][l], gs['gate_a_b'][l], gs['gate_x_b'][l], gs['lru_lambda'][l] = g_lru[4], g_lru[5], g_lru[6], g_lru[7]
        gs['gate_a_w'][l] = _diagonal_blocks(dwa)
        gs['gate_x_w'][l] = _diagonal_blocks(dwx)

    grad_x = dh.reshape(batch, per_example, D_MODEL)[:, N_META:n_real]
    grad_meta = _meta_grad(dh, batch, per_example)
    small_grads = {n: jnp.stack(v) for n, v in gs.items()}
    return sq_err, grad_x, grad_meta, small_grads, handed_over


PACK_UNIT = SUBLANES * LANES


def _size(shape):
    size = 1
    for d in shape:
        size *= d
    return size


def _pack(arrays):
    parts = []
    for a in arrays:
        flat = a.reshape(-1)
        padded = -(-flat.shape[0] // PACK_UNIT) * PACK_UNIT
        if padded != flat.shape[0]:
            flat = jnp.pad(flat, (0, padded - flat.shape[0]))
        parts.append(flat.reshape(-1, LANES))
    return jnp.concatenate(parts, axis=0)


def _unpack(buf, shapes):
    out, at = [], 0
    for shp in shapes:
        size = _size(shp)
        rows = -(-size // PACK_UNIT) * SUBLANES
        part = buf[at:at + rows]
        if rows * LANES != size:
            part = part.reshape(-1)[:size]
        out.append(part.reshape(shp))
        at += rows
    return out


def kernel(x, meta_tokens, pre_mix_norm, w_in, attn_sinks, conv_w, conv_b, gate_a_w, gate_a_b, gate_x_w, gate_x_b, lru_lambda, attn_group_norm, lru_group_norm, w_out, post_mix_norm, pre_mlp_norm, w_up, w_down, post_mlp_norm, loss_target, m_meta_tokens, m_pre_mix_norm, m_w_in, m_attn_sinks, m_conv_w, m_conv_b, m_gate_a_w, m_gate_a_b, m_gate_x_w, m_gate_x_b, m_lru_lambda, m_attn_group_norm, m_lru_group_norm, m_w_out, m_post_mix_norm, m_pre_mlp_norm, m_w_up, m_w_down, m_post_mlp_norm, v_meta_tokens, v_pre_mix_norm, v_w_in, v_attn_sinks, v_conv_w, v_conv_b, v_gate_a_w, v_gate_a_b, v_gate_x_w, v_gate_x_b, v_lru_lambda, v_attn_group_norm, v_lru_group_norm, v_w_out, v_post_mix_norm, v_pre_mlp_norm, v_w_up, v_w_down, v_post_mlp_norm):
    given = dict(locals())
    w = {n: given[n] for n in WEIGHT_NAMES}
    m = {n: given['m_' + n] for n in WEIGHT_NAMES}
    v = {n: given['v_' + n] for n in WEIGHT_NAMES}
    depth = w_in.shape[0]
    for d in (w, m, v):
        d['w_in'] = jnp.swapaxes(d['w_in'], 1, 2)
    chip = 2 * lax.axis_index("x") + lax.axis_index("y")
    chip1 = chip.reshape(1).astype(jnp.int32)

    in_flight = []
    all_started = chip1
    for l in range(depth):
        bufs = []
        for names in SAME_SHAPE:
            bufs += _place_shard([w[n] for n in names], l, chip1, "place_" + names[0])
        handles, all_started = _gather_start(bufs, all_started, "gather_start_%d" % l, halves=True)
        in_flight.append(dict(zip(BIG_NAMES, handles)))
    first_use = {'w_in': ('w_in',), 'w_out': ('w_out', 'w_up', 'w_down')}

    def big_weight(name, l, after):
        if l == 0 and name == 'w_in':
            after = all_started
        names = first_use.get(name, ()) if l == 0 else (BIG_NAMES if name == 'w_in' else ())
        if names:
            relayed, _ = _gather_relay([in_flight[l][n] for n in names], after, "gather_relay_%d_%s" % (l, name))
            in_flight[l].update(zip(names, relayed))
        send, recv, buf = in_flight[l][name]
        full = _gather_wait(send, recv, buf, after, "gather_wait_%d_%s" % (l, name), relayed=True)
        if name == 'w_in':
            return full.reshape(IN_COLS, D_MODEL)
        if name == 'w_out':
            return full.reshape(D_MODEL, D_MODEL)
        if name == 'w_down':
            return full.reshape(D_FF, D_MODEL)
        return full

    gathered = _exchange_chips([meta_tokens, conv_w], "gather_small_weights", scatter=False)
    full_meta = jnp.concatenate([gathered[0][s] for s in range(N_CHIPS)], axis=1)
    small = {n: w[n] for n in REPLICATED_NAMES}
    small['conv_w'] = jnp.concatenate([gathered[1][s] for s in range(N_CHIPS)], axis=2)

    scattering = [{} for _ in range(depth)]

    pending = {}

    def on_layer_grads(l, big):
        if l > 0 and 'w_in' not in big:
            pending.update(big)
            return None
        big = {**pending, **big}
        pending.clear()
        names = list(big)
        pieces = [big[n].reshape(N_CHIPS, -1, D_MODEL) for n in names]
        handles, started = _scatter_start(pieces, "scatter_start_%d_%s" % (l, names[0]))
        scattering[l].update(zip(names, handles))
        return started

    sq_err, grad_x, grad_meta, small_grads, all_handed_over = _local_step(x, loss_target, full_meta, small, depth, big_weight,
                                                                           on_layer_grads)
    loss = lax.psum(sq_err[0, 0] * (0.5 / D_MODEL), ("x", "y", "c"))

    small_names = list(REPLICATED_NAMES) + list(COLUMN_SHARDED_SMALL)
    small_full = dict(small_grads)
    small_full['meta_tokens'] = grad_meta
    device1 = (2 * chip + lax.axis_index("c")).reshape(1).astype(jnp.int32)
    packed = _place_slot(_pack([small_full[n] for n in small_names]), device1, N_DEV, "place_small_grads")
    [small_flight], done = _gather_start([packed], all_handed_over, "gather_small_grads_start", all_devices=True)

    partial = {n: None for n in BIG_NAMES}
    for l in reversed(range(depth)):
        for names in SAME_SHAPE:
            arrived = []
            for n in names:
                send, recv, pieces, land = scattering[l][n]
                arrived.append(_scatter_wait(send, recv, pieces, land, done, "scatter_wait_%d_%s" % (l, n)))
            stacked = None if partial[names[0]] is None else [partial[n] for n in names]
            sums, done = _sum_pieces([a[0] for a in arrived], [a[1] for a in arrived], chip1, l, depth, stacked)
            partial.update(zip(names, sums))
    sibling = _swap_sibling([partial[n] for n in BIG_NAMES], "swap_partial_grads")
    results = {n: _adamw(w[n], m[n], v[n], [partial[n], other], "adamw_" + n) for n, other in zip(BIG_NAMES, sibling)}

    slots = _gather_wait(*small_flight, results['w_down'][0], "gather_small_grads_wait", all_devices=True)
    summed = _unpack(_sum_slots(slots, "sum_small_grads"), [small_full[n].shape for n in small_names])
    grads = dict(zip(small_names, summed))
    grads['meta_tokens'] = lax.dynamic_slice_in_dim(grads['meta_tokens'], chip * meta_tokens.shape[1], meta_tokens.shape[1], 1)
    grads['conv_w'] = lax.dynamic_slice_in_dim(grads['conv_w'], chip * conv_w.shape[2], conv_w.shape[2], 2)

    out_g, out_d, out_m, out_v = {}, {}, {}, {}
    for n in BIG_NAMES:
        out_g[n], out_d[n], out_m[n], out_v[n] = [jnp.swapaxes(r, 1, 2) if n == 'w_in' else r for r in results[n][:4]]
    shapes = [w[n].shape for n in small_names]
    res = _adamw(_pack([w[n] for n in small_names]), _pack([m[n] for n in small_names]), _pack([v[n] for n in small_names]),
                 [_pack([grads[n] for n in small_names])], "adamw_small")
    for k, store in enumerate((out_g, out_d, out_m, out_v)):
        for n, a in zip(small_names, _unpack(res[k], shapes)):
            store[n] = a

    return (loss, grad_x, *[out_g[n] for n in WEIGHT_NAMES], *[out_d[n] for n in WEIGHT_NAMES],
            *[out_m[n] for n in WEIGHT_NAMES], *[out_v[n] for n in WEIGHT_NAMES])
```
